```python
import math
import jax
import jax.numpy as jnp
from jax import lax
import numpy as np

D_MODEL = 1024
BATCH = 2
SEQ = 8192
DEPTH = 2

GRID_W = 64
CTX_LEN = 256
N_BRANCH = 4
BRANCH_W = D_MODEL // 4
FNET_GROUPS = 4
FNET_GROUP_DIM = BRANCH_W // FNET_GROUPS
S5_GROUP_CH = 16
S5_GROUPS = BRANCH_W // S5_GROUP_CH
S5_STATE = 64
RET_HEADS = 4
RET_DIM = BRANCH_W // RET_HEADS
RET_CHUNK = 128
NA_HEADS = 4
NA_DIM = BRANCH_W // NA_HEADS
NA_WIN_ROWS = 8
NA_WIN_COLS = 16
ROPE_BASE = 10000.0
D_FF = -(-(8 * D_MODEL // 3) // 256) * 256
N_EXPERTS = 8
TOP_K = 2
D_FF_EXPERT = 7 * D_MODEL // 2
N_DENSE = (DEPTH + 1) // 2
N_MOE = DEPTH // 2
EPS = 1e-6
IN_WIDTHS = (BRANCH_W,) * 9 + (N_BRANCH * D_MODEL,)
IN_W = sum(IN_WIDTHS)
IN_SPLITS = tuple(sum(IN_WIDTHS[:i + 1]) for i in range(len(IN_WIDTHS) - 1))
F32 = jnp.float32

kernel_name = 'hybrid_gated_mixer_dit_block'


def rms_norm(x, g):
    xf = x.astype(F32)
    y = xf * lax.rsqrt(jnp.mean(xf * xf, axis=-1, keepdims=True) + EPS)
    return (y * g.astype(F32)).astype(x.dtype)


def split_heads(t, n_heads):
    return t.reshape(t.shape[0], t.shape[1], n_heads, t.shape[-1] // n_heads)


def flip_seq(t, rev):
    return jnp.flip(t, axis=1) if rev else t


def axial_rope(n_tokens, head_dim):
    t = jnp.arange(n_tokens)
    row = (t // GRID_W).astype(F32)
    col = (t % GRID_W).astype(F32)
    n_freq = head_dim // 4
    inv_freq = 1.0 / (ROPE_BASE ** (jnp.arange(n_freq, dtype=F32) / n_freq))
    ang = jnp.concatenate([row[:, None] * inv_freq, col[:, None] * inv_freq], axis=-1)
    return jnp.cos(ang), jnp.sin(ang)


def apply_rope(x, cos, sin):
    x1, x2 = jnp.split(x, 2, axis=-1)
    c, s = cos[:, None, :], sin[:, None, :]
    return jnp.concatenate([x1 * c - x2 * s, x1 * s + x2 * c], axis=-1)


def fourier_mix(u):
    b_, l_, _ = u.shape
    ug = u.astype(F32).reshape(b_, l_, FNET_GROUPS, FNET_GROUP_DIM)
    f = jnp.fft.fft2(ug, axes=(1, 3), norm='ortho').real
    return f.reshape(b_, l_, BRANCH_W).astype(u.dtype)


def linear_combine(left, right):
    a_l, b_l = left
    a_r, b_r = right
    return a_l * a_r, a_r * b_l + b_r


def s5_states(u, a_re, a_im, log_dt, b_re, b_im, h0):
    lam = lax.complex(a_re, a_im)
    a_bar = jnp.exp(lam * jnp.exp(log_dt)[:, None])
    b_bar = ((a_bar - 1.0) / lam)[..., None] * lax.complex(b_re, b_im)
    bu = jnp.einsum('blgh,gph->blgp', u.astype(jnp.complex64), b_bar)
    bu = bu.at[:, 0].add(a_bar * h0)
    a = jnp.broadcast_to(a_bar, bu.shape)
    _, h = lax.associative_scan(linear_combine, (a, bu), axis=1)
    return h


def s5_mixer(u_l, u_c, a_re, a_im, log_dt, b_re, b_im, c_re, c_im, d_skip, w_glu, b_glu, need_ctx_out):
    dtype = u_l.dtype
    b_ = u_l.shape[0]

    def groups(u):
        return u.astype(F32).reshape(u.shape[0], u.shape[1], S5_GROUPS, S5_GROUP_CH)

    ul, uc = groups(u_l), groups(u_c)
    d = d_skip.astype(F32).reshape(S5_GROUPS, S5_GROUP_CH)
    y_l = d * ul
    y_c = d * uc if need_ctx_out else None
    for dirn in range(2):
        rev = dirn == 1
        prm = [p[dirn].astype(F32) for p in (a_re, a_im, log_dt, b_re, b_im)]
        c_mat = lax.complex(c_re[dirn].astype(F32), c_im[dirn].astype(F32))
        h0 = jnp.zeros((b_, S5_GROUPS, S5_STATE), jnp.complex64)
        h_c = s5_states(flip_seq(uc, rev), *prm, h0)
        h_l = s5_states(flip_seq(ul, rev), *prm, h_c[:, -1])
        y_l = y_l + flip_seq(jnp.einsum('blgp,ghp->blgh', h_l, c_mat).real, rev)
        if need_ctx_out:
            y_c = y_c + flip_seq(jnp.einsum('blgp,ghp->blgh', h_c, c_mat).real, rev)

    def glu(y):
        z = jax.nn.gelu(y).reshape(y.shape[0], y.shape[1], BRANCH_W).astype(dtype)
        return z * jax.nn.sigmoid(z @ w_glu + b_glu)

    return glu(y_l), (glu(y_c) if need_ctx_out else None)


def retention_scan(q, k, v, log_gamma, s0, with_output):
    b_, l_, h_, dk = k.shape
    dv = v.shape[-1]
    n_ch = l_ // RET_CHUNK
    kc = k.reshape(b_, n_ch, RET_CHUNK, h_, dk)
    vc = v.reshape(b_, n_ch, RET_CHUNK, h_, dv)
    pos = jnp.arange(RET_CHUNK, dtype=F32)
    k_decay = jnp.exp((RET_CHUNK - 1.0 - pos)[:, None] * log_gamma[None, :])
    chunk_decay = jnp.exp(RET_CHUNK * log_gamma)[None, :, None, None]
    kv = jnp.einsum('bnjhd,bnjhe->nbhde', kc * k_decay[:, :, None], vc)

    def step(s, kv_n):
        return chunk_decay * s + kv_n, s

    s_final, s_prev = lax.scan(step, s0, kv)
    if not with_output:
        return None, s_final
    qc = q.reshape(b_, n_ch, RET_CHUNK, h_, dk)
    diff = pos[:, None] - pos[None, :]
    intra = jnp.where(diff >= 0, jnp.exp(jnp.maximum(diff, 0.0)[None] * log_gamma[:, None, None]), 0.0)
    q_decay = jnp.exp((pos + 1.0)[:, None] * log_gamma[None, :])
    scores = jnp.einsum('bnihd,bnjhd->bnhij', qc, kc) * intra
    inner = jnp.einsum('bnhij,bnjhe->bnihe', scores, vc)
    cross = jnp.einsum('bnihd,nbhde->bnihe', qc * q_decay[:, :, None], s_prev)
    return (inner + cross).reshape(b_, l_, h_, dv), s_final


def head_norm(o, g):
    mu = jnp.mean(o, axis=-1, keepdims=True)
    var = jnp.mean(jnp.square(o - mu), axis=-1, keepdims=True)
    y = (o - mu) * lax.rsqrt(var + EPS)
    return y.reshape(o.shape[0], o.shape[1], -1) * g.astype(F32)


def retention_mixer(q_l, k_l, v_l, g_l, q_c, k_c, v_c, g_c, ret_decay, ret_gn, need_ctx_out):
    dtype = g_l.dtype
    log_gamma = jax.nn.log_sigmoid(ret_decay.astype(F32))
    s0 = jnp.zeros((q_l.shape[0], RET_HEADS, RET_DIM, RET_DIM), F32)
    o_l = jnp.zeros_like(v_l)
    o_c = jnp.zeros_like(v_c) if need_ctx_out else None
    for dirn in range(2):
        rev = dirn == 1
        oc_d, s_c = retention_scan(flip_seq(q_c, rev), flip_seq(k_c, rev), flip_seq(v_c, rev),
                                   log_gamma[dirn], s0, need_ctx_out)
        ol_d, _ = retention_scan(flip_seq(q_l, rev), flip_seq(k_l, rev), flip_seq(v_l, rev),
                                 log_gamma[dirn], s_c, True)
        o_l = o_l + flip_seq(ol_d, rev)
        if need_ctx_out:
            o_c = o_c + flip_seq(oc_d, rev)
    out_l = (jax.nn.silu(g_l.astype(F32)) * head_norm(o_l, ret_gn)).astype(dtype)
    out_c = (jax.nn.silu(g_c.astype(F32)) * head_norm(o_c, ret_gn)).astype(dtype) if need_ctx_out else None
    return out_l, out_c


def neighborhood_attention(q_l, k_l, v_l, q_c, k_c, v_c, rpb, need_ctx_out):
    b_, n_, h_, dh = q_l.shape
    rows = n_ // GRID_W
    kr = min(NA_WIN_ROWS, rows)
    kw = NA_WIN_COLS
    scale = dh ** -0.5
    qg = q_l.reshape(b_, rows, GRID_W, h_, dh)
    kg = k_l.reshape(b_, rows, GRID_W, h_, dh)
    vg = v_l.reshape(b_, rows, GRID_W, h_, dh)
    r = jnp.arange(rows)
    row_idx = jnp.clip(r - kr // 2, 0, rows - kr)[:, None] + jnp.arange(kr)[None, :]
    k_band = kg[:, row_idx]
    v_band = vg[:, row_idx]
    col = jnp.arange(GRID_W)
    col_start = jnp.clip(col - kw // 2, 0, GRID_W - kw)
    in_win = (col[None, :] >= col_start[:, None]) & (col[None, :] < col_start[:, None] + kw)
    dr = row_idx - r[:, None] + (NA_WIN_ROWS - 1)
    dc = jnp.clip(col[None, :] - col[:, None], -(kw - 1), kw - 1) + (kw - 1)
    bias = rpb[:, dr[:, None, :, None], dc[None, :, None, :]].astype(F32)
    s_band = jnp.einsum('brqhd,brikhd->bhrqik', qg, k_band).astype(F32) * scale + bias[None]
    s_band = jnp.where(in_win[:, None, :], s_band, -jnp.inf)
    s_ctx = jnp.einsum('brqhd,bchd->bhrqc', qg, k_c).astype(F32) * scale
    n_band = kr * GRID_W
    s_all = jnp.concatenate([s_band.reshape(b_, h_, rows, GRID_W, n_band), s_ctx], axis=-1)
    p = jax.nn.softmax(s_all, axis=-1).astype(v_l.dtype)
    p_band = p[..., :n_band].reshape(b_, h_, rows, GRID_W, kr, GRID_W)
    o = (jnp.einsum('bhrqik,brikhd->brqhd', p_band, v_band)
         + jnp.einsum('bhrqc,bchd->brqhd', p[..., n_band:], v_c))
    out_l = o.reshape(b_, n_, h_ * dh)
    out_c = None
    if need_ctx_out:
        s_cc = jnp.einsum('bqhd,bkhd->bhqk', q_c, k_c).astype(F32) * scale
        p_cc = jax.nn.softmax(s_cc, axis=-1).astype(v_c.dtype)
        out_c = jnp.einsum('bhqk,bkhd->bqhd', p_cc, v_c).reshape(b_, q_c.shape[1], h_ * dh)
    return out_l, out_c


def merge_branches(outs, gate_logits, w_branch, w_out):
    gates = jax.nn.sigmoid(gate_logits.astype(F32)).astype(gate_logits.dtype)
    y = gates[..., :D_MODEL] * (outs[0] @ w_branch[0])
    for b in range(1, N_BRANCH):
        y = y + gates[..., b * D_MODEL:(b + 1) * D_MODEL] * (outs[b] @ w_branch[b])
    return y @ w_out


def token_mixers(h_l, h_c, w_in, s5_a_re, s5_a_im, s5_log_dt, s5_b_re, s5_b_im, s5_c_re, s5_c_im,
                 s5_d, s5_w_glu, s5_b_glu, ret_decay, ret_gn, na_rpb, w_branch, w_out,
                 rope_cos, rope_sin, need_ctx_out):
    f_l, s_l, rq_l, rk_l, rv_l, rg_l, nq_l, nk_l, nv_l, gt_l = jnp.split(h_l @ w_in, IN_SPLITS, axis=-1)
    f_c, s_c, rq_c, rk_c, rv_c, rg_c, nq_c, nk_c, nv_c, gt_c = jnp.split(h_c @ w_in, IN_SPLITS, axis=-1)
    a_l = fourier_mix(f_l)
    b_l, b_c = s5_mixer(s_l, s_c, s5_a_re, s5_a_im, s5_log_dt, s5_b_re, s5_b_im, s5_c_re, s5_c_im,
                        s5_d, s5_w_glu, s5_b_glu, need_ctx_out)
    k_scale = RET_DIM ** -0.5

    def rh(t):
        return split_heads(t.astype(F32), RET_HEADS)

    q_lr = apply_rope(rh(rq_l), rope_cos, rope_sin)
    k_lr = apply_rope(rh(rk_l), rope_cos, rope_sin) * k_scale
    r_l, r_c = retention_mixer(q_lr, k_lr, rh(rv_l), rg_l, rh(rq_c), rh(rk_c) * k_scale, rh(rv_c), rg_c,
                               ret_decay, ret_gn, need_ctx_out)
    n_l, n_c = neighborhood_attention(split_heads(nq_l, NA_HEADS), split_heads(nk_l, NA_HEADS),
                                      split_heads(nv_l, NA_HEADS), split_heads(nq_c, NA_HEADS),
                                      split_heads(nk_c, NA_HEADS), split_heads(nv_c, NA_HEADS),
                                      na_rpb, need_ctx_out)
    y_l = merge_branches((a_l, b_l, r_l, n_l), gt_l, w_branch, w_out)
    y_c = merge_branches((fourier_mix(f_c), b_c, r_c, n_c), gt_c, w_branch, w_out) if need_ctx_out else None
    return y_l, y_c


def swiglu(h, w_gate, w_up, w_down):
    return (jax.nn.silu(h @ w_gate) * (h @ w_up)) @ w_down


def moe_swiglu(h, w_router, b_router, w_gate, w_up, w_down):
    logits = (h @ w_router).astype(F32) + b_router.astype(F32)
    top_val, top_idx = lax.top_k(logits, TOP_K)
    top_w = jax.nn.softmax(top_val, axis=-1)
    combine = jnp.sum(jax.nn.one_hot(top_idx, N_EXPERTS, dtype=F32) * top_w[..., None], axis=-2).astype(h.dtype)
    out = jnp.zeros_like(h)
    for e in range(N_EXPERTS):
        out = out + combine[..., e:e + 1] * swiglu(h, w_gate[e], w_up[e], w_down[e])
    return out


def setup_inputs(seed: int = 0) -> dict:
    key = jax.random.key(seed)
    ks = iter(jax.random.split(key, 40))

    def nrm(shape, scale):
        return jax.random.normal(next(ks), shape, F32) * scale

    L = DEPTH
    s5_shape = (L, 2, S5_GROUPS, S5_STATE)
    ret_init = jnp.asarray(np.log(2.0 ** (5 + np.arange(RET_HEADS)) - 1.0), F32)
    return {
        'x': nrm((BATCH, SEQ, D_MODEL), 1.0),
        'c': nrm((BATCH, D_MODEL), 1.0),
        'ctx': nrm((BATCH, CTX_LEN, D_MODEL), 1.0),
        'c_ctx': nrm((D_MODEL,), 1.0),
        'w_mod': nrm((L, D_MODEL, 6 * D_MODEL), 0.5 * D_MODEL ** -0.5),
        'b_mod': nrm((L, 6 * D_MODEL), 0.01),
        'norm_g': 1.0 + nrm((L, 4, D_MODEL), 0.05),
        'w_in': nrm((L, D_MODEL, IN_W), D_MODEL ** -0.5),
        's5_a_re': -0.5 + nrm(s5_shape, 0.01),
        's5_a_im': math.pi * jnp.arange(S5_STATE, dtype=F32) + nrm(s5_shape, 0.01),
        's5_log_dt': jax.random.uniform(next(ks), (L, 2, S5_GROUPS), F32, math.log(1e-3), math.log(1e-1)),
        's5_b_re': nrm((L, 2, S5_GROUPS, S5_STATE, S5_GROUP_CH), (2 * S5_GROUP_CH) ** -0.5),
        's5_b_im': nrm((L, 2, S5_GROUPS, S5_STATE, S5_GROUP_CH), (2 * S5_GROUP_CH) ** -0.5),
        's5_c_re': nrm((L, 2, S5_GROUPS, S5_GROUP_CH, S5_STATE), S5_STATE ** -0.5),
        's5_c_im': nrm((L, 2, S5_GROUPS, S5_GROUP_CH, S5_STATE), S5_STATE ** -0.5),
        's5_d': nrm((L, BRANCH_W), 1.0),
        's5_w_glu': nrm((L, BRANCH_W, BRANCH_W), BRANCH_W ** -0.5),
        's5_b_glu': nrm((L, BRANCH_W), 0.01),
        'ret_decay': ret_init + nrm((L, 2, RET_HEADS), 0.05),
        'ret_gn': 1.0 + nrm((L, BRANCH_W), 0.05),
        'na_rpb': nrm((L, NA_HEADS, 2 * NA_WIN_ROWS - 1, 2 * NA_WIN_COLS - 1), 0.02),
        'w_branch': nrm((L, N_BRANCH, BRANCH_W, D_MODEL), BRANCH_W ** -0.5),
        'w_out': nrm((L, D_MODEL, D_MODEL), D_MODEL ** -0.5),
        'ffn_w_gate': nrm((N_DENSE, D_MODEL, D_FF), D_MODEL ** -0.5),
        'ffn_w_up': nrm((N_DENSE, D_MODEL, D_FF), D_MODEL ** -0.5),
        'ffn_w_down': nrm((N_DENSE, D_FF, D_MODEL), D_FF ** -0.5),
        'moe_w_router': nrm((N_MOE, D_MODEL, N_EXPERTS), D_MODEL ** -0.5),
        'moe_b_router': nrm((N_MOE, N_EXPERTS), 0.01),
        'moe_w_gate': nrm((N_MOE, N_EXPERTS, D_MODEL, D_FF_EXPERT), D_MODEL ** -0.5),
        'moe_w_up': nrm((N_MOE, N_EXPERTS, D_MODEL, D_FF_EXPERT), D_MODEL ** -0.5),
        'moe_w_down': nrm((N_MOE, N_EXPERTS, D_FF_EXPERT, D_MODEL), D_FF_EXPERT ** -0.5),
    }


def reference(x, c, ctx, c_ctx, w_mod, b_mod, norm_g, w_in, s5_a_re, s5_a_im, s5_log_dt, s5_b_re, s5_b_im,
              s5_c_re, s5_c_im, s5_d, s5_w_glu, s5_b_glu, ret_decay, ret_gn, na_rpb, w_branch, w_out,
              ffn_w_gate, ffn_w_up, ffn_w_down, moe_w_router, moe_b_router, moe_w_gate, moe_w_up, moe_w_down):
    rope_cos, rope_sin = axial_rope(x.shape[1], RET_DIM)
    cond = jnp.concatenate([c, c_ctx[None, :]], axis=0)
    for layer in range(DEPTH):
        last = layer == DEPTH - 1
        mod = jax.nn.silu(cond) @ w_mod[layer] + b_mod[layer]
        sh_a, sc_a, g_a, sh_f, sc_f, g_f = jnp.split(mod[:-1, None, :], 6, axis=-1)
        csh_a, csc_a, cg_a, csh_f, csc_f, cg_f = jnp.split(mod[-1:, None, :], 6, axis=-1)
        h_l = rms_norm(x, norm_g[layer, 0]) * (1 + sc_a) + sh_a
        h_c = rms_norm(ctx, norm_g[layer, 0]) * (1 + csc_a) + csh_a
        y_l, y_c = token_mixers(h_l, h_c, w_in[layer], s5_a_re[layer], s5_a_im[layer], s5_log_dt[layer],
                                s5_b_re[layer], s5_b_im[layer], s5_c_re[layer], s5_c_im[layer], s5_d[layer],
                                s5_w_glu[layer], s5_b_glu[layer], ret_decay[layer], ret_gn[layer],
                                na_rpb[layer], w_branch[layer], w_out[layer], rope_cos, rope_sin, not last)
        x = x + g_a * rms_norm(y_l, norm_g[layer, 1])
        if not last:
            ctx = ctx + cg_a * rms_norm(y_c, norm_g[layer, 1])
        i = layer // 2
        if layer % 2 == 0:
            def ffn(h):
                return swiglu(h, ffn_w_gate[i], ffn_w_up[i], ffn_w_down[i])
        else:
            def ffn(h):
                return moe_swiglu(h, moe_w_router[i], moe_b_router[i], moe_w_gate[i], moe_w_up[i], moe_w_down[i])
        h_l = rms_norm(x, norm_g[layer, 2]) * (1 + sc_f) + sh_f
        x = x + g_f * rms_norm(ffn(h_l), norm_g[layer, 3])
        if not last:
            h_c = rms_norm(ctx, norm_g[layer, 2]) * (1 + csc_f) + csh_f
            ctx = ctx + cg_f * rms_norm(ffn(h_c), norm_g[layer, 3])
    return x
```

```python
import functools
import math

import numpy as np
import jax
import jax.numpy as jnp
from jax import lax
from jax.experimental import pallas as pl
from jax.experimental.pallas import tpu as pltpu

F32 = jnp.float32
BF16 = jnp.bfloat16

D_MODEL = 1024
BRANCH_W = 256
N_BRANCH = 4
GRID_W = 64
FNET_GROUP_DIM = 64
S5_GROUP_CH = 16
S5_GROUPS = 16
S5_STATE = 64
S5_CHUNK = 32
S5_PAIRS = S5_GROUPS // 2
RET_HEADS = 4
RET_DIM = 64
RET_CHUNK = 128
NA_HEADS = 4
NA_DIM = 64
NA_WIN_ROWS = 8
NA_WIN_COLS = 16
NA_QROWS = 8
ROPE_BASE = 10000.0
N_EXPERTS = 8
EPS = 1e-6
FFT_N2 = 256
NEG_BIG = -1e30
VMEM_LIMIT_BYTES = 50 * 1024 * 1024

COL_F, COL_S, COL_RQ, COL_RK, COL_RV, COL_RG, COL_NQ, COL_NK, COL_NV = range(16, 25)
IN_W = 9 * BRANCH_W + N_BRANCH * D_MODEL
IN_TN = 1280
IN_F_TILE = (N_BRANCH * D_MODEL) // IN_TN
IN_F_OFF = N_BRANCH * D_MODEL - IN_F_TILE * IN_TN


def _cparams(*sem):
    return pltpu.CompilerParams(dimension_semantics=sem, vmem_limit_bytes=VMEM_LIMIT_BYTES)


def _sigmoid(v):
    return 1.0 / (1.0 + jnp.exp(-v))


def _silu(v):
    return v * _sigmoid(v)


def _gelu_tanh(v):
    return 0.5 * v * (1.0 + jnp.tanh(math.sqrt(2.0 / math.pi) * (v + 0.044715 * (v * v * v))))


def _rms(v, g):
    ms = jnp.mean(v * v, axis=-1, keepdims=True)
    return v * lax.rsqrt(ms + EPS) * g


def _split_bf16(v):
    hi = v.astype(BF16)
    lo = (v - hi.astype(F32)).astype(BF16)
    return hi, lo


def _dot(a, b):
    return jnp.dot(a, b, preferred_element_type=F32)


def _dot_nt(a, b):
    return lax.dot_general(a, b, (((1,), (1,)), ((), ())), preferred_element_type=F32)


def _dot_tn(a, b):
    return lax.dot_general(a, b, (((0,), (0,)), ((), ())), preferred_element_type=F32)


def _mod_kernel(ct_ref, w_ref, b_ref, o_ref):
    ct = ct_ref[...]
    s = _silu(ct)
    w = w_ref[...]
    rows = [jnp.sum(w * s[:, r:r + 1], axis=0, keepdims=True) for r in range(8)]
    o_ref[...] = jnp.concatenate(rows, axis=0) + b_ref[...]


def _modulation(cond, w_mod, b_mod):
    n_layers, d, n = w_mod.shape
    tn = 512
    ct = jnp.zeros((8, d), F32).at[:cond.shape[0]].set(cond).T
    return pl.pallas_call(
        _mod_kernel,
        grid=(n_layers, n // tn),
        in_specs=[
            pl.BlockSpec((d, 8), lambda l, j: (0, 0)),
            pl.BlockSpec((None, d, tn), lambda l, j: (l, 0, j)),
            pl.BlockSpec((None, 1, tn), lambda l, j: (l, 0, j)),
        ],
        out_specs=pl.BlockSpec((None, 8, tn), lambda l, j: (l, 0, j)),
        out_shape=jax.ShapeDtypeStruct((n_layers, 8, n), F32),
        compiler_params=_cparams("arbitrary", "arbitrary"),
        name="adaln_mod",
    )(ct, w_mod, b_mod.reshape(n_layers, 1, n))


def _mod_rows(mod_ref, i, tiles_per_mod, mod_base, first):
    r = mod_base + i // tiles_per_mod
    return [mod_ref[pl.ds(r, 1), (first + k) * D_MODEL:(first + k + 1) * D_MODEL] for k in range(3)]


def _in_kernel(x_ref, mod_ref, g_ref, w_ref, proj_ref, f_ref, h_scr, *, tiles_per_mod, mod_base):
    i = pl.program_id(0)
    j = pl.program_id(1)

    @pl.when(j == 0)
    def _():
        sh, sc, _ = _mod_rows(mod_ref, i, tiles_per_mod, mod_base, 0)
        h_scr[...] = (_rms(x_ref[...], g_ref[...]) * (1.0 + sc) + sh).astype(BF16)

    res = _dot(h_scr[...], w_ref[...])
    proj_ref[...] = res.astype(BF16)

    @pl.when(j == IN_F_TILE)
    def _():
        f_ref[...] = res[:, IN_F_OFF:IN_F_OFF + BRANCH_W].astype(BF16)


def _in_proj(x, mod, g, w_bf, *, rows_per_mod, mod_base):
    rows, d = x.shape
    tm = min(512, rows)
    kern = functools.partial(_in_kernel, tiles_per_mod=max(rows_per_mod // tm, 1), mod_base=mod_base)
    return pl.pallas_call(
        kern,
        grid=(rows // tm, IN_W // IN_TN),
        in_specs=[
            pl.BlockSpec((tm, d), lambda i, j: (i, 0)),
            pl.BlockSpec(mod.shape, lambda i, j: (0, 0)),
            pl.BlockSpec((1, d), lambda i, j: (0, 0)),
            pl.BlockSpec((d, IN_TN), lambda i, j: (0, j)),
        ],
        out_specs=[
            pl.BlockSpec((tm, IN_TN), lambda i, j: (i, j)),
            pl.BlockSpec((tm, BRANCH_W), lambda i, j: (i, 0)),
        ],
        out_shape=[
            jax.ShapeDtypeStruct((rows, IN_W), BF16),
            jax.ShapeDtypeStruct((rows, BRANCH_W), BF16),
        ],
        scratch_shapes=[pltpu.VMEM((tm, d), BF16)],
        compiler_params=_cparams("arbitrary", "arbitrary"),
        name="in_proj",
    )(x, mod, g.reshape(1, d), w_bf)


def _fft_a_kernel(x_ref, cs_ref, tc_ref, ts_ref, zr_ref, zi_ref, *, n1, n1p):
    y = _dot(cs_ref[...].astype(BF16), x_ref[...])
    yr = y[:n1]
    yi = y[n1p:n1p + n1]
    tc = tc_ref[...]
    ts = ts_ref[...]
    zr_ref[...] = (yr * tc + yi * ts).astype(BF16)
    zi_ref[...] = (yi * tc - yr * ts).astype(BF16)


def _fft_b_kernel(zr_ref, zi_ref, cs_ref, cc_ref, sc_ref, o_ref, *, kb, scale, has_imag):
    cs = cs_ref[...].astype(BF16)
    cc = cc_ref[...].astype(BF16)
    sc = sc_ref[...].astype(BF16)
    for kk in range(kb):
        a = _dot(cs, zr_ref[kk])
        if has_imag:
            b = _dot(cs, zi_ref[kk])
            xr = a[:FFT_N2] + b[FFT_N2:]
            xi = b[:FFT_N2] - a[FFT_N2:]
        else:
            xr = a[:FFT_N2]
            xi = -a[FFT_N2:]
        out = _dot(xr.astype(BF16), cc) + _dot(xi.astype(BF16), sc)
        o_ref[:, kk * BRANCH_W:(kk + 1) * BRANCH_W] = (out * scale).astype(BF16)


def _dft_tables(n):
    k = np.arange(n)
    ang = 2.0 * np.pi * ((k[:, None] * k[None, :]) % n) / n
    return np.cos(ang), np.sin(ang)


def _fft_b_call(zr, zi, n1, batch, seq_len, has_imag):
    c2, s2 = _dft_tables(FFT_N2)
    cs2 = jnp.asarray(np.concatenate([c2, s2], axis=0), F32)
    c64, s64 = _dft_tables(FNET_GROUP_DIM)
    eye = np.eye(BRANCH_W // FNET_GROUP_DIM)
    cc = jnp.asarray(np.kron(eye, c64), F32)
    sc = jnp.asarray(np.kron(eye, s64), F32)
    kb = min(8, n1)
    scale = 1.0 / math.sqrt(seq_len * FNET_GROUP_DIM)
    kern = functools.partial(_fft_b_kernel, kb=kb, scale=scale, has_imag=has_imag)
    zspec = pl.BlockSpec((None, kb, FFT_N2, BRANCH_W), lambda b, i: (b, i, 0, 0))
    out = pl.pallas_call(
        kern,
        grid=(batch, n1 // kb),
        in_specs=[
            zspec, zspec,
            pl.BlockSpec((2 * FFT_N2, FFT_N2), lambda b, i: (0, 0)),
            pl.BlockSpec((BRANCH_W, BRANCH_W), lambda b, i: (0, 0)),
            pl.BlockSpec((BRANCH_W, BRANCH_W), lambda b, i: (0, 0)),
        ],
        out_specs=pl.BlockSpec((None, FFT_N2, kb * BRANCH_W), lambda b, i: (b, 0, i)),
        out_shape=jax.ShapeDtypeStruct((batch, FFT_N2, n1 * BRANCH_W), BF16),
        compiler_params=_cparams("arbitrary", "arbitrary"),
        name="fourier_stage_b",
    )(zr, zi, cs2, cc, sc)
    return out.reshape(batch * seq_len, BRANCH_W)


def _fourier_latent(f, batch, seq_len):
    n1 = seq_len // FFT_N2
    wide = FFT_N2 * BRANCH_W
    c1, s1 = _dft_tables(n1)
    n1p = max(n1, 8)
    cs1 = np.zeros((2 * n1p, n1))
    cs1[:n1] = c1
    cs1[n1p:n1p + n1] = -s1
    k1 = np.arange(n1)[:, None]
    l2 = np.arange(FFT_N2)[None, :]
    tw = 2.0 * np.pi * (k1 * l2) / seq_len
    tc = jnp.asarray(np.repeat(np.cos(tw), BRANCH_W, axis=1), F32)
    ts = jnp.asarray(np.repeat(np.sin(tw), BRANCH_W, axis=1), F32)
    cw = min(8192, wide)
    xv = f.reshape(batch, n1, wide)
    spec = pl.BlockSpec((None, n1, cw), lambda b, j: (b, 0, j))
    tspec = pl.BlockSpec((n1, cw), lambda b, j: (0, j))
    zr, zi = pl.pallas_call(
        functools.partial(_fft_a_kernel, n1=n1, n1p=n1p),
        grid=(batch, wide // cw),
        in_specs=[spec, pl.BlockSpec((2 * n1p, n1), lambda b, j: (0, 0)), tspec, tspec],
        out_specs=[spec, spec],
        out_shape=[jax.ShapeDtypeStruct((batch, n1, wide), BF16)] * 2,
        compiler_params=_cparams("arbitrary", "arbitrary"),
        name="fourier_stage_a",
    )(xv, jnp.asarray(cs1, F32), tc, ts)
    zr = zr.reshape(batch, n1, FFT_N2, BRANCH_W)
    zi = zi.reshape(batch, n1, FFT_N2, BRANCH_W)
    return _fft_b_call(zr, zi, n1, batch, seq_len, True)


def _fourier_ctx(f, batch, ctx_len):
    assert ctx_len == FFT_N2
    z = f.reshape(batch, 1, FFT_N2, BRANCH_W)
    return _fft_b_call(z, z, 1, batch, ctx_len, False)


def _s5_tables(a_re, a_im, log_dt, b_re, b_im, c_re, c_im, d_skip, batch):
    t = S5_CHUNK
    g, p, hc = S5_GROUPS, S5_STATE, S5_GROUP_CH
    lam = lax.complex(a_re.astype(F32), a_im.astype(F32))
    dt = jnp.exp(log_dt.astype(F32))[..., None]
    ks = jnp.arange(t + 1, dtype=F32)
    apow = jnp.exp((lam * dt)[..., None] * ks)
    a_bar = apow[..., 1]
    b_bar = ((a_bar - 1.0) / lam)[..., None] * lax.complex(b_re.astype(F32), b_im.astype(F32))
    cm = lax.complex(c_re.astype(F32), c_im.astype(F32))
    kimp = jnp.real(jnp.einsum('dghp,dgpk,dgpj->dgkhj', cm, apow[..., :t], b_bar,
                               precision=lax.Precision.HIGHEST))
    kf, kb = kimp[0], kimp[1]
    lag = jnp.arange(t)[None, :] - jnp.arange(t)[:, None]
    kf_g = kf[:, jnp.clip(lag, 0, t - 1)]
    kb_g = kb[:, jnp.clip(-lag, 0, t - 1)]
    lag5 = lag[None, :, :, None, None]
    mfull = jnp.where(lag5 > 0, kf_g, 0.0) + jnp.where(lag5 < 0, kb_g, 0.0) + jnp.where(lag5 == 0, kf_g + kb_g, 0.0)
    m = mfull.transpose(0, 1, 4, 2, 3).reshape(g, t * hc, t * hc)
    m = m.reshape(S5_PAIRS, 2, t * hc, t * hc).astype(BF16)

    wf = jnp.einsum('gpj,gph->gjhp', apow[0][..., t - 1::-1][..., :t], b_bar[0])
    wb = jnp.einsum('gpj,gph->gjhp', apow[1][..., :t], b_bar[1])
    wf = wf.reshape(g, t * hc, p)
    wb = wb.reshape(g, t * hc, p)
    kinds = [jnp.real(wf), jnp.imag(wf), jnp.real(wb), jnp.imag(wb)]
    eye2 = jnp.eye(2, dtype=F32)
    we_cols = []
    for kd in kinds:
        kp = kd.reshape(S5_PAIRS, 2, t * hc, p)
        blk = jnp.einsum('qirp,ij->qirjp', kp, eye2).reshape(S5_PAIRS, 2 * t * hc, 2 * p)
        we_cols.append(blk)
    we = jnp.concatenate(we_cols, axis=-1).astype(BF16)

    vf = jnp.einsum('ghp,gpt->gpth', cm[0], apow[0][..., 1:t + 1])
    vb = jnp.einsum('ghp,gpt->gpth', cm[1], apow[1][..., t:0:-1])
    vf = vf.reshape(g, p, t * hc)
    vb = vb.reshape(g, p, t * hc)
    vkinds = [jnp.real(vf), -jnp.imag(vf), jnp.real(vb), -jnp.imag(vb)]
    v_rows = []
    for kd in vkinds:
        kp = kd.reshape(S5_PAIRS, 2, p, t * hc)
        blk = jnp.einsum('qipc,ij->qipjc', kp, eye2).reshape(S5_PAIRS, 2 * p, 2 * t * hc)
        v_rows.append(blk)
    v1 = jnp.concatenate(v_rows, axis=1)
    v = jnp.concatenate([v1, v1], axis=1).astype(BF16)

    def lanes(z):
        return jnp.tile(z.reshape(1, g * p), (1, batch))

    at = apow[..., t]
    a_tab = jnp.concatenate([lanes(jnp.real(at[0])), lanes(jnp.imag(at[0])),
                             lanes(jnp.real(at[1])), lanes(jnp.imag(at[1]))], axis=0)
    dvec = jnp.tile(d_skip.astype(F32).reshape(S5_PAIRS, 2, 1, hc), (1, 1, t, 1)).reshape(S5_PAIRS, 1, 2 * t * hc)
    return dict(m=m, we=we, v=v, a_tab=a_tab, dvec=dvec)


def _s5_e_kernel(u_ref, we_ref, ref_, imf_, reb_, imb_):
    e = _dot(u_ref[...], we_ref[...])
    ref_[...] = e[:, 0:128]
    imf_[...] = e[:, 128:256]
    reb_[...] = e[:, 256:384]
    imb_[...] = e[:, 384:512]


def _s5_scan_kernel(a_ref, ref_, imf_, reb_, imb_, prf, pif, prb, pib, *, n_rows, n_ctx):
    afr = a_ref[0:1, :]
    afi = a_ref[1:2, :]
    abr = a_ref[2:3, :]
    abi = a_ref[3:4, :]
    zero = jnp.zeros_like(afr)

    def body(s, carry):
        sfr, sfi, sbr, sbi = carry
        nf = s
        nb = jnp.where(s < n_ctx, n_ctx - 1 - s, n_rows - 1 + n_ctx - s)
        prf[pl.ds(nf, 1), :] = sfr
        pif[pl.ds(nf, 1), :] = sfi
        prb[pl.ds(nb, 1), :] = sbr
        pib[pl.ds(nb, 1), :] = sbi
        efr = ref_[pl.ds(nf, 1), :]
        efi = imf_[pl.ds(nf, 1), :]
        ebr = reb_[pl.ds(nb, 1), :]
        ebi = imb_[pl.ds(nb, 1), :]
        nfr = afr * sfr - afi * sfi + efr
        nfi = afr * sfi + afi * sfr + efi
        nbr = abr * sbr - abi * sbi + ebr
        nbi = abr * sbi + abi * sbr + ebi
        return nfr, nfi, nbr, nbi

    lax.fori_loop(0, n_rows, body, (zero, zero, zero, zero))


def _s5_y_kernel(u_ref, m_ref, v_ref, d_ref, prf, pif, prb, pib, y_ref):
    u = u_ref[...]
    half = S5_CHUNK * S5_GROUP_CH
    y_intra = jnp.concatenate([_dot(u[:, :half], m_ref[0]), _dot(u[:, half:], m_ref[1])], axis=-1)
    pcat = jnp.concatenate([prf[...], pif[...], prb[...], pib[...]], axis=-1)
    hi, lo = _split_bf16(pcat)
    y_cross = _dot(jnp.concatenate([hi, lo], axis=-1), v_ref[...])
    y_ref[...] = y_intra + y_cross + d_ref[...] * u.astype(F32)


def _s5_core(u, tabs, batch, n_rows, n_ctx):
    width = batch * S5_PAIRS * 128
    cols = 2 * S5_CHUNK * S5_GROUP_CH
    u_spec = pl.BlockSpec((None, None, n_rows, cols), lambda q, b: (q, b, 0, 0))
    st_spec = pl.BlockSpec((n_rows, 128), lambda q, b: (0, b * S5_PAIRS + q))
    st_shape = jax.ShapeDtypeStruct((n_rows, width), F32)
    e4 = pl.pallas_call(
        _s5_e_kernel,
        grid=(S5_PAIRS, batch),
        in_specs=[u_spec, pl.BlockSpec((None, cols, 512), lambda q, b: (q, 0, 0))],
        out_specs=[st_spec] * 4,
        out_shape=[st_shape] * 4,
        compiler_params=_cparams("arbitrary", "arbitrary"),
        name="s5_chunk_states",
    )(u, tabs['we'])
    p4 = pl.pallas_call(
        functools.partial(_s5_scan_kernel, n_rows=n_rows, n_ctx=n_ctx),
        out_shape=[st_shape] * 4,
        compiler_params=pltpu.CompilerParams(vmem_limit_bytes=VMEM_LIMIT_BYTES),
        name="s5_state_scan",
    )(tabs['a_tab'], *e4)
    y = pl.pallas_call(
        _s5_y_kernel,
        grid=(S5_PAIRS, batch),
        in_specs=[
            u_spec,
            pl.BlockSpec((None, 2, cols // 2, cols // 2), lambda q, b: (q, 0, 0, 0)),
            pl.BlockSpec((None, cols, cols), lambda q, b: (q, 0, 0)),
            pl.BlockSpec((None, 1, cols), lambda q, b: (q, 0, 0)),
            st_spec, st_spec, st_spec, st_spec,
        ],
        out_specs=pl.BlockSpec((None, None, n_rows, cols), lambda q, b: (q, b, 0, 0)),
        out_shape=jax.ShapeDtypeStruct((S5_PAIRS, batch, n_rows, cols), F32),
        compiler_params=_cparams("arbitrary", "arbitrary"),
        name="s5_outputs",
    )(u, tabs['m'], tabs['v'], tabs['dvec'], *p4)
    return y


def _s5_to_chunks(s, batch):
    n = s.shape[0] // batch // S5_CHUNK
    v = s.reshape(batch, n, S5_CHUNK, S5_PAIRS, 2, S5_GROUP_CH)
    return v.transpose(3, 0, 1, 4, 2, 5).reshape(S5_PAIRS, batch, n, 2 * S5_CHUNK * S5_GROUP_CH)


def _s5_from_chunks(y, batch):
    n = y.shape[2]
    v = y.reshape(S5_PAIRS, batch, n, 2, S5_CHUNK, S5_GROUP_CH)
    return v.transpose(1, 2, 4, 0, 3, 5).reshape(batch * n * S5_CHUNK, BRANCH_W)


def _s5_mixer(s_lat, s_ctx, tabs, batch):
    ul = _s5_to_chunks(s_lat, batch)
    uc = _s5_to_chunks(s_ctx, batch)
    n_ctx = uc.shape[2]
    u = jnp.concatenate([uc, ul], axis=2)
    y = _s5_core(u, tabs, batch, u.shape[2], n_ctx)
    return _s5_from_chunks(y[:, :, n_ctx:], batch), _s5_from_chunks(y[:, :, :n_ctx], batch)


def _ret_tables(ret_decay):
    c = RET_CHUNK
    lg = jax.nn.log_sigmoid(ret_decay.astype(F32))
    lane_h = jnp.repeat(jnp.arange(RET_HEADS), RET_DIM)
    lgl = lg[:, lane_h]
    pos = jnp.arange(c, dtype=F32)[:, None]
    qd = jnp.stack([jnp.exp((pos + 1.0) * lgl[0][None]), jnp.exp((c - pos) * lgl[1][None])])
    kd = jnp.stack([jnp.exp((c - 1.0 - pos) * lgl[0][None]), jnp.exp(pos * lgl[1][None])])
    bmask = (lane_h[:, None] == lane_h[None, :]).astype(F32)
    cd = jnp.exp(c * lgl)[:, :, None] * bmask[None]
    diff = pos - pos.T
    dm = []
    for h in range(RET_HEADS):
        fw = jnp.where(diff >= 0, jnp.exp(jnp.maximum(diff, 0.0) * lg[0, h]), 0.0)
        bw = jnp.where(diff <= 0, jnp.exp(jnp.maximum(-diff, 0.0) * lg[1, h]), 0.0)
        dm.append(fw + bw)
    dm = jnp.concatenate(dm, axis=0)
    hmask = (jnp.arange(RET_HEADS)[:, None] == lane_h[None, :]).astype(F32)
    return dict(qd=qd, kd=kd, cd=cd, bmask=bmask, dm=dm, hmask=hmask)


def _rope_tables(n_tokens):
    t = np.arange(n_tokens)
    row = (t // GRID_W).astype(np.float64)
    col = (t % GRID_W).astype(np.float64)
    n_freq = RET_DIM // 4
    inv_freq = 1.0 / (ROPE_BASE ** (np.arange(n_freq, dtype=np.float64) / n_freq))
    ang = np.concatenate([row[:, None] * inv_freq, col[:, None] * inv_freq], axis=-1)
    cos = np.cos(ang)
    sin = np.sin(ang)
    cos_t = np.tile(np.concatenate([cos, cos], axis=-1), (1, RET_HEADS))
    sin_t = np.tile(np.concatenate([-sin, sin], axis=-1), (1, RET_HEADS))
    half = RET_DIM // 2
    perm = np.arange(BRANCH_W) ^ half
    swap = np.zeros((BRANCH_W, BRANCH_W), np.float32)
    swap[perm, np.arange(BRANCH_W)] = 1.0
    return jnp.asarray(cos_t, F32), jnp.asarray(sin_t, F32), jnp.asarray(swap, BF16)


def _ret_chunk(q, k, v, s, qd, kd, cd, bmask, dm, hmask, with_intra):
    cross = _dot((q * qd).astype(BF16), s.astype(BF16))
    s_new = cd * s + bmask * _dot_tn((k * kd).astype(BF16), v)
    if not with_intra:
        return cross, s_new
    qb = q.astype(BF16)
    kb = k.astype(BF16)
    qs = jnp.concatenate([qb * hmask[h:h + 1].astype(BF16) for h in range(RET_HEADS)], axis=0)
    scores = _dot_nt(qs, kb) * dm
    ov = _dot(scores.astype(BF16), v)
    c = q.shape[0]
    inner = ov[0:c] * hmask[0:1]
    for h in range(1, RET_HEADS):
        inner = inner + ov[h * c:(h + 1) * c] * hmask[h:h + 1]
    return inner + cross, s_new


def _ret_kernel(q_ref, k_ref, v_ref, qc_ref, kc_ref, vc_ref, cos_ref, sin_ref, swap_ref,
                qd_ref, kd_ref, cd_ref, bm_ref, dm_ref, hm_ref, o_ref, oc_ref, s_scr, *, n_chunks, n_ctx_chunks):
    dirn = pl.program_id(1)
    i = pl.program_id(2)
    c = RET_CHUNK
    k_scale = RET_DIM ** -0.5
    bmask = bm_ref[...]
    dm = dm_ref[...]
    hmask = hm_ref[...]

    def run(d):
        qd = qd_ref[d]
        kd = kd_ref[d]
        cd = cd_ref[d]
        intra = d == 0

        @pl.when(i == 0)
        def _():
            s = jnp.zeros((BRANCH_W, BRANCH_W), F32)
            order = range(n_ctx_chunks) if d == 0 else range(n_ctx_chunks - 1, -1, -1)
            for cc in order:
                sl = slice(cc * c, (cc + 1) * c)
                o, s = _ret_chunk(qc_ref[sl, :].astype(F32), kc_ref[sl, :].astype(F32) * k_scale, vc_ref[sl, :],
                                  s, qd, kd, cd, bmask, dm, hmask, intra)
                oc_ref[sl, :] = o
            s_scr[...] = s

        swap = swap_ref[...]
        order = range(n_chunks) if d == 0 else range(n_chunks - 1, -1, -1)
        s = s_scr[...]
        for cc in order:
            sl = slice(cc * c, (cc + 1) * c)
            cos = cos_ref[sl, :]
            sin = sin_ref[sl, :]
            qb = q_ref[sl, :]
            kb = k_ref[sl, :]
            q = qb.astype(F32) * cos + _dot(qb, swap) * sin
            k = (kb.astype(F32) * cos + _dot(kb, swap) * sin) * k_scale
            o, s = _ret_chunk(q, k, v_ref[sl, :], s, qd, kd, cd, bmask, dm, hmask, intra)
            o_ref[sl, :] = o
        s_scr[...] = s

    @pl.when(dirn == 0)
    def _():
        run(0)

    @pl.when(dirn == 1)
    def _():
        run(1)


def _retention(proj_l, proj_c, tabs, rope, batch, seq_len, ctx_len):
    n_chunks = 4
    blk = n_chunks * RET_CHUNK
    nblk = seq_len // blk
    cos_t, sin_t, swap = rope

    def pos(d, i):
        return i + d * (nblk - 1 - 2 * i)

    def lat(col):
        return pl.BlockSpec((blk, BRANCH_W), lambda b, d, i: (b * nblk + pos(d, i), col))

    def ctx(col):
        return pl.BlockSpec((ctx_len, BRANCH_W), lambda b, d, i: (b, col))

    def const(shape):
        return pl.BlockSpec(shape, lambda b, d, i: (0,) * len(shape))

    tab_spec = pl.BlockSpec((blk, BRANCH_W), lambda b, d, i: (pos(d, i), 0))
    kern = functools.partial(_ret_kernel, n_chunks=n_chunks, n_ctx_chunks=ctx_len // RET_CHUNK)
    c = RET_CHUNK
    o, oc = pl.pallas_call(
        kern,
        grid=(batch, 2, nblk),
        in_specs=[
            lat(COL_RQ), lat(COL_RK), lat(COL_RV), ctx(COL_RQ), ctx(COL_RK), ctx(COL_RV),
            tab_spec, tab_spec, const((BRANCH_W, BRANCH_W)),
            const((2, c, BRANCH_W)), const((2, c, BRANCH_W)), const((2, BRANCH_W, BRANCH_W)),
            const((BRANCH_W, BRANCH_W)), const((RET_HEADS * c, c)), const((RET_HEADS, BRANCH_W)),
        ],
        out_specs=[
            pl.BlockSpec((None, blk, BRANCH_W), lambda b, d, i: (d, b * nblk + pos(d, i), 0)),
            pl.BlockSpec((None, ctx_len, BRANCH_W), lambda b, d, i: (d, b, 0)),
        ],
        out_shape=[
            jax.ShapeDtypeStruct((2, batch * seq_len, BRANCH_W), F32),
            jax.ShapeDtypeStruct((2, batch * ctx_len, BRANCH_W), F32),
        ],
        scratch_shapes=[pltpu.VMEM((BRANCH_W, BRANCH_W), F32)],
        compiler_params=_cparams("arbitrary", "arbitrary", "arbitrary"),
        name="retention",
    )(proj_l, proj_l, proj_l, proj_c, proj_c, proj_c, cos_t, sin_t, swap,
      tabs['qd'], tabs['kd'], tabs['cd'], tabs['bmask'], tabs['dm'], tabs['hmask'])
    return o, oc


def _na_tables(rpb):
    kr, kw = NA_WIN_ROWS, NA_WIN_COLS
    col = np.arange(GRID_W)
    col_start = np.clip(col - kw // 2, 0, GRID_W - kw)
    in_win = (col[None, :] >= col_start[:, None]) & (col[None, :] < col_start[:, None] + kw)
    dc = np.clip(col[None, :] - col[:, None], -(kw - 1), kw - 1) + (kw - 1)
    var = np.arange(kr)[:, None] + np.arange(kr)[None, :]
    bias = rpb.astype(F32)[:, var[:, None, :, None], dc[None, :, None, :]]
    bias = jnp.where(jnp.asarray(in_win)[None, None, :, None, :], bias, NEG_BIG)
    bias = bias.transpose(1, 0, 2, 3, 4).reshape(kr, NA_HEADS * GRID_W, kr * GRID_W)
    lane_h = np.repeat(np.arange(NA_HEADS), NA_DIM)
    hmask = (np.arange(NA_HEADS)[:, None] == lane_h[None, :]).astype(np.float32)
    return bias, jnp.asarray(hmask, F32)


def _attend(qs, keys, vals, bias, kc, vc):
    s_ctx = _dot_nt(qs, kc)
    m = jnp.max(s_ctx, axis=-1, keepdims=True)
    if keys is not None:
        s_band = _dot_nt(qs, keys) + bias
        m = jnp.maximum(m, jnp.max(s_band, axis=-1, keepdims=True))
        p_band = jnp.exp(s_band - m)
    p_ctx = jnp.exp(s_ctx - m)
    l = jnp.sum(p_ctx, axis=-1, keepdims=True)
    o = _dot(p_ctx.astype(BF16), vc)
    if keys is not None:
        l = l + jnp.sum(p_band, axis=-1, keepdims=True)
        o = o + _dot(p_band.astype(BF16), vals)
    return o / l


def _stack_heads(q, hmask_scaled):
    return jnp.concatenate([q * hmask_scaled[h:h + 1] for h in range(NA_HEADS)], axis=0)


def _unstack_heads(o, hmask, n):
    out = o[0:n] * hmask[0:1]
    for h in range(1, NA_HEADS):
        out = out + o[h * n:(h + 1) * n] * hmask[h:h + 1]
    return out


def _na_kernel(q_ref, k_ref, v_ref, kc_ref, vc_ref, bias_ref, hm_ref, o_ref, *, n_grid_rows):
    i = pl.program_id(1)
    hmask = hm_ref[...]
    hms = (hmask * (NA_DIM ** -0.5)).astype(BF16)
    kc = kc_ref[...]
    vc = vc_ref[...]
    band = NA_WIN_ROWS * GRID_W
    for rr in range(NA_QROWS):
        r = i * NA_QROWS + rr
        rs = jnp.clip(r - NA_WIN_ROWS // 2, 0, n_grid_rows - NA_WIN_ROWS)
        var = rs - r + (NA_WIN_ROWS - 1)
        start = pl.multiple_of(rs * GRID_W, GRID_W)
        keys = k_ref[pl.ds(start, band), :]
        vals = v_ref[pl.ds(start, band), :]
        qs = _stack_heads(q_ref[rr * GRID_W:(rr + 1) * GRID_W, :], hms)
        o = _attend(qs, keys, vals, bias_ref[var], kc, vc)
        o_ref[rr * GRID_W:(rr + 1) * GRID_W, :] = _unstack_heads(o, hmask, GRID_W).astype(BF16)


def _na_ctx_kernel(q_ref, kc_ref, vc_ref, hm_ref, o_ref):
    hmask = hm_ref[...]
    hms = (hmask * (NA_DIM ** -0.5)).astype(BF16)
    n = q_ref.shape[0]
    o = _attend(_stack_heads(q_ref[...], hms), None, None, None, kc_ref[...], vc_ref[...])
    o_ref[...] = _unstack_heads(o, hmask, n).astype(BF16)


def _neighborhood(proj_l, proj_c, bias, hmask, batch, seq_len, ctx_len, need_ctx_out):
    rows = seq_len // GRID_W
    qblk = NA_QROWS * GRID_W
    nq = seq_len // qblk
    out_l = pl.pallas_call(
        functools.partial(_na_kernel, n_grid_rows=rows),
        grid=(batch, nq),
        in_specs=[
            pl.BlockSpec((qblk, BRANCH_W), lambda b, i: (b * nq + i, COL_NQ)),
            pl.BlockSpec((seq_len, BRANCH_W), lambda b, i: (b, COL_NK)),
            pl.BlockSpec((seq_len, BRANCH_W), lambda b, i: (b, COL_NV)),
            pl.BlockSpec((ctx_len, BRANCH_W), lambda b, i: (b, COL_NK)),
            pl.BlockSpec((ctx_len, BRANCH_W), lambda b, i: (b, COL_NV)),
            pl.BlockSpec(bias.shape, lambda b, i: (0, 0, 0)),
            pl.BlockSpec(hmask.shape, lambda b, i: (0, 0)),
        ],
        out_specs=pl.BlockSpec((qblk, BRANCH_W), lambda b, i: (b * nq + i, 0)),
        out_shape=jax.ShapeDtypeStruct((batch * seq_len, BRANCH_W), BF16),
        compiler_params=_cparams("arbitrary", "arbitrary"),
        name="neighborhood_attn",
    )(proj_l, proj_l, proj_l, proj_c, proj_c, bias, hmask)
    out_c = None
    if need_ctx_out:
        out_c = pl.pallas_call(
            _na_ctx_kernel,
            grid=(batch,),
            in_specs=[
                pl.BlockSpec((ctx_len, BRANCH_W), lambda b: (b, COL_NQ)),
                pl.BlockSpec((ctx_len, BRANCH_W), lambda b: (b, COL_NK)),
                pl.BlockSpec((ctx_len, BRANCH_W), lambda b: (b, COL_NV)),
                pl.BlockSpec(hmask.shape, lambda b: (0, 0)),
            ],
            out_specs=pl.BlockSpec((ctx_len, BRANCH_W), lambda b: (b, 0)),
            out_shape=jax.ShapeDtypeStruct((batch * ctx_len, BRANCH_W), BF16),
            compiler_params=_cparams("arbitrary"),
            name="context_attn",
        )(proj_c, proj_c, proj_c, hmask)
    return out_l, out_c


def _merge_kernel(x_ref, mod_ref, g_ref, gt0, gt1, gt2, gt3, a_ref, s5_ref, ro_ref, rg_ref, na_ref,
                  wglu_ref, bglu_ref, gn_ref, avg_ref, wb_ref, wo_ref, o_ref, *, tiles_per_mod, mod_base):
    i = pl.program_id(0)
    _, _, gate_a = _mod_rows(mod_ref, i, tiles_per_mod, mod_base, 0)
    z = _gelu_tanh(s5_ref[...]).astype(BF16)
    zf = z.astype(F32)
    b_s5 = (zf * _sigmoid(_dot(z, wglu_ref[...]) + bglu_ref[...])).astype(BF16)
    o = ro_ref[0] + ro_ref[1]
    avg = avg_ref[...]
    hi, lo = _split_bf16(o)
    mu = _dot(hi, avg) + _dot(lo, avg)
    dlt = o - mu
    hi, lo = _split_bf16(dlt * dlt)
    var = _dot(hi, avg) + _dot(lo, avg)
    hn = dlt * lax.rsqrt(var + EPS) * gn_ref[...]
    b_ret = (_silu(rg_ref[...].astype(F32)) * hn).astype(BF16)
    outs = (a_ref[...], b_s5, b_ret, na_ref[...])
    gates = (gt0, gt1, gt2, gt3)
    y = _sigmoid(gates[0][...].astype(F32)) * _dot(outs[0], wb_ref[0])
    for b in range(1, N_BRANCH):
        y = y + _sigmoid(gates[b][...].astype(F32)) * _dot(outs[b], wb_ref[b])
    yo = _dot(y.astype(BF16), wo_ref[...])
    o_ref[...] = x_ref[...] + gate_a * _rms(yo, g_ref[...])


def _merge(x, mod, g1, proj, a, s5y, ret_o, na, lw, *, rows_per_mod, mod_base):
    rows, d = x.shape
    tm = min(512, rows)
    nt = rows // tm

    def row(shape, col=0):
        return pl.BlockSpec(shape, lambda i: (i, col))

    def const(arr):
        return pl.BlockSpec(arr.shape, lambda i: (0,) * arr.ndim)

    kern = functools.partial(_merge_kernel, tiles_per_mod=max(rows_per_mod // tm, 1), mod_base=mod_base)
    ins = [x, mod, g1.reshape(1, d), proj, proj, proj, proj, a, s5y, ret_o, proj, na,
           lw['w_glu'], lw['b_glu'], lw['ret_gn'], lw['avg'], lw['w_branch'], lw['w_out']]
    specs = [
        row((tm, d)), const(mod), pl.BlockSpec((1, d), lambda i: (0, 0)),
        row((tm, d), 0), row((tm, d), 1), row((tm, d), 2), row((tm, d), 3),
        row((tm, BRANCH_W)), row((tm, BRANCH_W)),
        pl.BlockSpec((2, tm, BRANCH_W), lambda i: (0, i, 0)),
        row((tm, BRANCH_W), COL_RG), row((tm, BRANCH_W)),
        const(lw['w_glu']), const(lw['b_glu']), const(lw['ret_gn']), const(lw['avg']),
        const(lw['w_branch']), const(lw['w_out']),
    ]
    return pl.pallas_call(
        kern,
        grid=(nt,),
        in_specs=specs,
        out_specs=row((tm, d)),
        out_shape=jax.ShapeDtypeStruct((rows, d), F32),
        compiler_params=_cparams("arbitrary"),
        name="merge_out",
    )(*ins)


def _ffn_kernel(x_ref, mod_ref, g2_ref, g3_ref, wg_ref, wu_ref, wd_ref, o_ref, h_scr, acc_scr,
                *, tiles_per_mod, mod_base, n_f):
    i = pl.program_id(0)
    f = pl.program_id(1)

    @pl.when(f == 0)
    def _():
        sh, sc, _ = _mod_rows(mod_ref, i, tiles_per_mod, mod_base, 3)
        h_scr[...] = (_rms(x_ref[...], g2_ref[...]) * (1.0 + sc) + sh).astype(BF16)
        acc_scr[...] = jnp.zeros_like(acc_scr)

    h = h_scr[...]
    act = (_silu(_dot(h, wg_ref[...])) * _dot(h, wu_ref[...])).astype(BF16)
    acc_scr[...] += _dot(act, wd_ref[...])

    @pl.when(f == n_f - 1)
    def _():
        _, _, gate_f = _mod_rows(mod_ref, i, tiles_per_mod, mod_base, 3)
        o_ref[...] = x_ref[...] + gate_f * _rms(acc_scr[...], g3_ref[...])


def _ffn_dense(x, mod, g2, g3, wg, wu, wd, *, rows_per_mod, mod_base):
    rows, d = x.shape
    d_ff = wg.shape[1]
    tm = min(512, rows)
    tf = d_ff // 2 if (d_ff // 2) % 128 == 0 else d_ff
    n_f = d_ff // tf
    kern = functools.partial(_ffn_kernel, tiles_per_mod=max(rows_per_mod // tm, 1), mod_base=mod_base, n_f=n_f)
    return pl.pallas_call(
        kern,
        grid=(rows // tm, n_f),
        in_specs=[
            pl.BlockSpec((tm, d), lambda i, f: (i, 0)),
            pl.BlockSpec(mod.shape, lambda i, f: (0, 0)),
            pl.BlockSpec((1, d), lambda i, f: (0, 0)),
            pl.BlockSpec((1, d), lambda i, f: (0, 0)),
            pl.BlockSpec((d, tf), lambda i, f: (0, f)),
            pl.BlockSpec((d, tf), lambda i, f: (0, f)),
            pl.BlockSpec((tf, d), lambda i, f: (f, 0)),
        ],
        out_specs=pl.BlockSpec((tm, d), lambda i, f: (i, 0)),
        out_shape=jax.ShapeDtypeStruct((rows, d), F32),
        scratch_shapes=[pltpu.VMEM((tm, d), BF16), pltpu.VMEM((tm, d), F32)],
        compiler_params=_cparams("arbitrary", "arbitrary"),
        name="ffn_dense",
    )(x, mod, g2.reshape(1, d), g3.reshape(1, d), wg, wu, wd)


def _router_kernel(x_ref, mod_ref, g2_ref, wr_ref, br_ref, h_ref, comb_ref, *, tiles_per_mod, mod_base):
    i = pl.program_id(0)
    sh, sc, _ = _mod_rows(mod_ref, i, tiles_per_mod, mod_base, 3)
    h = _rms(x_ref[...], g2_ref[...]) * (1.0 + sc) + sh
    h_ref[...] = h.astype(BF16)
    h_hi, h_lo = _split_bf16(h)
    w_hi, w_lo = _split_bf16(wr_ref[...])
    logits = _dot(h_hi, w_hi) + _dot(h_lo, w_hi) + _dot(h_hi, w_lo) + br_ref[...]
    lane = lax.broadcasted_iota(jnp.int32, logits.shape, 1)
    v1 = jnp.max(logits, axis=-1, keepdims=True)
    i1 = jnp.min(jnp.where(logits == v1, lane, 128), axis=-1, keepdims=True)
    rest = jnp.where(lane == i1, NEG_BIG, logits)
    v2 = jnp.max(rest, axis=-1, keepdims=True)
    i2 = jnp.min(jnp.where(rest == v2, lane, 128), axis=-1, keepdims=True)
    e = jnp.exp(v2 - v1)
    w1 = 1.0 / (1.0 + e)
    w2 = e / (1.0 + e)
    comb_ref[...] = jnp.where(lane == i1, w1, 0.0) + jnp.where(lane == i2, w2, 0.0)


def _router(x, mod, g2, w_router, b_router, *, rows_per_mod, mod_base):
    rows, d = x.shape
    tm = min(512, rows)
    wr = jnp.zeros((d, 128), F32).at[:, :N_EXPERTS].set(w_router)
    br = jnp.full((1, 128), NEG_BIG, F32).at[0, :N_EXPERTS].set(b_router)
    kern = functools.partial(_router_kernel, tiles_per_mod=max(rows_per_mod // tm, 1), mod_base=mod_base)
    return pl.pallas_call(
        kern,
        grid=(rows // tm,),
        in_specs=[
            pl.BlockSpec((tm, d), lambda i: (i, 0)),
            pl.BlockSpec(mod.shape, lambda i: (0, 0)),
            pl.BlockSpec((1, d), lambda i: (0, 0)),
            pl.BlockSpec((d, 128), lambda i: (0, 0)),
            pl.BlockSpec((1, 128), lambda i: (0, 0)),
        ],
        out_specs=[pl.BlockSpec((tm, d), lambda i: (i, 0)), pl.BlockSpec((tm, 128), lambda i: (i, 0))],
        out_shape=[jax.ShapeDtypeStruct((rows, d), BF16), jax.ShapeDtypeStruct((rows, 128), F32)],
        compiler_params=_cparams("arbitrary"),
        name="moe_router",
    )(x, mod, g2.reshape(1, d), wr, br)


def _moe_kernel(x_ref, h_ref, comb_ref, mod_ref, g3_ref, wg_ref, wu_ref, wd_ref, o_ref, acc_scr,
                *, tiles_per_mod, mod_base, n_f):
    i = pl.program_id(0)
    e = pl.program_id(1)
    f = pl.program_id(2)

    @pl.when((e == 0) & (f == 0))
    def _():
        acc_scr[...] = jnp.zeros_like(acc_scr)

    h = h_ref[...]
    comb = comb_ref[...]
    lane = lax.broadcasted_iota(jnp.int32, comb.shape, 1)
    ce = jnp.sum(jnp.where(lane == e, comb, 0.0), axis=-1, keepdims=True)
    act = (_silu(_dot(h, wg_ref[...])) * _dot(h, wu_ref[...])).astype(BF16)
    acc_scr[...] += ce * _dot(act, wd_ref[...])

    @pl.when((e == N_EXPERTS - 1) & (f == n_f - 1))
    def _():
        _, _, gate_f = _mod_rows(mod_ref, i, tiles_per_mod, mod_base, 3)
        o_ref[...] = x_ref[...] + gate_f * _rms(acc_scr[...], g3_ref[...])


def _moe_dense(x, h, comb, mod, g3, wg, wu, wd, *, rows_per_mod, mod_base):
    rows, d = x.shape
    d_ff = wg.shape[2]
    tm = min(512, rows)
    tf = 512
    n_f = d_ff // tf
    kern = functools.partial(_moe_kernel, tiles_per_mod=max(rows_per_mod // tm, 1), mod_base=mod_base, n_f=n_f)
    return pl.pallas_call(
        kern,
        grid=(rows // tm, N_EXPERTS, n_f),
        in_specs=[
            pl.BlockSpec((tm, d), lambda i, e, f: (i, 0)),
            pl.BlockSpec((tm, d), lambda i, e, f: (i, 0)),
            pl.BlockSpec((tm, 128), lambda i, e, f: (i, 0)),
            pl.BlockSpec(mod.shape, lambda i, e, f: (0, 0)),
            pl.BlockSpec((1, d), lambda i, e, f: (0, 0)),
            pl.BlockSpec((None, d, tf), lambda i, e, f: (e, 0, f)),
            pl.BlockSpec((None, d, tf), lambda i, e, f: (e, 0, f)),
            pl.BlockSpec((None, tf, d), lambda i, e, f: (e, f, 0)),
        ],
        out_specs=pl.BlockSpec((tm, d), lambda i, e, f: (i, 0)),
        out_shape=jax.ShapeDtypeStruct((rows, d), F32),
        scratch_shapes=[pltpu.VMEM((tm, d), F32)],
        compiler_params=_cparams("arbitrary", "arbitrary", "arbitrary"),
        name="moe_experts",
    )(x, h, comb, mod, g3.reshape(1, d), wg, wu, wd)


def _permute_w_in(w_in):
    nb = 9 * BRANCH_W
    return jnp.concatenate([w_in[:, nb:], w_in[:, :nb]], axis=1).astype(BF16)


def kernel(x, c, ctx, c_ctx, w_mod, b_mod, norm_g, w_in, s5_a_re, s5_a_im, s5_log_dt, s5_b_re, s5_b_im, s5_c_re, s5_c_im, s5_d, s5_w_glu, s5_b_glu, ret_decay, ret_gn, na_rpb, w_branch, w_out, ffn_w_gate, ffn_w_up, ffn_w_down, moe_w_router, moe_b_router, moe_w_gate, moe_w_up, moe_w_down):
    batch, seq_len, d = x.shape
    ctx_len = ctx.shape[1]
    depth = w_mod.shape[0]
    cond = jnp.concatenate([c, c_ctx[None, :]], axis=0)
    mod_all = _modulation(cond, w_mod, b_mod)
    rope = _rope_tables(seq_len)
    lane_h = np.repeat(np.arange(RET_HEADS), RET_DIM)
    avg = jnp.asarray((lane_h[:, None] == lane_h[None, :]).astype(np.float32) / RET_DIM, BF16)

    xl = x.reshape(batch * seq_len, d)
    xc = ctx.reshape(batch * ctx_len, d)
    lat = dict(rows_per_mod=seq_len, mod_base=0)
    cxt = dict(rows_per_mod=batch * ctx_len, mod_base=batch)

    for layer in range(depth):
        last = layer == depth - 1
        need_ctx = not last
        mod = mod_all[layer]
        ng = norm_g[layer]
        w_in_bf = _permute_w_in(w_in[layer])
        s5_tabs = _s5_tables(s5_a_re[layer], s5_a_im[layer], s5_log_dt[layer], s5_b_re[layer], s5_b_im[layer],
                             s5_c_re[layer], s5_c_im[layer], s5_d[layer], batch)
        ret_tabs = _ret_tables(ret_decay[layer])
        na_bias, na_hmask = _na_tables(na_rpb[layer])
        lw = dict(w_glu=s5_w_glu[layer].astype(BF16), b_glu=s5_b_glu[layer].reshape(1, BRANCH_W).astype(F32),
                  ret_gn=ret_gn[layer].reshape(1, BRANCH_W).astype(F32), avg=avg,
                  w_branch=w_branch[layer].astype(BF16), w_out=w_out[layer].astype(BF16))

        proj_l, f_l = _in_proj(xl, mod, ng[0], w_in_bf, **lat)
        proj_c, f_c = _in_proj(xc, mod, ng[0], w_in_bf, **cxt)

        a_l = _fourier_latent(f_l, batch, seq_len)
        s_l, s_c = _s5_mixer(proj_l[:, COL_S * BRANCH_W:(COL_S + 1) * BRANCH_W],
                             proj_c[:, COL_S * BRANCH_W:(COL_S + 1) * BRANCH_W], s5_tabs, batch)
        r_l, r_c = _retention(proj_l, proj_c, ret_tabs, rope, batch, seq_len, ctx_len)
        n_l, n_c = _neighborhood(proj_l, proj_c, na_bias, na_hmask, batch, seq_len, ctx_len, need_ctx)

        xl = _merge(xl, mod, ng[1], proj_l, a_l, s_l, r_l, n_l, lw, **lat)
        if need_ctx:
            a_c = _fourier_ctx(f_c, batch, ctx_len)
            xc = _merge(xc, mod, ng[1], proj_c, a_c, s_c, r_c, n_c, lw, **cxt)

        i = layer // 2
        if layer % 2 == 0:
            wg, wu, wd = ffn_w_gate[i].astype(BF16), ffn_w_up[i].astype(BF16), ffn_w_down[i].astype(BF16)
            xl = _ffn_dense(xl, mod, ng[2], ng[3], wg, wu, wd, **lat)
            if need_ctx:
                xc = _ffn_dense(xc, mod, ng[2], ng[3], wg, wu, wd, **cxt)
        else:
            wg, wu, wd = moe_w_gate[i].astype(BF16), moe_w_up[i].astype(BF16), moe_w_down[i].astype(BF16)
            h, comb = _router(xl, mod, ng[2], moe_w_router[i], moe_b_router[i], **lat)
            xl = _moe_dense(xl, h, comb, mod, ng[3], wg, wu, wd, **lat)
            if need_ctx:
                hc, combc = _router(xc, mod, ng[2], moe_w_router[i], moe_b_router[i], **cxt)
                xc = _moe_dense(xc, hc, combc, mod, ng[3], wg, wu, wd, **cxt)
    return xl.reshape(batch, seq_len, d)
```

```python
import functools
import math

import numpy as np
import jax
import jax.numpy as jnp
from jax import lax
from jax.experimental import pallas as pl
from jax.experimental.pallas import tpu as pltpu
from jax.experimental.pallas import tpu_sc as plsc

F32 = jnp.float32
BF16 = jnp.bfloat16

D_MODEL = 1024
BRANCH_W = 256
N_BRANCH = 4
GRID_W = 64
FNET_GROUP_DIM = 64
S5_GROUP_CH = 16
S5_GROUPS = 16
S5_STATE = 64
S5_CHUNK = 32
S5_PAIRS = S5_GROUPS // 2
RET_HEADS = 4
RET_DIM = 64
RET_CHUNK = 128
NA_HEADS = 4
NA_DIM = 64
NA_WIN_ROWS = 8
NA_WIN_COLS = 16
NA_QROWS = 8
ROPE_BASE = 10000.0
N_EXPERTS = 8
EPS = 1e-6
FFT_N2 = 256
NEG_BIG = -1e30
VMEM_LIMIT_BYTES = 50 * 1024 * 1024
SC_CORES = 2
SC_SUBCORES = 16
SC_WORKERS = SC_CORES * SC_SUBCORES
SC_GATHER_ROWS = 64
MOE_ROW_TILE = 512

COL_F, COL_S, COL_RQ, COL_RK, COL_RV, COL_RG, COL_NQ, COL_NK, COL_NV = range(16, 25)
IN_W = 9 * BRANCH_W + N_BRANCH * D_MODEL
IN_TN = 1280
IN_F_TILE = (N_BRANCH * D_MODEL) // IN_TN
IN_F_OFF = N_BRANCH * D_MODEL - IN_F_TILE * IN_TN


def _cparams(*sem):
    return pltpu.CompilerParams(dimension_semantics=sem, vmem_limit_bytes=VMEM_LIMIT_BYTES)


def _sigmoid(v):
    return 1.0 / (1.0 + jnp.exp(-v))


def _silu(v):
    return v * _sigmoid(v)


def _gelu_tanh(v):
    return 0.5 * v * (1.0 + jnp.tanh(math.sqrt(2.0 / math.pi) * (v + 0.044715 * (v * v * v))))


def _rms(v, g):
    ms = jnp.mean(v * v, axis=-1, keepdims=True)
    return v * lax.rsqrt(ms + EPS) * g


def _split_bf16(v):
    hi = v.astype(BF16)
    lo = (v - hi.astype(F32)).astype(BF16)
    return hi, lo


def _pack_pairs(v):
    n = v.shape[1] // 2
    lo = lax.bitcast_convert_type(v[:, :n].astype(BF16).astype(F32), jnp.int32)
    hi = lax.bitcast_convert_type(v[:, n:].astype(BF16).astype(F32), jnp.int32)
    return (hi & -65536) | ((lo >> 16) & 65535)


def _unpack_pairs(w):
    lo = lax.bitcast_convert_type(w << 16, F32)
    hi = lax.bitcast_convert_type(w & -65536, F32)
    return jnp.concatenate([lo, hi], axis=-1)


def _dot(a, b):
    return jnp.dot(a, b, preferred_element_type=F32)


def _dot_nt(a, b):
    return lax.dot_general(a, b, (((1,), (1,)), ((), ())), preferred_element_type=F32)


def _dot_tn(a, b):
    return lax.dot_general(a, b, (((0,), (0,)), ((), ())), preferred_element_type=F32)


def _mod_kernel(ct_ref, w_ref, b_ref, o_ref):
    ct = ct_ref[...]
    s = _silu(ct)
    w = w_ref[...]
    rows = [jnp.sum(w * s[:, r:r + 1], axis=0, keepdims=True) for r in range(8)]
    o_ref[...] = jnp.concatenate(rows, axis=0) + b_ref[...]


def _modulation(cond, w_mod, b_mod):
    n_layers, d, n = w_mod.shape
    tn = 512
    ct = jnp.zeros((8, d), F32).at[:cond.shape[0]].set(cond).T
    return pl.pallas_call(
        _mod_kernel,
        grid=(n_layers, n // tn),
        in_specs=[
            pl.BlockSpec((d, 8), lambda l, j: (0, 0)),
            pl.BlockSpec((None, d, tn), lambda l, j: (l, 0, j)),
            pl.BlockSpec((None, 1, tn), lambda l, j: (l, 0, j)),
        ],
        out_specs=pl.BlockSpec((None, 8, tn), lambda l, j: (l, 0, j)),
        out_shape=jax.ShapeDtypeStruct((n_layers, 8, n), F32),
        compiler_params=_cparams("arbitrary", "arbitrary"),
        name="adaln_mod",
    )(ct, w_mod, b_mod.reshape(n_layers, 1, n))


def _mod_rows(mod_ref, i, tiles_per_mod, mod_base, first):
    r = mod_base + i // tiles_per_mod
    return [mod_ref[pl.ds(r, 1), (first + k) * D_MODEL:(first + k + 1) * D_MODEL] for k in range(3)]


def _in_kernel(x_ref, mod_ref, g_ref, w_ref, proj_ref, f_ref, h_scr, *, tiles_per_mod, mod_base):
    i = pl.program_id(0)
    j = pl.program_id(1)

    @pl.when(j == 0)
    def _():
        sh, sc, _ = _mod_rows(mod_ref, i, tiles_per_mod, mod_base, 0)
        h_scr[...] = (_rms(x_ref[...], g_ref[...]) * (1.0 + sc) + sh).astype(BF16)

    res = _dot(h_scr[...], w_ref[...])
    proj_ref[...] = res.astype(BF16)

    @pl.when(j == IN_F_TILE)
    def _():
        f_ref[...] = res[:, IN_F_OFF:IN_F_OFF + BRANCH_W].astype(BF16)


def _in_proj(x, mod, g, w_bf, *, rows_per_mod, mod_base):
    rows, d = x.shape
    tm = min(512, rows)
    kern = functools.partial(_in_kernel, tiles_per_mod=max(rows_per_mod // tm, 1), mod_base=mod_base)
    return pl.pallas_call(
        kern,
        grid=(rows // tm, IN_W // IN_TN),
        in_specs=[
            pl.BlockSpec((tm, d), lambda i, j: (i, 0)),
            pl.BlockSpec(mod.shape, lambda i, j: (0, 0)),
            pl.BlockSpec((1, d), lambda i, j: (0, 0)),
            pl.BlockSpec((d, IN_TN), lambda i, j: (0, j)),
        ],
        out_specs=[
            pl.BlockSpec((tm, IN_TN), lambda i, j: (i, j)),
            pl.BlockSpec((tm, BRANCH_W), lambda i, j: (i, 0)),
        ],
        out_shape=[
            jax.ShapeDtypeStruct((rows, IN_W), BF16),
            jax.ShapeDtypeStruct((rows, BRANCH_W), BF16),
        ],
        scratch_shapes=[pltpu.VMEM((tm, d), BF16)],
        compiler_params=_cparams("arbitrary", "arbitrary"),
        name="in_proj",
    )(x, mod, g.reshape(1, d), w_bf)


def _fft_a_kernel(x_ref, cs_ref, tc_ref, ts_ref, zr_ref, zi_ref, *, n1, n1p):
    y = _dot(cs_ref[...].astype(BF16), x_ref[...])
    yr = y[:n1]
    yi = y[n1p:n1p + n1]
    tc = tc_ref[...]
    ts = ts_ref[...]
    zr_ref[...] = (yr * tc + yi * ts).astype(BF16)
    zi_ref[...] = (yi * tc - yr * ts).astype(BF16)


def _fft_b_kernel(zr_ref, zi_ref, cs_ref, cc_ref, sc_ref, o_ref, *, kb, scale, has_imag):
    cs = cs_ref[...].astype(BF16)
    cc = cc_ref[...].astype(BF16)
    sc = sc_ref[...].astype(BF16)
    for kk in range(kb):
        a = _dot(cs, zr_ref[kk])
        if has_imag:
            b = _dot(cs, zi_ref[kk])
            xr = a[:FFT_N2] + b[FFT_N2:]
            xi = b[:FFT_N2] - a[FFT_N2:]
        else:
            xr = a[:FFT_N2]
            xi = -a[FFT_N2:]
        out = _dot(xr.astype(BF16), cc) + _dot(xi.astype(BF16), sc)
        o_ref[:, kk * BRANCH_W:(kk + 1) * BRANCH_W] = (out * scale).astype(BF16)


def _dft_tables(n):
    k = np.arange(n)
    ang = 2.0 * np.pi * ((k[:, None] * k[None, :]) % n) / n
    return np.cos(ang), np.sin(ang)


def _fft_b_call(zr, zi, n1, batch, seq_len, has_imag):
    c2, s2 = _dft_tables(FFT_N2)
    cs2 = jnp.asarray(np.concatenate([c2, s2], axis=0), F32)
    c64, s64 = _dft_tables(FNET_GROUP_DIM)
    eye = np.eye(BRANCH_W // FNET_GROUP_DIM)
    cc = jnp.asarray(np.kron(eye, c64), F32)
    sc = jnp.asarray(np.kron(eye, s64), F32)
    kb = min(8, n1)
    scale = 1.0 / math.sqrt(seq_len * FNET_GROUP_DIM)
    kern = functools.partial(_fft_b_kernel, kb=kb, scale=scale, has_imag=has_imag)
    zspec = pl.BlockSpec((None, kb, FFT_N2, BRANCH_W), lambda b, i: (b, i, 0, 0))
    out = pl.pallas_call(
        kern,
        grid=(batch, n1 // kb),
        in_specs=[
            zspec, zspec,
            pl.BlockSpec((2 * FFT_N2, FFT_N2), lambda b, i: (0, 0)),
            pl.BlockSpec((BRANCH_W, BRANCH_W), lambda b, i: (0, 0)),
            pl.BlockSpec((BRANCH_W, BRANCH_W), lambda b, i: (0, 0)),
        ],
        out_specs=pl.BlockSpec((None, FFT_N2, kb * BRANCH_W), lambda b, i: (b, 0, i)),
        out_shape=jax.ShapeDtypeStruct((batch, FFT_N2, n1 * BRANCH_W), BF16),
        compiler_params=_cparams("arbitrary", "arbitrary"),
        name="fourier_stage_b",
    )(zr, zi, cs2, cc, sc)
    return out.reshape(batch * seq_len, BRANCH_W)


def _fourier_latent(f, batch, seq_len):
    n1 = seq_len // FFT_N2
    wide = FFT_N2 * BRANCH_W
    c1, s1 = _dft_tables(n1)
    n1p = max(n1, 8)
    cs1 = np.zeros((2 * n1p, n1))
    cs1[:n1] = c1
    cs1[n1p:n1p + n1] = -s1
    k1 = np.arange(n1)[:, None]
    l2 = np.arange(FFT_N2)[None, :]
    tw = 2.0 * np.pi * (k1 * l2) / seq_len
    tc = jnp.asarray(np.repeat(np.cos(tw), BRANCH_W, axis=1), F32)
    ts = jnp.asarray(np.repeat(np.sin(tw), BRANCH_W, axis=1), F32)
    cw = min(8192, wide)
    xv = f.reshape(batch, n1, wide)
    spec = pl.BlockSpec((None, n1, cw), lambda b, j: (b, 0, j))
    tspec = pl.BlockSpec((n1, cw), lambda b, j: (0, j))
    zr, zi = pl.pallas_call(
        functools.partial(_fft_a_kernel, n1=n1, n1p=n1p),
        grid=(batch, wide // cw),
        in_specs=[spec, pl.BlockSpec((2 * n1p, n1), lambda b, j: (0, 0)), tspec, tspec],
        out_specs=[spec, spec],
        out_shape=[jax.ShapeDtypeStruct((batch, n1, wide), BF16)] * 2,
        compiler_params=_cparams("arbitrary", "arbitrary"),
        name="fourier_stage_a",
    )(xv, jnp.asarray(cs1, F32), tc, ts)
    zr = zr.reshape(batch, n1, FFT_N2, BRANCH_W)
    zi = zi.reshape(batch, n1, FFT_N2, BRANCH_W)
    return _fft_b_call(zr, zi, n1, batch, seq_len, True)


def _fourier_ctx(f, batch, ctx_len):
    assert ctx_len == FFT_N2
    z = f.reshape(batch, 1, FFT_N2, BRANCH_W)
    return _fft_b_call(z, z, 1, batch, ctx_len, False)


def _s5_tables(a_re, a_im, log_dt, b_re, b_im, c_re, c_im, d_skip, batch):
    t = S5_CHUNK
    g, p, hc = S5_GROUPS, S5_STATE, S5_GROUP_CH
    lam = lax.complex(a_re.astype(F32), a_im.astype(F32))
    dt = jnp.exp(log_dt.astype(F32))[..., None]
    ks = jnp.arange(t + 1, dtype=F32)
    apow = jnp.exp((lam * dt)[..., None] * ks)
    a_bar = apow[..., 1]
    b_bar = ((a_bar - 1.0) / lam)[..., None] * lax.complex(b_re.astype(F32), b_im.astype(F32))
    cm = lax.complex(c_re.astype(F32), c_im.astype(F32))
    kimp = jnp.real(jnp.einsum('dghp,dgpk,dgpj->dgkhj', cm, apow[..., :t], b_bar,
                               precision=lax.Precision.HIGHEST))
    kf, kb = kimp[0], kimp[1]
    kfull = jnp.concatenate([kb[:, :0:-1], kf[:, :1] + kb[:, :1], kf[:, 1:]], axis=1)
    jj, tt = np.meshgrid(np.arange(t), np.arange(t), indexing='ij')
    toep = np.zeros((t, t, 2 * t - 1), np.float32)
    toep[jj, tt, tt - jj + t - 1] = 1.0
    m = jnp.einsum('jtl,glhk->gjkth', jnp.asarray(toep), kfull, precision=lax.Precision.HIGHEST)
    m = m.reshape(S5_PAIRS, 2, t * hc, t * hc).astype(BF16)

    wf = jnp.einsum('gpj,gph->gjhp', apow[0][..., t - 1::-1][..., :t], b_bar[0])
    wb = jnp.einsum('gpj,gph->gjhp', apow[1][..., :t], b_bar[1])
    wf = wf.reshape(g, t * hc, p)
    wb = wb.reshape(g, t * hc, p)
    kinds = [jnp.real(wf), jnp.imag(wf), jnp.real(wb), jnp.imag(wb)]
    eye2 = jnp.eye(2, dtype=F32)
    we_cols = []
    for kd in kinds:
        kp = kd.reshape(S5_PAIRS, 2, t * hc, p)
        blk = jnp.einsum('qirp,ij->qirjp', kp, eye2).reshape(S5_PAIRS, 2 * t * hc, 2 * p)
        we_cols.append(blk)
    we = jnp.concatenate(we_cols, axis=-1).astype(BF16)

    vf = jnp.einsum('ghp,gpt->gpth', cm[0], apow[0][..., 1:t + 1])
    vb = jnp.einsum('ghp,gpt->gpth', cm[1], apow[1][..., t:0:-1])
    vf = vf.reshape(g, p, t * hc)
    vb = vb.reshape(g, p, t * hc)
    vkinds = [jnp.real(vf), -jnp.imag(vf), jnp.real(vb), -jnp.imag(vb)]
    v_rows = []
    for kd in vkinds:
        kp = kd.reshape(S5_PAIRS, 2, p, t * hc)
        blk = jnp.einsum('qipc,ij->qipjc', kp, eye2).reshape(S5_PAIRS, 2 * p, 2 * t * hc)
        v_rows.append(blk)
    v1 = jnp.concatenate(v_rows, axis=1)
    v = jnp.concatenate([v1, v1], axis=1).astype(BF16)

    def lanes(z):
        return jnp.tile(z.reshape(1, g * p), (1, batch))

    at = apow[..., t]
    a_tab = jnp.concatenate([lanes(jnp.real(at[0])), lanes(jnp.imag(at[0])),
                             lanes(jnp.real(at[1])), lanes(jnp.imag(at[1]))], axis=0)
    dvec = jnp.tile(d_skip.astype(F32).reshape(S5_PAIRS, 2, 1, hc), (1, 1, t, 1)).reshape(S5_PAIRS, 1, 2 * t * hc)
    return dict(m=m, we=we, v=v, a_tab=a_tab, dvec=dvec)


def _s5_e_kernel(u_ref, we_ref, ref_, imf_, reb_, imb_):
    e = _dot(u_ref[...], we_ref[...])
    ref_[...] = e[:, 0:128]
    imf_[...] = e[:, 128:256]
    reb_[...] = e[:, 256:384]
    imb_[...] = e[:, 384:512]


def _s5_scan_kernel(a_ref, ref_, imf_, reb_, imb_, prf, pif, prb, pib, *, n_rows, n_ctx):
    afr = a_ref[0:1, :]
    afi = a_ref[1:2, :]
    abr = a_ref[2:3, :]
    abi = a_ref[3:4, :]
    zero = jnp.zeros_like(afr)

    def body(s, carry):
        sfr, sfi, sbr, sbi = carry
        nf = s
        nb = jnp.where(s < n_ctx, n_ctx - 1 - s, n_rows - 1 + n_ctx - s)
        prf[pl.ds(nf, 1), :] = sfr
        pif[pl.ds(nf, 1), :] = sfi
        prb[pl.ds(nb, 1), :] = sbr
        pib[pl.ds(nb, 1), :] = sbi
        efr = ref_[pl.ds(nf, 1), :]
        efi = imf_[pl.ds(nf, 1), :]
        ebr = reb_[pl.ds(nb, 1), :]
        ebi = imb_[pl.ds(nb, 1), :]
        nfr = afr * sfr - afi * sfi + efr
        nfi = afr * sfi + afi * sfr + efi
        nbr = abr * sbr - abi * sbi + ebr
        nbi = abr * sbi + abi * sbr + ebi
        return nfr, nfi, nbr, nbi

    lax.fori_loop(0, n_rows, body, (zero, zero, zero, zero))


def _s5_y_kernel(u_ref, m_ref, v_ref, d_ref, prf, pif, prb, pib, y_ref):
    u = u_ref[...]
    half = S5_CHUNK * S5_GROUP_CH
    y_intra = jnp.concatenate([_dot(u[:, :half], m_ref[0]), _dot(u[:, half:], m_ref[1])], axis=-1)
    pcat = jnp.concatenate([prf[...], pif[...], prb[...], pib[...]], axis=-1)
    hi, lo = _split_bf16(pcat)
    y_cross = _dot(jnp.concatenate([hi, lo], axis=-1), v_ref[...])
    y_ref[...] = y_intra + y_cross + d_ref[...] * u.astype(F32)


def _s5_core(u, tabs, batch, n_rows, n_ctx):
    width = batch * S5_PAIRS * 128
    cols = 2 * S5_CHUNK * S5_GROUP_CH
    u_spec = pl.BlockSpec((None, None, n_rows, cols), lambda q, b: (q, b, 0, 0))
    st_spec = pl.BlockSpec((n_rows, 128), lambda q, b: (0, b * S5_PAIRS + q))
    st_shape = jax.ShapeDtypeStruct((n_rows, width), F32)
    e4 = pl.pallas_call(
        _s5_e_kernel,
        grid=(S5_PAIRS, batch),
        in_specs=[u_spec, pl.BlockSpec((None, cols, 512), lambda q, b: (q, 0, 0))],
        out_specs=[st_spec] * 4,
        out_shape=[st_shape] * 4,
        compiler_params=_cparams("arbitrary", "arbitrary"),
        name="s5_chunk_states",
    )(u, tabs['we'])
    p4 = pl.pallas_call(
        functools.partial(_s5_scan_kernel, n_rows=n_rows, n_ctx=n_ctx),
        out_shape=[st_shape] * 4,
        compiler_params=pltpu.CompilerParams(vmem_limit_bytes=VMEM_LIMIT_BYTES),
        name="s5_state_scan",
    )(tabs['a_tab'], *e4)
    y = pl.pallas_call(
        _s5_y_kernel,
        grid=(S5_PAIRS, batch),
        in_specs=[
            u_spec,
            pl.BlockSpec((None, 2, cols // 2, cols // 2), lambda q, b: (q, 0, 0, 0)),
            pl.BlockSpec((None, cols, cols), lambda q, b: (q, 0, 0)),
            pl.BlockSpec((None, 1, cols), lambda q, b: (q, 0, 0)),
            st_spec, st_spec, st_spec, st_spec,
        ],
        out_specs=pl.BlockSpec((None, None, n_rows, cols), lambda q, b: (q, b, 0, 0)),
        out_shape=jax.ShapeDtypeStruct((S5_PAIRS, batch, n_rows, cols), F32),
        compiler_params=_cparams("arbitrary", "arbitrary"),
        name="s5_outputs",
    )(u, tabs['m'], tabs['v'], tabs['dvec'], *p4)
    return y


def _s5_to_chunks(s, batch):
    n = s.shape[0] // batch // S5_CHUNK
    v = s.reshape(batch, n, S5_CHUNK, S5_PAIRS, 2, S5_GROUP_CH)
    return v.transpose(3, 0, 1, 4, 2, 5).reshape(S5_PAIRS, batch, n, 2 * S5_CHUNK * S5_GROUP_CH)


def _s5_from_chunks(y, batch):
    n = y.shape[2]
    v = y.reshape(S5_PAIRS, batch, n, 2, S5_CHUNK, S5_GROUP_CH)
    return v.transpose(1, 2, 4, 0, 3, 5).reshape(batch * n * S5_CHUNK, BRANCH_W)


def _s5_mixer(s_lat, s_ctx, tabs, batch):
    ul = _s5_to_chunks(s_lat, batch)
    uc = _s5_to_chunks(s_ctx, batch)
    n_ctx = uc.shape[2]
    u = jnp.concatenate([uc, ul], axis=2)
    y = _s5_core(u, tabs, batch, u.shape[2], n_ctx)
    return _s5_from_chunks(y[:, :, n_ctx:], batch), _s5_from_chunks(y[:, :, :n_ctx], batch)


def _ret_tables(ret_decay):
    c = RET_CHUNK
    lg = jax.nn.log_sigmoid(ret_decay.astype(F32))
    lane_h = jnp.repeat(jnp.arange(RET_HEADS), RET_DIM)
    lgl = lg[:, lane_h]
    pos = jnp.arange(c, dtype=F32)[:, None]
    qd = jnp.stack([jnp.exp((pos + 1.0) * lgl[0][None]), jnp.exp((c - pos) * lgl[1][None])])
    kd = jnp.stack([jnp.exp((c - 1.0 - pos) * lgl[0][None]), jnp.exp(pos * lgl[1][None])])
    bmask = (lane_h[:, None] == lane_h[None, :]).astype(F32)
    cd = jnp.exp(c * lgl)[:, :, None] * bmask[None]
    diff = pos - pos.T
    dm = []
    for h in range(RET_HEADS):
        fw = jnp.where(diff >= 0, jnp.exp(jnp.maximum(diff, 0.0) * lg[0, h]), 0.0)
        bw = jnp.where(diff <= 0, jnp.exp(jnp.maximum(-diff, 0.0) * lg[1, h]), 0.0)
        dm.append(fw + bw)
    dm = jnp.concatenate(dm, axis=0)
    hmask = (jnp.arange(RET_HEADS)[:, None] == lane_h[None, :]).astype(F32)
    return dict(qd=qd, kd=kd, cd=cd, bmask=bmask, dm=dm, hmask=hmask)


def _rope_tables(n_tokens):
    t = np.arange(n_tokens)
    row = (t // GRID_W).astype(np.float64)
    col = (t % GRID_W).astype(np.float64)
    n_freq = RET_DIM // 4
    inv_freq = 1.0 / (ROPE_BASE ** (np.arange(n_freq, dtype=np.float64) / n_freq))
    ang = np.concatenate([row[:, None] * inv_freq, col[:, None] * inv_freq], axis=-1)
    cos = np.cos(ang)
    sin = np.sin(ang)
    cos_t = np.tile(np.concatenate([cos, cos], axis=-1), (1, RET_HEADS))
    sin_t = np.tile(np.concatenate([-sin, sin], axis=-1), (1, RET_HEADS))
    half = RET_DIM // 2
    perm = np.arange(BRANCH_W) ^ half
    swap = np.zeros((BRANCH_W, BRANCH_W), np.float32)
    swap[perm, np.arange(BRANCH_W)] = 1.0
    return jnp.asarray(cos_t, F32), jnp.asarray(sin_t, F32), jnp.asarray(swap, BF16)


def _ret_chunk(q, k, v, s, qd, kd, cd, bmask, dm, hmask, with_intra):
    cross = _dot((q * qd).astype(BF16), s.astype(BF16))
    s_new = cd * s + bmask * _dot_tn((k * kd).astype(BF16), v)
    if not with_intra:
        return cross, s_new
    qb = q.astype(BF16)
    kb = k.astype(BF16)
    qs = jnp.concatenate([qb * hmask[h:h + 1].astype(BF16) for h in range(RET_HEADS)], axis=0)
    scores = _dot_nt(qs, kb) * dm
    ov = _dot(scores.astype(BF16), v)
    c = q.shape[0]
    inner = ov[0:c] * hmask[0:1]
    for h in range(1, RET_HEADS):
        inner = inner + ov[h * c:(h + 1) * c] * hmask[h:h + 1]
    return inner + cross, s_new


def _ret_kernel(q_ref, k_ref, v_ref, qc_ref, kc_ref, vc_ref, cos_ref, sin_ref, swap_ref,
                qd_ref, kd_ref, cd_ref, bm_ref, dm_ref, hm_ref, o_ref, oc_ref, s_scr, *, n_chunks, n_ctx_chunks):
    dirn = pl.program_id(1)
    i = pl.program_id(2)
    c = RET_CHUNK
    k_scale = RET_DIM ** -0.5
    bmask = bm_ref[...]
    dm = dm_ref[...]
    hmask = hm_ref[...]

    def run(d):
        qd = qd_ref[d]
        kd = kd_ref[d]
        cd = cd_ref[d]
        intra = d == 0

        @pl.when(i == 0)
        def _():
            s = jnp.zeros((BRANCH_W, BRANCH_W), F32)
            order = range(n_ctx_chunks) if d == 0 else range(n_ctx_chunks - 1, -1, -1)
            for cc in order:
                sl = slice(cc * c, (cc + 1) * c)
                o, s = _ret_chunk(qc_ref[sl, :].astype(F32), kc_ref[sl, :].astype(F32) * k_scale, vc_ref[sl, :],
                                  s, qd, kd, cd, bmask, dm, hmask, intra)
                oc_ref[sl, :] = o
            s_scr[...] = s

        swap = swap_ref[...]
        order = range(n_chunks) if d == 0 else range(n_chunks - 1, -1, -1)
        s = s_scr[...]
        for cc in order:
            sl = slice(cc * c, (cc + 1) * c)
            cos = cos_ref[sl, :]
            sin = sin_ref[sl, :]
            qb = q_ref[sl, :]
            kb = k_ref[sl, :]
            q = qb.astype(F32) * cos + _dot(qb, swap) * sin
            k = (kb.astype(F32) * cos + _dot(kb, swap) * sin) * k_scale
            o, s = _ret_chunk(q, k, v_ref[sl, :], s, qd, kd, cd, bmask, dm, hmask, intra)
            o_ref[sl, :] = o
        s_scr[...] = s

    @pl.when(dirn == 0)
    def _():
        run(0)

    @pl.when(dirn == 1)
    def _():
        run(1)


def _retention(proj_l, proj_c, tabs, rope, batch, seq_len, ctx_len):
    n_chunks = 4
    blk = n_chunks * RET_CHUNK
    nblk = seq_len // blk
    cos_t, sin_t, swap = rope

    def pos(d, i):
        return i + d * (nblk - 1 - 2 * i)

    def lat(col):
        return pl.BlockSpec((blk, BRANCH_W), lambda b, d, i: (b * nblk + pos(d, i), col))

    def ctx(col):
        return pl.BlockSpec((ctx_len, BRANCH_W), lambda b, d, i: (b, col))

    def const(shape):
        return pl.BlockSpec(shape, lambda b, d, i: (0,) * len(shape))

    tab_spec = pl.BlockSpec((blk, BRANCH_W), lambda b, d, i: (pos(d, i), 0))
    kern = functools.partial(_ret_kernel, n_chunks=n_chunks, n_ctx_chunks=ctx_len // RET_CHUNK)
    c = RET_CHUNK
    o, oc = pl.pallas_call(
        kern,
        grid=(batch, 2, nblk),
        in_specs=[
            lat(COL_RQ), lat(COL_RK), lat(COL_RV), ctx(COL_RQ), ctx(COL_RK), ctx(COL_RV),
            tab_spec, tab_spec, const((BRANCH_W, BRANCH_W)),
            const((2, c, BRANCH_W)), const((2, c, BRANCH_W)), const((2, BRANCH_W, BRANCH_W)),
            const((BRANCH_W, BRANCH_W)), const((RET_HEADS * c, c)), const((RET_HEADS, BRANCH_W)),
        ],
        out_specs=[
            pl.BlockSpec((None, blk, BRANCH_W), lambda b, d, i: (d, b * nblk + pos(d, i), 0)),
            pl.BlockSpec((None, ctx_len, BRANCH_W), lambda b, d, i: (d, b, 0)),
        ],
        out_shape=[
            jax.ShapeDtypeStruct((2, batch * seq_len, BRANCH_W), F32),
            jax.ShapeDtypeStruct((2, batch * ctx_len, BRANCH_W), F32),
        ],
        scratch_shapes=[pltpu.VMEM((BRANCH_W, BRANCH_W), F32)],
        compiler_params=_cparams("arbitrary", "arbitrary", "arbitrary"),
        name="retention",
    )(proj_l, proj_l, proj_l, proj_c, proj_c, proj_c, cos_t, sin_t, swap,
      tabs['qd'], tabs['kd'], tabs['cd'], tabs['bmask'], tabs['dm'], tabs['hmask'])
    return o, oc


def _na_tables(rpb):
    kr, kw = NA_WIN_ROWS, NA_WIN_COLS
    col = np.arange(GRID_W)
    col_start = np.clip(col - kw // 2, 0, GRID_W - kw)
    in_win = (col[None, :] >= col_start[:, None]) & (col[None, :] < col_start[:, None] + kw)
    dc = np.clip(col[None, :] - col[:, None], -(kw - 1), kw - 1) + (kw - 1)
    var = np.arange(kr)[:, None] + np.arange(kr)[None, :]
    pick_r = (var[:, :, None] == np.arange(2 * kr - 1)[None, None, :]).astype(np.float32)
    pick_c = (dc[:, :, None] == np.arange(2 * kw - 1)[None, None, :]).astype(np.float32)
    bias = jnp.einsum('vir,hrc,qkc->vhqik', jnp.asarray(pick_r), rpb.astype(F32), jnp.asarray(pick_c),
                      precision=lax.Precision.HIGHEST)
    bias = jnp.where(jnp.asarray(in_win)[None, None, :, None, :], bias, NEG_BIG)
    bias = bias.reshape(kr, NA_HEADS * GRID_W, kr * GRID_W)
    lane_h = np.repeat(np.arange(NA_HEADS), NA_DIM)
    hmask = (np.arange(NA_HEADS)[:, None] == lane_h[None, :]).astype(np.float32)
    return bias, jnp.asarray(hmask, F32)


def _attend(qs, keys, vals, bias, kc, vc):
    s_ctx = _dot_nt(qs, kc)
    m = jnp.max(s_ctx, axis=-1, keepdims=True)
    if keys is not None:
        s_band = _dot_nt(qs, keys) + bias
        m = jnp.maximum(m, jnp.max(s_band, axis=-1, keepdims=True))
        p_band = jnp.exp(s_band - m)
    p_ctx = jnp.exp(s_ctx - m)
    l = jnp.sum(p_ctx, axis=-1, keepdims=True)
    o = _dot(p_ctx.astype(BF16), vc)
    if keys is not None:
        l = l + jnp.sum(p_band, axis=-1, keepdims=True)
        o = o + _dot(p_band.astype(BF16), vals)
    return o / l


def _stack_heads(q, hmask_scaled):
    return jnp.concatenate([q * hmask_scaled[h:h + 1] for h in range(NA_HEADS)], axis=0)


def _unstack_heads(o, hmask, n):
    out = o[0:n] * hmask[0:1]
    for h in range(1, NA_HEADS):
        out = out + o[h * n:(h + 1) * n] * hmask[h:h + 1]
    return out


def _na_kernel(q_ref, k_ref, v_ref, kc_ref, vc_ref, bias_ref, hm_ref, o_ref, *, n_grid_rows):
    i = pl.program_id(1)
    hmask = hm_ref[...]
    hms = (hmask * (NA_DIM ** -0.5)).astype(BF16)
    kc = kc_ref[...]
    vc = vc_ref[...]
    band = NA_WIN_ROWS * GRID_W
    for rr in range(NA_QROWS):
        r = i * NA_QROWS + rr
        rs = jnp.clip(r - NA_WIN_ROWS // 2, 0, n_grid_rows - NA_WIN_ROWS)
        var = rs - r + (NA_WIN_ROWS - 1)
        start = pl.multiple_of(rs * GRID_W, GRID_W)
        keys = k_ref[pl.ds(start, band), :]
        vals = v_ref[pl.ds(start, band), :]
        qs = _stack_heads(q_ref[rr * GRID_W:(rr + 1) * GRID_W, :], hms)
        o = _attend(qs, keys, vals, bias_ref[var], kc, vc)
        o_ref[rr * GRID_W:(rr + 1) * GRID_W, :] = _unstack_heads(o, hmask, GRID_W).astype(BF16)


def _na_ctx_kernel(q_ref, kc_ref, vc_ref, hm_ref, o_ref):
    hmask = hm_ref[...]
    hms = (hmask * (NA_DIM ** -0.5)).astype(BF16)
    n = q_ref.shape[0]
    o = _attend(_stack_heads(q_ref[...], hms), None, None, None, kc_ref[...], vc_ref[...])
    o_ref[...] = _unstack_heads(o, hmask, n).astype(BF16)


def _neighborhood(proj_l, proj_c, bias, hmask, batch, seq_len, ctx_len, need_ctx_out):
    rows = seq_len // GRID_W
    qblk = NA_QROWS * GRID_W
    nq = seq_len // qblk
    out_l = pl.pallas_call(
        functools.partial(_na_kernel, n_grid_rows=rows),
        grid=(batch, nq),
        in_specs=[
            pl.BlockSpec((qblk, BRANCH_W), lambda b, i: (b * nq + i, COL_NQ)),
            pl.BlockSpec((seq_len, BRANCH_W), lambda b, i: (b, COL_NK)),
            pl.BlockSpec((seq_len, BRANCH_W), lambda b, i: (b, COL_NV)),
            pl.BlockSpec((ctx_len, BRANCH_W), lambda b, i: (b, COL_NK)),
            pl.BlockSpec((ctx_len, BRANCH_W), lambda b, i: (b, COL_NV)),
            pl.BlockSpec(bias.shape, lambda b, i: (0, 0, 0)),
            pl.BlockSpec(hmask.shape, lambda b, i: (0, 0)),
        ],
        out_specs=pl.BlockSpec((qblk, BRANCH_W), lambda b, i: (b * nq + i, 0)),
        out_shape=jax.ShapeDtypeStruct((batch * seq_len, BRANCH_W), BF16),
        compiler_params=_cparams("arbitrary", "arbitrary"),
        name="neighborhood_attn",
    )(proj_l, proj_l, proj_l, proj_c, proj_c, bias, hmask)
    out_c = None
    if need_ctx_out:
        out_c = pl.pallas_call(
            _na_ctx_kernel,
            grid=(batch,),
            in_specs=[
                pl.BlockSpec((ctx_len, BRANCH_W), lambda b: (b, COL_NQ)),
                pl.BlockSpec((ctx_len, BRANCH_W), lambda b: (b, COL_NK)),
                pl.BlockSpec((ctx_len, BRANCH_W), lambda b: (b, COL_NV)),
                pl.BlockSpec(hmask.shape, lambda b: (0, 0)),
            ],
            out_specs=pl.BlockSpec((ctx_len, BRANCH_W), lambda b: (b, 0)),
            out_shape=jax.ShapeDtypeStruct((batch * ctx_len, BRANCH_W), BF16),
            compiler_params=_cparams("arbitrary"),
            name="context_attn",
        )(proj_c, proj_c, proj_c, hmask)
    return out_l, out_c


def _merge_kernel(x_ref, mod_ref, g_ref, gt0, gt1, gt2, gt3, a_ref, s5_ref, ro_ref, rg_ref, na_ref,
                  wglu_ref, bglu_ref, gn_ref, avg_ref, wb_ref, wo_ref, o_ref, *, tiles_per_mod, mod_base):
    i = pl.program_id(0)
    _, _, gate_a = _mod_rows(mod_ref, i, tiles_per_mod, mod_base, 0)
    z = _gelu_tanh(s5_ref[...]).astype(BF16)
    zf = z.astype(F32)
    b_s5 = (zf * _sigmoid(_dot(z, wglu_ref[...]) + bglu_ref[...])).astype(BF16)
    o = ro_ref[0] + ro_ref[1]
    avg = avg_ref[...]
    hi, lo = _split_bf16(o)
    mu = _dot(hi, avg) + _dot(lo, avg)
    dlt = o - mu
    hi, lo = _split_bf16(dlt * dlt)
    var = _dot(hi, avg) + _dot(lo, avg)
    hn = dlt * lax.rsqrt(var + EPS) * gn_ref[...]
    b_ret = (_silu(rg_ref[...].astype(F32)) * hn).astype(BF16)
    outs = (a_ref[...], b_s5, b_ret, na_ref[...])
    gates = (gt0, gt1, gt2, gt3)
    y = _sigmoid(gates[0][...].astype(F32)) * _dot(outs[0], wb_ref[0])
    for b in range(1, N_BRANCH):
        y = y + _sigmoid(gates[b][...].astype(F32)) * _dot(outs[b], wb_ref[b])
    yo = _dot(y.astype(BF16), wo_ref[...])
    o_ref[...] = x_ref[...] + gate_a * _rms(yo, g_ref[...])


def _merge(x, mod, g1, proj, a, s5y, ret_o, na, lw, *, rows_per_mod, mod_base):
    rows, d = x.shape
    tm = min(512, rows)
    nt = rows // tm

    def row(shape, col=0):
        return pl.BlockSpec(shape, lambda i: (i, col))

    def const(arr):
        return pl.BlockSpec(arr.shape, lambda i: (0,) * arr.ndim)

    kern = functools.partial(_merge_kernel, tiles_per_mod=max(rows_per_mod // tm, 1), mod_base=mod_base)
    ins = [x, mod, g1.reshape(1, d), proj, proj, proj, proj, a, s5y, ret_o, proj, na,
           lw['w_glu'], lw['b_glu'], lw['ret_gn'], lw['avg'], lw['w_branch'], lw['w_out']]
    specs = [
        row((tm, d)), const(mod), pl.BlockSpec((1, d), lambda i: (0, 0)),
        row((tm, d), 0), row((tm, d), 1), row((tm, d), 2), row((tm, d), 3),
        row((tm, BRANCH_W)), row((tm, BRANCH_W)),
        pl.BlockSpec((2, tm, BRANCH_W), lambda i: (0, i, 0)),
        row((tm, BRANCH_W), COL_RG), row((tm, BRANCH_W)),
        const(lw['w_glu']), const(lw['b_glu']), const(lw['ret_gn']), const(lw['avg']),
        const(lw['w_branch']), const(lw['w_out']),
    ]
    return pl.pallas_call(
        kern,
        grid=(nt,),
        in_specs=specs,
        out_specs=row((tm, d)),
        out_shape=jax.ShapeDtypeStruct((rows, d), F32),
        compiler_params=_cparams("arbitrary"),
        name="merge_out",
    )(*ins)


def _ffn_kernel(x_ref, mod_ref, g2_ref, g3_ref, wg_ref, wu_ref, wd_ref, o_ref, h_scr, acc_scr,
                *, tiles_per_mod, mod_base, n_f):
    i = pl.program_id(0)
    f = pl.program_id(1)

    @pl.when(f == 0)
    def _():
        sh, sc, _ = _mod_rows(mod_ref, i, tiles_per_mod, mod_base, 3)
        h_scr[...] = (_rms(x_ref[...], g2_ref[...]) * (1.0 + sc) + sh).astype(BF16)
        acc_scr[...] = jnp.zeros_like(acc_scr)

    h = h_scr[...]
    act = (_silu(_dot(h, wg_ref[...])) * _dot(h, wu_ref[...])).astype(BF16)
    acc_scr[...] += _dot(act, wd_ref[...])

    @pl.when(f == n_f - 1)
    def _():
        _, _, gate_f = _mod_rows(mod_ref, i, tiles_per_mod, mod_base, 3)
        o_ref[...] = x_ref[...] + gate_f * _rms(acc_scr[...], g3_ref[...])


def _ffn_dense(x, mod, g2, g3, wg, wu, wd, *, rows_per_mod, mod_base):
    rows, d = x.shape
    d_ff = wg.shape[1]
    tm = min(512, rows)
    tf = d_ff // 2 if (d_ff // 2) % 128 == 0 else d_ff
    n_f = d_ff // tf
    kern = functools.partial(_ffn_kernel, tiles_per_mod=max(rows_per_mod // tm, 1), mod_base=mod_base, n_f=n_f)
    return pl.pallas_call(
        kern,
        grid=(rows // tm, n_f),
        in_specs=[
            pl.BlockSpec((tm, d), lambda i, f: (i, 0)),
            pl.BlockSpec(mod.shape, lambda i, f: (0, 0)),
            pl.BlockSpec((1, d), lambda i, f: (0, 0)),
            pl.BlockSpec((1, d), lambda i, f: (0, 0)),
            pl.BlockSpec((d, tf), lambda i, f: (0, f)),
            pl.BlockSpec((d, tf), lambda i, f: (0, f)),
            pl.BlockSpec((tf, d), lambda i, f: (f, 0)),
        ],
        out_specs=pl.BlockSpec((tm, d), lambda i, f: (i, 0)),
        out_shape=jax.ShapeDtypeStruct((rows, d), F32),
        scratch_shapes=[pltpu.VMEM((tm, d), BF16), pltpu.VMEM((tm, d), F32)],
        compiler_params=_cparams("arbitrary", "arbitrary"),
        name="ffn_dense",
    )(x, mod, g2.reshape(1, d), g3.reshape(1, d), wg, wu, wd)


def _router_kernel(x_ref, mod_ref, g2_ref, wr_ref, br_ref, h_ref, comb_ref, *, tiles_per_mod, mod_base):
    i = pl.program_id(0)
    sh, sc, _ = _mod_rows(mod_ref, i, tiles_per_mod, mod_base, 3)
    h = _rms(x_ref[...], g2_ref[...]) * (1.0 + sc) + sh
    h_ref[...] = _pack_pairs(h)
    h_hi, h_lo = _split_bf16(h)
    w_hi, w_lo = _split_bf16(wr_ref[...])
    logits = _dot(h_hi, w_hi) + _dot(h_lo, w_hi) + _dot(h_hi, w_lo) + br_ref[...]
    lane = lax.broadcasted_iota(jnp.int32, logits.shape, 1)
    v1 = jnp.max(logits, axis=-1, keepdims=True)
    i1 = jnp.min(jnp.where(logits == v1, lane, 128), axis=-1, keepdims=True)
    rest = jnp.where(lane == i1, NEG_BIG, logits)
    v2 = jnp.max(rest, axis=-1, keepdims=True)
    i2 = jnp.min(jnp.where(rest == v2, lane, 128), axis=-1, keepdims=True)
    e = jnp.exp(v2 - v1)
    w1 = 1.0 / (1.0 + e)
    w2 = e / (1.0 + e)
    meta = jnp.where(lane == 0, i1.astype(F32), 0.0) + jnp.where(lane == 1, i2.astype(F32), 0.0)
    comb_ref[...] = meta + jnp.where(lane == 2, w1, 0.0) + jnp.where(lane == 3, w2, 0.0)


def _router(x, mod, g2, w_router, b_router, *, rows_per_mod, mod_base):
    rows, d = x.shape
    tm = min(512, rows)
    wr = jnp.zeros((d, 128), F32).at[:, :N_EXPERTS].set(w_router)
    br = jnp.full((1, 128), NEG_BIG, F32).at[0, :N_EXPERTS].set(b_router)
    kern = functools.partial(_router_kernel, tiles_per_mod=max(rows_per_mod // tm, 1), mod_base=mod_base)
    return pl.pallas_call(
        kern,
        grid=(rows // tm,),
        in_specs=[
            pl.BlockSpec((tm, d), lambda i: (i, 0)),
            pl.BlockSpec(mod.shape, lambda i: (0, 0)),
            pl.BlockSpec((1, d), lambda i: (0, 0)),
            pl.BlockSpec((d, 128), lambda i: (0, 0)),
            pl.BlockSpec((1, 128), lambda i: (0, 0)),
        ],
        out_specs=[pl.BlockSpec((tm, d // 2), lambda i: (i, 0)), pl.BlockSpec((tm, 128), lambda i: (i, 0))],
        out_shape=[jax.ShapeDtypeStruct((rows, d // 2), jnp.int32), jax.ShapeDtypeStruct((rows, 128), F32)],
        compiler_params=_cparams("arbitrary"),
        name="moe_router",
    )(x, mod, g2.reshape(1, d), wr, br)


def _sc_gather(table, idx):
    n_idx = idx.shape[0]
    width = table.shape[1]
    per_worker = n_idx // SC_WORKERS
    chunk_rows = math.gcd(per_worker, SC_GATHER_ROWS)
    n_chunks = per_worker // chunk_rows
    assert per_worker * SC_WORKERS == n_idx and chunk_rows % 8 == 0
    mesh = plsc.VectorSubcoreMesh(core_axis_name="c", subcore_axis_name="s")

    @functools.partial(
        pl.kernel, mesh=mesh,
        out_type=jax.ShapeDtypeStruct((n_idx, width), table.dtype),
        scratch_types=[
            pltpu.VMEM((chunk_rows,), jnp.int32),
            pltpu.VMEM((chunk_rows, width), table.dtype),
            pltpu.SemaphoreType.DMA,
        ],
        name="sc_row_gather",
    )
    def gather(table_hbm, idx_hbm, out_hbm, idx_v, rows_v, sem):
        wid = lax.axis_index("s") * SC_CORES + lax.axis_index("c")
        base = wid * per_worker

        @pl.loop(0, n_chunks)
        def _(j):
            off = base + j * chunk_rows
            pltpu.sync_copy(idx_hbm.at[pl.ds(off, chunk_rows)], idx_v)
            pltpu.async_copy(table_hbm.at[idx_v], rows_v, sem).wait()
            pltpu.sync_copy(rows_v, out_hbm.at[pl.ds(off, chunk_rows)])

    return gather(table, idx)


def _moe_plan(meta, rows):
    tile = MOE_ROW_TILE
    n_tiles = (2 * rows) // tile + N_EXPERTS
    n_slots = n_tiles * tile
    experts = jnp.concatenate([meta[:, 0], meta[:, 1]]).astype(jnp.int32)
    onehot = (experts[:, None] == jnp.arange(N_EXPERTS)[None, :]).astype(jnp.int32)
    csum = jnp.cumsum(onehot, axis=0)
    counts = csum[-1]
    rank = jnp.sum(onehot * csum, axis=1) - 1
    padded = ((counts + tile - 1) // tile) * tile
    ends = jnp.cumsum(padded)
    starts = ends - padded
    pos = jnp.sum(onehot * starts[None, :], axis=1) + rank
    token = jnp.arange(2 * rows, dtype=jnp.int32) % rows
    src = jnp.zeros((n_slots,), jnp.int32).at[pos].set(token, unique_indices=True)
    tile_start = jnp.arange(n_tiles, dtype=jnp.int32) * tile
    used = tile_start < ends[-1]
    tile_e = jnp.minimum(jnp.sum((tile_start[:, None] >= ends[None, :]).astype(jnp.int32), axis=1), N_EXPERTS - 1)
    last_e = jnp.max(jnp.where(used, tile_e, 0))
    tile_e = jnp.where(used, tile_e, last_e)
    valid_end = jnp.sum((tile_e[:, None] == jnp.arange(N_EXPERTS)[None, :]) * (starts + counts)[None, :], axis=1)
    n_valid = jnp.where(used, jnp.clip(valid_end - tile_start, 0, tile), 0).astype(jnp.int32)
    return pos.astype(jnp.int32), src, tile_e.astype(jnp.int32), n_valid


def _moe_group_kernel(eid_ref, nval_ref, hs_ref, wg_ref, wu_ref, wd_ref, y_ref, acc_scr, *, n_f):
    w = pl.program_id(0)
    f = pl.program_id(1)
    nv = nval_ref[w]

    @pl.when(nv > 0)
    def _():
        h = _unpack_pairs(hs_ref[...]).astype(BF16)
        act = (_silu(_dot(h, wg_ref[...])) * _dot(h, wu_ref[...])).astype(BF16)
        part = _dot(act, wd_ref[...])

        @pl.when(f == 0)
        def _():
            acc_scr[...] = part

        @pl.when(f > 0)
        def _():
            acc_scr[...] += part

        @pl.when(f == n_f - 1)
        def _():
            y_ref[...] = _pack_pairs(acc_scr[...])


def _moe_grouped(hs, tile_e, n_valid, wg, wu, wd):
    n_slots = hs.shape[0]
    d = wg.shape[1]
    d_ff = wg.shape[2]
    tile = MOE_ROW_TILE
    n_f = 2
    tf = d_ff // n_f

    def f_idx(f, nval, w):
        return jnp.where(nval[w] > 0, f, n_f - 1)

    grid_spec = pltpu.PrefetchScalarGridSpec(
        num_scalar_prefetch=2,
        grid=(n_slots // tile, n_f),
        in_specs=[
            pl.BlockSpec((tile, d // 2), lambda w, f, eid, nval: (w, 0)),
            pl.BlockSpec((None, d, tf), lambda w, f, eid, nval: (eid[w], 0, f_idx(f, nval, w))),
            pl.BlockSpec((None, d, tf), lambda w, f, eid, nval: (eid[w], 0, f_idx(f, nval, w))),
            pl.BlockSpec((None, tf, d), lambda w, f, eid, nval: (eid[w], f_idx(f, nval, w), 0)),
        ],
        out_specs=pl.BlockSpec((tile, d // 2), lambda w, f, eid, nval: (w, 0)),
        scratch_shapes=[pltpu.VMEM((tile, d), F32)],
    )
    return pl.pallas_call(
        functools.partial(_moe_group_kernel, n_f=n_f),
        grid_spec=grid_spec,
        out_shape=jax.ShapeDtypeStruct((n_slots, d // 2), jnp.int32),
        compiler_params=_cparams("arbitrary", "arbitrary"),
        name="moe_experts",
    )(tile_e, n_valid, hs, wg, wu, wd)


def _moe_out_kernel(x_ref, y1_ref, y2_ref, meta_ref, mod_ref, g3_ref, o_ref, *, tiles_per_mod, mod_base):
    i = pl.program_id(0)
    _, _, gate_f = _mod_rows(mod_ref, i, tiles_per_mod, mod_base, 3)
    meta = meta_ref[...]
    y = meta[:, 2:3] * _unpack_pairs(y1_ref[...]) + meta[:, 3:4] * _unpack_pairs(y2_ref[...])
    o_ref[...] = x_ref[...] + gate_f * _rms(y, g3_ref[...])


def _moe_combine(x, yg, meta, mod, g3, *, rows_per_mod, mod_base):
    rows, d = x.shape
    tm = min(512, rows)
    nt = rows // tm
    kern = functools.partial(_moe_out_kernel, tiles_per_mod=max(rows_per_mod // tm, 1), mod_base=mod_base)
    return pl.pallas_call(
        kern,
        grid=(nt,),
        in_specs=[
            pl.BlockSpec((tm, d), lambda i: (i, 0)),
            pl.BlockSpec((tm, d // 2), lambda i: (i, 0)),
            pl.BlockSpec((tm, d // 2), lambda i: (nt + i, 0)),
            pl.BlockSpec((tm, 128), lambda i: (i, 0)),
            pl.BlockSpec(mod.shape, lambda i: (0, 0)),
            pl.BlockSpec((1, d), lambda i: (0, 0)),
        ],
        out_specs=pl.BlockSpec((tm, d), lambda i: (i, 0)),
        out_shape=jax.ShapeDtypeStruct((rows, d), F32),
        compiler_params=_cparams("arbitrary"),
        name="moe_combine",
    )(x, yg, yg, meta, mod, g3.reshape(1, d))


def _moe_sparse(x, h, meta, mod, g3, wg, wu, wd, *, rows_per_mod, mod_base):
    rows = x.shape[0]
    pos, src, tile_e, n_valid = _moe_plan(meta, rows)
    hs = _sc_gather(h, src)
    ys = _moe_grouped(hs, tile_e, n_valid, wg, wu, wd)
    yg = _sc_gather(ys, pos)
    return _moe_combine(x, yg, meta, mod, g3, rows_per_mod=rows_per_mod, mod_base=mod_base)


def _permute_w_in(w_in):
    nb = 9 * BRANCH_W
    return jnp.concatenate([w_in[:, nb:], w_in[:, :nb]], axis=1).astype(BF16)


def kernel(x, c, ctx, c_ctx, w_mod, b_mod, norm_g, w_in, s5_a_re, s5_a_im, s5_log_dt, s5_b_re, s5_b_im, s5_c_re, s5_c_im, s5_d, s5_w_glu, s5_b_glu, ret_decay, ret_gn, na_rpb, w_branch, w_out, ffn_w_gate, ffn_w_up, ffn_w_down, moe_w_router, moe_b_router, moe_w_gate, moe_w_up, moe_w_down):
    batch, seq_len, d = x.shape
    ctx_len = ctx.shape[1]
    depth = w_mod.shape[0]
    cond = jnp.concatenate([c, c_ctx[None, :]], axis=0)
    mod_all = _modulation(cond, w_mod, b_mod)
    rope = _rope_tables(seq_len)
    lane_h = np.repeat(np.arange(RET_HEADS), RET_DIM)
    avg = jnp.asarray((lane_h[:, None] == lane_h[None, :]).astype(np.float32) / RET_DIM, BF16)

    xl = x.reshape(batch * seq_len, d)
    xc = ctx.reshape(batch * ctx_len, d)
    lat = dict(rows_per_mod=seq_len, mod_base=0)
    cxt = dict(rows_per_mod=batch * ctx_len, mod_base=batch)

    for layer in range(depth):
        last = layer == depth - 1
        need_ctx = not last
        mod = mod_all[layer]
        ng = norm_g[layer]
        w_in_bf = _permute_w_in(w_in[layer])
        s5_tabs = _s5_tables(s5_a_re[layer], s5_a_im[layer], s5_log_dt[layer], s5_b_re[layer], s5_b_im[layer],
                             s5_c_re[layer], s5_c_im[layer], s5_d[layer], batch)
        ret_tabs = _ret_tables(ret_decay[layer])
        na_bias, na_hmask = _na_tables(na_rpb[layer])
        lw = dict(w_glu=s5_w_glu[layer].astype(BF16), b_glu=s5_b_glu[layer].reshape(1, BRANCH_W).astype(F32),
                  ret_gn=ret_gn[layer].reshape(1, BRANCH_W).astype(F32), avg=avg,
                  w_branch=w_branch[layer].astype(BF16), w_out=w_out[layer].astype(BF16))

        proj_l, f_l = _in_proj(xl, mod, ng[0], w_in_bf, **lat)
        proj_c, f_c = _in_proj(xc, mod, ng[0], w_in_bf, **cxt)

        a_l = _fourier_latent(f_l, batch, seq_len)
        s_l, s_c = _s5_mixer(proj_l[:, COL_S * BRANCH_W:(COL_S + 1) * BRANCH_W],
                             proj_c[:, COL_S * BRANCH_W:(COL_S + 1) * BRANCH_W], s5_tabs, batch)
        r_l, r_c = _retention(proj_l, proj_c, ret_tabs, rope, batch, seq_len, ctx_len)
        n_l, n_c = _neighborhood(proj_l, proj_c, na_bias, na_hmask, batch, seq_len, ctx_len, need_ctx)

        xl = _merge(xl, mod, ng[1], proj_l, a_l, s_l, r_l, n_l, lw, **lat)
        if need_ctx:
            a_c = _fourier_ctx(f_c, batch, ctx_len)
            xc = _merge(xc, mod, ng[1], proj_c, a_c, s_c, r_c, n_c, lw, **cxt)

        i = layer // 2
        if layer % 2 == 0:
            wg, wu, wd = ffn_w_gate[i].astype(BF16), ffn_w_up[i].astype(BF16), ffn_w_down[i].astype(BF16)
            xl = _ffn_dense(xl, mod, ng[2], ng[3], wg, wu, wd, **lat)
            if need_ctx:
                xc = _ffn_dense(xc, mod, ng[2], ng[3], wg, wu, wd, **cxt)
        else:
            wg, wu, wd = moe_w_gate[i].astype(BF16), moe_w_up[i].astype(BF16), moe_w_down[i].astype(BF16)
            h, meta = _router(xl, mod, ng[2], moe_w_router[i], moe_b_router[i], **lat)
            xl = _moe_sparse(xl, h, meta, mod, ng[3], wg, wu, wd, **lat)
            if need_ctx:
                hc, metac = _router(xc, mod, ng[2], moe_w_router[i], moe_b_router[i], **cxt)
                xc = _moe_sparse(xc, hc, metac, mod, ng[3], wg, wu, wd, **cxt)
    return xl.reshape(batch, seq_len, d)
```

```python
import functools
import math

import numpy as np
import jax
import jax.numpy as jnp
from jax import lax
from jax.experimental import pallas as pl
from jax.experimental.pallas import tpu as pltpu
from jax.experimental.pallas import tpu_sc as plsc

F32 = jnp.float32
BF16 = jnp.bfloat16

D_MODEL = 1024
BRANCH_W = 256
N_BRANCH = 4
GRID_W = 64
FNET_GROUP_DIM = 64
S5_GROUP_CH = 16
S5_GROUPS = 16
S5_STATE = 64
S5_CHUNK = 32
S5_PAIRS = S5_GROUPS // 2
RET_HEADS = 4
RET_DIM = 64
RET_CHUNK = 128
NA_HEADS = 4
NA_DIM = 64
NA_WIN_ROWS = 8
NA_WIN_COLS = 16
NA_QROWS = 8
ROPE_BASE = 10000.0
N_EXPERTS = 8
EPS = 1e-6
FFT_N2 = 256
NEG_BIG = -1e30
VMEM_LIMIT_BYTES = 50 * 1024 * 1024
SC_CORES = 2
SC_SUBCORES = 16
SC_WORKERS = SC_CORES * SC_SUBCORES
SC_GATHER_ROWS = 64
MOE_ROW_TILE = 512

COL_F, COL_S, COL_RQ, COL_RK, COL_RV, COL_RG, COL_NQ, COL_NK, COL_NV = range(16, 25)
IN_W = 9 * BRANCH_W + N_BRANCH * D_MODEL
IN_TN = 1280
IN_F_TILE = (N_BRANCH * D_MODEL) // IN_TN
IN_F_OFF = N_BRANCH * D_MODEL - IN_F_TILE * IN_TN


def _cparams(*sem):
    return pltpu.CompilerParams(dimension_semantics=sem, vmem_limit_bytes=VMEM_LIMIT_BYTES)


def _sigmoid(v):
    return 0.5 * jnp.tanh(0.5 * v) + 0.5


def _silu(v):
    return v * _sigmoid(v)


def _gelu_tanh(v):
    return 0.5 * v * (1.0 + jnp.tanh(math.sqrt(2.0 / math.pi) * (v + 0.044715 * (v * v * v))))


def _rms(v, g):
    ms = jnp.mean(v * v, axis=-1, keepdims=True)
    return v * lax.rsqrt(ms + EPS) * g


def _split_bf16(v):
    hi = v.astype(BF16)
    lo = (v - hi.astype(F32)).astype(BF16)
    return hi, lo


def _pack_pairs(v):
    n = v.shape[1] // 2
    lo = lax.bitcast_convert_type(v[:, :n].astype(BF16).astype(F32), jnp.int32)
    hi = lax.bitcast_convert_type(v[:, n:].astype(BF16).astype(F32), jnp.int32)
    return (hi & -65536) | ((lo >> 16) & 65535)


def _unpack_pairs(w):
    lo = lax.bitcast_convert_type(w << 16, F32)
    hi = lax.bitcast_convert_type(w & -65536, F32)
    return jnp.concatenate([lo, hi], axis=-1)


def _dot(a, b):
    return jnp.dot(a, b, preferred_element_type=F32)


def _dot_nt(a, b):
    return lax.dot_general(a, b, (((1,), (1,)), ((), ())), preferred_element_type=F32)


def _dot_tn(a, b):
    return lax.dot_general(a, b, (((0,), (0,)), ((), ())), preferred_element_type=F32)


def _mod_kernel(ct_ref, w_ref, b_ref, o_ref):
    ct = ct_ref[...]
    s = _silu(ct)
    w = w_ref[...]
    rows = [jnp.sum(w * s[:, r:r + 1], axis=0, keepdims=True) for r in range(8)]
    o_ref[...] = jnp.concatenate(rows, axis=0) + b_ref[...]


def _modulation(cond, w_mod, b_mod):
    n_layers, d, n = w_mod.shape
    tn = 512
    ct = jnp.zeros((8, d), F32).at[:cond.shape[0]].set(cond).T
    return pl.pallas_call(
        _mod_kernel,
        grid=(n_layers, n // tn),
        in_specs=[
            pl.BlockSpec((d, 8), lambda l, j: (0, 0)),
            pl.BlockSpec((None, d, tn), lambda l, j: (l, 0, j)),
            pl.BlockSpec((None, 1, tn), lambda l, j: (l, 0, j)),
        ],
        out_specs=pl.BlockSpec((None, 8, tn), lambda l, j: (l, 0, j)),
        out_shape=jax.ShapeDtypeStruct((n_layers, 8, n), F32),
        compiler_params=_cparams("arbitrary", "arbitrary"),
        name="adaln_mod",
    )(ct, w_mod, b_mod.reshape(n_layers, 1, n))


def _mod_rows(mod_ref, i, tiles_per_mod, mod_base, first):
    r = mod_base + i // tiles_per_mod
    return [mod_ref[pl.ds(r, 1), (first + k) * D_MODEL:(first + k + 1) * D_MODEL] for k in range(3)]


def _in_kernel(x_ref, mod_ref, g_ref, w_ref, proj_ref, f_ref, h_scr, *, tiles_per_mod, mod_base):
    i = pl.program_id(0)
    j = pl.program_id(1)

    @pl.when(j == 0)
    def _():
        sh, sc, _ = _mod_rows(mod_ref, i, tiles_per_mod, mod_base, 0)
        h_scr[...] = (_rms(x_ref[...], g_ref[...]) * (1.0 + sc) + sh).astype(BF16)

    res = _dot(h_scr[...], w_ref[...])
    proj_ref[...] = res.astype(BF16)

    @pl.when(j == IN_F_TILE)
    def _():
        f_ref[...] = res[:, IN_F_OFF:IN_F_OFF + BRANCH_W].astype(BF16)


def _in_proj(x, mod, g, w_bf, *, rows_per_mod, mod_base):
    rows, d = x.shape
    tm = math.gcd(1024, rows_per_mod)
    kern = functools.partial(_in_kernel, tiles_per_mod=max(rows_per_mod // tm, 1), mod_base=mod_base)
    return pl.pallas_call(
        kern,
        grid=(rows // tm, IN_W // IN_TN),
        in_specs=[
            pl.BlockSpec((tm, d), lambda i, j: (i, 0)),
            pl.BlockSpec(mod.shape, lambda i, j: (0, 0)),
            pl.BlockSpec((1, d), lambda i, j: (0, 0)),
            pl.BlockSpec((d, IN_TN), lambda i, j: (0, j)),
        ],
        out_specs=[
            pl.BlockSpec((tm, IN_TN), lambda i, j: (i, j)),
            pl.BlockSpec((tm, BRANCH_W), lambda i, j: (i, 0)),
        ],
        out_shape=[
            jax.ShapeDtypeStruct((rows, IN_W), BF16),
            jax.ShapeDtypeStruct((rows, BRANCH_W), BF16),
        ],
        scratch_shapes=[pltpu.VMEM((tm, d), BF16)],
        compiler_params=_cparams("arbitrary", "arbitrary"),
        name="in_proj",
    )(x, mod, g.reshape(1, d), w_bf)


def _fft_a_kernel(x_ref, cs_ref, tc_ref, ts_ref, zr_ref, zi_ref, *, n1, n1p):
    y = _dot(cs_ref[...].astype(BF16), x_ref[...])
    yr = y[:n1]
    yi = y[n1p:n1p + n1]
    tc = tc_ref[...]
    ts = ts_ref[...]
    zr_ref[...] = (yr * tc + yi * ts).astype(BF16)
    zi_ref[...] = (yi * tc - yr * ts).astype(BF16)


def _fft_b_kernel(zr_ref, zi_ref, cs_ref, cc_ref, sc_ref, o_ref, *, kb, scale, has_imag):
    cs = cs_ref[...].astype(BF16)
    cc = cc_ref[...].astype(BF16)
    sc = sc_ref[...].astype(BF16)
    for kk in range(kb):
        a = _dot(cs, zr_ref[kk])
        if has_imag:
            b = _dot(cs, zi_ref[kk])
            xr = a[:FFT_N2] + b[FFT_N2:]
            xi = b[:FFT_N2] - a[FFT_N2:]
        else:
            xr = a[:FFT_N2]
            xi = -a[FFT_N2:]
        out = _dot(xr.astype(BF16), cc) + _dot(xi.astype(BF16), sc)
        o_ref[:, kk * BRANCH_W:(kk + 1) * BRANCH_W] = (out * scale).astype(BF16)


def _dft_tables(n):
    k = np.arange(n)
    ang = 2.0 * np.pi * ((k[:, None] * k[None, :]) % n) / n
    return np.cos(ang), np.sin(ang)


def _fft_b_call(zr, zi, n1, batch, seq_len, has_imag):
    c2, s2 = _dft_tables(FFT_N2)
    cs2 = jnp.asarray(np.concatenate([c2, s2], axis=0), F32)
    c64, s64 = _dft_tables(FNET_GROUP_DIM)
    eye = np.eye(BRANCH_W // FNET_GROUP_DIM)
    cc = jnp.asarray(np.kron(eye, c64), F32)
    sc = jnp.asarray(np.kron(eye, s64), F32)
    kb = min(8, n1)
    scale = 1.0 / math.sqrt(seq_len * FNET_GROUP_DIM)
    kern = functools.partial(_fft_b_kernel, kb=kb, scale=scale, has_imag=has_imag)
    zspec = pl.BlockSpec((None, kb, FFT_N2, BRANCH_W), lambda b, i: (b, i, 0, 0))
    out = pl.pallas_call(
        kern,
        grid=(batch, n1 // kb),
        in_specs=[
            zspec, zspec,
            pl.BlockSpec((2 * FFT_N2, FFT_N2), lambda b, i: (0, 0)),
            pl.BlockSpec((BRANCH_W, BRANCH_W), lambda b, i: (0, 0)),
            pl.BlockSpec((BRANCH_W, BRANCH_W), lambda b, i: (0, 0)),
        ],
        out_specs=pl.BlockSpec((None, FFT_N2, kb * BRANCH_W), lambda b, i: (b, 0, i)),
        out_shape=jax.ShapeDtypeStruct((batch, FFT_N2, n1 * BRANCH_W), BF16),
        compiler_params=_cparams("arbitrary", "arbitrary"),
        name="fourier_stage_b",
    )(zr, zi, cs2, cc, sc)
    return out.reshape(batch * seq_len, BRANCH_W)


def _fourier_latent(f, batch, seq_len):
    n1 = seq_len // FFT_N2
    wide = FFT_N2 * BRANCH_W
    c1, s1 = _dft_tables(n1)
    n1p = max(n1, 8)
    cs1 = np.zeros((2 * n1p, n1))
    cs1[:n1] = c1
    cs1[n1p:n1p + n1] = -s1
    k1 = np.arange(n1)[:, None]
    l2 = np.arange(FFT_N2)[None, :]
    tw = 2.0 * np.pi * (k1 * l2) / seq_len
    tc = jnp.asarray(np.repeat(np.cos(tw), BRANCH_W, axis=1), F32)
    ts = jnp.asarray(np.repeat(np.sin(tw), BRANCH_W, axis=1), F32)
    cw = min(8192, wide)
    xv = f.reshape(batch, n1, wide)
    spec = pl.BlockSpec((None, n1, cw), lambda b, j: (b, 0, j))
    tspec = pl.BlockSpec((n1, cw), lambda b, j: (0, j))
    zr, zi = pl.pallas_call(
        functools.partial(_fft_a_kernel, n1=n1, n1p=n1p),
        grid=(batch, wide // cw),
        in_specs=[spec, pl.BlockSpec((2 * n1p, n1), lambda b, j: (0, 0)), tspec, tspec],
        out_specs=[spec, spec],
        out_shape=[jax.ShapeDtypeStruct((batch, n1, wide), BF16)] * 2,
        compiler_params=_cparams("arbitrary", "arbitrary"),
        name="fourier_stage_a",
    )(xv, jnp.asarray(cs1, F32), tc, ts)
    zr = zr.reshape(batch, n1, FFT_N2, BRANCH_W)
    zi = zi.reshape(batch, n1, FFT_N2, BRANCH_W)
    return _fft_b_call(zr, zi, n1, batch, seq_len, True)


def _fourier_ctx(f, batch, ctx_len):
    assert ctx_len == FFT_N2
    z = f.reshape(batch, 1, FFT_N2, BRANCH_W)
    return _fft_b_call(z, z, 1, batch, ctx_len, False)


def _s5_tables(a_re, a_im, log_dt, b_re, b_im, c_re, c_im, d_skip, batch):
    t = S5_CHUNK
    g, p, hc = S5_GROUPS, S5_STATE, S5_GROUP_CH
    lam = lax.complex(a_re.astype(F32), a_im.astype(F32))
    dt = jnp.exp(log_dt.astype(F32))[..., None]
    ks = jnp.arange(t + 1, dtype=F32)
    apow = jnp.exp((lam * dt)[..., None] * ks)
    a_bar = apow[..., 1]
    b_bar = ((a_bar - 1.0) / lam)[..., None] * lax.complex(b_re.astype(F32), b_im.astype(F32))
    cm = lax.complex(c_re.astype(F32), c_im.astype(F32))
    kimp = jnp.real(jnp.einsum('dghp,dgpk,dgpj->dgkhj', cm, apow[..., :t], b_bar,
                               precision=lax.Precision.HIGHEST))
    kf, kb = kimp[0], kimp[1]
    kfull = jnp.concatenate([kb[:, :0:-1], kf[:, :1] + kb[:, :1], kf[:, 1:]], axis=1)
    strip = kfull.transpose(0, 3, 1, 2).reshape(g, hc, (2 * t - 1) * hc)
    strip = jnp.pad(strip, ((0, 0), (0, 0), (0, 2 * t * hc - strip.shape[-1])))
    strip = strip.reshape(S5_PAIRS, 2, hc, 2 * t * hc)

    def pair_blocks(kd, axis):
        r, c = kd.shape[1:]
        kp = kd.reshape(S5_PAIRS, 2, r, c)
        z = jnp.zeros_like(kp[:, 0])
        top = jnp.concatenate([kp[:, 0], z], axis=-1)
        bot = jnp.concatenate([z, kp[:, 1]], axis=-1)
        return jnp.concatenate([top, bot], axis=1)

    wf = jnp.einsum('gpj,gph->gjhp', apow[0][..., t - 1::-1][..., :t], b_bar[0])
    wb = jnp.einsum('gpj,gph->gjhp', apow[1][..., :t], b_bar[1])
    wf = wf.reshape(g, t * hc, p)
    wb = wb.reshape(g, t * hc, p)
    kinds = [jnp.real(wf), jnp.imag(wf), jnp.real(wb), jnp.imag(wb)]
    we = jnp.concatenate([pair_blocks(kd, 0) for kd in kinds], axis=-1).astype(BF16)

    vf = jnp.einsum('ghp,gpt->gpth', cm[0], apow[0][..., 1:t + 1])
    vb = jnp.einsum('ghp,gpt->gpth', cm[1], apow[1][..., t:0:-1])
    vf = vf.reshape(g, p, t * hc)
    vb = vb.reshape(g, p, t * hc)
    vkinds = [jnp.real(vf), -jnp.imag(vf), jnp.real(vb), -jnp.imag(vb)]
    v1 = jnp.concatenate([pair_blocks(kd, 0) for kd in vkinds], axis=1)
    v = jnp.concatenate([v1, v1], axis=1).astype(BF16)

    def lanes(z):
        return jnp.tile(z.reshape(1, g * p), (1, batch))

    at = apow[..., t]
    a_tab = jnp.concatenate([lanes(jnp.real(at[0])), lanes(jnp.imag(at[0])),
                             lanes(jnp.real(at[1])), lanes(jnp.imag(at[1]))], axis=0)
    dvec = jnp.tile(d_skip.astype(F32).reshape(S5_PAIRS, 2, 1, hc), (1, 1, t, 1)).reshape(S5_PAIRS, 1, 2 * t * hc)
    return dict(strip=strip, we=we, v=v, a_tab=a_tab, dvec=dvec)


def _s5_e_kernel(u_ref, we_ref, ref_, imf_, reb_, imb_):
    e = _dot(u_ref[...], we_ref[...])
    ref_[...] = e[:, 0:128]
    imf_[...] = e[:, 128:256]
    reb_[...] = e[:, 256:384]
    imb_[...] = e[:, 384:512]


def _s5_scan_kernel(a_ref, ref_, imf_, reb_, imb_, prf, pif, prb, pib, *, n_rows, n_ctx):
    afr = a_ref[0:1, :]
    afi = a_ref[1:2, :]
    abr = a_ref[2:3, :]
    abi = a_ref[3:4, :]
    zero = jnp.zeros_like(afr)

    def body(s, carry):
        sfr, sfi, sbr, sbi = carry
        nf = s
        nb = jnp.where(s < n_ctx, n_ctx - 1 - s, n_rows - 1 + n_ctx - s)
        prf[pl.ds(nf, 1), :] = sfr
        pif[pl.ds(nf, 1), :] = sfi
        prb[pl.ds(nb, 1), :] = sbr
        pib[pl.ds(nb, 1), :] = sbi
        efr = ref_[pl.ds(nf, 1), :]
        efi = imf_[pl.ds(nf, 1), :]
        ebr = reb_[pl.ds(nb, 1), :]
        ebi = imb_[pl.ds(nb, 1), :]
        nfr = afr * sfr - afi * sfi + efr
        nfi = afr * sfi + afi * sfr + efi
        nbr = abr * sbr - abi * sbi + ebr
        nbi = abr * sbi + abi * sbr + ebi
        return nfr, nfi, nbr, nbi

    lax.fori_loop(0, n_rows, body, (zero, zero, zero, zero))


def _s5_y_kernel(u_ref, strip_ref, v_ref, d_ref, prf, pif, prb, pib, y_ref, m_scr):
    half = S5_CHUNK * S5_GROUP_CH

    @pl.when(pl.program_id(1) == 0)
    def _():
        for gi in range(2):
            strip = strip_ref[gi]
            for j in range(S5_CHUNK):
                off = (S5_CHUNK - 1 - j) * S5_GROUP_CH
                win = strip if off == 0 else pltpu.roll(strip, 2 * half - off, axis=1)
                m_scr[gi, j * S5_GROUP_CH:(j + 1) * S5_GROUP_CH, :] = win[:, :half].astype(BF16)

    u = u_ref[...]
    y_intra = jnp.concatenate([_dot(u[:, :half], m_scr[0]), _dot(u[:, half:], m_scr[1])], axis=-1)
    pcat = jnp.concatenate([prf[...], pif[...], prb[...], pib[...]], axis=-1)
    hi, lo = _split_bf16(pcat)
    y_cross = _dot(jnp.concatenate([hi, lo], axis=-1), v_ref[...])
    y_ref[...] = y_intra + y_cross + d_ref[...] * u.astype(F32)


def _s5_core(u, tabs, batch, n_rows, n_ctx):
    width = batch * S5_PAIRS * 128
    cols = 2 * S5_CHUNK * S5_GROUP_CH
    u_spec = pl.BlockSpec((None, None, n_rows, cols), lambda q, b: (q, b, 0, 0))
    st_spec = pl.BlockSpec((n_rows, 128), lambda q, b: (0, b * S5_PAIRS + q))
    st_shape = jax.ShapeDtypeStruct((n_rows, width), F32)
    e4 = pl.pallas_call(
        _s5_e_kernel,
        grid=(S5_PAIRS, batch),
        in_specs=[u_spec, pl.BlockSpec((None, cols, 512), lambda q, b: (q, 0, 0))],
        out_specs=[st_spec] * 4,
        out_shape=[st_shape] * 4,
        compiler_params=_cparams("arbitrary", "arbitrary"),
        name="s5_chunk_states",
    )(u, tabs['we'])
    p4 = pl.pallas_call(
        functools.partial(_s5_scan_kernel, n_rows=n_rows, n_ctx=n_ctx),
        out_shape=[st_shape] * 4,
        compiler_params=pltpu.CompilerParams(vmem_limit_bytes=VMEM_LIMIT_BYTES),
        name="s5_state_scan",
    )(tabs['a_tab'], *e4)
    y = pl.pallas_call(
        _s5_y_kernel,
        grid=(S5_PAIRS, batch),
        in_specs=[
            u_spec,
            pl.BlockSpec((None, 2, S5_GROUP_CH, cols), lambda q, b: (q, 0, 0, 0)),
            pl.BlockSpec((None, cols, cols), lambda q, b: (q, 0, 0)),
            pl.BlockSpec((None, 1, cols), lambda q, b: (q, 0, 0)),
            st_spec, st_spec, st_spec, st_spec,
        ],
        out_specs=pl.BlockSpec((None, None, n_rows, cols), lambda q, b: (q, b, 0, 0)),
        out_shape=jax.ShapeDtypeStruct((S5_PAIRS, batch, n_rows, cols), F32),
        scratch_shapes=[pltpu.VMEM((2, cols // 2, cols // 2), BF16)],
        compiler_params=_cparams("arbitrary", "arbitrary"),
        name="s5_outputs",
    )(u, tabs['strip'], tabs['v'], tabs['dvec'], *p4)
    return y


def _s5_to_chunks(s, batch):
    n = s.shape[0] // batch // S5_CHUNK
    v = s.reshape(batch, n, S5_CHUNK, S5_PAIRS, 2, S5_GROUP_CH)
    return v.transpose(3, 0, 1, 4, 2, 5).reshape(S5_PAIRS, batch, n, 2 * S5_CHUNK * S5_GROUP_CH)


def _s5_from_chunks(y, batch):
    n = y.shape[2]
    v = y.reshape(S5_PAIRS, batch, n, 2, S5_CHUNK, S5_GROUP_CH)
    return v.transpose(1, 2, 4, 0, 3, 5).reshape(batch * n * S5_CHUNK, BRANCH_W)


def _s5_mixer(s_lat, s_ctx, tabs, batch):
    ul = _s5_to_chunks(s_lat, batch)
    uc = _s5_to_chunks(s_ctx, batch)
    n_ctx = uc.shape[2]
    u = jnp.concatenate([uc, ul], axis=2)
    y = _s5_core(u, tabs, batch, u.shape[2], n_ctx)
    return _s5_from_chunks(y[:, :, n_ctx:], batch), _s5_from_chunks(y[:, :, :n_ctx], batch)


def _ret_tables(ret_decay):
    c = RET_CHUNK
    lg = jax.nn.log_sigmoid(ret_decay.astype(F32))
    lane_h = jnp.repeat(jnp.arange(RET_HEADS), RET_DIM)
    lgl = lg[:, lane_h]
    pos = jnp.arange(c, dtype=F32)[:, None]
    qd = jnp.stack([jnp.exp((pos + 1.0) * lgl[0][None]), jnp.exp((c - pos) * lgl[1][None])])
    kd = jnp.stack([jnp.exp((c - 1.0 - pos) * lgl[0][None]), jnp.exp(pos * lgl[1][None])])
    bmask = (lane_h[:, None] == lane_h[None, :]).astype(F32)
    cd = jnp.exp(c * lgl)[:, :, None] * bmask[None]
    diff = pos - pos.T
    dm = []
    for h in range(RET_HEADS):
        fw = jnp.where(diff >= 0, jnp.exp(jnp.maximum(diff, 0.0) * lg[0, h]), 0.0)
        bw = jnp.where(diff <= 0, jnp.exp(jnp.maximum(-diff, 0.0) * lg[1, h]), 0.0)
        dm.append(fw + bw)
    dm = jnp.concatenate(dm, axis=0)
    hmask = (jnp.arange(RET_HEADS)[:, None] == lane_h[None, :]).astype(F32)
    return dict(qd=qd, kd=kd, cd=cd, bmask=bmask, dm=dm, hmask=hmask)


def _rope_tables(n_tokens):
    t = np.arange(n_tokens)
    row = (t // GRID_W).astype(np.float64)
    col = (t % GRID_W).astype(np.float64)
    n_freq = RET_DIM // 4
    inv_freq = 1.0 / (ROPE_BASE ** (np.arange(n_freq, dtype=np.float64) / n_freq))
    ang = np.concatenate([row[:, None] * inv_freq, col[:, None] * inv_freq], axis=-1)
    cos = np.cos(ang)
    sin = np.sin(ang)
    cos_t = np.tile(np.concatenate([cos, cos], axis=-1), (1, RET_HEADS))
    sin_t = np.tile(np.concatenate([-sin, sin], axis=-1), (1, RET_HEADS))
    half = RET_DIM // 2
    perm = np.arange(BRANCH_W) ^ half
    swap = np.zeros((BRANCH_W, BRANCH_W), np.float32)
    swap[perm, np.arange(BRANCH_W)] = 1.0
    return jnp.asarray(cos_t, F32), jnp.asarray(sin_t, F32), jnp.asarray(swap, BF16)


def _ret_chunk(q, k, v, s, qd, kd, cd, bmask, dm, hmask, with_intra):
    cross = _dot((q * qd).astype(BF16), s.astype(BF16))
    s_new = cd * s + bmask * _dot_tn((k * kd).astype(BF16), v)
    if not with_intra:
        return cross, s_new
    qb = q.astype(BF16)
    kb = k.astype(BF16)
    qs = jnp.concatenate([qb * hmask[h:h + 1].astype(BF16) for h in range(RET_HEADS)], axis=0)
    scores = _dot_nt(qs, kb) * dm
    ov = _dot(scores.astype(BF16), v)
    c = q.shape[0]
    inner = ov[0:c] * hmask[0:1]
    for h in range(1, RET_HEADS):
        inner = inner + ov[h * c:(h + 1) * c] * hmask[h:h + 1]
    return inner + cross, s_new


def _ret_kernel(q_ref, k_ref, v_ref, qc_ref, kc_ref, vc_ref, cos_ref, sin_ref, swap_ref,
                qd_ref, kd_ref, cd_ref, bm_ref, dm_ref, hm_ref, o_ref, oc_ref, s_scr, *, n_chunks, n_ctx_chunks):
    dirn = pl.program_id(1)
    i = pl.program_id(2)
    c = RET_CHUNK
    k_scale = RET_DIM ** -0.5
    bmask = bm_ref[...]
    dm = dm_ref[...]
    hmask = hm_ref[...]

    def run(d):
        qd = qd_ref[d]
        kd = kd_ref[d]
        cd = cd_ref[d]
        intra = d == 0

        @pl.when(i == 0)
        def _():
            s = jnp.zeros((BRANCH_W, BRANCH_W), F32)
            order = range(n_ctx_chunks) if d == 0 else range(n_ctx_chunks - 1, -1, -1)
            for cc in order:
                sl = slice(cc * c, (cc + 1) * c)
                o, s = _ret_chunk(qc_ref[sl, :].astype(F32), kc_ref[sl, :].astype(F32) * k_scale, vc_ref[sl, :],
                                  s, qd, kd, cd, bmask, dm, hmask, intra)
                oc_ref[sl, :] = o
            s_scr[...] = s

        swap = swap_ref[...]
        order = range(n_chunks) if d == 0 else range(n_chunks - 1, -1, -1)
        s = s_scr[...]
        for cc in order:
            sl = slice(cc * c, (cc + 1) * c)
            cos = cos_ref[sl, :]
            sin = sin_ref[sl, :]
            qb = q_ref[sl, :]
            kb = k_ref[sl, :]
            q = qb.astype(F32) * cos + _dot(qb, swap) * sin
            k = (kb.astype(F32) * cos + _dot(kb, swap) * sin) * k_scale
            o, s = _ret_chunk(q, k, v_ref[sl, :], s, qd, kd, cd, bmask, dm, hmask, intra)
            o_ref[sl, :] = o
        s_scr[...] = s

    @pl.when(dirn == 0)
    def _():
        run(0)

    @pl.when(dirn == 1)
    def _():
        run(1)


def _retention(proj_l, proj_c, tabs, rope, batch, seq_len, ctx_len):
    n_chunks = 4
    blk = n_chunks * RET_CHUNK
    nblk = seq_len // blk
    cos_t, sin_t, swap = rope

    def pos(d, i):
        return i + d * (nblk - 1 - 2 * i)

    def lat(col):
        return pl.BlockSpec((blk, BRANCH_W), lambda b, d, i: (b * nblk + pos(d, i), col))

    def ctx(col):
        return pl.BlockSpec((ctx_len, BRANCH_W), lambda b, d, i: (b, col))

    def const(shape):
        return pl.BlockSpec(shape, lambda b, d, i: (0,) * len(shape))

    tab_spec = pl.BlockSpec((blk, BRANCH_W), lambda b, d, i: (pos(d, i), 0))
    kern = functools.partial(_ret_kernel, n_chunks=n_chunks, n_ctx_chunks=ctx_len // RET_CHUNK)
    c = RET_CHUNK
    o, oc = pl.pallas_call(
        kern,
        grid=(batch, 2, nblk),
        in_specs=[
            lat(COL_RQ), lat(COL_RK), lat(COL_RV), ctx(COL_RQ), ctx(COL_RK), ctx(COL_RV),
            tab_spec, tab_spec, const((BRANCH_W, BRANCH_W)),
            const((2, c, BRANCH_W)), const((2, c, BRANCH_W)), const((2, BRANCH_W, BRANCH_W)),
            const((BRANCH_W, BRANCH_W)), const((RET_HEADS * c, c)), const((RET_HEADS, BRANCH_W)),
        ],
        out_specs=[
            pl.BlockSpec((None, blk, BRANCH_W), lambda b, d, i: (d, b * nblk + pos(d, i), 0)),
            pl.BlockSpec((None, ctx_len, BRANCH_W), lambda b, d, i: (d, b, 0)),
        ],
        out_shape=[
            jax.ShapeDtypeStruct((2, batch * seq_len, BRANCH_W), F32),
            jax.ShapeDtypeStruct((2, batch * ctx_len, BRANCH_W), F32),
        ],
        scratch_shapes=[pltpu.VMEM((BRANCH_W, BRANCH_W), F32)],
        compiler_params=_cparams("arbitrary", "arbitrary", "arbitrary"),
        name="retention",
    )(proj_l, proj_l, proj_l, proj_c, proj_c, proj_c, cos_t, sin_t, swap,
      tabs['qd'], tabs['kd'], tabs['cd'], tabs['bmask'], tabs['dm'], tabs['hmask'])
    return o, oc


def _na_tables(rpb):
    kr, kw = NA_WIN_ROWS, NA_WIN_COLS
    col = np.arange(GRID_W)
    col_start = np.clip(col - kw // 2, 0, GRID_W - kw)
    in_win = (col[None, :] >= col_start[:, None]) & (col[None, :] < col_start[:, None] + kw)
    dc = np.clip(col[None, :] - col[:, None], -(kw - 1), kw - 1) + (kw - 1)
    var = np.arange(kr)[:, None] + np.arange(kr)[None, :]
    pick_r = (var[:, :, None] == np.arange(2 * kr - 1)[None, None, :]).astype(np.float32)
    pick_c = (dc[:, :, None] == np.arange(2 * kw - 1)[None, None, :]).astype(np.float32)
    bias = jnp.einsum('vir,hrc,qkc->vhqik', jnp.asarray(pick_r), rpb.astype(F32), jnp.asarray(pick_c),
                      precision=lax.Precision.HIGHEST)
    bias = jnp.where(jnp.asarray(in_win)[None, None, :, None, :], bias, NEG_BIG)
    bias = bias.reshape(kr, NA_HEADS * GRID_W, kr * GRID_W)
    lane_h = np.repeat(np.arange(NA_HEADS), NA_DIM)
    hmask = (np.arange(NA_HEADS)[:, None] == lane_h[None, :]).astype(np.float32)
    return bias, jnp.asarray(hmask, F32)


def _attend(qs, keys, vals, bias, kc, vc):
    s_ctx = _dot_nt(qs, kc)
    m = jnp.max(s_ctx, axis=-1, keepdims=True)
    if keys is not None:
        s_band = _dot_nt(qs, keys) + bias
        m = jnp.maximum(m, jnp.max(s_band, axis=-1, keepdims=True))
        p_band = jnp.exp(s_band - m)
    p_ctx = jnp.exp(s_ctx - m)
    l = jnp.sum(p_ctx, axis=-1, keepdims=True)
    o = _dot(p_ctx.astype(BF16), vc)
    if keys is not None:
        l = l + jnp.sum(p_band, axis=-1, keepdims=True)
        o = o + _dot(p_band.astype(BF16), vals)
    return o / l


def _stack_heads(q, hmask_scaled):
    return jnp.concatenate([q * hmask_scaled[h:h + 1] for h in range(NA_HEADS)], axis=0)


def _unstack_heads(o, hmask, n):
    out = o[0:n] * hmask[0:1]
    for h in range(1, NA_HEADS):
        out = out + o[h * n:(h + 1) * n] * hmask[h:h + 1]
    return out


def _na_kernel(q_ref, k_ref, v_ref, kc_ref, vc_ref, bias_ref, hm_ref, o_ref, *, n_grid_rows):
    i = pl.program_id(1)
    hmask = hm_ref[...]
    hms = (hmask * (NA_DIM ** -0.5)).astype(BF16)
    kc = kc_ref[...]
    vc = vc_ref[...]
    band = NA_WIN_ROWS * GRID_W
    for rr in range(NA_QROWS):
        r = i * NA_QROWS + rr
        rs = jnp.clip(r - NA_WIN_ROWS // 2, 0, n_grid_rows - NA_WIN_ROWS)
        var = rs - r + (NA_WIN_ROWS - 1)
        start = pl.multiple_of(rs * GRID_W, GRID_W)
        keys = k_ref[pl.ds(start, band), :]
        vals = v_ref[pl.ds(start, band), :]
        qs = _stack_heads(q_ref[rr * GRID_W:(rr + 1) * GRID_W, :], hms)
        o = _attend(qs, keys, vals, bias_ref[var], kc, vc)
        o_ref[rr * GRID_W:(rr + 1) * GRID_W, :] = _unstack_heads(o, hmask, GRID_W).astype(BF16)


def _na_ctx_kernel(q_ref, kc_ref, vc_ref, hm_ref, o_ref):
    hmask = hm_ref[...]
    hms = (hmask * (NA_DIM ** -0.5)).astype(BF16)
    n = q_ref.shape[0]
    o = _attend(_stack_heads(q_ref[...], hms), None, None, None, kc_ref[...], vc_ref[...])
    o_ref[...] = _unstack_heads(o, hmask, n).astype(BF16)


def _neighborhood(proj_l, proj_c, bias, hmask, batch, seq_len, ctx_len, need_ctx_out):
    rows = seq_len // GRID_W
    qblk = NA_QROWS * GRID_W
    nq = seq_len // qblk
    out_l = pl.pallas_call(
        functools.partial(_na_kernel, n_grid_rows=rows),
        grid=(batch, nq),
        in_specs=[
            pl.BlockSpec((qblk, BRANCH_W), lambda b, i: (b * nq + i, COL_NQ)),
            pl.BlockSpec((seq_len, BRANCH_W), lambda b, i: (b, COL_NK)),
            pl.BlockSpec((seq_len, BRANCH_W), lambda b, i: (b, COL_NV)),
            pl.BlockSpec((ctx_len, BRANCH_W), lambda b, i: (b, COL_NK)),
            pl.BlockSpec((ctx_len, BRANCH_W), lambda b, i: (b, COL_NV)),
            pl.BlockSpec(bias.shape, lambda b, i: (0, 0, 0)),
            pl.BlockSpec(hmask.shape, lambda b, i: (0, 0)),
        ],
        out_specs=pl.BlockSpec((qblk, BRANCH_W), lambda b, i: (b * nq + i, 0)),
        out_shape=jax.ShapeDtypeStruct((batch * seq_len, BRANCH_W), BF16),
        compiler_params=_cparams("arbitrary", "arbitrary"),
        name="neighborhood_attn",
    )(proj_l, proj_l, proj_l, proj_c, proj_c, bias, hmask)
    out_c = None
    if need_ctx_out:
        out_c = pl.pallas_call(
            _na_ctx_kernel,
            grid=(batch,),
            in_specs=[
                pl.BlockSpec((ctx_len, BRANCH_W), lambda b: (b, COL_NQ)),
                pl.BlockSpec((ctx_len, BRANCH_W), lambda b: (b, COL_NK)),
                pl.BlockSpec((ctx_len, BRANCH_W), lambda b: (b, COL_NV)),
                pl.BlockSpec(hmask.shape, lambda b: (0, 0)),
            ],
            out_specs=pl.BlockSpec((ctx_len, BRANCH_W), lambda b: (b, 0)),
            out_shape=jax.ShapeDtypeStruct((batch * ctx_len, BRANCH_W), BF16),
            compiler_params=_cparams("arbitrary"),
            name="context_attn",
        )(proj_c, proj_c, proj_c, hmask)
    return out_l, out_c


def _merge_kernel(x_ref, mod_ref, g_ref, gt0, gt1, gt2, gt3, a_ref, s5_ref, ro_ref, rg_ref, na_ref,
                  wglu_ref, bglu_ref, gn_ref, avg_ref, wb_ref, wo_ref, o_ref, *, tiles_per_mod, mod_base):
    i = pl.program_id(0)
    _, _, gate_a = _mod_rows(mod_ref, i, tiles_per_mod, mod_base, 0)
    z = _gelu_tanh(s5_ref[...]).astype(BF16)
    zf = z.astype(F32)
    b_s5 = (zf * _sigmoid(_dot(z, wglu_ref[...]) + bglu_ref[...])).astype(BF16)
    o = ro_ref[0] + ro_ref[1]
    avg = avg_ref[...]
    hi, lo = _split_bf16(o)
    mu = _dot(hi, avg) + _dot(lo, avg)
    dlt = o - mu
    hi, lo = _split_bf16(dlt * dlt)
    var = _dot(hi, avg) + _dot(lo, avg)
    hn = dlt * lax.rsqrt(var + EPS) * gn_ref[...]
    b_ret = (_silu(rg_ref[...].astype(F32)) * hn).astype(BF16)
    outs = (a_ref[...], b_s5, b_ret, na_ref[...])
    gates = (gt0, gt1, gt2, gt3)
    y = _sigmoid(gates[0][...].astype(F32)) * _dot(outs[0], wb_ref[0])
    for b in range(1, N_BRANCH):
        y = y + _sigmoid(gates[b][...].astype(F32)) * _dot(outs[b], wb_ref[b])
    yo = _dot(y.astype(BF16), wo_ref[...])
    o_ref[...] = x_ref[...] + gate_a * _rms(yo, g_ref[...])


def _merge(x, mod, g1, proj, a, s5y, ret_o, na, lw, *, rows_per_mod, mod_base):
    rows, d = x.shape
    tm = min(512, rows)
    nt = rows // tm

    def row(shape, col=0):
        return pl.BlockSpec(shape, lambda i: (i, col))

    def const(arr):
        return pl.BlockSpec(arr.shape, lambda i: (0,) * arr.ndim)

    kern = functools.partial(_merge_kernel, tiles_per_mod=max(rows_per_mod // tm, 1), mod_base=mod_base)
    ins = [x, mod, g1.reshape(1, d), proj, proj, proj, proj, a, s5y, ret_o, proj, na,
           lw['w_glu'], lw['b_glu'], lw['ret_gn'], lw['avg'], lw['w_branch'], lw['w_out']]
    specs = [
        row((tm, d)), const(mod), pl.BlockSpec((1, d), lambda i: (0, 0)),
        row((tm, d), 0), row((tm, d), 1), row((tm, d), 2), row((tm, d), 3),
        row((tm, BRANCH_W)), row((tm, BRANCH_W)),
        pl.BlockSpec((2, tm, BRANCH_W), lambda i: (0, i, 0)),
        row((tm, BRANCH_W), COL_RG), row((tm, BRANCH_W)),
        const(lw['w_glu']), const(lw['b_glu']), const(lw['ret_gn']), const(lw['avg']),
        const(lw['w_branch']), const(lw['w_out']),
    ]
    return pl.pallas_call(
        kern,
        grid=(nt,),
        in_specs=specs,
        out_specs=row((tm, d)),
        out_shape=jax.ShapeDtypeStruct((rows, d), F32),
        compiler_params=_cparams("arbitrary"),
        name="merge_out",
    )(*ins)


def _ffn_kernel(x_ref, mod_ref, g2_ref, g3_ref, wg_ref, wu_ref, wd_ref, o_ref, h_scr, acc_scr,
                *, tiles_per_mod, mod_base, n_f):
    i = pl.program_id(0)
    f = pl.program_id(1)

    @pl.when(f == 0)
    def _():
        sh, sc, _ = _mod_rows(mod_ref, i, tiles_per_mod, mod_base, 3)
        h_scr[...] = (_rms(x_ref[...], g2_ref[...]) * (1.0 + sc) + sh).astype(BF16)
        acc_scr[...] = jnp.zeros_like(acc_scr)

    h = h_scr[...]
    act = (_silu(_dot(h, wg_ref[...])) * _dot(h, wu_ref[...])).astype(BF16)
    acc_scr[...] += _dot(act, wd_ref[...])

    @pl.when(f == n_f - 1)
    def _():
        _, _, gate_f = _mod_rows(mod_ref, i, tiles_per_mod, mod_base, 3)
        o_ref[...] = x_ref[...] + gate_f * _rms(acc_scr[...], g3_ref[...])


def _ffn_dense(x, mod, g2, g3, wg, wu, wd, *, rows_per_mod, mod_base):
    rows, d = x.shape
    d_ff = wg.shape[1]
    tm = min(512, rows)
    tf = d_ff // 2 if (d_ff // 2) % 128 == 0 else d_ff
    n_f = d_ff // tf
    kern = functools.partial(_ffn_kernel, tiles_per_mod=max(rows_per_mod // tm, 1), mod_base=mod_base, n_f=n_f)
    return pl.pallas_call(
        kern,
        grid=(rows // tm, n_f),
        in_specs=[
            pl.BlockSpec((tm, d), lambda i, f: (i, 0)),
            pl.BlockSpec(mod.shape, lambda i, f: (0, 0)),
            pl.BlockSpec((1, d), lambda i, f: (0, 0)),
            pl.BlockSpec((1, d), lambda i, f: (0, 0)),
            pl.BlockSpec((d, tf), lambda i, f: (0, f)),
            pl.BlockSpec((d, tf), lambda i, f: (0, f)),
            pl.BlockSpec((tf, d), lambda i, f: (f, 0)),
        ],
        out_specs=pl.BlockSpec((tm, d), lambda i, f: (i, 0)),
        out_shape=jax.ShapeDtypeStruct((rows, d), F32),
        scratch_shapes=[pltpu.VMEM((tm, d), BF16), pltpu.VMEM((tm, d), F32)],
        compiler_params=_cparams("arbitrary", "arbitrary"),
        name="ffn_dense",
    )(x, mod, g2.reshape(1, d), g3.reshape(1, d), wg, wu, wd)


def _router_kernel(x_ref, mod_ref, g2_ref, wr_ref, br_ref, h_ref, comb_ref, *, tiles_per_mod, mod_base):
    i = pl.program_id(0)
    sh, sc, _ = _mod_rows(mod_ref, i, tiles_per_mod, mod_base, 3)
    h = _rms(x_ref[...], g2_ref[...]) * (1.0 + sc) + sh
    h_ref[...] = _pack_pairs(h)
    h_hi, h_lo = _split_bf16(h)
    w_hi, w_lo = _split_bf16(wr_ref[...])
    logits = _dot(h_hi, w_hi) + _dot(h_lo, w_hi) + _dot(h_hi, w_lo) + br_ref[...]
    lane = lax.broadcasted_iota(jnp.int32, logits.shape, 1)
    v1 = jnp.max(logits, axis=-1, keepdims=True)
    i1 = jnp.min(jnp.where(logits == v1, lane, 128), axis=-1, keepdims=True)
    rest = jnp.where(lane == i1, NEG_BIG, logits)
    v2 = jnp.max(rest, axis=-1, keepdims=True)
    i2 = jnp.min(jnp.where(rest == v2, lane, 128), axis=-1, keepdims=True)
    e = jnp.exp(v2 - v1)
    w1 = 1.0 / (1.0 + e)
    w2 = e / (1.0 + e)
    meta = jnp.where(lane == 0, i1.astype(F32), 0.0) + jnp.where(lane == 1, i2.astype(F32), 0.0)
    comb_ref[...] = meta + jnp.where(lane == 2, w1, 0.0) + jnp.where(lane == 3, w2, 0.0)


def _router(x, mod, g2, w_router, b_router, *, rows_per_mod, mod_base):
    rows, d = x.shape
    tm = min(512, rows)
    wr = jnp.zeros((d, 128), F32).at[:, :N_EXPERTS].set(w_router)
    br = jnp.full((1, 128), NEG_BIG, F32).at[0, :N_EXPERTS].set(b_router)
    kern = functools.partial(_router_kernel, tiles_per_mod=max(rows_per_mod // tm, 1), mod_base=mod_base)
    return pl.pallas_call(
        kern,
        grid=(rows // tm,),
        in_specs=[
            pl.BlockSpec((tm, d), lambda i: (i, 0)),
            pl.BlockSpec(mod.shape, lambda i: (0, 0)),
            pl.BlockSpec((1, d), lambda i: (0, 0)),
            pl.BlockSpec((d, 128), lambda i: (0, 0)),
            pl.BlockSpec((1, 128), lambda i: (0, 0)),
        ],
        out_specs=[pl.BlockSpec((tm, d // 2), lambda i: (i, 0)), pl.BlockSpec((tm, 128), lambda i: (i, 0))],
        out_shape=[jax.ShapeDtypeStruct((rows, d // 2), jnp.int32), jax.ShapeDtypeStruct((rows, 128), F32)],
        compiler_params=_cparams("arbitrary"),
        name="moe_router",
    )(x, mod, g2.reshape(1, d), wr, br)


def _sc_gather(table, idx):
    n_idx = idx.shape[0]
    width = table.shape[1]
    per_worker = n_idx // SC_WORKERS
    chunk_rows = math.gcd(per_worker, SC_GATHER_ROWS)
    n_chunks = per_worker // chunk_rows
    assert per_worker * SC_WORKERS == n_idx and chunk_rows % 8 == 0
    mesh = plsc.VectorSubcoreMesh(core_axis_name="c", subcore_axis_name="s")

    @functools.partial(
        pl.kernel, mesh=mesh,
        out_type=jax.ShapeDtypeStruct((n_idx, width), table.dtype),
        scratch_types=[
            pltpu.VMEM((chunk_rows,), jnp.int32),
            pltpu.VMEM((chunk_rows, width), table.dtype),
            pltpu.SemaphoreType.DMA,
        ],
        name="sc_row_gather",
    )
    def gather(table_hbm, idx_hbm, out_hbm, idx_v, rows_v, sem):
        wid = lax.axis_index("s") * SC_CORES + lax.axis_index("c")
        base = wid * per_worker

        @pl.loop(0, n_chunks)
        def _(j):
            off = base + j * chunk_rows
            pltpu.sync_copy(idx_hbm.at[pl.ds(off, chunk_rows)], idx_v)
            pltpu.async_copy(table_hbm.at[idx_v], rows_v, sem).wait()
            pltpu.sync_copy(rows_v, out_hbm.at[pl.ds(off, chunk_rows)])

    return gather(table, idx)


def _moe_plan(meta, rows):
    tile = MOE_ROW_TILE
    n_tiles = (2 * rows) // tile + N_EXPERTS
    n_slots = n_tiles * tile
    experts = jnp.concatenate([meta[:, 0], meta[:, 1]]).astype(jnp.int32)
    onehot = (experts[:, None] == jnp.arange(N_EXPERTS)[None, :]).astype(jnp.int32)
    csum = jnp.cumsum(onehot, axis=0)
    counts = csum[-1]
    rank = jnp.sum(onehot * csum, axis=1) - 1
    padded = ((counts + tile - 1) // tile) * tile
    ends = jnp.cumsum(padded)
    starts = ends - padded
    pos = jnp.sum(onehot * starts[None, :], axis=1) + rank
    token = jnp.arange(2 * rows, dtype=jnp.int32) % rows
    src = jnp.zeros((n_slots,), jnp.int32).at[pos].set(token, unique_indices=True)
    tile_start = jnp.arange(n_tiles, dtype=jnp.int32) * tile
    used = tile_start < ends[-1]
    tile_e = jnp.minimum(jnp.sum((tile_start[:, None] >= ends[None, :]).astype(jnp.int32), axis=1), N_EXPERTS - 1)
    last_e = jnp.max(jnp.where(used, tile_e, 0))
    tile_e = jnp.where(used, tile_e, last_e)
    valid_end = jnp.sum((tile_e[:, None] == jnp.arange(N_EXPERTS)[None, :]) * (starts + counts)[None, :], axis=1)
    n_valid = jnp.where(used, jnp.clip(valid_end - tile_start, 0, tile), 0).astype(jnp.int32)
    return pos.astype(jnp.int32), src, tile_e.astype(jnp.int32), n_valid


def _moe_group_kernel(eid_ref, nval_ref, hs_ref, wg_ref, wu_ref, wd_ref, y_ref, acc_scr, *, n_f):
    w = pl.program_id(0)
    f = pl.program_id(1)
    nv = nval_ref[w]

    @pl.when(nv > 0)
    def _():
        h = _unpack_pairs(hs_ref[...]).astype(BF16)
        act = (_silu(_dot(h, wg_ref[...])) * _dot(h, wu_ref[...])).astype(BF16)
        part = _dot(act, wd_ref[...])

        @pl.when(f == 0)
        def _():
            acc_scr[...] = part

        @pl.when(f > 0)
        def _():
            acc_scr[...] += part

        @pl.when(f == n_f - 1)
        def _():
            y_ref[...] = _pack_pairs(acc_scr[...])


def _moe_grouped(hs, tile_e, n_valid, wg, wu, wd):
    n_slots = hs.shape[0]
    d = wg.shape[1]
    d_ff = wg.shape[2]
    tile = MOE_ROW_TILE
    n_f = 2
    tf = d_ff // n_f

    def f_idx(f, nval, w):
        return jnp.where(nval[w] > 0, f, n_f - 1)

    grid_spec = pltpu.PrefetchScalarGridSpec(
        num_scalar_prefetch=2,
        grid=(n_slots // tile, n_f),
        in_specs=[
            pl.BlockSpec((tile, d // 2), lambda w, f, eid, nval: (w, 0)),
            pl.BlockSpec((None, d, tf), lambda w, f, eid, nval: (eid[w], 0, f_idx(f, nval, w))),
            pl.BlockSpec((None, d, tf), lambda w, f, eid, nval: (eid[w], 0, f_idx(f, nval, w))),
            pl.BlockSpec((None, tf, d), lambda w, f, eid, nval: (eid[w], f_idx(f, nval, w), 0)),
        ],
        out_specs=pl.BlockSpec((tile, d // 2), lambda w, f, eid, nval: (w, 0)),
        scratch_shapes=[pltpu.VMEM((tile, d), F32)],
    )
    return pl.pallas_call(
        functools.partial(_moe_group_kernel, n_f=n_f),
        grid_spec=grid_spec,
        out_shape=jax.ShapeDtypeStruct((n_slots, d // 2), jnp.int32),
        compiler_params=_cparams("arbitrary", "arbitrary"),
        name="moe_experts",
    )(tile_e, n_valid, hs, wg, wu, wd)


def _moe_out_kernel(x_ref, y1_ref, y2_ref, meta_ref, mod_ref, g3_ref, o_ref, *, tiles_per_mod, mod_base):
    i = pl.program_id(0)
    _, _, gate_f = _mod_rows(mod_ref, i, tiles_per_mod, mod_base, 3)
    meta = meta_ref[...]
    y = meta[:, 2:3] * _unpack_pairs(y1_ref[...]) + meta[:, 3:4] * _unpack_pairs(y2_ref[...])
    o_ref[...] = x_ref[...] + gate_f * _rms(y, g3_ref[...])


def _moe_combine(x, yg, meta, mod, g3, *, rows_per_mod, mod_base):
    rows, d = x.shape
    tm = min(512, rows)
    nt = rows // tm
    kern = functools.partial(_moe_out_kernel, tiles_per_mod=max(rows_per_mod // tm, 1), mod_base=mod_base)
    return pl.pallas_call(
        kern,
        grid=(nt,),
        in_specs=[
            pl.BlockSpec((tm, d), lambda i: (i, 0)),
            pl.BlockSpec((tm, d // 2), lambda i: (i, 0)),
            pl.BlockSpec((tm, d // 2), lambda i: (nt + i, 0)),
            pl.BlockSpec((tm, 128), lambda i: (i, 0)),
            pl.BlockSpec(mod.shape, lambda i: (0, 0)),
            pl.BlockSpec((1, d), lambda i: (0, 0)),
        ],
        out_specs=pl.BlockSpec((tm, d), lambda i: (i, 0)),
        out_shape=jax.ShapeDtypeStruct((rows, d), F32),
        compiler_params=_cparams("arbitrary"),
        name="moe_combine",
    )(x, yg, yg, meta, mod, g3.reshape(1, d))


def _moe_sparse(x, h, meta, mod, g3, wg, wu, wd, *, rows_per_mod, mod_base):
    rows = x.shape[0]
    pos, src, tile_e, n_valid = _moe_plan(meta, rows)
    hs = _sc_gather(h, src)
    ys = _moe_grouped(hs, tile_e, n_valid, wg, wu, wd)
    yg = _sc_gather(ys, pos)
    return _moe_combine(x, yg, meta, mod, g3, rows_per_mod=rows_per_mod, mod_base=mod_base)


def _cast_kernel(w_ref, o_ref):
    o_ref[...] = w_ref[...].astype(BF16)


def _cast_bf16(w):
    w3 = w if w.ndim == 3 else w[None]
    n_e, k, n = w3.shape
    bk = min(k, 256)
    out = pl.pallas_call(
        _cast_kernel,
        grid=(n_e, k // bk),
        in_specs=[pl.BlockSpec((None, bk, n), lambda e, i: (e, i, 0))],
        out_specs=pl.BlockSpec((None, bk, n), lambda e, i: (e, i, 0)),
        out_shape=jax.ShapeDtypeStruct(w3.shape, BF16),
        compiler_params=_cparams("arbitrary", "arbitrary"),
        name="cast_weights",
    )(w3)
    return out if w.ndim == 3 else out[0]


def _permute_w_in(w_in):
    k, n = w_in.shape
    n_blocks = n // BRANCH_W
    shift = 9
    return pl.pallas_call(
        _cast_kernel,
        grid=(n_blocks,),
        in_specs=[pl.BlockSpec((k, BRANCH_W), lambda j: (0, (j + shift) % n_blocks))],
        out_specs=pl.BlockSpec((k, BRANCH_W), lambda j: (0, j)),
        out_shape=jax.ShapeDtypeStruct((k, n), BF16),
        compiler_params=_cparams("arbitrary"),
        name="cast_permute_w_in",
    )(w_in)


def kernel(x, c, ctx, c_ctx, w_mod, b_mod, norm_g, w_in, s5_a_re, s5_a_im, s5_log_dt, s5_b_re, s5_b_im, s5_c_re, s5_c_im, s5_d, s5_w_glu, s5_b_glu, ret_decay, ret_gn, na_rpb, w_branch, w_out, ffn_w_gate, ffn_w_up, ffn_w_down, moe_w_router, moe_b_router, moe_w_gate, moe_w_up, moe_w_down):
    batch, seq_len, d = x.shape
    ctx_len = ctx.shape[1]
    depth = w_mod.shape[0]
    cond = jnp.concatenate([c, c_ctx[None, :]], axis=0)
    mod_all = _modulation(cond, w_mod, b_mod)
    rope = _rope_tables(seq_len)
    lane_h = np.repeat(np.arange(RET_HEADS), RET_DIM)
    avg = jnp.asarray((lane_h[:, None] == lane_h[None, :]).astype(np.float32) / RET_DIM, BF16)

    xl = x.reshape(batch * seq_len, d)
    xc = ctx.reshape(batch * ctx_len, d)
    lat = dict(rows_per_mod=seq_len, mod_base=0)
    cxt = dict(rows_per_mod=batch * ctx_len, mod_base=batch)

    for layer in range(depth):
        last = layer == depth - 1
        need_ctx = not last
        mod = mod_all[layer]
        ng = norm_g[layer]
        w_in_bf = _permute_w_in(w_in[layer])
        s5_tabs = _s5_tables(s5_a_re[layer], s5_a_im[layer], s5_log_dt[layer], s5_b_re[layer], s5_b_im[layer],
                             s5_c_re[layer], s5_c_im[layer], s5_d[layer], batch)
        ret_tabs = _ret_tables(ret_decay[layer])
        na_bias, na_hmask = _na_tables(na_rpb[layer])
        lw = dict(w_glu=s5_w_glu[layer].astype(BF16), b_glu=s5_b_glu[layer].reshape(1, BRANCH_W).astype(F32),
                  ret_gn=ret_gn[layer].reshape(1, BRANCH_W).astype(F32), avg=avg,
                  w_branch=_cast_bf16(w_branch[layer]), w_out=_cast_bf16(w_out[layer]))

        proj_l, f_l = _in_proj(xl, mod, ng[0], w_in_bf, **lat)
        proj_c, f_c = _in_proj(xc, mod, ng[0], w_in_bf, **cxt)

        a_l = _fourier_latent(f_l, batch, seq_len)
        s_l, s_c = _s5_mixer(proj_l[:, COL_S * BRANCH_W:(COL_S + 1) * BRANCH_W],
                             proj_c[:, COL_S * BRANCH_W:(COL_S + 1) * BRANCH_W], s5_tabs, batch)
        r_l, r_c = _retention(proj_l, proj_c, ret_tabs, rope, batch, seq_len, ctx_len)
        n_l, n_c = _neighborhood(proj_l, proj_c, na_bias, na_hmask, batch, seq_len, ctx_len, need_ctx)

        xl = _merge(xl, mod, ng[1], proj_l, a_l, s_l, r_l, n_l, lw, **lat)
        if need_ctx:
            a_c = _fourier_ctx(f_c, batch, ctx_len)
            xc = _merge(xc, mod, ng[1], proj_c, a_c, s_c, r_c, n_c, lw, **cxt)

        i = layer // 2
        if layer % 2 == 0:
            wg, wu, wd = _cast_bf16(ffn_w_gate[i]), _cast_bf16(ffn_w_up[i]), _cast_bf16(ffn_w_down[i])
            xl = _ffn_dense(xl, mod, ng[2], ng[3], wg, wu, wd, **lat)
            if need_ctx:
                xc = _ffn_dense(xc, mod, ng[2], ng[3], wg, wu, wd, **cxt)
        else:
            wg, wu, wd = _cast_bf16(moe_w_gate[i]), _cast_bf16(moe_w_up[i]), _cast_bf16(moe_w_down[i])
            h, meta = _router(xl, mod, ng[2], moe_w_router[i], moe_b_router[i], **lat)
            xl = _moe_sparse(xl, h, meta, mod, ng[3], wg, wu, wd, **lat)
            if need_ctx:
                hc, metac = _router(xc, mod, ng[2], moe_w_router[i], moe_b_router[i], **cxt)
                xc = _moe_sparse(xc, hc, metac, mod, ng[3], wg, wu, wd, **cxt)
    return xl.reshape(batch, seq_len, d)
```

```python
import functools
import math

import numpy as np
import jax
import jax.numpy as jnp
from jax import lax
from jax.experimental import pallas as pl
from jax.experimental.pallas import tpu as pltpu
from jax.experimental.pallas import tpu_sc as plsc

F32 = jnp.float32
BF16 = jnp.bfloat16

D_MODEL = 1024
BRANCH_W = 256
N_BRANCH = 4
GRID_W = 64
FNET_GROUP_DIM = 64
S5_GROUP_CH = 16
S5_GROUPS = 16
S5_STATE = 64
S5_CHUNK = 32
S5_PAIRS = S5_GROUPS // 2
RET_HEADS = 4
RET_DIM = 64
RET_CHUNK = 128
NA_HEADS = 4
NA_DIM = 64
NA_WIN_ROWS = 8
NA_WIN_COLS = 16
NA_QROWS = 8
ROPE_BASE = 10000.0
N_EXPERTS = 8
EPS = 1e-6
FFT_N2 = 256
NEG_BIG = -1e30
VMEM_LIMIT_BYTES = 50 * 1024 * 1024
SC_CORES = 2
SC_SUBCORES = 16
SC_WORKERS = SC_CORES * SC_SUBCORES
SC_GATHER_ROWS = 64
MOE_ROW_TILE = 1024
MOE_FF_TILE = 512

COL_F, COL_S, COL_RQ, COL_RK, COL_RV, COL_RG, COL_NQ, COL_NK, COL_NV = range(16, 25)
IN_W = 9 * BRANCH_W + N_BRANCH * D_MODEL
IN_TN = 1280
IN_F_TILE = (N_BRANCH * D_MODEL) // IN_TN
IN_F_OFF = N_BRANCH * D_MODEL - IN_F_TILE * IN_TN


def _cparams(*sem):
    return pltpu.CompilerParams(dimension_semantics=sem, vmem_limit_bytes=VMEM_LIMIT_BYTES)


def _sigmoid(v):
    return 0.5 * jnp.tanh(0.5 * v) + 0.5


def _silu(v):
    return v * _sigmoid(v)


def _gelu_tanh(v):
    return 0.5 * v * (1.0 + jnp.tanh(math.sqrt(2.0 / math.pi) * (v + 0.044715 * (v * v * v))))


def _rms(v, g):
    ms = jnp.mean(v * v, axis=-1, keepdims=True)
    return v * lax.rsqrt(ms + EPS) * g


def _split_bf16(v):
    hi = v.astype(BF16)
    lo = (v - hi.astype(F32)).astype(BF16)
    return hi, lo


def _pack_pairs(v):
    n = v.shape[1] // 2
    lo = lax.bitcast_convert_type(v[:, :n].astype(BF16).astype(F32), jnp.int32)
    hi = lax.bitcast_convert_type(v[:, n:].astype(BF16).astype(F32), jnp.int32)
    return (hi & -65536) | ((lo >> 16) & 65535)


def _unpack_pairs(w):
    lo = lax.bitcast_convert_type(w << 16, F32)
    hi = lax.bitcast_convert_type(w & -65536, F32)
    return jnp.concatenate([lo, hi], axis=-1)


def _dot(a, b):
    return jnp.dot(a, b, preferred_element_type=F32)


def _dot_nt(a, b):
    return lax.dot_general(a, b, (((1,), (1,)), ((), ())), preferred_element_type=F32)


def _dot_tn(a, b):
    return lax.dot_general(a, b, (((0,), (0,)), ((), ())), preferred_element_type=F32)


def _mod_kernel(ct_ref, w_ref, b_ref, o_ref):
    ct = ct_ref[...]
    s = _silu(ct)
    w = w_ref[...]
    rows = [jnp.sum(w * s[:, r:r + 1], axis=0, keepdims=True) for r in range(8)]
    o_ref[...] = jnp.concatenate(rows, axis=0) + b_ref[...]


def _modulation(cond, w_mod, b_mod):
    n_layers, d, n = w_mod.shape
    tn = 512
    ct = jnp.zeros((8, d), F32).at[:cond.shape[0]].set(cond).T
    return pl.pallas_call(
        _mod_kernel,
        grid=(n_layers, n // tn),
        in_specs=[
            pl.BlockSpec((d, 8), lambda l, j: (0, 0)),
            pl.BlockSpec((None, d, tn), lambda l, j: (l, 0, j)),
            pl.BlockSpec((None, 1, tn), lambda l, j: (l, 0, j)),
        ],
        out_specs=pl.BlockSpec((None, 8, tn), lambda l, j: (l, 0, j)),
        out_shape=jax.ShapeDtypeStruct((n_layers, 8, n), F32),
        compiler_params=_cparams("arbitrary", "arbitrary"),
        name="adaln_mod",
    )(ct, w_mod, b_mod.reshape(n_layers, 1, n))


def _mod_rows(mod_ref, i, tiles_per_mod, mod_base, first):
    r = mod_base + i // tiles_per_mod
    return [mod_ref[pl.ds(r, 1), (first + k) * D_MODEL:(first + k + 1) * D_MODEL] for k in range(3)]


def _in_kernel(x_ref, mod_ref, g_ref, w_ref, proj_ref, f_ref, h_scr, *, tiles_per_mod, mod_base):
    i = pl.program_id(0)
    j = pl.program_id(1)

    @pl.when(j == 0)
    def _():
        sh, sc, _ = _mod_rows(mod_ref, i, tiles_per_mod, mod_base, 0)
        h_scr[...] = (_rms(x_ref[...], g_ref[...]) * (1.0 + sc) + sh).astype(BF16)

    res = _dot(h_scr[...], w_ref[...])
    proj_ref[...] = res.astype(BF16)

    @pl.when(j == IN_F_TILE)
    def _():
        f_ref[...] = res[:, IN_F_OFF:IN_F_OFF + BRANCH_W].astype(BF16)


def _in_proj(x, mod, g, w_bf, *, rows_per_mod, mod_base):
    rows, d = x.shape
    tm = math.gcd(1024, rows_per_mod)
    kern = functools.partial(_in_kernel, tiles_per_mod=max(rows_per_mod // tm, 1), mod_base=mod_base)
    return pl.pallas_call(
        kern,
        grid=(rows // tm, IN_W // IN_TN),
        in_specs=[
            pl.BlockSpec((tm, d), lambda i, j: (i, 0)),
            pl.BlockSpec(mod.shape, lambda i, j: (0, 0)),
            pl.BlockSpec((1, d), lambda i, j: (0, 0)),
            pl.BlockSpec((d, IN_TN), lambda i, j: (0, j)),
        ],
        out_specs=[
            pl.BlockSpec((tm, IN_TN), lambda i, j: (i, j)),
            pl.BlockSpec((tm, BRANCH_W), lambda i, j: (i, 0)),
        ],
        out_shape=[
            jax.ShapeDtypeStruct((rows, IN_W), BF16),
            jax.ShapeDtypeStruct((rows, BRANCH_W), BF16),
        ],
        scratch_shapes=[pltpu.VMEM((tm, d), BF16)],
        compiler_params=_cparams("arbitrary", "arbitrary"),
        name="in_proj",
    )(x, mod, g.reshape(1, d), w_bf)


def _fft_a_kernel(x_ref, cs_ref, tc_ref, ts_ref, zr_ref, zi_ref, *, n1, n1p):
    y = _dot(cs_ref[...].astype(BF16), x_ref[...])
    yr = y[:n1]
    yi = y[n1p:n1p + n1]
    tc = tc_ref[...]
    ts = ts_ref[...]
    zr_ref[...] = (yr * tc + yi * ts).astype(BF16)
    zi_ref[...] = (yi * tc - yr * ts).astype(BF16)


def _fft_b_kernel(zr_ref, zi_ref, cs_ref, cc_ref, sc_ref, o_ref, *, kb, scale, has_imag):
    cs = cs_ref[...].astype(BF16)
    cc = cc_ref[...].astype(BF16)
    sc = sc_ref[...].astype(BF16)
    for kk in range(kb):
        a = _dot(cs, zr_ref[kk])
        if has_imag:
            b = _dot(cs, zi_ref[kk])
            xr = a[:FFT_N2] + b[FFT_N2:]
            xi = b[:FFT_N2] - a[FFT_N2:]
        else:
            xr = a[:FFT_N2]
            xi = -a[FFT_N2:]
        out = _dot(xr.astype(BF16), cc) + _dot(xi.astype(BF16), sc)
        o_ref[:, kk * BRANCH_W:(kk + 1) * BRANCH_W] = (out * scale).astype(BF16)


def _dft_tables(n):
    k = np.arange(n)
    ang = 2.0 * np.pi * ((k[:, None] * k[None, :]) % n) / n
    return np.cos(ang), np.sin(ang)


def _fft_b_call(zr, zi, n1, batch, seq_len, has_imag):
    c2, s2 = _dft_tables(FFT_N2)
    cs2 = jnp.asarray(np.concatenate([c2, s2], axis=0), F32)
    c64, s64 = _dft_tables(FNET_GROUP_DIM)
    eye = np.eye(BRANCH_W // FNET_GROUP_DIM)
    cc = jnp.asarray(np.kron(eye, c64), F32)
    sc = jnp.asarray(np.kron(eye, s64), F32)
    kb = min(8, n1)
    scale = 1.0 / math.sqrt(seq_len * FNET_GROUP_DIM)
    kern = functools.partial(_fft_b_kernel, kb=kb, scale=scale, has_imag=has_imag)
    zspec = pl.BlockSpec((None, kb, FFT_N2, BRANCH_W), lambda b, i: (b, i, 0, 0))
    out = pl.pallas_call(
        kern,
        grid=(batch, n1 // kb),
        in_specs=[
            zspec, zspec,
            pl.BlockSpec((2 * FFT_N2, FFT_N2), lambda b, i: (0, 0)),
            pl.BlockSpec((BRANCH_W, BRANCH_W), lambda b, i: (0, 0)),
            pl.BlockSpec((BRANCH_W, BRANCH_W), lambda b, i: (0, 0)),
        ],
        out_specs=pl.BlockSpec((None, FFT_N2, kb * BRANCH_W), lambda b, i: (b, 0, i)),
        out_shape=jax.ShapeDtypeStruct((batch, FFT_N2, n1 * BRANCH_W), BF16),
        compiler_params=_cparams("arbitrary", "arbitrary"),
        name="fourier_stage_b",
    )(zr, zi, cs2, cc, sc)
    return out.reshape(batch * seq_len, BRANCH_W)


def _fourier_latent(f, batch, seq_len):
    n1 = seq_len // FFT_N2
    wide = FFT_N2 * BRANCH_W
    c1, s1 = _dft_tables(n1)
    n1p = max(n1, 8)
    cs1 = np.zeros((2 * n1p, n1))
    cs1[:n1] = c1
    cs1[n1p:n1p + n1] = -s1
    k1 = np.arange(n1)[:, None]
    l2 = np.arange(FFT_N2)[None, :]
    tw = 2.0 * np.pi * (k1 * l2) / seq_len
    tc = jnp.asarray(np.repeat(np.cos(tw), BRANCH_W, axis=1), F32)
    ts = jnp.asarray(np.repeat(np.sin(tw), BRANCH_W, axis=1), F32)
    cw = min(8192, wide)
    xv = f.reshape(batch, n1, wide)
    spec = pl.BlockSpec((None, n1, cw), lambda b, j: (b, 0, j))
    tspec = pl.BlockSpec((n1, cw), lambda b, j: (0, j))
    zr, zi = pl.pallas_call(
        functools.partial(_fft_a_kernel, n1=n1, n1p=n1p),
        grid=(batch, wide // cw),
        in_specs=[spec, pl.BlockSpec((2 * n1p, n1), lambda b, j: (0, 0)), tspec, tspec],
        out_specs=[spec, spec],
        out_shape=[jax.ShapeDtypeStruct((batch, n1, wide), BF16)] * 2,
        compiler_params=_cparams("arbitrary", "arbitrary"),
        name="fourier_stage_a",
    )(xv, jnp.asarray(cs1, F32), tc, ts)
    zr = zr.reshape(batch, n1, FFT_N2, BRANCH_W)
    zi = zi.reshape(batch, n1, FFT_N2, BRANCH_W)
    return _fft_b_call(zr, zi, n1, batch, seq_len, True)


def _fourier_ctx(f, batch, ctx_len):
    assert ctx_len == FFT_N2
    z = f.reshape(batch, 1, FFT_N2, BRANCH_W)
    return _fft_b_call(z, z, 1, batch, ctx_len, False)


def _s5_tables(a_re, a_im, log_dt, b_re, b_im, c_re, c_im, d_skip, batch):
    t = S5_CHUNK
    g, p, hc = S5_GROUPS, S5_STATE, S5_GROUP_CH
    lam = lax.complex(a_re.astype(F32), a_im.astype(F32))
    dt = jnp.exp(log_dt.astype(F32))[..., None]
    ks = jnp.arange(t + 1, dtype=F32)
    apow = jnp.exp((lam * dt)[..., None] * ks)
    a_bar = apow[..., 1]
    b_bar = ((a_bar - 1.0) / lam)[..., None] * lax.complex(b_re.astype(F32), b_im.astype(F32))
    cm = lax.complex(c_re.astype(F32), c_im.astype(F32))
    kimp = jnp.real(jnp.einsum('dghp,dgpk,dgpj->dgkhj', cm, apow[..., :t], b_bar,
                               precision=lax.Precision.HIGHEST))
    kf, kb = kimp[0], kimp[1]
    kfull = jnp.concatenate([kb[:, :0:-1], kf[:, :1] + kb[:, :1], kf[:, 1:]], axis=1)
    strip = kfull.transpose(0, 3, 1, 2).reshape(g, hc, (2 * t - 1) * hc)
    strip = jnp.pad(strip, ((0, 0), (0, 0), (0, 2 * t * hc - strip.shape[-1])))
    strip = strip.reshape(S5_PAIRS, 2, hc, 2 * t * hc)

    def pair_blocks(kd, axis):
        r, c = kd.shape[1:]
        kp = kd.reshape(S5_PAIRS, 2, r, c)
        z = jnp.zeros_like(kp[:, 0])
        top = jnp.concatenate([kp[:, 0], z], axis=-1)
        bot = jnp.concatenate([z, kp[:, 1]], axis=-1)
        return jnp.concatenate([top, bot], axis=1)

    wf = jnp.einsum('gpj,gph->gjhp', apow[0][..., t - 1::-1][..., :t], b_bar[0])
    wb = jnp.einsum('gpj,gph->gjhp', apow[1][..., :t], b_bar[1])
    wf = wf.reshape(g, t * hc, p)
    wb = wb.reshape(g, t * hc, p)
    kinds = [jnp.real(wf), jnp.imag(wf), jnp.real(wb), jnp.imag(wb)]
    we = jnp.concatenate([pair_blocks(kd, 0) for kd in kinds], axis=-1).astype(BF16)

    vf = jnp.einsum('ghp,gpt->gpth', cm[0], apow[0][..., 1:t + 1])
    vb = jnp.einsum('ghp,gpt->gpth', cm[1], apow[1][..., t:0:-1])
    vf = vf.reshape(g, p, t * hc)
    vb = vb.reshape(g, p, t * hc)
    vkinds = [jnp.real(vf), -jnp.imag(vf), jnp.real(vb), -jnp.imag(vb)]
    v1 = jnp.concatenate([pair_blocks(kd, 0) for kd in vkinds], axis=1)
    v = jnp.concatenate([v1, v1], axis=1).astype(BF16)

    def lanes(z):
        return jnp.tile(z.reshape(1, g * p), (1, batch))

    at = apow[..., t]
    a_tab = jnp.concatenate([lanes(jnp.real(at[0])), lanes(jnp.imag(at[0])),
                             lanes(jnp.real(at[1])), lanes(jnp.imag(at[1]))], axis=0)
    dvec = jnp.tile(d_skip.astype(F32).reshape(S5_PAIRS, 2, 1, hc), (1, 1, t, 1)).reshape(S5_PAIRS, 1, 2 * t * hc)
    return dict(strip=strip, we=we, v=v, a_tab=a_tab, dvec=dvec)


def _s5_e_kernel(u_ref, we_ref, ref_, imf_, reb_, imb_):
    e = _dot(u_ref[...], we_ref[...])
    ref_[...] = e[:, 0:128]
    imf_[...] = e[:, 128:256]
    reb_[...] = e[:, 256:384]
    imb_[...] = e[:, 384:512]


def _s5_scan_kernel(a_ref, ref_, imf_, reb_, imb_, prf, pif, prb, pib, *, n_rows, n_ctx):
    afr = a_ref[0:1, :]
    afi = a_ref[1:2, :]
    abr = a_ref[2:3, :]
    abi = a_ref[3:4, :]
    zero = jnp.zeros_like(afr)

    def body(s, carry):
        sfr, sfi, sbr, sbi = carry
        nf = s
        nb = jnp.where(s < n_ctx, n_ctx - 1 - s, n_rows - 1 + n_ctx - s)
        prf[pl.ds(nf, 1), :] = sfr
        pif[pl.ds(nf, 1), :] = sfi
        prb[pl.ds(nb, 1), :] = sbr
        pib[pl.ds(nb, 1), :] = sbi
        efr = ref_[pl.ds(nf, 1), :]
        efi = imf_[pl.ds(nf, 1), :]
        ebr = reb_[pl.ds(nb, 1), :]
        ebi = imb_[pl.ds(nb, 1), :]
        nfr = afr * sfr - afi * sfi + efr
        nfi = afr * sfi + afi * sfr + efi
        nbr = abr * sbr - abi * sbi + ebr
        nbi = abr * sbi + abi * sbr + ebi
        return nfr, nfi, nbr, nbi

    lax.fori_loop(0, n_rows, body, (zero, zero, zero, zero))


def _s5_y_kernel(u_ref, strip_ref, v_ref, d_ref, prf, pif, prb, pib, y_ref, m_scr):
    half = S5_CHUNK * S5_GROUP_CH

    @pl.when(pl.program_id(1) == 0)
    def _():
        for gi in range(2):
            strip = strip_ref[gi]
            for j in range(S5_CHUNK):
                off = (S5_CHUNK - 1 - j) * S5_GROUP_CH
                win = strip if off == 0 else pltpu.roll(strip, 2 * half - off, axis=1)
                m_scr[gi, j * S5_GROUP_CH:(j + 1) * S5_GROUP_CH, :] = win[:, :half].astype(BF16)

    u = u_ref[...]
    y_intra = jnp.concatenate([_dot(u[:, :half], m_scr[0]), _dot(u[:, half:], m_scr[1])], axis=-1)
    pcat = jnp.concatenate([prf[...], pif[...], prb[...], pib[...]], axis=-1)
    hi, lo = _split_bf16(pcat)
    y_cross = _dot(jnp.concatenate([hi, lo], axis=-1), v_ref[...])
    y_ref[...] = y_intra + y_cross + d_ref[...] * u.astype(F32)


def _s5_core(u, tabs, batch, n_rows, n_ctx):
    width = batch * S5_PAIRS * 128
    cols = 2 * S5_CHUNK * S5_GROUP_CH
    u_spec = pl.BlockSpec((None, None, n_rows, cols), lambda q, b: (q, b, 0, 0))
    st_spec = pl.BlockSpec((n_rows, 128), lambda q, b: (0, b * S5_PAIRS + q))
    st_shape = jax.ShapeDtypeStruct((n_rows, width), F32)
    e4 = pl.pallas_call(
        _s5_e_kernel,
        grid=(S5_PAIRS, batch),
        in_specs=[u_spec, pl.BlockSpec((None, cols, 512), lambda q, b: (q, 0, 0))],
        out_specs=[st_spec] * 4,
        out_shape=[st_shape] * 4,
        compiler_params=_cparams("arbitrary", "arbitrary"),
        name="s5_chunk_states",
    )(u, tabs['we'])
    p4 = pl.pallas_call(
        functools.partial(_s5_scan_kernel, n_rows=n_rows, n_ctx=n_ctx),
        out_shape=[st_shape] * 4,
        compiler_params=pltpu.CompilerParams(vmem_limit_bytes=VMEM_LIMIT_BYTES),
        name="s5_state_scan",
    )(tabs['a_tab'], *e4)
    y = pl.pallas_call(
        _s5_y_kernel,
        grid=(S5_PAIRS, batch),
        in_specs=[
            u_spec,
            pl.BlockSpec((None, 2, S5_GROUP_CH, cols), lambda q, b: (q, 0, 0, 0)),
            pl.BlockSpec((None, cols, cols), lambda q, b: (q, 0, 0)),
            pl.BlockSpec((None, 1, cols), lambda q, b: (q, 0, 0)),
            st_spec, st_spec, st_spec, st_spec,
        ],
        out_specs=pl.BlockSpec((None, None, n_rows, cols), lambda q, b: (q, b, 0, 0)),
        out_shape=jax.ShapeDtypeStruct((S5_PAIRS, batch, n_rows, cols), F32),
        scratch_shapes=[pltpu.VMEM((2, cols // 2, cols // 2), BF16)],
        compiler_params=_cparams("arbitrary", "arbitrary"),
        name="s5_outputs",
    )(u, tabs['strip'], tabs['v'], tabs['dvec'], *p4)
    return y


def _s5_to_chunks(s, batch):
    n = s.shape[0] // batch // S5_CHUNK
    v = s.reshape(batch, n, S5_CHUNK, S5_PAIRS, 2, S5_GROUP_CH)
    return v.transpose(3, 0, 1, 4, 2, 5).reshape(S5_PAIRS, batch, n, 2 * S5_CHUNK * S5_GROUP_CH)


def _s5_from_chunks(y, batch):
    n = y.shape[2]
    v = y.reshape(S5_PAIRS, batch, n, 2, S5_CHUNK, S5_GROUP_CH)
    return v.transpose(1, 2, 4, 0, 3, 5).reshape(batch * n * S5_CHUNK, BRANCH_W)


def _s5_mixer(s_lat, s_ctx, tabs, batch):
    ul = _s5_to_chunks(s_lat, batch)
    uc = _s5_to_chunks(s_ctx, batch)
    n_ctx = uc.shape[2]
    u = jnp.concatenate([uc, ul], axis=2)
    y = _s5_core(u, tabs, batch, u.shape[2], n_ctx)
    return _s5_from_chunks(y[:, :, n_ctx:], batch), _s5_from_chunks(y[:, :, :n_ctx], batch)


def _ret_tables(ret_decay):
    c = RET_CHUNK
    lg = jax.nn.log_sigmoid(ret_decay.astype(F32))
    lane_h = jnp.repeat(jnp.arange(RET_HEADS), RET_DIM)
    lgl = lg[:, lane_h]
    pos = jnp.arange(c, dtype=F32)[:, None]
    qd = jnp.stack([jnp.exp((pos + 1.0) * lgl[0][None]), jnp.exp((c - pos) * lgl[1][None])])
    kd = jnp.stack([jnp.exp((c - 1.0 - pos) * lgl[0][None]), jnp.exp(pos * lgl[1][None])])
    bmask = (lane_h[:, None] == lane_h[None, :]).astype(F32)
    cd = jnp.exp(c * lgl)[:, :, None] * bmask[None]
    diff = pos - pos.T
    dm = []
    for h in range(RET_HEADS):
        fw = jnp.where(diff >= 0, jnp.exp(jnp.maximum(diff, 0.0) * lg[0, h]), 0.0)
        bw = jnp.where(diff <= 0, jnp.exp(jnp.maximum(-diff, 0.0) * lg[1, h]), 0.0)
        dm.append(fw + bw)
    dm = jnp.concatenate(dm, axis=0)
    hmask = (jnp.arange(RET_HEADS)[:, None] == lane_h[None, :]).astype(F32)
    return dict(qd=qd, kd=kd, cd=cd, bmask=bmask, dm=dm, hmask=hmask)


def _rope_tables(n_tokens):
    t = np.arange(n_tokens)
    row = (t // GRID_W).astype(np.float64)
    col = (t % GRID_W).astype(np.float64)
    n_freq = RET_DIM // 4
    inv_freq = 1.0 / (ROPE_BASE ** (np.arange(n_freq, dtype=np.float64) / n_freq))
    ang = np.concatenate([row[:, None] * inv_freq, col[:, None] * inv_freq], axis=-1)
    cos = np.cos(ang)
    sin = np.sin(ang)
    cos_t = np.tile(np.concatenate([cos, cos], axis=-1), (1, RET_HEADS))
    sin_t = np.tile(np.concatenate([-sin, sin], axis=-1), (1, RET_HEADS))
    half = RET_DIM // 2
    perm = np.arange(BRANCH_W) ^ half
    swap = np.zeros((BRANCH_W, BRANCH_W), np.float32)
    swap[perm, np.arange(BRANCH_W)] = 1.0
    return jnp.asarray(cos_t, F32), jnp.asarray(sin_t, F32), jnp.asarray(swap, BF16)


def _ret_chunk(q, k, v, s, qd, kd, cd, bmask, dm, hmask, with_intra):
    cross = _dot((q * qd).astype(BF16), s.astype(BF16))
    s_new = cd * s + bmask * _dot_tn((k * kd).astype(BF16), v)
    if not with_intra:
        return cross, s_new
    qb = q.astype(BF16)
    kb = k.astype(BF16)
    qs = jnp.concatenate([qb * hmask[h:h + 1].astype(BF16) for h in range(RET_HEADS)], axis=0)
    scores = _dot_nt(qs, kb) * dm
    ov = _dot(scores.astype(BF16), v)
    c = q.shape[0]
    inner = ov[0:c] * hmask[0:1]
    for h in range(1, RET_HEADS):
        inner = inner + ov[h * c:(h + 1) * c] * hmask[h:h + 1]
    return inner + cross, s_new


def _ret_kernel(q_ref, k_ref, v_ref, qc_ref, kc_ref, vc_ref, cos_ref, sin_ref, swap_ref,
                qd_ref, kd_ref, cd_ref, bm_ref, dm_ref, hm_ref, o_ref, oc_ref, s_scr, *, n_chunks, n_ctx_chunks):
    dirn = pl.program_id(1)
    i = pl.program_id(2)
    c = RET_CHUNK
    k_scale = RET_DIM ** -0.5
    bmask = bm_ref[...]
    dm = dm_ref[...]
    hmask = hm_ref[...]

    def run(d):
        qd = qd_ref[d]
        kd = kd_ref[d]
        cd = cd_ref[d]
        intra = d == 0

        @pl.when(i == 0)
        def _():
            s = jnp.zeros((BRANCH_W, BRANCH_W), F32)
            order = range(n_ctx_chunks) if d == 0 else range(n_ctx_chunks - 1, -1, -1)
            for cc in order:
                sl = slice(cc * c, (cc + 1) * c)
                o, s = _ret_chunk(qc_ref[sl, :].astype(F32), kc_ref[sl, :].astype(F32) * k_scale, vc_ref[sl, :],
                                  s, qd, kd, cd, bmask, dm, hmask, intra)
                oc_ref[sl, :] = o
            s_scr[...] = s

        swap = swap_ref[...]
        order = range(n_chunks) if d == 0 else range(n_chunks - 1, -1, -1)
        s = s_scr[...]
        for cc in order:
            sl = slice(cc * c, (cc + 1) * c)
            cos = cos_ref[sl, :]
            sin = sin_ref[sl, :]
            qb = q_ref[sl, :]
            kb = k_ref[sl, :]
            q = qb.astype(F32) * cos + _dot(qb, swap) * sin
            k = (kb.astype(F32) * cos + _dot(kb, swap) * sin) * k_scale
            o, s = _ret_chunk(q, k, v_ref[sl, :], s, qd, kd, cd, bmask, dm, hmask, intra)
            o_ref[sl, :] = o
        s_scr[...] = s

    @pl.when(dirn == 0)
    def _():
        run(0)

    @pl.when(dirn == 1)
    def _():
        run(1)


def _retention(proj_l, proj_c, tabs, rope, batch, seq_len, ctx_len):
    n_chunks = 4
    blk = n_chunks * RET_CHUNK
    nblk = seq_len // blk
    cos_t, sin_t, swap = rope

    def pos(d, i):
        return i + d * (nblk - 1 - 2 * i)

    def lat(col):
        return pl.BlockSpec((blk, BRANCH_W), lambda b, d, i: (b * nblk + pos(d, i), col))

    def ctx(col):
        return pl.BlockSpec((ctx_len, BRANCH_W), lambda b, d, i: (b, col))

    def const(shape):
        return pl.BlockSpec(shape, lambda b, d, i: (0,) * len(shape))

    tab_spec = pl.BlockSpec((blk, BRANCH_W), lambda b, d, i: (pos(d, i), 0))
    kern = functools.partial(_ret_kernel, n_chunks=n_chunks, n_ctx_chunks=ctx_len // RET_CHUNK)
    c = RET_CHUNK
    o, oc = pl.pallas_call(
        kern,
        grid=(batch, 2, nblk),
        in_specs=[
            lat(COL_RQ), lat(COL_RK), lat(COL_RV), ctx(COL_RQ), ctx(COL_RK), ctx(COL_RV),
            tab_spec, tab_spec, const((BRANCH_W, BRANCH_W)),
            const((2, c, BRANCH_W)), const((2, c, BRANCH_W)), const((2, BRANCH_W, BRANCH_W)),
            const((BRANCH_W, BRANCH_W)), const((RET_HEADS * c, c)), const((RET_HEADS, BRANCH_W)),
        ],
        out_specs=[
            pl.BlockSpec((None, blk, BRANCH_W), lambda b, d, i: (d, b * nblk + pos(d, i), 0)),
            pl.BlockSpec((None, ctx_len, BRANCH_W), lambda b, d, i: (d, b, 0)),
        ],
        out_shape=[
            jax.ShapeDtypeStruct((2, batch * seq_len, BRANCH_W), F32),
            jax.ShapeDtypeStruct((2, batch * ctx_len, BRANCH_W), F32),
        ],
        scratch_shapes=[pltpu.VMEM((BRANCH_W, BRANCH_W), F32)],
        compiler_params=_cparams("arbitrary", "arbitrary", "arbitrary"),
        name="retention",
    )(proj_l, proj_l, proj_l, proj_c, proj_c, proj_c, cos_t, sin_t, swap,
      tabs['qd'], tabs['kd'], tabs['cd'], tabs['bmask'], tabs['dm'], tabs['hmask'])
    return o, oc


def _na_tables(rpb):
    kr, kw = NA_WIN_ROWS, NA_WIN_COLS
    col = np.arange(GRID_W)
    col_start = np.clip(col - kw // 2, 0, GRID_W - kw)
    in_win = (col[None, :] >= col_start[:, None]) & (col[None, :] < col_start[:, None] + kw)
    dc = np.clip(col[None, :] - col[:, None], -(kw - 1), kw - 1) + (kw - 1)
    var = np.arange(kr)[:, None] + np.arange(kr)[None, :]
    pick_r = (var[:, :, None] == np.arange(2 * kr - 1)[None, None, :]).astype(np.float32)
    pick_c = (dc[:, :, None] == np.arange(2 * kw - 1)[None, None, :]).astype(np.float32)
    bias = jnp.einsum('vir,hrc,qkc->vhqik', jnp.asarray(pick_r), rpb.astype(F32), jnp.asarray(pick_c),
                      precision=lax.Precision.HIGHEST)
    bias = jnp.where(jnp.asarray(in_win)[None, None, :, None, :], bias, NEG_BIG)
    bias = bias.reshape(kr, NA_HEADS * GRID_W, kr * GRID_W)
    lane_h = np.repeat(np.arange(NA_HEADS), NA_DIM)
    hmask = (np.arange(NA_HEADS)[:, None] == lane_h[None, :]).astype(np.float32)
    return bias, jnp.asarray(hmask, F32)


def _attend(qs, keys, vals, bias, kc, vc):
    s_ctx = _dot_nt(qs, kc)
    m = jnp.max(s_ctx, axis=-1, keepdims=True)
    if keys is not None:
        s_band = _dot_nt(qs, keys) + bias
        m = jnp.maximum(m, jnp.max(s_band, axis=-1, keepdims=True))
        p_band = jnp.exp(s_band - m)
    p_ctx = jnp.exp(s_ctx - m)
    l = jnp.sum(p_ctx, axis=-1, keepdims=True)
    o = _dot(p_ctx.astype(BF16), vc)
    if keys is not None:
        l = l + jnp.sum(p_band, axis=-1, keepdims=True)
        o = o + _dot(p_band.astype(BF16), vals)
    return o / l


def _stack_heads(q, hmask_scaled):
    return jnp.concatenate([q * hmask_scaled[h:h + 1] for h in range(NA_HEADS)], axis=0)


def _unstack_heads(o, hmask, n):
    out = o[0:n] * hmask[0:1]
    for h in range(1, NA_HEADS):
        out = out + o[h * n:(h + 1) * n] * hmask[h:h + 1]
    return out


def _na_kernel(q_ref, k_ref, v_ref, kc_ref, vc_ref, bias_ref, hm_ref, o_ref, *, n_grid_rows):
    i = pl.program_id(1)
    hmask = hm_ref[...]
    hms = (hmask * (NA_DIM ** -0.5)).astype(BF16)
    kc = kc_ref[...]
    vc = vc_ref[...]
    band = NA_WIN_ROWS * GRID_W
    for rr in range(NA_QROWS):
        r = i * NA_QROWS + rr
        rs = jnp.clip(r - NA_WIN_ROWS // 2, 0, n_grid_rows - NA_WIN_ROWS)
        var = rs - r + (NA_WIN_ROWS - 1)
        start = pl.multiple_of(rs * GRID_W, GRID_W)
        keys = k_ref[pl.ds(start, band), :]
        vals = v_ref[pl.ds(start, band), :]
        qs = _stack_heads(q_ref[rr * GRID_W:(rr + 1) * GRID_W, :], hms)
        o = _attend(qs, keys, vals, bias_ref[var], kc, vc)
        o_ref[rr * GRID_W:(rr + 1) * GRID_W, :] = _unstack_heads(o, hmask, GRID_W).astype(BF16)


def _na_ctx_kernel(q_ref, kc_ref, vc_ref, hm_ref, o_ref):
    hmask = hm_ref[...]
    hms = (hmask * (NA_DIM ** -0.5)).astype(BF16)
    n = q_ref.shape[0]
    o = _attend(_stack_heads(q_ref[...], hms), None, None, None, kc_ref[...], vc_ref[...])
    o_ref[...] = _unstack_heads(o, hmask, n).astype(BF16)


def _neighborhood(proj_l, proj_c, bias, hmask, batch, seq_len, ctx_len, need_ctx_out):
    rows = seq_len // GRID_W
    qblk = NA_QROWS * GRID_W
    nq = seq_len // qblk
    out_l = pl.pallas_call(
        functools.partial(_na_kernel, n_grid_rows=rows),
        grid=(batch, nq),
        in_specs=[
            pl.BlockSpec((qblk, BRANCH_W), lambda b, i: (b * nq + i, COL_NQ)),
            pl.BlockSpec((seq_len, BRANCH_W), lambda b, i: (b, COL_NK)),
            pl.BlockSpec((seq_len, BRANCH_W), lambda b, i: (b, COL_NV)),
            pl.BlockSpec((ctx_len, BRANCH_W), lambda b, i: (b, COL_NK)),
            pl.BlockSpec((ctx_len, BRANCH_W), lambda b, i: (b, COL_NV)),
            pl.BlockSpec(bias.shape, lambda b, i: (0, 0, 0)),
            pl.BlockSpec(hmask.shape, lambda b, i: (0, 0)),
        ],
        out_specs=pl.BlockSpec((qblk, BRANCH_W), lambda b, i: (b * nq + i, 0)),
        out_shape=jax.ShapeDtypeStruct((batch * seq_len, BRANCH_W), BF16),
        compiler_params=_cparams("arbitrary", "arbitrary"),
        name="neighborhood_attn",
    )(proj_l, proj_l, proj_l, proj_c, proj_c, bias, hmask)
    out_c = None
    if need_ctx_out:
        out_c = pl.pallas_call(
            _na_ctx_kernel,
            grid=(batch,),
            in_specs=[
                pl.BlockSpec((ctx_len, BRANCH_W), lambda b: (b, COL_NQ)),
                pl.BlockSpec((ctx_len, BRANCH_W), lambda b: (b, COL_NK)),
                pl.BlockSpec((ctx_len, BRANCH_W), lambda b: (b, COL_NV)),
                pl.BlockSpec(hmask.shape, lambda b: (0, 0)),
            ],
            out_specs=pl.BlockSpec((ctx_len, BRANCH_W), lambda b: (b, 0)),
            out_shape=jax.ShapeDtypeStruct((batch * ctx_len, BRANCH_W), BF16),
            compiler_params=_cparams("arbitrary"),
            name="context_attn",
        )(proj_c, proj_c, proj_c, hmask)
    return out_l, out_c


def _merge_kernel(x_ref, mod_ref, g_ref, gt0, gt1, gt2, gt3, a_ref, s5_ref, ro_ref, rg_ref, na_ref,
                  wglu_ref, bglu_ref, gn_ref, avg_ref, wb_ref, wo_ref, o_ref, *, tiles_per_mod, mod_base):
    i = pl.program_id(0)
    _, _, gate_a = _mod_rows(mod_ref, i, tiles_per_mod, mod_base, 0)
    z = _gelu_tanh(s5_ref[...]).astype(BF16)
    zf = z.astype(F32)
    b_s5 = (zf * _sigmoid(_dot(z, wglu_ref[...]) + bglu_ref[...])).astype(BF16)
    o = ro_ref[0] + ro_ref[1]
    avg = avg_ref[...]
    hi, lo = _split_bf16(o)
    mu = _dot(hi, avg) + _dot(lo, avg)
    dlt = o - mu
    hi, lo = _split_bf16(dlt * dlt)
    var = _dot(hi, avg) + _dot(lo, avg)
    hn = dlt * lax.rsqrt(var + EPS) * gn_ref[...]
    b_ret = (_silu(rg_ref[...].astype(F32)) * hn).astype(BF16)
    outs = (a_ref[...], b_s5, b_ret, na_ref[...])
    gates = (gt0, gt1, gt2, gt3)
    y = _sigmoid(gates[0][...].astype(F32)) * _dot(outs[0], wb_ref[0])
    for b in range(1, N_BRANCH):
        y = y + _sigmoid(gates[b][...].astype(F32)) * _dot(outs[b], wb_ref[b])
    yo = _dot(y.astype(BF16), wo_ref[...])
    o_ref[...] = x_ref[...] + gate_a * _rms(yo, g_ref[...])


def _merge(x, mod, g1, proj, a, s5y, ret_o, na, lw, *, rows_per_mod, mod_base):
    rows, d = x.shape
    tm = min(512, rows)
    nt = rows // tm

    def row(shape, col=0):
        return pl.BlockSpec(shape, lambda i: (i, col))

    def const(arr):
        return pl.BlockSpec(arr.shape, lambda i: (0,) * arr.ndim)

    kern = functools.partial(_merge_kernel, tiles_per_mod=max(rows_per_mod // tm, 1), mod_base=mod_base)
    ins = [x, mod, g1.reshape(1, d), proj, proj, proj, proj, a, s5y, ret_o, proj, na,
           lw['w_glu'], lw['b_glu'], lw['ret_gn'], lw['avg'], lw['w_branch'], lw['w_out']]
    specs = [
        row((tm, d)), const(mod), pl.BlockSpec((1, d), lambda i: (0, 0)),
        row((tm, d), 0), row((tm, d), 1), row((tm, d), 2), row((tm, d), 3),
        row((tm, BRANCH_W)), row((tm, BRANCH_W)),
        pl.BlockSpec((2, tm, BRANCH_W), lambda i: (0, i, 0)),
        row((tm, BRANCH_W), COL_RG), row((tm, BRANCH_W)),
        const(lw['w_glu']), const(lw['b_glu']), const(lw['ret_gn']), const(lw['avg']),
        const(lw['w_branch']), const(lw['w_out']),
    ]
    return pl.pallas_call(
        kern,
        grid=(nt,),
        in_specs=specs,
        out_specs=row((tm, d)),
        out_shape=jax.ShapeDtypeStruct((rows, d), F32),
        compiler_params=_cparams("arbitrary"),
        name="merge_out",
    )(*ins)


def _ffn_kernel(x_ref, mod_ref, g2_ref, g3_ref, wg_ref, wu_ref, wd_ref, o_ref, h_scr, acc_scr,
                *, tiles_per_mod, mod_base, n_f):
    i = pl.program_id(0)
    f = pl.program_id(1)

    @pl.when(f == 0)
    def _():
        sh, sc, _ = _mod_rows(mod_ref, i, tiles_per_mod, mod_base, 3)
        h_scr[...] = (_rms(x_ref[...], g2_ref[...]) * (1.0 + sc) + sh).astype(BF16)
        acc_scr[...] = jnp.zeros_like(acc_scr)

    h = h_scr[...]
    act = (_silu(_dot(h, wg_ref[...])) * _dot(h, wu_ref[...])).astype(BF16)
    acc_scr[...] += _dot(act, wd_ref[...])

    @pl.when(f == n_f - 1)
    def _():
        _, _, gate_f = _mod_rows(mod_ref, i, tiles_per_mod, mod_base, 3)
        o_ref[...] = x_ref[...] + gate_f * _rms(acc_scr[...], g3_ref[...])


def _ffn_dense(x, mod, g2, g3, wg, wu, wd, *, rows_per_mod, mod_base):
    rows, d = x.shape
    d_ff = wg.shape[1]
    tm = min(512, rows)
    tf = d_ff // 2 if (d_ff // 2) % 128 == 0 else d_ff
    n_f = d_ff // tf
    kern = functools.partial(_ffn_kernel, tiles_per_mod=max(rows_per_mod // tm, 1), mod_base=mod_base, n_f=n_f)
    return pl.pallas_call(
        kern,
        grid=(rows // tm, n_f),
        in_specs=[
            pl.BlockSpec((tm, d), lambda i, f: (i, 0)),
            pl.BlockSpec(mod.shape, lambda i, f: (0, 0)),
            pl.BlockSpec((1, d), lambda i, f: (0, 0)),
            pl.BlockSpec((1, d), lambda i, f: (0, 0)),
            pl.BlockSpec((d, tf), lambda i, f: (0, f)),
            pl.BlockSpec((d, tf), lambda i, f: (0, f)),
            pl.BlockSpec((tf, d), lambda i, f: (f, 0)),
        ],
        out_specs=pl.BlockSpec((tm, d), lambda i, f: (i, 0)),
        out_shape=jax.ShapeDtypeStruct((rows, d), F32),
        scratch_shapes=[pltpu.VMEM((tm, d), BF16), pltpu.VMEM((tm, d), F32)],
        compiler_params=_cparams("arbitrary", "arbitrary"),
        name="ffn_dense",
    )(x, mod, g2.reshape(1, d), g3.reshape(1, d), wg, wu, wd)


def _router_kernel(x_ref, mod_ref, g2_ref, wr_ref, br_ref, h_ref, comb_ref, *, tiles_per_mod, mod_base):
    i = pl.program_id(0)
    sh, sc, _ = _mod_rows(mod_ref, i, tiles_per_mod, mod_base, 3)
    h = _rms(x_ref[...], g2_ref[...]) * (1.0 + sc) + sh
    h_ref[...] = _pack_pairs(h)
    h_hi, h_lo = _split_bf16(h)
    w_hi, w_lo = _split_bf16(wr_ref[...])
    logits = _dot(h_hi, w_hi) + _dot(h_lo, w_hi) + _dot(h_hi, w_lo) + br_ref[...]
    lane = lax.broadcasted_iota(jnp.int32, logits.shape, 1)
    v1 = jnp.max(logits, axis=-1, keepdims=True)
    i1 = jnp.min(jnp.where(logits == v1, lane, 128), axis=-1, keepdims=True)
    rest = jnp.where(lane == i1, NEG_BIG, logits)
    v2 = jnp.max(rest, axis=-1, keepdims=True)
    i2 = jnp.min(jnp.where(rest == v2, lane, 128), axis=-1, keepdims=True)
    e = jnp.exp(v2 - v1)
    w1 = 1.0 / (1.0 + e)
    w2 = e / (1.0 + e)
    meta = jnp.where(lane == 0, i1.astype(F32), 0.0) + jnp.where(lane == 1, i2.astype(F32), 0.0)
    comb_ref[...] = meta + jnp.where(lane == 2, w1, 0.0) + jnp.where(lane == 3, w2, 0.0)


def _router(x, mod, g2, w_router, b_router, *, rows_per_mod, mod_base):
    rows, d = x.shape
    tm = min(512, rows)
    wr = jnp.zeros((d, 128), F32).at[:, :N_EXPERTS].set(w_router)
    br = jnp.full((1, 128), NEG_BIG, F32).at[0, :N_EXPERTS].set(b_router)
    kern = functools.partial(_router_kernel, tiles_per_mod=max(rows_per_mod // tm, 1), mod_base=mod_base)
    return pl.pallas_call(
        kern,
        grid=(rows // tm,),
        in_specs=[
            pl.BlockSpec((tm, d), lambda i: (i, 0)),
            pl.BlockSpec(mod.shape, lambda i: (0, 0)),
            pl.BlockSpec((1, d), lambda i: (0, 0)),
            pl.BlockSpec((d, 128), lambda i: (0, 0)),
            pl.BlockSpec((1, 128), lambda i: (0, 0)),
        ],
        out_specs=[pl.BlockSpec((tm, d // 2), lambda i: (i, 0)), pl.BlockSpec((tm, 128), lambda i: (i, 0))],
        out_shape=[jax.ShapeDtypeStruct((rows, d // 2), jnp.int32), jax.ShapeDtypeStruct((rows, 128), F32)],
        compiler_params=_cparams("arbitrary"),
        name="moe_router",
    )(x, mod, g2.reshape(1, d), wr, br)


def _sc_gather(table, idx):
    n_idx = idx.shape[0]
    width = table.shape[1]
    per_worker = n_idx // SC_WORKERS
    chunk_rows = math.gcd(per_worker, SC_GATHER_ROWS)
    n_chunks = per_worker // chunk_rows
    assert per_worker * SC_WORKERS == n_idx and chunk_rows % 8 == 0
    mesh = plsc.VectorSubcoreMesh(core_axis_name="c", subcore_axis_name="s")

    @functools.partial(
        pl.kernel, mesh=mesh,
        out_type=jax.ShapeDtypeStruct((n_idx, width), table.dtype),
        scratch_types=[
            pltpu.VMEM((chunk_rows,), jnp.int32),
            pltpu.VMEM((chunk_rows, width), table.dtype),
            pltpu.SemaphoreType.DMA,
        ],
        name="sc_row_gather",
    )
    def gather(table_hbm, idx_hbm, out_hbm, idx_v, rows_v, sem):
        wid = lax.axis_index("s") * SC_CORES + lax.axis_index("c")
        base = wid * per_worker

        @pl.loop(0, n_chunks)
        def _(j):
            off = base + j * chunk_rows
            pltpu.sync_copy(idx_hbm.at[pl.ds(off, chunk_rows)], idx_v)
            pltpu.async_copy(table_hbm.at[idx_v], rows_v, sem).wait()
            pltpu.sync_copy(rows_v, out_hbm.at[pl.ds(off, chunk_rows)])

    return gather(table, idx)


def _sc_scatter(table, idx, n_out):
    n_idx = idx.shape[0]
    rows, width = table.shape
    per_worker = n_idx // SC_WORKERS
    chunk_rows = math.gcd(per_worker, SC_GATHER_ROWS)
    n_chunks = per_worker // chunk_rows
    assert per_worker * SC_WORKERS == n_idx and chunk_rows % 8 == 0 and rows % per_worker == 0
    mesh = plsc.VectorSubcoreMesh(core_axis_name="c", subcore_axis_name="s")

    @functools.partial(
        pl.kernel, mesh=mesh,
        out_type=jax.ShapeDtypeStruct((n_out, width), table.dtype),
        scratch_types=[
            pltpu.VMEM((chunk_rows,), jnp.int32),
            pltpu.VMEM((chunk_rows, width), table.dtype),
            pltpu.SemaphoreType.DMA,
        ],
        name="sc_row_scatter",
    )
    def scatter(table_hbm, idx_hbm, out_hbm, idx_v, rows_v, sem):
        wid = lax.axis_index("s") * SC_CORES + lax.axis_index("c")
        base = wid * per_worker

        @pl.loop(0, n_chunks)
        def _(j):
            off = base + j * chunk_rows
            pltpu.sync_copy(idx_hbm.at[pl.ds(off, chunk_rows)], idx_v)
            pltpu.sync_copy(table_hbm.at[pl.ds(lax.rem(off, rows), chunk_rows)], rows_v)
            pltpu.async_copy(rows_v, out_hbm.at[idx_v], sem).wait()

    return scatter(table, idx)


def _moe_plan(meta, rows):
    tile = MOE_ROW_TILE
    n_tiles = (2 * rows) // tile + N_EXPERTS
    n_slots = n_tiles * tile
    experts = jnp.concatenate([meta[:, 0], meta[:, 1]]).astype(jnp.int32)
    onehot = (experts[:, None] == jnp.arange(N_EXPERTS)[None, :]).astype(jnp.int32)
    csum = jnp.cumsum(onehot, axis=0)
    counts = csum[-1]
    rank = jnp.sum(onehot * csum, axis=1) - 1
    padded = ((counts + tile - 1) // tile) * tile
    ends = jnp.cumsum(padded)
    starts = ends - padded
    pos = jnp.sum(onehot * starts[None, :], axis=1) + rank
    tile_start = jnp.arange(n_tiles, dtype=jnp.int32) * tile
    used = tile_start < ends[-1]
    tile_e = jnp.minimum(jnp.sum((tile_start[:, None] >= ends[None, :]).astype(jnp.int32), axis=1), N_EXPERTS - 1)
    last_e = jnp.max(jnp.where(used, tile_e, 0))
    tile_e = jnp.where(used, tile_e, last_e)
    valid_end = jnp.sum((tile_e[:, None] == jnp.arange(N_EXPERTS)[None, :]) * (starts + counts)[None, :], axis=1)
    n_valid = jnp.where(used, jnp.clip(valid_end - tile_start, 0, tile), 0).astype(jnp.int32)
    return pos.astype(jnp.int32), n_slots, tile_e.astype(jnp.int32), n_valid


def _moe_group_kernel(eid_ref, nval_ref, hs_ref, wg_ref, wu_ref, wd_ref, y_ref, h_scr, acc_scr, *, n_f):
    w = pl.program_id(0)
    f = pl.program_id(1)
    nv = nval_ref[w]

    @pl.when(nv > 0)
    def _():
        @pl.when(f == 0)
        def _():
            hv = _unpack_pairs(hs_ref[...])
            row = lax.broadcasted_iota(jnp.int32, hv.shape, 0)
            h_scr[...] = jnp.where(row < nv, hv, 0.0).astype(BF16)

        h = h_scr[...]
        gate = _dot(h, wg_ref[...].astype(BF16))
        up = _dot(h, wu_ref[...].astype(BF16))
        part = _dot((_silu(gate) * up).astype(BF16), wd_ref[...].astype(BF16))

        @pl.when(f == 0)
        def _():
            acc_scr[...] = part

        @pl.when(f > 0)
        def _():
            acc_scr[...] += part

        @pl.when(f == n_f - 1)
        def _():
            y_ref[...] = _pack_pairs(acc_scr[...])


def _moe_grouped(hs, tile_e, n_valid, wg, wu, wd):
    n_slots = hs.shape[0]
    d = wg.shape[1]
    d_ff = wg.shape[2]
    tile = MOE_ROW_TILE
    tf = MOE_FF_TILE
    n_f = d_ff // tf

    def f_idx(f, nval, w):
        return jnp.where(nval[w] > 0, f, n_f - 1)

    grid_spec = pltpu.PrefetchScalarGridSpec(
        num_scalar_prefetch=2,
        grid=(n_slots // tile, n_f),
        in_specs=[
            pl.BlockSpec((tile, d // 2), lambda w, f, eid, nval: (w, 0)),
            pl.BlockSpec((None, d, tf), lambda w, f, eid, nval: (eid[w], 0, f_idx(f, nval, w))),
            pl.BlockSpec((None, d, tf), lambda w, f, eid, nval: (eid[w], 0, f_idx(f, nval, w))),
            pl.BlockSpec((None, tf, d), lambda w, f, eid, nval: (eid[w], f_idx(f, nval, w), 0)),
        ],
        out_specs=pl.BlockSpec((tile, d // 2), lambda w, f, eid, nval: (w, 0)),
        scratch_shapes=[pltpu.VMEM((tile, d), BF16), pltpu.VMEM((tile, d), F32)],
    )
    return pl.pallas_call(
        functools.partial(_moe_group_kernel, n_f=n_f),
        grid_spec=grid_spec,
        out_shape=jax.ShapeDtypeStruct((n_slots, d // 2), jnp.int32),
        compiler_params=_cparams("arbitrary", "arbitrary"),
        name="moe_experts",
    )(tile_e, n_valid, hs, wg, wu, wd)


def _moe_out_kernel(x_ref, y1_ref, y2_ref, meta_ref, mod_ref, g3_ref, o_ref, *, tiles_per_mod, mod_base):
    i = pl.program_id(0)
    _, _, gate_f = _mod_rows(mod_ref, i, tiles_per_mod, mod_base, 3)
    meta = meta_ref[...]
    y = meta[:, 2:3] * _unpack_pairs(y1_ref[...]) + meta[:, 3:4] * _unpack_pairs(y2_ref[...])
    o_ref[...] = x_ref[...] + gate_f * _rms(y, g3_ref[...])


def _moe_combine(x, yg, meta, mod, g3, *, rows_per_mod, mod_base):
    rows, d = x.shape
    tm = min(512, rows)
    nt = rows // tm
    kern = functools.partial(_moe_out_kernel, tiles_per_mod=max(rows_per_mod // tm, 1), mod_base=mod_base)
    return pl.pallas_call(
        kern,
        grid=(nt,),
        in_specs=[
            pl.BlockSpec((tm, d), lambda i: (i, 0)),
            pl.BlockSpec((tm, d // 2), lambda i: (i, 0)),
            pl.BlockSpec((tm, d // 2), lambda i: (nt + i, 0)),
            pl.BlockSpec((tm, 128), lambda i: (i, 0)),
            pl.BlockSpec(mod.shape, lambda i: (0, 0)),
            pl.BlockSpec((1, d), lambda i: (0, 0)),
        ],
        out_specs=pl.BlockSpec((tm, d), lambda i: (i, 0)),
        out_shape=jax.ShapeDtypeStruct((rows, d), F32),
        compiler_params=_cparams("arbitrary"),
        name="moe_combine",
    )(x, yg, yg, meta, mod, g3.reshape(1, d))


def _moe_sparse(x, h, meta, mod, g3, wg, wu, wd, *, rows_per_mod, mod_base):
    rows = x.shape[0]
    pos, n_slots, tile_e, n_valid = _moe_plan(meta, rows)
    hs = _sc_scatter(h, pos, n_slots)
    ys = _moe_grouped(hs, tile_e, n_valid, wg, wu, wd)
    yg = _sc_gather(ys, pos)
    return _moe_combine(x, yg, meta, mod, g3, rows_per_mod=rows_per_mod, mod_base=mod_base)


def _cast_kernel(w_ref, o_ref):
    o_ref[...] = w_ref[...].astype(BF16)


def _cast_bf16(w):
    w3 = w if w.ndim == 3 else w[None]
    n_e, k, n = w3.shape
    bk = min(k, 256)
    out = pl.pallas_call(
        _cast_kernel,
        grid=(n_e, k // bk),
        in_specs=[pl.BlockSpec((None, bk, n), lambda e, i: (e, i, 0))],
        out_specs=pl.BlockSpec((None, bk, n), lambda e, i: (e, i, 0)),
        out_shape=jax.ShapeDtypeStruct(w3.shape, BF16),
        compiler_params=_cparams("arbitrary", "arbitrary"),
        name="cast_weights",
    )(w3)
    return out if w.ndim == 3 else out[0]


def _permute_w_in(w_in):
    k, n = w_in.shape
    n_blocks = n // BRANCH_W
    shift = 9
    return pl.pallas_call(
        _cast_kernel,
        grid=(n_blocks,),
        in_specs=[pl.BlockSpec((k, BRANCH_W), lambda j: (0, (j + shift) % n_blocks))],
        out_specs=pl.BlockSpec((k, BRANCH_W), lambda j: (0, j)),
        out_shape=jax.ShapeDtypeStruct((k, n), BF16),
        compiler_params=_cparams("arbitrary"),
        name="cast_permute_w_in",
    )(w_in)


def kernel(x, c, ctx, c_ctx, w_mod, b_mod, norm_g, w_in, s5_a_re, s5_a_im, s5_log_dt, s5_b_re, s5_b_im, s5_c_re, s5_c_im, s5_d, s5_w_glu, s5_b_glu, ret_decay, ret_gn, na_rpb, w_branch, w_out, ffn_w_gate, ffn_w_up, ffn_w_down, moe_w_router, moe_b_router, moe_w_gate, moe_w_up, moe_w_down):
    batch, seq_len, d = x.shape
    ctx_len = ctx.shape[1]
    depth = w_mod.shape[0]
    cond = jnp.concatenate([c, c_ctx[None, :]], axis=0)
    mod_all = _modulation(cond, w_mod, b_mod)
    rope = _rope_tables(seq_len)
    lane_h = np.repeat(np.arange(RET_HEADS), RET_DIM)
    avg = jnp.asarray((lane_h[:, None] == lane_h[None, :]).astype(np.float32) / RET_DIM, BF16)

    xl = x.reshape(batch * seq_len, d)
    xc = ctx.reshape(batch * ctx_len, d)
    lat = dict(rows_per_mod=seq_len, mod_base=0)
    cxt = dict(rows_per_mod=batch * ctx_len, mod_base=batch)

    for layer in range(depth):
        last = layer == depth - 1
        need_ctx = not last
        mod = mod_all[layer]
        ng = norm_g[layer]
        w_in_bf = _permute_w_in(w_in[layer])
        s5_tabs = _s5_tables(s5_a_re[layer], s5_a_im[layer], s5_log_dt[layer], s5_b_re[layer], s5_b_im[layer],
                             s5_c_re[layer], s5_c_im[layer], s5_d[layer], batch)
        ret_tabs = _ret_tables(ret_decay[layer])
        na_bias, na_hmask = _na_tables(na_rpb[layer])
        lw = dict(w_glu=s5_w_glu[layer].astype(BF16), b_glu=s5_b_glu[layer].reshape(1, BRANCH_W).astype(F32),
                  ret_gn=ret_gn[layer].reshape(1, BRANCH_W).astype(F32), avg=avg,
                  w_branch=_cast_bf16(w_branch[layer]), w_out=_cast_bf16(w_out[layer]))

        proj_l, f_l = _in_proj(xl, mod, ng[0], w_in_bf, **lat)
        proj_c, f_c = _in_proj(xc, mod, ng[0], w_in_bf, **cxt)

        a_l = _fourier_latent(f_l, batch, seq_len)
        s_l, s_c = _s5_mixer(proj_l[:, COL_S * BRANCH_W:(COL_S + 1) * BRANCH_W],
                             proj_c[:, COL_S * BRANCH_W:(COL_S + 1) * BRANCH_W], s5_tabs, batch)
        r_l, r_c = _retention(proj_l, proj_c, ret_tabs, rope, batch, seq_len, ctx_len)
        n_l, n_c = _neighborhood(proj_l, proj_c, na_bias, na_hmask, batch, seq_len, ctx_len, need_ctx)

        xl = _merge(xl, mod, ng[1], proj_l, a_l, s_l, r_l, n_l, lw, **lat)
        if need_ctx:
            a_c = _fourier_ctx(f_c, batch, ctx_len)
            xc = _merge(xc, mod, ng[1], proj_c, a_c, s_c, r_c, n_c, lw, **cxt)

        i = layer // 2
        if layer % 2 == 0:
            wg, wu, wd = _cast_bf16(ffn_w_gate[i]), _cast_bf16(ffn_w_up[i]), _cast_bf16(ffn_w_down[i])
            xl = _ffn_dense(xl, mod, ng[2], ng[3], wg, wu, wd, **lat)
            if need_ctx:
                xc = _ffn_dense(xc, mod, ng[2], ng[3], wg, wu, wd, **cxt)
        else:
            wg, wu, wd = moe_w_gate[i], moe_w_up[i], moe_w_down[i]
            h, meta = _router(xl, mod, ng[2], moe_w_router[i], moe_b_router[i], **lat)
            xl = _moe_sparse(xl, h, meta, mod, ng[3], wg, wu, wd, **lat)
            if need_ctx:
                hc, metac = _router(xc, mod, ng[2], moe_w_router[i], moe_b_router[i], **cxt)
                xc = _moe_sparse(xc, hc, metac, mod, ng[3], wg, wu, wd, **cxt)
    return xl.reshape(batch, seq_len, d)
```

```python
import functools
import math

import numpy as np
import jax
import jax.numpy as jnp
from jax import lax
from jax.experimental import pallas as pl
from jax.experimental.pallas import tpu as pltpu
from jax.experimental.pallas import tpu_sc as plsc

F32 = jnp.float32
BF16 = jnp.bfloat16

D_MODEL = 1024
BRANCH_W = 256
N_BRANCH = 4
GRID_W = 64
FNET_GROUP_DIM = 64
S5_GROUP_CH = 16
S5_GROUPS = 16
S5_STATE = 64
S5_CHUNK = 32
S5_PAIRS = S5_GROUPS // 2
RET_HEADS = 4
RET_DIM = 64
RET_CHUNK = 128
NA_HEADS = 4
NA_DIM = 64
NA_WIN_ROWS = 8
NA_WIN_COLS = 16
NA_QROWS = 8
ROPE_BASE = 10000.0
N_EXPERTS = 8
EPS = 1e-6
FFT_N2 = 256
NEG_BIG = -1e30
VMEM_LIMIT_BYTES = 50 * 1024 * 1024
SC_CORES = 2
SC_SUBCORES = 16
SC_WORKERS = SC_CORES * SC_SUBCORES
SC_GATHER_ROWS = 64
MOE_ROW_TILE = 1024
MOE_FF_TILE = 512

COL_F, COL_S, COL_RQ, COL_RK, COL_RV, COL_RG, COL_NQ, COL_NK, COL_NV = range(16, 25)
IN_W = 9 * BRANCH_W + N_BRANCH * D_MODEL
IN_TN = 1280
IN_F_TILE = (N_BRANCH * D_MODEL) // IN_TN
IN_F_OFF = N_BRANCH * D_MODEL - IN_F_TILE * IN_TN


def _cparams(*sem):
    return pltpu.CompilerParams(dimension_semantics=sem, vmem_limit_bytes=VMEM_LIMIT_BYTES)


def _sigmoid(v):
    return 0.5 * jnp.tanh(0.5 * v) + 0.5


def _silu(v):
    return v * _sigmoid(v)


def _gelu_tanh(v):
    return 0.5 * v * (1.0 + jnp.tanh(math.sqrt(2.0 / math.pi) * (v + 0.044715 * (v * v * v))))


def _rms(v, g):
    ms = jnp.mean(v * v, axis=-1, keepdims=True)
    return v * lax.rsqrt(ms + EPS) * g


def _split_bf16(v):
    hi = v.astype(BF16)
    lo = (v - hi.astype(F32)).astype(BF16)
    return hi, lo


def _pack_pairs(v):
    n = v.shape[1] // 2
    lo = lax.bitcast_convert_type(v[:, :n].astype(BF16).astype(F32), jnp.int32)
    hi = lax.bitcast_convert_type(v[:, n:].astype(BF16).astype(F32), jnp.int32)
    return (hi & -65536) | ((lo >> 16) & 65535)


def _unpack_pairs(w):
    lo = lax.bitcast_convert_type(w << 16, F32)
    hi = lax.bitcast_convert_type(w & -65536, F32)
    return jnp.concatenate([lo, hi], axis=-1)


def _dot(a, b):
    return jnp.dot(a, b, preferred_element_type=F32)


def _dot_nt(a, b):
    return lax.dot_general(a, b, (((1,), (1,)), ((), ())), preferred_element_type=F32)


def _dot_tn(a, b):
    return lax.dot_general(a, b, (((0,), (0,)), ((), ())), preferred_element_type=F32)


def _mod_kernel(ct_ref, w_ref, b_ref, o_ref):
    ct = ct_ref[...]
    s = _silu(ct)
    w = w_ref[...]
    rows = [jnp.sum(w * s[:, r:r + 1], axis=0, keepdims=True) for r in range(8)]
    o_ref[...] = jnp.concatenate(rows, axis=0) + b_ref[...]


def _modulation(cond, w_mod, b_mod):
    n_layers, d, n = w_mod.shape
    tn = 512
    ct = jnp.zeros((8, d), F32).at[:cond.shape[0]].set(cond).T
    return pl.pallas_call(
        _mod_kernel,
        grid=(n_layers, n // tn),
        in_specs=[
            pl.BlockSpec((d, 8), lambda l, j: (0, 0)),
            pl.BlockSpec((None, d, tn), lambda l, j: (l, 0, j)),
            pl.BlockSpec((None, 1, tn), lambda l, j: (l, 0, j)),
        ],
        out_specs=pl.BlockSpec((None, 8, tn), lambda l, j: (l, 0, j)),
        out_shape=jax.ShapeDtypeStruct((n_layers, 8, n), F32),
        compiler_params=_cparams("arbitrary", "arbitrary"),
        name="adaln_mod",
    )(ct, w_mod, b_mod.reshape(n_layers, 1, n))


def _mod_rows(mod_ref, i, tiles_per_mod, mod_base, first):
    r = mod_base + i // tiles_per_mod
    return [mod_ref[pl.ds(r, 1), (first + k) * D_MODEL:(first + k + 1) * D_MODEL] for k in range(3)]


def _in_kernel(x_ref, mod_ref, g_ref, w_ref, proj_ref, f_ref, h_scr, *, tiles_per_mod, mod_base):
    i = pl.program_id(0)
    j = pl.program_id(1)

    @pl.when(j == 0)
    def _():
        sh, sc, _ = _mod_rows(mod_ref, i, tiles_per_mod, mod_base, 0)
        h_scr[...] = (_rms(x_ref[...], g_ref[...]) * (1.0 + sc) + sh).astype(BF16)

    res = _dot(h_scr[...], w_ref[...])
    proj_ref[...] = res.astype(BF16)

    @pl.when(j == IN_F_TILE)
    def _():
        f_ref[...] = res[:, IN_F_OFF:IN_F_OFF + BRANCH_W].astype(BF16)


def _in_proj(x, mod, g, w_bf, *, rows_per_mod, mod_base):
    rows, d = x.shape
    tm = math.gcd(1024, rows_per_mod)
    kern = functools.partial(_in_kernel, tiles_per_mod=max(rows_per_mod // tm, 1), mod_base=mod_base)
    return pl.pallas_call(
        kern,
        grid=(rows // tm, IN_W // IN_TN),
        in_specs=[
            pl.BlockSpec((tm, d), lambda i, j: (i, 0)),
            pl.BlockSpec(mod.shape, lambda i, j: (0, 0)),
            pl.BlockSpec((1, d), lambda i, j: (0, 0)),
            pl.BlockSpec((d, IN_TN), lambda i, j: (0, j)),
        ],
        out_specs=[
            pl.BlockSpec((tm, IN_TN), lambda i, j: (i, j)),
            pl.BlockSpec((tm, BRANCH_W), lambda i, j: (i, 0)),
        ],
        out_shape=[
            jax.ShapeDtypeStruct((rows, IN_W), BF16),
            jax.ShapeDtypeStruct((rows, BRANCH_W), BF16),
        ],
        scratch_shapes=[pltpu.VMEM((tm, d), BF16)],
        compiler_params=_cparams("arbitrary", "arbitrary"),
        name="in_proj",
    )(x, mod, g.reshape(1, d), w_bf)


def _fft_a_kernel(x_ref, cs_ref, tc_ref, ts_ref, zr_ref, zi_ref, *, n1, n1p):
    y = _dot(cs_ref[...].astype(BF16), x_ref[...])
    yr = y[:n1]
    yi = y[n1p:n1p + n1]
    tc = tc_ref[...]
    ts = ts_ref[...]
    zr_ref[...] = (yr * tc + yi * ts).astype(BF16)
    zi_ref[...] = (yi * tc - yr * ts).astype(BF16)


def _fft_b_kernel(zr_ref, zi_ref, cs_ref, cc_ref, sc_ref, o_ref, *, kb, scale, has_imag):
    cs = cs_ref[...].astype(BF16)
    cc = cc_ref[...].astype(BF16)
    sc = sc_ref[...].astype(BF16)
    for kk in range(kb):
        a = _dot(cs, zr_ref[kk])
        if has_imag:
            b = _dot(cs, zi_ref[kk])
            xr = a[:FFT_N2] + b[FFT_N2:]
            xi = b[:FFT_N2] - a[FFT_N2:]
        else:
            xr = a[:FFT_N2]
            xi = -a[FFT_N2:]
        out = _dot(xr.astype(BF16), cc) + _dot(xi.astype(BF16), sc)
        o_ref[:, kk * BRANCH_W:(kk + 1) * BRANCH_W] = (out * scale).astype(BF16)


def _dft_tables(n):
    k = np.arange(n)
    ang = 2.0 * np.pi * ((k[:, None] * k[None, :]) % n) / n
    return np.cos(ang), np.sin(ang)


def _fft_b_call(zr, zi, n1, batch, seq_len, has_imag):
    c2, s2 = _dft_tables(FFT_N2)
    cs2 = jnp.asarray(np.concatenate([c2, s2], axis=0), F32)
    c64, s64 = _dft_tables(FNET_GROUP_DIM)
    eye = np.eye(BRANCH_W // FNET_GROUP_DIM)
    cc = jnp.asarray(np.kron(eye, c64), F32)
    sc = jnp.asarray(np.kron(eye, s64), F32)
    kb = min(8, n1)
    scale = 1.0 / math.sqrt(seq_len * FNET_GROUP_DIM)
    kern = functools.partial(_fft_b_kernel, kb=kb, scale=scale, has_imag=has_imag)
    zspec = pl.BlockSpec((None, kb, FFT_N2, BRANCH_W), lambda b, i: (b, i, 0, 0))
    out = pl.pallas_call(
        kern,
        grid=(batch, n1 // kb),
        in_specs=[
            zspec, zspec,
            pl.BlockSpec((2 * FFT_N2, FFT_N2), lambda b, i: (0, 0)),
            pl.BlockSpec((BRANCH_W, BRANCH_W), lambda b, i: (0, 0)),
            pl.BlockSpec((BRANCH_W, BRANCH_W), lambda b, i: (0, 0)),
        ],
        out_specs=pl.BlockSpec((None, FFT_N2, kb * BRANCH_W), lambda b, i: (b, 0, i)),
        out_shape=jax.ShapeDtypeStruct((batch, FFT_N2, n1 * BRANCH_W), BF16),
        compiler_params=_cparams("arbitrary", "arbitrary"),
        name="fourier_stage_b",
    )(zr, zi, cs2, cc, sc)
    return out.reshape(batch * seq_len, BRANCH_W)


def _fourier_latent(f, batch, seq_len):
    n1 = seq_len // FFT_N2
    wide = FFT_N2 * BRANCH_W
    c1, s1 = _dft_tables(n1)
    n1p = max(n1, 8)
    cs1 = np.zeros((2 * n1p, n1))
    cs1[:n1] = c1
    cs1[n1p:n1p + n1] = -s1
    k1 = np.arange(n1)[:, None]
    l2 = np.arange(FFT_N2)[None, :]
    tw = 2.0 * np.pi * (k1 * l2) / seq_len
    tc = jnp.asarray(np.repeat(np.cos(tw), BRANCH_W, axis=1), F32)
    ts = jnp.asarray(np.repeat(np.sin(tw), BRANCH_W, axis=1), F32)
    cw = min(8192, wide)
    xv = f.reshape(batch, n1, wide)
    spec = pl.BlockSpec((None, n1, cw), lambda b, j: (b, 0, j))
    tspec = pl.BlockSpec((n1, cw), lambda b, j: (0, j))
    zr, zi = pl.pallas_call(
        functools.partial(_fft_a_kernel, n1=n1, n1p=n1p),
        grid=(batch, wide // cw),
        in_specs=[spec, pl.BlockSpec((2 * n1p, n1), lambda b, j: (0, 0)), tspec, tspec],
        out_specs=[spec, spec],
        out_shape=[jax.ShapeDtypeStruct((batch, n1, wide), BF16)] * 2,
        compiler_params=_cparams("arbitrary", "arbitrary"),
        name="fourier_stage_a",
    )(xv, jnp.asarray(cs1, F32), tc, ts)
    zr = zr.reshape(batch, n1, FFT_N2, BRANCH_W)
    zi = zi.reshape(batch, n1, FFT_N2, BRANCH_W)
    return _fft_b_call(zr, zi, n1, batch, seq_len, True)


def _fourier_ctx(f, batch, ctx_len):
    assert ctx_len == FFT_N2
    z = f.reshape(batch, 1, FFT_N2, BRANCH_W)
    return _fft_b_call(z, z, 1, batch, ctx_len, False)


def _s5_tables(a_re, a_im, log_dt, b_re, b_im, c_re, c_im, d_skip, batch):
    t = S5_CHUNK
    g, p, hc = S5_GROUPS, S5_STATE, S5_GROUP_CH
    lam = lax.complex(a_re.astype(F32), a_im.astype(F32))
    dt = jnp.exp(log_dt.astype(F32))[..., None]
    ks = jnp.arange(t + 1, dtype=F32)
    apow = jnp.exp((lam * dt)[..., None] * ks)
    a_bar = apow[..., 1]
    b_bar = ((a_bar - 1.0) / lam)[..., None] * lax.complex(b_re.astype(F32), b_im.astype(F32))
    cm = lax.complex(c_re.astype(F32), c_im.astype(F32))
    kimp = jnp.real(jnp.einsum('dghp,dgpk,dgpj->dgkhj', cm, apow[..., :t], b_bar,
                               precision=lax.Precision.HIGHEST))
    kf, kb = kimp[0], kimp[1]
    kfull = jnp.concatenate([kb[:, :0:-1], kf[:, :1] + kb[:, :1], kf[:, 1:]], axis=1)
    strip = kfull.transpose(0, 3, 1, 2).reshape(g, hc, (2 * t - 1) * hc)
    strip = jnp.pad(strip, ((0, 0), (0, 0), (0, 2 * t * hc - strip.shape[-1])))
    strip = strip.reshape(S5_PAIRS, 2, hc, 2 * t * hc)

    def pair_blocks(kd, axis):
        r, c = kd.shape[1:]
        kp = kd.reshape(S5_PAIRS, 2, r, c)
        z = jnp.zeros_like(kp[:, 0])
        top = jnp.concatenate([kp[:, 0], z], axis=-1)
        bot = jnp.concatenate([z, kp[:, 1]], axis=-1)
        return jnp.concatenate([top, bot], axis=1)

    wf = jnp.einsum('gpj,gph->gjhp', apow[0][..., t - 1::-1][..., :t], b_bar[0])
    wb = jnp.einsum('gpj,gph->gjhp', apow[1][..., :t], b_bar[1])
    wf = wf.reshape(g, t * hc, p)
    wb = wb.reshape(g, t * hc, p)
    kinds = [jnp.real(wf), jnp.imag(wf), jnp.real(wb), jnp.imag(wb)]
    we = jnp.concatenate([pair_blocks(kd, 0) for kd in kinds], axis=-1).astype(BF16)

    vf = jnp.einsum('ghp,gpt->gpth', cm[0], apow[0][..., 1:t + 1])
    vb = jnp.einsum('ghp,gpt->gpth', cm[1], apow[1][..., t:0:-1])
    vf = vf.reshape(g, p, t * hc)
    vb = vb.reshape(g, p, t * hc)
    vkinds = [jnp.real(vf), -jnp.imag(vf), jnp.real(vb), -jnp.imag(vb)]
    v1 = jnp.concatenate([pair_blocks(kd, 0) for kd in vkinds], axis=1)
    v = jnp.concatenate([v1, v1], axis=1).astype(BF16)

    def lanes(z):
        return jnp.tile(z.reshape(1, g * p), (1, batch))

    at = apow[..., t]
    a_tab = jnp.concatenate([lanes(jnp.real(at[0])), lanes(jnp.imag(at[0])),
                             lanes(jnp.real(at[1])), lanes(jnp.imag(at[1]))], axis=0)
    dvec = jnp.tile(d_skip.astype(F32).reshape(S5_PAIRS, 2, 1, hc), (1, 1, t, 1)).reshape(S5_PAIRS, 1, 2 * t * hc)
    return dict(strip=strip, we=we, v=v, a_tab=a_tab, dvec=dvec)


def _s5_e_kernel(u_ref, we_ref, ref_, imf_, reb_, imb_):
    e = _dot(u_ref[...], we_ref[...])
    ref_[...] = e[:, 0:128]
    imf_[...] = e[:, 128:256]
    reb_[...] = e[:, 256:384]
    imb_[...] = e[:, 384:512]


def _s5_scan_kernel(a_ref, ref_, imf_, reb_, imb_, prf, pif, prb, pib, *, n_rows, n_ctx):
    afr = a_ref[0:1, :]
    afi = a_ref[1:2, :]
    abr = a_ref[2:3, :]
    abi = a_ref[3:4, :]
    zero = jnp.zeros_like(afr)

    def body(s, carry):
        sfr, sfi, sbr, sbi = carry
        nf = s
        nb = jnp.where(s < n_ctx, n_ctx - 1 - s, n_rows - 1 + n_ctx - s)
        prf[pl.ds(nf, 1), :] = sfr
        pif[pl.ds(nf, 1), :] = sfi
        prb[pl.ds(nb, 1), :] = sbr
        pib[pl.ds(nb, 1), :] = sbi
        efr = ref_[pl.ds(nf, 1), :]
        efi = imf_[pl.ds(nf, 1), :]
        ebr = reb_[pl.ds(nb, 1), :]
        ebi = imb_[pl.ds(nb, 1), :]
        nfr = afr * sfr - afi * sfi + efr
        nfi = afr * sfi + afi * sfr + efi
        nbr = abr * sbr - abi * sbi + ebr
        nbi = abr * sbi + abi * sbr + ebi
        return nfr, nfi, nbr, nbi

    lax.fori_loop(0, n_rows, body, (zero, zero, zero, zero))


def _s5_y_kernel(u_ref, strip_ref, v_ref, d_ref, prf, pif, prb, pib, yc_ref, yl_ref, m_scr):
    half = S5_CHUNK * S5_GROUP_CH
    n_ctx = yc_ref.shape[0]

    @pl.when(pl.program_id(1) == 0)
    def _():
        for gi in range(2):
            strip = strip_ref[gi]
            for j in range(S5_CHUNK):
                off = (S5_CHUNK - 1 - j) * S5_GROUP_CH
                win = strip if off == 0 else pltpu.roll(strip, 2 * half - off, axis=1)
                m_scr[gi, j * S5_GROUP_CH:(j + 1) * S5_GROUP_CH, :] = win[:, :half].astype(BF16)

    u = u_ref[...]
    y_intra = jnp.concatenate([_dot(u[:, :half], m_scr[0]), _dot(u[:, half:], m_scr[1])], axis=-1)
    pcat = jnp.concatenate([prf[...], pif[...], prb[...], pib[...]], axis=-1)
    hi, lo = _split_bf16(pcat)
    y_cross = _dot(jnp.concatenate([hi, lo], axis=-1), v_ref[...])
    y = y_intra + y_cross + d_ref[...] * u.astype(F32)
    yc_ref[...] = y[:n_ctx].astype(BF16)
    yl_ref[...] = y[n_ctx:].astype(BF16)


def _s5_core(u, tabs, batch, n_rows, n_ctx):
    width = batch * S5_PAIRS * 128
    cols = 2 * S5_CHUNK * S5_GROUP_CH
    u_spec = pl.BlockSpec((None, None, n_rows, cols), lambda q, b: (q, b, 0, 0))
    st_spec = pl.BlockSpec((n_rows, 128), lambda q, b: (0, b * S5_PAIRS + q))
    st_shape = jax.ShapeDtypeStruct((n_rows, width), F32)
    e4 = pl.pallas_call(
        _s5_e_kernel,
        grid=(S5_PAIRS, batch),
        in_specs=[u_spec, pl.BlockSpec((None, cols, 512), lambda q, b: (q, 0, 0))],
        out_specs=[st_spec] * 4,
        out_shape=[st_shape] * 4,
        compiler_params=_cparams("arbitrary", "arbitrary"),
        name="s5_chunk_states",
    )(u, tabs['we'])
    p4 = pl.pallas_call(
        functools.partial(_s5_scan_kernel, n_rows=n_rows, n_ctx=n_ctx),
        out_shape=[st_shape] * 4,
        compiler_params=pltpu.CompilerParams(vmem_limit_bytes=VMEM_LIMIT_BYTES),
        name="s5_state_scan",
    )(tabs['a_tab'], *e4)
    y = pl.pallas_call(
        _s5_y_kernel,
        grid=(S5_PAIRS, batch),
        in_specs=[
            u_spec,
            pl.BlockSpec((None, 2, S5_GROUP_CH, cols), lambda q, b: (q, 0, 0, 0)),
            pl.BlockSpec((None, cols, cols), lambda q, b: (q, 0, 0)),
            pl.BlockSpec((None, 1, cols), lambda q, b: (q, 0, 0)),
            st_spec, st_spec, st_spec, st_spec,
        ],
        out_specs=[
            pl.BlockSpec((None, None, n_ctx, cols), lambda q, b: (q, b, 0, 0)),
            pl.BlockSpec((None, None, n_rows - n_ctx, cols), lambda q, b: (q, b, 0, 0)),
        ],
        out_shape=[
            jax.ShapeDtypeStruct((S5_PAIRS, batch, n_ctx, cols), BF16),
            jax.ShapeDtypeStruct((S5_PAIRS, batch, n_rows - n_ctx, cols), BF16),
        ],
        scratch_shapes=[pltpu.VMEM((2, cols // 2, cols // 2), BF16)],
        compiler_params=_cparams("arbitrary", "arbitrary"),
        name="s5_outputs",
    )(u, tabs['strip'], tabs['v'], tabs['dvec'], *p4)
    return y


def _s5_to_chunks(s, batch):
    n = s.shape[0] // batch // S5_CHUNK
    v = s.reshape(batch, n, S5_CHUNK, S5_PAIRS, 2, S5_GROUP_CH)
    return v.transpose(3, 0, 1, 4, 2, 5).reshape(S5_PAIRS, batch, n, 2 * S5_CHUNK * S5_GROUP_CH)


def _s5_from_chunks(y, batch):
    n = y.shape[2]
    v = y.reshape(S5_PAIRS, batch, n, 2, S5_CHUNK, S5_GROUP_CH)
    return v.transpose(1, 2, 4, 0, 3, 5).reshape(batch * n * S5_CHUNK, BRANCH_W)


def _s5_mixer(s_lat, s_ctx, tabs, batch):
    ul = _s5_to_chunks(s_lat, batch)
    uc = _s5_to_chunks(s_ctx, batch)
    n_ctx = uc.shape[2]
    u = jnp.concatenate([uc, ul], axis=2)
    yc, yl = _s5_core(u, tabs, batch, u.shape[2], n_ctx)
    return _s5_from_chunks(yl, batch), _s5_from_chunks(yc, batch)


def _ret_tables(ret_decay):
    c = RET_CHUNK
    lg = jax.nn.log_sigmoid(ret_decay.astype(F32))
    lane_h = jnp.repeat(jnp.arange(RET_HEADS), RET_DIM)
    lgl = lg[:, lane_h]
    pos = jnp.arange(c, dtype=F32)[:, None]
    qd = jnp.stack([jnp.exp((pos + 1.0) * lgl[0][None]), jnp.exp((c - pos) * lgl[1][None])])
    kd = jnp.stack([jnp.exp((c - 1.0 - pos) * lgl[0][None]), jnp.exp(pos * lgl[1][None])])
    bmask = (lane_h[:, None] == lane_h[None, :]).astype(F32)
    cd = jnp.exp(c * lgl)[:, :, None] * bmask[None]
    diff = pos - pos.T
    dm = []
    for h in range(RET_HEADS):
        fw = jnp.where(diff >= 0, jnp.exp(jnp.maximum(diff, 0.0) * lg[0, h]), 0.0)
        bw = jnp.where(diff <= 0, jnp.exp(jnp.maximum(-diff, 0.0) * lg[1, h]), 0.0)
        dm.append(fw + bw)
    dm = jnp.concatenate(dm, axis=0)
    hmask = (jnp.arange(RET_HEADS)[:, None] == lane_h[None, :]).astype(F32)
    return dict(qd=qd, kd=kd, cd=cd, bmask=bmask, dm=dm, hmask=hmask)


def _rope_tables(n_tokens):
    t = np.arange(n_tokens)
    row = (t // GRID_W).astype(np.float64)
    col = (t % GRID_W).astype(np.float64)
    n_freq = RET_DIM // 4
    inv_freq = 1.0 / (ROPE_BASE ** (np.arange(n_freq, dtype=np.float64) / n_freq))
    ang = np.concatenate([row[:, None] * inv_freq, col[:, None] * inv_freq], axis=-1)
    cos = np.cos(ang)
    sin = np.sin(ang)
    cos_t = np.tile(np.concatenate([cos, cos], axis=-1), (1, RET_HEADS))
    sin_t = np.tile(np.concatenate([-sin, sin], axis=-1), (1, RET_HEADS))
    half = RET_DIM // 2
    perm = np.arange(BRANCH_W) ^ half
    swap = np.zeros((BRANCH_W, BRANCH_W), np.float32)
    swap[perm, np.arange(BRANCH_W)] = 1.0
    return jnp.asarray(cos_t, F32), jnp.asarray(sin_t, F32), jnp.asarray(swap, BF16)


def _ret_chunk(q, k, v, s, qd, kd, cd, bmask, dm, hmask, with_intra):
    cross = _dot((q * qd).astype(BF16), s.astype(BF16))
    s_new = cd * s + bmask * _dot_tn((k * kd).astype(BF16), v)
    if not with_intra:
        return cross, s_new
    qb = q.astype(BF16)
    kb = k.astype(BF16)
    qs = jnp.concatenate([qb * hmask[h:h + 1].astype(BF16) for h in range(RET_HEADS)], axis=0)
    scores = _dot_nt(qs, kb) * dm
    ov = _dot(scores.astype(BF16), v)
    c = q.shape[0]
    inner = ov[0:c] * hmask[0:1]
    for h in range(1, RET_HEADS):
        inner = inner + ov[h * c:(h + 1) * c] * hmask[h:h + 1]
    return inner + cross, s_new


def _ret_kernel(q_ref, k_ref, v_ref, qc_ref, kc_ref, vc_ref, cos_ref, sin_ref, swap_ref,
                qd_ref, kd_ref, cd_ref, bm_ref, dm_ref, hm_ref, o_ref, oc_ref, s_scr, *, n_chunks, n_ctx_chunks):
    dirn = pl.program_id(1)
    i = pl.program_id(2)
    c = RET_CHUNK
    k_scale = RET_DIM ** -0.5
    bmask = bm_ref[...]
    dm = dm_ref[...]
    hmask = hm_ref[...]

    def run(d):
        qd = qd_ref[d]
        kd = kd_ref[d]
        cd = cd_ref[d]
        intra = d == 0

        @pl.when(i == 0)
        def _():
            s = jnp.zeros((BRANCH_W, BRANCH_W), F32)
            order = range(n_ctx_chunks) if d == 0 else range(n_ctx_chunks - 1, -1, -1)
            for cc in order:
                sl = slice(cc * c, (cc + 1) * c)
                o, s = _ret_chunk(qc_ref[sl, :].astype(F32), kc_ref[sl, :].astype(F32) * k_scale, vc_ref[sl, :],
                                  s, qd, kd, cd, bmask, dm, hmask, intra)
                oc_ref[sl, :] = o
            s_scr[...] = s

        swap = swap_ref[...]
        order = range(n_chunks) if d == 0 else range(n_chunks - 1, -1, -1)
        s = s_scr[...]
        for cc in order:
            sl = slice(cc * c, (cc + 1) * c)
            cos = cos_ref[sl, :]
            sin = sin_ref[sl, :]
            qb = q_ref[sl, :]
            kb = k_ref[sl, :]
            q = qb.astype(F32) * cos + _dot(qb, swap) * sin
            k = (kb.astype(F32) * cos + _dot(kb, swap) * sin) * k_scale
            o, s = _ret_chunk(q, k, v_ref[sl, :], s, qd, kd, cd, bmask, dm, hmask, intra)
            o_ref[sl, :] = o
        s_scr[...] = s

    @pl.when(dirn == 0)
    def _():
        run(0)

    @pl.when(dirn == 1)
    def _():
        run(1)


def _retention(proj_l, proj_c, tabs, rope, batch, seq_len, ctx_len):
    n_chunks = 8
    blk = n_chunks * RET_CHUNK
    nblk = seq_len // blk
    cos_t, sin_t, swap = rope

    def pos(d, i):
        return i + d * (nblk - 1 - 2 * i)

    def lat(col):
        return pl.BlockSpec((blk, BRANCH_W), lambda b, d, i: (b * nblk + pos(d, i), col))

    def ctx(col):
        return pl.BlockSpec((ctx_len, BRANCH_W), lambda b, d, i: (b, col))

    def const(shape):
        return pl.BlockSpec(shape, lambda b, d, i: (0,) * len(shape))

    tab_spec = pl.BlockSpec((blk, BRANCH_W), lambda b, d, i: (pos(d, i), 0))
    kern = functools.partial(_ret_kernel, n_chunks=n_chunks, n_ctx_chunks=ctx_len // RET_CHUNK)
    c = RET_CHUNK
    o, oc = pl.pallas_call(
        kern,
        grid=(batch, 2, nblk),
        in_specs=[
            lat(COL_RQ), lat(COL_RK), lat(COL_RV), ctx(COL_RQ), ctx(COL_RK), ctx(COL_RV),
            tab_spec, tab_spec, const((BRANCH_W, BRANCH_W)),
            const((2, c, BRANCH_W)), const((2, c, BRANCH_W)), const((2, BRANCH_W, BRANCH_W)),
            const((BRANCH_W, BRANCH_W)), const((RET_HEADS * c, c)), const((RET_HEADS, BRANCH_W)),
        ],
        out_specs=[
            pl.BlockSpec((None, blk, BRANCH_W), lambda b, d, i: (d, b * nblk + pos(d, i), 0)),
            pl.BlockSpec((None, ctx_len, BRANCH_W), lambda b, d, i: (d, b, 0)),
        ],
        out_shape=[
            jax.ShapeDtypeStruct((2, batch * seq_len, BRANCH_W), F32),
            jax.ShapeDtypeStruct((2, batch * ctx_len, BRANCH_W), F32),
        ],
        scratch_shapes=[pltpu.VMEM((BRANCH_W, BRANCH_W), F32)],
        compiler_params=_cparams("arbitrary", "arbitrary", "arbitrary"),
        name="retention",
    )(proj_l, proj_l, proj_l, proj_c, proj_c, proj_c, cos_t, sin_t, swap,
      tabs['qd'], tabs['kd'], tabs['cd'], tabs['bmask'], tabs['dm'], tabs['hmask'])
    return o, oc


def _na_tables(rpb):
    kr, kw = NA_WIN_ROWS, NA_WIN_COLS
    col = np.arange(GRID_W)
    col_start = np.clip(col - kw // 2, 0, GRID_W - kw)
    in_win = (col[None, :] >= col_start[:, None]) & (col[None, :] < col_start[:, None] + kw)
    dc = np.clip(col[None, :] - col[:, None], -(kw - 1), kw - 1) + (kw - 1)
    var = np.arange(kr)[:, None] + np.arange(kr)[None, :]
    pick_r = (var[:, :, None] == np.arange(2 * kr - 1)[None, None, :]).astype(np.float32)
    pick_c = (dc[:, :, None] == np.arange(2 * kw - 1)[None, None, :]).astype(np.float32)
    bias = jnp.einsum('vir,hrc,qkc->vhqik', jnp.asarray(pick_r), rpb.astype(F32), jnp.asarray(pick_c),
                      precision=lax.Precision.HIGHEST)
    bias = jnp.where(jnp.asarray(in_win)[None, None, :, None, :], bias, NEG_BIG)
    bias = bias.reshape(kr, NA_HEADS * GRID_W, kr * GRID_W)
    lane_h = np.repeat(np.arange(NA_HEADS), NA_DIM)
    hmask = (np.arange(NA_HEADS)[:, None] == lane_h[None, :]).astype(np.float32)
    return bias, jnp.asarray(hmask, F32)


def _attend(qs, keys, vals, bias, kc, vc):
    s_ctx = _dot_nt(qs, kc)
    m = jnp.max(s_ctx, axis=-1, keepdims=True)
    if keys is not None:
        s_band = _dot_nt(qs, keys) + bias
        m = jnp.maximum(m, jnp.max(s_band, axis=-1, keepdims=True))
        p_band = jnp.exp(s_band - m)
    p_ctx = jnp.exp(s_ctx - m)
    l = jnp.sum(p_ctx, axis=-1, keepdims=True)
    o = _dot(p_ctx.astype(BF16), vc)
    if keys is not None:
        l = l + jnp.sum(p_band, axis=-1, keepdims=True)
        o = o + _dot(p_band.astype(BF16), vals)
    return o / l


def _stack_heads(q, hmask_scaled):
    return jnp.concatenate([q * hmask_scaled[h:h + 1] for h in range(NA_HEADS)], axis=0)


def _unstack_heads(o, hmask, n):
    out = o[0:n] * hmask[0:1]
    for h in range(1, NA_HEADS):
        out = out + o[h * n:(h + 1) * n] * hmask[h:h + 1]
    return out


def _na_kernel(q_ref, k_ref, v_ref, kc_ref, vc_ref, bias_ref, hm_ref, o_ref, *, n_grid_rows):
    i = pl.program_id(1)
    hmask = hm_ref[...]
    hms = (hmask * (NA_DIM ** -0.5)).astype(BF16)
    kc = kc_ref[...]
    vc = vc_ref[...]
    band = NA_WIN_ROWS * GRID_W
    for rr in range(NA_QROWS):
        r = i * NA_QROWS + rr
        rs = jnp.clip(r - NA_WIN_ROWS // 2, 0, n_grid_rows - NA_WIN_ROWS)
        var = rs - r + (NA_WIN_ROWS - 1)
        start = pl.multiple_of(rs * GRID_W, GRID_W)
        keys = k_ref[pl.ds(start, band), :]
        vals = v_ref[pl.ds(start, band), :]
        qs = _stack_heads(q_ref[rr * GRID_W:(rr + 1) * GRID_W, :], hms)
        o = _attend(qs, keys, vals, bias_ref[var], kc, vc)
        o_ref[rr * GRID_W:(rr + 1) * GRID_W, :] = _unstack_heads(o, hmask, GRID_W).astype(BF16)


def _na_ctx_kernel(q_ref, kc_ref, vc_ref, hm_ref, o_ref):
    hmask = hm_ref[...]
    hms = (hmask * (NA_DIM ** -0.5)).astype(BF16)
    n = q_ref.shape[0]
    o = _attend(_stack_heads(q_ref[...], hms), None, None, None, kc_ref[...], vc_ref[...])
    o_ref[...] = _unstack_heads(o, hmask, n).astype(BF16)


def _neighborhood(proj_l, proj_c, bias, hmask, batch, seq_len, ctx_len, need_ctx_out):
    rows = seq_len // GRID_W
    qblk = NA_QROWS * GRID_W
    nq = seq_len // qblk
    out_l = pl.pallas_call(
        functools.partial(_na_kernel, n_grid_rows=rows),
        grid=(batch, nq),
        in_specs=[
            pl.BlockSpec((qblk, BRANCH_W), lambda b, i: (b * nq + i, COL_NQ)),
            pl.BlockSpec((seq_len, BRANCH_W), lambda b, i: (b, COL_NK)),
            pl.BlockSpec((seq_len, BRANCH_W), lambda b, i: (b, COL_NV)),
            pl.BlockSpec((ctx_len, BRANCH_W), lambda b, i: (b, COL_NK)),
            pl.BlockSpec((ctx_len, BRANCH_W), lambda b, i: (b, COL_NV)),
            pl.BlockSpec(bias.shape, lambda b, i: (0, 0, 0)),
            pl.BlockSpec(hmask.shape, lambda b, i: (0, 0)),
        ],
        out_specs=pl.BlockSpec((qblk, BRANCH_W), lambda b, i: (b * nq + i, 0)),
        out_shape=jax.ShapeDtypeStruct((batch * seq_len, BRANCH_W), BF16),
        compiler_params=_cparams("arbitrary", "arbitrary"),
        name="neighborhood_attn",
    )(proj_l, proj_l, proj_l, proj_c, proj_c, bias, hmask)
    out_c = None
    if need_ctx_out:
        out_c = pl.pallas_call(
            _na_ctx_kernel,
            grid=(batch,),
            in_specs=[
                pl.BlockSpec((ctx_len, BRANCH_W), lambda b: (b, COL_NQ)),
                pl.BlockSpec((ctx_len, BRANCH_W), lambda b: (b, COL_NK)),
                pl.BlockSpec((ctx_len, BRANCH_W), lambda b: (b, COL_NV)),
                pl.BlockSpec(hmask.shape, lambda b: (0, 0)),
            ],
            out_specs=pl.BlockSpec((ctx_len, BRANCH_W), lambda b: (b, 0)),
            out_shape=jax.ShapeDtypeStruct((batch * ctx_len, BRANCH_W), BF16),
            compiler_params=_cparams("arbitrary"),
            name="context_attn",
        )(proj_c, proj_c, proj_c, hmask)
    return out_l, out_c


def _merge_kernel(x_ref, mod_ref, g_ref, gt0, gt1, gt2, gt3, a_ref, s5_ref, ro_ref, rg_ref, na_ref,
                  wglu_ref, bglu_ref, gn_ref, avg_ref, wb_ref, wo_ref, o_ref, *, tiles_per_mod, mod_base):
    i = pl.program_id(0)
    _, _, gate_a = _mod_rows(mod_ref, i, tiles_per_mod, mod_base, 0)
    z = _gelu_tanh(s5_ref[...].astype(F32)).astype(BF16)
    zf = z.astype(F32)
    b_s5 = (zf * _sigmoid(_dot(z, wglu_ref[...]) + bglu_ref[...])).astype(BF16)
    o = ro_ref[0] + ro_ref[1]
    avg = avg_ref[...]
    hi, lo = _split_bf16(o)
    mu = _dot(hi, avg) + _dot(lo, avg)
    dlt = o - mu
    hi, lo = _split_bf16(dlt * dlt)
    var = _dot(hi, avg) + _dot(lo, avg)
    hn = dlt * lax.rsqrt(var + EPS) * gn_ref[...]
    b_ret = (_silu(rg_ref[...].astype(F32)) * hn).astype(BF16)
    outs = (a_ref[...], b_s5, b_ret, na_ref[...])
    gates = (gt0, gt1, gt2, gt3)
    y = (1.0 + jnp.tanh(gates[0][...].astype(F32))) * _dot(outs[0], wb_ref[0])
    for b in range(1, N_BRANCH):
        y = y + (1.0 + jnp.tanh(gates[b][...].astype(F32))) * _dot(outs[b], wb_ref[b])
    yo = _dot(y.astype(BF16), wo_ref[...])
    o_ref[...] = x_ref[...] + gate_a * _rms(yo, g_ref[...])


def _merge(x, mod, g1, proj, a, s5y, ret_o, na, lw, *, rows_per_mod, mod_base):
    rows, d = x.shape
    tm = min(512, rows)
    nt = rows // tm

    def row(shape, col=0):
        return pl.BlockSpec(shape, lambda i: (i, col))

    def const(arr):
        return pl.BlockSpec(arr.shape, lambda i: (0,) * arr.ndim)

    kern = functools.partial(_merge_kernel, tiles_per_mod=max(rows_per_mod // tm, 1), mod_base=mod_base)
    ins = [x, mod, g1.reshape(1, d), proj, proj, proj, proj, a, s5y, ret_o, proj, na,
           lw['w_glu'], lw['b_glu'], lw['ret_gn'], lw['avg'], lw['w_branch'], lw['w_out']]
    specs = [
        row((tm, d)), const(mod), pl.BlockSpec((1, d), lambda i: (0, 0)),
        row((tm, d), 0), row((tm, d), 1), row((tm, d), 2), row((tm, d), 3),
        row((tm, BRANCH_W)), row((tm, BRANCH_W)),
        pl.BlockSpec((2, tm, BRANCH_W), lambda i: (0, i, 0)),
        row((tm, BRANCH_W), COL_RG), row((tm, BRANCH_W)),
        const(lw['w_glu']), const(lw['b_glu']), const(lw['ret_gn']), const(lw['avg']),
        const(lw['w_branch']), const(lw['w_out']),
    ]
    return pl.pallas_call(
        kern,
        grid=(nt,),
        in_specs=specs,
        out_specs=row((tm, d)),
        out_shape=jax.ShapeDtypeStruct((rows, d), F32),
        compiler_params=_cparams("arbitrary"),
        name="merge_out",
    )(*ins)


def _ffn_kernel(x_ref, mod_ref, g2_ref, g3_ref, wg_ref, wu_ref, wd_ref, o_ref, h_scr, acc_scr,
                *, tiles_per_mod, mod_base, n_f):
    i = pl.program_id(0)
    f = pl.program_id(1)

    @pl.when(f == 0)
    def _():
        sh, sc, _ = _mod_rows(mod_ref, i, tiles_per_mod, mod_base, 3)
        h_scr[...] = (_rms(x_ref[...], g2_ref[...]) * (1.0 + sc) + sh).astype(BF16)
        acc_scr[...] = jnp.zeros_like(acc_scr)

    h = h_scr[...]
    act = (_silu(_dot(h, wg_ref[...])) * _dot(h, wu_ref[...])).astype(BF16)
    acc_scr[...] += _dot(act, wd_ref[...])

    @pl.when(f == n_f - 1)
    def _():
        _, _, gate_f = _mod_rows(mod_ref, i, tiles_per_mod, mod_base, 3)
        o_ref[...] = x_ref[...] + gate_f * _rms(acc_scr[...], g3_ref[...])


def _ffn_dense(x, mod, g2, g3, wg, wu, wd, *, rows_per_mod, mod_base):
    rows, d = x.shape
    d_ff = wg.shape[1]
    tm = min(512, rows)
    tf = d_ff // 2 if (d_ff // 2) % 128 == 0 else d_ff
    n_f = d_ff // tf
    kern = functools.partial(_ffn_kernel, tiles_per_mod=max(rows_per_mod // tm, 1), mod_base=mod_base, n_f=n_f)
    return pl.pallas_call(
        kern,
        grid=(rows // tm, n_f),
        in_specs=[
            pl.BlockSpec((tm, d), lambda i, f: (i, 0)),
            pl.BlockSpec(mod.shape, lambda i, f: (0, 0)),
            pl.BlockSpec((1, d), lambda i, f: (0, 0)),
            pl.BlockSpec((1, d), lambda i, f: (0, 0)),
            pl.BlockSpec((d, tf), lambda i, f: (0, f)),
            pl.BlockSpec((d, tf), lambda i, f: (0, f)),
            pl.BlockSpec((tf, d), lambda i, f: (f, 0)),
        ],
        out_specs=pl.BlockSpec((tm, d), lambda i, f: (i, 0)),
        out_shape=jax.ShapeDtypeStruct((rows, d), F32),
        scratch_shapes=[pltpu.VMEM((tm, d), BF16), pltpu.VMEM((tm, d), F32)],
        compiler_params=_cparams("arbitrary", "arbitrary"),
        name="ffn_dense",
    )(x, mod, g2.reshape(1, d), g3.reshape(1, d), wg, wu, wd)


def _router_kernel(x_ref, mod_ref, g2_ref, wr_ref, br_ref, h_ref, comb_ref, *, tiles_per_mod, mod_base):
    i = pl.program_id(0)
    sh, sc, _ = _mod_rows(mod_ref, i, tiles_per_mod, mod_base, 3)
    h = _rms(x_ref[...], g2_ref[...]) * (1.0 + sc) + sh
    h_ref[...] = _pack_pairs(h)
    h_hi, h_lo = _split_bf16(h)
    w_hi, w_lo = _split_bf16(wr_ref[...])
    logits = _dot(h_hi, w_hi) + _dot(h_lo, w_hi) + _dot(h_hi, w_lo) + br_ref[...]
    lane = lax.broadcasted_iota(jnp.int32, logits.shape, 1)
    v1 = jnp.max(logits, axis=-1, keepdims=True)
    i1 = jnp.min(jnp.where(logits == v1, lane, 128), axis=-1, keepdims=True)
    rest = jnp.where(lane == i1, NEG_BIG, logits)
    v2 = jnp.max(rest, axis=-1, keepdims=True)
    i2 = jnp.min(jnp.where(rest == v2, lane, 128), axis=-1, keepdims=True)
    e = jnp.exp(v2 - v1)
    w1 = 1.0 / (1.0 + e)
    w2 = e / (1.0 + e)
    meta = jnp.where(lane == 0, i1.astype(F32), 0.0) + jnp.where(lane == 1, i2.astype(F32), 0.0)
    comb_ref[...] = meta + jnp.where(lane == 2, w1, 0.0) + jnp.where(lane == 3, w2, 0.0)


def _router(x, mod, g2, w_router, b_router, *, rows_per_mod, mod_base):
    rows, d = x.shape
    tm = min(512, rows)
    wr = jnp.zeros((d, 128), F32).at[:, :N_EXPERTS].set(w_router)
    br = jnp.full((1, 128), NEG_BIG, F32).at[0, :N_EXPERTS].set(b_router)
    kern = functools.partial(_router_kernel, tiles_per_mod=max(rows_per_mod // tm, 1), mod_base=mod_base)
    return pl.pallas_call(
        kern,
        grid=(rows // tm,),
        in_specs=[
            pl.BlockSpec((tm, d), lambda i: (i, 0)),
            pl.BlockSpec(mod.shape, lambda i: (0, 0)),
            pl.BlockSpec((1, d), lambda i: (0, 0)),
            pl.BlockSpec((d, 128), lambda i: (0, 0)),
            pl.BlockSpec((1, 128), lambda i: (0, 0)),
        ],
        out_specs=[pl.BlockSpec((tm, d // 2), lambda i: (i, 0)), pl.BlockSpec((tm, 128), lambda i: (i, 0))],
        out_shape=[jax.ShapeDtypeStruct((rows, d // 2), jnp.int32), jax.ShapeDtypeStruct((rows, 128), F32)],
        compiler_params=_cparams("arbitrary"),
        name="moe_router",
    )(x, mod, g2.reshape(1, d), wr, br)


def _sc_gather(table, idx):
    n_idx = idx.shape[0]
    width = table.shape[1]
    per_worker = n_idx // SC_WORKERS
    chunk_rows = math.gcd(per_worker, SC_GATHER_ROWS)
    n_chunks = per_worker // chunk_rows
    assert per_worker * SC_WORKERS == n_idx and chunk_rows % 8 == 0
    mesh = plsc.VectorSubcoreMesh(core_axis_name="c", subcore_axis_name="s")

    @functools.partial(
        pl.kernel, mesh=mesh,
        out_type=jax.ShapeDtypeStruct((n_idx, width), table.dtype),
        scratch_types=[
            pltpu.VMEM((chunk_rows,), jnp.int32),
            pltpu.VMEM((chunk_rows, width), table.dtype),
            pltpu.SemaphoreType.DMA,
        ],
        name="sc_row_gather",
    )
    def gather(table_hbm, idx_hbm, out_hbm, idx_v, rows_v, sem):
        wid = lax.axis_index("s") * SC_CORES + lax.axis_index("c")
        base = wid * per_worker

        @pl.loop(0, n_chunks)
        def _(j):
            off = base + j * chunk_rows
            pltpu.sync_copy(idx_hbm.at[pl.ds(off, chunk_rows)], idx_v)
            pltpu.async_copy(table_hbm.at[idx_v], rows_v, sem).wait()
            pltpu.sync_copy(rows_v, out_hbm.at[pl.ds(off, chunk_rows)])

    return gather(table, idx)


def _sc_scatter(table, idx, n_out):
    n_idx = idx.shape[0]
    rows, width = table.shape
    per_worker = n_idx // SC_WORKERS
    chunk_rows = math.gcd(per_worker, SC_GATHER_ROWS)
    n_chunks = per_worker // chunk_rows
    assert per_worker * SC_WORKERS == n_idx and chunk_rows % 8 == 0 and rows % per_worker == 0
    mesh = plsc.VectorSubcoreMesh(core_axis_name="c", subcore_axis_name="s")

    @functools.partial(
        pl.kernel, mesh=mesh,
        out_type=jax.ShapeDtypeStruct((n_out, width), table.dtype),
        scratch_types=[
            pltpu.VMEM((chunk_rows,), jnp.int32),
            pltpu.VMEM((chunk_rows, width), table.dtype),
            pltpu.SemaphoreType.DMA,
        ],
        name="sc_row_scatter",
    )
    def scatter(table_hbm, idx_hbm, out_hbm, idx_v, rows_v, sem):
        wid = lax.axis_index("s") * SC_CORES + lax.axis_index("c")
        base = wid * per_worker

        @pl.loop(0, n_chunks)
        def _(j):
            off = base + j * chunk_rows
            pltpu.sync_copy(idx_hbm.at[pl.ds(off, chunk_rows)], idx_v)
            pltpu.sync_copy(table_hbm.at[pl.ds(lax.rem(off, rows), chunk_rows)], rows_v)
            pltpu.async_copy(rows_v, out_hbm.at[idx_v], sem).wait()

    return scatter(table, idx)


def _moe_plan(meta, rows):
    tile = MOE_ROW_TILE
    n_tiles = (2 * rows) // tile + N_EXPERTS
    n_slots = n_tiles * tile
    experts = jnp.concatenate([meta[:, 0], meta[:, 1]]).astype(jnp.int32)
    onehot = (experts[:, None] == jnp.arange(N_EXPERTS)[None, :]).astype(jnp.int32)
    csum = jnp.cumsum(onehot, axis=0)
    counts = csum[-1]
    rank = jnp.sum(onehot * csum, axis=1) - 1
    padded = ((counts + tile - 1) // tile) * tile
    ends = jnp.cumsum(padded)
    starts = ends - padded
    pos = jnp.sum(onehot * starts[None, :], axis=1) + rank
    tile_start = jnp.arange(n_tiles, dtype=jnp.int32) * tile
    used = tile_start < ends[-1]
    tile_e = jnp.minimum(jnp.sum((tile_start[:, None] >= ends[None, :]).astype(jnp.int32), axis=1), N_EXPERTS - 1)
    last_e = jnp.max(jnp.where(used, tile_e, 0))
    tile_e = jnp.where(used, tile_e, last_e)
    valid_end = jnp.sum((tile_e[:, None] == jnp.arange(N_EXPERTS)[None, :]) * (starts + counts)[None, :], axis=1)
    n_valid = jnp.where(used, jnp.clip(valid_end - tile_start, 0, tile), 0).astype(jnp.int32)
    return pos.astype(jnp.int32), n_slots, tile_e.astype(jnp.int32), n_valid


def _moe_group_kernel(eid_ref, nval_ref, hs_ref, wg_ref, wu_ref, wd_ref, y_ref, h_scr, acc_scr, *, n_f):
    w = pl.program_id(0)
    f = pl.program_id(1)
    nv = nval_ref[w]

    @pl.when(nv > 0)
    def _():
        @pl.when(f == 0)
        def _():
            hv = _unpack_pairs(hs_ref[...])
            row = lax.broadcasted_iota(jnp.int32, hv.shape, 0)
            h_scr[...] = jnp.where(row < nv, hv, 0.0).astype(BF16)

        h = h_scr[...]
        gate = _dot(h, wg_ref[...].astype(BF16))
        up = _dot(h, wu_ref[...].astype(BF16))
        part = _dot((_silu(gate) * up).astype(BF16), wd_ref[...].astype(BF16))

        @pl.when(f == 0)
        def _():
            acc_scr[...] = part

        @pl.when(f > 0)
        def _():
            acc_scr[...] += part

        @pl.when(f == n_f - 1)
        def _():
            y_ref[...] = _pack_pairs(acc_scr[...])


def _moe_grouped(hs, tile_e, n_valid, wg, wu, wd):
    n_slots = hs.shape[0]
    d = wg.shape[1]
    d_ff = wg.shape[2]
    tile = MOE_ROW_TILE
    tf = MOE_FF_TILE
    n_f = d_ff // tf

    def f_idx(f, nval, w):
        return jnp.where(nval[w] > 0, f, n_f - 1)

    grid_spec = pltpu.PrefetchScalarGridSpec(
        num_scalar_prefetch=2,
        grid=(n_slots // tile, n_f),
        in_specs=[
            pl.BlockSpec((tile, d // 2), lambda w, f, eid, nval: (w, 0)),
            pl.BlockSpec((None, d, tf), lambda w, f, eid, nval: (eid[w], 0, f_idx(f, nval, w))),
            pl.BlockSpec((None, d, tf), lambda w, f, eid, nval: (eid[w], 0, f_idx(f, nval, w))),
            pl.BlockSpec((None, tf, d), lambda w, f, eid, nval: (eid[w], f_idx(f, nval, w), 0)),
        ],
        out_specs=pl.BlockSpec((tile, d // 2), lambda w, f, eid, nval: (w, 0)),
        scratch_shapes=[pltpu.VMEM((tile, d), BF16), pltpu.VMEM((tile, d), F32)],
    )
    return pl.pallas_call(
        functools.partial(_moe_group_kernel, n_f=n_f),
        grid_spec=grid_spec,
        out_shape=jax.ShapeDtypeStruct((n_slots, d // 2), jnp.int32),
        compiler_params=_cparams("arbitrary", "arbitrary"),
        name="moe_experts",
    )(tile_e, n_valid, hs, wg, wu, wd)


def _moe_out_kernel(x_ref, y1_ref, y2_ref, meta_ref, mod_ref, g3_ref, o_ref, *, tiles_per_mod, mod_base):
    i = pl.program_id(0)
    _, _, gate_f = _mod_rows(mod_ref, i, tiles_per_mod, mod_base, 3)
    meta = meta_ref[...]
    y = meta[:, 2:3] * _unpack_pairs(y1_ref[...]) + meta[:, 3:4] * _unpack_pairs(y2_ref[...])
    o_ref[...] = x_ref[...] + gate_f * _rms(y, g3_ref[...])


def _moe_combine(x, yg, meta, mod, g3, *, rows_per_mod, mod_base):
    rows, d = x.shape
    tm = min(512, rows)
    nt = rows // tm
    kern = functools.partial(_moe_out_kernel, tiles_per_mod=max(rows_per_mod // tm, 1), mod_base=mod_base)
    return pl.pallas_call(
        kern,
        grid=(nt,),
        in_specs=[
            pl.BlockSpec((tm, d), lambda i: (i, 0)),
            pl.BlockSpec((tm, d // 2), lambda i: (i, 0)),
            pl.BlockSpec((tm, d // 2), lambda i: (nt + i, 0)),
            pl.BlockSpec((tm, 128), lambda i: (i, 0)),
            pl.BlockSpec(mod.shape, lambda i: (0, 0)),
            pl.BlockSpec((1, d), lambda i: (0, 0)),
        ],
        out_specs=pl.BlockSpec((tm, d), lambda i: (i, 0)),
        out_shape=jax.ShapeDtypeStruct((rows, d), F32),
        compiler_params=_cparams("arbitrary"),
        name="moe_combine",
    )(x, yg, yg, meta, mod, g3.reshape(1, d))


def _moe_sparse(x, h, meta, mod, g3, wg, wu, wd, *, rows_per_mod, mod_base):
    rows = x.shape[0]
    pos, n_slots, tile_e, n_valid = _moe_plan(meta, rows)
    hs = _sc_scatter(h, pos, n_slots)
    ys = _moe_grouped(hs, tile_e, n_valid, wg, wu, wd)
    yg = _sc_gather(ys, pos)
    return _moe_combine(x, yg, meta, mod, g3, rows_per_mod=rows_per_mod, mod_base=mod_base)


def _cast_kernel(w_ref, o_ref, *, scale):
    w = w_ref[...]
    o_ref[...] = (w if scale == 1.0 else w * scale).astype(BF16)


def _cast_bf16(w_stack, layer, scale=1.0):
    squeeze = w_stack.ndim == 3
    w4 = w_stack[:, None] if squeeze else w_stack
    _, n_e, k, n = w4.shape
    bk = min(k, 256)
    out = pl.pallas_call(
        functools.partial(_cast_kernel, scale=scale),
        grid=(n_e, k // bk),
        in_specs=[pl.BlockSpec((None, None, bk, n), lambda e, i: (layer, e, i, 0))],
        out_specs=pl.BlockSpec((None, bk, n), lambda e, i: (e, i, 0)),
        out_shape=jax.ShapeDtypeStruct((n_e, k, n), BF16),
        compiler_params=_cparams("arbitrary", "arbitrary"),
        name="cast_weights",
    )(w4)
    return out[0] if squeeze else out


def _permute_w_in(w_in_stack, layer):
    _, k, n = w_in_stack.shape
    n_blocks = n // BRANCH_W
    shift = 9
    n_gate_blocks = N_BRANCH * D_MODEL // BRANCH_W

    def permute_kernel(w_ref, o_ref):
        scale = jnp.where(pl.program_id(0) < n_gate_blocks, 0.5, 1.0)
        o_ref[...] = (w_ref[...] * scale).astype(BF16)

    return pl.pallas_call(
        permute_kernel,
        grid=(n_blocks,),
        in_specs=[pl.BlockSpec((None, k, BRANCH_W), lambda j: (layer, 0, (j + shift) % n_blocks))],
        out_specs=pl.BlockSpec((k, BRANCH_W), lambda j: (0, j)),
        out_shape=jax.ShapeDtypeStruct((k, n), BF16),
        compiler_params=_cparams("arbitrary"),
        name="cast_permute_w_in",
    )(w_in_stack)


def kernel(x, c, ctx, c_ctx, w_mod, b_mod, norm_g, w_in, s5_a_re, s5_a_im, s5_log_dt, s5_b_re, s5_b_im, s5_c_re, s5_c_im, s5_d, s5_w_glu, s5_b_glu, ret_decay, ret_gn, na_rpb, w_branch, w_out, ffn_w_gate, ffn_w_up, ffn_w_down, moe_w_router, moe_b_router, moe_w_gate, moe_w_up, moe_w_down):
    batch, seq_len, d = x.shape
    ctx_len = ctx.shape[1]
    depth = w_mod.shape[0]
    cond = jnp.concatenate([c, c_ctx[None, :]], axis=0)
    mod_all = _modulation(cond, w_mod, b_mod)
    rope = _rope_tables(seq_len)
    lane_h = np.repeat(np.arange(RET_HEADS), RET_DIM)
    avg = jnp.asarray((lane_h[:, None] == lane_h[None, :]).astype(np.float32) / RET_DIM, BF16)

    xl = x.reshape(batch * seq_len, d)
    xc = ctx.reshape(batch * ctx_len, d)
    lat = dict(rows_per_mod=seq_len, mod_base=0)
    cxt = dict(rows_per_mod=batch * ctx_len, mod_base=batch)

    for layer in range(depth):
        last = layer == depth - 1
        need_ctx = not last
        mod = mod_all[layer]
        ng = norm_g[layer]
        w_in_bf = _permute_w_in(w_in, layer)
        s5_tabs = _s5_tables(s5_a_re[layer], s5_a_im[layer], s5_log_dt[layer], s5_b_re[layer], s5_b_im[layer],
                             s5_c_re[layer], s5_c_im[layer], s5_d[layer], batch)
        ret_tabs = _ret_tables(ret_decay[layer])
        na_bias, na_hmask = _na_tables(na_rpb[layer])
        lw = dict(w_glu=s5_w_glu[layer].astype(BF16), b_glu=s5_b_glu[layer].reshape(1, BRANCH_W).astype(F32),
                  ret_gn=ret_gn[layer].reshape(1, BRANCH_W).astype(F32), avg=avg,
                  w_branch=_cast_bf16(w_branch, layer, 0.5), w_out=_cast_bf16(w_out, layer))

        proj_l, f_l = _in_proj(xl, mod, ng[0], w_in_bf, **lat)
        proj_c, f_c = _in_proj(xc, mod, ng[0], w_in_bf, **cxt)

        a_l = _fourier_latent(f_l, batch, seq_len)
        s_l, s_c = _s5_mixer(proj_l[:, COL_S * BRANCH_W:(COL_S + 1) * BRANCH_W],
                             proj_c[:, COL_S * BRANCH_W:(COL_S + 1) * BRANCH_W], s5_tabs, batch)
        r_l, r_c = _retention(proj_l, proj_c, ret_tabs, rope, batch, seq_len, ctx_len)
        n_l, n_c = _neighborhood(proj_l, proj_c, na_bias, na_hmask, batch, seq_len, ctx_len, need_ctx)

        xl = _merge(xl, mod, ng[1], proj_l, a_l, s_l, r_l, n_l, lw, **lat)
        if need_ctx:
            a_c = _fourier_ctx(f_c, batch, ctx_len)
            xc = _merge(xc, mod, ng[1], proj_c, a_c, s_c, r_c, n_c, lw, **cxt)

        i = layer // 2
        if layer % 2 == 0:
            wg, wu, wd = _cast_bf16(ffn_w_gate, i), _cast_bf16(ffn_w_up, i), _cast_bf16(ffn_w_down, i)
            xl = _ffn_dense(xl, mod, ng[2], ng[3], wg, wu, wd, **lat)
            if need_ctx:
                xc = _ffn_dense(xc, mod, ng[2], ng[3], wg, wu, wd, **cxt)
        else:
            wg, wu, wd = moe_w_gate[i], moe_w_up[i], moe_w_down[i]
            h, meta = _router(xl, mod, ng[2], moe_w_router[i], moe_b_router[i], **lat)
            xl = _moe_sparse(xl, h, meta, mod, ng[3], wg, wu, wd, **lat)
            if need_ctx:
                hc, metac = _router(xc, mod, ng[2], moe_w_router[i], moe_b_router[i], **cxt)
                xc = _moe_sparse(xc, hc, metac, mod, ng[3], wg, wu, wd, **cxt)
    return xl.reshape(batch, seq_len, d)
```

```python
import functools
import math

import numpy as np
import jax
import jax.numpy as jnp
from jax import lax
from jax.experimental import pallas as pl
from jax.experimental.pallas import tpu as pltpu
from jax.experimental.pallas import tpu_sc as plsc

F32 = jnp.float32
BF16 = jnp.bfloat16

D_MODEL = 1024
BRANCH_W = 256
N_BRANCH = 4
GRID_W = 64
FNET_GROUP_DIM = 64
S5_GROUP_CH = 16
S5_GROUPS = 16
S5_STATE = 64
S5_CHUNK = 32
S5_PAIRS = S5_GROUPS // 2
RET_HEADS = 4
RET_DIM = 64
RET_CHUNK = 128
NA_HEADS = 4
NA_DIM = 64
NA_WIN_ROWS = 8
NA_WIN_COLS = 16
NA_QROWS = 8
ROPE_BASE = 10000.0
N_EXPERTS = 8
EPS = 1e-6
FFT_N2 = 256
NEG_BIG = -1e30
VMEM_LIMIT_BYTES = 50 * 1024 * 1024
SC_CORES = 2
SC_SUBCORES = 16
SC_WORKERS = SC_CORES * SC_SUBCORES
SC_GATHER_ROWS = 64
MOE_ROW_TILE = 1024
MOE_FF_TILE = 512

COL_F, COL_S, COL_RQ, COL_RK, COL_RV, COL_RG, COL_NQ, COL_NK, COL_NV = range(16, 25)
IN_W = 9 * BRANCH_W + N_BRANCH * D_MODEL
IN_TN = 1280
IN_F_TILE = (N_BRANCH * D_MODEL) // IN_TN
IN_F_OFF = N_BRANCH * D_MODEL - IN_F_TILE * IN_TN


def _cparams(*sem):
    return pltpu.CompilerParams(dimension_semantics=sem, vmem_limit_bytes=VMEM_LIMIT_BYTES)


def _sigmoid(v):
    return 0.5 * jnp.tanh(0.5 * v) + 0.5


def _silu(v):
    return v * _sigmoid(v)


def _gelu_tanh(v):
    return 0.5 * v * (1.0 + jnp.tanh(math.sqrt(2.0 / math.pi) * (v + 0.044715 * (v * v * v))))


def _rms(v, g):
    ms = jnp.mean(v * v, axis=-1, keepdims=True)
    return v * lax.rsqrt(ms + EPS) * g


def _split_bf16(v):
    hi = v.astype(BF16)
    lo = (v - hi.astype(F32)).astype(BF16)
    return hi, lo


def _pack_pairs(v):
    n = v.shape[1] // 2
    lo = lax.bitcast_convert_type(v[:, :n].astype(BF16).astype(F32), jnp.int32)
    hi = lax.bitcast_convert_type(v[:, n:].astype(BF16).astype(F32), jnp.int32)
    return (hi & -65536) | ((lo >> 16) & 65535)


def _unpack_pairs(w):
    lo = lax.bitcast_convert_type(w << 16, F32)
    hi = lax.bitcast_convert_type(w & -65536, F32)
    return jnp.concatenate([lo, hi], axis=-1)


def _dot(a, b):
    return jnp.dot(a, b, preferred_element_type=F32)


def _dot_nt(a, b):
    return lax.dot_general(a, b, (((1,), (1,)), ((), ())), preferred_element_type=F32)


def _dot_tn(a, b):
    return lax.dot_general(a, b, (((0,), (0,)), ((), ())), preferred_element_type=F32)


def _mod_kernel(ct_ref, w_ref, b_ref, o_ref):
    ct = ct_ref[...]
    s = _silu(ct)
    w = w_ref[...]
    rows = [jnp.sum(w * s[:, r:r + 1], axis=0, keepdims=True) for r in range(8)]
    o_ref[...] = jnp.concatenate(rows, axis=0) + b_ref[...]


def _modulation(cond, w_mod, b_mod):
    n_layers, d, n = w_mod.shape
    tn = 512
    ct = jnp.zeros((8, d), F32).at[:cond.shape[0]].set(cond).T
    return pl.pallas_call(
        _mod_kernel,
        grid=(n_layers, n // tn),
        in_specs=[
            pl.BlockSpec((d, 8), lambda l, j: (0, 0)),
            pl.BlockSpec((None, d, tn), lambda l, j: (l, 0, j)),
            pl.BlockSpec((None, 1, tn), lambda l, j: (l, 0, j)),
        ],
        out_specs=pl.BlockSpec((None, 8, tn), lambda l, j: (l, 0, j)),
        out_shape=jax.ShapeDtypeStruct((n_layers, 8, n), F32),
        compiler_params=_cparams("arbitrary", "arbitrary"),
        name="adaln_mod",
    )(ct, w_mod, b_mod.reshape(n_layers, 1, n))


def _mod_rows(mod_ref, i, tiles_per_mod, mod_base, first):
    r = mod_base + i // tiles_per_mod
    return [mod_ref[pl.ds(r, 1), (first + k) * D_MODEL:(first + k + 1) * D_MODEL] for k in range(3)]


def _in_kernel(x_ref, mod_ref, g_ref, w_ref, proj_ref, f_ref, h_scr, *, tiles_per_mod, mod_base):
    i = pl.program_id(0)
    j = pl.program_id(1)

    @pl.when(j == 0)
    def _():
        sh, sc, _ = _mod_rows(mod_ref, i, tiles_per_mod, mod_base, 0)
        h_scr[...] = (_rms(x_ref[...], g_ref[...]) * (1.0 + sc) + sh).astype(BF16)

    res = _dot(h_scr[...], w_ref[...])
    proj_ref[...] = res.astype(BF16)

    @pl.when(j == IN_F_TILE)
    def _():
        f_ref[...] = res[:, IN_F_OFF:IN_F_OFF + BRANCH_W].astype(BF16)


def _in_proj(x, mod, g, w_bf, *, rows_per_mod, mod_base):
    rows, d = x.shape
    tm = math.gcd(1024, rows_per_mod)
    kern = functools.partial(_in_kernel, tiles_per_mod=max(rows_per_mod // tm, 1), mod_base=mod_base)
    return pl.pallas_call(
        kern,
        grid=(rows // tm, IN_W // IN_TN),
        in_specs=[
            pl.BlockSpec((tm, d), lambda i, j: (i, 0)),
            pl.BlockSpec(mod.shape, lambda i, j: (0, 0)),
            pl.BlockSpec((1, d), lambda i, j: (0, 0)),
            pl.BlockSpec((d, IN_TN), lambda i, j: (0, j)),
        ],
        out_specs=[
            pl.BlockSpec((tm, IN_TN), lambda i, j: (i, j)),
            pl.BlockSpec((tm, BRANCH_W), lambda i, j: (i, 0)),
        ],
        out_shape=[
            jax.ShapeDtypeStruct((rows, IN_W), BF16),
            jax.ShapeDtypeStruct((rows, BRANCH_W), BF16),
        ],
        scratch_shapes=[pltpu.VMEM((tm, d), BF16)],
        compiler_params=_cparams("arbitrary", "arbitrary"),
        name="in_proj",
    )(x, mod, g.reshape(1, d), w_bf)


def _fft_a_kernel(x_ref, cs_ref, tc_ref, ts_ref, zr_ref, zi_ref, *, n1, n1p):
    y = _dot(cs_ref[...].astype(BF16), x_ref[...])
    yr = y[:n1]
    yi = y[n1p:n1p + n1]
    tc = tc_ref[...]
    ts = ts_ref[...]
    zr_ref[...] = (yr * tc + yi * ts).astype(BF16)
    zi_ref[...] = (yi * tc - yr * ts).astype(BF16)


def _fft_b_kernel(zr_ref, zi_ref, cs_ref, cc_ref, sc_ref, o_ref, *, kb, scale, has_imag):
    cs = cs_ref[...].astype(BF16)
    cc = cc_ref[...].astype(BF16)
    sc = sc_ref[...].astype(BF16)
    for kk in range(kb):
        a = _dot(cs, zr_ref[kk])
        if has_imag:
            b = _dot(cs, zi_ref[kk])
            xr = a[:FFT_N2] + b[FFT_N2:]
            xi = b[:FFT_N2] - a[FFT_N2:]
        else:
            xr = a[:FFT_N2]
            xi = -a[FFT_N2:]
        out = _dot(xr.astype(BF16), cc) + _dot(xi.astype(BF16), sc)
        o_ref[:, kk * BRANCH_W:(kk + 1) * BRANCH_W] = (out * scale).astype(BF16)


def _dft_tables(n):
    k = np.arange(n)
    ang = 2.0 * np.pi * ((k[:, None] * k[None, :]) % n) / n
    return np.cos(ang), np.sin(ang)


def _fft_b_call(zr, zi, n1, batch, seq_len, has_imag):
    c2, s2 = _dft_tables(FFT_N2)
    cs2 = jnp.asarray(np.concatenate([c2, s2], axis=0), F32)
    c64, s64 = _dft_tables(FNET_GROUP_DIM)
    eye = np.eye(BRANCH_W // FNET_GROUP_DIM)
    cc = jnp.asarray(np.kron(eye, c64), F32)
    sc = jnp.asarray(np.kron(eye, s64), F32)
    kb = min(8, n1)
    scale = 1.0 / math.sqrt(seq_len * FNET_GROUP_DIM)
    kern = functools.partial(_fft_b_kernel, kb=kb, scale=scale, has_imag=has_imag)
    zspec = pl.BlockSpec((None, kb, FFT_N2, BRANCH_W), lambda b, i: (b, i, 0, 0))
    out = pl.pallas_call(
        kern,
        grid=(batch, n1 // kb),
        in_specs=[
            zspec, zspec,
            pl.BlockSpec((2 * FFT_N2, FFT_N2), lambda b, i: (0, 0)),
            pl.BlockSpec((BRANCH_W, BRANCH_W), lambda b, i: (0, 0)),
            pl.BlockSpec((BRANCH_W, BRANCH_W), lambda b, i: (0, 0)),
        ],
        out_specs=pl.BlockSpec((None, FFT_N2, kb * BRANCH_W), lambda b, i: (b, 0, i)),
        out_shape=jax.ShapeDtypeStruct((batch, FFT_N2, n1 * BRANCH_W), BF16),
        compiler_params=_cparams("arbitrary", "arbitrary"),
        name="fourier_stage_b",
    )(zr, zi, cs2, cc, sc)
    return out.reshape(batch * seq_len, BRANCH_W)


def _fourier_latent(f, batch, seq_len):
    n1 = seq_len // FFT_N2
    wide = FFT_N2 * BRANCH_W
    c1, s1 = _dft_tables(n1)
    n1p = max(n1, 8)
    cs1 = np.zeros((2 * n1p, n1))
    cs1[:n1] = c1
    cs1[n1p:n1p + n1] = -s1
    k1 = np.arange(n1)[:, None]
    l2 = np.arange(FFT_N2)[None, :]
    tw = 2.0 * np.pi * (k1 * l2) / seq_len
    tc = jnp.asarray(np.repeat(np.cos(tw), BRANCH_W, axis=1), F32)
    ts = jnp.asarray(np.repeat(np.sin(tw), BRANCH_W, axis=1), F32)
    cw = min(8192, wide)
    xv = f.reshape(batch, n1, wide)
    spec = pl.BlockSpec((None, n1, cw), lambda b, j: (b, 0, j))
    tspec = pl.BlockSpec((n1, cw), lambda b, j: (0, j))
    zr, zi = pl.pallas_call(
        functools.partial(_fft_a_kernel, n1=n1, n1p=n1p),
        grid=(batch, wide // cw),
        in_specs=[spec, pl.BlockSpec((2 * n1p, n1), lambda b, j: (0, 0)), tspec, tspec],
        out_specs=[spec, spec],
        out_shape=[jax.ShapeDtypeStruct((batch, n1, wide), BF16)] * 2,
        compiler_params=_cparams("arbitrary", "arbitrary"),
        name="fourier_stage_a",
    )(xv, jnp.asarray(cs1, F32), tc, ts)
    zr = zr.reshape(batch, n1, FFT_N2, BRANCH_W)
    zi = zi.reshape(batch, n1, FFT_N2, BRANCH_W)
    return _fft_b_call(zr, zi, n1, batch, seq_len, True)


def _fourier_ctx(f, batch, ctx_len):
    assert ctx_len == FFT_N2
    z = f.reshape(batch, 1, FFT_N2, BRANCH_W)
    return _fft_b_call(z, z, 1, batch, ctx_len, False)


def _s5_tables(a_re, a_im, log_dt, b_re, b_im, c_re, c_im, d_skip, batch):
    t = S5_CHUNK
    g, p, hc = S5_GROUPS, S5_STATE, S5_GROUP_CH
    lam = lax.complex(a_re.astype(F32), a_im.astype(F32))
    dt = jnp.exp(log_dt.astype(F32))[..., None]
    ks = jnp.arange(t + 1, dtype=F32)
    apow = jnp.exp((lam * dt)[..., None] * ks)
    a_bar = apow[..., 1]
    b_bar = ((a_bar - 1.0) / lam)[..., None] * lax.complex(b_re.astype(F32), b_im.astype(F32))
    cm = lax.complex(c_re.astype(F32), c_im.astype(F32))
    kimp = jnp.real(jnp.einsum('dghp,dgpk,dgpj->dgkhj', cm, apow[..., :t], b_bar,
                               precision=lax.Precision.HIGHEST))
    kf, kb = kimp[0], kimp[1]
    kfull = jnp.concatenate([kb[:, :0:-1], kf[:, :1] + kb[:, :1], kf[:, 1:]], axis=1)
    strip = kfull.transpose(0, 3, 1, 2).reshape(g, hc, (2 * t - 1) * hc)
    strip = jnp.pad(strip, ((0, 0), (0, 0), (0, 2 * t * hc - strip.shape[-1])))
    strip = strip.reshape(S5_PAIRS, 2, hc, 2 * t * hc)

    def pair_blocks(kd, axis):
        r, c = kd.shape[1:]
        kp = kd.reshape(S5_PAIRS, 2, r, c)
        z = jnp.zeros_like(kp[:, 0])
        top = jnp.concatenate([kp[:, 0], z], axis=-1)
        bot = jnp.concatenate([z, kp[:, 1]], axis=-1)
        return jnp.concatenate([top, bot], axis=1)

    wf = jnp.einsum('gpj,gph->gjhp', apow[0][..., t - 1::-1][..., :t], b_bar[0])
    wb = jnp.einsum('gpj,gph->gjhp', apow[1][..., :t], b_bar[1])
    wf = wf.reshape(g, t * hc, p)
    wb = wb.reshape(g, t * hc, p)
    kinds = [jnp.real(wf), jnp.imag(wf), jnp.real(wb), jnp.imag(wb)]
    we = jnp.concatenate([pair_blocks(kd, 0) for kd in kinds], axis=-1).astype(BF16)

    vf = jnp.einsum('ghp,gpt->gpth', cm[0], apow[0][..., 1:t + 1])
    vb = jnp.einsum('ghp,gpt->gpth', cm[1], apow[1][..., t:0:-1])
    vf = vf.reshape(g, p, t * hc)
    vb = vb.reshape(g, p, t * hc)
    vkinds = [jnp.real(vf), -jnp.imag(vf), jnp.real(vb), -jnp.imag(vb)]
    v1 = jnp.concatenate([pair_blocks(kd, 0) for kd in vkinds], axis=1)
    v = jnp.concatenate([v1, v1], axis=1).astype(BF16)

    def lanes(z):
        return jnp.tile(z.reshape(1, g * p), (1, batch))

    at = apow[..., t]
    a_tab = jnp.concatenate([lanes(jnp.real(at[0])), lanes(jnp.imag(at[0])),
                             lanes(jnp.real(at[1])), lanes(jnp.imag(at[1]))], axis=0)
    dvec = jnp.tile(d_skip.astype(F32).reshape(S5_PAIRS, 2, 1, hc), (1, 1, t, 1)).reshape(S5_PAIRS, 1, 2 * t * hc)
    return dict(strip=strip, we=we, v=v, a_tab=a_tab, dvec=dvec)


def _s5_e_kernel(u_ref, we_ref, ref_, imf_, reb_, imb_):
    e = _dot(u_ref[...], we_ref[...])
    ref_[...] = e[:, 0:128]
    imf_[...] = e[:, 128:256]
    reb_[...] = e[:, 256:384]
    imb_[...] = e[:, 384:512]


def _s5_scan_kernel(a_ref, ref_, imf_, reb_, imb_, prf, pif, prb, pib, *, n_rows, n_ctx):
    afr = a_ref[0:1, :]
    afi = a_ref[1:2, :]
    abr = a_ref[2:3, :]
    abi = a_ref[3:4, :]
    zero = jnp.zeros_like(afr)

    def body(s, carry):
        sfr, sfi, sbr, sbi = carry
        nf = s
        nb = jnp.where(s < n_ctx, n_ctx - 1 - s, n_rows - 1 + n_ctx - s)
        prf[pl.ds(nf, 1), :] = sfr
        pif[pl.ds(nf, 1), :] = sfi
        prb[pl.ds(nb, 1), :] = sbr
        pib[pl.ds(nb, 1), :] = sbi
        efr = ref_[pl.ds(nf, 1), :]
        efi = imf_[pl.ds(nf, 1), :]
        ebr = reb_[pl.ds(nb, 1), :]
        ebi = imb_[pl.ds(nb, 1), :]
        nfr = afr * sfr - afi * sfi + efr
        nfi = afr * sfi + afi * sfr + efi
        nbr = abr * sbr - abi * sbi + ebr
        nbi = abr * sbi + abi * sbr + ebi
        return nfr, nfi, nbr, nbi

    lax.fori_loop(0, n_rows, body, (zero, zero, zero, zero))


def _s5_y_kernel(u_ref, strip_ref, v_ref, d_ref, prf, pif, prb, pib, yc_ref, yl_ref, m_scr):
    half = S5_CHUNK * S5_GROUP_CH
    n_ctx = yc_ref.shape[0]

    @pl.when(pl.program_id(1) == 0)
    def _():
        for gi in range(2):
            strip = strip_ref[gi]
            for j in range(S5_CHUNK):
                off = (S5_CHUNK - 1 - j) * S5_GROUP_CH
                win = strip if off == 0 else pltpu.roll(strip, 2 * half - off, axis=1)
                m_scr[gi, j * S5_GROUP_CH:(j + 1) * S5_GROUP_CH, :] = win[:, :half].astype(BF16)

    u = u_ref[...]
    y_intra = jnp.concatenate([_dot(u[:, :half], m_scr[0]), _dot(u[:, half:], m_scr[1])], axis=-1)
    pcat = jnp.concatenate([prf[...], pif[...], prb[...], pib[...]], axis=-1)
    hi, lo = _split_bf16(pcat)
    y_cross = _dot(jnp.concatenate([hi, lo], axis=-1), v_ref[...])
    y = y_intra + y_cross + d_ref[...] * u.astype(F32)
    yc_ref[...] = y[:n_ctx].astype(BF16)
    yl_ref[...] = y[n_ctx:].astype(BF16)


def _s5_core(u, tabs, batch, n_rows, n_ctx):
    width = batch * S5_PAIRS * 128
    cols = 2 * S5_CHUNK * S5_GROUP_CH
    u_spec = pl.BlockSpec((None, None, n_rows, cols), lambda q, b: (q, b, 0, 0))
    st_spec = pl.BlockSpec((n_rows, 128), lambda q, b: (0, b * S5_PAIRS + q))
    st_shape = jax.ShapeDtypeStruct((n_rows, width), F32)
    e4 = pl.pallas_call(
        _s5_e_kernel,
        grid=(S5_PAIRS, batch),
        in_specs=[u_spec, pl.BlockSpec((None, cols, 512), lambda q, b: (q, 0, 0))],
        out_specs=[st_spec] * 4,
        out_shape=[st_shape] * 4,
        compiler_params=_cparams("arbitrary", "arbitrary"),
        name="s5_chunk_states",
    )(u, tabs['we'])
    p4 = pl.pallas_call(
        functools.partial(_s5_scan_kernel, n_rows=n_rows, n_ctx=n_ctx),
        out_shape=[st_shape] * 4,
        compiler_params=pltpu.CompilerParams(vmem_limit_bytes=VMEM_LIMIT_BYTES),
        name="s5_state_scan",
    )(tabs['a_tab'], *e4)
    y = pl.pallas_call(
        _s5_y_kernel,
        grid=(S5_PAIRS, batch),
        in_specs=[
            u_spec,
            pl.BlockSpec((None, 2, S5_GROUP_CH, cols), lambda q, b: (q, 0, 0, 0)),
            pl.BlockSpec((None, cols, cols), lambda q, b: (q, 0, 0)),
            pl.BlockSpec((None, 1, cols), lambda q, b: (q, 0, 0)),
            st_spec, st_spec, st_spec, st_spec,
        ],
        out_specs=[
            pl.BlockSpec((None, None, n_ctx, cols), lambda q, b: (q, b, 0, 0)),
            pl.BlockSpec((None, None, n_rows - n_ctx, cols), lambda q, b: (q, b, 0, 0)),
        ],
        out_shape=[
            jax.ShapeDtypeStruct((S5_PAIRS, batch, n_ctx, cols), BF16),
            jax.ShapeDtypeStruct((S5_PAIRS, batch, n_rows - n_ctx, cols), BF16),
        ],
        scratch_shapes=[pltpu.VMEM((2, cols // 2, cols // 2), BF16)],
        compiler_params=_cparams("arbitrary", "arbitrary"),
        name="s5_outputs",
    )(u, tabs['strip'], tabs['v'], tabs['dvec'], *p4)
    return y


def _s5_to_chunks(s, batch):
    n = s.shape[0] // batch // S5_CHUNK
    v = s.reshape(batch, n, S5_CHUNK, S5_PAIRS, 2, S5_GROUP_CH)
    return v.transpose(3, 0, 1, 4, 2, 5).reshape(S5_PAIRS, batch, n, 2 * S5_CHUNK * S5_GROUP_CH)


def _s5_from_chunks(y, batch):
    n = y.shape[2]
    v = y.reshape(S5_PAIRS, batch, n, 2, S5_CHUNK, S5_GROUP_CH)
    return v.transpose(1, 2, 4, 0, 3, 5).reshape(batch * n * S5_CHUNK, BRANCH_W)


def _s5_mixer(s_lat, s_ctx, tabs, batch):
    ul = _s5_to_chunks(s_lat, batch)
    uc = _s5_to_chunks(s_ctx, batch)
    n_ctx = uc.shape[2]
    u = jnp.concatenate([uc, ul], axis=2)
    yc, yl = _s5_core(u, tabs, batch, u.shape[2], n_ctx)
    return _s5_from_chunks(yl, batch), _s5_from_chunks(yc, batch)


def _ret_tables(ret_decay):
    c = RET_CHUNK
    lg = jax.nn.log_sigmoid(ret_decay.astype(F32))
    lane_h = jnp.repeat(jnp.arange(RET_HEADS), RET_DIM)
    lgl = lg[:, lane_h]
    pos = jnp.arange(c, dtype=F32)[:, None]
    qd = jnp.stack([jnp.exp((pos + 1.0) * lgl[0][None]), jnp.exp((c - pos) * lgl[1][None])])
    kd = jnp.stack([jnp.exp((c - 1.0 - pos) * lgl[0][None]), jnp.exp(pos * lgl[1][None])])
    bmask = (lane_h[:, None] == lane_h[None, :]).astype(F32)
    cd = jnp.exp(c * lgl)[:, :, None] * bmask[None]
    diff = pos - pos.T
    dm = []
    for h in range(RET_HEADS):
        fw = jnp.where(diff >= 0, jnp.exp(jnp.maximum(diff, 0.0) * lg[0, h]), 0.0)
        bw = jnp.where(diff <= 0, jnp.exp(jnp.maximum(-diff, 0.0) * lg[1, h]), 0.0)
        dm.append(fw + bw)
    dm = jnp.concatenate(dm, axis=0)
    hmask = (jnp.arange(RET_HEADS)[:, None] == lane_h[None, :]).astype(F32)
    return dict(qd=qd, kd=kd, cd=cd, bmask=bmask, dm=dm, hmask=hmask)


def _rope_tables(n_tokens):
    t = np.arange(n_tokens)
    row = (t // GRID_W).astype(np.float64)
    col = (t % GRID_W).astype(np.float64)
    n_freq = RET_DIM // 4
    inv_freq = 1.0 / (ROPE_BASE ** (np.arange(n_freq, dtype=np.float64) / n_freq))
    ang = np.concatenate([row[:, None] * inv_freq, col[:, None] * inv_freq], axis=-1)
    cos = np.cos(ang)
    sin = np.sin(ang)
    cos_t = np.tile(np.concatenate([cos, cos], axis=-1), (1, RET_HEADS))
    sin_t = np.tile(np.concatenate([-sin, sin], axis=-1), (1, RET_HEADS))
    half = RET_DIM // 2
    perm = np.arange(BRANCH_W) ^ half
    swap = np.zeros((BRANCH_W, BRANCH_W), np.float32)
    swap[perm, np.arange(BRANCH_W)] = 1.0
    return jnp.asarray(cos_t, F32), jnp.asarray(sin_t, F32), jnp.asarray(swap, BF16)


def _ret_chunk(q, k, v, s, qd, kd, cd, bmask, dm, hmask, with_intra):
    cross = _dot((q * qd).astype(BF16), s.astype(BF16))
    s_new = cd * s + bmask * _dot_tn((k * kd).astype(BF16), v)
    if not with_intra:
        return cross, s_new
    qb = q.astype(BF16)
    kb = k.astype(BF16)
    qs = jnp.concatenate([qb * hmask[h:h + 1].astype(BF16) for h in range(RET_HEADS)], axis=0)
    scores = _dot_nt(qs, kb) * dm
    ov = _dot(scores.astype(BF16), v)
    c = q.shape[0]
    inner = ov[0:c] * hmask[0:1]
    for h in range(1, RET_HEADS):
        inner = inner + ov[h * c:(h + 1) * c] * hmask[h:h + 1]
    return inner + cross, s_new


def _ret_kernel(q_ref, k_ref, v_ref, qc_ref, kc_ref, vc_ref, cos_ref, sin_ref, swap_ref,
                qd_ref, kd_ref, cd_ref, bm_ref, dm_ref, hm_ref, o_ref, oc_ref, s_scr, *, n_chunks, n_ctx_chunks):
    dirn = pl.program_id(1)
    i = pl.program_id(2)
    c = RET_CHUNK
    k_scale = RET_DIM ** -0.5
    bmask = bm_ref[...]
    dm = dm_ref[...]
    hmask = hm_ref[...]

    def run(d):
        qd = qd_ref[d]
        kd = kd_ref[d]
        cd = cd_ref[d]
        intra = d == 0

        @pl.when(i == 0)
        def _():
            s = jnp.zeros((BRANCH_W, BRANCH_W), F32)
            order = range(n_ctx_chunks) if d == 0 else range(n_ctx_chunks - 1, -1, -1)
            for cc in order:
                sl = slice(cc * c, (cc + 1) * c)
                o, s = _ret_chunk(qc_ref[sl, :].astype(F32), kc_ref[sl, :].astype(F32) * k_scale, vc_ref[sl, :],
                                  s, qd, kd, cd, bmask, dm, hmask, intra)
                oc_ref[sl, :] = o
            s_scr[...] = s

        swap = swap_ref[...]
        order = range(n_chunks) if d == 0 else range(n_chunks - 1, -1, -1)
        s = s_scr[...]
        for cc in order:
            sl = slice(cc * c, (cc + 1) * c)
            cos = cos_ref[sl, :]
            sin = sin_ref[sl, :]
            qb = q_ref[sl, :]
            kb = k_ref[sl, :]
            q = qb.astype(F32) * cos + _dot(qb, swap) * sin
            k = (kb.astype(F32) * cos + _dot(kb, swap) * sin) * k_scale
            o, s = _ret_chunk(q, k, v_ref[sl, :], s, qd, kd, cd, bmask, dm, hmask, intra)
            o_ref[sl, :] = o
        s_scr[...] = s

    @pl.when(dirn == 0)
    def _():
        run(0)

    @pl.when(dirn == 1)
    def _():
        run(1)


def _retention(proj_l, proj_c, tabs, rope, batch, seq_len, ctx_len):
    n_chunks = 8
    blk = n_chunks * RET_CHUNK
    nblk = seq_len // blk
    cos_t, sin_t, swap = rope

    def pos(d, i):
        return i + d * (nblk - 1 - 2 * i)

    def lat(col):
        return pl.BlockSpec((blk, BRANCH_W), lambda b, d, i: (b * nblk + pos(d, i), col))

    def ctx(col):
        return pl.BlockSpec((ctx_len, BRANCH_W), lambda b, d, i: (b, col))

    def const(shape):
        return pl.BlockSpec(shape, lambda b, d, i: (0,) * len(shape))

    tab_spec = pl.BlockSpec((blk, BRANCH_W), lambda b, d, i: (pos(d, i), 0))
    kern = functools.partial(_ret_kernel, n_chunks=n_chunks, n_ctx_chunks=ctx_len // RET_CHUNK)
    c = RET_CHUNK
    o, oc = pl.pallas_call(
        kern,
        grid=(batch, 2, nblk),
        in_specs=[
            lat(COL_RQ), lat(COL_RK), lat(COL_RV), ctx(COL_RQ), ctx(COL_RK), ctx(COL_RV),
            tab_spec, tab_spec, const((BRANCH_W, BRANCH_W)),
            const((2, c, BRANCH_W)), const((2, c, BRANCH_W)), const((2, BRANCH_W, BRANCH_W)),
            const((BRANCH_W, BRANCH_W)), const((RET_HEADS * c, c)), const((RET_HEADS, BRANCH_W)),
        ],
        out_specs=[
            pl.BlockSpec((None, blk, BRANCH_W), lambda b, d, i: (d, b * nblk + pos(d, i), 0)),
            pl.BlockSpec((None, ctx_len, BRANCH_W), lambda b, d, i: (d, b, 0)),
        ],
        out_shape=[
            jax.ShapeDtypeStruct((2, batch * seq_len, BRANCH_W), F32),
            jax.ShapeDtypeStruct((2, batch * ctx_len, BRANCH_W), F32),
        ],
        scratch_shapes=[pltpu.VMEM((BRANCH_W, BRANCH_W), F32)],
        compiler_params=_cparams("arbitrary", "arbitrary", "arbitrary"),
        name="retention",
    )(proj_l, proj_l, proj_l, proj_c, proj_c, proj_c, cos_t, sin_t, swap,
      tabs['qd'], tabs['kd'], tabs['cd'], tabs['bmask'], tabs['dm'], tabs['hmask'])
    return o, oc


def _na_tables(rpb):
    kr, kw = NA_WIN_ROWS, NA_WIN_COLS
    col = np.arange(GRID_W)
    col_start = np.clip(col - kw // 2, 0, GRID_W - kw)
    in_win = (col[None, :] >= col_start[:, None]) & (col[None, :] < col_start[:, None] + kw)
    dc = np.clip(col[None, :] - col[:, None], -(kw - 1), kw - 1) + (kw - 1)
    var = np.arange(kr)[:, None] + np.arange(kr)[None, :]
    pick_r = (var[:, :, None] == np.arange(2 * kr - 1)[None, None, :]).astype(np.float32)
    pick_c = (dc[:, :, None] == np.arange(2 * kw - 1)[None, None, :]).astype(np.float32)
    bias = jnp.einsum('vir,hrc,qkc->vhqik', jnp.asarray(pick_r), rpb.astype(F32), jnp.asarray(pick_c),
                      precision=lax.Precision.HIGHEST)
    bias = jnp.where(jnp.asarray(in_win)[None, None, :, None, :], bias, NEG_BIG)
    bias = bias.reshape(kr, NA_HEADS * GRID_W, kr * GRID_W)
    lane_h = np.repeat(np.arange(NA_HEADS), NA_DIM)
    hmask = (np.arange(NA_HEADS)[:, None] == lane_h[None, :]).astype(np.float32)
    return bias, jnp.asarray(hmask, F32)


def _attend(qs, keys, vals, bias, kc, vc):
    s_ctx = _dot_nt(qs, kc)
    m = jnp.max(s_ctx, axis=-1, keepdims=True)
    if keys is not None:
        s_band = _dot_nt(qs, keys) + bias
        m = jnp.maximum(m, jnp.max(s_band, axis=-1, keepdims=True))
        p_band = jnp.exp(s_band - m)
    p_ctx = jnp.exp(s_ctx - m)
    l = jnp.sum(p_ctx, axis=-1, keepdims=True)
    o = _dot(p_ctx.astype(BF16), vc)
    if keys is not None:
        l = l + jnp.sum(p_band, axis=-1, keepdims=True)
        o = o + _dot(p_band.astype(BF16), vals)
    return o / l


def _stack_heads(q, hmask_scaled):
    return jnp.concatenate([q * hmask_scaled[h:h + 1] for h in range(NA_HEADS)], axis=0)


def _unstack_heads(o, hmask, n):
    out = o[0:n] * hmask[0:1]
    for h in range(1, NA_HEADS):
        out = out + o[h * n:(h + 1) * n] * hmask[h:h + 1]
    return out


def _na_kernel(q_ref, k_ref, v_ref, kc_ref, vc_ref, bias_ref, hm_ref, o_ref, *, n_grid_rows):
    i = pl.program_id(1)
    hmask = hm_ref[...]
    hms = (hmask * (NA_DIM ** -0.5)).astype(BF16)
    kc = kc_ref[...]
    vc = vc_ref[...]
    band = NA_WIN_ROWS * GRID_W
    for rr in range(NA_QROWS):
        r = i * NA_QROWS + rr
        rs = jnp.clip(r - NA_WIN_ROWS // 2, 0, n_grid_rows - NA_WIN_ROWS)
        var = rs - r + (NA_WIN_ROWS - 1)
        start = pl.multiple_of(rs * GRID_W, GRID_W)
        keys = k_ref[pl.ds(start, band), :]
        vals = v_ref[pl.ds(start, band), :]
        qs = _stack_heads(q_ref[rr * GRID_W:(rr + 1) * GRID_W, :], hms)
        o = _attend(qs, keys, vals, bias_ref[var], kc, vc)
        o_ref[rr * GRID_W:(rr + 1) * GRID_W, :] = _unstack_heads(o, hmask, GRID_W).astype(BF16)


def _na_ctx_kernel(q_ref, kc_ref, vc_ref, hm_ref, o_ref):
    hmask = hm_ref[...]
    hms = (hmask * (NA_DIM ** -0.5)).astype(BF16)
    n = q_ref.shape[0]
    o = _attend(_stack_heads(q_ref[...], hms), None, None, None, kc_ref[...], vc_ref[...])
    o_ref[...] = _unstack_heads(o, hmask, n).astype(BF16)


def _neighborhood(proj_l, proj_c, bias, hmask, batch, seq_len, ctx_len, need_ctx_out):
    rows = seq_len // GRID_W
    qblk = NA_QROWS * GRID_W
    nq = seq_len // qblk
    out_l = pl.pallas_call(
        functools.partial(_na_kernel, n_grid_rows=rows),
        grid=(batch, nq),
        in_specs=[
            pl.BlockSpec((qblk, BRANCH_W), lambda b, i: (b * nq + i, COL_NQ)),
            pl.BlockSpec((seq_len, BRANCH_W), lambda b, i: (b, COL_NK)),
            pl.BlockSpec((seq_len, BRANCH_W), lambda b, i: (b, COL_NV)),
            pl.BlockSpec((ctx_len, BRANCH_W), lambda b, i: (b, COL_NK)),
            pl.BlockSpec((ctx_len, BRANCH_W), lambda b, i: (b, COL_NV)),
            pl.BlockSpec(bias.shape, lambda b, i: (0, 0, 0)),
            pl.BlockSpec(hmask.shape, lambda b, i: (0, 0)),
        ],
        out_specs=pl.BlockSpec((qblk, BRANCH_W), lambda b, i: (b * nq + i, 0)),
        out_shape=jax.ShapeDtypeStruct((batch * seq_len, BRANCH_W), BF16),
        compiler_params=_cparams("arbitrary", "arbitrary"),
        name="neighborhood_attn",
    )(proj_l, proj_l, proj_l, proj_c, proj_c, bias, hmask)
    out_c = None
    if need_ctx_out:
        out_c = pl.pallas_call(
            _na_ctx_kernel,
            grid=(batch,),
            in_specs=[
                pl.BlockSpec((ctx_len, BRANCH_W), lambda b: (b, COL_NQ)),
                pl.BlockSpec((ctx_len, BRANCH_W), lambda b: (b, COL_NK)),
                pl.BlockSpec((ctx_len, BRANCH_W), lambda b: (b, COL_NV)),
                pl.BlockSpec(hmask.shape, lambda b: (0, 0)),
            ],
            out_specs=pl.BlockSpec((ctx_len, BRANCH_W), lambda b: (b, 0)),
            out_shape=jax.ShapeDtypeStruct((batch * ctx_len, BRANCH_W), BF16),
            compiler_params=_cparams("arbitrary"),
            name="context_attn",
        )(proj_c, proj_c, proj_c, hmask)
    return out_l, out_c


def _merge_kernel(x_ref, mod_ref, g_ref, gt0, gt1, gt2, gt3, a_ref, s5_ref, ro_ref, rg_ref, na_ref,
                  wglu_ref, bglu_ref, gn_ref, avg_ref, wb_ref, wo_ref, o_ref, *, tiles_per_mod, mod_base):
    i = pl.program_id(0)
    _, _, gate_a = _mod_rows(mod_ref, i, tiles_per_mod, mod_base, 0)
    z = _gelu_tanh(s5_ref[...].astype(F32)).astype(BF16)
    zf = z.astype(F32)
    b_s5 = (zf * _sigmoid(_dot(z, wglu_ref[...]) + bglu_ref[...])).astype(BF16)
    o = ro_ref[0] + ro_ref[1]
    avg = avg_ref[...]
    hi, lo = _split_bf16(o)
    mu = _dot(hi, avg) + _dot(lo, avg)
    dlt = o - mu
    hi, lo = _split_bf16(dlt * dlt)
    var = _dot(hi, avg) + _dot(lo, avg)
    hn = dlt * lax.rsqrt(var + EPS) * gn_ref[...]
    b_ret = (_silu(rg_ref[...].astype(F32)) * hn).astype(BF16)
    outs = (a_ref[...], b_s5, b_ret, na_ref[...])
    gates = (gt0, gt1, gt2, gt3)
    y = (1.0 + jnp.tanh(gates[0][...].astype(F32))) * _dot(outs[0], wb_ref[0])
    for b in range(1, N_BRANCH):
        y = y + (1.0 + jnp.tanh(gates[b][...].astype(F32))) * _dot(outs[b], wb_ref[b])
    yo = _dot(y.astype(BF16), wo_ref[...])
    o_ref[...] = x_ref[...] + gate_a * _rms(yo, g_ref[...])


def _merge(x, mod, g1, proj, a, s5y, ret_o, na, lw, *, rows_per_mod, mod_base):
    rows, d = x.shape
    tm = min(512, rows)
    nt = rows // tm

    def row(shape, col=0):
        return pl.BlockSpec(shape, lambda i: (i, col))

    def const(arr):
        return pl.BlockSpec(arr.shape, lambda i: (0,) * arr.ndim)

    kern = functools.partial(_merge_kernel, tiles_per_mod=max(rows_per_mod // tm, 1), mod_base=mod_base)
    ins = [x, mod, g1.reshape(1, d), proj, proj, proj, proj, a, s5y, ret_o, proj, na,
           lw['w_glu'], lw['b_glu'], lw['ret_gn'], lw['avg'], lw['w_branch'], lw['w_out']]
    specs = [
        row((tm, d)), const(mod), pl.BlockSpec((1, d), lambda i: (0, 0)),
        row((tm, d), 0), row((tm, d), 1), row((tm, d), 2), row((tm, d), 3),
        row((tm, BRANCH_W)), row((tm, BRANCH_W)),
        pl.BlockSpec((2, tm, BRANCH_W), lambda i: (0, i, 0)),
        row((tm, BRANCH_W), COL_RG), row((tm, BRANCH_W)),
        const(lw['w_glu']), const(lw['b_glu']), const(lw['ret_gn']), const(lw['avg']),
        const(lw['w_branch']), const(lw['w_out']),
    ]
    return pl.pallas_call(
        kern,
        grid=(nt,),
        in_specs=specs,
        out_specs=row((tm, d)),
        out_shape=jax.ShapeDtypeStruct((rows, d), F32),
        compiler_params=_cparams("arbitrary"),
        name="merge_out",
    )(*ins)


def _ffn_kernel(x_ref, mod_ref, g2_ref, g3_ref, wg_ref, wu_ref, wd_ref, o_ref, h_scr, acc_scr,
                *, tiles_per_mod, mod_base, n_f):
    i = pl.program_id(0)
    f = pl.program_id(1)

    @pl.when(f == 0)
    def _():
        sh, sc, _ = _mod_rows(mod_ref, i, tiles_per_mod, mod_base, 3)
        h_scr[...] = (_rms(x_ref[...], g2_ref[...]) * (1.0 + sc) + sh).astype(BF16)
        acc_scr[...] = jnp.zeros_like(acc_scr)

    h = h_scr[...]
    act = (_silu(_dot(h, wg_ref[...])) * _dot(h, wu_ref[...])).astype(BF16)
    acc_scr[...] += _dot(act, wd_ref[...])

    @pl.when(f == n_f - 1)
    def _():
        _, _, gate_f = _mod_rows(mod_ref, i, tiles_per_mod, mod_base, 3)
        o_ref[...] = x_ref[...] + gate_f * _rms(acc_scr[...], g3_ref[...])


def _ffn_dense(x, mod, g2, g3, wg, wu, wd, *, rows_per_mod, mod_base):
    rows, d = x.shape
    d_ff = wg.shape[1]
    tm = min(512, rows)
    tf = d_ff // 2 if (d_ff // 2) % 128 == 0 else d_ff
    n_f = d_ff // tf
    kern = functools.partial(_ffn_kernel, tiles_per_mod=max(rows_per_mod // tm, 1), mod_base=mod_base, n_f=n_f)
    return pl.pallas_call(
        kern,
        grid=(rows // tm, n_f),
        in_specs=[
            pl.BlockSpec((tm, d), lambda i, f: (i, 0)),
            pl.BlockSpec(mod.shape, lambda i, f: (0, 0)),
            pl.BlockSpec((1, d), lambda i, f: (0, 0)),
            pl.BlockSpec((1, d), lambda i, f: (0, 0)),
            pl.BlockSpec((d, tf), lambda i, f: (0, f)),
            pl.BlockSpec((d, tf), lambda i, f: (0, f)),
            pl.BlockSpec((tf, d), lambda i, f: (f, 0)),
        ],
        out_specs=pl.BlockSpec((tm, d), lambda i, f: (i, 0)),
        out_shape=jax.ShapeDtypeStruct((rows, d), F32),
        scratch_shapes=[pltpu.VMEM((tm, d), BF16), pltpu.VMEM((tm, d), F32)],
        compiler_params=_cparams("arbitrary", "arbitrary"),
        name="ffn_dense",
    )(x, mod, g2.reshape(1, d), g3.reshape(1, d), wg, wu, wd)


def _router_kernel(x_ref, mod_ref, g2_ref, wr_ref, br_ref, h_ref, comb_ref, *, tiles_per_mod, mod_base):
    i = pl.program_id(0)
    sh, sc, _ = _mod_rows(mod_ref, i, tiles_per_mod, mod_base, 3)
    h = _rms(x_ref[...], g2_ref[...]) * (1.0 + sc) + sh
    h_ref[...] = _pack_pairs(h)
    h_hi, h_lo = _split_bf16(h)
    w_hi, w_lo = _split_bf16(wr_ref[...])
    logits = _dot(h_hi, w_hi) + _dot(h_lo, w_hi) + _dot(h_hi, w_lo) + br_ref[...]
    lane = lax.broadcasted_iota(jnp.int32, logits.shape, 1)
    v1 = jnp.max(logits, axis=-1, keepdims=True)
    i1 = jnp.min(jnp.where(logits == v1, lane, 128), axis=-1, keepdims=True)
    rest = jnp.where(lane == i1, NEG_BIG, logits)
    v2 = jnp.max(rest, axis=-1, keepdims=True)
    i2 = jnp.min(jnp.where(rest == v2, lane, 128), axis=-1, keepdims=True)
    e = jnp.exp(v2 - v1)
    w1 = 1.0 / (1.0 + e)
    w2 = e / (1.0 + e)
    meta = jnp.where(lane == 0, i1.astype(F32), 0.0) + jnp.where(lane == 1, i2.astype(F32), 0.0)
    comb_ref[...] = meta + jnp.where(lane == 2, w1, 0.0) + jnp.where(lane == 3, w2, 0.0)


def _router(x, mod, g2, w_router, b_router, *, rows_per_mod, mod_base):
    rows, d = x.shape
    tm = min(512, rows)
    wr = jnp.zeros((d, 128), F32).at[:, :N_EXPERTS].set(w_router)
    br = jnp.full((1, 128), NEG_BIG, F32).at[0, :N_EXPERTS].set(b_router)
    kern = functools.partial(_router_kernel, tiles_per_mod=max(rows_per_mod // tm, 1), mod_base=mod_base)
    return pl.pallas_call(
        kern,
        grid=(rows // tm,),
        in_specs=[
            pl.BlockSpec((tm, d), lambda i: (i, 0)),
            pl.BlockSpec(mod.shape, lambda i: (0, 0)),
            pl.BlockSpec((1, d), lambda i: (0, 0)),
            pl.BlockSpec((d, 128), lambda i: (0, 0)),
            pl.BlockSpec((1, 128), lambda i: (0, 0)),
        ],
        out_specs=[pl.BlockSpec((tm, d // 2), lambda i: (i, 0)), pl.BlockSpec((tm, 128), lambda i: (i, 0))],
        out_shape=[jax.ShapeDtypeStruct((rows, d // 2), jnp.int32), jax.ShapeDtypeStruct((rows, 128), F32)],
        compiler_params=_cparams("arbitrary"),
        name="moe_router",
    )(x, mod, g2.reshape(1, d), wr, br)


def _sc_gather(table, idx):
    n_idx = idx.shape[0]
    width = table.shape[1]
    per_worker = n_idx // SC_WORKERS
    chunk_rows = math.gcd(per_worker, SC_GATHER_ROWS)
    n_chunks = per_worker // chunk_rows
    assert per_worker * SC_WORKERS == n_idx and chunk_rows % 8 == 0
    mesh = plsc.VectorSubcoreMesh(core_axis_name="c", subcore_axis_name="s")

    assert n_chunks % 2 == 0
    buf = [pltpu.VMEM((chunk_rows,), jnp.int32), pltpu.VMEM((chunk_rows, width), table.dtype),
           pltpu.SemaphoreType.DMA, pltpu.SemaphoreType.DMA]

    @functools.partial(
        pl.kernel, mesh=mesh,
        out_type=jax.ShapeDtypeStruct((n_idx, width), table.dtype),
        scratch_types=buf + buf,
        name="sc_row_gather",
    )
    def gather(table_hbm, idx_hbm, out_hbm, idx0, rows0, g0, w0, idx1, rows1, g1, w1):
        wid = lax.axis_index("s") * SC_CORES + lax.axis_index("c")
        base = wid * per_worker
        slots = ((idx0, rows0, g0, w0), (idx1, rows1, g1, w1))

        def fetch(j, slot):
            idx_v, rows_v, g, _ = slots[slot]
            pltpu.sync_copy(idx_hbm.at[pl.ds(base + j * chunk_rows, chunk_rows)], idx_v)
            pltpu.make_async_copy(table_hbm.at[idx_v], rows_v, g).start()

        def store(j, slot):
            idx_v, rows_v, g, w = slots[slot]
            pltpu.make_async_copy(table_hbm.at[idx_v], rows_v, g).wait()
            pltpu.make_async_copy(rows_v, out_hbm.at[pl.ds(base + j * chunk_rows, chunk_rows)], w).start()

        def drain(j, slot):
            _, rows_v, _, w = slots[slot]
            pltpu.make_async_copy(rows_v, out_hbm.at[pl.ds(base + j * chunk_rows, chunk_rows)], w).wait()

        fetch(0, 0)

        @pl.loop(0, n_chunks // 2)
        def _(jj):
            j = 2 * jj

            @pl.when(jj > 0)
            def _():
                drain(j - 1, 1)

            fetch(j + 1, 1)
            store(j, 0)

            @pl.when(j + 2 < n_chunks)
            def _():
                drain(j, 0)
                fetch(j + 2, 0)

            store(j + 1, 1)

        drain(n_chunks - 2, 0)
        drain(n_chunks - 1, 1)

    return gather(table, idx)


def _sc_scatter(table, idx, n_out):
    n_idx = idx.shape[0]
    rows, width = table.shape
    per_worker = n_idx // SC_WORKERS
    chunk_rows = math.gcd(per_worker, SC_GATHER_ROWS)
    n_chunks = per_worker // chunk_rows
    assert per_worker * SC_WORKERS == n_idx and chunk_rows % 8 == 0 and rows % per_worker == 0
    mesh = plsc.VectorSubcoreMesh(core_axis_name="c", subcore_axis_name="s")

    assert n_chunks % 2 == 0
    buf = [pltpu.VMEM((chunk_rows,), jnp.int32), pltpu.VMEM((chunk_rows, width), table.dtype),
           pltpu.SemaphoreType.DMA, pltpu.SemaphoreType.DMA]

    @functools.partial(
        pl.kernel, mesh=mesh,
        out_type=jax.ShapeDtypeStruct((n_out, width), table.dtype),
        scratch_types=buf + buf,
        name="sc_row_scatter",
    )
    def scatter(table_hbm, idx_hbm, out_hbm, idx0, rows0, l0, w0, idx1, rows1, l1, w1):
        wid = lax.axis_index("s") * SC_CORES + lax.axis_index("c")
        base = wid * per_worker
        slots = ((idx0, rows0, l0, w0), (idx1, rows1, l1, w1))

        def src(j):
            return table_hbm.at[pl.ds(lax.rem(base + j * chunk_rows, rows), chunk_rows)]

        def fetch(j, slot):
            idx_v, rows_v, l, _ = slots[slot]
            pltpu.sync_copy(idx_hbm.at[pl.ds(base + j * chunk_rows, chunk_rows)], idx_v)
            pltpu.make_async_copy(src(j), rows_v, l).start()

        def store(j, slot):
            idx_v, rows_v, l, w = slots[slot]
            pltpu.make_async_copy(src(j), rows_v, l).wait()
            pltpu.make_async_copy(rows_v, out_hbm.at[idx_v], w).start()

        def drain(slot):
            idx_v, rows_v, _, w = slots[slot]
            pltpu.make_async_copy(rows_v, out_hbm.at[idx_v], w).wait()

        fetch(0, 0)

        @pl.loop(0, n_chunks // 2)
        def _(jj):
            j = 2 * jj

            @pl.when(jj > 0)
            def _():
                drain(1)

            fetch(j + 1, 1)
            store(j, 0)

            @pl.when(j + 2 < n_chunks)
            def _():
                drain(0)
                fetch(j + 2, 0)

            store(j + 1, 1)

        drain(0)
        drain(1)

    return scatter(table, idx)


def _moe_plan(meta, rows):
    tile = MOE_ROW_TILE
    n_tiles = (2 * rows) // tile + N_EXPERTS
    n_slots = n_tiles * tile
    experts = jnp.concatenate([meta[:, 0], meta[:, 1]]).astype(jnp.int32)
    onehot = (experts[:, None] == jnp.arange(N_EXPERTS)[None, :]).astype(jnp.int32)
    csum = jnp.cumsum(onehot, axis=0)
    counts = csum[-1]
    rank = jnp.sum(onehot * csum, axis=1) - 1
    padded = ((counts + tile - 1) // tile) * tile
    ends = jnp.cumsum(padded)
    starts = ends - padded
    pos = jnp.sum(onehot * starts[None, :], axis=1) + rank
    tile_start = jnp.arange(n_tiles, dtype=jnp.int32) * tile
    used = tile_start < ends[-1]
    tile_e = jnp.minimum(jnp.sum((tile_start[:, None] >= ends[None, :]).astype(jnp.int32), axis=1), N_EXPERTS - 1)
    last_e = jnp.max(jnp.where(used, tile_e, 0))
    tile_e = jnp.where(used, tile_e, last_e)
    valid_end = jnp.sum((tile_e[:, None] == jnp.arange(N_EXPERTS)[None, :]) * (starts + counts)[None, :], axis=1)
    n_valid = jnp.where(used, jnp.clip(valid_end - tile_start, 0, tile), 0).astype(jnp.int32)
    return pos.astype(jnp.int32), n_slots, tile_e.astype(jnp.int32), n_valid


def _moe_group_kernel(eid_ref, nval_ref, hs_ref, wg_ref, wu_ref, wd_ref, y_ref, h_scr, acc_scr, *, n_f):
    w = pl.program_id(0)
    f = pl.program_id(1)
    nv = nval_ref[w]

    def run(n_rows):
        rows = slice(0, n_rows)

        @pl.when(f == 0)
        def _():
            hv = _unpack_pairs(hs_ref[rows, :])
            row = lax.broadcasted_iota(jnp.int32, hv.shape, 0)
            h_scr[rows, :] = jnp.where(row < nv, hv, 0.0).astype(BF16)

        h = h_scr[rows, :]
        gate = _dot(h, wg_ref[...].astype(BF16))
        up = _dot(h, wu_ref[...].astype(BF16))
        part = _dot((_silu(gate) * up).astype(BF16), wd_ref[...].astype(BF16))

        @pl.when(f == 0)
        def _():
            acc_scr[rows, :] = part

        @pl.when(f > 0)
        def _():
            acc_scr[rows, :] += part

        @pl.when(f == n_f - 1)
        def _():
            y_ref[rows, :] = _pack_pairs(acc_scr[rows, :])

    half = hs_ref.shape[0] // 2

    @pl.when(nv > half)
    def _():
        run(hs_ref.shape[0])

    @pl.when((nv > 0) & (nv <= half))
    def _():
        run(half)


def _moe_grouped(hs, tile_e, n_valid, wg, wu, wd):
    n_slots = hs.shape[0]
    d = wg.shape[1]
    d_ff = wg.shape[2]
    tile = MOE_ROW_TILE
    tf = MOE_FF_TILE
    n_f = d_ff // tf

    def f_idx(f, nval, w):
        return jnp.where(nval[w] > 0, f, n_f - 1)

    grid_spec = pltpu.PrefetchScalarGridSpec(
        num_scalar_prefetch=2,
        grid=(n_slots // tile, n_f),
        in_specs=[
            pl.BlockSpec((tile, d // 2), lambda w, f, eid, nval: (w, 0)),
            pl.BlockSpec((None, d, tf), lambda w, f, eid, nval: (eid[w], 0, f_idx(f, nval, w))),
            pl.BlockSpec((None, d, tf), lambda w, f, eid, nval: (eid[w], 0, f_idx(f, nval, w))),
            pl.BlockSpec((None, tf, d), lambda w, f, eid, nval: (eid[w], f_idx(f, nval, w), 0)),
        ],
        out_specs=pl.BlockSpec((tile, d // 2), lambda w, f, eid, nval: (w, 0)),
        scratch_shapes=[pltpu.VMEM((tile, d), BF16), pltpu.VMEM((tile, d), F32)],
    )
    return pl.pallas_call(
        functools.partial(_moe_group_kernel, n_f=n_f),
        grid_spec=grid_spec,
        out_shape=jax.ShapeDtypeStruct((n_slots, d // 2), jnp.int32),
        compiler_params=_cparams("arbitrary", "arbitrary"),
        name="moe_experts",
    )(tile_e, n_valid, hs, wg, wu, wd)


def _moe_out_kernel(x_ref, y1_ref, y2_ref, meta_ref, mod_ref, g3_ref, o_ref, *, tiles_per_mod, mod_base):
    i = pl.program_id(0)
    _, _, gate_f = _mod_rows(mod_ref, i, tiles_per_mod, mod_base, 3)
    meta = meta_ref[...]
    y = meta[:, 2:3] * _unpack_pairs(y1_ref[...]) + meta[:, 3:4] * _unpack_pairs(y2_ref[...])
    o_ref[...] = x_ref[...] + gate_f * _rms(y, g3_ref[...])


def _moe_combine(x, yg, meta, mod, g3, *, rows_per_mod, mod_base):
    rows, d = x.shape
    tm = min(512, rows)
    nt = rows // tm
    kern = functools.partial(_moe_out_kernel, tiles_per_mod=max(rows_per_mod // tm, 1), mod_base=mod_base)
    return pl.pallas_call(
        kern,
        grid=(nt,),
        in_specs=[
            pl.BlockSpec((tm, d), lambda i: (i, 0)),
            pl.BlockSpec((tm, d // 2), lambda i: (i, 0)),
            pl.BlockSpec((tm, d // 2), lambda i: (nt + i, 0)),
            pl.BlockSpec((tm, 128), lambda i: (i, 0)),
            pl.BlockSpec(mod.shape, lambda i: (0, 0)),
            pl.BlockSpec((1, d), lambda i: (0, 0)),
        ],
        out_specs=pl.BlockSpec((tm, d), lambda i: (i, 0)),
        out_shape=jax.ShapeDtypeStruct((rows, d), F32),
        compiler_params=_cparams("arbitrary"),
        name="moe_combine",
    )(x, yg, yg, meta, mod, g3.reshape(1, d))


def _moe_sparse(x, h, meta, mod, g3, wg, wu, wd, *, rows_per_mod, mod_base):
    rows = x.shape[0]
    pos, n_slots, tile_e, n_valid = _moe_plan(meta, rows)
    hs = _sc_scatter(h, pos, n_slots)
    ys = _moe_grouped(hs, tile_e, n_valid, wg, wu, wd)
    yg = _sc_gather(ys, pos)
    return _moe_combine(x, yg, meta, mod, g3, rows_per_mod=rows_per_mod, mod_base=mod_base)


def _cast_kernel(w_ref, o_ref, *, scale):
    w = w_ref[...]
    o_ref[...] = (w if scale == 1.0 else w * scale).astype(BF16)


def _cast_bf16(w_stack, layer, scale=1.0):
    squeeze = w_stack.ndim == 3
    w4 = w_stack[:, None] if squeeze else w_stack
    _, n_e, k, n = w4.shape
    bk = min(k, 256)
    out = pl.pallas_call(
        functools.partial(_cast_kernel, scale=scale),
        grid=(n_e, k // bk),
        in_specs=[pl.BlockSpec((None, None, bk, n), lambda e, i: (layer, e, i, 0))],
        out_specs=pl.BlockSpec((None, bk, n), lambda e, i: (e, i, 0)),
        out_shape=jax.ShapeDtypeStruct((n_e, k, n), BF16),
        compiler_params=_cparams("arbitrary", "arbitrary"),
        name="cast_weights",
    )(w4)
    return out[0] if squeeze else out


def _permute_w_in(w_in_stack, layer):
    _, k, n = w_in_stack.shape
    n_blocks = n // BRANCH_W
    shift = 9
    n_gate_blocks = N_BRANCH * D_MODEL // BRANCH_W

    def permute_kernel(w_ref, o_ref):
        scale = jnp.where(pl.program_id(0) < n_gate_blocks, 0.5, 1.0)
        o_ref[...] = (w_ref[...] * scale).astype(BF16)

    return pl.pallas_call(
        permute_kernel,
        grid=(n_blocks,),
        in_specs=[pl.BlockSpec((None, k, BRANCH_W), lambda j: (layer, 0, (j + shift) % n_blocks))],
        out_specs=pl.BlockSpec((k, BRANCH_W), lambda j: (0, j)),
        out_shape=jax.ShapeDtypeStruct((k, n), BF16),
        compiler_params=_cparams("arbitrary"),
        name="cast_permute_w_in",
    )(w_in_stack)


def kernel(x, c, ctx, c_ctx, w_mod, b_mod, norm_g, w_in, s5_a_re, s5_a_im, s5_log_dt, s5_b_re, s5_b_im, s5_c_re, s5_c_im, s5_d, s5_w_glu, s5_b_glu, ret_decay, ret_gn, na_rpb, w_branch, w_out, ffn_w_gate, ffn_w_up, ffn_w_down, moe_w_router, moe_b_router, moe_w_gate, moe_w_up, moe_w_down):
    batch, seq_len, d = x.shape
    ctx_len = ctx.shape[1]
    depth = w_mod.shape[0]
    cond = jnp.concatenate([c, c_ctx[None, :]], axis=0)
    mod_all = _modulation(cond, w_mod, b_mod)
    rope = _rope_tables(seq_len)
    lane_h = np.repeat(np.arange(RET_HEADS), RET_DIM)
    avg = jnp.asarray((lane_h[:, None] == lane_h[None, :]).astype(np.float32) / RET_DIM, BF16)

    xl = x.reshape(batch * seq_len, d)
    xc = ctx.reshape(batch * ctx_len, d)
    lat = dict(rows_per_mod=seq_len, mod_base=0)
    cxt = dict(rows_per_mod=batch * ctx_len, mod_base=batch)

    for layer in range(depth):
        last = layer == depth - 1
        need_ctx = not last
        mod = mod_all[layer]
        ng = norm_g[layer]
        w_in_bf = _permute_w_in(w_in, layer)
        s5_tabs = _s5_tables(s5_a_re[layer], s5_a_im[layer], s5_log_dt[layer], s5_b_re[layer], s5_b_im[layer],
                             s5_c_re[layer], s5_c_im[layer], s5_d[layer], batch)
        ret_tabs = _ret_tables(ret_decay[layer])
        na_bias, na_hmask = _na_tables(na_rpb[layer])
        lw = dict(w_glu=s5_w_glu[layer].astype(BF16), b_glu=s5_b_glu[layer].reshape(1, BRANCH_W).astype(F32),
                  ret_gn=ret_gn[layer].reshape(1, BRANCH_W).astype(F32), avg=avg,
                  w_branch=_cast_bf16(w_branch, layer, 0.5), w_out=_cast_bf16(w_out, layer))

        proj_l, f_l = _in_proj(xl, mod, ng[0], w_in_bf, **lat)
        proj_c, f_c = _in_proj(xc, mod, ng[0], w_in_bf, **cxt)

        a_l = _fourier_latent(f_l, batch, seq_len)
        s_l, s_c = _s5_mixer(proj_l[:, COL_S * BRANCH_W:(COL_S + 1) * BRANCH_W],
                             proj_c[:, COL_S * BRANCH_W:(COL_S + 1) * BRANCH_W], s5_tabs, batch)
        r_l, r_c = _retention(proj_l, proj_c, ret_tabs, rope, batch, seq_len, ctx_len)
        n_l, n_c = _neighborhood(proj_l, proj_c, na_bias, na_hmask, batch, seq_len, ctx_len, need_ctx)

        xl = _merge(xl, mod, ng[1], proj_l, a_l, s_l, r_l, n_l, lw, **lat)
        if need_ctx:
            a_c = _fourier_ctx(f_c, batch, ctx_len)
            xc = _merge(xc, mod, ng[1], proj_c, a_c, s_c, r_c, n_c, lw, **cxt)

        i = layer // 2
        if layer % 2 == 0:
            wg, wu, wd = _cast_bf16(ffn_w_gate, i), _cast_bf16(ffn_w_up, i), _cast_bf16(ffn_w_down, i)
            xl = _ffn_dense(xl, mod, ng[2], ng[3], wg, wu, wd, **lat)
            if need_ctx:
                xc = _ffn_dense(xc, mod, ng[2], ng[3], wg, wu, wd, **cxt)
        else:
            wg, wu, wd = moe_w_gate[i], moe_w_up[i], moe_w_down[i]
            h, meta = _router(xl, mod, ng[2], moe_w_router[i], moe_b_router[i], **lat)
            xl = _moe_sparse(xl, h, meta, mod, ng[3], wg, wu, wd, **lat)
            if need_ctx:
                hc, metac = _router(xc, mod, ng[2], moe_w_router[i], moe_b_router[i], **cxt)
                xc = _moe_sparse(xc, hc, metac, mod, ng[3], wg, wu, wd, **cxt)
    return xl.reshape(batch, seq_len, d)
```

```python
import functools
import math

import numpy as np
import jax
import jax.numpy as jnp
from jax import lax
from jax.experimental import pallas as pl
from jax.experimental.pallas import tpu as pltpu
from jax.experimental.pallas import tpu_sc as plsc

F32 = jnp.float32
BF16 = jnp.bfloat16

D_MODEL = 1024
BRANCH_W = 256
N_BRANCH = 4
GRID_W = 64
FNET_GROUP_DIM = 64
S5_GROUP_CH = 16
S5_GROUPS = 16
S5_STATE = 64
S5_CHUNK = 32
S5_PAIRS = S5_GROUPS // 2
RET_HEADS = 4
RET_DIM = 64
RET_CHUNK = 128
NA_HEADS = 4
NA_DIM = 64
NA_WIN_ROWS = 8
NA_WIN_COLS = 16
NA_QROWS = 8
ROPE_BASE = 10000.0
N_EXPERTS = 8
EPS = 1e-6
FFT_N2 = 256
NEG_BIG = -1e30
VMEM_LIMIT_BYTES = 50 * 1024 * 1024
SC_CORES = 2
SC_SUBCORES = 16
SC_WORKERS = SC_CORES * SC_SUBCORES
SC_GATHER_ROWS = 64
MOE_ROW_TILE = 1024
MOE_FF_TILE = 512

COL_F, COL_S, COL_RQ, COL_RK, COL_RV, COL_RG, COL_NQ, COL_NK, COL_NV = range(16, 25)
IN_W = 9 * BRANCH_W + N_BRANCH * D_MODEL
IN_TN = 1280
IN_F_TILE = (N_BRANCH * D_MODEL) // IN_TN
IN_F_OFF = N_BRANCH * D_MODEL - IN_F_TILE * IN_TN


def _cparams(*sem):
    return pltpu.CompilerParams(dimension_semantics=sem, vmem_limit_bytes=VMEM_LIMIT_BYTES)


def _sigmoid(v):
    return 0.5 * jnp.tanh(0.5 * v) + 0.5


def _silu(v):
    return v * _sigmoid(v)


def _gelu_tanh(v):
    return 0.5 * v * (1.0 + jnp.tanh(math.sqrt(2.0 / math.pi) * (v + 0.044715 * (v * v * v))))


def _rms(v, g):
    ms = jnp.mean(v * v, axis=-1, keepdims=True)
    return v * lax.rsqrt(ms + EPS) * g


def _split_bf16(v):
    hi = v.astype(BF16)
    lo = (v - hi.astype(F32)).astype(BF16)
    return hi, lo


def _pack_pairs(v):
    n = v.shape[1] // 2
    lo = lax.bitcast_convert_type(v[:, :n].astype(BF16).astype(F32), jnp.int32)
    hi = lax.bitcast_convert_type(v[:, n:].astype(BF16).astype(F32), jnp.int32)
    return (hi & -65536) | ((lo >> 16) & 65535)


def _unpack_pairs(w):
    lo = lax.bitcast_convert_type(w << 16, F32)
    hi = lax.bitcast_convert_type(w & -65536, F32)
    return jnp.concatenate([lo, hi], axis=-1)


def _dot(a, b):
    return jnp.dot(a, b, preferred_element_type=F32)


def _dot_nt(a, b):
    return lax.dot_general(a, b, (((1,), (1,)), ((), ())), preferred_element_type=F32)


def _dot_tn(a, b):
    return lax.dot_general(a, b, (((0,), (0,)), ((), ())), preferred_element_type=F32)


def _mod_kernel(ct_ref, w_ref, b_ref, o_ref):
    ct = ct_ref[...]
    s = _silu(ct)
    w = w_ref[...]
    rows = [jnp.sum(w * s[:, r:r + 1], axis=0, keepdims=True) for r in range(8)]
    o_ref[...] = jnp.concatenate(rows, axis=0) + b_ref[...]


def _modulation(cond, w_mod, b_mod):
    n_layers, d, n = w_mod.shape
    tn = 512
    ct = jnp.zeros((8, d), F32).at[:cond.shape[0]].set(cond).T
    return pl.pallas_call(
        _mod_kernel,
        grid=(n_layers, n // tn),
        in_specs=[
            pl.BlockSpec((d, 8), lambda l, j: (0, 0)),
            pl.BlockSpec((None, d, tn), lambda l, j: (l, 0, j)),
            pl.BlockSpec((None, 1, tn), lambda l, j: (l, 0, j)),
        ],
        out_specs=pl.BlockSpec((None, 8, tn), lambda l, j: (l, 0, j)),
        out_shape=jax.ShapeDtypeStruct((n_layers, 8, n), F32),
        compiler_params=_cparams("arbitrary", "arbitrary"),
        name="adaln_mod",
    )(ct, w_mod, b_mod.reshape(n_layers, 1, n))


def _mod_rows(mod_ref, i, tiles_per_mod, mod_base, first):
    r = mod_base + i // tiles_per_mod
    return [mod_ref[pl.ds(r, 1), (first + k) * D_MODEL:(first + k + 1) * D_MODEL] for k in range(3)]


def _in_kernel(x_ref, mod_ref, g_ref, w_ref, proj_ref, f_ref, h_scr, *, tiles_per_mod, mod_base):
    i = pl.program_id(0)
    j = pl.program_id(1)

    @pl.when(j == 0)
    def _():
        sh, sc, _ = _mod_rows(mod_ref, i, tiles_per_mod, mod_base, 0)
        h_scr[...] = (_rms(x_ref[...], g_ref[...]) * (1.0 + sc) + sh).astype(BF16)

    res = _dot(h_scr[...], w_ref[...])
    proj_ref[...] = res.astype(BF16)

    @pl.when(j == IN_F_TILE)
    def _():
        f_ref[...] = res[:, IN_F_OFF:IN_F_OFF + BRANCH_W].astype(BF16)


def _in_proj(x, mod, g, w_bf, *, rows_per_mod, mod_base):
    rows, d = x.shape
    tm = math.gcd(1024, rows_per_mod)
    kern = functools.partial(_in_kernel, tiles_per_mod=max(rows_per_mod // tm, 1), mod_base=mod_base)
    return pl.pallas_call(
        kern,
        grid=(rows // tm, IN_W // IN_TN),
        in_specs=[
            pl.BlockSpec((tm, d), lambda i, j: (i, 0)),
            pl.BlockSpec(mod.shape, lambda i, j: (0, 0)),
            pl.BlockSpec((1, d), lambda i, j: (0, 0)),
            pl.BlockSpec((d, IN_TN), lambda i, j: (0, j)),
        ],
        out_specs=[
            pl.BlockSpec((tm, IN_TN), lambda i, j: (i, j)),
            pl.BlockSpec((tm, BRANCH_W), lambda i, j: (i, 0)),
        ],
        out_shape=[
            jax.ShapeDtypeStruct((rows, IN_W), BF16),
            jax.ShapeDtypeStruct((rows, BRANCH_W), BF16),
        ],
        scratch_shapes=[pltpu.VMEM((tm, d), BF16)],
        compiler_params=_cparams("arbitrary", "arbitrary"),
        name="in_proj",
    )(x, mod, g.reshape(1, d), w_bf)


def _fft_a_kernel(x_ref, cs_ref, tc_ref, ts_ref, zr_ref, zi_ref, *, n1, n1p):
    y = _dot(cs_ref[...].astype(BF16), x_ref[...])
    yr = y[:n1]
    yi = y[n1p:n1p + n1]
    tc = tc_ref[...]
    ts = ts_ref[...]
    zr_ref[...] = (yr * tc + yi * ts).astype(BF16)
    zi_ref[...] = (yi * tc - yr * ts).astype(BF16)


def _fft_b_kernel(zr_ref, zi_ref, cs_ref, cc_ref, sc_ref, o_ref, *, kb, scale, has_imag):
    cs = cs_ref[...].astype(BF16)
    cc = cc_ref[...].astype(BF16)
    sc = sc_ref[...].astype(BF16)
    for kk in range(kb):
        a = _dot(cs, zr_ref[kk])
        if has_imag:
            b = _dot(cs, zi_ref[kk])
            xr = a[:FFT_N2] + b[FFT_N2:]
            xi = b[:FFT_N2] - a[FFT_N2:]
        else:
            xr = a[:FFT_N2]
            xi = -a[FFT_N2:]
        out = _dot(xr.astype(BF16), cc) + _dot(xi.astype(BF16), sc)
        o_ref[:, kk * BRANCH_W:(kk + 1) * BRANCH_W] = (out * scale).astype(BF16)


def _dft_tables(n):
    k = np.arange(n)
    ang = 2.0 * np.pi * ((k[:, None] * k[None, :]) % n) / n
    return np.cos(ang), np.sin(ang)


def _fft_b_call(zr, zi, n1, batch, seq_len, has_imag):
    c2, s2 = _dft_tables(FFT_N2)
    cs2 = jnp.asarray(np.concatenate([c2, s2], axis=0), F32)
    c64, s64 = _dft_tables(FNET_GROUP_DIM)
    eye = np.eye(BRANCH_W // FNET_GROUP_DIM)
    cc = jnp.asarray(np.kron(eye, c64), F32)
    sc = jnp.asarray(np.kron(eye, s64), F32)
    kb = min(8, n1)
    scale = 1.0 / math.sqrt(seq_len * FNET_GROUP_DIM)
    kern = functools.partial(_fft_b_kernel, kb=kb, scale=scale, has_imag=has_imag)
    zspec = pl.BlockSpec((None, kb, FFT_N2, BRANCH_W), lambda b, i: (b, i, 0, 0))
    out = pl.pallas_call(
        kern,
        grid=(batch, n1 // kb),
        in_specs=[
            zspec, zspec,
            pl.BlockSpec((2 * FFT_N2, FFT_N2), lambda b, i: (0, 0)),
            pl.BlockSpec((BRANCH_W, BRANCH_W), lambda b, i: (0, 0)),
            pl.BlockSpec((BRANCH_W, BRANCH_W), lambda b, i: (0, 0)),
        ],
        out_specs=pl.BlockSpec((None, FFT_N2, kb * BRANCH_W), lambda b, i: (b, 0, i)),
        out_shape=jax.ShapeDtypeStruct((batch, FFT_N2, n1 * BRANCH_W), BF16),
        compiler_params=_cparams("arbitrary", "arbitrary"),
        name="fourier_stage_b",
    )(zr, zi, cs2, cc, sc)
    return out.reshape(batch * seq_len, BRANCH_W)


def _fourier_latent(f, batch, seq_len):
    n1 = seq_len // FFT_N2
    wide = FFT_N2 * BRANCH_W
    c1, s1 = _dft_tables(n1)
    n1p = max(n1, 8)
    cs1 = np.zeros((2 * n1p, n1))
    cs1[:n1] = c1
    cs1[n1p:n1p + n1] = -s1
    k1 = np.arange(n1)[:, None]
    l2 = np.arange(FFT_N2)[None, :]
    tw = 2.0 * np.pi * (k1 * l2) / seq_len
    tc = jnp.asarray(np.repeat(np.cos(tw), BRANCH_W, axis=1), F32)
    ts = jnp.asarray(np.repeat(np.sin(tw), BRANCH_W, axis=1), F32)
    cw = min(8192, wide)
    xv = f.reshape(batch, n1, wide)
    spec = pl.BlockSpec((None, n1, cw), lambda b, j: (b, 0, j))
    tspec = pl.BlockSpec((n1, cw), lambda b, j: (0, j))
    zr, zi = pl.pallas_call(
        functools.partial(_fft_a_kernel, n1=n1, n1p=n1p),
        grid=(batch, wide // cw),
        in_specs=[spec, pl.BlockSpec((2 * n1p, n1), lambda b, j: (0, 0)), tspec, tspec],
        out_specs=[spec, spec],
        out_shape=[jax.ShapeDtypeStruct((batch, n1, wide), BF16)] * 2,
        compiler_params=_cparams("arbitrary", "arbitrary"),
        name="fourier_stage_a",
    )(xv, jnp.asarray(cs1, F32), tc, ts)
    zr = zr.reshape(batch, n1, FFT_N2, BRANCH_W)
    zi = zi.reshape(batch, n1, FFT_N2, BRANCH_W)
    return _fft_b_call(zr, zi, n1, batch, seq_len, True)


def _fourier_ctx(f, batch, ctx_len):
    assert ctx_len == FFT_N2
    z = f.reshape(batch, 1, FFT_N2, BRANCH_W)
    return _fft_b_call(z, z, 1, batch, ctx_len, False)


def _s5_tables(a_re, a_im, log_dt, b_re, b_im, c_re, c_im, d_skip, batch):
    t = S5_CHUNK
    g, p, hc = S5_GROUPS, S5_STATE, S5_GROUP_CH
    lam = lax.complex(a_re.astype(F32), a_im.astype(F32))
    dt = jnp.exp(log_dt.astype(F32))[..., None]
    ks = jnp.arange(t + 1, dtype=F32)
    apow = jnp.exp((lam * dt)[..., None] * ks)
    a_bar = apow[..., 1]
    b_bar = ((a_bar - 1.0) / lam)[..., None] * lax.complex(b_re.astype(F32), b_im.astype(F32))
    cm = lax.complex(c_re.astype(F32), c_im.astype(F32))
    kimp = jnp.real(jnp.einsum('dghp,dgpk,dgpj->dgkhj', cm, apow[..., :t], b_bar,
                               precision=lax.Precision.HIGHEST))
    kf, kb = kimp[0], kimp[1]
    kfull = jnp.concatenate([kb[:, :0:-1], kf[:, :1] + kb[:, :1], kf[:, 1:]], axis=1)
    strip = kfull.transpose(0, 3, 1, 2).reshape(g, hc, (2 * t - 1) * hc)
    strip = jnp.pad(strip, ((0, 0), (0, 0), (0, 2 * t * hc - strip.shape[-1])))
    strip = strip.reshape(S5_PAIRS, 2, hc, 2 * t * hc)

    def pair_blocks(kd, axis):
        r, c = kd.shape[1:]
        kp = kd.reshape(S5_PAIRS, 2, r, c)
        z = jnp.zeros_like(kp[:, 0])
        top = jnp.concatenate([kp[:, 0], z], axis=-1)
        bot = jnp.concatenate([z, kp[:, 1]], axis=-1)
        return jnp.concatenate([top, bot], axis=1)

    wf = jnp.einsum('gpj,gph->gjhp', apow[0][..., t - 1::-1][..., :t], b_bar[0])
    wb = jnp.einsum('gpj,gph->gjhp', apow[1][..., :t], b_bar[1])
    wf = wf.reshape(g, t * hc, p)
    wb = wb.reshape(g, t * hc, p)
    kinds = [jnp.real(wf), jnp.imag(wf), jnp.real(wb), jnp.imag(wb)]
    we = jnp.concatenate([pair_blocks(kd, 0) for kd in kinds], axis=-1).astype(BF16)

    vf = jnp.einsum('ghp,gpt->gpth', cm[0], apow[0][..., 1:t + 1])
    vb = jnp.einsum('ghp,gpt->gpth', cm[1], apow[1][..., t:0:-1])
    vf = vf.reshape(g, p, t * hc)
    vb = vb.reshape(g, p, t * hc)
    vkinds = [jnp.real(vf), -jnp.imag(vf), jnp.real(vb), -jnp.imag(vb)]
    v1 = jnp.concatenate([pair_blocks(kd, 0) for kd in vkinds], axis=1)
    v = jnp.concatenate([v1, v1], axis=1).astype(BF16)

    def lanes(z):
        return jnp.tile(z.reshape(1, g * p), (1, batch))

    at = apow[..., t]
    a_tab = jnp.concatenate([lanes(jnp.real(at[0])), lanes(jnp.imag(at[0])),
                             lanes(jnp.real(at[1])), lanes(jnp.imag(at[1]))], axis=0)
    dvec = jnp.tile(d_skip.astype(F32).reshape(S5_PAIRS, 2, 1, hc), (1, 1, t, 1)).reshape(S5_PAIRS, 1, 2 * t * hc)
    return dict(strip=strip, we=we, v=v, a_tab=a_tab, dvec=dvec)


def _s5_e_kernel(u_ref, we_ref, ref_, imf_, reb_, imb_):
    e = _dot(u_ref[...], we_ref[...])
    ref_[...] = e[:, 0:128]
    imf_[...] = e[:, 128:256]
    reb_[...] = e[:, 256:384]
    imb_[...] = e[:, 384:512]


def _s5_scan_kernel(a_ref, ref_, imf_, reb_, imb_, prf, pif, prb, pib, *, n_rows, n_ctx):
    afr = a_ref[0:1, :]
    afi = a_ref[1:2, :]
    abr = a_ref[2:3, :]
    abi = a_ref[3:4, :]
    zero = jnp.zeros_like(afr)

    def body(s, carry):
        sfr, sfi, sbr, sbi = carry
        nf = s
        nb = jnp.where(s < n_ctx, n_ctx - 1 - s, n_rows - 1 + n_ctx - s)
        prf[pl.ds(nf, 1), :] = sfr
        pif[pl.ds(nf, 1), :] = sfi
        prb[pl.ds(nb, 1), :] = sbr
        pib[pl.ds(nb, 1), :] = sbi
        efr = ref_[pl.ds(nf, 1), :]
        efi = imf_[pl.ds(nf, 1), :]
        ebr = reb_[pl.ds(nb, 1), :]
        ebi = imb_[pl.ds(nb, 1), :]
        nfr = afr * sfr - afi * sfi + efr
        nfi = afr * sfi + afi * sfr + efi
        nbr = abr * sbr - abi * sbi + ebr
        nbi = abr * sbi + abi * sbr + ebi
        return nfr, nfi, nbr, nbi

    lax.fori_loop(0, n_rows, body, (zero, zero, zero, zero))


def _s5_y_kernel(u_ref, strip_ref, v_ref, d_ref, prf, pif, prb, pib, yc_ref, yl_ref, m_scr):
    half = S5_CHUNK * S5_GROUP_CH
    n_ctx = yc_ref.shape[0]

    @pl.when(pl.program_id(1) == 0)
    def _():
        for gi in range(2):
            strip = strip_ref[gi]
            for j in range(S5_CHUNK):
                off = (S5_CHUNK - 1 - j) * S5_GROUP_CH
                win = strip if off == 0 else pltpu.roll(strip, 2 * half - off, axis=1)
                m_scr[gi, j * S5_GROUP_CH:(j + 1) * S5_GROUP_CH, :] = win[:, :half].astype(BF16)

    u = u_ref[...]
    y_intra = jnp.concatenate([_dot(u[:, :half], m_scr[0]), _dot(u[:, half:], m_scr[1])], axis=-1)
    pcat = jnp.concatenate([prf[...], pif[...], prb[...], pib[...]], axis=-1)
    hi, lo = _split_bf16(pcat)
    y_cross = _dot(jnp.concatenate([hi, lo], axis=-1), v_ref[...])
    y = y_intra + y_cross + d_ref[...] * u.astype(F32)
    yc_ref[...] = y[:n_ctx].astype(BF16)
    yl_ref[...] = y[n_ctx:].astype(BF16)


def _s5_core(u, tabs, batch, n_rows, n_ctx):
    width = batch * S5_PAIRS * 128
    cols = 2 * S5_CHUNK * S5_GROUP_CH
    u_spec = pl.BlockSpec((None, None, n_rows, cols), lambda q, b: (q, b, 0, 0))
    st_spec = pl.BlockSpec((n_rows, 128), lambda q, b: (0, b * S5_PAIRS + q))
    st_shape = jax.ShapeDtypeStruct((n_rows, width), F32)
    e4 = pl.pallas_call(
        _s5_e_kernel,
        grid=(S5_PAIRS, batch),
        in_specs=[u_spec, pl.BlockSpec((None, cols, 512), lambda q, b: (q, 0, 0))],
        out_specs=[st_spec] * 4,
        out_shape=[st_shape] * 4,
        compiler_params=_cparams("arbitrary", "arbitrary"),
        name="s5_chunk_states",
    )(u, tabs['we'])
    p4 = pl.pallas_call(
        functools.partial(_s5_scan_kernel, n_rows=n_rows, n_ctx=n_ctx),
        out_shape=[st_shape] * 4,
        compiler_params=pltpu.CompilerParams(vmem_limit_bytes=VMEM_LIMIT_BYTES),
        name="s5_state_scan",
    )(tabs['a_tab'], *e4)
    y = pl.pallas_call(
        _s5_y_kernel,
        grid=(S5_PAIRS, batch),
        in_specs=[
            u_spec,
            pl.BlockSpec((None, 2, S5_GROUP_CH, cols), lambda q, b: (q, 0, 0, 0)),
            pl.BlockSpec((None, cols, cols), lambda q, b: (q, 0, 0)),
            pl.BlockSpec((None, 1, cols), lambda q, b: (q, 0, 0)),
            st_spec, st_spec, st_spec, st_spec,
        ],
        out_specs=[
            pl.BlockSpec((None, None, n_ctx, cols), lambda q, b: (q, b, 0, 0)),
            pl.BlockSpec((None, None, n_rows - n_ctx, cols), lambda q, b: (q, b, 0, 0)),
        ],
        out_shape=[
            jax.ShapeDtypeStruct((S5_PAIRS, batch, n_ctx, cols), BF16),
            jax.ShapeDtypeStruct((S5_PAIRS, batch, n_rows - n_ctx, cols), BF16),
        ],
        scratch_shapes=[pltpu.VMEM((2, cols // 2, cols // 2), BF16)],
        compiler_params=_cparams("arbitrary", "arbitrary"),
        name="s5_outputs",
    )(u, tabs['strip'], tabs['v'], tabs['dvec'], *p4)
    return y


def _s5_to_chunks(s, batch):
    n = s.shape[0] // batch // S5_CHUNK
    v = s.reshape(batch, n, S5_CHUNK, S5_PAIRS, 2, S5_GROUP_CH)
    return v.transpose(3, 0, 1, 4, 2, 5).reshape(S5_PAIRS, batch, n, 2 * S5_CHUNK * S5_GROUP_CH)


def _s5_from_chunks(y, batch):
    n = y.shape[2]
    v = y.reshape(S5_PAIRS, batch, n, 2, S5_CHUNK, S5_GROUP_CH)
    return v.transpose(1, 2, 4, 0, 3, 5).reshape(batch * n * S5_CHUNK, BRANCH_W)


def _s5_mixer(s_lat, s_ctx, tabs, batch):
    ul = _s5_to_chunks(s_lat, batch)
    uc = _s5_to_chunks(s_ctx, batch)
    n_ctx = uc.shape[2]
    u = jnp.concatenate([uc, ul], axis=2)
    yc, yl = _s5_core(u, tabs, batch, u.shape[2], n_ctx)
    return _s5_from_chunks(yl, batch), _s5_from_chunks(yc, batch)


def _ret_tables(ret_decay):
    c = RET_CHUNK
    lg = jax.nn.log_sigmoid(ret_decay.astype(F32))
    lane_h = jnp.repeat(jnp.arange(RET_HEADS), RET_DIM)
    lgl = lg[:, lane_h]
    pos = jnp.arange(c, dtype=F32)[:, None]
    qd = jnp.stack([jnp.exp((pos + 1.0) * lgl[0][None]), jnp.exp((c - pos) * lgl[1][None])])
    kd = jnp.stack([jnp.exp((c - 1.0 - pos) * lgl[0][None]), jnp.exp(pos * lgl[1][None])])
    bmask = (lane_h[:, None] == lane_h[None, :]).astype(F32)
    cd = jnp.exp(c * lgl)[:, :, None] * bmask[None]
    diff = pos - pos.T
    dm = []
    for h in range(RET_HEADS):
        fw = jnp.where(diff >= 0, jnp.exp(jnp.maximum(diff, 0.0) * lg[0, h]), 0.0)
        bw = jnp.where(diff <= 0, jnp.exp(jnp.maximum(-diff, 0.0) * lg[1, h]), 0.0)
        dm.append(fw + bw)
    dm = jnp.concatenate(dm, axis=0)
    hmask = (jnp.arange(RET_HEADS)[:, None] == lane_h[None, :]).astype(F32)
    return dict(qd=qd, kd=kd, cd=cd, bmask=bmask, dm=dm, hmask=hmask)


def _rope_tables(n_tokens):
    t = np.arange(n_tokens)
    row = (t // GRID_W).astype(np.float64)
    col = (t % GRID_W).astype(np.float64)
    n_freq = RET_DIM // 4
    inv_freq = 1.0 / (ROPE_BASE ** (np.arange(n_freq, dtype=np.float64) / n_freq))
    ang = np.concatenate([row[:, None] * inv_freq, col[:, None] * inv_freq], axis=-1)
    cos = np.cos(ang)
    sin = np.sin(ang)
    cos_t = np.tile(np.concatenate([cos, cos], axis=-1), (1, RET_HEADS))
    sin_t = np.tile(np.concatenate([-sin, sin], axis=-1), (1, RET_HEADS))
    half = RET_DIM // 2
    perm = np.arange(BRANCH_W) ^ half
    swap = np.zeros((BRANCH_W, BRANCH_W), np.float32)
    swap[perm, np.arange(BRANCH_W)] = 1.0
    return jnp.asarray(cos_t, F32), jnp.asarray(sin_t, F32), jnp.asarray(swap, BF16)


def _ret_chunk(q, k, v, s, qd, kd, cd, bmask, dm, hmask, with_intra):
    cross = _dot((q * qd).astype(BF16), s.astype(BF16))
    s_new = cd * s + bmask * _dot_tn((k * kd).astype(BF16), v)
    if not with_intra:
        return cross, s_new
    qb = q.astype(BF16)
    kb = k.astype(BF16)
    qs = jnp.concatenate([qb * hmask[h:h + 1].astype(BF16) for h in range(RET_HEADS)], axis=0)
    scores = _dot_nt(qs, kb) * dm
    ov = _dot(scores.astype(BF16), v)
    c = q.shape[0]
    inner = ov[0:c] * hmask[0:1]
    for h in range(1, RET_HEADS):
        inner = inner + ov[h * c:(h + 1) * c] * hmask[h:h + 1]
    return inner + cross, s_new


def _ret_kernel(q_ref, k_ref, v_ref, qc_ref, kc_ref, vc_ref, cos_ref, sin_ref, swap_ref,
                qd_ref, kd_ref, cd_ref, bm_ref, dm_ref, hm_ref, o_ref, oc_ref, s_scr, *, n_chunks, n_ctx_chunks):
    dirn = pl.program_id(1)
    i = pl.program_id(2)
    c = RET_CHUNK
    k_scale = RET_DIM ** -0.5
    bmask = bm_ref[...]
    dm = dm_ref[...]
    hmask = hm_ref[...]

    def run(d):
        qd = qd_ref[d]
        kd = kd_ref[d]
        cd = cd_ref[d]
        intra = d == 0

        @pl.when(i == 0)
        def _():
            s = jnp.zeros((BRANCH_W, BRANCH_W), F32)
            order = range(n_ctx_chunks) if d == 0 else range(n_ctx_chunks - 1, -1, -1)
            for cc in order:
                sl = slice(cc * c, (cc + 1) * c)
                o, s = _ret_chunk(qc_ref[sl, :].astype(F32), kc_ref[sl, :].astype(F32) * k_scale, vc_ref[sl, :],
                                  s, qd, kd, cd, bmask, dm, hmask, intra)
                oc_ref[sl, :] = o
            s_scr[...] = s

        swap = swap_ref[...]
        order = range(n_chunks) if d == 0 else range(n_chunks - 1, -1, -1)
        s = s_scr[...]
        for cc in order:
            sl = slice(cc * c, (cc + 1) * c)
            cos = cos_ref[sl, :]
            sin = sin_ref[sl, :]
            qb = q_ref[sl, :]
            kb = k_ref[sl, :]
            q = qb.astype(F32) * cos + _dot(qb, swap) * sin
            k = (kb.astype(F32) * cos + _dot(kb, swap) * sin) * k_scale
            o, s = _ret_chunk(q, k, v_ref[sl, :], s, qd, kd, cd, bmask, dm, hmask, intra)
            o_ref[sl, :] = o
        s_scr[...] = s

    @pl.when(dirn == 0)
    def _():
        run(0)

    @pl.when(dirn == 1)
    def _():
        run(1)


def _retention(proj_l, proj_c, tabs, rope, batch, seq_len, ctx_len):
    n_chunks = 8
    blk = n_chunks * RET_CHUNK
    nblk = seq_len // blk
    cos_t, sin_t, swap = rope

    def pos(d, i):
        return i + d * (nblk - 1 - 2 * i)

    def lat(col):
        return pl.BlockSpec((blk, BRANCH_W), lambda b, d, i: (b * nblk + pos(d, i), col))

    def ctx(col):
        return pl.BlockSpec((ctx_len, BRANCH_W), lambda b, d, i: (b, col))

    def const(shape):
        return pl.BlockSpec(shape, lambda b, d, i: (0,) * len(shape))

    tab_spec = pl.BlockSpec((blk, BRANCH_W), lambda b, d, i: (pos(d, i), 0))
    kern = functools.partial(_ret_kernel, n_chunks=n_chunks, n_ctx_chunks=ctx_len // RET_CHUNK)
    c = RET_CHUNK
    o, oc = pl.pallas_call(
        kern,
        grid=(batch, 2, nblk),
        in_specs=[
            lat(COL_RQ), lat(COL_RK), lat(COL_RV), ctx(COL_RQ), ctx(COL_RK), ctx(COL_RV),
            tab_spec, tab_spec, const((BRANCH_W, BRANCH_W)),
            const((2, c, BRANCH_W)), const((2, c, BRANCH_W)), const((2, BRANCH_W, BRANCH_W)),
            const((BRANCH_W, BRANCH_W)), const((RET_HEADS * c, c)), const((RET_HEADS, BRANCH_W)),
        ],
        out_specs=[
            pl.BlockSpec((None, blk, BRANCH_W), lambda b, d, i: (d, b * nblk + pos(d, i), 0)),
            pl.BlockSpec((None, ctx_len, BRANCH_W), lambda b, d, i: (d, b, 0)),
        ],
        out_shape=[
            jax.ShapeDtypeStruct((2, batch * seq_len, BRANCH_W), F32),
            jax.ShapeDtypeStruct((2, batch * ctx_len, BRANCH_W), F32),
        ],
        scratch_shapes=[pltpu.VMEM((BRANCH_W, BRANCH_W), F32)],
        compiler_params=_cparams("arbitrary", "arbitrary", "arbitrary"),
        name="retention",
    )(proj_l, proj_l, proj_l, proj_c, proj_c, proj_c, cos_t, sin_t, swap,
      tabs['qd'], tabs['kd'], tabs['cd'], tabs['bmask'], tabs['dm'], tabs['hmask'])
    return o, oc


def _na_tables(rpb):
    kr, kw = NA_WIN_ROWS, NA_WIN_COLS
    col = np.arange(GRID_W)
    col_start = np.clip(col - kw // 2, 0, GRID_W - kw)
    in_win = (col[None, :] >= col_start[:, None]) & (col[None, :] < col_start[:, None] + kw)
    dc = np.clip(col[None, :] - col[:, None], -(kw - 1), kw - 1) + (kw - 1)
    var = np.arange(kr)[:, None] + np.arange(kr)[None, :]
    pick_r = (var[:, :, None] == np.arange(2 * kr - 1)[None, None, :]).astype(np.float32)
    pick_c = (dc[:, :, None] == np.arange(2 * kw - 1)[None, None, :]).astype(np.float32)
    bias = jnp.einsum('vir,hrc,qkc->vhqik', jnp.asarray(pick_r), rpb.astype(F32), jnp.asarray(pick_c),
                      precision=lax.Precision.HIGHEST)
    bias = jnp.where(jnp.asarray(in_win)[None, None, :, None, :], bias, NEG_BIG)
    bias = bias.reshape(kr, NA_HEADS * GRID_W, kr * GRID_W)
    lane_h = np.repeat(np.arange(NA_HEADS), NA_DIM)
    hmask = (np.arange(NA_HEADS)[:, None] == lane_h[None, :]).astype(np.float32)
    return bias, jnp.asarray(hmask, F32)


def _attend(qs, keys, vals, bias, kc, vc):
    s_ctx = _dot_nt(qs, kc)
    m = jnp.max(s_ctx, axis=-1, keepdims=True)
    if keys is not None:
        s_band = _dot_nt(qs, keys) + bias
        m = jnp.maximum(m, jnp.max(s_band, axis=-1, keepdims=True))
        p_band = jnp.exp(s_band - m)
    p_ctx = jnp.exp(s_ctx - m)
    l = jnp.sum(p_ctx, axis=-1, keepdims=True)
    o = _dot(p_ctx.astype(BF16), vc)
    if keys is not None:
        l = l + jnp.sum(p_band, axis=-1, keepdims=True)
        o = o + _dot(p_band.astype(BF16), vals)
    return o / l


def _stack_heads(q, hmask_scaled):
    return jnp.concatenate([q * hmask_scaled[h:h + 1] for h in range(NA_HEADS)], axis=0)


def _unstack_heads(o, hmask, n):
    out = o[0:n] * hmask[0:1]
    for h in range(1, NA_HEADS):
        out = out + o[h * n:(h + 1) * n] * hmask[h:h + 1]
    return out


def _na_kernel(q_ref, k_ref, v_ref, kc_ref, vc_ref, bias_ref, hm_ref, o_ref, *, n_grid_rows):
    i = pl.program_id(1)
    hmask = hm_ref[...]
    hms = (hmask * (NA_DIM ** -0.5)).astype(BF16)
    kc = kc_ref[...]
    vc = vc_ref[...]
    band = NA_WIN_ROWS * GRID_W
    for rr in range(NA_QROWS):
        r = i * NA_QROWS + rr
        rs = jnp.clip(r - NA_WIN_ROWS // 2, 0, n_grid_rows - NA_WIN_ROWS)
        var = rs - r + (NA_WIN_ROWS - 1)
        start = pl.multiple_of(rs * GRID_W, GRID_W)
        keys = k_ref[pl.ds(start, band), :]
        vals = v_ref[pl.ds(start, band), :]
        qs = _stack_heads(q_ref[rr * GRID_W:(rr + 1) * GRID_W, :], hms)
        o = _attend(qs, keys, vals, bias_ref[var], kc, vc)
        o_ref[rr * GRID_W:(rr + 1) * GRID_W, :] = _unstack_heads(o, hmask, GRID_W).astype(BF16)


def _na_ctx_kernel(q_ref, kc_ref, vc_ref, hm_ref, o_ref):
    hmask = hm_ref[...]
    hms = (hmask * (NA_DIM ** -0.5)).astype(BF16)
    n = q_ref.shape[0]
    o = _attend(_stack_heads(q_ref[...], hms), None, None, None, kc_ref[...], vc_ref[...])
    o_ref[...] = _unstack_heads(o, hmask, n).astype(BF16)


def _neighborhood(proj_l, proj_c, bias, hmask, batch, seq_len, ctx_len, need_ctx_out):
    rows = seq_len // GRID_W
    qblk = NA_QROWS * GRID_W
    nq = seq_len // qblk
    out_l = pl.pallas_call(
        functools.partial(_na_kernel, n_grid_rows=rows),
        grid=(batch, nq),
        in_specs=[
            pl.BlockSpec((qblk, BRANCH_W), lambda b, i: (b * nq + i, COL_NQ)),
            pl.BlockSpec((seq_len, BRANCH_W), lambda b, i: (b, COL_NK)),
            pl.BlockSpec((seq_len, BRANCH_W), lambda b, i: (b, COL_NV)),
            pl.BlockSpec((ctx_len, BRANCH_W), lambda b, i: (b, COL_NK)),
            pl.BlockSpec((ctx_len, BRANCH_W), lambda b, i: (b, COL_NV)),
            pl.BlockSpec(bias.shape, lambda b, i: (0, 0, 0)),
            pl.BlockSpec(hmask.shape, lambda b, i: (0, 0)),
        ],
        out_specs=pl.BlockSpec((qblk, BRANCH_W), lambda b, i: (b * nq + i, 0)),
        out_shape=jax.ShapeDtypeStruct((batch * seq_len, BRANCH_W), BF16),
        compiler_params=_cparams("arbitrary", "arbitrary"),
        name="neighborhood_attn",
    )(proj_l, proj_l, proj_l, proj_c, proj_c, bias, hmask)
    out_c = None
    if need_ctx_out:
        out_c = pl.pallas_call(
            _na_ctx_kernel,
            grid=(batch,),
            in_specs=[
                pl.BlockSpec((ctx_len, BRANCH_W), lambda b: (b, COL_NQ)),
                pl.BlockSpec((ctx_len, BRANCH_W), lambda b: (b, COL_NK)),
                pl.BlockSpec((ctx_len, BRANCH_W), lambda b: (b, COL_NV)),
                pl.BlockSpec(hmask.shape, lambda b: (0, 0)),
            ],
            out_specs=pl.BlockSpec((ctx_len, BRANCH_W), lambda b: (b, 0)),
            out_shape=jax.ShapeDtypeStruct((batch * ctx_len, BRANCH_W), BF16),
            compiler_params=_cparams("arbitrary"),
            name="context_attn",
        )(proj_c, proj_c, proj_c, hmask)
    return out_l, out_c


def _merge_kernel(x_ref, mod_ref, g_ref, gt0, gt1, gt2, gt3, a_ref, s5_ref, ro_ref, rg_ref, na_ref,
                  wglu_ref, bglu_ref, gn_ref, avg_ref, wb_ref, wo_ref, o_ref, *, tiles_per_mod, mod_base):
    i = pl.program_id(0)
    _, _, gate_a = _mod_rows(mod_ref, i, tiles_per_mod, mod_base, 0)
    z = _gelu_tanh(s5_ref[...].astype(F32)).astype(BF16)
    zf = z.astype(F32)
    b_s5 = (zf * _sigmoid(_dot(z, wglu_ref[...]) + bglu_ref[...])).astype(BF16)
    o = ro_ref[0] + ro_ref[1]
    avg = avg_ref[...]
    hi, lo = _split_bf16(o)
    mu = _dot(hi, avg) + _dot(lo, avg)
    dlt = o - mu
    hi, lo = _split_bf16(dlt * dlt)
    var = _dot(hi, avg) + _dot(lo, avg)
    hn = dlt * lax.rsqrt(var + EPS) * gn_ref[...]
    b_ret = (_silu(rg_ref[...].astype(F32)) * hn).astype(BF16)
    outs = (a_ref[...], b_s5, b_ret, na_ref[...])
    gates = (gt0, gt1, gt2, gt3)
    y = (1.0 + jnp.tanh(gates[0][...].astype(F32))) * _dot(outs[0], wb_ref[0])
    for b in range(1, N_BRANCH):
        y = y + (1.0 + jnp.tanh(gates[b][...].astype(F32))) * _dot(outs[b], wb_ref[b])
    yo = _dot(y.astype(BF16), wo_ref[...])
    o_ref[...] = x_ref[...] + gate_a * _rms(yo, g_ref[...])


def _merge(x, mod, g1, proj, a, s5y, ret_o, na, lw, *, rows_per_mod, mod_base):
    rows, d = x.shape
    tm = min(512, rows)
    nt = rows // tm

    def row(shape, col=0):
        return pl.BlockSpec(shape, lambda i: (i, col))

    def const(arr):
        return pl.BlockSpec(arr.shape, lambda i: (0,) * arr.ndim)

    kern = functools.partial(_merge_kernel, tiles_per_mod=max(rows_per_mod // tm, 1), mod_base=mod_base)
    ins = [x, mod, g1.reshape(1, d), proj, proj, proj, proj, a, s5y, ret_o, proj, na,
           lw['w_glu'], lw['b_glu'], lw['ret_gn'], lw['avg'], lw['w_branch'], lw['w_out']]
    specs = [
        row((tm, d)), const(mod), pl.BlockSpec((1, d), lambda i: (0, 0)),
        row((tm, d), 0), row((tm, d), 1), row((tm, d), 2), row((tm, d), 3),
        row((tm, BRANCH_W)), row((tm, BRANCH_W)),
        pl.BlockSpec((2, tm, BRANCH_W), lambda i: (0, i, 0)),
        row((tm, BRANCH_W), COL_RG), row((tm, BRANCH_W)),
        const(lw['w_glu']), const(lw['b_glu']), const(lw['ret_gn']), const(lw['avg']),
        const(lw['w_branch']), const(lw['w_out']),
    ]
    return pl.pallas_call(
        kern,
        grid=(nt,),
        in_specs=specs,
        out_specs=row((tm, d)),
        out_shape=jax.ShapeDtypeStruct((rows, d), F32),
        compiler_params=_cparams("arbitrary"),
        name="merge_out",
    )(*ins)


def _ffn_kernel(x_ref, mod_ref, g2_ref, g3_ref, wg_ref, wu_ref, wd_ref, o_ref, h_scr, acc_scr,
                *, tiles_per_mod, mod_base, n_f):
    i = pl.program_id(0)
    f = pl.program_id(1)

    @pl.when(f == 0)
    def _():
        sh, sc, _ = _mod_rows(mod_ref, i, tiles_per_mod, mod_base, 3)
        h_scr[...] = (_rms(x_ref[...], g2_ref[...]) * (1.0 + sc) + sh).astype(BF16)
        acc_scr[...] = jnp.zeros_like(acc_scr)

    h = h_scr[...]
    act = (_silu(_dot(h, wg_ref[...])) * _dot(h, wu_ref[...])).astype(BF16)
    acc_scr[...] += _dot(act, wd_ref[...])

    @pl.when(f == n_f - 1)
    def _():
        _, _, gate_f = _mod_rows(mod_ref, i, tiles_per_mod, mod_base, 3)
        o_ref[...] = x_ref[...] + gate_f * _rms(acc_scr[...], g3_ref[...])


def _ffn_dense(x, mod, g2, g3, wg, wu, wd, *, rows_per_mod, mod_base):
    rows, d = x.shape
    d_ff = wg.shape[1]
    tm = min(512, rows)
    tf = d_ff // 2 if (d_ff // 2) % 128 == 0 else d_ff
    n_f = d_ff // tf
    kern = functools.partial(_ffn_kernel, tiles_per_mod=max(rows_per_mod // tm, 1), mod_base=mod_base, n_f=n_f)
    return pl.pallas_call(
        kern,
        grid=(rows // tm, n_f),
        in_specs=[
            pl.BlockSpec((tm, d), lambda i, f: (i, 0)),
            pl.BlockSpec(mod.shape, lambda i, f: (0, 0)),
            pl.BlockSpec((1, d), lambda i, f: (0, 0)),
            pl.BlockSpec((1, d), lambda i, f: (0, 0)),
            pl.BlockSpec((d, tf), lambda i, f: (0, f)),
            pl.BlockSpec((d, tf), lambda i, f: (0, f)),
            pl.BlockSpec((tf, d), lambda i, f: (f, 0)),
        ],
        out_specs=pl.BlockSpec((tm, d), lambda i, f: (i, 0)),
        out_shape=jax.ShapeDtypeStruct((rows, d), F32),
        scratch_shapes=[pltpu.VMEM((tm, d), BF16), pltpu.VMEM((tm, d), F32)],
        compiler_params=_cparams("arbitrary", "arbitrary"),
        name="ffn_dense",
    )(x, mod, g2.reshape(1, d), g3.reshape(1, d), wg, wu, wd)


def _router_kernel(x_ref, mod_ref, g2_ref, wr_ref, br_ref, h_ref, comb_ref, *, tiles_per_mod, mod_base):
    i = pl.program_id(0)
    sh, sc, _ = _mod_rows(mod_ref, i, tiles_per_mod, mod_base, 3)
    h = _rms(x_ref[...], g2_ref[...]) * (1.0 + sc) + sh
    h_ref[...] = _pack_pairs(h)
    h_hi, h_lo = _split_bf16(h)
    w_hi, w_lo = _split_bf16(wr_ref[...])
    logits = _dot(h_hi, w_hi) + _dot(h_lo, w_hi) + _dot(h_hi, w_lo) + br_ref[...]
    lane = lax.broadcasted_iota(jnp.int32, logits.shape, 1)
    v1 = jnp.max(logits, axis=-1, keepdims=True)
    i1 = jnp.min(jnp.where(logits == v1, lane, 128), axis=-1, keepdims=True)
    rest = jnp.where(lane == i1, NEG_BIG, logits)
    v2 = jnp.max(rest, axis=-1, keepdims=True)
    i2 = jnp.min(jnp.where(rest == v2, lane, 128), axis=-1, keepdims=True)
    e = jnp.exp(v2 - v1)
    w1 = 1.0 / (1.0 + e)
    w2 = e / (1.0 + e)
    meta = jnp.where(lane == 0, i1.astype(F32), 0.0) + jnp.where(lane == 1, i2.astype(F32), 0.0)
    comb_ref[...] = meta + jnp.where(lane == 2, w1, 0.0) + jnp.where(lane == 3, w2, 0.0)


def _router(x, mod, g2, w_router, b_router, *, rows_per_mod, mod_base):
    rows, d = x.shape
    tm = min(512, rows)
    wr = jnp.zeros((d, 128), F32).at[:, :N_EXPERTS].set(w_router)
    br = jnp.full((1, 128), NEG_BIG, F32).at[0, :N_EXPERTS].set(b_router)
    kern = functools.partial(_router_kernel, tiles_per_mod=max(rows_per_mod // tm, 1), mod_base=mod_base)
    return pl.pallas_call(
        kern,
        grid=(rows // tm,),
        in_specs=[
            pl.BlockSpec((tm, d), lambda i: (i, 0)),
            pl.BlockSpec(mod.shape, lambda i: (0, 0)),
            pl.BlockSpec((1, d), lambda i: (0, 0)),
            pl.BlockSpec((d, 128), lambda i: (0, 0)),
            pl.BlockSpec((1, 128), lambda i: (0, 0)),
        ],
        out_specs=[pl.BlockSpec((tm, d // 2), lambda i: (i, 0)), pl.BlockSpec((tm, 128), lambda i: (i, 0))],
        out_shape=[jax.ShapeDtypeStruct((rows, d // 2), jnp.int32), jax.ShapeDtypeStruct((rows, 128), F32)],
        compiler_params=_cparams("arbitrary"),
        name="moe_router",
    )(x, mod, g2.reshape(1, d), wr, br)


def _sc_gather(table, idx):
    n_idx = idx.shape[0]
    width = table.shape[1]
    per_worker = n_idx // SC_WORKERS
    chunk_rows = math.gcd(per_worker, SC_GATHER_ROWS)
    n_chunks = per_worker // chunk_rows
    assert per_worker * SC_WORKERS == n_idx and chunk_rows % 8 == 0
    mesh = plsc.VectorSubcoreMesh(core_axis_name="c", subcore_axis_name="s")

    assert n_chunks % 2 == 0
    buf = [pltpu.VMEM((chunk_rows,), jnp.int32), pltpu.VMEM((chunk_rows, width), table.dtype),
           pltpu.SemaphoreType.DMA, pltpu.SemaphoreType.DMA]

    @functools.partial(
        pl.kernel, mesh=mesh,
        out_type=jax.ShapeDtypeStruct((n_idx, width), table.dtype),
        scratch_types=buf + buf,
        name="sc_row_gather",
    )
    def gather(table_hbm, idx_hbm, out_hbm, idx0, rows0, g0, w0, idx1, rows1, g1, w1):
        wid = lax.axis_index("s") * SC_CORES + lax.axis_index("c")
        base = wid * per_worker
        slots = ((idx0, rows0, g0, w0), (idx1, rows1, g1, w1))

        def fetch(j, slot):
            idx_v, rows_v, g, _ = slots[slot]
            pltpu.sync_copy(idx_hbm.at[pl.ds(base + j * chunk_rows, chunk_rows)], idx_v)
            pltpu.make_async_copy(table_hbm.at[idx_v], rows_v, g).start()

        def store(j, slot):
            idx_v, rows_v, g, w = slots[slot]
            pltpu.make_async_copy(table_hbm.at[idx_v], rows_v, g).wait()
            pltpu.make_async_copy(rows_v, out_hbm.at[pl.ds(base + j * chunk_rows, chunk_rows)], w).start()

        def drain(j, slot):
            _, rows_v, _, w = slots[slot]
            pltpu.make_async_copy(rows_v, out_hbm.at[pl.ds(base + j * chunk_rows, chunk_rows)], w).wait()

        fetch(0, 0)

        @pl.loop(0, n_chunks // 2)
        def _(jj):
            j = 2 * jj

            @pl.when(jj > 0)
            def _():
                drain(j - 1, 1)

            fetch(j + 1, 1)
            store(j, 0)

            @pl.when(j + 2 < n_chunks)
            def _():
                drain(j, 0)
                fetch(j + 2, 0)

            store(j + 1, 1)

        drain(n_chunks - 2, 0)
        drain(n_chunks - 1, 1)

    return gather(table, idx)


def _sc_scatter(table, idx, n_out):
    n_idx = idx.shape[0]
    rows, width = table.shape
    per_worker = n_idx // SC_WORKERS
    chunk_rows = math.gcd(per_worker, SC_GATHER_ROWS)
    n_chunks = per_worker // chunk_rows
    assert per_worker * SC_WORKERS == n_idx and chunk_rows % 8 == 0 and rows % per_worker == 0
    mesh = plsc.VectorSubcoreMesh(core_axis_name="c", subcore_axis_name="s")

    assert n_chunks % 2 == 0
    buf = [pltpu.VMEM((chunk_rows,), jnp.int32), pltpu.VMEM((chunk_rows, width), table.dtype),
           pltpu.SemaphoreType.DMA, pltpu.SemaphoreType.DMA]

    @functools.partial(
        pl.kernel, mesh=mesh,
        out_type=jax.ShapeDtypeStruct((n_out, width), table.dtype),
        scratch_types=buf + buf,
        name="sc_row_scatter",
    )
    def scatter(table_hbm, idx_hbm, out_hbm, idx0, rows0, l0, w0, idx1, rows1, l1, w1):
        wid = lax.axis_index("s") * SC_CORES + lax.axis_index("c")
        base = wid * per_worker
        slots = ((idx0, rows0, l0, w0), (idx1, rows1, l1, w1))

        def src(j):
            return table_hbm.at[pl.ds(lax.rem(base + j * chunk_rows, rows), chunk_rows)]

        def fetch(j, slot):
            idx_v, rows_v, l, _ = slots[slot]
            pltpu.sync_copy(idx_hbm.at[pl.ds(base + j * chunk_rows, chunk_rows)], idx_v)
            pltpu.make_async_copy(src(j), rows_v, l).start()

        def store(j, slot):
            idx_v, rows_v, l, w = slots[slot]
            pltpu.make_async_copy(src(j), rows_v, l).wait()
            pltpu.make_async_copy(rows_v, out_hbm.at[idx_v], w).start()

        def drain(slot):
            idx_v, rows_v, _, w = slots[slot]
            pltpu.make_async_copy(rows_v, out_hbm.at[idx_v], w).wait()

        fetch(0, 0)

        @pl.loop(0, n_chunks // 2)
        def _(jj):
            j = 2 * jj

            @pl.when(jj > 0)
            def _():
                drain(1)

            fetch(j + 1, 1)
            store(j, 0)

            @pl.when(j + 2 < n_chunks)
            def _():
                drain(0)
                fetch(j + 2, 0)

            store(j + 1, 1)

        drain(0)
        drain(1)

    return scatter(table, idx)


def _moe_plan(meta, rows):
    tile = MOE_ROW_TILE
    n_tiles = (2 * rows) // tile + N_EXPERTS
    n_slots = n_tiles * tile
    experts = jnp.concatenate([meta[:, 0], meta[:, 1]]).astype(jnp.int32)
    onehot = (experts[:, None] == jnp.arange(N_EXPERTS)[None, :]).astype(jnp.int32)
    csum = jnp.cumsum(onehot, axis=0)
    counts = csum[-1]
    rank = jnp.sum(onehot * csum, axis=1) - 1
    padded = ((counts + tile - 1) // tile) * tile
    ends = jnp.cumsum(padded)
    starts = ends - padded
    pos = jnp.sum(onehot * starts[None, :], axis=1) + rank
    tile_start = jnp.arange(n_tiles, dtype=jnp.int32) * tile
    used = tile_start < ends[-1]
    tile_e = jnp.minimum(jnp.sum((tile_start[:, None] >= ends[None, :]).astype(jnp.int32), axis=1), N_EXPERTS - 1)
    last_e = jnp.max(jnp.where(used, tile_e, 0))
    tile_e = jnp.where(used, tile_e, last_e)
    valid_end = jnp.sum((tile_e[:, None] == jnp.arange(N_EXPERTS)[None, :]) * (starts + counts)[None, :], axis=1)
    n_valid = jnp.where(used, jnp.clip(valid_end - tile_start, 0, tile), 0).astype(jnp.int32)
    return pos.astype(jnp.int32), n_slots, tile_e.astype(jnp.int32), n_valid


def _moe_group_kernel(eid_ref, nval_ref, hs_ref, wg_ref, wu_ref, wd_ref, y_ref, h_scr, acc_scr, *, n_f):
    w = pl.program_id(0)
    f = pl.program_id(1)
    nv = nval_ref[w]

    def run(n_rows):
        rows = slice(0, n_rows)

        @pl.when(f == 0)
        def _():
            hv = _unpack_pairs(hs_ref[rows, :])
            row = lax.broadcasted_iota(jnp.int32, hv.shape, 0)
            h_scr[rows, :] = jnp.where(row < nv, hv, 0.0).astype(BF16)
            acc_scr[rows, :] = jnp.zeros((n_rows, acc_scr.shape[1]), F32)

        h = h_scr[rows, :]
        gate = _dot(h, wg_ref[...].astype(BF16))
        up = _dot(h, wu_ref[...].astype(BF16))
        acc_scr[rows, :] += _dot((_silu(gate) * up).astype(BF16), wd_ref[...].astype(BF16))

        @pl.when(f == n_f - 1)
        def _():
            y_ref[rows, :] = _pack_pairs(acc_scr[rows, :])

    half = hs_ref.shape[0] // 2

    @pl.when(nv > half)
    def _():
        run(hs_ref.shape[0])

    @pl.when((nv > 0) & (nv <= half))
    def _():
        run(half)


def _moe_grouped(hs, tile_e, n_valid, wg, wu, wd):
    n_slots = hs.shape[0]
    d = wg.shape[1]
    d_ff = wg.shape[2]
    tile = MOE_ROW_TILE
    tf = MOE_FF_TILE
    n_f = d_ff // tf

    def f_idx(f, nval, w):
        return jnp.where(nval[w] > 0, f, n_f - 1)

    grid_spec = pltpu.PrefetchScalarGridSpec(
        num_scalar_prefetch=2,
        grid=(n_slots // tile, n_f),
        in_specs=[
            pl.BlockSpec((tile, d // 2), lambda w, f, eid, nval: (w, 0)),
            pl.BlockSpec((None, d, tf), lambda w, f, eid, nval: (eid[w], 0, f_idx(f, nval, w))),
            pl.BlockSpec((None, d, tf), lambda w, f, eid, nval: (eid[w], 0, f_idx(f, nval, w))),
            pl.BlockSpec((None, tf, d), lambda w, f, eid, nval: (eid[w], f_idx(f, nval, w), 0)),
        ],
        out_specs=pl.BlockSpec((tile, d // 2), lambda w, f, eid, nval: (w, 0)),
        scratch_shapes=[pltpu.VMEM((tile, d), BF16), pltpu.VMEM((tile, d), F32)],
    )
    return pl.pallas_call(
        functools.partial(_moe_group_kernel, n_f=n_f),
        grid_spec=grid_spec,
        out_shape=jax.ShapeDtypeStruct((n_slots, d // 2), jnp.int32),
        compiler_params=_cparams("arbitrary", "arbitrary"),
        name="moe_experts",
    )(tile_e, n_valid, hs, wg, wu, wd)


def _moe_out_kernel(x_ref, y1_ref, y2_ref, meta_ref, mod_ref, g3_ref, o_ref, *, tiles_per_mod, mod_base):
    i = pl.program_id(0)
    _, _, gate_f = _mod_rows(mod_ref, i, tiles_per_mod, mod_base, 3)
    meta = meta_ref[...]
    y = meta[:, 2:3] * _unpack_pairs(y1_ref[...]) + meta[:, 3:4] * _unpack_pairs(y2_ref[...])
    o_ref[...] = x_ref[...] + gate_f * _rms(y, g3_ref[...])


def _moe_combine(x, yg, meta, mod, g3, *, rows_per_mod, mod_base):
    rows, d = x.shape
    tm = min(512, rows)
    nt = rows // tm
    kern = functools.partial(_moe_out_kernel, tiles_per_mod=max(rows_per_mod // tm, 1), mod_base=mod_base)
    return pl.pallas_call(
        kern,
        grid=(nt,),
        in_specs=[
            pl.BlockSpec((tm, d), lambda i: (i, 0)),
            pl.BlockSpec((tm, d // 2), lambda i: (i, 0)),
            pl.BlockSpec((tm, d // 2), lambda i: (nt + i, 0)),
            pl.BlockSpec((tm, 128), lambda i: (i, 0)),
            pl.BlockSpec(mod.shape, lambda i: (0, 0)),
            pl.BlockSpec((1, d), lambda i: (0, 0)),
        ],
        out_specs=pl.BlockSpec((tm, d), lambda i: (i, 0)),
        out_shape=jax.ShapeDtypeStruct((rows, d), F32),
        compiler_params=_cparams("arbitrary"),
        name="moe_combine",
    )(x, yg, yg, meta, mod, g3.reshape(1, d))


def _moe_sparse(x, h, meta, mod, g3, wg, wu, wd, *, rows_per_mod, mod_base):
    rows = x.shape[0]
    pos, n_slots, tile_e, n_valid = _moe_plan(meta, rows)
    hs = _sc_scatter(h, pos, n_slots)
    ys = _moe_grouped(hs, tile_e, n_valid, wg, wu, wd)
    yg = _sc_gather(ys, pos)
    return _moe_combine(x, yg, meta, mod, g3, rows_per_mod=rows_per_mod, mod_base=mod_base)


def _cast_kernel(w_ref, o_ref, *, scale):
    w = w_ref[...]
    o_ref[...] = (w if scale == 1.0 else w * scale).astype(BF16)


def _cast_bf16(w_stack, layer, scale=1.0):
    squeeze = w_stack.ndim == 3
    w4 = w_stack[:, None] if squeeze else w_stack
    _, n_e, k, n = w4.shape
    bk = min(k, 256)
    out = pl.pallas_call(
        functools.partial(_cast_kernel, scale=scale),
        grid=(n_e, k // bk),
        in_specs=[pl.BlockSpec((None, None, bk, n), lambda e, i: (layer, e, i, 0))],
        out_specs=pl.BlockSpec((None, bk, n), lambda e, i: (e, i, 0)),
        out_shape=jax.ShapeDtypeStruct((n_e, k, n), BF16),
        compiler_params=_cparams("arbitrary", "arbitrary"),
        name="cast_weights",
    )(w4)
    return out[0] if squeeze else out


def _permute_w_in(w_in_stack, layer):
    _, k, n = w_in_stack.shape
    n_blocks = n // BRANCH_W
    shift = 9
    n_gate_blocks = N_BRANCH * D_MODEL // BRANCH_W

    def permute_kernel(w_ref, o_ref):
        scale = jnp.where(pl.program_id(0) < n_gate_blocks, 0.5, 1.0)
        o_ref[...] = (w_ref[...] * scale).astype(BF16)

    return pl.pallas_call(
        permute_kernel,
        grid=(n_blocks,),
        in_specs=[pl.BlockSpec((None, k, BRANCH_W), lambda j: (layer, 0, (j + shift) % n_blocks))],
        out_specs=pl.BlockSpec((k, BRANCH_W), lambda j: (0, j)),
        out_shape=jax.ShapeDtypeStruct((k, n), BF16),
        compiler_params=_cparams("arbitrary"),
        name="cast_permute_w_in",
    )(w_in_stack)


def kernel(x, c, ctx, c_ctx, w_mod, b_mod, norm_g, w_in, s5_a_re, s5_a_im, s5_log_dt, s5_b_re, s5_b_im, s5_c_re, s5_c_im, s5_d, s5_w_glu, s5_b_glu, ret_decay, ret_gn, na_rpb, w_branch, w_out, ffn_w_gate, ffn_w_up, ffn_w_down, moe_w_router, moe_b_router, moe_w_gate, moe_w_up, moe_w_down):
    batch, seq_len, d = x.shape
    ctx_len = ctx.shape[1]
    depth = w_mod.shape[0]
    cond = jnp.concatenate([c, c_ctx[None, :]], axis=0)
    mod_all = _modulation(cond, w_mod, b_mod)
    rope = _rope_tables(seq_len)
    lane_h = np.repeat(np.arange(RET_HEADS), RET_DIM)
    avg = jnp.asarray((lane_h[:, None] == lane_h[None, :]).astype(np.float32) / RET_DIM, BF16)

    xl = x.reshape(batch * seq_len, d)
    xc = ctx.reshape(batch * ctx_len, d)
    lat = dict(rows_per_mod=seq_len, mod_base=0)
    cxt = dict(rows_per_mod=batch * ctx_len, mod_base=batch)

    for layer in range(depth):
        last = layer == depth - 1
        need_ctx = not last
        mod = mod_all[layer]
        ng = norm_g[layer]
        w_in_bf = _permute_w_in(w_in, layer)
        s5_tabs = _s5_tables(s5_a_re[layer], s5_a_im[layer], s5_log_dt[layer], s5_b_re[layer], s5_b_im[layer],
                             s5_c_re[layer], s5_c_im[layer], s5_d[layer], batch)
        ret_tabs = _ret_tables(ret_decay[layer])
        na_bias, na_hmask = _na_tables(na_rpb[layer])
        lw = dict(w_glu=s5_w_glu[layer].astype(BF16), b_glu=s5_b_glu[layer].reshape(1, BRANCH_W).astype(F32),
                  ret_gn=ret_gn[layer].reshape(1, BRANCH_W).astype(F32), avg=avg,
                  w_branch=_cast_bf16(w_branch, layer, 0.5), w_out=_cast_bf16(w_out, layer))

        proj_l, f_l = _in_proj(xl, mod, ng[0], w_in_bf, **lat)
        proj_c, f_c = _in_proj(xc, mod, ng[0], w_in_bf, **cxt)

        a_l = _fourier_latent(f_l, batch, seq_len)
        s_l, s_c = _s5_mixer(proj_l[:, COL_S * BRANCH_W:(COL_S + 1) * BRANCH_W],
                             proj_c[:, COL_S * BRANCH_W:(COL_S + 1) * BRANCH_W], s5_tabs, batch)
        r_l, r_c = _retention(proj_l, proj_c, ret_tabs, rope, batch, seq_len, ctx_len)
        n_l, n_c = _neighborhood(proj_l, proj_c, na_bias, na_hmask, batch, seq_len, ctx_len, need_ctx)

        xl = _merge(xl, mod, ng[1], proj_l, a_l, s_l, r_l, n_l, lw, **lat)
        if need_ctx:
            a_c = _fourier_ctx(f_c, batch, ctx_len)
            xc = _merge(xc, mod, ng[1], proj_c, a_c, s_c, r_c, n_c, lw, **cxt)

        i = layer // 2
        if layer % 2 == 0:
            wg, wu, wd = _cast_bf16(ffn_w_gate, i), _cast_bf16(ffn_w_up, i), _cast_bf16(ffn_w_down, i)
            xl = _ffn_dense(xl, mod, ng[2], ng[3], wg, wu, wd, **lat)
            if need_ctx:
                xc = _ffn_dense(xc, mod, ng[2], ng[3], wg, wu, wd, **cxt)
        else:
            wg, wu, wd = moe_w_gate[i], moe_w_up[i], moe_w_down[i]
            h, meta = _router(xl, mod, ng[2], moe_w_router[i], moe_b_router[i], **lat)
            xl = _moe_sparse(xl, h, meta, mod, ng[3], wg, wu, wd, **lat)
            if need_ctx:
                hc, metac = _router(xc, mod, ng[2], moe_w_router[i], moe_b_router[i], **cxt)
                xc = _moe_sparse(xc, hc, metac, mod, ng[3], wg, wu, wd, **cxt)
    return xl.reshape(batch, seq_len, d)
```

```python
import functools
import math

import numpy as np
import jax
import jax.numpy as jnp
from jax import lax
from jax.experimental import pallas as pl
from jax.experimental.pallas import tpu as pltpu
from jax.experimental.pallas import tpu_sc as plsc

F32 = jnp.float32
BF16 = jnp.bfloat16

D_MODEL = 1024
BRANCH_W = 256
N_BRANCH = 4
GRID_W = 64
FNET_GROUP_DIM = 64
S5_GROUP_CH = 16
S5_GROUPS = 16
S5_STATE = 64
S5_CHUNK = 32
S5_PAIRS = S5_GROUPS // 2
RET_HEADS = 4
RET_DIM = 64
RET_CHUNK = 128
NA_HEADS = 4
NA_DIM = 64
NA_WIN_ROWS = 8
NA_WIN_COLS = 16
NA_QROWS = 8
ROPE_BASE = 10000.0
N_EXPERTS = 8
EPS = 1e-6
FFT_N2 = 256
NEG_BIG = -1e30
VMEM_LIMIT_BYTES = 50 * 1024 * 1024
SC_CORES = 2
SC_SUBCORES = 16
SC_WORKERS = SC_CORES * SC_SUBCORES
SC_GATHER_ROWS = 64
MOE_ROW_TILE = 1024
MOE_FF_TILE = 512

COL_F, COL_S, COL_RQ, COL_RK, COL_RV, COL_RG, COL_NQ, COL_NK, COL_NV = range(16, 25)
IN_W = 9 * BRANCH_W + N_BRANCH * D_MODEL
IN_TN = 1280
IN_F_TILE = (N_BRANCH * D_MODEL) // IN_TN
IN_F_OFF = N_BRANCH * D_MODEL - IN_F_TILE * IN_TN


def _cparams(*sem):
    return pltpu.CompilerParams(dimension_semantics=sem, vmem_limit_bytes=VMEM_LIMIT_BYTES)


def _sigmoid(v):
    return 0.5 * jnp.tanh(0.5 * v) + 0.5


def _silu(v):
    return v * _sigmoid(v)


def _gelu_tanh(v):
    return 0.5 * v * (1.0 + jnp.tanh(math.sqrt(2.0 / math.pi) * (v + 0.044715 * (v * v * v))))


def _rms(v, g):
    ms = jnp.mean(v * v, axis=-1, keepdims=True)
    return v * lax.rsqrt(ms + EPS) * g


def _split_bf16(v):
    hi = v.astype(BF16)
    lo = (v - hi.astype(F32)).astype(BF16)
    return hi, lo


def _pack_pairs(v):
    n = v.shape[1] // 2
    lo = lax.bitcast_convert_type(v[:, :n].astype(BF16).astype(F32), jnp.int32)
    hi = lax.bitcast_convert_type(v[:, n:].astype(BF16).astype(F32), jnp.int32)
    return (hi & -65536) | ((lo >> 16) & 65535)


def _unpack_pairs(w):
    lo = lax.bitcast_convert_type(w << 16, F32)
    hi = lax.bitcast_convert_type(w & -65536, F32)
    return jnp.concatenate([lo, hi], axis=-1)


def _dot(a, b):
    return jnp.dot(a, b, preferred_element_type=F32)


def _dot_nt(a, b):
    return lax.dot_general(a, b, (((1,), (1,)), ((), ())), preferred_element_type=F32)


def _dot_tn(a, b):
    return lax.dot_general(a, b, (((0,), (0,)), ((), ())), preferred_element_type=F32)


def _mod_kernel(ct_ref, w_ref, b_ref, o_ref, *, n_cond):
    ct = ct_ref[...]
    s = _silu(ct)
    w = w_ref[...]
    rows = [jnp.sum(w * s[:, r:r + 1], axis=0, keepdims=True) for r in range(n_cond)]
    rows.append(jnp.zeros((8 - n_cond, w.shape[1]), F32))
    o_ref[...] = jnp.concatenate(rows, axis=0) + b_ref[...]


def _modulation(cond, w_mod, b_mod):
    n_layers, d, n = w_mod.shape
    tn = 512
    ct = jnp.zeros((8, d), F32).at[:cond.shape[0]].set(cond).T
    return pl.pallas_call(
        functools.partial(_mod_kernel, n_cond=cond.shape[0]),
        grid=(n_layers, n // tn),
        in_specs=[
            pl.BlockSpec((d, 8), lambda l, j: (0, 0)),
            pl.BlockSpec((None, d, tn), lambda l, j: (l, 0, j)),
            pl.BlockSpec((None, 1, tn), lambda l, j: (l, 0, j)),
        ],
        out_specs=pl.BlockSpec((None, 8, tn), lambda l, j: (l, 0, j)),
        out_shape=jax.ShapeDtypeStruct((n_layers, 8, n), F32),
        compiler_params=_cparams("arbitrary", "arbitrary"),
        name="adaln_mod",
    )(ct, w_mod, b_mod.reshape(n_layers, 1, n))


def _mod_rows(mod_ref, i, tiles_per_mod, mod_base, first):
    r = mod_base + i // tiles_per_mod
    return [mod_ref[pl.ds(r, 1), (first + k) * D_MODEL:(first + k + 1) * D_MODEL] for k in range(3)]


def _in_kernel(x_ref, mod_ref, g_ref, w_ref, proj_ref, f_ref, h_scr, *, tiles_per_mod, mod_base):
    i = pl.program_id(0)
    j = pl.program_id(1)

    @pl.when(j == 0)
    def _():
        sh, sc, _ = _mod_rows(mod_ref, i, tiles_per_mod, mod_base, 0)
        h_scr[...] = (_rms(x_ref[...], g_ref[...]) * (1.0 + sc) + sh).astype(BF16)

    res = _dot(h_scr[...], w_ref[...])
    proj_ref[...] = res.astype(BF16)

    @pl.when(j == IN_F_TILE)
    def _():
        f_ref[...] = res[:, IN_F_OFF:IN_F_OFF + BRANCH_W].astype(BF16)


def _in_proj(x, mod, g, w_bf, *, rows_per_mod, mod_base):
    rows, d = x.shape
    tm = math.gcd(1024, rows_per_mod)
    kern = functools.partial(_in_kernel, tiles_per_mod=max(rows_per_mod // tm, 1), mod_base=mod_base)
    return pl.pallas_call(
        kern,
        grid=(rows // tm, IN_W // IN_TN),
        in_specs=[
            pl.BlockSpec((tm, d), lambda i, j: (i, 0)),
            pl.BlockSpec(mod.shape, lambda i, j: (0, 0)),
            pl.BlockSpec((1, d), lambda i, j: (0, 0)),
            pl.BlockSpec((d, IN_TN), lambda i, j: (0, j)),
        ],
        out_specs=[
            pl.BlockSpec((tm, IN_TN), lambda i, j: (i, j)),
            pl.BlockSpec((tm, BRANCH_W), lambda i, j: (i, 0)),
        ],
        out_shape=[
            jax.ShapeDtypeStruct((rows, IN_W), BF16),
            jax.ShapeDtypeStruct((rows, BRANCH_W), BF16),
        ],
        scratch_shapes=[pltpu.VMEM((tm, d), BF16)],
        compiler_params=_cparams("arbitrary", "arbitrary"),
        name="in_proj",
    )(x, mod, g.reshape(1, d), w_bf)


def _fft_a_kernel(x_ref, cs_ref, tc_ref, ts_ref, zr_ref, zi_ref, *, n1, n1p):
    y = _dot(cs_ref[...].astype(BF16), x_ref[...])
    yr = y[:n1]
    yi = y[n1p:n1p + n1]
    tc = tc_ref[...]
    ts = ts_ref[...]
    zr_ref[...] = (yr * tc + yi * ts).astype(BF16)
    zi_ref[...] = (yi * tc - yr * ts).astype(BF16)


def _fft_b_kernel(zr_ref, zi_ref, cs_ref, cc_ref, sc_ref, o_ref, *, kb, scale, has_imag):
    cs = cs_ref[...].astype(BF16)
    cc = cc_ref[...].astype(BF16)
    sc = sc_ref[...].astype(BF16)
    for kk in range(kb):
        a = _dot(cs, zr_ref[kk])
        if has_imag:
            b = _dot(cs, zi_ref[kk])
            xr = a[:FFT_N2] + b[FFT_N2:]
            xi = b[:FFT_N2] - a[FFT_N2:]
        else:
            xr = a[:FFT_N2]
            xi = -a[FFT_N2:]
        out = _dot(xr.astype(BF16), cc) + _dot(xi.astype(BF16), sc)
        o_ref[:, kk * BRANCH_W:(kk + 1) * BRANCH_W] = (out * scale).astype(BF16)


def _dft_tables(n):
    k = np.arange(n)
    ang = 2.0 * np.pi * ((k[:, None] * k[None, :]) % n) / n
    return np.cos(ang), np.sin(ang)


def _fft_b_call(zr, zi, n1, batch, seq_len, has_imag):
    c2, s2 = _dft_tables(FFT_N2)
    cs2 = jnp.asarray(np.concatenate([c2, s2], axis=0), F32)
    c64, s64 = _dft_tables(FNET_GROUP_DIM)
    eye = np.eye(BRANCH_W // FNET_GROUP_DIM)
    cc = jnp.asarray(np.kron(eye, c64), F32)
    sc = jnp.asarray(np.kron(eye, s64), F32)
    kb = min(8, n1)
    scale = 1.0 / math.sqrt(seq_len * FNET_GROUP_DIM)
    kern = functools.partial(_fft_b_kernel, kb=kb, scale=scale, has_imag=has_imag)
    zspec = pl.BlockSpec((None, kb, FFT_N2, BRANCH_W), lambda b, i: (b, i, 0, 0))
    out = pl.pallas_call(
        kern,
        grid=(batch, n1 // kb),
        in_specs=[
            zspec, zspec,
            pl.BlockSpec((2 * FFT_N2, FFT_N2), lambda b, i: (0, 0)),
            pl.BlockSpec((BRANCH_W, BRANCH_W), lambda b, i: (0, 0)),
            pl.BlockSpec((BRANCH_W, BRANCH_W), lambda b, i: (0, 0)),
        ],
        out_specs=pl.BlockSpec((None, FFT_N2, kb * BRANCH_W), lambda b, i: (b, 0, i)),
        out_shape=jax.ShapeDtypeStruct((batch, FFT_N2, n1 * BRANCH_W), BF16),
        compiler_params=_cparams("arbitrary", "arbitrary"),
        name="fourier_stage_b",
    )(zr, zi, cs2, cc, sc)
    return out.reshape(batch * seq_len, BRANCH_W)


def _fourier_latent(f, batch, seq_len):
    n1 = seq_len // FFT_N2
    wide = FFT_N2 * BRANCH_W
    c1, s1 = _dft_tables(n1)
    n1p = max(n1, 8)
    cs1 = np.zeros((2 * n1p, n1))
    cs1[:n1] = c1
    cs1[n1p:n1p + n1] = -s1
    k1 = np.arange(n1)[:, None]
    l2 = np.arange(FFT_N2)[None, :]
    tw = 2.0 * np.pi * (k1 * l2) / seq_len
    tc = jnp.asarray(np.repeat(np.cos(tw), BRANCH_W, axis=1), F32)
    ts = jnp.asarray(np.repeat(np.sin(tw), BRANCH_W, axis=1), F32)
    cw = min(8192, wide)
    xv = f.reshape(batch, n1, wide)
    spec = pl.BlockSpec((None, n1, cw), lambda b, j: (b, 0, j))
    tspec = pl.BlockSpec((n1, cw), lambda b, j: (0, j))
    zr, zi = pl.pallas_call(
        functools.partial(_fft_a_kernel, n1=n1, n1p=n1p),
        grid=(batch, wide // cw),
        in_specs=[spec, pl.BlockSpec((2 * n1p, n1), lambda b, j: (0, 0)), tspec, tspec],
        out_specs=[spec, spec],
        out_shape=[jax.ShapeDtypeStruct((batch, n1, wide), BF16)] * 2,
        compiler_params=_cparams("arbitrary", "arbitrary"),
        name="fourier_stage_a",
    )(xv, jnp.asarray(cs1, F32), tc, ts)
    zr = zr.reshape(batch, n1, FFT_N2, BRANCH_W)
    zi = zi.reshape(batch, n1, FFT_N2, BRANCH_W)
    return _fft_b_call(zr, zi, n1, batch, seq_len, True)


def _fourier_ctx(f, batch, ctx_len):
    assert ctx_len == FFT_N2
    z = f.reshape(batch, 1, FFT_N2, BRANCH_W)
    return _fft_b_call(z, z, 1, batch, ctx_len, False)


def _s5_tables(a_re, a_im, log_dt, b_re, b_im, c_re, c_im, d_skip, batch):
    t = S5_CHUNK
    g, p, hc = S5_GROUPS, S5_STATE, S5_GROUP_CH
    lam = lax.complex(a_re.astype(F32), a_im.astype(F32))
    dt = jnp.exp(log_dt.astype(F32))[..., None]
    ks = jnp.arange(t + 1, dtype=F32)
    apow = jnp.exp((lam * dt)[..., None] * ks)
    a_bar = apow[..., 1]
    b_bar = ((a_bar - 1.0) / lam)[..., None] * lax.complex(b_re.astype(F32), b_im.astype(F32))
    cm = lax.complex(c_re.astype(F32), c_im.astype(F32))
    kimp = jnp.real(jnp.einsum('dghp,dgpk,dgpj->dgkhj', cm, apow[..., :t], b_bar,
                               precision=lax.Precision.HIGHEST))
    kf, kb = kimp[0], kimp[1]
    kfull = jnp.concatenate([kb[:, :0:-1], kf[:, :1] + kb[:, :1], kf[:, 1:]], axis=1)
    strip = kfull.transpose(0, 3, 1, 2).reshape(g, hc, (2 * t - 1) * hc)
    strip = jnp.pad(strip, ((0, 0), (0, 0), (0, 2 * t * hc - strip.shape[-1])))
    strip = strip.reshape(S5_PAIRS, 2, hc, 2 * t * hc)

    def pair_blocks(kd, axis):
        r, c = kd.shape[1:]
        kp = kd.reshape(S5_PAIRS, 2, r, c)
        z = jnp.zeros_like(kp[:, 0])
        top = jnp.concatenate([kp[:, 0], z], axis=-1)
        bot = jnp.concatenate([z, kp[:, 1]], axis=-1)
        return jnp.concatenate([top, bot], axis=1)

    wf = jnp.einsum('gpj,gph->gjhp', apow[0][..., t - 1::-1][..., :t], b_bar[0])
    wb = jnp.einsum('gpj,gph->gjhp', apow[1][..., :t], b_bar[1])
    wf = wf.reshape(g, t * hc, p)
    wb = wb.reshape(g, t * hc, p)
    kinds = [jnp.real(wf), jnp.imag(wf), jnp.real(wb), jnp.imag(wb)]
    we = jnp.concatenate([pair_blocks(kd, 0) for kd in kinds], axis=-1).astype(BF16)

    vf = jnp.einsum('ghp,gpt->gpth', cm[0], apow[0][..., 1:t + 1])
    vb = jnp.einsum('ghp,gpt->gpth', cm[1], apow[1][..., t:0:-1])
    vf = vf.reshape(g, p, t * hc)
    vb = vb.reshape(g, p, t * hc)
    vkinds = [jnp.real(vf), -jnp.imag(vf), jnp.real(vb), -jnp.imag(vb)]
    v1 = jnp.concatenate([pair_blocks(kd, 0) for kd in vkinds], axis=1)
    v = jnp.concatenate([v1, v1], axis=1).astype(BF16)

    def lanes(z):
        return jnp.tile(z.reshape(1, g * p), (1, batch))

    at = apow[..., t]
    a_tab = jnp.concatenate([lanes(jnp.real(at[0])), lanes(jnp.imag(at[0])),
                             lanes(jnp.real(at[1])), lanes(jnp.imag(at[1]))], axis=0)
    dvec = jnp.tile(d_skip.astype(F32).reshape(S5_PAIRS, 2, 1, hc), (1, 1, t, 1)).reshape(S5_PAIRS, 1, 2 * t * hc)
    return dict(strip=strip, we=we, v=v, a_tab=a_tab, dvec=dvec)


def _s5_e_kernel(u_ref, we_ref, ref_, imf_, reb_, imb_):
    e = _dot(u_ref[...], we_ref[...])
    ref_[...] = e[:, 0:128]
    imf_[...] = e[:, 128:256]
    reb_[...] = e[:, 256:384]
    imb_[...] = e[:, 384:512]


def _s5_scan_kernel(a_ref, ref_, imf_, reb_, imb_, prf, pif, prb, pib, *, n_rows, n_ctx):
    afr = a_ref[0:1, :]
    afi = a_ref[1:2, :]
    abr = a_ref[2:3, :]
    abi = a_ref[3:4, :]
    zero = jnp.zeros_like(afr)

    def body(s, carry):
        sfr, sfi, sbr, sbi = carry
        nf = s
        nb = jnp.where(s < n_ctx, n_ctx - 1 - s, n_rows - 1 + n_ctx - s)
        prf[pl.ds(nf, 1), :] = sfr
        pif[pl.ds(nf, 1), :] = sfi
        prb[pl.ds(nb, 1), :] = sbr
        pib[pl.ds(nb, 1), :] = sbi
        efr = ref_[pl.ds(nf, 1), :]
        efi = imf_[pl.ds(nf, 1), :]
        ebr = reb_[pl.ds(nb, 1), :]
        ebi = imb_[pl.ds(nb, 1), :]
        nfr = afr * sfr - afi * sfi + efr
        nfi = afr * sfi + afi * sfr + efi
        nbr = abr * sbr - abi * sbi + ebr
        nbi = abr * sbi + abi * sbr + ebi
        return nfr, nfi, nbr, nbi

    lax.fori_loop(0, n_rows, body, (zero, zero, zero, zero))


def _s5_y_kernel(u_ref, strip_ref, v_ref, d_ref, prf, pif, prb, pib, yc_ref, yl_ref, m_scr):
    half = S5_CHUNK * S5_GROUP_CH
    n_ctx = yc_ref.shape[0]

    @pl.when(pl.program_id(1) == 0)
    def _():
        for gi in range(2):
            strip = strip_ref[gi]
            for j in range(S5_CHUNK):
                off = (S5_CHUNK - 1 - j) * S5_GROUP_CH
                win = strip if off == 0 else pltpu.roll(strip, 2 * half - off, axis=1)
                m_scr[gi, j * S5_GROUP_CH:(j + 1) * S5_GROUP_CH, :] = win[:, :half].astype(BF16)

    u = u_ref[...]
    y_intra = jnp.concatenate([_dot(u[:, :half], m_scr[0]), _dot(u[:, half:], m_scr[1])], axis=-1)
    pcat = jnp.concatenate([prf[...], pif[...], prb[...], pib[...]], axis=-1)
    hi, lo = _split_bf16(pcat)
    y_cross = _dot(jnp.concatenate([hi, lo], axis=-1), v_ref[...])
    y = y_intra + y_cross + d_ref[...] * u.astype(F32)
    yc_ref[...] = y[:n_ctx].astype(BF16)
    yl_ref[...] = y[n_ctx:].astype(BF16)


def _s5_core(u, tabs, batch, n_rows, n_ctx):
    width = batch * S5_PAIRS * 128
    cols = 2 * S5_CHUNK * S5_GROUP_CH
    u_spec = pl.BlockSpec((None, None, n_rows, cols), lambda q, b: (q, b, 0, 0))
    st_spec = pl.BlockSpec((n_rows, 128), lambda q, b: (0, b * S5_PAIRS + q))
    st_shape = jax.ShapeDtypeStruct((n_rows, width), F32)
    e4 = pl.pallas_call(
        _s5_e_kernel,
        grid=(S5_PAIRS, batch),
        in_specs=[u_spec, pl.BlockSpec((None, cols, 512), lambda q, b: (q, 0, 0))],
        out_specs=[st_spec] * 4,
        out_shape=[st_shape] * 4,
        compiler_params=_cparams("arbitrary", "arbitrary"),
        name="s5_chunk_states",
    )(u, tabs['we'])
    p4 = pl.pallas_call(
        functools.partial(_s5_scan_kernel, n_rows=n_rows, n_ctx=n_ctx),
        out_shape=[st_shape] * 4,
        compiler_params=pltpu.CompilerParams(vmem_limit_bytes=VMEM_LIMIT_BYTES),
        name="s5_state_scan",
    )(tabs['a_tab'], *e4)
    y = pl.pallas_call(
        _s5_y_kernel,
        grid=(S5_PAIRS, batch),
        in_specs=[
            u_spec,
            pl.BlockSpec((None, 2, S5_GROUP_CH, cols), lambda q, b: (q, 0, 0, 0)),
            pl.BlockSpec((None, cols, cols), lambda q, b: (q, 0, 0)),
            pl.BlockSpec((None, 1, cols), lambda q, b: (q, 0, 0)),
            st_spec, st_spec, st_spec, st_spec,
        ],
        out_specs=[
            pl.BlockSpec((None, None, n_ctx, cols), lambda q, b: (q, b, 0, 0)),
            pl.BlockSpec((None, None, n_rows - n_ctx, cols), lambda q, b: (q, b, 0, 0)),
        ],
        out_shape=[
            jax.ShapeDtypeStruct((S5_PAIRS, batch, n_ctx, cols), BF16),
            jax.ShapeDtypeStruct((S5_PAIRS, batch, n_rows - n_ctx, cols), BF16),
        ],
        scratch_shapes=[pltpu.VMEM((2, cols // 2, cols // 2), BF16)],
        compiler_params=_cparams("arbitrary", "arbitrary"),
        name="s5_outputs",
    )(u, tabs['strip'], tabs['v'], tabs['dvec'], *p4)
    return y


def _s5_to_chunks(s, batch):
    n = s.shape[0] // batch // S5_CHUNK
    v = s.reshape(batch, n, S5_CHUNK, S5_PAIRS, 2, S5_GROUP_CH)
    return v.transpose(3, 0, 1, 4, 2, 5).reshape(S5_PAIRS, batch, n, 2 * S5_CHUNK * S5_GROUP_CH)


def _s5_from_chunks(y, batch):
    n = y.shape[2]
    v = y.reshape(S5_PAIRS, batch, n, 2, S5_CHUNK, S5_GROUP_CH)
    return v.transpose(1, 2, 4, 0, 3, 5).reshape(batch * n * S5_CHUNK, BRANCH_W)


def _s5_mixer(s_lat, s_ctx, tabs, batch):
    ul = _s5_to_chunks(s_lat, batch)
    uc = _s5_to_chunks(s_ctx, batch)
    n_ctx = uc.shape[2]
    u = jnp.concatenate([uc, ul], axis=2)
    yc, yl = _s5_core(u, tabs, batch, u.shape[2], n_ctx)
    return _s5_from_chunks(yl, batch), _s5_from_chunks(yc, batch)


def _ret_tables(ret_decay):
    c = RET_CHUNK
    lg = jax.nn.log_sigmoid(ret_decay.astype(F32))
    lane_h = jnp.repeat(jnp.arange(RET_HEADS), RET_DIM)
    lgl = lg[:, lane_h]
    pos = jnp.arange(c, dtype=F32)[:, None]
    qd = jnp.stack([jnp.exp((pos + 1.0) * lgl[0][None]), jnp.exp((c - pos) * lgl[1][None])])
    kd = jnp.stack([jnp.exp((c - 1.0 - pos) * lgl[0][None]), jnp.exp(pos * lgl[1][None])])
    bmask = (lane_h[:, None] == lane_h[None, :]).astype(F32)
    cd = jnp.exp(c * lgl)[:, :, None] * bmask[None]
    diff = pos - pos.T
    dm = []
    for h in range(RET_HEADS):
        fw = jnp.where(diff >= 0, jnp.exp(jnp.maximum(diff, 0.0) * lg[0, h]), 0.0)
        bw = jnp.where(diff <= 0, jnp.exp(jnp.maximum(-diff, 0.0) * lg[1, h]), 0.0)
        dm.append(fw + bw)
    dm = jnp.concatenate(dm, axis=0)
    hmask = (jnp.arange(RET_HEADS)[:, None] == lane_h[None, :]).astype(F32)
    return dict(qd=qd, kd=kd, cd=cd, bmask=bmask, dm=dm, hmask=hmask)


def _rope_tables(n_tokens):
    t = np.arange(n_tokens)
    row = (t // GRID_W).astype(np.float64)
    col = (t % GRID_W).astype(np.float64)
    n_freq = RET_DIM // 4
    inv_freq = 1.0 / (ROPE_BASE ** (np.arange(n_freq, dtype=np.float64) / n_freq))
    ang = np.concatenate([row[:, None] * inv_freq, col[:, None] * inv_freq], axis=-1)
    cos = np.cos(ang)
    sin = np.sin(ang)
    cos_t = np.tile(np.concatenate([cos, cos], axis=-1), (1, RET_HEADS))
    sin_t = np.tile(np.concatenate([-sin, sin], axis=-1), (1, RET_HEADS))
    half = RET_DIM // 2
    perm = np.arange(BRANCH_W) ^ half
    swap = np.zeros((BRANCH_W, BRANCH_W), np.float32)
    swap[perm, np.arange(BRANCH_W)] = 1.0
    return jnp.asarray(cos_t, F32), jnp.asarray(sin_t, F32), jnp.asarray(swap, BF16)


def _ret_chunk(q, k, v, s, qd, kd, cd, bmask, dm, hmask, with_intra):
    cross = _dot((q * qd).astype(BF16), s.astype(BF16))
    s_new = cd * s + bmask * _dot_tn((k * kd).astype(BF16), v)
    if not with_intra:
        return cross, s_new
    qb = q.astype(BF16)
    kb = k.astype(BF16)
    qs = jnp.concatenate([qb * hmask[h:h + 1].astype(BF16) for h in range(RET_HEADS)], axis=0)
    scores = _dot_nt(qs, kb) * dm
    ov = _dot(scores.astype(BF16), v)
    c = q.shape[0]
    inner = ov[0:c] * hmask[0:1]
    for h in range(1, RET_HEADS):
        inner = inner + ov[h * c:(h + 1) * c] * hmask[h:h + 1]
    return inner + cross, s_new


def _ret_kernel(qf_ref, kf_ref, vf_ref, qb_ref, kb_ref, vb_ref, qc_ref, kc_ref, vc_ref,
                cosf_ref, sinf_ref, cosb_ref, sinb_ref, swap_ref,
                qd_ref, kd_ref, cd_ref, bm_ref, dm_ref, hm_ref,
                of_ref, ob_ref, ocf_ref, ocb_ref, sf_scr, sb_scr, *, n_chunks, n_ctx_chunks):
    i = pl.program_id(1)
    c = RET_CHUNK
    k_scale = RET_DIM ** -0.5
    bmask = bm_ref[...]
    dm = dm_ref[...]
    hmask = hm_ref[...]
    tabs = [(qd_ref[d], kd_ref[d], cd_ref[d]) for d in range(2)]

    @pl.when(i == 0)
    def _():
        for d, oc_ref, s_scr in ((0, ocf_ref, sf_scr), (1, ocb_ref, sb_scr)):
            qd, kd, cd = tabs[d]
            s = jnp.zeros((BRANCH_W, BRANCH_W), F32)
            order = range(n_ctx_chunks) if d == 0 else range(n_ctx_chunks - 1, -1, -1)
            for cc in order:
                sl = slice(cc * c, (cc + 1) * c)
                o, s = _ret_chunk(qc_ref[sl, :].astype(F32), kc_ref[sl, :].astype(F32) * k_scale, vc_ref[sl, :],
                                  s, qd, kd, cd, bmask, dm, hmask, d == 0)
                oc_ref[sl, :] = o
            s_scr[...] = s

    swap = swap_ref[...]

    def rope(x_ref, cos_ref, sin_ref, sl):
        xb = x_ref[sl, :]
        return xb.astype(F32) * cos_ref[sl, :] + _dot(xb, swap) * sin_ref[sl, :]

    sf = sf_scr[...]
    sb = sb_scr[...]
    for step in range(n_chunks):
        sl = slice(step * c, (step + 1) * c)
        q = rope(qf_ref, cosf_ref, sinf_ref, sl)
        k = rope(kf_ref, cosf_ref, sinf_ref, sl) * k_scale
        o, sf = _ret_chunk(q, k, vf_ref[sl, :], sf, *tabs[0], bmask, dm, hmask, True)
        of_ref[sl, :] = o
        cb = n_chunks - 1 - step
        sl = slice(cb * c, (cb + 1) * c)
        q = rope(qb_ref, cosb_ref, sinb_ref, sl)
        k = rope(kb_ref, cosb_ref, sinb_ref, sl) * k_scale
        o, sb = _ret_chunk(q, k, vb_ref[sl, :], sb, *tabs[1], bmask, dm, hmask, False)
        ob_ref[sl, :] = o
    sf_scr[...] = sf
    sb_scr[...] = sb


def _retention(proj_l, proj_c, tabs, rope, batch, seq_len, ctx_len):
    n_chunks = 4
    blk = n_chunks * RET_CHUNK
    nblk = seq_len // blk
    cos_t, sin_t, swap = rope

    def lat(col, back):
        if back:
            return pl.BlockSpec((blk, BRANCH_W), lambda b, i: (b * nblk + nblk - 1 - i, col))
        return pl.BlockSpec((blk, BRANCH_W), lambda b, i: (b * nblk + i, col))

    def ctx(col):
        return pl.BlockSpec((ctx_len, BRANCH_W), lambda b, i: (b, col))

    def const(shape):
        return pl.BlockSpec(shape, lambda b, i: (0,) * len(shape))

    tab_f = pl.BlockSpec((blk, BRANCH_W), lambda b, i: (i, 0))
    tab_b = pl.BlockSpec((blk, BRANCH_W), lambda b, i: (nblk - 1 - i, 0))
    kern = functools.partial(_ret_kernel, n_chunks=n_chunks, n_ctx_chunks=ctx_len // RET_CHUNK)
    c = RET_CHUNK
    ctx_out = pl.BlockSpec((ctx_len, BRANCH_W), lambda b, i: (b, 0))
    o_f, o_b, oc_f, oc_b = pl.pallas_call(
        kern,
        grid=(batch, nblk),
        in_specs=[
            lat(COL_RQ, False), lat(COL_RK, False), lat(COL_RV, False),
            lat(COL_RQ, True), lat(COL_RK, True), lat(COL_RV, True),
            ctx(COL_RQ), ctx(COL_RK), ctx(COL_RV),
            tab_f, tab_f, tab_b, tab_b, const((BRANCH_W, BRANCH_W)),
            const((2, c, BRANCH_W)), const((2, c, BRANCH_W)), const((2, BRANCH_W, BRANCH_W)),
            const((BRANCH_W, BRANCH_W)), const((RET_HEADS * c, c)), const((RET_HEADS, BRANCH_W)),
        ],
        out_specs=[lat(0, False), lat(0, True), ctx_out, ctx_out],
        out_shape=[
            jax.ShapeDtypeStruct((batch * seq_len, BRANCH_W), F32),
            jax.ShapeDtypeStruct((batch * seq_len, BRANCH_W), F32),
            jax.ShapeDtypeStruct((batch * ctx_len, BRANCH_W), F32),
            jax.ShapeDtypeStruct((batch * ctx_len, BRANCH_W), F32),
        ],
        scratch_shapes=[pltpu.VMEM((BRANCH_W, BRANCH_W), F32), pltpu.VMEM((BRANCH_W, BRANCH_W), F32)],
        compiler_params=_cparams("arbitrary", "arbitrary"),
        name="retention",
    )(proj_l, proj_l, proj_l, proj_l, proj_l, proj_l, proj_c, proj_c, proj_c,
      cos_t, sin_t, cos_t, sin_t, swap,
      tabs['qd'], tabs['kd'], tabs['cd'], tabs['bmask'], tabs['dm'], tabs['hmask'])
    return (o_f, o_b), (oc_f, oc_b)


def _na_tables(rpb):
    kr, kw = NA_WIN_ROWS, NA_WIN_COLS
    col = np.arange(GRID_W)
    col_start = np.clip(col - kw // 2, 0, GRID_W - kw)
    in_win = (col[None, :] >= col_start[:, None]) & (col[None, :] < col_start[:, None] + kw)
    dc = np.clip(col[None, :] - col[:, None], -(kw - 1), kw - 1) + (kw - 1)
    var = np.arange(kr)[:, None] + np.arange(kr)[None, :]
    pick_r = (var[:, :, None] == np.arange(2 * kr - 1)[None, None, :]).astype(np.float32)
    pick_c = (dc[:, :, None] == np.arange(2 * kw - 1)[None, None, :]).astype(np.float32)
    bias = jnp.einsum('vir,hrc,qkc->vhqik', jnp.asarray(pick_r), rpb.astype(F32), jnp.asarray(pick_c),
                      precision=lax.Precision.HIGHEST)
    bias = jnp.where(jnp.asarray(in_win)[None, None, :, None, :], bias, NEG_BIG)
    bias = bias.reshape(kr, NA_HEADS * GRID_W, kr * GRID_W)
    lane_h = np.repeat(np.arange(NA_HEADS), NA_DIM)
    hmask = (np.arange(NA_HEADS)[:, None] == lane_h[None, :]).astype(np.float32)
    return bias, jnp.asarray(hmask, F32)


def _attend(qs, keys, vals, bias, kc, vc):
    s_ctx = _dot_nt(qs, kc)
    m = jnp.max(s_ctx, axis=-1, keepdims=True)
    if keys is not None:
        s_band = _dot_nt(qs, keys) + bias
        m = jnp.maximum(m, jnp.max(s_band, axis=-1, keepdims=True))
        p_band = jnp.exp(s_band - m)
    p_ctx = jnp.exp(s_ctx - m)
    l = jnp.sum(p_ctx, axis=-1, keepdims=True)
    o = _dot(p_ctx.astype(BF16), vc)
    if keys is not None:
        l = l + jnp.sum(p_band, axis=-1, keepdims=True)
        o = o + _dot(p_band.astype(BF16), vals)
    return o / l


def _stack_heads(q, hmask_scaled):
    return jnp.concatenate([q * hmask_scaled[h:h + 1] for h in range(NA_HEADS)], axis=0)


def _unstack_heads(o, hmask, n):
    out = o[0:n] * hmask[0:1]
    for h in range(1, NA_HEADS):
        out = out + o[h * n:(h + 1) * n] * hmask[h:h + 1]
    return out


def _na_kernel(q_ref, k_ref, v_ref, kc_ref, vc_ref, bias_ref, hm_ref, o_ref, *, n_grid_rows):
    i = pl.program_id(1)
    hmask = hm_ref[...]
    hms = (hmask * (NA_DIM ** -0.5)).astype(BF16)
    kc = kc_ref[...]
    vc = vc_ref[...]
    band = NA_WIN_ROWS * GRID_W
    for rr in range(NA_QROWS):
        r = i * NA_QROWS + rr
        rs = jnp.clip(r - NA_WIN_ROWS // 2, 0, n_grid_rows - NA_WIN_ROWS)
        var = rs - r + (NA_WIN_ROWS - 1)
        start = pl.multiple_of(rs * GRID_W, GRID_W)
        keys = k_ref[pl.ds(start, band), :]
        vals = v_ref[pl.ds(start, band), :]
        qs = _stack_heads(q_ref[rr * GRID_W:(rr + 1) * GRID_W, :], hms)
        o = _attend(qs, keys, vals, bias_ref[var], kc, vc)
        o_ref[rr * GRID_W:(rr + 1) * GRID_W, :] = _unstack_heads(o, hmask, GRID_W).astype(BF16)


def _na_ctx_kernel(q_ref, kc_ref, vc_ref, hm_ref, o_ref):
    hmask = hm_ref[...]
    hms = (hmask * (NA_DIM ** -0.5)).astype(BF16)
    n = q_ref.shape[0]
    o = _attend(_stack_heads(q_ref[...], hms), None, None, None, kc_ref[...], vc_ref[...])
    o_ref[...] = _unstack_heads(o, hmask, n).astype(BF16)


def _neighborhood(proj_l, proj_c, bias, hmask, batch, seq_len, ctx_len, need_ctx_out):
    rows = seq_len // GRID_W
    qblk = NA_QROWS * GRID_W
    nq = seq_len // qblk
    out_l = pl.pallas_call(
        functools.partial(_na_kernel, n_grid_rows=rows),
        grid=(batch, nq),
        in_specs=[
            pl.BlockSpec((qblk, BRANCH_W), lambda b, i: (b * nq + i, COL_NQ)),
            pl.BlockSpec((seq_len, BRANCH_W), lambda b, i: (b, COL_NK)),
            pl.BlockSpec((seq_len, BRANCH_W), lambda b, i: (b, COL_NV)),
            pl.BlockSpec((ctx_len, BRANCH_W), lambda b, i: (b, COL_NK)),
            pl.BlockSpec((ctx_len, BRANCH_W), lambda b, i: (b, COL_NV)),
            pl.BlockSpec(bias.shape, lambda b, i: (0, 0, 0)),
            pl.BlockSpec(hmask.shape, lambda b, i: (0, 0)),
        ],
        out_specs=pl.BlockSpec((qblk, BRANCH_W), lambda b, i: (b * nq + i, 0)),
        out_shape=jax.ShapeDtypeStruct((batch * seq_len, BRANCH_W), BF16),
        compiler_params=_cparams("arbitrary", "arbitrary"),
        name="neighborhood_attn",
    )(proj_l, proj_l, proj_l, proj_c, proj_c, bias, hmask)
    out_c = None
    if need_ctx_out:
        out_c = pl.pallas_call(
            _na_ctx_kernel,
            grid=(batch,),
            in_specs=[
                pl.BlockSpec((ctx_len, BRANCH_W), lambda b: (b, COL_NQ)),
                pl.BlockSpec((ctx_len, BRANCH_W), lambda b: (b, COL_NK)),
                pl.BlockSpec((ctx_len, BRANCH_W), lambda b: (b, COL_NV)),
                pl.BlockSpec(hmask.shape, lambda b: (0, 0)),
            ],
            out_specs=pl.BlockSpec((ctx_len, BRANCH_W), lambda b: (b, 0)),
            out_shape=jax.ShapeDtypeStruct((batch * ctx_len, BRANCH_W), BF16),
            compiler_params=_cparams("arbitrary"),
            name="context_attn",
        )(proj_c, proj_c, proj_c, hmask)
    return out_l, out_c


def _merge_kernel(x_ref, mod_ref, g_ref, gt0, gt1, gt2, gt3, a_ref, s5_ref, rof_ref, rob_ref, rg_ref, na_ref,
                  wglu_ref, bglu_ref, gn_ref, avg_ref, wb_ref, wo_ref, o_ref, *, tiles_per_mod, mod_base):
    i = pl.program_id(0)
    _, _, gate_a = _mod_rows(mod_ref, i, tiles_per_mod, mod_base, 0)
    z = _gelu_tanh(s5_ref[...].astype(F32)).astype(BF16)
    zf = z.astype(F32)
    b_s5 = (zf * _sigmoid(_dot(z, wglu_ref[...]) + bglu_ref[...])).astype(BF16)
    o = rof_ref[...] + rob_ref[...]
    avg = avg_ref[...]
    hi, lo = _split_bf16(o)
    mu = _dot(hi, avg) + _dot(lo, avg)
    dlt = o - mu
    hi, lo = _split_bf16(dlt * dlt)
    var = _dot(hi, avg) + _dot(lo, avg)
    hn = dlt * lax.rsqrt(var + EPS) * gn_ref[...]
    b_ret = (_silu(rg_ref[...].astype(F32)) * hn).astype(BF16)
    outs = (a_ref[...], b_s5, b_ret, na_ref[...])
    gates = (gt0, gt1, gt2, gt3)
    y = (1.0 + jnp.tanh(gates[0][...].astype(F32))) * _dot(outs[0], wb_ref[0])
    for b in range(1, N_BRANCH):
        y = y + (1.0 + jnp.tanh(gates[b][...].astype(F32))) * _dot(outs[b], wb_ref[b])
    yo = _dot(y.astype(BF16), wo_ref[...])
    o_ref[...] = x_ref[...] + gate_a * _rms(yo, g_ref[...])


def _merge(x, mod, g1, proj, a, s5y, ret_o, na, lw, *, rows_per_mod, mod_base):
    rows, d = x.shape
    tm = min(512, rows)
    nt = rows // tm

    def row(shape, col=0):
        return pl.BlockSpec(shape, lambda i: (i, col))

    def const(arr):
        return pl.BlockSpec(arr.shape, lambda i: (0,) * arr.ndim)

    kern = functools.partial(_merge_kernel, tiles_per_mod=max(rows_per_mod // tm, 1), mod_base=mod_base)
    ins = [x, mod, g1.reshape(1, d), proj, proj, proj, proj, a, s5y, ret_o[0], ret_o[1], proj, na,
           lw['w_glu'], lw['b_glu'], lw['ret_gn'], lw['avg'], lw['w_branch'], lw['w_out']]
    specs = [
        row((tm, d)), const(mod), pl.BlockSpec((1, d), lambda i: (0, 0)),
        row((tm, d), 0), row((tm, d), 1), row((tm, d), 2), row((tm, d), 3),
        row((tm, BRANCH_W)), row((tm, BRANCH_W)),
        row((tm, BRANCH_W)), row((tm, BRANCH_W)),
        row((tm, BRANCH_W), COL_RG), row((tm, BRANCH_W)),
        const(lw['w_glu']), const(lw['b_glu']), const(lw['ret_gn']), const(lw['avg']),
        const(lw['w_branch']), const(lw['w_out']),
    ]
    return pl.pallas_call(
        kern,
        grid=(nt,),
        in_specs=specs,
        out_specs=row((tm, d)),
        out_shape=jax.ShapeDtypeStruct((rows, d), F32),
        compiler_params=_cparams("arbitrary"),
        name="merge_out",
    )(*ins)


def _ffn_kernel(x_ref, mod_ref, g2_ref, g3_ref, wg_ref, wu_ref, wd_ref, o_ref, h_scr, acc_scr,
                *, tiles_per_mod, mod_base, n_f):
    i = pl.program_id(0)
    f = pl.program_id(1)

    @pl.when(f == 0)
    def _():
        sh, sc, _ = _mod_rows(mod_ref, i, tiles_per_mod, mod_base, 3)
        h_scr[...] = (_rms(x_ref[...], g2_ref[...]) * (1.0 + sc) + sh).astype(BF16)
        acc_scr[...] = jnp.zeros_like(acc_scr)

    h = h_scr[...]
    act = (_silu(_dot(h, wg_ref[...])) * _dot(h, wu_ref[...])).astype(BF16)
    acc_scr[...] += _dot(act, wd_ref[...])

    @pl.when(f == n_f - 1)
    def _():
        _, _, gate_f = _mod_rows(mod_ref, i, tiles_per_mod, mod_base, 3)
        o_ref[...] = x_ref[...] + gate_f * _rms(acc_scr[...], g3_ref[...])


def _ffn_dense(x, mod, g2, g3, wg, wu, wd, *, rows_per_mod, mod_base):
    rows, d = x.shape
    d_ff = wg.shape[1]
    tm = min(512, rows)
    tf = d_ff // 2 if (d_ff // 2) % 128 == 0 else d_ff
    n_f = d_ff // tf
    kern = functools.partial(_ffn_kernel, tiles_per_mod=max(rows_per_mod // tm, 1), mod_base=mod_base, n_f=n_f)
    return pl.pallas_call(
        kern,
        grid=(rows // tm, n_f),
        in_specs=[
            pl.BlockSpec((tm, d), lambda i, f: (i, 0)),
            pl.BlockSpec(mod.shape, lambda i, f: (0, 0)),
            pl.BlockSpec((1, d), lambda i, f: (0, 0)),
            pl.BlockSpec((1, d), lambda i, f: (0, 0)),
            pl.BlockSpec((d, tf), lambda i, f: (0, f)),
            pl.BlockSpec((d, tf), lambda i, f: (0, f)),
            pl.BlockSpec((tf, d), lambda i, f: (f, 0)),
        ],
        out_specs=pl.BlockSpec((tm, d), lambda i, f: (i, 0)),
        out_shape=jax.ShapeDtypeStruct((rows, d), F32),
        scratch_shapes=[pltpu.VMEM((tm, d), BF16), pltpu.VMEM((tm, d), F32)],
        compiler_params=_cparams("arbitrary", "arbitrary"),
        name="ffn_dense",
    )(x, mod, g2.reshape(1, d), g3.reshape(1, d), wg, wu, wd)


def _router_kernel(x_ref, mod_ref, g2_ref, wr_ref, br_ref, h_ref, comb_ref, *, tiles_per_mod, mod_base):
    i = pl.program_id(0)
    sh, sc, _ = _mod_rows(mod_ref, i, tiles_per_mod, mod_base, 3)
    h = _rms(x_ref[...], g2_ref[...]) * (1.0 + sc) + sh
    h_ref[...] = _pack_pairs(h)
    h_hi, h_lo = _split_bf16(h)
    w_hi, w_lo = _split_bf16(wr_ref[...])
    logits = _dot(h_hi, w_hi) + _dot(h_lo, w_hi) + _dot(h_hi, w_lo) + br_ref[...]
    lane = lax.broadcasted_iota(jnp.int32, logits.shape, 1)
    v1 = jnp.max(logits, axis=-1, keepdims=True)
    i1 = jnp.min(jnp.where(logits == v1, lane, 128), axis=-1, keepdims=True)
    rest = jnp.where(lane == i1, NEG_BIG, logits)
    v2 = jnp.max(rest, axis=-1, keepdims=True)
    i2 = jnp.min(jnp.where(rest == v2, lane, 128), axis=-1, keepdims=True)
    e = jnp.exp(v2 - v1)
    w1 = 1.0 / (1.0 + e)
    w2 = e / (1.0 + e)
    meta = jnp.where(lane == 0, i1.astype(F32), 0.0) + jnp.where(lane == 1, i2.astype(F32), 0.0)
    comb_ref[...] = meta + jnp.where(lane == 2, w1, 0.0) + jnp.where(lane == 3, w2, 0.0)


def _router(x, mod, g2, w_router, b_router, *, rows_per_mod, mod_base):
    rows, d = x.shape
    tm = min(512, rows)
    wr = jnp.zeros((d, 128), F32).at[:, :N_EXPERTS].set(w_router)
    br = jnp.full((1, 128), NEG_BIG, F32).at[0, :N_EXPERTS].set(b_router)
    kern = functools.partial(_router_kernel, tiles_per_mod=max(rows_per_mod // tm, 1), mod_base=mod_base)
    return pl.pallas_call(
        kern,
        grid=(rows // tm,),
        in_specs=[
            pl.BlockSpec((tm, d), lambda i: (i, 0)),
            pl.BlockSpec(mod.shape, lambda i: (0, 0)),
            pl.BlockSpec((1, d), lambda i: (0, 0)),
            pl.BlockSpec((d, 128), lambda i: (0, 0)),
            pl.BlockSpec((1, 128), lambda i: (0, 0)),
        ],
        out_specs=[pl.BlockSpec((tm, d // 2), lambda i: (i, 0)), pl.BlockSpec((tm, 128), lambda i: (i, 0))],
        out_shape=[jax.ShapeDtypeStruct((rows, d // 2), jnp.int32), jax.ShapeDtypeStruct((rows, 128), F32)],
        compiler_params=_cparams("arbitrary"),
        name="moe_router",
    )(x, mod, g2.reshape(1, d), wr, br)


def _sc_gather(table, idx):
    n_idx = idx.shape[0]
    width = table.shape[1]
    per_worker = n_idx // SC_WORKERS
    chunk_rows = math.gcd(per_worker, SC_GATHER_ROWS)
    n_chunks = per_worker // chunk_rows
    assert per_worker * SC_WORKERS == n_idx and chunk_rows % 8 == 0
    mesh = plsc.VectorSubcoreMesh(core_axis_name="c", subcore_axis_name="s")

    assert n_chunks % 2 == 0
    buf = [pltpu.VMEM((chunk_rows,), jnp.int32), pltpu.VMEM((chunk_rows, width), table.dtype),
           pltpu.SemaphoreType.DMA, pltpu.SemaphoreType.DMA]

    @functools.partial(
        pl.kernel, mesh=mesh,
        out_type=jax.ShapeDtypeStruct((n_idx, width), table.dtype),
        scratch_types=buf + buf,
        name="sc_row_gather",
    )
    def gather(table_hbm, idx_hbm, out_hbm, idx0, rows0, g0, w0, idx1, rows1, g1, w1):
        wid = lax.axis_index("s") * SC_CORES + lax.axis_index("c")
        base = wid * per_worker
        slots = ((idx0, rows0, g0, w0), (idx1, rows1, g1, w1))

        def fetch(j, slot):
            idx_v, rows_v, g, _ = slots[slot]
            pltpu.sync_copy(idx_hbm.at[pl.ds(base + j * chunk_rows, chunk_rows)], idx_v)
            pltpu.make_async_copy(table_hbm.at[idx_v], rows_v, g).start()

        def store(j, slot):
            idx_v, rows_v, g, w = slots[slot]
            pltpu.make_async_copy(table_hbm.at[idx_v], rows_v, g).wait()
            pltpu.make_async_copy(rows_v, out_hbm.at[pl.ds(base + j * chunk_rows, chunk_rows)], w).start()

        def drain(j, slot):
            _, rows_v, _, w = slots[slot]
            pltpu.make_async_copy(rows_v, out_hbm.at[pl.ds(base + j * chunk_rows, chunk_rows)], w).wait()

        fetch(0, 0)

        @pl.loop(0, n_chunks // 2)
        def _(jj):
            j = 2 * jj

            @pl.when(jj > 0)
            def _():
                drain(j - 1, 1)

            fetch(j + 1, 1)
            store(j, 0)

            @pl.when(j + 2 < n_chunks)
            def _():
                drain(j, 0)
                fetch(j + 2, 0)

            store(j + 1, 1)

        drain(n_chunks - 2, 0)
        drain(n_chunks - 1, 1)

    return gather(table, idx)


def _sc_scatter(table, idx, n_out):
    n_idx = idx.shape[0]
    rows, width = table.shape
    per_worker = n_idx // SC_WORKERS
    chunk_rows = math.gcd(per_worker, SC_GATHER_ROWS)
    n_chunks = per_worker // chunk_rows
    assert per_worker * SC_WORKERS == n_idx and chunk_rows % 8 == 0 and rows % per_worker == 0
    mesh = plsc.VectorSubcoreMesh(core_axis_name="c", subcore_axis_name="s")

    assert n_chunks % 2 == 0
    buf = [pltpu.VMEM((chunk_rows,), jnp.int32), pltpu.VMEM((chunk_rows, width), table.dtype),
           pltpu.SemaphoreType.DMA, pltpu.SemaphoreType.DMA]

    @functools.partial(
        pl.kernel, mesh=mesh,
        out_type=jax.ShapeDtypeStruct((n_out, width), table.dtype),
        scratch_types=buf + buf,
        name="sc_row_scatter",
    )
    def scatter(table_hbm, idx_hbm, out_hbm, idx0, rows0, l0, w0, idx1, rows1, l1, w1):
        wid = lax.axis_index("s") * SC_CORES + lax.axis_index("c")
        base = wid * per_worker
        slots = ((idx0, rows0, l0, w0), (idx1, rows1, l1, w1))

        def src(j):
            return table_hbm.at[pl.ds(lax.rem(base + j * chunk_rows, rows), chunk_rows)]

        def fetch(j, slot):
            idx_v, rows_v, l, _ = slots[slot]
            pltpu.sync_copy(idx_hbm.at[pl.ds(base + j * chunk_rows, chunk_rows)], idx_v)
            pltpu.make_async_copy(src(j), rows_v, l).start()

        def store(j, slot):
            idx_v, rows_v, l, w = slots[slot]
            pltpu.make_async_copy(src(j), rows_v, l).wait()
            pltpu.make_async_copy(rows_v, out_hbm.at[idx_v], w).start()

        def drain(slot):
            idx_v, rows_v, _, w = slots[slot]
            pltpu.make_async_copy(rows_v, out_hbm.at[idx_v], w).wait()

        fetch(0, 0)

        @pl.loop(0, n_chunks // 2)
        def _(jj):
            j = 2 * jj

            @pl.when(jj > 0)
            def _():
                drain(1)

            fetch(j + 1, 1)
            store(j, 0)

            @pl.when(j + 2 < n_chunks)
            def _():
                drain(0)
                fetch(j + 2, 0)

            store(j + 1, 1)

        drain(0)
        drain(1)

    return scatter(table, idx)


def _moe_plan(meta, rows):
    tile = MOE_ROW_TILE
    n_tiles = (2 * rows) // tile + N_EXPERTS
    n_slots = n_tiles * tile
    experts = jnp.concatenate([meta[:, 0], meta[:, 1]]).astype(jnp.int32)
    onehot = (experts[:, None] == jnp.arange(N_EXPERTS)[None, :]).astype(jnp.int32)
    csum = jnp.cumsum(onehot, axis=0)
    counts = csum[-1]
    rank = jnp.sum(onehot * csum, axis=1) - 1
    padded = ((counts + tile - 1) // tile) * tile
    ends = jnp.cumsum(padded)
    starts = ends - padded
    pos = jnp.sum(onehot * starts[None, :], axis=1) + rank
    tile_start = jnp.arange(n_tiles, dtype=jnp.int32) * tile
    used = tile_start < ends[-1]
    tile_e = jnp.minimum(jnp.sum((tile_start[:, None] >= ends[None, :]).astype(jnp.int32), axis=1), N_EXPERTS - 1)
    last_e = jnp.max(jnp.where(used, tile_e, 0))
    tile_e = jnp.where(used, tile_e, last_e)
    valid_end = jnp.sum((tile_e[:, None] == jnp.arange(N_EXPERTS)[None, :]) * (starts + counts)[None, :], axis=1)
    n_valid = jnp.where(used, jnp.clip(valid_end - tile_start, 0, tile), 0).astype(jnp.int32)
    return pos.astype(jnp.int32), n_slots, tile_e.astype(jnp.int32), n_valid


def _moe_group_kernel(eid_ref, nval_ref, hs_ref, wg_ref, wu_ref, wd_ref, y_ref, h_scr, acc_scr, *, n_f):
    w = pl.program_id(0)
    f = pl.program_id(1)
    nv = nval_ref[w]

    def run(n_rows):
        rows = slice(0, n_rows)

        @pl.when(f == 0)
        def _():
            hv = _unpack_pairs(hs_ref[rows, :])
            row = lax.broadcasted_iota(jnp.int32, hv.shape, 0)
            h_scr[rows, :] = jnp.where(row < nv, hv, 0.0).astype(BF16)
            acc_scr[rows, :] = jnp.zeros((n_rows, acc_scr.shape[1]), F32)

        h = h_scr[rows, :]
        gate = _dot(h, wg_ref[...].astype(BF16))
        up = _dot(h, wu_ref[...].astype(BF16))
        acc_scr[rows, :] += _dot((_silu(gate) * up).astype(BF16), wd_ref[...].astype(BF16))

        @pl.when(f == n_f - 1)
        def _():
            y_ref[rows, :] = _pack_pairs(acc_scr[rows, :])

    half = hs_ref.shape[0] // 2

    @pl.when(nv > half)
    def _():
        run(hs_ref.shape[0])

    @pl.when((nv > 0) & (nv <= half))
    def _():
        run(half)


def _moe_grouped(hs, tile_e, n_valid, wg, wu, wd):
    n_slots = hs.shape[0]
    d = wg.shape[1]
    d_ff = wg.shape[2]
    tile = MOE_ROW_TILE
    tf = MOE_FF_TILE
    n_f = d_ff // tf

    def f_idx(f, nval, w):
        return jnp.where(nval[w] > 0, f, n_f - 1)

    grid_spec = pltpu.PrefetchScalarGridSpec(
        num_scalar_prefetch=2,
        grid=(n_slots // tile, n_f),
        in_specs=[
            pl.BlockSpec((tile, d // 2), lambda w, f, eid, nval: (w, 0)),
            pl.BlockSpec((None, d, tf), lambda w, f, eid, nval: (eid[w], 0, f_idx(f, nval, w))),
            pl.BlockSpec((None, d, tf), lambda w, f, eid, nval: (eid[w], 0, f_idx(f, nval, w))),
            pl.BlockSpec((None, tf, d), lambda w, f, eid, nval: (eid[w], f_idx(f, nval, w), 0)),
        ],
        out_specs=pl.BlockSpec((tile, d // 2), lambda w, f, eid, nval: (w, 0)),
        scratch_shapes=[pltpu.VMEM((tile, d), BF16), pltpu.VMEM((tile, d), F32)],
    )
    return pl.pallas_call(
        functools.partial(_moe_group_kernel, n_f=n_f),
        grid_spec=grid_spec,
        out_shape=jax.ShapeDtypeStruct((n_slots, d // 2), jnp.int32),
        compiler_params=_cparams("arbitrary", "arbitrary"),
        name="moe_experts",
    )(tile_e, n_valid, hs, wg, wu, wd)


def _moe_out_kernel(x_ref, y1_ref, y2_ref, meta_ref, mod_ref, g3_ref, o_ref, *, tiles_per_mod, mod_base):
    i = pl.program_id(0)
    _, _, gate_f = _mod_rows(mod_ref, i, tiles_per_mod, mod_base, 3)
    meta = meta_ref[...]
    y = meta[:, 2:3] * _unpack_pairs(y1_ref[...]) + meta[:, 3:4] * _unpack_pairs(y2_ref[...])
    o_ref[...] = x_ref[...] + gate_f * _rms(y, g3_ref[...])


def _moe_combine(x, yg, meta, mod, g3, *, rows_per_mod, mod_base):
    rows, d = x.shape
    tm = min(512, rows)
    nt = rows // tm
    kern = functools.partial(_moe_out_kernel, tiles_per_mod=max(rows_per_mod // tm, 1), mod_base=mod_base)
    return pl.pallas_call(
        kern,
        grid=(nt,),
        in_specs=[
            pl.BlockSpec((tm, d), lambda i: (i, 0)),
            pl.BlockSpec((tm, d // 2), lambda i: (i, 0)),
            pl.BlockSpec((tm, d // 2), lambda i: (nt + i, 0)),
            pl.BlockSpec((tm, 128), lambda i: (i, 0)),
            pl.BlockSpec(mod.shape, lambda i: (0, 0)),
            pl.BlockSpec((1, d), lambda i: (0, 0)),
        ],
        out_specs=pl.BlockSpec((tm, d), lambda i: (i, 0)),
        out_shape=jax.ShapeDtypeStruct((rows, d), F32),
        compiler_params=_cparams("arbitrary"),
        name="moe_combine",
    )(x, yg, yg, meta, mod, g3.reshape(1, d))


def _moe_sparse(x, h, meta, mod, g3, wg, wu, wd, *, rows_per_mod, mod_base):
    rows = x.shape[0]
    pos, n_slots, tile_e, n_valid = _moe_plan(meta, rows)
    hs = _sc_scatter(h, pos, n_slots)
    ys = _moe_grouped(hs, tile_e, n_valid, wg, wu, wd)
    yg = _sc_gather(ys, pos)
    return _moe_combine(x, yg, meta, mod, g3, rows_per_mod=rows_per_mod, mod_base=mod_base)


def _cast_kernel(w_ref, o_ref, *, scale):
    w = w_ref[...]
    o_ref[...] = (w if scale == 1.0 else w * scale).astype(BF16)


def _cast_bf16(w_stack, layer, scale=1.0):
    squeeze = w_stack.ndim == 3
    w4 = w_stack[:, None] if squeeze else w_stack
    _, n_e, k, n = w4.shape
    bk = min(k, 256)
    out = pl.pallas_call(
        functools.partial(_cast_kernel, scale=scale),
        grid=(n_e, k // bk),
        in_specs=[pl.BlockSpec((None, None, bk, n), lambda e, i: (layer, e, i, 0))],
        out_specs=pl.BlockSpec((None, bk, n), lambda e, i: (e, i, 0)),
        out_shape=jax.ShapeDtypeStruct((n_e, k, n), BF16),
        compiler_params=_cparams("arbitrary", "arbitrary"),
        name="cast_weights",
    )(w4)
    return out[0] if squeeze else out


def _permute_w_in(w_in_stack, layer):
    _, k, n = w_in_stack.shape
    n_blocks = n // BRANCH_W
    shift = 9
    n_gate_blocks = N_BRANCH * D_MODEL // BRANCH_W

    def permute_kernel(w_ref, o_ref):
        scale = jnp.where(pl.program_id(0) < n_gate_blocks, 0.5, 1.0)
        o_ref[...] = (w_ref[...] * scale).astype(BF16)

    return pl.pallas_call(
        permute_kernel,
        grid=(n_blocks,),
        in_specs=[pl.BlockSpec((None, k, BRANCH_W), lambda j: (layer, 0, (j + shift) % n_blocks))],
        out_specs=pl.BlockSpec((k, BRANCH_W), lambda j: (0, j)),
        out_shape=jax.ShapeDtypeStruct((k, n), BF16),
        compiler_params=_cparams("arbitrary"),
        name="cast_permute_w_in",
    )(w_in_stack)


def kernel(x, c, ctx, c_ctx, w_mod, b_mod, norm_g, w_in, s5_a_re, s5_a_im, s5_log_dt, s5_b_re, s5_b_im, s5_c_re, s5_c_im, s5_d, s5_w_glu, s5_b_glu, ret_decay, ret_gn, na_rpb, w_branch, w_out, ffn_w_gate, ffn_w_up, ffn_w_down, moe_w_router, moe_b_router, moe_w_gate, moe_w_up, moe_w_down):
    batch, seq_len, d = x.shape
    ctx_len = ctx.shape[1]
    depth = w_mod.shape[0]
    cond = jnp.concatenate([c, c_ctx[None, :]], axis=0)
    mod_all = _modulation(cond, w_mod, b_mod)
    rope = _rope_tables(seq_len)
    lane_h = np.repeat(np.arange(RET_HEADS), RET_DIM)
    avg = jnp.asarray((lane_h[:, None] == lane_h[None, :]).astype(np.float32) / RET_DIM, BF16)

    xl = x.reshape(batch * seq_len, d)
    xc = ctx.reshape(batch * ctx_len, d)
    lat = dict(rows_per_mod=seq_len, mod_base=0)
    cxt = dict(rows_per_mod=batch * ctx_len, mod_base=batch)

    for layer in range(depth):
        last = layer == depth - 1
        need_ctx = not last
        mod = mod_all[layer]
        ng = norm_g[layer]
        w_in_bf = _permute_w_in(w_in, layer)
        s5_tabs = _s5_tables(s5_a_re[layer], s5_a_im[layer], s5_log_dt[layer], s5_b_re[layer], s5_b_im[layer],
                             s5_c_re[layer], s5_c_im[layer], s5_d[layer], batch)
        ret_tabs = _ret_tables(ret_decay[layer])
        na_bias, na_hmask = _na_tables(na_rpb[layer])
        lw = dict(w_glu=s5_w_glu[layer].astype(BF16), b_glu=s5_b_glu[layer].reshape(1, BRANCH_W).astype(F32),
                  ret_gn=ret_gn[layer].reshape(1, BRANCH_W).astype(F32), avg=avg,
                  w_branch=_cast_bf16(w_branch, layer, 0.5), w_out=_cast_bf16(w_out, layer))

        proj_l, f_l = _in_proj(xl, mod, ng[0], w_in_bf, **lat)
        proj_c, f_c = _in_proj(xc, mod, ng[0], w_in_bf, **cxt)

        a_l = _fourier_latent(f_l, batch, seq_len)
        s_l, s_c = _s5_mixer(proj_l[:, COL_S * BRANCH_W:(COL_S + 1) * BRANCH_W],
                             proj_c[:, COL_S * BRANCH_W:(COL_S + 1) * BRANCH_W], s5_tabs, batch)
        r_l, r_c = _retention(proj_l, proj_c, ret_tabs, rope, batch, seq_len, ctx_len)
        n_l, n_c = _neighborhood(proj_l, proj_c, na_bias, na_hmask, batch, seq_len, ctx_len, need_ctx)

        xl = _merge(xl, mod, ng[1], proj_l, a_l, s_l, r_l, n_l, lw, **lat)
        if need_ctx:
            a_c = _fourier_ctx(f_c, batch, ctx_len)
            xc = _merge(xc, mod, ng[1], proj_c, a_c, s_c, r_c, n_c, lw, **cxt)

        i = layer // 2
        if layer % 2 == 0:
            wg, wu, wd = _cast_bf16(ffn_w_gate, i), _cast_bf16(ffn_w_up, i), _cast_bf16(ffn_w_down, i)
            xl = _ffn_dense(xl, mod, ng[2], ng[3], wg, wu, wd, **lat)
            if need_ctx:
                xc = _ffn_dense(xc, mod, ng[2], ng[3], wg, wu, wd, **cxt)
        else:
            wg, wu, wd = moe_w_gate[i], moe_w_up[i], moe_w_down[i]
            h, meta = _router(xl, mod, ng[2], moe_w_router[i], moe_b_router[i], **lat)
            xl = _moe_sparse(xl, h, meta, mod, ng[3], wg, wu, wd, **lat)
            if need_ctx:
                hc, metac = _router(xc, mod, ng[2], moe_w_router[i], moe_b_router[i], **cxt)
                xc = _moe_sparse(xc, hc, metac, mod, ng[3], wg, wu, wd, **cxt)
    return xl.reshape(batch, seq_len, d)
```

```python
import functools
import math

import numpy as np
import jax
import jax.numpy as jnp
from jax import lax
from jax.experimental import pallas as pl
from jax.experimental.pallas import tpu as pltpu
from jax.experimental.pallas import tpu_sc as plsc

F32 = jnp.float32
BF16 = jnp.bfloat16

D_MODEL = 1024
BRANCH_W = 256
N_BRANCH = 4
GRID_W = 64
FNET_GROUP_DIM = 64
S5_GROUP_CH = 16
S5_GROUPS = 16
S5_STATE = 64
S5_CHUNK = 32
S5_PAIRS = S5_GROUPS // 2
RET_HEADS = 4
RET_DIM = 64
RET_CHUNK = 128
NA_HEADS = 4
NA_DIM = 64
NA_WIN_ROWS = 8
NA_WIN_COLS = 16
NA_QROWS = 8
ROPE_BASE = 10000.0
N_EXPERTS = 8
EPS = 1e-6
FFT_N2 = 256
NEG_BIG = -1e30
VMEM_LIMIT_BYTES = 50 * 1024 * 1024
SC_CORES = 2
SC_SUBCORES = 16
SC_WORKERS = SC_CORES * SC_SUBCORES
SC_GATHER_ROWS = 64
MOE_ROW_TILE = 1024
MOE_FF_TILE = 512

COL_F, COL_S, COL_RQ, COL_RK, COL_RV, COL_RG, COL_NQ, COL_NK, COL_NV = range(16, 25)
IN_W = 9 * BRANCH_W + N_BRANCH * D_MODEL
IN_TN = 1280
IN_F_TILE = (N_BRANCH * D_MODEL) // IN_TN
IN_F_OFF = N_BRANCH * D_MODEL - IN_F_TILE * IN_TN
IN_S_OFF = IN_F_OFF + BRANCH_W


def _cparams(*sem):
    return pltpu.CompilerParams(dimension_semantics=sem, vmem_limit_bytes=VMEM_LIMIT_BYTES)


def _sigmoid(v):
    return 0.5 * jnp.tanh(0.5 * v) + 0.5


def _silu(v):
    return v * _sigmoid(v)


def _gelu_tanh(v):
    return 0.5 * v * (1.0 + jnp.tanh(math.sqrt(2.0 / math.pi) * (v + 0.044715 * (v * v * v))))


def _rms(v, g):
    ms = jnp.mean(v * v, axis=-1, keepdims=True)
    return v * lax.rsqrt(ms + EPS) * g


def _split_bf16(v):
    hi = v.astype(BF16)
    lo = (v - hi.astype(F32)).astype(BF16)
    return hi, lo


def _pack_pairs(v):
    n = v.shape[1] // 2
    lo = lax.bitcast_convert_type(v[:, :n].astype(BF16).astype(F32), jnp.int32)
    hi = lax.bitcast_convert_type(v[:, n:].astype(BF16).astype(F32), jnp.int32)
    return (hi & -65536) | ((lo >> 16) & 65535)


def _unpack_pairs(w):
    lo = lax.bitcast_convert_type(w << 16, F32)
    hi = lax.bitcast_convert_type(w & -65536, F32)
    return jnp.concatenate([lo, hi], axis=-1)


def _dot(a, b):
    return jnp.dot(a, b, preferred_element_type=F32)


def _dot_nt(a, b):
    return lax.dot_general(a, b, (((1,), (1,)), ((), ())), preferred_element_type=F32)


def _dot_tn(a, b):
    return lax.dot_general(a, b, (((0,), (0,)), ((), ())), preferred_element_type=F32)


def _mod_kernel(ct_ref, w_ref, b_ref, o_ref, *, n_cond):
    ct = ct_ref[...]
    s = _silu(ct)
    w = w_ref[...]
    rows = [jnp.sum(w * s[:, r:r + 1], axis=0, keepdims=True) for r in range(n_cond)]
    rows.append(jnp.zeros((8 - n_cond, w.shape[1]), F32))
    o_ref[...] = jnp.concatenate(rows, axis=0) + b_ref[...]


def _modulation(cond, w_mod, b_mod):
    n_layers, d, n = w_mod.shape
    tn = 512
    ct = jnp.zeros((8, d), F32).at[:cond.shape[0]].set(cond).T
    return pl.pallas_call(
        functools.partial(_mod_kernel, n_cond=cond.shape[0]),
        grid=(n_layers, n // tn),
        in_specs=[
            pl.BlockSpec((d, 8), lambda l, j: (0, 0)),
            pl.BlockSpec((None, d, tn), lambda l, j: (l, 0, j)),
            pl.BlockSpec((None, 1, tn), lambda l, j: (l, 0, j)),
        ],
        out_specs=pl.BlockSpec((None, 8, tn), lambda l, j: (l, 0, j)),
        out_shape=jax.ShapeDtypeStruct((n_layers, 8, n), F32),
        compiler_params=_cparams("arbitrary", "arbitrary"),
        name="adaln_mod",
    )(ct, w_mod, b_mod.reshape(n_layers, 1, n))


def _mod_rows(mod_ref, i, tiles_per_mod, mod_base, first):
    r = mod_base + i // tiles_per_mod
    return [mod_ref[pl.ds(r, 1), (first + k) * D_MODEL:(first + k + 1) * D_MODEL] for k in range(3)]


def _in_kernel(x_ref, mod_ref, g_ref, w_ref, proj_ref, f_ref, sa_ref, sb_ref, h_scr, *, tiles_per_mod, mod_base):
    i = pl.program_id(0)
    j = pl.program_id(1)

    @pl.when(j == 0)
    def _():
        sh, sc, _ = _mod_rows(mod_ref, i, tiles_per_mod, mod_base, 0)
        h_scr[...] = (_rms(x_ref[...], g_ref[...]) * (1.0 + sc) + sh).astype(BF16)

    res = _dot(h_scr[...], w_ref[...])
    proj_ref[...] = res.astype(BF16)

    @pl.when(j == IN_F_TILE)
    def _():
        f_ref[...] = res[:, IN_F_OFF:IN_F_OFF + BRANCH_W].astype(BF16)
        sa_ref[...] = res[:, IN_S_OFF:IN_S_OFF + 128]
        sb_ref[...] = res[:, IN_S_OFF + 128:IN_S_OFF + 256]


def _in_proj(x, mod, g, w_bf, *, rows_per_mod, mod_base):
    rows, d = x.shape
    tm = math.gcd(1024, rows_per_mod)
    kern = functools.partial(_in_kernel, tiles_per_mod=max(rows_per_mod // tm, 1), mod_base=mod_base)
    return pl.pallas_call(
        kern,
        grid=(rows // tm, IN_W // IN_TN),
        in_specs=[
            pl.BlockSpec((tm, d), lambda i, j: (i, 0)),
            pl.BlockSpec(mod.shape, lambda i, j: (0, 0)),
            pl.BlockSpec((1, d), lambda i, j: (0, 0)),
            pl.BlockSpec((d, IN_TN), lambda i, j: (0, j)),
        ],
        out_specs=[
            pl.BlockSpec((tm, IN_TN), lambda i, j: (i, j)),
            pl.BlockSpec((tm, BRANCH_W), lambda i, j: (i, 0)),
            pl.BlockSpec((tm, 128), lambda i, j: (i, 0)),
            pl.BlockSpec((tm, 128), lambda i, j: (i, 0)),
        ],
        out_shape=[
            jax.ShapeDtypeStruct((rows, IN_W), BF16),
            jax.ShapeDtypeStruct((rows, BRANCH_W), BF16),
            jax.ShapeDtypeStruct((rows, 128), F32),
            jax.ShapeDtypeStruct((rows, 128), F32),
        ],
        scratch_shapes=[pltpu.VMEM((tm, d), BF16)],
        compiler_params=_cparams("arbitrary", "arbitrary"),
        name="in_proj",
    )(x, mod, g.reshape(1, d), w_bf)


def _fft_a_kernel(x_ref, cs_ref, tc_ref, ts_ref, zr_ref, zi_ref, *, n1, n1p):
    y = _dot(cs_ref[...].astype(BF16), x_ref[...])
    yr = y[:n1]
    yi = y[n1p:n1p + n1]
    tc = tc_ref[...]
    ts = ts_ref[...]
    zr_ref[...] = (yr * tc + yi * ts).astype(BF16)
    zi_ref[...] = (yi * tc - yr * ts).astype(BF16)


def _fft_b_kernel(zr_ref, zi_ref, cs_ref, cc_ref, sc_ref, o_ref, *, kb, scale, has_imag):
    cs = cs_ref[...].astype(BF16)
    cc = cc_ref[...].astype(BF16)
    sc = sc_ref[...].astype(BF16)
    for kk in range(kb):
        a = _dot(cs, zr_ref[kk])
        if has_imag:
            b = _dot(cs, zi_ref[kk])
            xr = a[:FFT_N2] + b[FFT_N2:]
            xi = b[:FFT_N2] - a[FFT_N2:]
        else:
            xr = a[:FFT_N2]
            xi = -a[FFT_N2:]
        out = _dot(xr.astype(BF16), cc) + _dot(xi.astype(BF16), sc)
        o_ref[:, kk * BRANCH_W:(kk + 1) * BRANCH_W] = (out * scale).astype(BF16)


def _dft_tables(n):
    k = np.arange(n)
    ang = 2.0 * np.pi * ((k[:, None] * k[None, :]) % n) / n
    return np.cos(ang), np.sin(ang)


def _fft_b_call(zr, zi, n1, batch, seq_len, has_imag):
    c2, s2 = _dft_tables(FFT_N2)
    cs2 = jnp.asarray(np.concatenate([c2, s2], axis=0), F32)
    c64, s64 = _dft_tables(FNET_GROUP_DIM)
    eye = np.eye(BRANCH_W // FNET_GROUP_DIM)
    cc = jnp.asarray(np.kron(eye, c64), F32)
    sc = jnp.asarray(np.kron(eye, s64), F32)
    kb = min(8, n1)
    scale = 1.0 / math.sqrt(seq_len * FNET_GROUP_DIM)
    kern = functools.partial(_fft_b_kernel, kb=kb, scale=scale, has_imag=has_imag)
    zspec = pl.BlockSpec((None, kb, FFT_N2, BRANCH_W), lambda b, i: (b, i, 0, 0))
    out = pl.pallas_call(
        kern,
        grid=(batch, n1 // kb),
        in_specs=[
            zspec, zspec,
            pl.BlockSpec((2 * FFT_N2, FFT_N2), lambda b, i: (0, 0)),
            pl.BlockSpec((BRANCH_W, BRANCH_W), lambda b, i: (0, 0)),
            pl.BlockSpec((BRANCH_W, BRANCH_W), lambda b, i: (0, 0)),
        ],
        out_specs=pl.BlockSpec((None, FFT_N2, kb * BRANCH_W), lambda b, i: (b, 0, i)),
        out_shape=jax.ShapeDtypeStruct((batch, FFT_N2, n1 * BRANCH_W), BF16),
        compiler_params=_cparams("arbitrary", "arbitrary"),
        name="fourier_stage_b",
    )(zr, zi, cs2, cc, sc)
    return out.reshape(batch * seq_len, BRANCH_W)


def _fourier_latent(f, batch, seq_len):
    n1 = seq_len // FFT_N2
    wide = FFT_N2 * BRANCH_W
    c1, s1 = _dft_tables(n1)
    n1p = max(n1, 8)
    cs1 = np.zeros((2 * n1p, n1))
    cs1[:n1] = c1
    cs1[n1p:n1p + n1] = -s1
    k1 = np.arange(n1)[:, None]
    l2 = np.arange(FFT_N2)[None, :]
    tw = 2.0 * np.pi * (k1 * l2) / seq_len
    tc = jnp.asarray(np.repeat(np.cos(tw), BRANCH_W, axis=1), F32)
    ts = jnp.asarray(np.repeat(np.sin(tw), BRANCH_W, axis=1), F32)
    cw = min(8192, wide)
    xv = f.reshape(batch, n1, wide)
    spec = pl.BlockSpec((None, n1, cw), lambda b, j: (b, 0, j))
    tspec = pl.BlockSpec((n1, cw), lambda b, j: (0, j))
    zr, zi = pl.pallas_call(
        functools.partial(_fft_a_kernel, n1=n1, n1p=n1p),
        grid=(batch, wide // cw),
        in_specs=[spec, pl.BlockSpec((2 * n1p, n1), lambda b, j: (0, 0)), tspec, tspec],
        out_specs=[spec, spec],
        out_shape=[jax.ShapeDtypeStruct((batch, n1, wide), BF16)] * 2,
        compiler_params=_cparams("arbitrary", "arbitrary"),
        name="fourier_stage_a",
    )(xv, jnp.asarray(cs1, F32), tc, ts)
    zr = zr.reshape(batch, n1, FFT_N2, BRANCH_W)
    zi = zi.reshape(batch, n1, FFT_N2, BRANCH_W)
    return _fft_b_call(zr, zi, n1, batch, seq_len, True)


def _fourier_ctx(f, batch, ctx_len):
    assert ctx_len == FFT_N2
    z = f.reshape(batch, 1, FFT_N2, BRANCH_W)
    return _fft_b_call(z, z, 1, batch, ctx_len, False)


def _s5_tables(a_re, a_im, log_dt, b_re, b_im, c_re, c_im, d_skip, batch):
    t = S5_CHUNK
    g, p, hc = S5_GROUPS, S5_STATE, S5_GROUP_CH
    lam = lax.complex(a_re.astype(F32), a_im.astype(F32))
    dt = jnp.exp(log_dt.astype(F32))[..., None]
    ks = jnp.arange(t + 1, dtype=F32)
    apow = jnp.exp((lam * dt)[..., None] * ks)
    a_bar = apow[..., 1]
    b_bar = ((a_bar - 1.0) / lam)[..., None] * lax.complex(b_re.astype(F32), b_im.astype(F32))
    cm = lax.complex(c_re.astype(F32), c_im.astype(F32))
    kimp = jnp.real(jnp.einsum('dghp,dgpk,dgpj->dgkhj', cm, apow[..., :t], b_bar,
                               precision=lax.Precision.HIGHEST))
    kf, kb = kimp[0], kimp[1]
    kfull = jnp.concatenate([kb[:, :0:-1], kf[:, :1] + kb[:, :1], kf[:, 1:]], axis=1)
    kp = kfull.reshape(S5_PAIRS, 2, 2 * t - 1, hc, hc)
    blk = [kp[:, gi].transpose(0, 3, 1, 2) for gi in range(2)]
    zb = jnp.zeros_like(blk[0])
    strip = jnp.concatenate([jnp.stack([blk[0], zb], axis=3), jnp.stack([zb, blk[1]], axis=3)], axis=1)
    strip = strip.reshape(S5_PAIRS, 2 * hc, (2 * t - 1) * 2 * hc)
    strip = jnp.pad(strip, ((0, 0), (0, 0), (0, 2 * hc)))

    wf = jnp.einsum('gpj,gph->gjhp', apow[0][..., t - 1::-1][..., :t], b_bar[0])
    wb = jnp.einsum('gpj,gph->gjhp', apow[1][..., :t], b_bar[1])
    kinds = [jnp.real(wf), jnp.imag(wf), jnp.real(wb), jnp.imag(wb)]

    def we_pair(kd):
        k5 = kd.reshape(S5_PAIRS, 2, t, hc, p)
        z = jnp.zeros_like(k5[:, 0])
        rows = jnp.stack([jnp.concatenate([k5[:, 0], z], axis=-1), jnp.concatenate([z, k5[:, 1]], axis=-1)], axis=2)
        return rows.reshape(S5_PAIRS, 2 * t * hc, 2 * p)

    we = jnp.concatenate([we_pair(kd) for kd in kinds], axis=-1).astype(BF16)

    vf = jnp.einsum('ghp,gpt->gpth', cm[0], apow[0][..., 1:t + 1])
    vb = jnp.einsum('ghp,gpt->gpth', cm[1], apow[1][..., t:0:-1])
    vkinds = [jnp.real(vf), -jnp.imag(vf), jnp.real(vb), -jnp.imag(vb)]

    def v_pair(kd):
        k5 = kd.reshape(S5_PAIRS, 2, p, t, hc)
        z = jnp.zeros_like(k5[:, 0])
        rows = jnp.concatenate([jnp.stack([k5[:, 0], z], axis=3), jnp.stack([z, k5[:, 1]], axis=3)], axis=1)
        return rows.reshape(S5_PAIRS, 2 * p, 2 * t * hc)

    v1 = jnp.concatenate([v_pair(kd) for kd in vkinds], axis=1)
    v = jnp.concatenate([v1, v1], axis=1).astype(BF16)

    def lanes(z):
        return jnp.tile(z.reshape(1, g * p), (1, batch))

    at = apow[..., t]
    a_tab = jnp.concatenate([lanes(jnp.real(at[0])), lanes(jnp.imag(at[0])),
                             lanes(jnp.real(at[1])), lanes(jnp.imag(at[1]))], axis=0)
    dvec = jnp.tile(d_skip.astype(F32).reshape(S5_PAIRS, 1, 2 * hc), (1, t, 1)).reshape(S5_PAIRS, 1, 2 * t * hc)
    return dict(strip=strip, we=we, v=v, a_tab=a_tab, dvec=dvec)


def _s5_pack_kernel(xa_ref, xb_ref, u_ref, *, n_chunks):
    per_half = S5_PAIRS // 2
    for half, x_ref in enumerate((xa_ref, xb_ref)):
        rows = [x_ref[pl.ds(tau, n_chunks, stride=S5_CHUNK), :] for tau in range(S5_CHUNK)]
        for qq in range(per_half):
            pieces = [r[:, qq * 32:(qq + 1) * 32] for r in rows]
            u_ref[half * per_half + qq] = jnp.concatenate(pieces, axis=-1).astype(BF16)


def _s5_unpack_kernel(y_ref, oa_ref, ob_ref, *, n_chunks):
    per_half = S5_PAIRS // 2
    for half, o_ref in enumerate((oa_ref, ob_ref)):
        ys = [y_ref[half * per_half + qq].astype(F32) for qq in range(per_half)]
        for t in range(S5_CHUNK):
            pieces = [y[:, t * 32:(t + 1) * 32] for y in ys]
            o_ref[pl.ds(t, n_chunks, stride=S5_CHUNK), :] = jnp.concatenate(pieces, axis=-1)


def _s5_pack(sa, sb, batch):
    n_chunks = sa.shape[0] // batch // S5_CHUNK
    rows = n_chunks * S5_CHUNK
    cols = 2 * S5_CHUNK * S5_GROUP_CH
    half = pl.BlockSpec((rows, 128), lambda b: (b, 0))
    return pl.pallas_call(
        functools.partial(_s5_pack_kernel, n_chunks=n_chunks),
        grid=(batch,),
        in_specs=[half, half],
        out_specs=pl.BlockSpec((S5_PAIRS, None, n_chunks, cols), lambda b: (0, b, 0, 0)),
        out_shape=jax.ShapeDtypeStruct((S5_PAIRS, batch, n_chunks, cols), BF16),
        compiler_params=_cparams("arbitrary"),
        name="s5_pack",
    )(sa, sb)


def _s5_unpack(y, batch):
    n_chunks = y.shape[2]
    rows = n_chunks * S5_CHUNK
    cols = y.shape[3]
    half = pl.BlockSpec((rows, 128), lambda b: (b, 0))
    return pl.pallas_call(
        functools.partial(_s5_unpack_kernel, n_chunks=n_chunks),
        grid=(batch,),
        in_specs=[pl.BlockSpec((S5_PAIRS, None, n_chunks, cols), lambda b: (0, b, 0, 0))],
        out_specs=[half, half],
        out_shape=[jax.ShapeDtypeStruct((batch * rows, 128), F32)] * 2,
        compiler_params=_cparams("arbitrary"),
        name="s5_unpack",
    )(y)


def _s5_e_kernel(ul_ref, uc_ref, we_ref, ref_, imf_, reb_, imb_):
    u = jnp.concatenate([ul_ref[...], uc_ref[...]], axis=0)
    e = _dot(u, we_ref[...])
    ref_[...] = e[:, 0:128]
    imf_[...] = e[:, 128:256]
    reb_[...] = e[:, 256:384]
    imb_[...] = e[:, 384:512]


def _s5_scan_kernel(a_ref, ref_, imf_, reb_, imb_, prf, pif, prb, pib, *, n_rows, n_ctx):
    afr = a_ref[0:1, :]
    afi = a_ref[1:2, :]
    abr = a_ref[2:3, :]
    abi = a_ref[3:4, :]
    zero = jnp.zeros_like(afr)

    n_lat = n_rows - n_ctx

    def body(s, carry):
        sfr, sfi, sbr, sbi = carry
        nf = jnp.where(s < n_ctx, n_lat + s, s - n_ctx)
        nb = n_rows - 1 - s
        prf[pl.ds(nf, 1), :] = sfr
        pif[pl.ds(nf, 1), :] = sfi
        prb[pl.ds(nb, 1), :] = sbr
        pib[pl.ds(nb, 1), :] = sbi
        efr = ref_[pl.ds(nf, 1), :]
        efi = imf_[pl.ds(nf, 1), :]
        ebr = reb_[pl.ds(nb, 1), :]
        ebi = imb_[pl.ds(nb, 1), :]
        nfr = afr * sfr - afi * sfi + efr
        nfi = afr * sfi + afi * sfr + efi
        nbr = abr * sbr - abi * sbi + ebr
        nbi = abr * sbi + abi * sbr + ebi
        return nfr, nfi, nbr, nbi

    lax.fori_loop(0, n_rows, body, (zero, zero, zero, zero))


def _s5_y_kernel(ul_ref, uc_ref, strip_ref, v_ref, d_ref, prf, pif, prb, pib, yl_ref, yc_ref, m_scr):
    width = 2 * S5_GROUP_CH
    cols = S5_CHUNK * width
    n_lat = yl_ref.shape[0]

    @pl.when(pl.program_id(1) == 0)
    def _():
        strip = strip_ref[...]
        for j in range(S5_CHUNK):
            off = (S5_CHUNK - 1 - j) * width
            win = strip if off == 0 else pltpu.roll(strip, 2 * cols - off, axis=1)
            m_scr[j * width:(j + 1) * width, :] = win[:, :cols].astype(BF16)

    u = jnp.concatenate([ul_ref[...], uc_ref[...]], axis=0)
    y_intra = _dot(u, m_scr[...])
    pcat = jnp.concatenate([prf[...], pif[...], prb[...], pib[...]], axis=-1)
    hi, lo = _split_bf16(pcat)
    y_cross = _dot(jnp.concatenate([hi, lo], axis=-1), v_ref[...])
    y = y_intra + y_cross + d_ref[...] * u.astype(F32)
    yl_ref[...] = y[:n_lat].astype(BF16)
    yc_ref[...] = y[n_lat:].astype(BF16)


def _s5_core(ul, uc, tabs, batch):
    n_lat, n_ctx = ul.shape[2], uc.shape[2]
    n_rows = n_lat + n_ctx
    width = batch * S5_PAIRS * 128
    cols = 2 * S5_CHUNK * S5_GROUP_CH
    ul_spec = pl.BlockSpec((None, None, n_lat, cols), lambda q, b: (q, b, 0, 0))
    uc_spec = pl.BlockSpec((None, None, n_ctx, cols), lambda q, b: (q, b, 0, 0))
    st_spec = pl.BlockSpec((n_rows, 128), lambda q, b: (0, b * S5_PAIRS + q))
    st_shape = jax.ShapeDtypeStruct((n_rows, width), F32)
    e4 = pl.pallas_call(
        _s5_e_kernel,
        grid=(S5_PAIRS, batch),
        in_specs=[ul_spec, uc_spec, pl.BlockSpec((None, cols, 512), lambda q, b: (q, 0, 0))],
        out_specs=[st_spec] * 4,
        out_shape=[st_shape] * 4,
        compiler_params=_cparams("arbitrary", "arbitrary"),
        name="s5_chunk_states",
    )(ul, uc, tabs['we'])
    p4 = pl.pallas_call(
        functools.partial(_s5_scan_kernel, n_rows=n_rows, n_ctx=n_ctx),
        out_shape=[st_shape] * 4,
        compiler_params=pltpu.CompilerParams(vmem_limit_bytes=VMEM_LIMIT_BYTES),
        name="s5_state_scan",
    )(tabs['a_tab'], *e4)
    y = pl.pallas_call(
        _s5_y_kernel,
        grid=(S5_PAIRS, batch),
        in_specs=[
            ul_spec, uc_spec,
            pl.BlockSpec((None, 2 * S5_GROUP_CH, 2 * cols), lambda q, b: (q, 0, 0)),
            pl.BlockSpec((None, cols, cols), lambda q, b: (q, 0, 0)),
            pl.BlockSpec((None, 1, cols), lambda q, b: (q, 0, 0)),
            st_spec, st_spec, st_spec, st_spec,
        ],
        out_specs=[ul_spec, uc_spec],
        out_shape=[
            jax.ShapeDtypeStruct((S5_PAIRS, batch, n_lat, cols), BF16),
            jax.ShapeDtypeStruct((S5_PAIRS, batch, n_ctx, cols), BF16),
        ],
        scratch_shapes=[pltpu.VMEM((cols, cols), BF16)],
        compiler_params=_cparams("arbitrary", "arbitrary"),
        name="s5_outputs",
    )(ul, uc, tabs['strip'], tabs['v'], tabs['dvec'], *p4)
    return y


def _s5_mixer(s_lat, s_ctx, tabs, batch):
    ul = _s5_pack(*s_lat, batch)
    uc = _s5_pack(*s_ctx, batch)
    yl, yc = _s5_core(ul, uc, tabs, batch)
    return _s5_unpack(yl, batch), _s5_unpack(yc, batch)


def _ret_tables(ret_decay):
    c = RET_CHUNK
    lg = jax.nn.log_sigmoid(ret_decay.astype(F32))
    lane_h = jnp.repeat(jnp.arange(RET_HEADS), RET_DIM)
    lgl = lg[:, lane_h]
    pos = jnp.arange(c, dtype=F32)[:, None]
    qd = jnp.stack([jnp.exp((pos + 1.0) * lgl[0][None]), jnp.exp((c - pos) * lgl[1][None])])
    kd = jnp.stack([jnp.exp((c - 1.0 - pos) * lgl[0][None]), jnp.exp(pos * lgl[1][None])])
    bmask = (lane_h[:, None] == lane_h[None, :]).astype(F32)
    cd = jnp.exp(c * lgl)[:, :, None] * bmask[None]
    diff = pos - pos.T
    dm = []
    for h in range(RET_HEADS):
        fw = jnp.where(diff >= 0, jnp.exp(jnp.maximum(diff, 0.0) * lg[0, h]), 0.0)
        bw = jnp.where(diff <= 0, jnp.exp(jnp.maximum(-diff, 0.0) * lg[1, h]), 0.0)
        dm.append(fw + bw)
    dm = jnp.concatenate(dm, axis=0)
    hmask = (jnp.arange(RET_HEADS)[:, None] == lane_h[None, :]).astype(F32)
    return dict(qd=qd, kd=kd, cd=cd, bmask=bmask, dm=dm, hmask=hmask)


def _rope_tables(n_tokens):
    t = np.arange(n_tokens)
    row = (t // GRID_W).astype(np.float64)
    col = (t % GRID_W).astype(np.float64)
    n_freq = RET_DIM // 4
    inv_freq = 1.0 / (ROPE_BASE ** (np.arange(n_freq, dtype=np.float64) / n_freq))
    ang = np.concatenate([row[:, None] * inv_freq, col[:, None] * inv_freq], axis=-1)
    cos = np.cos(ang)
    sin = np.sin(ang)
    cos_t = np.tile(np.concatenate([cos, cos], axis=-1), (1, RET_HEADS))
    sin_t = np.tile(np.concatenate([-sin, sin], axis=-1), (1, RET_HEADS))
    half = RET_DIM // 2
    perm = np.arange(BRANCH_W) ^ half
    swap = np.zeros((BRANCH_W, BRANCH_W), np.float32)
    swap[perm, np.arange(BRANCH_W)] = 1.0
    return jnp.asarray(cos_t, F32), jnp.asarray(sin_t, F32), jnp.asarray(swap, BF16)


def _ret_chunk(q, k, v, s, qd, kd, cd, bmask, dm, hmask, with_intra):
    cross = _dot((q * qd).astype(BF16), s.astype(BF16))
    s_new = cd * s + bmask * _dot_tn((k * kd).astype(BF16), v)
    if not with_intra:
        return cross, s_new
    qb = q.astype(BF16)
    kb = k.astype(BF16)
    qs = jnp.concatenate([qb * hmask[h:h + 1].astype(BF16) for h in range(RET_HEADS)], axis=0)
    scores = _dot_nt(qs, kb) * dm
    ov = _dot(scores.astype(BF16), v)
    c = q.shape[0]
    inner = ov[0:c] * hmask[0:1]
    for h in range(1, RET_HEADS):
        inner = inner + ov[h * c:(h + 1) * c] * hmask[h:h + 1]
    return inner + cross, s_new


def _ret_kernel(qf_ref, kf_ref, vf_ref, qb_ref, kb_ref, vb_ref, qc_ref, kc_ref, vc_ref,
                cosf_ref, sinf_ref, cosb_ref, sinb_ref, swap_ref,
                qd_ref, kd_ref, cd_ref, bm_ref, dm_ref, hm_ref,
                of_ref, ob_ref, ocf_ref, ocb_ref, sf_scr, sb_scr, *, n_chunks, n_ctx_chunks):
    i = pl.program_id(1)
    c = RET_CHUNK
    k_scale = RET_DIM ** -0.5
    bmask = bm_ref[...]
    dm = dm_ref[...]
    hmask = hm_ref[...]
    tabs = [(qd_ref[d], kd_ref[d], cd_ref[d]) for d in range(2)]

    @pl.when(i == 0)
    def _():
        for d, oc_ref, s_scr in ((0, ocf_ref, sf_scr), (1, ocb_ref, sb_scr)):
            qd, kd, cd = tabs[d]
            s = jnp.zeros((BRANCH_W, BRANCH_W), F32)
            order = range(n_ctx_chunks) if d == 0 else range(n_ctx_chunks - 1, -1, -1)
            for cc in order:
                sl = slice(cc * c, (cc + 1) * c)
                o, s = _ret_chunk(qc_ref[sl, :].astype(F32), kc_ref[sl, :].astype(F32) * k_scale, vc_ref[sl, :],
                                  s, qd, kd, cd, bmask, dm, hmask, d == 0)
                oc_ref[sl, :] = o
            s_scr[...] = s

    swap = swap_ref[...]

    def rope(x_ref, cos_ref, sin_ref, sl):
        xb = x_ref[sl, :]
        return xb.astype(F32) * cos_ref[sl, :] + _dot(xb, swap) * sin_ref[sl, :]

    sf = sf_scr[...]
    sb = sb_scr[...]
    for step in range(n_chunks):
        sl = slice(step * c, (step + 1) * c)
        q = rope(qf_ref, cosf_ref, sinf_ref, sl)
        k = rope(kf_ref, cosf_ref, sinf_ref, sl) * k_scale
        o, sf = _ret_chunk(q, k, vf_ref[sl, :], sf, *tabs[0], bmask, dm, hmask, True)
        of_ref[sl, :] = o
        cb = n_chunks - 1 - step
        sl = slice(cb * c, (cb + 1) * c)
        q = rope(qb_ref, cosb_ref, sinb_ref, sl)
        k = rope(kb_ref, cosb_ref, sinb_ref, sl) * k_scale
        o, sb = _ret_chunk(q, k, vb_ref[sl, :], sb, *tabs[1], bmask, dm, hmask, False)
        ob_ref[sl, :] = o
    sf_scr[...] = sf
    sb_scr[...] = sb


def _retention(proj_l, proj_c, tabs, rope, batch, seq_len, ctx_len):
    n_chunks = 4
    blk = n_chunks * RET_CHUNK
    nblk = seq_len // blk
    cos_t, sin_t, swap = rope

    def lat(col, back):
        if back:
            return pl.BlockSpec((blk, BRANCH_W), lambda b, i: (b * nblk + nblk - 1 - i, col))
        return pl.BlockSpec((blk, BRANCH_W), lambda b, i: (b * nblk + i, col))

    def ctx(col):
        return pl.BlockSpec((ctx_len, BRANCH_W), lambda b, i: (b, col))

    def const(shape):
        return pl.BlockSpec(shape, lambda b, i: (0,) * len(shape))

    tab_f = pl.BlockSpec((blk, BRANCH_W), lambda b, i: (i, 0))
    tab_b = pl.BlockSpec((blk, BRANCH_W), lambda b, i: (nblk - 1 - i, 0))
    kern = functools.partial(_ret_kernel, n_chunks=n_chunks, n_ctx_chunks=ctx_len // RET_CHUNK)
    c = RET_CHUNK
    ctx_out = pl.BlockSpec((ctx_len, BRANCH_W), lambda b, i: (b, 0))
    o_f, o_b, oc_f, oc_b = pl.pallas_call(
        kern,
        grid=(batch, nblk),
        in_specs=[
            lat(COL_RQ, False), lat(COL_RK, False), lat(COL_RV, False),
            lat(COL_RQ, True), lat(COL_RK, True), lat(COL_RV, True),
            ctx(COL_RQ), ctx(COL_RK), ctx(COL_RV),
            tab_f, tab_f, tab_b, tab_b, const((BRANCH_W, BRANCH_W)),
            const((2, c, BRANCH_W)), const((2, c, BRANCH_W)), const((2, BRANCH_W, BRANCH_W)),
            const((BRANCH_W, BRANCH_W)), const((RET_HEADS * c, c)), const((RET_HEADS, BRANCH_W)),
        ],
        out_specs=[lat(0, False), lat(0, True), ctx_out, ctx_out],
        out_shape=[
            jax.ShapeDtypeStruct((batch * seq_len, BRANCH_W), F32),
            jax.ShapeDtypeStruct((batch * seq_len, BRANCH_W), F32),
            jax.ShapeDtypeStruct((batch * ctx_len, BRANCH_W), F32),
            jax.ShapeDtypeStruct((batch * ctx_len, BRANCH_W), F32),
        ],
        scratch_shapes=[pltpu.VMEM((BRANCH_W, BRANCH_W), F32), pltpu.VMEM((BRANCH_W, BRANCH_W), F32)],
        compiler_params=_cparams("arbitrary", "arbitrary"),
        name="retention",
    )(proj_l, proj_l, proj_l, proj_l, proj_l, proj_l, proj_c, proj_c, proj_c,
      cos_t, sin_t, cos_t, sin_t, swap,
      tabs['qd'], tabs['kd'], tabs['cd'], tabs['bmask'], tabs['dm'], tabs['hmask'])
    return (o_f, o_b), (oc_f, oc_b)


def _na_tables(rpb):
    kr, kw = NA_WIN_ROWS, NA_WIN_COLS
    col = np.arange(GRID_W)
    col_start = np.clip(col - kw // 2, 0, GRID_W - kw)
    in_win = (col[None, :] >= col_start[:, None]) & (col[None, :] < col_start[:, None] + kw)
    dc = np.clip(col[None, :] - col[:, None], -(kw - 1), kw - 1) + (kw - 1)
    var = np.arange(kr)[:, None] + np.arange(kr)[None, :]
    pick_r = (var[:, :, None] == np.arange(2 * kr - 1)[None, None, :]).astype(np.float32)
    pick_c = (dc[:, :, None] == np.arange(2 * kw - 1)[None, None, :]).astype(np.float32)
    bias = jnp.einsum('vir,hrc,qkc->vhqik', jnp.asarray(pick_r), rpb.astype(F32), jnp.asarray(pick_c),
                      precision=lax.Precision.HIGHEST)
    bias = jnp.where(jnp.asarray(in_win)[None, None, :, None, :], bias, NEG_BIG)
    bias = bias.reshape(kr, NA_HEADS * GRID_W, kr * GRID_W)
    lane_h = np.repeat(np.arange(NA_HEADS), NA_DIM)
    hmask = (np.arange(NA_HEADS)[:, None] == lane_h[None, :]).astype(np.float32)
    return bias, jnp.asarray(hmask, F32)


def _attend(qs, keys, vals, bias, kc, vc):
    s_ctx = _dot_nt(qs, kc)
    m = jnp.max(s_ctx, axis=-1, keepdims=True)
    if keys is not None:
        s_band = _dot_nt(qs, keys) + bias
        m = jnp.maximum(m, jnp.max(s_band, axis=-1, keepdims=True))
        p_band = jnp.exp(s_band - m)
    p_ctx = jnp.exp(s_ctx - m)
    l = jnp.sum(p_ctx, axis=-1, keepdims=True)
    o = _dot(p_ctx.astype(BF16), vc)
    if keys is not None:
        l = l + jnp.sum(p_band, axis=-1, keepdims=True)
        o = o + _dot(p_band.astype(BF16), vals)
    return o / l


def _stack_heads(q, hmask_scaled):
    return jnp.concatenate([q * hmask_scaled[h:h + 1] for h in range(NA_HEADS)], axis=0)


def _unstack_heads(o, hmask, n):
    out = o[0:n] * hmask[0:1]
    for h in range(1, NA_HEADS):
        out = out + o[h * n:(h + 1) * n] * hmask[h:h + 1]
    return out


def _na_kernel(q_ref, k_ref, v_ref, kc_ref, vc_ref, bias_ref, hm_ref, o_ref, *, n_grid_rows):
    i = pl.program_id(1)
    hmask = hm_ref[...]
    hms = (hmask * (NA_DIM ** -0.5)).astype(BF16)
    kc = kc_ref[...]
    vc = vc_ref[...]
    band = NA_WIN_ROWS * GRID_W
    for rr in range(NA_QROWS):
        r = i * NA_QROWS + rr
        rs = jnp.clip(r - NA_WIN_ROWS // 2, 0, n_grid_rows - NA_WIN_ROWS)
        var = rs - r + (NA_WIN_ROWS - 1)
        start = pl.multiple_of(rs * GRID_W, GRID_W)
        keys = k_ref[pl.ds(start, band), :]
        vals = v_ref[pl.ds(start, band), :]
        qs = _stack_heads(q_ref[rr * GRID_W:(rr + 1) * GRID_W, :], hms)
        o = _attend(qs, keys, vals, bias_ref[var], kc, vc)
        o_ref[rr * GRID_W:(rr + 1) * GRID_W, :] = _unstack_heads(o, hmask, GRID_W).astype(BF16)


def _na_ctx_kernel(q_ref, kc_ref, vc_ref, hm_ref, o_ref):
    hmask = hm_ref[...]
    hms = (hmask * (NA_DIM ** -0.5)).astype(BF16)
    n = q_ref.shape[0]
    o = _attend(_stack_heads(q_ref[...], hms), None, None, None, kc_ref[...], vc_ref[...])
    o_ref[...] = _unstack_heads(o, hmask, n).astype(BF16)


def _neighborhood(proj_l, proj_c, bias, hmask, batch, seq_len, ctx_len, need_ctx_out):
    rows = seq_len // GRID_W
    qblk = NA_QROWS * GRID_W
    nq = seq_len // qblk
    out_l = pl.pallas_call(
        functools.partial(_na_kernel, n_grid_rows=rows),
        grid=(batch, nq),
        in_specs=[
            pl.BlockSpec((qblk, BRANCH_W), lambda b, i: (b * nq + i, COL_NQ)),
            pl.BlockSpec((seq_len, BRANCH_W), lambda b, i: (b, COL_NK)),
            pl.BlockSpec((seq_len, BRANCH_W), lambda b, i: (b, COL_NV)),
            pl.BlockSpec((ctx_len, BRANCH_W), lambda b, i: (b, COL_NK)),
            pl.BlockSpec((ctx_len, BRANCH_W), lambda b, i: (b, COL_NV)),
            pl.BlockSpec(bias.shape, lambda b, i: (0, 0, 0)),
            pl.BlockSpec(hmask.shape, lambda b, i: (0, 0)),
        ],
        out_specs=pl.BlockSpec((qblk, BRANCH_W), lambda b, i: (b * nq + i, 0)),
        out_shape=jax.ShapeDtypeStruct((batch * seq_len, BRANCH_W), BF16),
        compiler_params=_cparams("arbitrary", "arbitrary"),
        name="neighborhood_attn",
    )(proj_l, proj_l, proj_l, proj_c, proj_c, bias, hmask)
    out_c = None
    if need_ctx_out:
        out_c = pl.pallas_call(
            _na_ctx_kernel,
            grid=(batch,),
            in_specs=[
                pl.BlockSpec((ctx_len, BRANCH_W), lambda b: (b, COL_NQ)),
                pl.BlockSpec((ctx_len, BRANCH_W), lambda b: (b, COL_NK)),
                pl.BlockSpec((ctx_len, BRANCH_W), lambda b: (b, COL_NV)),
                pl.BlockSpec(hmask.shape, lambda b: (0, 0)),
            ],
            out_specs=pl.BlockSpec((ctx_len, BRANCH_W), lambda b: (b, 0)),
            out_shape=jax.ShapeDtypeStruct((batch * ctx_len, BRANCH_W), BF16),
            compiler_params=_cparams("arbitrary"),
            name="context_attn",
        )(proj_c, proj_c, proj_c, hmask)
    return out_l, out_c


def _merge_kernel(x_ref, mod_ref, g_ref, gt0, gt1, gt2, gt3, a_ref, s5a_ref, s5b_ref, rof_ref, rob_ref, rg_ref, na_ref,
                  wglu_ref, bglu_ref, gn_ref, avg_ref, wb_ref, wo_ref, o_ref, *, tiles_per_mod, mod_base):
    i = pl.program_id(0)
    _, _, gate_a = _mod_rows(mod_ref, i, tiles_per_mod, mod_base, 0)
    z = _gelu_tanh(jnp.concatenate([s5a_ref[...], s5b_ref[...]], axis=-1)).astype(BF16)
    zf = z.astype(F32)
    b_s5 = (zf * _sigmoid(_dot(z, wglu_ref[...]) + bglu_ref[...])).astype(BF16)
    o = rof_ref[...] + rob_ref[...]
    avg = avg_ref[...]
    hi, lo = _split_bf16(o)
    mu = _dot(hi, avg) + _dot(lo, avg)
    dlt = o - mu
    hi, lo = _split_bf16(dlt * dlt)
    var = _dot(hi, avg) + _dot(lo, avg)
    hn = dlt * lax.rsqrt(var + EPS) * gn_ref[...]
    b_ret = (_silu(rg_ref[...].astype(F32)) * hn).astype(BF16)
    outs = (a_ref[...], b_s5, b_ret, na_ref[...])
    gates = (gt0, gt1, gt2, gt3)
    y = (1.0 + jnp.tanh(gates[0][...].astype(F32))) * _dot(outs[0], wb_ref[0])
    for b in range(1, N_BRANCH):
        y = y + (1.0 + jnp.tanh(gates[b][...].astype(F32))) * _dot(outs[b], wb_ref[b])
    yo = _dot(y.astype(BF16), wo_ref[...])
    o_ref[...] = x_ref[...] + gate_a * _rms(yo, g_ref[...])


def _merge(x, mod, g1, proj, a, s5y, ret_o, na, lw, *, rows_per_mod, mod_base):
    rows, d = x.shape
    tm = min(512, rows)
    nt = rows // tm

    def row(shape, col=0):
        return pl.BlockSpec(shape, lambda i: (i, col))

    def const(arr):
        return pl.BlockSpec(arr.shape, lambda i: (0,) * arr.ndim)

    kern = functools.partial(_merge_kernel, tiles_per_mod=max(rows_per_mod // tm, 1), mod_base=mod_base)
    ins = [x, mod, g1.reshape(1, d), proj, proj, proj, proj, a, s5y[0], s5y[1], ret_o[0], ret_o[1], proj, na,
           lw['w_glu'], lw['b_glu'], lw['ret_gn'], lw['avg'], lw['w_branch'], lw['w_out']]
    specs = [
        row((tm, d)), const(mod), pl.BlockSpec((1, d), lambda i: (0, 0)),
        row((tm, d), 0), row((tm, d), 1), row((tm, d), 2), row((tm, d), 3),
        row((tm, BRANCH_W)), row((tm, 128)), row((tm, 128)),
        row((tm, BRANCH_W)), row((tm, BRANCH_W)),
        row((tm, BRANCH_W), COL_RG), row((tm, BRANCH_W)),
        const(lw['w_glu']), const(lw['b_glu']), const(lw['ret_gn']), const(lw['avg']),
        const(lw['w_branch']), const(lw['w_out']),
    ]
    return pl.pallas_call(
        kern,
        grid=(nt,),
        in_specs=specs,
        out_specs=row((tm, d)),
        out_shape=jax.ShapeDtypeStruct((rows, d), F32),
        compiler_params=_cparams("arbitrary"),
        name="merge_out",
    )(*ins)


def _ffn_kernel(x_ref, mod_ref, g2_ref, g3_ref, wg_ref, wu_ref, wd_ref, o_ref, h_scr, acc_scr,
                *, tiles_per_mod, mod_base, n_f):
    i = pl.program_id(0)
    f = pl.program_id(1)

    @pl.when(f == 0)
    def _():
        sh, sc, _ = _mod_rows(mod_ref, i, tiles_per_mod, mod_base, 3)
        h_scr[...] = (_rms(x_ref[...], g2_ref[...]) * (1.0 + sc) + sh).astype(BF16)
        acc_scr[...] = jnp.zeros_like(acc_scr)

    h = h_scr[...]
    act = (_silu(_dot(h, wg_ref[...])) * _dot(h, wu_ref[...])).astype(BF16)
    acc_scr[...] += _dot(act, wd_ref[...])

    @pl.when(f == n_f - 1)
    def _():
        _, _, gate_f = _mod_rows(mod_ref, i, tiles_per_mod, mod_base, 3)
        o_ref[...] = x_ref[...] + gate_f * _rms(acc_scr[...], g3_ref[...])


def _ffn_dense(x, mod, g2, g3, wg, wu, wd, *, rows_per_mod, mod_base):
    rows, d = x.shape
    d_ff = wg.shape[1]
    tm = min(512, rows)
    tf = d_ff // 2 if (d_ff // 2) % 128 == 0 else d_ff
    n_f = d_ff // tf
    kern = functools.partial(_ffn_kernel, tiles_per_mod=max(rows_per_mod // tm, 1), mod_base=mod_base, n_f=n_f)
    return pl.pallas_call(
        kern,
        grid=(rows // tm, n_f),
        in_specs=[
            pl.BlockSpec((tm, d), lambda i, f: (i, 0)),
            pl.BlockSpec(mod.shape, lambda i, f: (0, 0)),
            pl.BlockSpec((1, d), lambda i, f: (0, 0)),
            pl.BlockSpec((1, d), lambda i, f: (0, 0)),
            pl.BlockSpec((d, tf), lambda i, f: (0, f)),
            pl.BlockSpec((d, tf), lambda i, f: (0, f)),
            pl.BlockSpec((tf, d), lambda i, f: (f, 0)),
        ],
        out_specs=pl.BlockSpec((tm, d), lambda i, f: (i, 0)),
        out_shape=jax.ShapeDtypeStruct((rows, d), F32),
        scratch_shapes=[pltpu.VMEM((tm, d), BF16), pltpu.VMEM((tm, d), F32)],
        compiler_params=_cparams("arbitrary", "arbitrary"),
        name="ffn_dense",
    )(x, mod, g2.reshape(1, d), g3.reshape(1, d), wg, wu, wd)


def _router_kernel(x_ref, mod_ref, g2_ref, wr_ref, br_ref, h_ref, comb_ref, *, tiles_per_mod, mod_base):
    i = pl.program_id(0)
    sh, sc, _ = _mod_rows(mod_ref, i, tiles_per_mod, mod_base, 3)
    h = _rms(x_ref[...], g2_ref[...]) * (1.0 + sc) + sh
    h_ref[...] = _pack_pairs(h)
    h_hi, h_lo = _split_bf16(h)
    w_hi, w_lo = _split_bf16(wr_ref[...])
    logits = _dot(h_hi, w_hi) + _dot(h_lo, w_hi) + _dot(h_hi, w_lo) + br_ref[...]
    lane = lax.broadcasted_iota(jnp.int32, logits.shape, 1)
    v1 = jnp.max(logits, axis=-1, keepdims=True)
    i1 = jnp.min(jnp.where(logits == v1, lane, 128), axis=-1, keepdims=True)
    rest = jnp.where(lane == i1, NEG_BIG, logits)
    v2 = jnp.max(rest, axis=-1, keepdims=True)
    i2 = jnp.min(jnp.where(rest == v2, lane, 128), axis=-1, keepdims=True)
    e = jnp.exp(v2 - v1)
    w1 = 1.0 / (1.0 + e)
    w2 = e / (1.0 + e)
    meta = jnp.where(lane == 0, i1.astype(F32), 0.0) + jnp.where(lane == 1, i2.astype(F32), 0.0)
    comb_ref[...] = meta + jnp.where(lane == 2, w1, 0.0) + jnp.where(lane == 3, w2, 0.0)


def _router(x, mod, g2, w_router, b_router, *, rows_per_mod, mod_base):
    rows, d = x.shape
    tm = min(512, rows)
    wr = jnp.zeros((d, 128), F32).at[:, :N_EXPERTS].set(w_router)
    br = jnp.full((1, 128), NEG_BIG, F32).at[0, :N_EXPERTS].set(b_router)
    kern = functools.partial(_router_kernel, tiles_per_mod=max(rows_per_mod // tm, 1), mod_base=mod_base)
    return pl.pallas_call(
        kern,
        grid=(rows // tm,),
        in_specs=[
            pl.BlockSpec((tm, d), lambda i: (i, 0)),
            pl.BlockSpec(mod.shape, lambda i: (0, 0)),
            pl.BlockSpec((1, d), lambda i: (0, 0)),
            pl.BlockSpec((d, 128), lambda i: (0, 0)),
            pl.BlockSpec((1, 128), lambda i: (0, 0)),
        ],
        out_specs=[pl.BlockSpec((tm, d // 2), lambda i: (i, 0)), pl.BlockSpec((tm, 128), lambda i: (i, 0))],
        out_shape=[jax.ShapeDtypeStruct((rows, d // 2), jnp.int32), jax.ShapeDtypeStruct((rows, 128), F32)],
        compiler_params=_cparams("arbitrary"),
        name="moe_router",
    )(x, mod, g2.reshape(1, d), wr, br)


def _sc_gather(table, idx):
    n_idx = idx.shape[0]
    width = table.shape[1]
    per_worker = n_idx // SC_WORKERS
    chunk_rows = math.gcd(per_worker, SC_GATHER_ROWS)
    n_chunks = per_worker // chunk_rows
    assert per_worker * SC_WORKERS == n_idx and chunk_rows % 8 == 0
    mesh = plsc.VectorSubcoreMesh(core_axis_name="c", subcore_axis_name="s")

    assert n_chunks % 2 == 0
    buf = [pltpu.VMEM((chunk_rows,), jnp.int32), pltpu.VMEM((chunk_rows, width), table.dtype),
           pltpu.SemaphoreType.DMA, pltpu.SemaphoreType.DMA]

    @functools.partial(
        pl.kernel, mesh=mesh,
        out_type=jax.ShapeDtypeStruct((n_idx, width), table.dtype),
        scratch_types=buf + buf,
        name="sc_row_gather",
    )
    def gather(table_hbm, idx_hbm, out_hbm, idx0, rows0, g0, w0, idx1, rows1, g1, w1):
        wid = lax.axis_index("s") * SC_CORES + lax.axis_index("c")
        base = wid * per_worker
        slots = ((idx0, rows0, g0, w0), (idx1, rows1, g1, w1))

        def fetch(j, slot):
            idx_v, rows_v, g, _ = slots[slot]
            pltpu.sync_copy(idx_hbm.at[pl.ds(base + j * chunk_rows, chunk_rows)], idx_v)
            pltpu.make_async_copy(table_hbm.at[idx_v], rows_v, g).start()

        def store(j, slot):
            idx_v, rows_v, g, w = slots[slot]
            pltpu.make_async_copy(table_hbm.at[idx_v], rows_v, g).wait()
            pltpu.make_async_copy(rows_v, out_hbm.at[pl.ds(base + j * chunk_rows, chunk_rows)], w).start()

        def drain(j, slot):
            _, rows_v, _, w = slots[slot]
            pltpu.make_async_copy(rows_v, out_hbm.at[pl.ds(base + j * chunk_rows, chunk_rows)], w).wait()

        fetch(0, 0)

        @pl.loop(0, n_chunks // 2)
        def _(jj):
            j = 2 * jj

            @pl.when(jj > 0)
            def _():
                drain(j - 1, 1)

            fetch(j + 1, 1)
            store(j, 0)

            @pl.when(j + 2 < n_chunks)
            def _():
                drain(j, 0)
                fetch(j + 2, 0)

            store(j + 1, 1)

        drain(n_chunks - 2, 0)
        drain(n_chunks - 1, 1)

    return gather(table, idx)


def _sc_scatter(table, idx, n_out):
    n_idx = idx.shape[0]
    rows, width = table.shape
    per_worker = n_idx // SC_WORKERS
    chunk_rows = math.gcd(per_worker, SC_GATHER_ROWS)
    n_chunks = per_worker // chunk_rows
    assert per_worker * SC_WORKERS == n_idx and chunk_rows % 8 == 0 and rows % per_worker == 0
    mesh = plsc.VectorSubcoreMesh(core_axis_name="c", subcore_axis_name="s")

    assert n_chunks % 2 == 0
    buf = [pltpu.VMEM((chunk_rows,), jnp.int32), pltpu.VMEM((chunk_rows, width), table.dtype),
           pltpu.SemaphoreType.DMA, pltpu.SemaphoreType.DMA]

    @functools.partial(
        pl.kernel, mesh=mesh,
        out_type=jax.ShapeDtypeStruct((n_out, width), table.dtype),
        scratch_types=buf + buf,
        name="sc_row_scatter",
    )
    def scatter(table_hbm, idx_hbm, out_hbm, idx0, rows0, l0, w0, idx1, rows1, l1, w1):
        wid = lax.axis_index("s") * SC_CORES + lax.axis_index("c")
        base = wid * per_worker
        slots = ((idx0, rows0, l0, w0), (idx1, rows1, l1, w1))

        def src(j):
            return table_hbm.at[pl.ds(lax.rem(base + j * chunk_rows, rows), chunk_rows)]

        def fetch(j, slot):
            idx_v, rows_v, l, _ = slots[slot]
            pltpu.sync_copy(idx_hbm.at[pl.ds(base + j * chunk_rows, chunk_rows)], idx_v)
            pltpu.make_async_copy(src(j), rows_v, l).start()

        def store(j, slot):
            idx_v, rows_v, l, w = slots[slot]
            pltpu.make_async_copy(src(j), rows_v, l).wait()
            pltpu.make_async_copy(rows_v, out_hbm.at[idx_v], w).start()

        def drain(slot):
            idx_v, rows_v, _, w = slots[slot]
            pltpu.make_async_copy(rows_v, out_hbm.at[idx_v], w).wait()

        fetch(0, 0)

        @pl.loop(0, n_chunks // 2)
        def _(jj):
            j = 2 * jj

            @pl.when(jj > 0)
            def _():
                drain(1)

            fetch(j + 1, 1)
            store(j, 0)

            @pl.when(j + 2 < n_chunks)
            def _():
                drain(0)
                fetch(j + 2, 0)

            store(j + 1, 1)

        drain(0)
        drain(1)

    return scatter(table, idx)


def _moe_plan(meta, rows):
    tile = MOE_ROW_TILE
    n_tiles = (2 * rows) // tile + N_EXPERTS
    n_slots = n_tiles * tile
    experts = jnp.concatenate([meta[:, 0], meta[:, 1]]).astype(jnp.int32)
    onehot = (experts[:, None] == jnp.arange(N_EXPERTS)[None, :]).astype(jnp.int32)
    csum = jnp.cumsum(onehot, axis=0)
    counts = csum[-1]
    rank = jnp.sum(onehot * csum, axis=1) - 1
    padded = ((counts + tile - 1) // tile) * tile
    ends = jnp.cumsum(padded)
    starts = ends - padded
    pos = jnp.sum(onehot * starts[None, :], axis=1) + rank
    tile_start = jnp.arange(n_tiles, dtype=jnp.int32) * tile
    used = tile_start < ends[-1]
    tile_e = jnp.minimum(jnp.sum((tile_start[:, None] >= ends[None, :]).astype(jnp.int32), axis=1), N_EXPERTS - 1)
    last_e = jnp.max(jnp.where(used, tile_e, 0))
    tile_e = jnp.where(used, tile_e, last_e)
    valid_end = jnp.sum((tile_e[:, None] == jnp.arange(N_EXPERTS)[None, :]) * (starts + counts)[None, :], axis=1)
    n_valid = jnp.where(used, jnp.clip(valid_end - tile_start, 0, tile), 0).astype(jnp.int32)
    return pos.astype(jnp.int32), n_slots, tile_e.astype(jnp.int32), n_valid


def _moe_group_kernel(eid_ref, nval_ref, hs_ref, wg_ref, wu_ref, wd_ref, y_ref, h_scr, acc_scr, *, n_f):
    w = pl.program_id(0)
    f = pl.program_id(1)
    nv = nval_ref[w]

    def run(n_rows):
        rows = slice(0, n_rows)

        @pl.when(f == 0)
        def _():
            hv = _unpack_pairs(hs_ref[rows, :])
            row = lax.broadcasted_iota(jnp.int32, hv.shape, 0)
            h_scr[rows, :] = jnp.where(row < nv, hv, 0.0).astype(BF16)
            acc_scr[rows, :] = jnp.zeros((n_rows, acc_scr.shape[1]), F32)

        h = h_scr[rows, :]
        gate = _dot(h, wg_ref[...].astype(BF16))
        up = _dot(h, wu_ref[...].astype(BF16))
        acc_scr[rows, :] += _dot((_silu(gate) * up).astype(BF16), wd_ref[...].astype(BF16))

        @pl.when(f == n_f - 1)
        def _():
            y_ref[rows, :] = _pack_pairs(acc_scr[rows, :])

    half = hs_ref.shape[0] // 2

    @pl.when(nv > half)
    def _():
        run(hs_ref.shape[0])

    @pl.when((nv > 0) & (nv <= half))
    def _():
        run(half)


def _moe_grouped(hs, tile_e, n_valid, wg, wu, wd):
    n_slots = hs.shape[0]
    d = wg.shape[1]
    d_ff = wg.shape[2]
    tile = MOE_ROW_TILE
    tf = MOE_FF_TILE
    n_f = d_ff // tf

    def f_idx(f, nval, w):
        return jnp.where(nval[w] > 0, f, n_f - 1)

    grid_spec = pltpu.PrefetchScalarGridSpec(
        num_scalar_prefetch=2,
        grid=(n_slots // tile, n_f),
        in_specs=[
            pl.BlockSpec((tile, d // 2), lambda w, f, eid, nval: (w, 0)),
            pl.BlockSpec((None, d, tf), lambda w, f, eid, nval: (eid[w], 0, f_idx(f, nval, w))),
            pl.BlockSpec((None, d, tf), lambda w, f, eid, nval: (eid[w], 0, f_idx(f, nval, w))),
            pl.BlockSpec((None, tf, d), lambda w, f, eid, nval: (eid[w], f_idx(f, nval, w), 0)),
        ],
        out_specs=pl.BlockSpec((tile, d // 2), lambda w, f, eid, nval: (w, 0)),
        scratch_shapes=[pltpu.VMEM((tile, d), BF16), pltpu.VMEM((tile, d), F32)],
    )
    return pl.pallas_call(
        functools.partial(_moe_group_kernel, n_f=n_f),
        grid_spec=grid_spec,
        out_shape=jax.ShapeDtypeStruct((n_slots, d // 2), jnp.int32),
        compiler_params=_cparams("arbitrary", "arbitrary"),
        name="moe_experts",
    )(tile_e, n_valid, hs, wg, wu, wd)


def _moe_out_kernel(x_ref, y1_ref, y2_ref, meta_ref, mod_ref, g3_ref, o_ref, *, tiles_per_mod, mod_base):
    i = pl.program_id(0)
    _, _, gate_f = _mod_rows(mod_ref, i, tiles_per_mod, mod_base, 3)
    meta = meta_ref[...]
    y = meta[:, 2:3] * _unpack_pairs(y1_ref[...]) + meta[:, 3:4] * _unpack_pairs(y2_ref[...])
    o_ref[...] = x_ref[...] + gate_f * _rms(y, g3_ref[...])


def _moe_combine(x, yg, meta, mod, g3, *, rows_per_mod, mod_base):
    rows, d = x.shape
    tm = min(512, rows)
    nt = rows // tm
    kern = functools.partial(_moe_out_kernel, tiles_per_mod=max(rows_per_mod // tm, 1), mod_base=mod_base)
    return pl.pallas_call(
        kern,
        grid=(nt,),
        in_specs=[
            pl.BlockSpec((tm, d), lambda i: (i, 0)),
            pl.BlockSpec((tm, d // 2), lambda i: (i, 0)),
            pl.BlockSpec((tm, d // 2), lambda i: (nt + i, 0)),
            pl.BlockSpec((tm, 128), lambda i: (i, 0)),
            pl.BlockSpec(mod.shape, lambda i: (0, 0)),
            pl.BlockSpec((1, d), lambda i: (0, 0)),
        ],
        out_specs=pl.BlockSpec((tm, d), lambda i: (i, 0)),
        out_shape=jax.ShapeDtypeStruct((rows, d), F32),
        compiler_params=_cparams("arbitrary"),
        name="moe_combine",
    )(x, yg, yg, meta, mod, g3.reshape(1, d))


def _moe_sparse(x, h, meta, mod, g3, wg, wu, wd, *, rows_per_mod, mod_base):
    rows = x.shape[0]
    pos, n_slots, tile_e, n_valid = _moe_plan(meta, rows)
    hs = _sc_scatter(h, pos, n_slots)
    ys = _moe_grouped(hs, tile_e, n_valid, wg, wu, wd)
    yg = _sc_gather(ys, pos)
    return _moe_combine(x, yg, meta, mod, g3, rows_per_mod=rows_per_mod, mod_base=mod_base)


def _cast_kernel(w_ref, o_ref, *, scale):
    w = w_ref[...]
    o_ref[...] = (w if scale == 1.0 else w * scale).astype(BF16)


def _cast_bf16(w_stack, layer, scale=1.0):
    squeeze = w_stack.ndim == 3
    w4 = w_stack[:, None] if squeeze else w_stack
    _, n_e, k, n = w4.shape
    bk = min(k, 256)
    out = pl.pallas_call(
        functools.partial(_cast_kernel, scale=scale),
        grid=(n_e, k // bk),
        in_specs=[pl.BlockSpec((None, None, bk, n), lambda e, i: (layer, e, i, 0))],
        out_specs=pl.BlockSpec((None, bk, n), lambda e, i: (e, i, 0)),
        out_shape=jax.ShapeDtypeStruct((n_e, k, n), BF16),
        compiler_params=_cparams("arbitrary", "arbitrary"),
        name="cast_weights",
    )(w4)
    return out[0] if squeeze else out


def _permute_w_in(w_in_stack, layer):
    _, k, n = w_in_stack.shape
    n_blocks = n // BRANCH_W
    shift = 9
    n_gate_blocks = N_BRANCH * D_MODEL // BRANCH_W

    def permute_kernel(w_ref, o_ref):
        scale = jnp.where(pl.program_id(0) < n_gate_blocks, 0.5, 1.0)
        o_ref[...] = (w_ref[...] * scale).astype(BF16)

    return pl.pallas_call(
        permute_kernel,
        grid=(n_blocks,),
        in_specs=[pl.BlockSpec((None, k, BRANCH_W), lambda j: (layer, 0, (j + shift) % n_blocks))],
        out_specs=pl.BlockSpec((k, BRANCH_W), lambda j: (0, j)),
        out_shape=jax.ShapeDtypeStruct((k, n), BF16),
        compiler_params=_cparams("arbitrary"),
        name="cast_permute_w_in",
    )(w_in_stack)


def kernel(x, c, ctx, c_ctx, w_mod, b_mod, norm_g, w_in, s5_a_re, s5_a_im, s5_log_dt, s5_b_re, s5_b_im, s5_c_re, s5_c_im, s5_d, s5_w_glu, s5_b_glu, ret_decay, ret_gn, na_rpb, w_branch, w_out, ffn_w_gate, ffn_w_up, ffn_w_down, moe_w_router, moe_b_router, moe_w_gate, moe_w_up, moe_w_down):
    batch, seq_len, d = x.shape
    ctx_len = ctx.shape[1]
    depth = w_mod.shape[0]
    cond = jnp.concatenate([c, c_ctx[None, :]], axis=0)
    mod_all = _modulation(cond, w_mod, b_mod)
    rope = _rope_tables(seq_len)
    lane_h = np.repeat(np.arange(RET_HEADS), RET_DIM)
    avg = jnp.asarray((lane_h[:, None] == lane_h[None, :]).astype(np.float32) / RET_DIM, BF16)

    xl = x.reshape(batch * seq_len, d)
    xc = ctx.reshape(batch * ctx_len, d)
    lat = dict(rows_per_mod=seq_len, mod_base=0)
    cxt = dict(rows_per_mod=batch * ctx_len, mod_base=batch)

    for layer in range(depth):
        last = layer == depth - 1
        need_ctx = not last
        mod = mod_all[layer]
        ng = norm_g[layer]
        w_in_bf = _permute_w_in(w_in, layer)
        s5_tabs = _s5_tables(s5_a_re[layer], s5_a_im[layer], s5_log_dt[layer], s5_b_re[layer], s5_b_im[layer],
                             s5_c_re[layer], s5_c_im[layer], s5_d[layer], batch)
        ret_tabs = _ret_tables(ret_decay[layer])
        na_bias, na_hmask = _na_tables(na_rpb[layer])
        lw = dict(w_glu=s5_w_glu[layer].astype(BF16), b_glu=s5_b_glu[layer].reshape(1, BRANCH_W).astype(F32),
                  ret_gn=ret_gn[layer].reshape(1, BRANCH_W).astype(F32), avg=avg,
                  w_branch=_cast_bf16(w_branch, layer, 0.5), w_out=_cast_bf16(w_out, layer))

        proj_l, f_l, *s_in_l = _in_proj(xl, mod, ng[0], w_in_bf, **lat)
        proj_c, f_c, *s_in_c = _in_proj(xc, mod, ng[0], w_in_bf, **cxt)

        a_l = _fourier_latent(f_l, batch, seq_len)
        s_l, s_c = _s5_mixer(s_in_l, s_in_c, s5_tabs, batch)
        r_l, r_c = _retention(proj_l, proj_c, ret_tabs, rope, batch, seq_len, ctx_len)
        n_l, n_c = _neighborhood(proj_l, proj_c, na_bias, na_hmask, batch, seq_len, ctx_len, need_ctx)

        xl = _merge(xl, mod, ng[1], proj_l, a_l, s_l, r_l, n_l, lw, **lat)
        if need_ctx:
            a_c = _fourier_ctx(f_c, batch, ctx_len)
            xc = _merge(xc, mod, ng[1], proj_c, a_c, s_c, r_c, n_c, lw, **cxt)

        i = layer // 2
        if layer % 2 == 0:
            wg, wu, wd = _cast_bf16(ffn_w_gate, i), _cast_bf16(ffn_w_up, i), _cast_bf16(ffn_w_down, i)
            xl = _ffn_dense(xl, mod, ng[2], ng[3], wg, wu, wd, **lat)
            if need_ctx:
                xc = _ffn_dense(xc, mod, ng[2], ng[3], wg, wu, wd, **cxt)
        else:
            wg, wu, wd = moe_w_gate[i], moe_w_up[i], moe_w_down[i]
            h, meta = _router(xl, mod, ng[2], moe_w_router[i], moe_b_router[i], **lat)
            xl = _moe_sparse(xl, h, meta, mod, ng[3], wg, wu, wd, **lat)
            if need_ctx:
                hc, metac = _router(xc, mod, ng[2], moe_w_router[i], moe_b_router[i], **cxt)
                xc = _moe_sparse(xc, hc, metac, mod, ng[3], wg, wu, wd, **cxt)
    return xl.reshape(batch, seq_len, d)
```

```python
import functools
import math

import numpy as np
import jax
import jax.numpy as jnp
from jax import lax
from jax.experimental import pallas as pl
from jax.experimental.pallas import tpu as pltpu
from jax.experimental.pallas import tpu_sc as plsc

F32 = jnp.float32
BF16 = jnp.bfloat16

D_MODEL = 1024
BRANCH_W = 256
N_BRANCH = 4
GRID_W = 64
FNET_GROUP_DIM = 64
S5_GROUP_CH = 16
S5_GROUPS = 16
S5_STATE = 64
S5_CHUNK = 32
S5_PAIRS = S5_GROUPS // 2
RET_HEADS = 4
RET_DIM = 64
RET_CHUNK = 128
NA_HEADS = 4
NA_DIM = 64
NA_WIN_ROWS = 8
NA_WIN_COLS = 16
NA_QROWS = 8
ROPE_BASE = 10000.0
N_EXPERTS = 8
EPS = 1e-6
FFT_N2 = 256
NEG_BIG = -1e30
VMEM_LIMIT_BYTES = 50 * 1024 * 1024
SC_CORES = 2
SC_SUBCORES = 16
SC_WORKERS = SC_CORES * SC_SUBCORES
SC_GATHER_ROWS = 64
MOE_ROW_TILE = 1024
MOE_FF_TILE = 512
MOE_META_W = 8

COL_F, COL_S, COL_RQ, COL_RK, COL_RV, COL_RG, COL_NQ, COL_NK, COL_NV = range(16, 25)
IN_W = 9 * BRANCH_W + N_BRANCH * D_MODEL
IN_TN = 1280
IN_F_TILE = (N_BRANCH * D_MODEL) // IN_TN
IN_F_OFF = N_BRANCH * D_MODEL - IN_F_TILE * IN_TN
IN_S_OFF = IN_F_OFF + BRANCH_W


def _cparams(*sem):
    return pltpu.CompilerParams(dimension_semantics=sem, vmem_limit_bytes=VMEM_LIMIT_BYTES)


def _sigmoid(v):
    return 0.5 * jnp.tanh(0.5 * v) + 0.5


def _silu(v):
    return v * _sigmoid(v)


def _gelu_tanh(v):
    return 0.5 * v * (1.0 + jnp.tanh(math.sqrt(2.0 / math.pi) * (v + 0.044715 * (v * v * v))))


def _rms(v, g):
    ms = jnp.mean(v * v, axis=-1, keepdims=True)
    return v * lax.rsqrt(ms + EPS) * g


def _split_bf16(v):
    hi = v.astype(BF16)
    lo = (v - hi.astype(F32)).astype(BF16)
    return hi, lo


def _pack_pairs(v):
    n = v.shape[1] // 2
    lo = lax.bitcast_convert_type(v[:, :n].astype(BF16).astype(F32), jnp.int32)
    hi = lax.bitcast_convert_type(v[:, n:].astype(BF16).astype(F32), jnp.int32)
    return (hi & -65536) | ((lo >> 16) & 65535)


def _unpack_pairs(w):
    lo = lax.bitcast_convert_type(w << 16, F32)
    hi = lax.bitcast_convert_type(w & -65536, F32)
    return jnp.concatenate([lo, hi], axis=-1)


def _dot(a, b):
    return jnp.dot(a, b, preferred_element_type=F32)


def _dot_nt(a, b):
    return lax.dot_general(a, b, (((1,), (1,)), ((), ())), preferred_element_type=F32)


def _dot_tn(a, b):
    return lax.dot_general(a, b, (((0,), (0,)), ((), ())), preferred_element_type=F32)


def _mod_kernel(ct_ref, w_ref, b_ref, o_ref, *, n_cond):
    ct = ct_ref[...]
    s = _silu(ct)
    w = w_ref[...]
    rows = [jnp.sum(w * s[:, r:r + 1], axis=0, keepdims=True) for r in range(n_cond)]
    rows.append(jnp.zeros((8 - n_cond, w.shape[1]), F32))
    o_ref[...] = jnp.concatenate(rows, axis=0) + b_ref[...]


def _modulation(cond, w_mod, b_mod):
    n_layers, d, n = w_mod.shape
    tn = 512
    ct = jnp.zeros((8, d), F32).at[:cond.shape[0]].set(cond).T
    return pl.pallas_call(
        functools.partial(_mod_kernel, n_cond=cond.shape[0]),
        grid=(n_layers, n // tn),
        in_specs=[
            pl.BlockSpec((d, 8), lambda l, j: (0, 0)),
            pl.BlockSpec((None, d, tn), lambda l, j: (l, 0, j)),
            pl.BlockSpec((None, 1, tn), lambda l, j: (l, 0, j)),
        ],
        out_specs=pl.BlockSpec((None, 8, tn), lambda l, j: (l, 0, j)),
        out_shape=jax.ShapeDtypeStruct((n_layers, 8, n), F32),
        compiler_params=_cparams("arbitrary", "arbitrary"),
        name="adaln_mod",
    )(ct, w_mod, b_mod.reshape(n_layers, 1, n))


def _mod_rows(mod_ref, i, tiles_per_mod, mod_base, first):
    r = mod_base + i // tiles_per_mod
    return [mod_ref[pl.ds(r, 1), (first + k) * D_MODEL:(first + k + 1) * D_MODEL] for k in range(3)]


def _in_kernel(x_ref, mod_ref, g_ref, w_ref, proj_ref, f_ref, sa_ref, sb_ref, h_scr, *, tiles_per_mod, mod_base):
    i = pl.program_id(0)
    j = pl.program_id(1)

    @pl.when(j == 0)
    def _():
        sh, sc, _ = _mod_rows(mod_ref, i, tiles_per_mod, mod_base, 0)
        h_scr[...] = (_rms(x_ref[...], g_ref[...]) * (1.0 + sc) + sh).astype(BF16)

    res = _dot(h_scr[...], w_ref[...])
    proj_ref[...] = res.astype(BF16)

    @pl.when(j == IN_F_TILE)
    def _():
        f_ref[...] = res[:, IN_F_OFF:IN_F_OFF + BRANCH_W].astype(BF16)
        sa_ref[...] = res[:, IN_S_OFF:IN_S_OFF + 128]
        sb_ref[...] = res[:, IN_S_OFF + 128:IN_S_OFF + 256]


def _in_proj(x, mod, g, w_bf, *, rows_per_mod, mod_base):
    rows, d = x.shape
    tm = math.gcd(1024, rows_per_mod)
    kern = functools.partial(_in_kernel, tiles_per_mod=max(rows_per_mod // tm, 1), mod_base=mod_base)
    return pl.pallas_call(
        kern,
        grid=(rows // tm, IN_W // IN_TN),
        in_specs=[
            pl.BlockSpec((tm, d), lambda i, j: (i, 0)),
            pl.BlockSpec(mod.shape, lambda i, j: (0, 0)),
            pl.BlockSpec((1, d), lambda i, j: (0, 0)),
            pl.BlockSpec((d, IN_TN), lambda i, j: (0, j)),
        ],
        out_specs=[
            pl.BlockSpec((tm, IN_TN), lambda i, j: (i, j)),
            pl.BlockSpec((tm, BRANCH_W), lambda i, j: (i, 0)),
            pl.BlockSpec((tm, 128), lambda i, j: (i, 0)),
            pl.BlockSpec((tm, 128), lambda i, j: (i, 0)),
        ],
        out_shape=[
            jax.ShapeDtypeStruct((rows, IN_W), BF16),
            jax.ShapeDtypeStruct((rows, BRANCH_W), BF16),
            jax.ShapeDtypeStruct((rows, 128), F32),
            jax.ShapeDtypeStruct((rows, 128), F32),
        ],
        scratch_shapes=[pltpu.VMEM((tm, d), BF16)],
        compiler_params=_cparams("arbitrary", "arbitrary"),
        name="in_proj",
    )(x, mod, g.reshape(1, d), w_bf)


def _fft_a_kernel(x_ref, cs_ref, tc_ref, ts_ref, zr_ref, zi_ref, *, n1, n1p):
    y = _dot(cs_ref[...].astype(BF16), x_ref[...])
    yr = y[:n1]
    yi = y[n1p:n1p + n1]
    tc = tc_ref[...]
    ts = ts_ref[...]
    zr_ref[...] = (yr * tc + yi * ts).astype(BF16)
    zi_ref[...] = (yi * tc - yr * ts).astype(BF16)


def _fft_b_kernel(zr_ref, zi_ref, cs_ref, cc_ref, sc_ref, o_ref, *, kb, scale, has_imag):
    cs = cs_ref[...].astype(BF16)
    cc = cc_ref[...].astype(BF16)
    sc = sc_ref[...].astype(BF16)
    for kk in range(kb):
        a = _dot(cs, zr_ref[kk])
        if has_imag:
            b = _dot(cs, zi_ref[kk])
            xr = a[:FFT_N2] + b[FFT_N2:]
            xi = b[:FFT_N2] - a[FFT_N2:]
        else:
            xr = a[:FFT_N2]
            xi = -a[FFT_N2:]
        out = _dot(xr.astype(BF16), cc) + _dot(xi.astype(BF16), sc)
        o_ref[:, kk * BRANCH_W:(kk + 1) * BRANCH_W] = (out * scale).astype(BF16)


def _dft_tables(n):
    k = np.arange(n)
    ang = 2.0 * np.pi * ((k[:, None] * k[None, :]) % n) / n
    return np.cos(ang), np.sin(ang)


def _fft_b_call(zr, zi, n1, batch, seq_len, has_imag):
    c2, s2 = _dft_tables(FFT_N2)
    cs2 = jnp.asarray(np.concatenate([c2, s2], axis=0), F32)
    c64, s64 = _dft_tables(FNET_GROUP_DIM)
    eye = np.eye(BRANCH_W // FNET_GROUP_DIM)
    cc = jnp.asarray(np.kron(eye, c64), F32)
    sc = jnp.asarray(np.kron(eye, s64), F32)
    kb = min(8, n1)
    scale = 1.0 / math.sqrt(seq_len * FNET_GROUP_DIM)
    kern = functools.partial(_fft_b_kernel, kb=kb, scale=scale, has_imag=has_imag)
    zspec = pl.BlockSpec((None, kb, FFT_N2, BRANCH_W), lambda b, i: (b, i, 0, 0))
    out = pl.pallas_call(
        kern,
        grid=(batch, n1 // kb),
        in_specs=[
            zspec, zspec,
            pl.BlockSpec((2 * FFT_N2, FFT_N2), lambda b, i: (0, 0)),
            pl.BlockSpec((BRANCH_W, BRANCH_W), lambda b, i: (0, 0)),
            pl.BlockSpec((BRANCH_W, BRANCH_W), lambda b, i: (0, 0)),
        ],
        out_specs=pl.BlockSpec((None, FFT_N2, kb * BRANCH_W), lambda b, i: (b, 0, i)),
        out_shape=jax.ShapeDtypeStruct((batch, FFT_N2, n1 * BRANCH_W), BF16),
        compiler_params=_cparams("arbitrary", "arbitrary"),
        name="fourier_stage_b",
    )(zr, zi, cs2, cc, sc)
    return out.reshape(batch * seq_len, BRANCH_W)


def _fourier_latent(f, batch, seq_len):
    n1 = seq_len // FFT_N2
    wide = FFT_N2 * BRANCH_W
    c1, s1 = _dft_tables(n1)
    n1p = max(n1, 8)
    cs1 = np.zeros((2 * n1p, n1))
    cs1[:n1] = c1
    cs1[n1p:n1p + n1] = -s1
    k1 = np.arange(n1)[:, None]
    l2 = np.arange(FFT_N2)[None, :]
    tw = 2.0 * np.pi * (k1 * l2) / seq_len
    tc = jnp.asarray(np.repeat(np.cos(tw), BRANCH_W, axis=1), F32)
    ts = jnp.asarray(np.repeat(np.sin(tw), BRANCH_W, axis=1), F32)
    cw = min(8192, wide)
    xv = f.reshape(batch, n1, wide)
    spec = pl.BlockSpec((None, n1, cw), lambda b, j: (b, 0, j))
    tspec = pl.BlockSpec((n1, cw), lambda b, j: (0, j))
    zr, zi = pl.pallas_call(
        functools.partial(_fft_a_kernel, n1=n1, n1p=n1p),
        grid=(batch, wide // cw),
        in_specs=[spec, pl.BlockSpec((2 * n1p, n1), lambda b, j: (0, 0)), tspec, tspec],
        out_specs=[spec, spec],
        out_shape=[jax.ShapeDtypeStruct((batch, n1, wide), BF16)] * 2,
        compiler_params=_cparams("arbitrary", "arbitrary"),
        name="fourier_stage_a",
    )(xv, jnp.asarray(cs1, F32), tc, ts)
    zr = zr.reshape(batch, n1, FFT_N2, BRANCH_W)
    zi = zi.reshape(batch, n1, FFT_N2, BRANCH_W)
    return _fft_b_call(zr, zi, n1, batch, seq_len, True)


def _fourier_ctx(f, batch, ctx_len):
    assert ctx_len == FFT_N2
    z = f.reshape(batch, 1, FFT_N2, BRANCH_W)
    return _fft_b_call(z, z, 1, batch, ctx_len, False)


def _s5_tables(a_re, a_im, log_dt, b_re, b_im, c_re, c_im, d_skip, batch):
    t = S5_CHUNK
    g, p, hc = S5_GROUPS, S5_STATE, S5_GROUP_CH
    lam = lax.complex(a_re.astype(F32), a_im.astype(F32))
    dt = jnp.exp(log_dt.astype(F32))[..., None]
    ks = jnp.arange(t + 1, dtype=F32)
    apow = jnp.exp((lam * dt)[..., None] * ks)
    a_bar = apow[..., 1]
    b_bar = ((a_bar - 1.0) / lam)[..., None] * lax.complex(b_re.astype(F32), b_im.astype(F32))
    cm = lax.complex(c_re.astype(F32), c_im.astype(F32))
    kimp = jnp.real(jnp.einsum('dghp,dgpk,dgpj->dgkhj', cm, apow[..., :t], b_bar,
                               precision=lax.Precision.HIGHEST))
    kf, kb = kimp[0], kimp[1]
    kfull = jnp.concatenate([kb[:, :0:-1], kf[:, :1] + kb[:, :1], kf[:, 1:]], axis=1)
    kp = kfull.reshape(S5_PAIRS, 2, 2 * t - 1, hc, hc)
    blk = [kp[:, gi].transpose(0, 3, 1, 2) for gi in range(2)]
    zb = jnp.zeros_like(blk[0])
    strip = jnp.concatenate([jnp.stack([blk[0], zb], axis=3), jnp.stack([zb, blk[1]], axis=3)], axis=1)
    strip = strip.reshape(S5_PAIRS, 2 * hc, (2 * t - 1) * 2 * hc)
    strip = jnp.pad(strip, ((0, 0), (0, 0), (0, 2 * hc)))

    wf = jnp.einsum('gpj,gph->gjhp', apow[0][..., t - 1::-1][..., :t], b_bar[0])
    wb = jnp.einsum('gpj,gph->gjhp', apow[1][..., :t], b_bar[1])
    kinds = [jnp.real(wf), jnp.imag(wf), jnp.real(wb), jnp.imag(wb)]

    def we_pair(kd):
        k5 = kd.reshape(S5_PAIRS, 2, t, hc, p)
        z = jnp.zeros_like(k5[:, 0])
        rows = jnp.stack([jnp.concatenate([k5[:, 0], z], axis=-1), jnp.concatenate([z, k5[:, 1]], axis=-1)], axis=2)
        return rows.reshape(S5_PAIRS, 2 * t * hc, 2 * p)

    we = jnp.concatenate([we_pair(kd) for kd in kinds], axis=-1).astype(BF16)

    vf = jnp.einsum('ghp,gpt->gpth', cm[0], apow[0][..., 1:t + 1])
    vb = jnp.einsum('ghp,gpt->gpth', cm[1], apow[1][..., t:0:-1])
    vkinds = [jnp.real(vf), -jnp.imag(vf), jnp.real(vb), -jnp.imag(vb)]

    def v_pair(kd):
        k5 = kd.reshape(S5_PAIRS, 2, p, t, hc)
        z = jnp.zeros_like(k5[:, 0])
        rows = jnp.concatenate([jnp.stack([k5[:, 0], z], axis=3), jnp.stack([z, k5[:, 1]], axis=3)], axis=1)
        return rows.reshape(S5_PAIRS, 2 * p, 2 * t * hc)

    v1 = jnp.concatenate([v_pair(kd) for kd in vkinds], axis=1)
    v = jnp.concatenate([v1, v1], axis=1).astype(BF16)

    def lanes(z):
        return jnp.tile(z.reshape(1, g * p), (1, batch))

    at = apow[..., t]
    a_tab = jnp.concatenate([lanes(jnp.real(at[0])), lanes(jnp.imag(at[0])),
                             lanes(jnp.real(at[1])), lanes(jnp.imag(at[1]))], axis=0)
    dvec = jnp.tile(d_skip.astype(F32).reshape(S5_PAIRS, 1, 2 * hc), (1, t, 1)).reshape(S5_PAIRS, 1, 2 * t * hc)
    return dict(strip=strip, we=we, v=v, a_tab=a_tab, dvec=dvec)


def _s5_pack_kernel(xa_ref, xb_ref, u_ref, *, n_chunks):
    per_half = S5_PAIRS // 2
    for half, x_ref in enumerate((xa_ref, xb_ref)):
        rows = [x_ref[pl.ds(tau, n_chunks, stride=S5_CHUNK), :] for tau in range(S5_CHUNK)]
        for qq in range(per_half):
            pieces = [r[:, qq * 32:(qq + 1) * 32] for r in rows]
            u_ref[half * per_half + qq] = jnp.concatenate(pieces, axis=-1).astype(BF16)


def _s5_unpack_kernel(y_ref, oa_ref, ob_ref, *, n_chunks):
    per_half = S5_PAIRS // 2
    for half, o_ref in enumerate((oa_ref, ob_ref)):
        ys = [y_ref[half * per_half + qq].astype(F32) for qq in range(per_half)]
        for t in range(S5_CHUNK):
            pieces = [y[:, t * 32:(t + 1) * 32] for y in ys]
            o_ref[pl.ds(t, n_chunks, stride=S5_CHUNK), :] = jnp.concatenate(pieces, axis=-1)


def _s5_pack(sa, sb, batch):
    n_chunks = sa.shape[0] // batch // S5_CHUNK
    rows = n_chunks * S5_CHUNK
    cols = 2 * S5_CHUNK * S5_GROUP_CH
    half = pl.BlockSpec((rows, 128), lambda b: (b, 0))
    return pl.pallas_call(
        functools.partial(_s5_pack_kernel, n_chunks=n_chunks),
        grid=(batch,),
        in_specs=[half, half],
        out_specs=pl.BlockSpec((S5_PAIRS, None, n_chunks, cols), lambda b: (0, b, 0, 0)),
        out_shape=jax.ShapeDtypeStruct((S5_PAIRS, batch, n_chunks, cols), BF16),
        compiler_params=_cparams("arbitrary"),
        name="s5_pack",
    )(sa, sb)


def _s5_unpack(y, batch):
    n_chunks = y.shape[2]
    rows = n_chunks * S5_CHUNK
    cols = y.shape[3]
    half = pl.BlockSpec((rows, 128), lambda b: (b, 0))
    return pl.pallas_call(
        functools.partial(_s5_unpack_kernel, n_chunks=n_chunks),
        grid=(batch,),
        in_specs=[pl.BlockSpec((S5_PAIRS, None, n_chunks, cols), lambda b: (0, b, 0, 0))],
        out_specs=[half, half],
        out_shape=[jax.ShapeDtypeStruct((batch * rows, 128), F32)] * 2,
        compiler_params=_cparams("arbitrary"),
        name="s5_unpack",
    )(y)


def _s5_e_kernel(ul_ref, uc_ref, we_ref, ref_, imf_, reb_, imb_):
    u = jnp.concatenate([ul_ref[...], uc_ref[...]], axis=0)
    e = _dot(u, we_ref[...])
    ref_[...] = e[:, 0:128]
    imf_[...] = e[:, 128:256]
    reb_[...] = e[:, 256:384]
    imb_[...] = e[:, 384:512]


def _s5_scan_kernel(a_ref, ref_, imf_, reb_, imb_, prf, pif, prb, pib, *, n_rows, n_ctx):
    afr = a_ref[0:1, :]
    afi = a_ref[1:2, :]
    abr = a_ref[2:3, :]
    abi = a_ref[3:4, :]
    zero = jnp.zeros_like(afr)

    n_lat = n_rows - n_ctx

    def body(s, carry):
        sfr, sfi, sbr, sbi = carry
        nf = jnp.where(s < n_ctx, n_lat + s, s - n_ctx)
        nb = n_rows - 1 - s
        prf[pl.ds(nf, 1), :] = sfr
        pif[pl.ds(nf, 1), :] = sfi
        prb[pl.ds(nb, 1), :] = sbr
        pib[pl.ds(nb, 1), :] = sbi
        efr = ref_[pl.ds(nf, 1), :]
        efi = imf_[pl.ds(nf, 1), :]
        ebr = reb_[pl.ds(nb, 1), :]
        ebi = imb_[pl.ds(nb, 1), :]
        nfr = afr * sfr - afi * sfi + efr
        nfi = afr * sfi + afi * sfr + efi
        nbr = abr * sbr - abi * sbi + ebr
        nbi = abr * sbi + abi * sbr + ebi
        return nfr, nfi, nbr, nbi

    lax.fori_loop(0, n_rows, body, (zero, zero, zero, zero))


def _s5_y_kernel(ul_ref, uc_ref, strip_ref, v_ref, d_ref, prf, pif, prb, pib, yl_ref, yc_ref, m_scr):
    width = 2 * S5_GROUP_CH
    cols = S5_CHUNK * width
    n_lat = yl_ref.shape[0]

    @pl.when(pl.program_id(1) == 0)
    def _():
        strip = strip_ref[...]
        for j in range(S5_CHUNK):
            off = (S5_CHUNK - 1 - j) * width
            win = strip if off == 0 else pltpu.roll(strip, 2 * cols - off, axis=1)
            m_scr[j * width:(j + 1) * width, :] = win[:, :cols].astype(BF16)

    u = jnp.concatenate([ul_ref[...], uc_ref[...]], axis=0)
    y_intra = _dot(u, m_scr[...])
    pcat = jnp.concatenate([prf[...], pif[...], prb[...], pib[...]], axis=-1)
    hi, lo = _split_bf16(pcat)
    y_cross = _dot(jnp.concatenate([hi, lo], axis=-1), v_ref[...])
    y = y_intra + y_cross + d_ref[...] * u.astype(F32)
    yl_ref[...] = y[:n_lat].astype(BF16)
    yc_ref[...] = y[n_lat:].astype(BF16)


def _s5_core(ul, uc, tabs, layer, batch):
    n_lat, n_ctx = ul.shape[2], uc.shape[2]
    n_rows = n_lat + n_ctx
    width = batch * S5_PAIRS * 128
    cols = 2 * S5_CHUNK * S5_GROUP_CH
    ul_spec = pl.BlockSpec((None, None, n_lat, cols), lambda q, b: (q, b, 0, 0))
    uc_spec = pl.BlockSpec((None, None, n_ctx, cols), lambda q, b: (q, b, 0, 0))
    st_spec = pl.BlockSpec((n_rows, 128), lambda q, b: (0, b * S5_PAIRS + q))
    st_shape = jax.ShapeDtypeStruct((n_rows, width), F32)
    e4 = pl.pallas_call(
        _s5_e_kernel,
        grid=(S5_PAIRS, batch),
        in_specs=[ul_spec, uc_spec, pl.BlockSpec((None, None, cols, 512), lambda q, b: (layer, q, 0, 0))],
        out_specs=[st_spec] * 4,
        out_shape=[st_shape] * 4,
        compiler_params=_cparams("arbitrary", "arbitrary"),
        name="s5_chunk_states",
    )(ul, uc, tabs['we'])
    p4 = pl.pallas_call(
        functools.partial(_s5_scan_kernel, n_rows=n_rows, n_ctx=n_ctx),
        out_shape=[st_shape] * 4,
        compiler_params=pltpu.CompilerParams(vmem_limit_bytes=VMEM_LIMIT_BYTES),
        name="s5_state_scan",
    )(tabs['a_tab'][layer], *e4)
    y = pl.pallas_call(
        _s5_y_kernel,
        grid=(S5_PAIRS, batch),
        in_specs=[
            ul_spec, uc_spec,
            pl.BlockSpec((None, None, 2 * S5_GROUP_CH, 2 * cols), lambda q, b: (layer, q, 0, 0)),
            pl.BlockSpec((None, None, cols, cols), lambda q, b: (layer, q, 0, 0)),
            pl.BlockSpec((None, None, 1, cols), lambda q, b: (layer, q, 0, 0)),
            st_spec, st_spec, st_spec, st_spec,
        ],
        out_specs=[ul_spec, uc_spec],
        out_shape=[
            jax.ShapeDtypeStruct((S5_PAIRS, batch, n_lat, cols), BF16),
            jax.ShapeDtypeStruct((S5_PAIRS, batch, n_ctx, cols), BF16),
        ],
        scratch_shapes=[pltpu.VMEM((cols, cols), BF16)],
        compiler_params=_cparams("arbitrary", "arbitrary"),
        name="s5_outputs",
    )(ul, uc, tabs['strip'], tabs['v'], tabs['dvec'], *p4)
    return y


def _s5_mixer(s_lat, s_ctx, tabs, layer, batch):
    ul = _s5_pack(*s_lat, batch)
    uc = _s5_pack(*s_ctx, batch)
    yl, yc = _s5_core(ul, uc, tabs, layer, batch)
    return _s5_unpack(yl, batch), _s5_unpack(yc, batch)


def _ret_tables(ret_decay):
    c = RET_CHUNK
    lg = jax.nn.log_sigmoid(ret_decay.astype(F32))
    lane_h = np.repeat(np.arange(RET_HEADS), RET_DIM)
    lgl = jnp.repeat(lg, RET_DIM, axis=1)
    pos = jnp.arange(c, dtype=F32)[:, None]
    qd = jnp.stack([jnp.exp((pos + 1.0) * lgl[0][None]), jnp.exp((c - pos) * lgl[1][None])])
    kd = jnp.stack([jnp.exp((c - 1.0 - pos) * lgl[0][None]), jnp.exp(pos * lgl[1][None])])
    bmask = jnp.asarray((lane_h[:, None] == lane_h[None, :]).astype(np.float32))
    cd = jnp.exp(c * lgl)[:, :, None] * bmask[None]
    diff = pos - pos.T
    dm = []
    for h in range(RET_HEADS):
        fw = jnp.where(diff >= 0, jnp.exp(jnp.maximum(diff, 0.0) * lg[0, h]), 0.0)
        bw = jnp.where(diff <= 0, jnp.exp(jnp.maximum(-diff, 0.0) * lg[1, h]), 0.0)
        dm.append(fw + bw)
    dm = jnp.concatenate(dm, axis=0)
    return dict(qd=qd, kd=kd, cd=cd, dm=dm)


def _ret_masks():
    lane_h = np.repeat(np.arange(RET_HEADS), RET_DIM)
    bmask = (lane_h[:, None] == lane_h[None, :]).astype(np.float32)
    hmask = (np.arange(RET_HEADS)[:, None] == lane_h[None, :]).astype(np.float32)
    return jnp.asarray(bmask), jnp.asarray(hmask)


def _rope_tables(n_tokens):
    t = np.arange(n_tokens)
    row = (t // GRID_W).astype(np.float64)
    col = (t % GRID_W).astype(np.float64)
    n_freq = RET_DIM // 4
    inv_freq = 1.0 / (ROPE_BASE ** (np.arange(n_freq, dtype=np.float64) / n_freq))
    ang = np.concatenate([row[:, None] * inv_freq, col[:, None] * inv_freq], axis=-1)
    cos = np.cos(ang)
    sin = np.sin(ang)
    cos_t = np.tile(np.concatenate([cos, cos], axis=-1), (1, RET_HEADS))
    sin_t = np.tile(np.concatenate([-sin, sin], axis=-1), (1, RET_HEADS))
    half = RET_DIM // 2
    perm = np.arange(BRANCH_W) ^ half
    swap = np.zeros((BRANCH_W, BRANCH_W), np.float32)
    swap[perm, np.arange(BRANCH_W)] = 1.0
    return jnp.asarray(cos_t, F32), jnp.asarray(sin_t, F32), jnp.asarray(swap, BF16)


def _ret_chunk(q, k, v, s, qd, kd, cd, bmask, dm, hmask, with_intra):
    cross = _dot((q * qd).astype(BF16), s.astype(BF16))
    s_new = cd * s + bmask * _dot_tn((k * kd).astype(BF16), v)
    if not with_intra:
        return cross, s_new
    qb = q.astype(BF16)
    kb = k.astype(BF16)
    qs = jnp.concatenate([qb * hmask[h:h + 1].astype(BF16) for h in range(RET_HEADS)], axis=0)
    scores = _dot_nt(qs, kb) * dm
    ov = _dot(scores.astype(BF16), v)
    c = q.shape[0]
    inner = ov[0:c] * hmask[0:1]
    for h in range(1, RET_HEADS):
        inner = inner + ov[h * c:(h + 1) * c] * hmask[h:h + 1]
    return inner + cross, s_new


def _ret_kernel(qf_ref, kf_ref, vf_ref, qb_ref, kb_ref, vb_ref, qc_ref, kc_ref, vc_ref,
                cosf_ref, sinf_ref, cosb_ref, sinb_ref, swap_ref,
                qd_ref, kd_ref, cd_ref, bm_ref, dm_ref, hm_ref,
                of_ref, ob_ref, ocf_ref, ocb_ref, sf_scr, sb_scr, *, n_chunks, n_ctx_chunks):
    i = pl.program_id(1)
    c = RET_CHUNK
    k_scale = RET_DIM ** -0.5
    bmask = bm_ref[...]
    dm = dm_ref[...]
    hmask = hm_ref[...]
    tabs = [(qd_ref[d], kd_ref[d], cd_ref[d]) for d in range(2)]

    @pl.when(i == 0)
    def _():
        for d, oc_ref, s_scr in ((0, ocf_ref, sf_scr), (1, ocb_ref, sb_scr)):
            qd, kd, cd = tabs[d]
            s = jnp.zeros((BRANCH_W, BRANCH_W), F32)
            order = range(n_ctx_chunks) if d == 0 else range(n_ctx_chunks - 1, -1, -1)
            for cc in order:
                sl = slice(cc * c, (cc + 1) * c)
                o, s = _ret_chunk(qc_ref[sl, :].astype(F32), kc_ref[sl, :].astype(F32) * k_scale, vc_ref[sl, :],
                                  s, qd, kd, cd, bmask, dm, hmask, d == 0)
                oc_ref[sl, :] = o
            s_scr[...] = s

    swap = swap_ref[...]

    def rope(x_ref, cos_ref, sin_ref, sl):
        xb = x_ref[sl, :]
        return xb.astype(F32) * cos_ref[sl, :] + _dot(xb, swap) * sin_ref[sl, :]

    sf = sf_scr[...]
    sb = sb_scr[...]
    for step in range(n_chunks):
        sl = slice(step * c, (step + 1) * c)
        q = rope(qf_ref, cosf_ref, sinf_ref, sl)
        k = rope(kf_ref, cosf_ref, sinf_ref, sl) * k_scale
        o, sf = _ret_chunk(q, k, vf_ref[sl, :], sf, *tabs[0], bmask, dm, hmask, True)
        of_ref[sl, :] = o
        cb = n_chunks - 1 - step
        sl = slice(cb * c, (cb + 1) * c)
        q = rope(qb_ref, cosb_ref, sinb_ref, sl)
        k = rope(kb_ref, cosb_ref, sinb_ref, sl) * k_scale
        o, sb = _ret_chunk(q, k, vb_ref[sl, :], sb, *tabs[1], bmask, dm, hmask, False)
        ob_ref[sl, :] = o
    sf_scr[...] = sf
    sb_scr[...] = sb


def _retention(proj_l, proj_c, tabs, layer, masks, rope, batch, seq_len, ctx_len):
    n_chunks = 4
    blk = n_chunks * RET_CHUNK
    nblk = seq_len // blk
    cos_t, sin_t, swap = rope

    def lat(col, back):
        if back:
            return pl.BlockSpec((blk, BRANCH_W), lambda b, i: (b * nblk + nblk - 1 - i, col))
        return pl.BlockSpec((blk, BRANCH_W), lambda b, i: (b * nblk + i, col))

    def ctx(col):
        return pl.BlockSpec((ctx_len, BRANCH_W), lambda b, i: (b, col))

    def const(shape):
        return pl.BlockSpec(shape, lambda b, i: (0,) * len(shape))

    def per_layer(shape):
        return pl.BlockSpec((None,) + shape, lambda b, i: (layer,) + (0,) * len(shape))

    tab_f = pl.BlockSpec((blk, BRANCH_W), lambda b, i: (i, 0))
    tab_b = pl.BlockSpec((blk, BRANCH_W), lambda b, i: (nblk - 1 - i, 0))
    kern = functools.partial(_ret_kernel, n_chunks=n_chunks, n_ctx_chunks=ctx_len // RET_CHUNK)
    c = RET_CHUNK
    ctx_out = pl.BlockSpec((ctx_len, BRANCH_W), lambda b, i: (b, 0))
    o_f, o_b, oc_f, oc_b = pl.pallas_call(
        kern,
        grid=(batch, nblk),
        in_specs=[
            lat(COL_RQ, False), lat(COL_RK, False), lat(COL_RV, False),
            lat(COL_RQ, True), lat(COL_RK, True), lat(COL_RV, True),
            ctx(COL_RQ), ctx(COL_RK), ctx(COL_RV),
            tab_f, tab_f, tab_b, tab_b, const((BRANCH_W, BRANCH_W)),
            per_layer((2, c, BRANCH_W)), per_layer((2, c, BRANCH_W)), per_layer((2, BRANCH_W, BRANCH_W)),
            const((BRANCH_W, BRANCH_W)), per_layer((RET_HEADS * c, c)), const((RET_HEADS, BRANCH_W)),
        ],
        out_specs=[lat(0, False), lat(0, True), ctx_out, ctx_out],
        out_shape=[
            jax.ShapeDtypeStruct((batch * seq_len, BRANCH_W), F32),
            jax.ShapeDtypeStruct((batch * seq_len, BRANCH_W), F32),
            jax.ShapeDtypeStruct((batch * ctx_len, BRANCH_W), F32),
            jax.ShapeDtypeStruct((batch * ctx_len, BRANCH_W), F32),
        ],
        scratch_shapes=[pltpu.VMEM((BRANCH_W, BRANCH_W), F32), pltpu.VMEM((BRANCH_W, BRANCH_W), F32)],
        compiler_params=_cparams("arbitrary", "arbitrary"),
        name="retention",
    )(proj_l, proj_l, proj_l, proj_l, proj_l, proj_l, proj_c, proj_c, proj_c,
      cos_t, sin_t, cos_t, sin_t, swap,
      tabs['qd'], tabs['kd'], tabs['cd'], masks[0], tabs['dm'], masks[1])
    return (o_f, o_b), (oc_f, oc_b)


def _na_tables(rpb):
    kr, kw = NA_WIN_ROWS, NA_WIN_COLS
    col = np.arange(GRID_W)
    col_start = np.clip(col - kw // 2, 0, GRID_W - kw)
    in_win = (col[None, :] >= col_start[:, None]) & (col[None, :] < col_start[:, None] + kw)
    dc = np.clip(col[None, :] - col[:, None], -(kw - 1), kw - 1) + (kw - 1)
    var = np.arange(kr)[:, None] + np.arange(kr)[None, :]
    pick_r = (var[:, :, None] == np.arange(2 * kr - 1)[None, None, :]).astype(np.float32)
    pick_c = (dc[:, :, None] == np.arange(2 * kw - 1)[None, None, :]).astype(np.float32)
    bias = jnp.einsum('vir,hrc,qkc->vhqik', jnp.asarray(pick_r), rpb.astype(F32), jnp.asarray(pick_c),
                      precision=lax.Precision.HIGHEST)
    bias = jnp.where(jnp.asarray(in_win)[None, None, :, None, :], bias, NEG_BIG)
    bias = bias.reshape(kr, NA_HEADS * GRID_W, kr * GRID_W)
    return bias


def _na_head_mask():
    lane_h = np.repeat(np.arange(NA_HEADS), NA_DIM)
    hmask = (np.arange(NA_HEADS)[:, None] == lane_h[None, :]).astype(np.float32)
    return jnp.asarray(hmask, F32)


def _attend(qs, keys, vals, bias, kc, vc):
    s_ctx = _dot_nt(qs, kc)
    m = jnp.max(s_ctx, axis=-1, keepdims=True)
    if keys is not None:
        s_band = _dot_nt(qs, keys) + bias
        m = jnp.maximum(m, jnp.max(s_band, axis=-1, keepdims=True))
        p_band = jnp.exp(s_band - m)
    p_ctx = jnp.exp(s_ctx - m)
    l = jnp.sum(p_ctx, axis=-1, keepdims=True)
    o = _dot(p_ctx.astype(BF16), vc)
    if keys is not None:
        l = l + jnp.sum(p_band, axis=-1, keepdims=True)
        o = o + _dot(p_band.astype(BF16), vals)
    return o / l


def _stack_heads(q, hmask_scaled):
    return jnp.concatenate([q * hmask_scaled[h:h + 1] for h in range(NA_HEADS)], axis=0)


def _unstack_heads(o, hmask, n):
    out = o[0:n] * hmask[0:1]
    for h in range(1, NA_HEADS):
        out = out + o[h * n:(h + 1) * n] * hmask[h:h + 1]
    return out


def _na_kernel(q_ref, k_ref, v_ref, kc_ref, vc_ref, bias_ref, hm_ref, o_ref, *, n_grid_rows):
    i = pl.program_id(1)
    hmask = hm_ref[...]
    hms = (hmask * (NA_DIM ** -0.5)).astype(BF16)
    kc = kc_ref[...]
    vc = vc_ref[...]
    band = NA_WIN_ROWS * GRID_W
    for rr in range(NA_QROWS):
        r = i * NA_QROWS + rr
        rs = jnp.clip(r - NA_WIN_ROWS // 2, 0, n_grid_rows - NA_WIN_ROWS)
        var = rs - r + (NA_WIN_ROWS - 1)
        start = pl.multiple_of(rs * GRID_W, GRID_W)
        keys = k_ref[pl.ds(start, band), :]
        vals = v_ref[pl.ds(start, band), :]
        qs = _stack_heads(q_ref[rr * GRID_W:(rr + 1) * GRID_W, :], hms)
        o = _attend(qs, keys, vals, bias_ref[var], kc, vc)
        o_ref[rr * GRID_W:(rr + 1) * GRID_W, :] = _unstack_heads(o, hmask, GRID_W).astype(BF16)


def _na_ctx_kernel(q_ref, kc_ref, vc_ref, hm_ref, o_ref):
    hmask = hm_ref[...]
    hms = (hmask * (NA_DIM ** -0.5)).astype(BF16)
    n = q_ref.shape[0]
    o = _attend(_stack_heads(q_ref[...], hms), None, None, None, kc_ref[...], vc_ref[...])
    o_ref[...] = _unstack_heads(o, hmask, n).astype(BF16)


def _neighborhood(proj_l, proj_c, bias, layer, hmask, batch, seq_len, ctx_len, need_ctx_out):
    rows = seq_len // GRID_W
    qblk = NA_QROWS * GRID_W
    nq = seq_len // qblk
    out_l = pl.pallas_call(
        functools.partial(_na_kernel, n_grid_rows=rows),
        grid=(batch, nq),
        in_specs=[
            pl.BlockSpec((qblk, BRANCH_W), lambda b, i: (b * nq + i, COL_NQ)),
            pl.BlockSpec((seq_len, BRANCH_W), lambda b, i: (b, COL_NK)),
            pl.BlockSpec((seq_len, BRANCH_W), lambda b, i: (b, COL_NV)),
            pl.BlockSpec((ctx_len, BRANCH_W), lambda b, i: (b, COL_NK)),
            pl.BlockSpec((ctx_len, BRANCH_W), lambda b, i: (b, COL_NV)),
            pl.BlockSpec((None,) + bias.shape[1:], lambda b, i: (layer, 0, 0, 0)),
            pl.BlockSpec(hmask.shape, lambda b, i: (0, 0)),
        ],
        out_specs=pl.BlockSpec((qblk, BRANCH_W), lambda b, i: (b * nq + i, 0)),
        out_shape=jax.ShapeDtypeStruct((batch * seq_len, BRANCH_W), BF16),
        compiler_params=_cparams("arbitrary", "arbitrary"),
        name="neighborhood_attn",
    )(proj_l, proj_l, proj_l, proj_c, proj_c, bias, hmask)
    out_c = None
    if need_ctx_out:
        out_c = pl.pallas_call(
            _na_ctx_kernel,
            grid=(batch,),
            in_specs=[
                pl.BlockSpec((ctx_len, BRANCH_W), lambda b: (b, COL_NQ)),
                pl.BlockSpec((ctx_len, BRANCH_W), lambda b: (b, COL_NK)),
                pl.BlockSpec((ctx_len, BRANCH_W), lambda b: (b, COL_NV)),
                pl.BlockSpec(hmask.shape, lambda b: (0, 0)),
            ],
            out_specs=pl.BlockSpec((ctx_len, BRANCH_W), lambda b: (b, 0)),
            out_shape=jax.ShapeDtypeStruct((batch * ctx_len, BRANCH_W), BF16),
            compiler_params=_cparams("arbitrary"),
            name="context_attn",
        )(proj_c, proj_c, proj_c, hmask)
    return out_l, out_c


def _merge_kernel(x_ref, mod_ref, g_ref, gt0, gt1, gt2, gt3, a_ref, s5a_ref, s5b_ref, rof_ref, rob_ref, rg_ref, na_ref,
                  wglu_ref, bglu_ref, gn_ref, avg_ref, wb_ref, wo_ref, o_ref, *, tiles_per_mod, mod_base):
    i = pl.program_id(0)
    _, _, gate_a = _mod_rows(mod_ref, i, tiles_per_mod, mod_base, 0)
    z = _gelu_tanh(jnp.concatenate([s5a_ref[...], s5b_ref[...]], axis=-1)).astype(BF16)
    zf = z.astype(F32)
    b_s5 = (zf * _sigmoid(_dot(z, wglu_ref[...]) + bglu_ref[...])).astype(BF16)
    o = rof_ref[...] + rob_ref[...]
    avg = avg_ref[...]
    hi, lo = _split_bf16(o)
    mu = _dot(hi, avg) + _dot(lo, avg)
    dlt = o - mu
    hi, lo = _split_bf16(dlt * dlt)
    var = _dot(hi, avg) + _dot(lo, avg)
    hn = dlt * lax.rsqrt(var + EPS) * gn_ref[...]
    b_ret = (_silu(rg_ref[...].astype(F32)) * hn).astype(BF16)
    outs = (a_ref[...], b_s5, b_ret, na_ref[...])
    gates = (gt0, gt1, gt2, gt3)
    y = (1.0 + jnp.tanh(gates[0][...].astype(F32))) * _dot(outs[0], wb_ref[0])
    for b in range(1, N_BRANCH):
        y = y + (1.0 + jnp.tanh(gates[b][...].astype(F32))) * _dot(outs[b], wb_ref[b])
    yo = _dot(y.astype(BF16), wo_ref[...])
    o_ref[...] = x_ref[...] + gate_a * _rms(yo, g_ref[...])


def _merge(x, mod, g1, proj, a, s5y, ret_o, na, lw, *, rows_per_mod, mod_base):
    rows, d = x.shape
    tm = min(512, rows)
    nt = rows // tm

    def row(shape, col=0):
        return pl.BlockSpec(shape, lambda i: (i, col))

    def const(arr):
        return pl.BlockSpec(arr.shape, lambda i: (0,) * arr.ndim)

    kern = functools.partial(_merge_kernel, tiles_per_mod=max(rows_per_mod // tm, 1), mod_base=mod_base)
    ins = [x, mod, g1.reshape(1, d), proj, proj, proj, proj, a, s5y[0], s5y[1], ret_o[0], ret_o[1], proj, na,
           lw['w_glu'], lw['b_glu'], lw['ret_gn'], lw['avg'], lw['w_branch'], lw['w_out']]
    specs = [
        row((tm, d)), const(mod), pl.BlockSpec((1, d), lambda i: (0, 0)),
        row((tm, d), 0), row((tm, d), 1), row((tm, d), 2), row((tm, d), 3),
        row((tm, BRANCH_W)), row((tm, 128)), row((tm, 128)),
        row((tm, BRANCH_W)), row((tm, BRANCH_W)),
        row((tm, BRANCH_W), COL_RG), row((tm, BRANCH_W)),
        const(lw['w_glu']), const(lw['b_glu']), const(lw['ret_gn']), const(lw['avg']),
        const(lw['w_branch']), const(lw['w_out']),
    ]
    return pl.pallas_call(
        kern,
        grid=(nt,),
        in_specs=specs,
        out_specs=row((tm, d)),
        out_shape=jax.ShapeDtypeStruct((rows, d), F32),
        compiler_params=_cparams("arbitrary"),
        name="merge_out",
    )(*ins)


def _ffn_kernel(x_ref, mod_ref, g2_ref, g3_ref, wg_ref, wu_ref, wd_ref, o_ref, h_scr, acc_scr,
                *, tiles_per_mod, mod_base, n_f):
    i = pl.program_id(0)
    f = pl.program_id(1)

    @pl.when(f == 0)
    def _():
        sh, sc, _ = _mod_rows(mod_ref, i, tiles_per_mod, mod_base, 3)
        h_scr[...] = (_rms(x_ref[...], g2_ref[...]) * (1.0 + sc) + sh).astype(BF16)
        acc_scr[...] = jnp.zeros_like(acc_scr)

    h = h_scr[...]
    act = (_silu(_dot(h, wg_ref[...])) * _dot(h, wu_ref[...])).astype(BF16)
    acc_scr[...] += _dot(act, wd_ref[...])

    @pl.when(f == n_f - 1)
    def _():
        _, _, gate_f = _mod_rows(mod_ref, i, tiles_per_mod, mod_base, 3)
        o_ref[...] = x_ref[...] + gate_f * _rms(acc_scr[...], g3_ref[...])


def _ffn_dense(x, mod, g2, g3, wg, wu, wd, *, rows_per_mod, mod_base):
    rows, d = x.shape
    d_ff = wg.shape[1]
    tm = min(512, rows)
    tf = d_ff // 2 if (d_ff // 2) % 128 == 0 else d_ff
    n_f = d_ff // tf
    kern = functools.partial(_ffn_kernel, tiles_per_mod=max(rows_per_mod // tm, 1), mod_base=mod_base, n_f=n_f)
    return pl.pallas_call(
        kern,
        grid=(rows // tm, n_f),
        in_specs=[
            pl.BlockSpec((tm, d), lambda i, f: (i, 0)),
            pl.BlockSpec(mod.shape, lambda i, f: (0, 0)),
            pl.BlockSpec((1, d), lambda i, f: (0, 0)),
            pl.BlockSpec((1, d), lambda i, f: (0, 0)),
            pl.BlockSpec((d, tf), lambda i, f: (0, f)),
            pl.BlockSpec((d, tf), lambda i, f: (0, f)),
            pl.BlockSpec((tf, d), lambda i, f: (f, 0)),
        ],
        out_specs=pl.BlockSpec((tm, d), lambda i, f: (i, 0)),
        out_shape=jax.ShapeDtypeStruct((rows, d), F32),
        scratch_shapes=[pltpu.VMEM((tm, d), BF16), pltpu.VMEM((tm, d), F32)],
        compiler_params=_cparams("arbitrary", "arbitrary"),
        name="ffn_dense",
    )(x, mod, g2.reshape(1, d), g3.reshape(1, d), wg, wu, wd)


def _router_kernel(x_ref, mod_ref, g2_ref, wr_ref, br_ref, h_ref, comb_ref, *, tiles_per_mod, mod_base):
    i = pl.program_id(0)
    sh, sc, _ = _mod_rows(mod_ref, i, tiles_per_mod, mod_base, 3)
    h = _rms(x_ref[...], g2_ref[...]) * (1.0 + sc) + sh
    h_ref[...] = _pack_pairs(h)
    h_hi, h_lo = _split_bf16(h)
    w_hi, w_lo = _split_bf16(wr_ref[...])
    logits = _dot(h_hi, w_hi) + _dot(h_lo, w_hi) + _dot(h_hi, w_lo) + br_ref[...]
    lane = lax.broadcasted_iota(jnp.int32, logits.shape, 1)
    v1 = jnp.max(logits, axis=-1, keepdims=True)
    i1 = jnp.min(jnp.where(logits == v1, lane, 128), axis=-1, keepdims=True)
    rest = jnp.where(lane == i1, NEG_BIG, logits)
    v2 = jnp.max(rest, axis=-1, keepdims=True)
    i2 = jnp.min(jnp.where(rest == v2, lane, 128), axis=-1, keepdims=True)
    e = jnp.exp(v2 - v1)
    w1 = 1.0 / (1.0 + e)
    w2 = e / (1.0 + e)
    meta = jnp.where(lane == 0, i1.astype(F32), 0.0) + jnp.where(lane == 1, i2.astype(F32), 0.0)
    meta = meta + jnp.where(lane == 2, w1, 0.0) + jnp.where(lane == 3, w2, 0.0)
    comb_ref[...] = meta[:, :MOE_META_W]


def _router(x, mod, g2, w_router, b_router, *, rows_per_mod, mod_base):
    rows, d = x.shape
    tm = min(512, rows)
    wr = jnp.zeros((d, 128), F32).at[:, :N_EXPERTS].set(w_router)
    br = jnp.full((1, 128), NEG_BIG, F32).at[0, :N_EXPERTS].set(b_router)
    kern = functools.partial(_router_kernel, tiles_per_mod=max(rows_per_mod // tm, 1), mod_base=mod_base)
    return pl.pallas_call(
        kern,
        grid=(rows // tm,),
        in_specs=[
            pl.BlockSpec((tm, d), lambda i: (i, 0)),
            pl.BlockSpec(mod.shape, lambda i: (0, 0)),
            pl.BlockSpec((1, d), lambda i: (0, 0)),
            pl.BlockSpec((d, 128), lambda i: (0, 0)),
            pl.BlockSpec((1, 128), lambda i: (0, 0)),
        ],
        out_specs=[pl.BlockSpec((tm, d // 2), lambda i: (i, 0)), pl.BlockSpec((tm, MOE_META_W), lambda i: (i, 0))],
        out_shape=[jax.ShapeDtypeStruct((rows, d // 2), jnp.int32), jax.ShapeDtypeStruct((rows, MOE_META_W), F32)],
        compiler_params=_cparams("arbitrary"),
        name="moe_router",
    )(x, mod, g2.reshape(1, d), wr, br)


def _sc_gather(table, idx):
    n_idx = idx.shape[0]
    width = table.shape[1]
    per_worker = n_idx // SC_WORKERS
    chunk_rows = math.gcd(per_worker, SC_GATHER_ROWS)
    n_chunks = per_worker // chunk_rows
    assert per_worker * SC_WORKERS == n_idx and chunk_rows % 8 == 0
    mesh = plsc.VectorSubcoreMesh(core_axis_name="c", subcore_axis_name="s")

    assert n_chunks % 2 == 0
    buf = [pltpu.VMEM((chunk_rows,), jnp.int32), pltpu.VMEM((chunk_rows, width), table.dtype),
           pltpu.SemaphoreType.DMA, pltpu.SemaphoreType.DMA]

    @functools.partial(
        pl.kernel, mesh=mesh,
        out_type=jax.ShapeDtypeStruct((n_idx, width), table.dtype),
        scratch_types=buf + buf,
        name="sc_row_gather",
    )
    def gather(table_hbm, idx_hbm, out_hbm, idx0, rows0, g0, w0, idx1, rows1, g1, w1):
        wid = lax.axis_index("s") * SC_CORES + lax.axis_index("c")
        base = wid * per_worker
        slots = ((idx0, rows0, g0, w0), (idx1, rows1, g1, w1))

        def fetch(j, slot):
            idx_v, rows_v, g, _ = slots[slot]
            pltpu.sync_copy(idx_hbm.at[pl.ds(base + j * chunk_rows, chunk_rows)], idx_v)
            pltpu.make_async_copy(table_hbm.at[idx_v], rows_v, g).start()

        def store(j, slot):
            idx_v, rows_v, g, w = slots[slot]
            pltpu.make_async_copy(table_hbm.at[idx_v], rows_v, g).wait()
            pltpu.make_async_copy(rows_v, out_hbm.at[pl.ds(base + j * chunk_rows, chunk_rows)], w).start()

        def drain(j, slot):
            _, rows_v, _, w = slots[slot]
            pltpu.make_async_copy(rows_v, out_hbm.at[pl.ds(base + j * chunk_rows, chunk_rows)], w).wait()

        fetch(0, 0)

        @pl.loop(0, n_chunks // 2)
        def _(jj):
            j = 2 * jj

            @pl.when(jj > 0)
            def _():
                drain(j - 1, 1)

            fetch(j + 1, 1)
            store(j, 0)

            @pl.when(j + 2 < n_chunks)
            def _():
                drain(j, 0)
                fetch(j + 2, 0)

            store(j + 1, 1)

        drain(n_chunks - 2, 0)
        drain(n_chunks - 1, 1)

    return gather(table, idx)


def _sc_scatter(table, idx, n_out):
    n_idx = idx.shape[0]
    rows, width = table.shape
    per_worker = n_idx // SC_WORKERS
    chunk_rows = math.gcd(per_worker, SC_GATHER_ROWS)
    n_chunks = per_worker // chunk_rows
    assert per_worker * SC_WORKERS == n_idx and chunk_rows % 8 == 0 and rows % per_worker == 0
    mesh = plsc.VectorSubcoreMesh(core_axis_name="c", subcore_axis_name="s")

    assert n_chunks % 2 == 0
    buf = [pltpu.VMEM((chunk_rows,), jnp.int32), pltpu.VMEM((chunk_rows, width), table.dtype),
           pltpu.SemaphoreType.DMA, pltpu.SemaphoreType.DMA]

    @functools.partial(
        pl.kernel, mesh=mesh,
        out_type=jax.ShapeDtypeStruct((n_out, width), table.dtype),
        scratch_types=buf + buf,
        name="sc_row_scatter",
    )
    def scatter(table_hbm, idx_hbm, out_hbm, idx0, rows0, l0, w0, idx1, rows1, l1, w1):
        wid = lax.axis_index("s") * SC_CORES + lax.axis_index("c")
        base = wid * per_worker
        slots = ((idx0, rows0, l0, w0), (idx1, rows1, l1, w1))

        def src(j):
            return table_hbm.at[pl.ds(lax.rem(base + j * chunk_rows, rows), chunk_rows)]

        def fetch(j, slot):
            idx_v, rows_v, l, _ = slots[slot]
            pltpu.sync_copy(idx_hbm.at[pl.ds(base + j * chunk_rows, chunk_rows)], idx_v)
            pltpu.make_async_copy(src(j), rows_v, l).start()

        def store(j, slot):
            idx_v, rows_v, l, w = slots[slot]
            pltpu.make_async_copy(src(j), rows_v, l).wait()
            pltpu.make_async_copy(rows_v, out_hbm.at[idx_v], w).start()

        def drain(slot):
            idx_v, rows_v, _, w = slots[slot]
            pltpu.make_async_copy(rows_v, out_hbm.at[idx_v], w).wait()

        fetch(0, 0)

        @pl.loop(0, n_chunks // 2)
        def _(jj):
            j = 2 * jj

            @pl.when(jj > 0)
            def _():
                drain(1)

            fetch(j + 1, 1)
            store(j, 0)

            @pl.when(j + 2 < n_chunks)
            def _():
                drain(0)
                fetch(j + 2, 0)

            store(j + 1, 1)

        drain(0)
        drain(1)

    return scatter(table, idx)


def _moe_plan(meta, rows):
    tile = MOE_ROW_TILE
    n_tiles = (2 * rows) // tile + N_EXPERTS
    n_slots = n_tiles * tile
    experts = jnp.concatenate([meta[:, 0], meta[:, 1]]).astype(jnp.int32)
    onehot = (experts[:, None] == jnp.arange(N_EXPERTS)[None, :]).astype(jnp.int32)
    csum = jnp.cumsum(onehot, axis=0)
    counts = csum[-1]
    rank = jnp.sum(onehot * csum, axis=1) - 1
    padded = ((counts + tile - 1) // tile) * tile
    ends = jnp.cumsum(padded)
    starts = ends - padded
    pos = jnp.sum(onehot * starts[None, :], axis=1) + rank
    tile_start = jnp.arange(n_tiles, dtype=jnp.int32) * tile
    used = tile_start < ends[-1]
    tile_e = jnp.minimum(jnp.sum((tile_start[:, None] >= ends[None, :]).astype(jnp.int32), axis=1), N_EXPERTS - 1)
    last_e = jnp.max(jnp.where(used, tile_e, 0))
    tile_e = jnp.where(used, tile_e, last_e)
    valid_end = jnp.sum((tile_e[:, None] == jnp.arange(N_EXPERTS)[None, :]) * (starts + counts)[None, :], axis=1)
    n_valid = jnp.where(used, jnp.clip(valid_end - tile_start, 0, tile), 0).astype(jnp.int32)
    return pos.astype(jnp.int32), n_slots, tile_e.astype(jnp.int32), n_valid


def _moe_group_kernel(eid_ref, nval_ref, hs_ref, wg_ref, wu_ref, wd_ref, y_ref, h_scr, acc_scr, *, n_f):
    w = pl.program_id(0)
    f = pl.program_id(1)
    nv = nval_ref[w]

    def run(n_rows):
        rows = slice(0, n_rows)

        @pl.when(f == 0)
        def _():
            hv = _unpack_pairs(hs_ref[rows, :])
            row = lax.broadcasted_iota(jnp.int32, hv.shape, 0)
            h_scr[rows, :] = jnp.where(row < nv, hv, 0.0).astype(BF16)
            acc_scr[rows, :] = jnp.zeros((n_rows, acc_scr.shape[1]), F32)

        h = h_scr[rows, :]
        gate = _dot(h, wg_ref[...].astype(BF16))
        up = _dot(h, wu_ref[...].astype(BF16))
        acc_scr[rows, :] += _dot((_silu(gate) * up).astype(BF16), wd_ref[...].astype(BF16))

        @pl.when(f == n_f - 1)
        def _():
            y_ref[rows, :] = _pack_pairs(acc_scr[rows, :])

    half = hs_ref.shape[0] // 2

    @pl.when(nv > half)
    def _():
        run(hs_ref.shape[0])

    @pl.when((nv > 0) & (nv <= half))
    def _():
        run(half)


def _moe_grouped(hs, tile_e, n_valid, wg, wu, wd):
    n_slots = hs.shape[0]
    d = wg.shape[1]
    d_ff = wg.shape[2]
    tile = MOE_ROW_TILE
    tf = MOE_FF_TILE
    n_f = d_ff // tf

    def f_idx(f, nval, w):
        return jnp.where(nval[w] > 0, f, n_f - 1)

    grid_spec = pltpu.PrefetchScalarGridSpec(
        num_scalar_prefetch=2,
        grid=(n_slots // tile, n_f),
        in_specs=[
            pl.BlockSpec((tile, d // 2), lambda w, f, eid, nval: (w, 0)),
            pl.BlockSpec((None, d, tf), lambda w, f, eid, nval: (eid[w], 0, f_idx(f, nval, w))),
            pl.BlockSpec((None, d, tf), lambda w, f, eid, nval: (eid[w], 0, f_idx(f, nval, w))),
            pl.BlockSpec((None, tf, d), lambda w, f, eid, nval: (eid[w], f_idx(f, nval, w), 0)),
        ],
        out_specs=pl.BlockSpec((tile, d // 2), lambda w, f, eid, nval: (w, 0)),
        scratch_shapes=[pltpu.VMEM((tile, d), BF16), pltpu.VMEM((tile, d), F32)],
    )
    return pl.pallas_call(
        functools.partial(_moe_group_kernel, n_f=n_f),
        grid_spec=grid_spec,
        out_shape=jax.ShapeDtypeStruct((n_slots, d // 2), jnp.int32),
        compiler_params=_cparams("arbitrary", "arbitrary"),
        name="moe_experts",
    )(tile_e, n_valid, hs, wg, wu, wd)


def _moe_out_kernel(x_ref, y1_ref, y2_ref, meta_ref, mod_ref, g3_ref, o_ref, *, tiles_per_mod, mod_base):
    i = pl.program_id(0)
    _, _, gate_f = _mod_rows(mod_ref, i, tiles_per_mod, mod_base, 3)
    meta = meta_ref[...]
    y = meta[:, 2:3] * _unpack_pairs(y1_ref[...]) + meta[:, 3:4] * _unpack_pairs(y2_ref[...])
    o_ref[...] = x_ref[...] + gate_f * _rms(y, g3_ref[...])


def _moe_combine(x, yg, meta, mod, g3, *, rows_per_mod, mod_base):
    rows, d = x.shape
    tm = min(512, rows)
    nt = rows // tm
    kern = functools.partial(_moe_out_kernel, tiles_per_mod=max(rows_per_mod // tm, 1), mod_base=mod_base)
    return pl.pallas_call(
        kern,
        grid=(nt,),
        in_specs=[
            pl.BlockSpec((tm, d), lambda i: (i, 0)),
            pl.BlockSpec((tm, d // 2), lambda i: (i, 0)),
            pl.BlockSpec((tm, d // 2), lambda i: (nt + i, 0)),
            pl.BlockSpec((tm, MOE_META_W), lambda i: (i, 0)),
            pl.BlockSpec(mod.shape, lambda i: (0, 0)),
            pl.BlockSpec((1, d), lambda i: (0, 0)),
        ],
        out_specs=pl.BlockSpec((tm, d), lambda i: (i, 0)),
        out_shape=jax.ShapeDtypeStruct((rows, d), F32),
        compiler_params=_cparams("arbitrary"),
        name="moe_combine",
    )(x, yg, yg, meta, mod, g3.reshape(1, d))


def _moe_sparse(x, h, meta, mod, g3, wg, wu, wd, *, rows_per_mod, mod_base):
    rows = x.shape[0]
    pos, n_slots, tile_e, n_valid = _moe_plan(meta, rows)
    hs = _sc_scatter(h, pos, n_slots)
    ys = _moe_grouped(hs, tile_e, n_valid, wg, wu, wd)
    yg = _sc_gather(ys, pos)
    return _moe_combine(x, yg, meta, mod, g3, rows_per_mod=rows_per_mod, mod_base=mod_base)


def _cast_kernel(w_ref, o_ref, *, scale):
    w = w_ref[...]
    o_ref[...] = (w if scale == 1.0 else w * scale).astype(BF16)


def _cast_bf16(w_stack, layer, scale=1.0):
    squeeze = w_stack.ndim == 3
    w4 = w_stack[:, None] if squeeze else w_stack
    _, n_e, k, n = w4.shape
    bk = min(k, 256)
    out = pl.pallas_call(
        functools.partial(_cast_kernel, scale=scale),
        grid=(n_e, k // bk),
        in_specs=[pl.BlockSpec((None, None, bk, n), lambda e, i: (layer, e, i, 0))],
        out_specs=pl.BlockSpec((None, bk, n), lambda e, i: (e, i, 0)),
        out_shape=jax.ShapeDtypeStruct((n_e, k, n), BF16),
        compiler_params=_cparams("arbitrary", "arbitrary"),
        name="cast_weights",
    )(w4)
    return out[0] if squeeze else out


def _permute_w_in(w_in_stack, layer):
    _, k, n = w_in_stack.shape
    n_blocks = n // BRANCH_W
    shift = 9
    n_gate_blocks = N_BRANCH * D_MODEL // BRANCH_W

    def permute_kernel(w_ref, o_ref):
        scale = jnp.where(pl.program_id(0) < n_gate_blocks, 0.5, 1.0)
        o_ref[...] = (w_ref[...] * scale).astype(BF16)

    return pl.pallas_call(
        permute_kernel,
        grid=(n_blocks,),
        in_specs=[pl.BlockSpec((None, k, BRANCH_W), lambda j: (layer, 0, (j + shift) % n_blocks))],
        out_specs=pl.BlockSpec((k, BRANCH_W), lambda j: (0, j)),
        out_shape=jax.ShapeDtypeStruct((k, n), BF16),
        compiler_params=_cparams("arbitrary"),
        name="cast_permute_w_in",
    )(w_in_stack)


def kernel(x, c, ctx, c_ctx, w_mod, b_mod, norm_g, w_in, s5_a_re, s5_a_im, s5_log_dt, s5_b_re, s5_b_im, s5_c_re, s5_c_im, s5_d, s5_w_glu, s5_b_glu, ret_decay, ret_gn, na_rpb, w_branch, w_out, ffn_w_gate, ffn_w_up, ffn_w_down, moe_w_router, moe_b_router, moe_w_gate, moe_w_up, moe_w_down):
    batch, seq_len, d = x.shape
    ctx_len = ctx.shape[1]
    depth = w_mod.shape[0]
    cond = jnp.concatenate([c, c_ctx[None, :]], axis=0)
    mod_all = _modulation(cond, w_mod, b_mod)
    rope = _rope_tables(seq_len)
    lane_h = np.repeat(np.arange(RET_HEADS), RET_DIM)
    avg = jnp.asarray((lane_h[:, None] == lane_h[None, :]).astype(np.float32) / RET_DIM, BF16)

    xl = x.reshape(batch * seq_len, d)
    xc = ctx.reshape(batch * ctx_len, d)
    lat = dict(rows_per_mod=seq_len, mod_base=0)
    cxt = dict(rows_per_mod=batch * ctx_len, mod_base=batch)

    s5_tabs = jax.vmap(functools.partial(_s5_tables, batch=batch))(
        s5_a_re, s5_a_im, s5_log_dt, s5_b_re, s5_b_im, s5_c_re, s5_c_im, s5_d)
    ret_tabs = jax.vmap(_ret_tables)(ret_decay)
    ret_masks = _ret_masks()
    na_bias = jax.vmap(_na_tables)(na_rpb)
    na_hmask = _na_head_mask()

    for layer in range(depth):
        last = layer == depth - 1
        need_ctx = not last
        mod = mod_all[layer]
        ng = norm_g[layer]
        w_in_bf = _permute_w_in(w_in, layer)
        lw = dict(w_glu=s5_w_glu[layer].astype(BF16), b_glu=s5_b_glu[layer].reshape(1, BRANCH_W).astype(F32),
                  ret_gn=ret_gn[layer].reshape(1, BRANCH_W).astype(F32), avg=avg,
                  w_branch=_cast_bf16(w_branch, layer, 0.5), w_out=_cast_bf16(w_out, layer))

        proj_l, f_l, *s_in_l = _in_proj(xl, mod, ng[0], w_in_bf, **lat)
        proj_c, f_c, *s_in_c = _in_proj(xc, mod, ng[0], w_in_bf, **cxt)

        a_l = _fourier_latent(f_l, batch, seq_len)
        s_l, s_c = _s5_mixer(s_in_l, s_in_c, s5_tabs, layer, batch)
        r_l, r_c = _retention(proj_l, proj_c, ret_tabs, layer, ret_masks, rope, batch, seq_len, ctx_len)
        n_l, n_c = _neighborhood(proj_l, proj_c, na_bias, layer, na_hmask, batch, seq_len, ctx_len, need_ctx)

        xl = _merge(xl, mod, ng[1], proj_l, a_l, s_l, r_l, n_l, lw, **lat)
        if need_ctx:
            a_c = _fourier_ctx(f_c, batch, ctx_len)
            xc = _merge(xc, mod, ng[1], proj_c, a_c, s_c, r_c, n_c, lw, **cxt)

        i = layer // 2
        if layer % 2 == 0:
            wg, wu, wd = _cast_bf16(ffn_w_gate, i), _cast_bf16(ffn_w_up, i), _cast_bf16(ffn_w_down, i)
            xl = _ffn_dense(xl, mod, ng[2], ng[3], wg, wu, wd, **lat)
            if need_ctx:
                xc = _ffn_dense(xc, mod, ng[2], ng[3], wg, wu, wd, **cxt)
        else:
            wg, wu, wd = moe_w_gate[i], moe_w_up[i], moe_w_down[i]
            h, meta = _router(xl, mod, ng[2], moe_w_router[i], moe_b_router[i], **lat)
            xl = _moe_sparse(xl, h, meta, mod, ng[3], wg, wu, wd, **lat)
            if need_ctx:
                hc, metac = _router(xc, mod, ng[2], moe_w_router[i], moe_b_router[i], **cxt)
                xc = _moe_sparse(xc, hc, metac, mod, ng[3], wg, wu, wd, **cxt)
    return xl.reshape(batch, seq_len, d)
```

```python
import functools
import math

import numpy as np
import jax
import jax.numpy as jnp
from jax import lax
from jax.experimental import pallas as pl
from jax.experimental.pallas import tpu as pltpu
from jax.experimental.pallas import tpu_sc as plsc

F32 = jnp.float32
BF16 = jnp.bfloat16

D_MODEL = 1024
BRANCH_W = 256
N_BRANCH = 4
GRID_W = 64
FNET_GROUP_DIM = 64
S5_GROUP_CH = 16
S5_GROUPS = 16
S5_STATE = 64
S5_CHUNK = 32
S5_PAIRS = S5_GROUPS // 2
RET_HEADS = 4
RET_DIM = 64
RET_CHUNK = 128
NA_HEADS = 4
NA_DIM = 64
NA_WIN_ROWS = 8
NA_WIN_COLS = 16
NA_QROWS = 8
ROPE_BASE = 10000.0
N_EXPERTS = 8
EPS = 1e-6
FFT_N2 = 256
NEG_BIG = -1e30
VMEM_LIMIT_BYTES = 50 * 1024 * 1024
SC_CORES = 2
SC_SUBCORES = 16
SC_WORKERS = SC_CORES * SC_SUBCORES
SC_GATHER_ROWS = 64
MOE_ROW_TILE = 1024
MOE_FF_TILE = 512
MOE_META_W = 8

COL_F, COL_S, COL_RQ, COL_RK, COL_RV, COL_RG, COL_NQ, COL_NK, COL_NV = range(16, 25)
IN_W = 9 * BRANCH_W + N_BRANCH * D_MODEL
IN_TN = 1280
IN_F_TILE = (N_BRANCH * D_MODEL) // IN_TN
IN_F_OFF = N_BRANCH * D_MODEL - IN_F_TILE * IN_TN
IN_S_OFF = IN_F_OFF + BRANCH_W


def _cparams(*sem):
    return pltpu.CompilerParams(dimension_semantics=sem, vmem_limit_bytes=VMEM_LIMIT_BYTES)


def _sigmoid(v):
    return 0.5 * jnp.tanh(0.5 * v) + 0.5


def _silu(v):
    return v * _sigmoid(v)


def _gelu_tanh(v):
    return 0.5 * v * (1.0 + jnp.tanh(math.sqrt(2.0 / math.pi) * (v + 0.044715 * (v * v * v))))


def _rms(v, g):
    ms = jnp.mean(v * v, axis=-1, keepdims=True)
    return v * lax.rsqrt(ms + EPS) * g


def _split_bf16(v):
    hi = v.astype(BF16)
    lo = (v - hi.astype(F32)).astype(BF16)
    return hi, lo


def _pack_pairs(v):
    n = v.shape[1] // 2
    lo = lax.bitcast_convert_type(v[:, :n].astype(BF16).astype(F32), jnp.int32)
    hi = lax.bitcast_convert_type(v[:, n:].astype(BF16).astype(F32), jnp.int32)
    return (hi & -65536) | ((lo >> 16) & 65535)


def _unpack_pairs(w):
    lo = lax.bitcast_convert_type(w << 16, F32)
    hi = lax.bitcast_convert_type(w & -65536, F32)
    return jnp.concatenate([lo, hi], axis=-1)


def _dot(a, b):
    return jnp.dot(a, b, preferred_element_type=F32)


def _dot_nt(a, b):
    return lax.dot_general(a, b, (((1,), (1,)), ((), ())), preferred_element_type=F32)


def _dot_tn(a, b):
    return lax.dot_general(a, b, (((0,), (0,)), ((), ())), preferred_element_type=F32)


def _mod_kernel(ct_ref, w_ref, b_ref, o_ref, *, n_cond):
    ct = ct_ref[...]
    s = _silu(ct)
    w = w_ref[...]
    rows = [jnp.sum(w * s[:, r:r + 1], axis=0, keepdims=True) for r in range(n_cond)]
    rows.append(jnp.zeros((8 - n_cond, w.shape[1]), F32))
    o_ref[...] = jnp.concatenate(rows, axis=0) + b_ref[...]


def _modulation(cond, w_mod, b_mod):
    n_layers, d, n = w_mod.shape
    tn = 512
    ct = jnp.zeros((8, d), F32).at[:cond.shape[0]].set(cond).T
    return pl.pallas_call(
        functools.partial(_mod_kernel, n_cond=cond.shape[0]),
        grid=(n_layers, n // tn),
        in_specs=[
            pl.BlockSpec((d, 8), lambda l, j: (0, 0)),
            pl.BlockSpec((None, d, tn), lambda l, j: (l, 0, j)),
            pl.BlockSpec((None, 1, tn), lambda l, j: (l, 0, j)),
        ],
        out_specs=pl.BlockSpec((None, 8, tn), lambda l, j: (l, 0, j)),
        out_shape=jax.ShapeDtypeStruct((n_layers, 8, n), F32),
        compiler_params=_cparams("arbitrary", "arbitrary"),
        name="adaln_mod",
    )(ct, w_mod, b_mod.reshape(n_layers, 1, n))


def _mod_rows(mod_ref, i, tiles_per_mod, mod_base, first):
    r = mod_base + i // tiles_per_mod
    return [mod_ref[pl.ds(r, 1), (first + k) * D_MODEL:(first + k + 1) * D_MODEL] for k in range(3)]


def _in_kernel(x_ref, mod_ref, g_ref, w_ref, proj_ref, f_ref, sa_ref, sb_ref, h_scr, *, tiles_per_mod, mod_base):
    i = pl.program_id(0)
    j = pl.program_id(1)

    @pl.when(j == 0)
    def _():
        sh, sc, _ = _mod_rows(mod_ref, i, tiles_per_mod, mod_base, 0)
        h_scr[...] = (_rms(x_ref[...], g_ref[...]) * (1.0 + sc) + sh).astype(BF16)

    res = _dot(h_scr[...], w_ref[...])
    proj_ref[...] = res.astype(BF16)

    @pl.when(j == IN_F_TILE)
    def _():
        f_ref[...] = res[:, IN_F_OFF:IN_F_OFF + BRANCH_W].astype(BF16)
        sa_ref[...] = res[:, IN_S_OFF:IN_S_OFF + 128]
        sb_ref[...] = res[:, IN_S_OFF + 128:IN_S_OFF + 256]


def _in_proj(x, mod, g, w_bf, *, rows_per_mod, mod_base):
    rows, d = x.shape
    tm = math.gcd(1024, rows_per_mod)
    kern = functools.partial(_in_kernel, tiles_per_mod=max(rows_per_mod // tm, 1), mod_base=mod_base)
    return pl.pallas_call(
        kern,
        grid=(rows // tm, IN_W // IN_TN),
        in_specs=[
            pl.BlockSpec((tm, d), lambda i, j: (i, 0)),
            pl.BlockSpec(mod.shape, lambda i, j: (0, 0)),
            pl.BlockSpec((1, d), lambda i, j: (0, 0)),
            pl.BlockSpec((d, IN_TN), lambda i, j: (0, j)),
        ],
        out_specs=[
            pl.BlockSpec((tm, IN_TN), lambda i, j: (i, j)),
            pl.BlockSpec((tm, BRANCH_W), lambda i, j: (i, 0)),
            pl.BlockSpec((tm, 128), lambda i, j: (i, 0)),
            pl.BlockSpec((tm, 128), lambda i, j: (i, 0)),
        ],
        out_shape=[
            jax.ShapeDtypeStruct((rows, IN_W), BF16),
            jax.ShapeDtypeStruct((rows, BRANCH_W), BF16),
            jax.ShapeDtypeStruct((rows, 128), F32),
            jax.ShapeDtypeStruct((rows, 128), F32),
        ],
        scratch_shapes=[pltpu.VMEM((tm, d), BF16)],
        compiler_params=_cparams("arbitrary", "arbitrary"),
        name="in_proj",
    )(x, mod, g.reshape(1, d), w_bf)


def _fft_a_kernel(x_ref, cs_ref, tc_ref, ts_ref, zr_ref, zi_ref, *, n1, n1p):
    y = _dot(cs_ref[...].astype(BF16), x_ref[...])
    yr = y[:n1]
    yi = y[n1p:n1p + n1]
    tc = tc_ref[...]
    ts = ts_ref[...]
    zr_ref[...] = (yr * tc + yi * ts).astype(BF16)
    zi_ref[...] = (yi * tc - yr * ts).astype(BF16)


def _fft_b_kernel(zr_ref, zi_ref, cs_ref, cc_ref, sc_ref, oa_ref, ob_ref, *, kb, n1, scale, has_imag):
    cs = cs_ref[...].astype(BF16)
    cc = cc_ref[...].astype(BF16)
    sc = sc_ref[...].astype(BF16)
    half = BRANCH_W // 2
    for kk in range(kb):
        a = _dot(cs, zr_ref[kk])
        if has_imag:
            b = _dot(cs, zi_ref[kk])
            xr = a[:FFT_N2] + b[FFT_N2:]
            xi = b[:FFT_N2] - a[FFT_N2:]
        else:
            xr = a[:FFT_N2]
            xi = -a[FFT_N2:]
        out = (_dot(xr.astype(BF16), cc) + _dot(xi.astype(BF16), sc)) * scale
        k1 = pl.program_id(1) * kb + kk
        oa_ref[pl.ds(k1, FFT_N2, stride=n1), :] = out[:, :half]
        ob_ref[pl.ds(k1, FFT_N2, stride=n1), :] = out[:, half:]


def _dft_tables(n):
    k = np.arange(n)
    ang = 2.0 * np.pi * ((k[:, None] * k[None, :]) % n) / n
    return np.cos(ang), np.sin(ang)


def _fft_b_call(zr, zi, n1, batch, seq_len, has_imag):
    c2, s2 = _dft_tables(FFT_N2)
    cs2 = jnp.asarray(np.concatenate([c2, s2], axis=0), F32)
    c64, s64 = _dft_tables(FNET_GROUP_DIM)
    eye = np.eye(BRANCH_W // FNET_GROUP_DIM)
    cc = jnp.asarray(np.kron(eye, c64), F32)
    sc = jnp.asarray(np.kron(eye, s64), F32)
    kb = min(8, n1)
    scale = 1.0 / math.sqrt(seq_len * FNET_GROUP_DIM)
    kern = functools.partial(_fft_b_kernel, kb=kb, n1=n1, scale=scale, has_imag=has_imag)
    zspec = pl.BlockSpec((None, kb, FFT_N2, BRANCH_W), lambda b, i: (b, i, 0, 0))
    half = pl.BlockSpec((seq_len, BRANCH_W // 2), lambda b, i: (b, 0))
    return pl.pallas_call(
        kern,
        grid=(batch, n1 // kb),
        in_specs=[
            zspec, zspec,
            pl.BlockSpec((2 * FFT_N2, FFT_N2), lambda b, i: (0, 0)),
            pl.BlockSpec((BRANCH_W, BRANCH_W), lambda b, i: (0, 0)),
            pl.BlockSpec((BRANCH_W, BRANCH_W), lambda b, i: (0, 0)),
        ],
        out_specs=[half, half],
        out_shape=[jax.ShapeDtypeStruct((batch * seq_len, BRANCH_W // 2), F32)] * 2,
        compiler_params=_cparams("arbitrary", "arbitrary"),
        name="fourier_stage_b",
    )(zr, zi, cs2, cc, sc)


def _fourier_latent(f, batch, seq_len):
    n1 = seq_len // FFT_N2
    wide = FFT_N2 * BRANCH_W
    c1, s1 = _dft_tables(n1)
    n1p = max(n1, 8)
    cs1 = np.zeros((2 * n1p, n1))
    cs1[:n1] = c1
    cs1[n1p:n1p + n1] = -s1
    k1 = np.arange(n1)[:, None]
    l2 = np.arange(FFT_N2)[None, :]
    tw = 2.0 * np.pi * (k1 * l2) / seq_len
    tc = jnp.asarray(np.repeat(np.cos(tw), BRANCH_W, axis=1), F32)
    ts = jnp.asarray(np.repeat(np.sin(tw), BRANCH_W, axis=1), F32)
    cw = min(8192, wide)
    xv = f.reshape(batch, n1, wide)
    spec = pl.BlockSpec((None, n1, cw), lambda b, j: (b, 0, j))
    tspec = pl.BlockSpec((n1, cw), lambda b, j: (0, j))
    zr, zi = pl.pallas_call(
        functools.partial(_fft_a_kernel, n1=n1, n1p=n1p),
        grid=(batch, wide // cw),
        in_specs=[spec, pl.BlockSpec((2 * n1p, n1), lambda b, j: (0, 0)), tspec, tspec],
        out_specs=[spec, spec],
        out_shape=[jax.ShapeDtypeStruct((batch, n1, wide), BF16)] * 2,
        compiler_params=_cparams("arbitrary", "arbitrary"),
        name="fourier_stage_a",
    )(xv, jnp.asarray(cs1, F32), tc, ts)
    zr = zr.reshape(batch, n1, FFT_N2, BRANCH_W)
    zi = zi.reshape(batch, n1, FFT_N2, BRANCH_W)
    return _fft_b_call(zr, zi, n1, batch, seq_len, True)


def _fourier_ctx(f, batch, ctx_len):
    assert ctx_len == FFT_N2
    z = f.reshape(batch, 1, FFT_N2, BRANCH_W)
    return _fft_b_call(z, z, 1, batch, ctx_len, False)


def _s5_tables(a_re, a_im, log_dt, b_re, b_im, c_re, c_im, d_skip, batch):
    t = S5_CHUNK
    g, p, hc = S5_GROUPS, S5_STATE, S5_GROUP_CH
    lam = lax.complex(a_re.astype(F32), a_im.astype(F32))
    dt = jnp.exp(log_dt.astype(F32))[..., None]
    ks = jnp.arange(t + 1, dtype=F32)
    apow = jnp.exp((lam * dt)[..., None] * ks)
    a_bar = apow[..., 1]
    b_bar = ((a_bar - 1.0) / lam)[..., None] * lax.complex(b_re.astype(F32), b_im.astype(F32))
    cm = lax.complex(c_re.astype(F32), c_im.astype(F32))
    kimp = jnp.real(jnp.einsum('dghp,dgpk,dgpj->dgjkh', cm, apow[..., :t], b_bar,
                               precision=lax.Precision.HIGHEST))
    kf, kb = kimp[0], kimp[1]
    kfull = jnp.concatenate([kb[:, :, :0:-1], kf[:, :, :1] + kb[:, :, :1], kf[:, :, 1:]], axis=2)
    kp = kfull.reshape(S5_PAIRS, 2, hc, 2 * t - 1, hc)
    blk = [kp[:, gi] for gi in range(2)]
    zb = jnp.zeros_like(blk[0])
    strip = jnp.concatenate([jnp.stack([blk[0], zb], axis=3), jnp.stack([zb, blk[1]], axis=3)], axis=1)
    strip = strip.reshape(S5_PAIRS, 2 * hc, (2 * t - 1) * 2 * hc)
    strip = jnp.pad(strip, ((0, 0), (0, 0), (0, 2 * hc)))

    wf = jnp.einsum('gpj,gph->gjhp', apow[0][..., t - 1::-1][..., :t], b_bar[0])
    wb = jnp.einsum('gpj,gph->gjhp', apow[1][..., :t], b_bar[1])
    kinds = [jnp.real(wf), jnp.imag(wf), jnp.real(wb), jnp.imag(wb)]

    def we_pair(kd):
        k5 = kd.reshape(S5_PAIRS, 2, t, hc, p)
        z = jnp.zeros_like(k5[:, 0])
        rows = jnp.stack([jnp.concatenate([k5[:, 0], z], axis=-1), jnp.concatenate([z, k5[:, 1]], axis=-1)], axis=2)
        return rows.reshape(S5_PAIRS, 2 * t * hc, 2 * p)

    we = jnp.concatenate([we_pair(kd) for kd in kinds], axis=-1).astype(BF16)

    vf = jnp.einsum('ghp,gpt->gpth', cm[0], apow[0][..., 1:t + 1])
    vb = jnp.einsum('ghp,gpt->gpth', cm[1], apow[1][..., t:0:-1])
    vkinds = [jnp.real(vf), -jnp.imag(vf), jnp.real(vb), -jnp.imag(vb)]

    def v_pair(kd):
        k5 = kd.reshape(S5_PAIRS, 2, p, t, hc)
        z = jnp.zeros_like(k5[:, 0])
        rows = jnp.concatenate([jnp.stack([k5[:, 0], z], axis=3), jnp.stack([z, k5[:, 1]], axis=3)], axis=1)
        return rows.reshape(S5_PAIRS, 2 * p, 2 * t * hc)

    v1 = jnp.concatenate([v_pair(kd) for kd in vkinds], axis=1)
    v = jnp.concatenate([v1, v1], axis=1).astype(BF16)

    def lanes(z):
        return jnp.tile(z.reshape(1, g * p), (1, batch))

    at = apow[..., t]
    a_tab = jnp.concatenate([lanes(jnp.real(at[0])), lanes(jnp.imag(at[0])),
                             lanes(jnp.real(at[1])), lanes(jnp.imag(at[1]))], axis=0)
    dvec = jnp.tile(d_skip.astype(F32).reshape(S5_PAIRS, 1, 2 * hc), (1, t, 1)).reshape(S5_PAIRS, 1, 2 * t * hc)
    return dict(strip=strip, we=we, v=v, a_tab=a_tab, dvec=dvec)


def _s5_pack_kernel(xa_ref, xb_ref, u_ref, *, n_chunks):
    per_half = S5_PAIRS // 2
    for half, x_ref in enumerate((xa_ref, xb_ref)):
        rows = [x_ref[pl.ds(tau, n_chunks, stride=S5_CHUNK), :] for tau in range(S5_CHUNK)]
        for qq in range(per_half):
            pieces = [r[:, qq * 32:(qq + 1) * 32] for r in rows]
            u_ref[half * per_half + qq] = jnp.concatenate(pieces, axis=-1).astype(BF16)


def _s5_unpack_kernel(y_ref, oa_ref, ob_ref, *, n_chunks):
    per_half = S5_PAIRS // 2
    for half, o_ref in enumerate((oa_ref, ob_ref)):
        ys = [y_ref[half * per_half + qq].astype(F32) for qq in range(per_half)]
        for t in range(S5_CHUNK):
            pieces = [y[:, t * 32:(t + 1) * 32] for y in ys]
            o_ref[pl.ds(t, n_chunks, stride=S5_CHUNK), :] = jnp.concatenate(pieces, axis=-1)


def _s5_pack(sa, sb, batch):
    n_chunks = sa.shape[0] // batch // S5_CHUNK
    rows = n_chunks * S5_CHUNK
    cols = 2 * S5_CHUNK * S5_GROUP_CH
    half = pl.BlockSpec((rows, 128), lambda b: (b, 0))
    return pl.pallas_call(
        functools.partial(_s5_pack_kernel, n_chunks=n_chunks),
        grid=(batch,),
        in_specs=[half, half],
        out_specs=pl.BlockSpec((S5_PAIRS, None, n_chunks, cols), lambda b: (0, b, 0, 0)),
        out_shape=jax.ShapeDtypeStruct((S5_PAIRS, batch, n_chunks, cols), BF16),
        compiler_params=_cparams("arbitrary"),
        name="s5_pack",
    )(sa, sb)


def _s5_unpack(y, batch):
    n_chunks = y.shape[2]
    rows = n_chunks * S5_CHUNK
    cols = y.shape[3]
    half = pl.BlockSpec((rows, 128), lambda b: (b, 0))
    return pl.pallas_call(
        functools.partial(_s5_unpack_kernel, n_chunks=n_chunks),
        grid=(batch,),
        in_specs=[pl.BlockSpec((S5_PAIRS, None, n_chunks, cols), lambda b: (0, b, 0, 0))],
        out_specs=[half, half],
        out_shape=[jax.ShapeDtypeStruct((batch * rows, 128), F32)] * 2,
        compiler_params=_cparams("arbitrary"),
        name="s5_unpack",
    )(y)


def _s5_e_kernel(ul_ref, uc_ref, we_ref, ref_, imf_, reb_, imb_):
    u = jnp.concatenate([ul_ref[...], uc_ref[...]], axis=0)
    e = _dot(u, we_ref[...])
    ref_[...] = e[:, 0:128]
    imf_[...] = e[:, 128:256]
    reb_[...] = e[:, 256:384]
    imb_[...] = e[:, 384:512]


def _s5_scan_kernel(a_ref, ref_, imf_, reb_, imb_, prf, pif, prb, pib, *, n_rows, n_ctx):
    afr = a_ref[0:1, :]
    afi = a_ref[1:2, :]
    abr = a_ref[2:3, :]
    abi = a_ref[3:4, :]
    zero = jnp.zeros_like(afr)

    n_lat = n_rows - n_ctx

    def body(s, carry):
        sfr, sfi, sbr, sbi = carry
        nf = jnp.where(s < n_ctx, n_lat + s, s - n_ctx)
        nb = n_rows - 1 - s
        prf[pl.ds(nf, 1), :] = sfr
        pif[pl.ds(nf, 1), :] = sfi
        prb[pl.ds(nb, 1), :] = sbr
        pib[pl.ds(nb, 1), :] = sbi
        efr = ref_[pl.ds(nf, 1), :]
        efi = imf_[pl.ds(nf, 1), :]
        ebr = reb_[pl.ds(nb, 1), :]
        ebi = imb_[pl.ds(nb, 1), :]
        nfr = afr * sfr - afi * sfi + efr
        nfi = afr * sfi + afi * sfr + efi
        nbr = abr * sbr - abi * sbi + ebr
        nbi = abr * sbi + abi * sbr + ebi
        return nfr, nfi, nbr, nbi

    lax.fori_loop(0, n_rows, body, (zero, zero, zero, zero))


def _s5_y_kernel(ul_ref, uc_ref, strip_ref, v_ref, d_ref, prf, pif, prb, pib, yl_ref, yc_ref, m_scr):
    width = 2 * S5_GROUP_CH
    cols = S5_CHUNK * width
    n_lat = yl_ref.shape[0]

    @pl.when(pl.program_id(1) == 0)
    def _():
        strip = strip_ref[...]
        for j in range(S5_CHUNK):
            off = (S5_CHUNK - 1 - j) * width
            win = strip if off == 0 else pltpu.roll(strip, 2 * cols - off, axis=1)
            m_scr[j * width:(j + 1) * width, :] = win[:, :cols].astype(BF16)

    u = jnp.concatenate([ul_ref[...], uc_ref[...]], axis=0)
    y_intra = _dot(u, m_scr[...])
    pcat = jnp.concatenate([prf[...], pif[...], prb[...], pib[...]], axis=-1)
    hi, lo = _split_bf16(pcat)
    y_cross = _dot(jnp.concatenate([hi, lo], axis=-1), v_ref[...])
    y = y_intra + y_cross + d_ref[...] * u.astype(F32)
    yl_ref[...] = y[:n_lat].astype(BF16)
    yc_ref[...] = y[n_lat:].astype(BF16)


def _s5_core(ul, uc, tabs, layer, batch):
    n_lat, n_ctx = ul.shape[2], uc.shape[2]
    n_rows = n_lat + n_ctx
    width = batch * S5_PAIRS * 128
    cols = 2 * S5_CHUNK * S5_GROUP_CH
    ul_spec = pl.BlockSpec((None, None, n_lat, cols), lambda q, b: (q, b, 0, 0))
    uc_spec = pl.BlockSpec((None, None, n_ctx, cols), lambda q, b: (q, b, 0, 0))
    st_spec = pl.BlockSpec((n_rows, 128), lambda q, b: (0, b * S5_PAIRS + q))
    st_shape = jax.ShapeDtypeStruct((n_rows, width), F32)
    e4 = pl.pallas_call(
        _s5_e_kernel,
        grid=(S5_PAIRS, batch),
        in_specs=[ul_spec, uc_spec, pl.BlockSpec((None, None, cols, 512), lambda q, b: (layer, q, 0, 0))],
        out_specs=[st_spec] * 4,
        out_shape=[st_shape] * 4,
        compiler_params=_cparams("arbitrary", "arbitrary"),
        name="s5_chunk_states",
    )(ul, uc, tabs['we'])
    p4 = pl.pallas_call(
        functools.partial(_s5_scan_kernel, n_rows=n_rows, n_ctx=n_ctx),
        out_shape=[st_shape] * 4,
        compiler_params=pltpu.CompilerParams(vmem_limit_bytes=VMEM_LIMIT_BYTES),
        name="s5_state_scan",
    )(tabs['a_tab'][layer], *e4)
    y = pl.pallas_call(
        _s5_y_kernel,
        grid=(S5_PAIRS, batch),
        in_specs=[
            ul_spec, uc_spec,
            pl.BlockSpec((None, None, 2 * S5_GROUP_CH, 2 * cols), lambda q, b: (layer, q, 0, 0)),
            pl.BlockSpec((None, None, cols, cols), lambda q, b: (layer, q, 0, 0)),
            pl.BlockSpec((None, None, 1, cols), lambda q, b: (layer, q, 0, 0)),
            st_spec, st_spec, st_spec, st_spec,
        ],
        out_specs=[ul_spec, uc_spec],
        out_shape=[
            jax.ShapeDtypeStruct((S5_PAIRS, batch, n_lat, cols), BF16),
            jax.ShapeDtypeStruct((S5_PAIRS, batch, n_ctx, cols), BF16),
        ],
        scratch_shapes=[pltpu.VMEM((cols, cols), BF16)],
        compiler_params=_cparams("arbitrary", "arbitrary"),
        name="s5_outputs",
    )(ul, uc, tabs['strip'], tabs['v'], tabs['dvec'], *p4)
    return y


def _s5_mixer(s_lat, s_ctx, tabs, layer, batch):
    ul = _s5_pack(*s_lat, batch)
    uc = _s5_pack(*s_ctx, batch)
    yl, yc = _s5_core(ul, uc, tabs, layer, batch)
    return _s5_unpack(yl, batch), _s5_unpack(yc, batch)


def _ret_tables(ret_decay):
    c = RET_CHUNK
    lg = jax.nn.log_sigmoid(ret_decay.astype(F32))
    lane_h = np.repeat(np.arange(RET_HEADS), RET_DIM)
    lgl = jnp.repeat(lg, RET_DIM, axis=1)
    pos = jnp.arange(c, dtype=F32)[:, None]
    qd = jnp.stack([jnp.exp((pos + 1.0) * lgl[0][None]), jnp.exp((c - pos) * lgl[1][None])])
    kd = jnp.stack([jnp.exp((c - 1.0 - pos) * lgl[0][None]), jnp.exp(pos * lgl[1][None])])
    bmask = jnp.asarray((lane_h[:, None] == lane_h[None, :]).astype(np.float32))
    cd = jnp.exp(c * lgl)[:, :, None] * bmask[None]
    diff = pos - pos.T
    dm = []
    for h in range(RET_HEADS):
        fw = jnp.where(diff >= 0, jnp.exp(jnp.maximum(diff, 0.0) * lg[0, h]), 0.0)
        bw = jnp.where(diff <= 0, jnp.exp(jnp.maximum(-diff, 0.0) * lg[1, h]), 0.0)
        dm.append(fw + bw)
    dm = jnp.concatenate(dm, axis=0)
    return dict(qd=qd, kd=kd, cd=cd, dm=dm)


def _ret_masks():
    lane_h = np.repeat(np.arange(RET_HEADS), RET_DIM)
    bmask = (lane_h[:, None] == lane_h[None, :]).astype(np.float32)
    hmask = (np.arange(RET_HEADS)[:, None] == lane_h[None, :]).astype(np.float32)
    return jnp.asarray(bmask), jnp.asarray(hmask)


def _rope_tables(n_tokens):
    t = np.arange(n_tokens)
    row = (t // GRID_W).astype(np.float64)
    col = (t % GRID_W).astype(np.float64)
    n_freq = RET_DIM // 4
    inv_freq = 1.0 / (ROPE_BASE ** (np.arange(n_freq, dtype=np.float64) / n_freq))
    ang = np.concatenate([row[:, None] * inv_freq, col[:, None] * inv_freq], axis=-1)
    cos = np.cos(ang)
    sin = np.sin(ang)
    cos_t = np.tile(np.concatenate([cos, cos], axis=-1), (1, RET_HEADS))
    sin_t = np.tile(np.concatenate([-sin, sin], axis=-1), (1, RET_HEADS))
    half = RET_DIM // 2
    perm = np.arange(BRANCH_W) ^ half
    swap = np.zeros((BRANCH_W, BRANCH_W), np.float32)
    swap[perm, np.arange(BRANCH_W)] = 1.0
    return jnp.asarray(cos_t, F32), jnp.asarray(sin_t, F32), jnp.asarray(swap, BF16)


def _ret_chunk(q, k, v, s, qd, kd, cd, bmask, dm, hmask, with_intra):
    cross = _dot((q * qd).astype(BF16), s.astype(BF16))
    s_new = cd * s + bmask * _dot_tn((k * kd).astype(BF16), v)
    if not with_intra:
        return cross, s_new
    qb = q.astype(BF16)
    kb = k.astype(BF16)
    qs = jnp.concatenate([qb * hmask[h:h + 1].astype(BF16) for h in range(RET_HEADS)], axis=0)
    scores = _dot_nt(qs, kb) * dm
    ov = _dot(scores.astype(BF16), v)
    c = q.shape[0]
    inner = ov[0:c] * hmask[0:1]
    for h in range(1, RET_HEADS):
        inner = inner + ov[h * c:(h + 1) * c] * hmask[h:h + 1]
    return inner + cross, s_new


def _ret_kernel(qf_ref, kf_ref, vf_ref, qb_ref, kb_ref, vb_ref, qc_ref, kc_ref, vc_ref,
                cosf_ref, sinf_ref, cosb_ref, sinb_ref, swap_ref,
                qd_ref, kd_ref, cd_ref, bm_ref, dm_ref, hm_ref,
                of_ref, ob_ref, ocf_ref, ocb_ref, sf_scr, sb_scr, *, n_chunks, n_ctx_chunks):
    i = pl.program_id(1)
    c = RET_CHUNK
    k_scale = RET_DIM ** -0.5
    bmask = bm_ref[...]
    dm = dm_ref[...]
    hmask = hm_ref[...]
    tabs = [(qd_ref[d], kd_ref[d], cd_ref[d]) for d in range(2)]

    @pl.when(i == 0)
    def _():
        for d, oc_ref, s_scr in ((0, ocf_ref, sf_scr), (1, ocb_ref, sb_scr)):
            qd, kd, cd = tabs[d]
            s = jnp.zeros((BRANCH_W, BRANCH_W), F32)
            order = range(n_ctx_chunks) if d == 0 else range(n_ctx_chunks - 1, -1, -1)
            for cc in order:
                sl = slice(cc * c, (cc + 1) * c)
                o, s = _ret_chunk(qc_ref[sl, :].astype(F32), kc_ref[sl, :].astype(F32) * k_scale, vc_ref[sl, :],
                                  s, qd, kd, cd, bmask, dm, hmask, d == 0)
                oc_ref[sl, :] = o
            s_scr[...] = s

    swap = swap_ref[...]

    def rope(x_ref, cos_ref, sin_ref):
        xb = x_ref[...]
        return xb.astype(F32) * cos_ref[...] + _dot(xb, swap) * sin_ref[...]

    q_f = rope(qf_ref, cosf_ref, sinf_ref)
    k_f = rope(kf_ref, cosf_ref, sinf_ref) * k_scale
    q_b = rope(qb_ref, cosb_ref, sinb_ref)
    k_b = rope(kb_ref, cosb_ref, sinb_ref) * k_scale
    sf = sf_scr[...]
    sb = sb_scr[...]
    for step in range(n_chunks):
        sl = slice(step * c, (step + 1) * c)
        o, sf = _ret_chunk(q_f[sl], k_f[sl], vf_ref[sl, :], sf, *tabs[0], bmask, dm, hmask, True)
        of_ref[sl, :] = o
        cb = n_chunks - 1 - step
        sl = slice(cb * c, (cb + 1) * c)
        o, sb = _ret_chunk(q_b[sl], k_b[sl], vb_ref[sl, :], sb, *tabs[1], bmask, dm, hmask, False)
        ob_ref[sl, :] = o
    sf_scr[...] = sf
    sb_scr[...] = sb


def _retention(proj_l, proj_c, tabs, layer, masks, rope, batch, seq_len, ctx_len):
    n_chunks = 4
    blk = n_chunks * RET_CHUNK
    nblk = seq_len // blk
    cos_t, sin_t, swap = rope

    def lat(col, back):
        if back:
            return pl.BlockSpec((blk, BRANCH_W), lambda b, i: (b * nblk + nblk - 1 - i, col))
        return pl.BlockSpec((blk, BRANCH_W), lambda b, i: (b * nblk + i, col))

    def ctx(col):
        return pl.BlockSpec((ctx_len, BRANCH_W), lambda b, i: (b, col))

    def const(shape):
        return pl.BlockSpec(shape, lambda b, i: (0,) * len(shape))

    def per_layer(shape):
        return pl.BlockSpec((None,) + shape, lambda b, i: (layer,) + (0,) * len(shape))

    tab_f = pl.BlockSpec((blk, BRANCH_W), lambda b, i: (i, 0))
    tab_b = pl.BlockSpec((blk, BRANCH_W), lambda b, i: (nblk - 1 - i, 0))
    kern = functools.partial(_ret_kernel, n_chunks=n_chunks, n_ctx_chunks=ctx_len // RET_CHUNK)
    c = RET_CHUNK
    ctx_out = pl.BlockSpec((ctx_len, BRANCH_W), lambda b, i: (b, 0))
    o_f, o_b, oc_f, oc_b = pl.pallas_call(
        kern,
        grid=(batch, nblk),
        in_specs=[
            lat(COL_RQ, False), lat(COL_RK, False), lat(COL_RV, False),
            lat(COL_RQ, True), lat(COL_RK, True), lat(COL_RV, True),
            ctx(COL_RQ), ctx(COL_RK), ctx(COL_RV),
            tab_f, tab_f, tab_b, tab_b, const((BRANCH_W, BRANCH_W)),
            per_layer((2, c, BRANCH_W)), per_layer((2, c, BRANCH_W)), per_layer((2, BRANCH_W, BRANCH_W)),
            const((BRANCH_W, BRANCH_W)), per_layer((RET_HEADS * c, c)), const((RET_HEADS, BRANCH_W)),
        ],
        out_specs=[lat(0, False), lat(0, True), ctx_out, ctx_out],
        out_shape=[
            jax.ShapeDtypeStruct((batch * seq_len, BRANCH_W), F32),
            jax.ShapeDtypeStruct((batch * seq_len, BRANCH_W), F32),
            jax.ShapeDtypeStruct((batch * ctx_len, BRANCH_W), F32),
            jax.ShapeDtypeStruct((batch * ctx_len, BRANCH_W), F32),
        ],
        scratch_shapes=[pltpu.VMEM((BRANCH_W, BRANCH_W), F32), pltpu.VMEM((BRANCH_W, BRANCH_W), F32)],
        compiler_params=_cparams("arbitrary", "arbitrary"),
        name="retention",
    )(proj_l, proj_l, proj_l, proj_l, proj_l, proj_l, proj_c, proj_c, proj_c,
      cos_t, sin_t, cos_t, sin_t, swap,
      tabs['qd'], tabs['kd'], tabs['cd'], masks[0], tabs['dm'], masks[1])
    return (o_f, o_b), (oc_f, oc_b)


def _na_tables(rpb):
    kr, kw = NA_WIN_ROWS, NA_WIN_COLS
    col = np.arange(GRID_W)
    col_start = np.clip(col - kw // 2, 0, GRID_W - kw)
    in_win = (col[None, :] >= col_start[:, None]) & (col[None, :] < col_start[:, None] + kw)
    dc = np.clip(col[None, :] - col[:, None], -(kw - 1), kw - 1) + (kw - 1)
    pick_c = (dc[:, :, None] == np.arange(2 * kw - 1)[None, None, :]).astype(np.float32)
    by = jnp.einsum('hrc,qkc->hqrk', rpb.astype(F32), jnp.asarray(pick_c), precision=lax.Precision.HIGHEST)
    by = jnp.where(jnp.asarray(in_win)[None, :, None, :], by, NEG_BIG)
    bias = jnp.stack([by[:, :, v:v + kr, :] for v in range(kr)], axis=0)
    return bias.reshape(kr, NA_HEADS * GRID_W, kr * GRID_W)


def _na_head_mask():
    lane_h = np.repeat(np.arange(NA_HEADS), NA_DIM)
    hmask = (np.arange(NA_HEADS)[:, None] == lane_h[None, :]).astype(np.float32)
    return jnp.asarray(hmask, F32)


def _attend(qs, keys, vals, bias, kc, vc):
    s_ctx = _dot_nt(qs, kc)
    m = jnp.max(s_ctx, axis=-1, keepdims=True)
    if keys is not None:
        s_band = _dot_nt(qs, keys) + bias
        m = jnp.maximum(m, jnp.max(s_band, axis=-1, keepdims=True))
        p_band = jnp.exp(s_band - m)
    p_ctx = jnp.exp(s_ctx - m)
    l = jnp.sum(p_ctx, axis=-1, keepdims=True)
    o = _dot(p_ctx.astype(BF16), vc)
    if keys is not None:
        l = l + jnp.sum(p_band, axis=-1, keepdims=True)
        o = o + _dot(p_band.astype(BF16), vals)
    return o / l


def _stack_heads(q, hmask_scaled):
    return jnp.concatenate([q * hmask_scaled[h:h + 1] for h in range(NA_HEADS)], axis=0)


def _unstack_heads(o, hmask, n):
    out = o[0:n] * hmask[0:1]
    for h in range(1, NA_HEADS):
        out = out + o[h * n:(h + 1) * n] * hmask[h:h + 1]
    return out


def _na_kernel(q_ref, k_ref, v_ref, kc_ref, vc_ref, bias_ref, hm_ref, o_ref, *, n_grid_rows):
    i = pl.program_id(1)
    hmask = hm_ref[...]
    hms = (hmask * (NA_DIM ** -0.5)).astype(BF16)
    kc = kc_ref[...]
    vc = vc_ref[...]
    band = NA_WIN_ROWS * GRID_W
    for rr in range(NA_QROWS):
        r = i * NA_QROWS + rr
        rs = jnp.clip(r - NA_WIN_ROWS // 2, 0, n_grid_rows - NA_WIN_ROWS)
        var = rs - r + (NA_WIN_ROWS - 1)
        start = pl.multiple_of(rs * GRID_W, GRID_W)
        keys = k_ref[pl.ds(start, band), :]
        vals = v_ref[pl.ds(start, band), :]
        qs = _stack_heads(q_ref[rr * GRID_W:(rr + 1) * GRID_W, :], hms)
        o = _attend(qs, keys, vals, bias_ref[var], kc, vc)
        o_ref[rr * GRID_W:(rr + 1) * GRID_W, :] = _unstack_heads(o, hmask, GRID_W).astype(BF16)


def _na_ctx_kernel(q_ref, kc_ref, vc_ref, hm_ref, o_ref):
    hmask = hm_ref[...]
    hms = (hmask * (NA_DIM ** -0.5)).astype(BF16)
    n = q_ref.shape[0]
    o = _attend(_stack_heads(q_ref[...], hms), None, None, None, kc_ref[...], vc_ref[...])
    o_ref[...] = _unstack_heads(o, hmask, n).astype(BF16)


def _neighborhood(proj_l, proj_c, bias, layer, hmask, batch, seq_len, ctx_len, need_ctx_out):
    rows = seq_len // GRID_W
    qblk = NA_QROWS * GRID_W
    nq = seq_len // qblk
    out_l = pl.pallas_call(
        functools.partial(_na_kernel, n_grid_rows=rows),
        grid=(batch, nq),
        in_specs=[
            pl.BlockSpec((qblk, BRANCH_W), lambda b, i: (b * nq + i, COL_NQ)),
            pl.BlockSpec((seq_len, BRANCH_W), lambda b, i: (b, COL_NK)),
            pl.BlockSpec((seq_len, BRANCH_W), lambda b, i: (b, COL_NV)),
            pl.BlockSpec((ctx_len, BRANCH_W), lambda b, i: (b, COL_NK)),
            pl.BlockSpec((ctx_len, BRANCH_W), lambda b, i: (b, COL_NV)),
            pl.BlockSpec((None,) + bias.shape[1:], lambda b, i: (layer, 0, 0, 0)),
            pl.BlockSpec(hmask.shape, lambda b, i: (0, 0)),
        ],
        out_specs=pl.BlockSpec((qblk, BRANCH_W), lambda b, i: (b * nq + i, 0)),
        out_shape=jax.ShapeDtypeStruct((batch * seq_len, BRANCH_W), BF16),
        compiler_params=_cparams("arbitrary", "arbitrary"),
        name="neighborhood_attn",
    )(proj_l, proj_l, proj_l, proj_c, proj_c, bias, hmask)
    out_c = None
    if need_ctx_out:
        out_c = pl.pallas_call(
            _na_ctx_kernel,
            grid=(batch,),
            in_specs=[
                pl.BlockSpec((ctx_len, BRANCH_W), lambda b: (b, COL_NQ)),
                pl.BlockSpec((ctx_len, BRANCH_W), lambda b: (b, COL_NK)),
                pl.BlockSpec((ctx_len, BRANCH_W), lambda b: (b, COL_NV)),
                pl.BlockSpec(hmask.shape, lambda b: (0, 0)),
            ],
            out_specs=pl.BlockSpec((ctx_len, BRANCH_W), lambda b: (b, 0)),
            out_shape=jax.ShapeDtypeStruct((batch * ctx_len, BRANCH_W), BF16),
            compiler_params=_cparams("arbitrary"),
            name="context_attn",
        )(proj_c, proj_c, proj_c, hmask)
    return out_l, out_c


def _merge_kernel(x_ref, mod_ref, g_ref, gt0, gt1, gt2, gt3, fa_ref, fb_ref, s5a_ref, s5b_ref, rof_ref, rob_ref, rg_ref, na_ref,
                  wglu_ref, bglu_ref, gn_ref, avg_ref, wb_ref, wo_ref, o_ref, *, tiles_per_mod, mod_base):
    i = pl.program_id(0)
    _, _, gate_a = _mod_rows(mod_ref, i, tiles_per_mod, mod_base, 0)
    z = _gelu_tanh(jnp.concatenate([s5a_ref[...], s5b_ref[...]], axis=-1)).astype(BF16)
    zf = z.astype(F32)
    b_s5 = (zf * _sigmoid(_dot(z, wglu_ref[...]) + bglu_ref[...])).astype(BF16)
    o = rof_ref[...] + rob_ref[...]
    avg = avg_ref[...]
    hi, lo = _split_bf16(o)
    mu = _dot(hi, avg) + _dot(lo, avg)
    dlt = o - mu
    hi, lo = _split_bf16(dlt * dlt)
    var = _dot(hi, avg) + _dot(lo, avg)
    hn = dlt * lax.rsqrt(var + EPS) * gn_ref[...]
    b_ret = (_silu(rg_ref[...].astype(F32)) * hn).astype(BF16)
    b_fnet = jnp.concatenate([fa_ref[...], fb_ref[...]], axis=-1).astype(BF16)
    outs = (b_fnet, b_s5, b_ret, na_ref[...])
    gates = (gt0, gt1, gt2, gt3)
    y = (1.0 + jnp.tanh(gates[0][...].astype(F32))) * _dot(outs[0], wb_ref[0])
    for b in range(1, N_BRANCH):
        y = y + (1.0 + jnp.tanh(gates[b][...].astype(F32))) * _dot(outs[b], wb_ref[b])
    yo = _dot(y.astype(BF16), wo_ref[...])
    o_ref[...] = x_ref[...] + gate_a * _rms(yo, g_ref[...])


def _merge(x, mod, g1, proj, a, s5y, ret_o, na, lw, *, rows_per_mod, mod_base):
    rows, d = x.shape
    tm = min(512, rows)
    nt = rows // tm

    def row(shape, col=0):
        return pl.BlockSpec(shape, lambda i: (i, col))

    def const(arr):
        return pl.BlockSpec(arr.shape, lambda i: (0,) * arr.ndim)

    kern = functools.partial(_merge_kernel, tiles_per_mod=max(rows_per_mod // tm, 1), mod_base=mod_base)
    ins = [x, mod, g1.reshape(1, d), proj, proj, proj, proj, a[0], a[1], s5y[0], s5y[1], ret_o[0], ret_o[1], proj, na,
           lw['w_glu'], lw['b_glu'], lw['ret_gn'], lw['avg'], lw['w_branch'], lw['w_out']]
    specs = [
        row((tm, d)), const(mod), pl.BlockSpec((1, d), lambda i: (0, 0)),
        row((tm, d), 0), row((tm, d), 1), row((tm, d), 2), row((tm, d), 3),
        row((tm, 128)), row((tm, 128)), row((tm, 128)), row((tm, 128)),
        row((tm, BRANCH_W)), row((tm, BRANCH_W)),
        row((tm, BRANCH_W), COL_RG), row((tm, BRANCH_W)),
        const(lw['w_glu']), const(lw['b_glu']), const(lw['ret_gn']), const(lw['avg']),
        const(lw['w_branch']), const(lw['w_out']),
    ]
    return pl.pallas_call(
        kern,
        grid=(nt,),
        in_specs=specs,
        out_specs=row((tm, d)),
        out_shape=jax.ShapeDtypeStruct((rows, d), F32),
        compiler_params=_cparams("arbitrary"),
        name="merge_out",
    )(*ins)


def _ffn_kernel(x_ref, mod_ref, g2_ref, g3_ref, wg_ref, wu_ref, wd_ref, o_ref, h_scr, acc_scr,
                *, tiles_per_mod, mod_base, n_f):
    i = pl.program_id(0)
    f = pl.program_id(1)

    @pl.when(f == 0)
    def _():
        sh, sc, _ = _mod_rows(mod_ref, i, tiles_per_mod, mod_base, 3)
        h_scr[...] = (_rms(x_ref[...], g2_ref[...]) * (1.0 + sc) + sh).astype(BF16)
        acc_scr[...] = jnp.zeros_like(acc_scr)

    h = h_scr[...]
    act = (_silu(_dot(h, wg_ref[...])) * _dot(h, wu_ref[...])).astype(BF16)
    acc_scr[...] += _dot(act, wd_ref[...])

    @pl.when(f == n_f - 1)
    def _():
        _, _, gate_f = _mod_rows(mod_ref, i, tiles_per_mod, mod_base, 3)
        o_ref[...] = x_ref[...] + gate_f * _rms(acc_scr[...], g3_ref[...])


def _ffn_dense(x, mod, g2, g3, wg, wu, wd, *, rows_per_mod, mod_base):
    rows, d = x.shape
    d_ff = wg.shape[1]
    tm = min(512, rows)
    tf = d_ff // 2 if (d_ff // 2) % 128 == 0 else d_ff
    n_f = d_ff // tf
    kern = functools.partial(_ffn_kernel, tiles_per_mod=max(rows_per_mod // tm, 1), mod_base=mod_base, n_f=n_f)
    return pl.pallas_call(
        kern,
        grid=(rows // tm, n_f),
        in_specs=[
            pl.BlockSpec((tm, d), lambda i, f: (i, 0)),
            pl.BlockSpec(mod.shape, lambda i, f: (0, 0)),
            pl.BlockSpec((1, d), lambda i, f: (0, 0)),
            pl.BlockSpec((1, d), lambda i, f: (0, 0)),
            pl.BlockSpec((d, tf), lambda i, f: (0, f)),
            pl.BlockSpec((d, tf), lambda i, f: (0, f)),
            pl.BlockSpec((tf, d), lambda i, f: (f, 0)),
        ],
        out_specs=pl.BlockSpec((tm, d), lambda i, f: (i, 0)),
        out_shape=jax.ShapeDtypeStruct((rows, d), F32),
        scratch_shapes=[pltpu.VMEM((tm, d), BF16), pltpu.VMEM((tm, d), F32)],
        compiler_params=_cparams("arbitrary", "arbitrary"),
        name="ffn_dense",
    )(x, mod, g2.reshape(1, d), g3.reshape(1, d), wg, wu, wd)


def _router_kernel(x_ref, mod_ref, g2_ref, wr_ref, br_ref, h_ref, comb_ref, *, tiles_per_mod, mod_base):
    i = pl.program_id(0)
    sh, sc, _ = _mod_rows(mod_ref, i, tiles_per_mod, mod_base, 3)
    h = _rms(x_ref[...], g2_ref[...]) * (1.0 + sc) + sh
    h_ref[...] = _pack_pairs(h)
    h_hi, h_lo = _split_bf16(h)
    w_hi, w_lo = _split_bf16(wr_ref[...])
    logits = _dot(h_hi, w_hi) + _dot(h_lo, w_hi) + _dot(h_hi, w_lo) + br_ref[...]
    lane = lax.broadcasted_iota(jnp.int32, logits.shape, 1)
    v1 = jnp.max(logits, axis=-1, keepdims=True)
    i1 = jnp.min(jnp.where(logits == v1, lane, 128), axis=-1, keepdims=True)
    rest = jnp.where(lane == i1, NEG_BIG, logits)
    v2 = jnp.max(rest, axis=-1, keepdims=True)
    i2 = jnp.min(jnp.where(rest == v2, lane, 128), axis=-1, keepdims=True)
    e = jnp.exp(v2 - v1)
    w1 = 1.0 / (1.0 + e)
    w2 = e / (1.0 + e)
    meta = jnp.where(lane == 0, i1.astype(F32), 0.0) + jnp.where(lane == 1, i2.astype(F32), 0.0)
    meta = meta + jnp.where(lane == 2, w1, 0.0) + jnp.where(lane == 3, w2, 0.0)
    comb_ref[...] = meta[:, :MOE_META_W]


def _router(x, mod, g2, w_router, b_router, *, rows_per_mod, mod_base):
    rows, d = x.shape
    tm = min(512, rows)
    wr = jnp.zeros((d, 128), F32).at[:, :N_EXPERTS].set(w_router)
    br = jnp.full((1, 128), NEG_BIG, F32).at[0, :N_EXPERTS].set(b_router)
    kern = functools.partial(_router_kernel, tiles_per_mod=max(rows_per_mod // tm, 1), mod_base=mod_base)
    return pl.pallas_call(
        kern,
        grid=(rows // tm,),
        in_specs=[
            pl.BlockSpec((tm, d), lambda i: (i, 0)),
            pl.BlockSpec(mod.shape, lambda i: (0, 0)),
            pl.BlockSpec((1, d), lambda i: (0, 0)),
            pl.BlockSpec((d, 128), lambda i: (0, 0)),
            pl.BlockSpec((1, 128), lambda i: (0, 0)),
        ],
        out_specs=[pl.BlockSpec((tm, d // 2), lambda i: (i, 0)), pl.BlockSpec((tm, MOE_META_W), lambda i: (i, 0))],
        out_shape=[jax.ShapeDtypeStruct((rows, d // 2), jnp.int32), jax.ShapeDtypeStruct((rows, MOE_META_W), F32)],
        compiler_params=_cparams("arbitrary"),
        name="moe_router",
    )(x, mod, g2.reshape(1, d), wr, br)


def _sc_gather(table, idx):
    n_idx = idx.shape[0]
    width = table.shape[1]
    per_worker = n_idx // SC_WORKERS
    chunk_rows = math.gcd(per_worker, SC_GATHER_ROWS)
    n_chunks = per_worker // chunk_rows
    assert per_worker * SC_WORKERS == n_idx and chunk_rows % 8 == 0
    mesh = plsc.VectorSubcoreMesh(core_axis_name="c", subcore_axis_name="s")

    assert n_chunks % 2 == 0
    buf = [pltpu.VMEM((chunk_rows,), jnp.int32), pltpu.VMEM((chunk_rows, width), table.dtype),
           pltpu.SemaphoreType.DMA, pltpu.SemaphoreType.DMA]

    @functools.partial(
        pl.kernel, mesh=mesh,
        out_type=jax.ShapeDtypeStruct((n_idx, width), table.dtype),
        scratch_types=buf + buf,
        name="sc_row_gather",
    )
    def gather(table_hbm, idx_hbm, out_hbm, idx0, rows0, g0, w0, idx1, rows1, g1, w1):
        wid = lax.axis_index("s") * SC_CORES + lax.axis_index("c")
        base = wid * per_worker
        slots = ((idx0, rows0, g0, w0), (idx1, rows1, g1, w1))

        def fetch(j, slot):
            idx_v, rows_v, g, _ = slots[slot]
            pltpu.sync_copy(idx_hbm.at[pl.ds(base + j * chunk_rows, chunk_rows)], idx_v)
            pltpu.make_async_copy(table_hbm.at[idx_v], rows_v, g).start()

        def store(j, slot):
            idx_v, rows_v, g, w = slots[slot]
            pltpu.make_async_copy(table_hbm.at[idx_v], rows_v, g).wait()
            pltpu.make_async_copy(rows_v, out_hbm.at[pl.ds(base + j * chunk_rows, chunk_rows)], w).start()

        def drain(j, slot):
            _, rows_v, _, w = slots[slot]
            pltpu.make_async_copy(rows_v, out_hbm.at[pl.ds(base + j * chunk_rows, chunk_rows)], w).wait()

        fetch(0, 0)

        @pl.loop(0, n_chunks // 2)
        def _(jj):
            j = 2 * jj

            @pl.when(jj > 0)
            def _():
                drain(j - 1, 1)

            fetch(j + 1, 1)
            store(j, 0)

            @pl.when(j + 2 < n_chunks)
            def _():
                drain(j, 0)
                fetch(j + 2, 0)

            store(j + 1, 1)

        drain(n_chunks - 2, 0)
        drain(n_chunks - 1, 1)

    return gather(table, idx)


def _sc_scatter(table, idx, n_out):
    n_idx = idx.shape[0]
    rows, width = table.shape
    per_worker = n_idx // SC_WORKERS
    chunk_rows = math.gcd(per_worker, SC_GATHER_ROWS)
    n_chunks = per_worker // chunk_rows
    assert per_worker * SC_WORKERS == n_idx and chunk_rows % 8 == 0 and rows % per_worker == 0
    mesh = plsc.VectorSubcoreMesh(core_axis_name="c", subcore_axis_name="s")

    assert n_chunks % 2 == 0
    buf = [pltpu.VMEM((chunk_rows,), jnp.int32), pltpu.VMEM((chunk_rows, width), table.dtype),
           pltpu.SemaphoreType.DMA, pltpu.SemaphoreType.DMA]

    @functools.partial(
        pl.kernel, mesh=mesh,
        out_type=jax.ShapeDtypeStruct((n_out, width), table.dtype),
        scratch_types=buf + buf,
        name="sc_row_scatter",
    )
    def scatter(table_hbm, idx_hbm, out_hbm, idx0, rows0, l0, w0, idx1, rows1, l1, w1):
        wid = lax.axis_index("s") * SC_CORES + lax.axis_index("c")
        base = wid * per_worker
        slots = ((idx0, rows0, l0, w0), (idx1, rows1, l1, w1))

        def src(j):
            return table_hbm.at[pl.ds(lax.rem(base + j * chunk_rows, rows), chunk_rows)]

        def fetch(j, slot):
            idx_v, rows_v, l, _ = slots[slot]
            pltpu.sync_copy(idx_hbm.at[pl.ds(base + j * chunk_rows, chunk_rows)], idx_v)
            pltpu.make_async_copy(src(j), rows_v, l).start()

        def store(j, slot):
            idx_v, rows_v, l, w = slots[slot]
            pltpu.make_async_copy(src(j), rows_v, l).wait()
            pltpu.make_async_copy(rows_v, out_hbm.at[idx_v], w).start()

        def drain(slot):
            idx_v, rows_v, _, w = slots[slot]
            pltpu.make_async_copy(rows_v, out_hbm.at[idx_v], w).wait()

        fetch(0, 0)

        @pl.loop(0, n_chunks // 2)
        def _(jj):
            j = 2 * jj

            @pl.when(jj > 0)
            def _():
                drain(1)

            fetch(j + 1, 1)
            store(j, 0)

            @pl.when(j + 2 < n_chunks)
            def _():
                drain(0)
                fetch(j + 2, 0)

            store(j + 1, 1)

        drain(0)
        drain(1)

    return scatter(table, idx)


def _moe_plan(meta, rows):
    tile = MOE_ROW_TILE
    n_tiles = (2 * rows) // tile + N_EXPERTS
    n_slots = n_tiles * tile
    experts = jnp.concatenate([meta[:, 0], meta[:, 1]]).astype(jnp.int32)
    onehot = (experts[:, None] == jnp.arange(N_EXPERTS)[None, :]).astype(jnp.int32)
    csum = jnp.cumsum(onehot, axis=0)
    counts = csum[-1]
    rank = jnp.sum(onehot * csum, axis=1) - 1
    padded = ((counts + tile - 1) // tile) * tile
    ends = jnp.cumsum(padded)
    starts = ends - padded
    pos = jnp.sum(onehot * starts[None, :], axis=1) + rank
    tile_start = jnp.arange(n_tiles, dtype=jnp.int32) * tile
    used = tile_start < ends[-1]
    tile_e = jnp.minimum(jnp.sum((tile_start[:, None] >= ends[None, :]).astype(jnp.int32), axis=1), N_EXPERTS - 1)
    last_e = jnp.max(jnp.where(used, tile_e, 0))
    tile_e = jnp.where(used, tile_e, last_e)
    valid_end = jnp.sum((tile_e[:, None] == jnp.arange(N_EXPERTS)[None, :]) * (starts + counts)[None, :], axis=1)
    n_valid = jnp.where(used, jnp.clip(valid_end - tile_start, 0, tile), 0).astype(jnp.int32)
    return pos.astype(jnp.int32), n_slots, tile_e.astype(jnp.int32), n_valid


def _moe_group_kernel(eid_ref, nval_ref, hs_ref, wg_ref, wu_ref, wd_ref, y_ref, h_scr, acc_scr, *, n_f):
    w = pl.program_id(0)
    f = pl.program_id(1)
    nv = nval_ref[w]

    def run(n_rows):
        rows = slice(0, n_rows)

        @pl.when(f == 0)
        def _():
            hv = _unpack_pairs(hs_ref[rows, :])
            row = lax.broadcasted_iota(jnp.int32, hv.shape, 0)
            h_scr[rows, :] = jnp.where(row < nv, hv, 0.0).astype(BF16)
            acc_scr[rows, :] = jnp.zeros((n_rows, acc_scr.shape[1]), F32)

        h = h_scr[rows, :]
        gate = _dot(h, wg_ref[...].astype(BF16))
        up = _dot(h, wu_ref[...].astype(BF16))
        acc_scr[rows, :] += _dot((_silu(gate) * up).astype(BF16), wd_ref[...].astype(BF16))

        @pl.when(f == n_f - 1)
        def _():
            y_ref[rows, :] = _pack_pairs(acc_scr[rows, :])

    half = hs_ref.shape[0] // 2

    @pl.when(nv > half)
    def _():
        run(hs_ref.shape[0])

    @pl.when((nv > 0) & (nv <= half))
    def _():
        run(half)


def _moe_grouped(hs, tile_e, n_valid, wg, wu, wd):
    n_slots = hs.shape[0]
    d = wg.shape[1]
    d_ff = wg.shape[2]
    tile = MOE_ROW_TILE
    tf = MOE_FF_TILE
    n_f = d_ff // tf

    def f_idx(f, nval, w):
        return jnp.where(nval[w] > 0, f, n_f - 1)

    grid_spec = pltpu.PrefetchScalarGridSpec(
        num_scalar_prefetch=2,
        grid=(n_slots // tile, n_f),
        in_specs=[
            pl.BlockSpec((tile, d // 2), lambda w, f, eid, nval: (w, 0)),
            pl.BlockSpec((None, d, tf), lambda w, f, eid, nval: (eid[w], 0, f_idx(f, nval, w))),
            pl.BlockSpec((None, d, tf), lambda w, f, eid, nval: (eid[w], 0, f_idx(f, nval, w))),
            pl.BlockSpec((None, tf, d), lambda w, f, eid, nval: (eid[w], f_idx(f, nval, w), 0)),
        ],
        out_specs=pl.BlockSpec((tile, d // 2), lambda w, f, eid, nval: (w, 0)),
        scratch_shapes=[pltpu.VMEM((tile, d), BF16), pltpu.VMEM((tile, d), F32)],
    )
    return pl.pallas_call(
        functools.partial(_moe_group_kernel, n_f=n_f),
        grid_spec=grid_spec,
        out_shape=jax.ShapeDtypeStruct((n_slots, d // 2), jnp.int32),
        compiler_params=_cparams("arbitrary", "arbitrary"),
        name="moe_experts",
    )(tile_e, n_valid, hs, wg, wu, wd)


def _moe_out_kernel(x_ref, y1_ref, y2_ref, meta_ref, mod_ref, g3_ref, o_ref, *, tiles_per_mod, mod_base):
    i = pl.program_id(0)
    _, _, gate_f = _mod_rows(mod_ref, i, tiles_per_mod, mod_base, 3)
    meta = meta_ref[...]
    y = meta[:, 2:3] * _unpack_pairs(y1_ref[...]) + meta[:, 3:4] * _unpack_pairs(y2_ref[...])
    o_ref[...] = x_ref[...] + gate_f * _rms(y, g3_ref[...])


def _moe_combine(x, yg, meta, mod, g3, *, rows_per_mod, mod_base):
    rows, d = x.shape
    tm = min(512, rows)
    nt = rows // tm
    kern = functools.partial(_moe_out_kernel, tiles_per_mod=max(rows_per_mod // tm, 1), mod_base=mod_base)
    return pl.pallas_call(
        kern,
        grid=(nt,),
        in_specs=[
            pl.BlockSpec((tm, d), lambda i: (i, 0)),
            pl.BlockSpec((tm, d // 2), lambda i: (i, 0)),
            pl.BlockSpec((tm, d // 2), lambda i: (nt + i, 0)),
            pl.BlockSpec((tm, MOE_META_W), lambda i: (i, 0)),
            pl.BlockSpec(mod.shape, lambda i: (0, 0)),
            pl.BlockSpec((1, d), lambda i: (0, 0)),
        ],
        out_specs=pl.BlockSpec((tm, d), lambda i: (i, 0)),
        out_shape=jax.ShapeDtypeStruct((rows, d), F32),
        compiler_params=_cparams("arbitrary"),
        name="moe_combine",
    )(x, yg, yg, meta, mod, g3.reshape(1, d))


def _moe_sparse(x, h, meta, mod, g3, wg, wu, wd, *, rows_per_mod, mod_base):
    rows = x.shape[0]
    pos, n_slots, tile_e, n_valid = _moe_plan(meta, rows)
    hs = _sc_scatter(h, pos, n_slots)
    ys = _moe_grouped(hs, tile_e, n_valid, wg, wu, wd)
    yg = _sc_gather(ys, pos)
    return _moe_combine(x, yg, meta, mod, g3, rows_per_mod=rows_per_mod, mod_base=mod_base)


def _cast_kernel(w_ref, o_ref, *, scale):
    w = w_ref[...]
    o_ref[...] = (w if scale == 1.0 else w * scale).astype(BF16)


def _cast_bf16(w_stack, layer, scale=1.0):
    squeeze = w_stack.ndim == 3
    w4 = w_stack[:, None] if squeeze else w_stack
    _, n_e, k, n = w4.shape
    bk = min(k, 256)
    out = pl.pallas_call(
        functools.partial(_cast_kernel, scale=scale),
        grid=(n_e, k // bk),
        in_specs=[pl.BlockSpec((None, None, bk, n), lambda e, i: (layer, e, i, 0))],
        out_specs=pl.BlockSpec((None, bk, n), lambda e, i: (e, i, 0)),
        out_shape=jax.ShapeDtypeStruct((n_e, k, n), BF16),
        compiler_params=_cparams("arbitrary", "arbitrary"),
        name="cast_weights",
    )(w4)
    return out[0] if squeeze else out


def _permute_w_in(w_in_stack, layer):
    _, k, n = w_in_stack.shape
    n_blocks = n // BRANCH_W
    shift = 9
    n_gate_blocks = N_BRANCH * D_MODEL // BRANCH_W

    def permute_kernel(w_ref, o_ref):
        scale = jnp.where(pl.program_id(0) < n_gate_blocks, 0.5, 1.0)
        o_ref[...] = (w_ref[...] * scale).astype(BF16)

    return pl.pallas_call(
        permute_kernel,
        grid=(n_blocks,),
        in_specs=[pl.BlockSpec((None, k, BRANCH_W), lambda j: (layer, 0, (j + shift) % n_blocks))],
        out_specs=pl.BlockSpec((k, BRANCH_W), lambda j: (0, j)),
        out_shape=jax.ShapeDtypeStruct((k, n), BF16),
        compiler_params=_cparams("arbitrary"),
        name="cast_permute_w_in",
    )(w_in_stack)


def kernel(x, c, ctx, c_ctx, w_mod, b_mod, norm_g, w_in, s5_a_re, s5_a_im, s5_log_dt, s5_b_re, s5_b_im, s5_c_re, s5_c_im, s5_d, s5_w_glu, s5_b_glu, ret_decay, ret_gn, na_rpb, w_branch, w_out, ffn_w_gate, ffn_w_up, ffn_w_down, moe_w_router, moe_b_router, moe_w_gate, moe_w_up, moe_w_down):
    batch, seq_len, d = x.shape
    ctx_len = ctx.shape[1]
    depth = w_mod.shape[0]
    cond = jnp.concatenate([c, c_ctx[None, :]], axis=0)
    mod_all = _modulation(cond, w_mod, b_mod)
    rope = _rope_tables(seq_len)
    lane_h = np.repeat(np.arange(RET_HEADS), RET_DIM)
    avg = jnp.asarray((lane_h[:, None] == lane_h[None, :]).astype(np.float32) / RET_DIM, BF16)

    xl = x.reshape(batch * seq_len, d)
    xc = ctx.reshape(batch * ctx_len, d)
    lat = dict(rows_per_mod=seq_len, mod_base=0)
    cxt = dict(rows_per_mod=batch * ctx_len, mod_base=batch)

    s5_tabs = jax.vmap(functools.partial(_s5_tables, batch=batch))(
        s5_a_re, s5_a_im, s5_log_dt, s5_b_re, s5_b_im, s5_c_re, s5_c_im, s5_d)
    ret_tabs = jax.vmap(_ret_tables)(ret_decay)
    ret_masks = _ret_masks()
    na_bias = jax.vmap(_na_tables)(na_rpb)
    na_hmask = _na_head_mask()

    for layer in range(depth):
        last = layer == depth - 1
        need_ctx = not last
        mod = mod_all[layer]
        ng = norm_g[layer]
        w_in_bf = _permute_w_in(w_in, layer)
        lw = dict(w_glu=s5_w_glu[layer].astype(BF16), b_glu=s5_b_glu[layer].reshape(1, BRANCH_W).astype(F32),
                  ret_gn=ret_gn[layer].reshape(1, BRANCH_W).astype(F32), avg=avg,
                  w_branch=_cast_bf16(w_branch, layer, 0.5), w_out=_cast_bf16(w_out, layer))

        proj_l, f_l, *s_in_l = _in_proj(xl, mod, ng[0], w_in_bf, **lat)
        proj_c, f_c, *s_in_c = _in_proj(xc, mod, ng[0], w_in_bf, **cxt)

        a_l = _fourier_latent(f_l, batch, seq_len)
        s_l, s_c = _s5_mixer(s_in_l, s_in_c, s5_tabs, layer, batch)
        r_l, r_c = _retention(proj_l, proj_c, ret_tabs, layer, ret_masks, rope, batch, seq_len, ctx_len)
        n_l, n_c = _neighborhood(proj_l, proj_c, na_bias, layer, na_hmask, batch, seq_len, ctx_len, need_ctx)

        xl = _merge(xl, mod, ng[1], proj_l, a_l, s_l, r_l, n_l, lw, **lat)
        if need_ctx:
            a_c = _fourier_ctx(f_c, batch, ctx_len)
            xc = _merge(xc, mod, ng[1], proj_c, a_c, s_c, r_c, n_c, lw, **cxt)

        i = layer // 2
        if layer % 2 == 0:
            wg, wu, wd = _cast_bf16(ffn_w_gate, i), _cast_bf16(ffn_w_up, i), _cast_bf16(ffn_w_down, i)
            xl = _ffn_dense(xl, mod, ng[2], ng[3], wg, wu, wd, **lat)
            if need_ctx:
                xc = _ffn_dense(xc, mod, ng[2], ng[3], wg, wu, wd, **cxt)
        else:
            wg, wu, wd = moe_w_gate[i], moe_w_up[i], moe_w_down[i]
            h, meta = _router(xl, mod, ng[2], moe_w_router[i], moe_b_router[i], **lat)
            xl = _moe_sparse(xl, h, meta, mod, ng[3], wg, wu, wd, **lat)
            if need_ctx:
                hc, metac = _router(xc, mod, ng[2], moe_w_router[i], moe_b_router[i], **cxt)
                xc = _moe_sparse(xc, hc, metac, mod, ng[3], wg, wu, wd, **cxt)
    return xl.reshape(batch, seq_len, d)
```

```python
import functools
import math

import numpy as np
import jax
import jax.numpy as jnp
from jax import lax
from jax.experimental import pallas as pl
from jax.experimental.pallas import tpu as pltpu
from jax.experimental.pallas import tpu_sc as plsc

F32 = jnp.float32
BF16 = jnp.bfloat16

D_MODEL = 1024
BRANCH_W = 256
N_BRANCH = 4
GRID_W = 64
FNET_GROUP_DIM = 64
S5_GROUP_CH = 16
S5_GROUPS = 16
S5_STATE = 64
S5_CHUNK = 32
S5_PAIRS = S5_GROUPS // 2
RET_HEADS = 4
RET_DIM = 64
RET_CHUNK = 128
NA_HEADS = 4
NA_DIM = 64
NA_WIN_ROWS = 8
NA_WIN_COLS = 16
NA_QROWS = 8
ROPE_BASE = 10000.0
N_EXPERTS = 8
EPS = 1e-6
FFT_N2 = 256
NEG_BIG = -1e30
VMEM_LIMIT_BYTES = 50 * 1024 * 1024
SC_CORES = 2
SC_SUBCORES = 16
SC_WORKERS = SC_CORES * SC_SUBCORES
SC_GATHER_ROWS = 64
MOE_ROW_TILE = 1024
MOE_FF_TILE = 512
MOE_META_W = 8

COL_F, COL_S, COL_RQ, COL_RK, COL_RV, COL_RG, COL_NQ, COL_NK, COL_NV = range(16, 25)
IN_W = 9 * BRANCH_W + N_BRANCH * D_MODEL
IN_TN = 1280
IN_F_TILE = (N_BRANCH * D_MODEL) // IN_TN
IN_F_OFF = N_BRANCH * D_MODEL - IN_F_TILE * IN_TN
IN_S_OFF = IN_F_OFF + BRANCH_W


def _cparams(*sem):
    return pltpu.CompilerParams(dimension_semantics=sem, vmem_limit_bytes=VMEM_LIMIT_BYTES)


def _sigmoid(v):
    return 0.5 * jnp.tanh(0.5 * v) + 0.5


def _silu(v):
    return v * _sigmoid(v)


def _gelu_tanh(v):
    return 0.5 * v * (1.0 + jnp.tanh(math.sqrt(2.0 / math.pi) * (v + 0.044715 * (v * v * v))))


def _rms(v, g):
    ms = jnp.mean(v * v, axis=-1, keepdims=True)
    return v * lax.rsqrt(ms + EPS) * g


def _split_bf16(v):
    hi = v.astype(BF16)
    lo = (v - hi.astype(F32)).astype(BF16)
    return hi, lo


def _pack_pairs(v):
    n = v.shape[1] // 2
    lo = lax.bitcast_convert_type(v[:, :n].astype(BF16).astype(F32), jnp.int32)
    hi = lax.bitcast_convert_type(v[:, n:].astype(BF16).astype(F32), jnp.int32)
    return (hi & -65536) | ((lo >> 16) & 65535)


def _unpack_pairs(w):
    lo = lax.bitcast_convert_type(w << 16, F32)
    hi = lax.bitcast_convert_type(w & -65536, F32)
    return jnp.concatenate([lo, hi], axis=-1)


def _dot(a, b):
    return jnp.dot(a, b, preferred_element_type=F32)


def _dot_nt(a, b):
    return lax.dot_general(a, b, (((1,), (1,)), ((), ())), preferred_element_type=F32)


def _dot_tn(a, b):
    return lax.dot_general(a, b, (((0,), (0,)), ((), ())), preferred_element_type=F32)


def _mod_kernel(ct_ref, w_ref, b_ref, o_ref, *, n_cond):
    ct = ct_ref[...]
    s = _silu(ct)
    w = w_ref[...]
    rows = [jnp.sum(w * s[:, r:r + 1], axis=0, keepdims=True) for r in range(n_cond)]
    rows.append(jnp.zeros((8 - n_cond, w.shape[1]), F32))
    o_ref[...] = jnp.concatenate(rows, axis=0) + b_ref[...]


def _modulation(cond, w_mod, b_mod):
    n_layers, d, n = w_mod.shape
    tn = 512
    ct = jnp.zeros((8, d), F32).at[:cond.shape[0]].set(cond).T
    return pl.pallas_call(
        functools.partial(_mod_kernel, n_cond=cond.shape[0]),
        grid=(n_layers, n // tn),
        in_specs=[
            pl.BlockSpec((d, 8), lambda l, j: (0, 0)),
            pl.BlockSpec((None, d, tn), lambda l, j: (l, 0, j)),
            pl.BlockSpec((None, 1, tn), lambda l, j: (l, 0, j)),
        ],
        out_specs=pl.BlockSpec((None, 8, tn), lambda l, j: (l, 0, j)),
        out_shape=jax.ShapeDtypeStruct((n_layers, 8, n), F32),
        compiler_params=_cparams("arbitrary", "arbitrary"),
        name="adaln_mod",
    )(ct, w_mod, b_mod.reshape(n_layers, 1, n))


def _mod_rows(mod_ref, i, tiles_per_mod, mod_base, first):
    r = mod_base + i // tiles_per_mod
    return [mod_ref[pl.ds(r, 1), (first + k) * D_MODEL:(first + k + 1) * D_MODEL] for k in range(3)]


def _in_kernel(x_ref, mod_ref, g_ref, w_ref, proj_ref, f_ref, sa_ref, sb_ref, h_scr, *, tiles_per_mod, mod_base):
    i = pl.program_id(0)
    j = pl.program_id(1)

    @pl.when(j == 0)
    def _():
        sh, sc, _ = _mod_rows(mod_ref, i, tiles_per_mod, mod_base, 0)
        h_scr[...] = (_rms(x_ref[...], g_ref[...]) * (1.0 + sc) + sh).astype(BF16)

    res = _dot(h_scr[...], w_ref[...])
    proj_ref[...] = res.astype(BF16)

    @pl.when(j == IN_F_TILE)
    def _():
        f_ref[...] = res[:, IN_F_OFF:IN_F_OFF + BRANCH_W].astype(BF16)
        sa_ref[...] = res[:, IN_S_OFF:IN_S_OFF + 128]
        sb_ref[...] = res[:, IN_S_OFF + 128:IN_S_OFF + 256]


def _in_proj(x, mod, g, w_bf, *, rows_per_mod, mod_base):
    rows, d = x.shape
    tm = math.gcd(1024, rows_per_mod)
    kern = functools.partial(_in_kernel, tiles_per_mod=max(rows_per_mod // tm, 1), mod_base=mod_base)
    return pl.pallas_call(
        kern,
        grid=(rows // tm, IN_W // IN_TN),
        in_specs=[
            pl.BlockSpec((tm, d), lambda i, j: (i, 0)),
            pl.BlockSpec(mod.shape, lambda i, j: (0, 0)),
            pl.BlockSpec((1, d), lambda i, j: (0, 0)),
            pl.BlockSpec((d, IN_TN), lambda i, j: (0, j)),
        ],
        out_specs=[
            pl.BlockSpec((tm, IN_TN), lambda i, j: (i, j)),
            pl.BlockSpec((tm, BRANCH_W), lambda i, j: (i, 0)),
            pl.BlockSpec((tm, 128), lambda i, j: (i, 0)),
            pl.BlockSpec((tm, 128), lambda i, j: (i, 0)),
        ],
        out_shape=[
            jax.ShapeDtypeStruct((rows, IN_W), BF16),
            jax.ShapeDtypeStruct((rows, BRANCH_W), BF16),
            jax.ShapeDtypeStruct((rows, 128), F32),
            jax.ShapeDtypeStruct((rows, 128), F32),
        ],
        scratch_shapes=[pltpu.VMEM((tm, d), BF16)],
        compiler_params=_cparams("arbitrary", "arbitrary"),
        name="in_proj",
    )(x, mod, g.reshape(1, d), w_bf)


def _fft_a_kernel(x_ref, cs_ref, tc_ref, ts_ref, zr_ref, zi_ref, *, n1, n1p):
    y = _dot(cs_ref[...].astype(BF16), x_ref[...])
    yr = y[:n1]
    yi = y[n1p:n1p + n1]
    tc = tc_ref[...]
    ts = ts_ref[...]
    zr_ref[...] = (yr * tc + yi * ts).astype(BF16)
    zi_ref[...] = (yi * tc - yr * ts).astype(BF16)


def _fft_b_kernel(zr_ref, zi_ref, cs_ref, cc_ref, sc_ref, oa_ref, ob_ref, *, kb, n1, scale, has_imag):
    cs = cs_ref[...].astype(BF16)
    cc = cc_ref[...].astype(BF16)
    sc = sc_ref[...].astype(BF16)
    half = BRANCH_W // 2
    for kk in range(kb):
        a = _dot(cs, zr_ref[kk])
        if has_imag:
            b = _dot(cs, zi_ref[kk])
            xr = a[:FFT_N2] + b[FFT_N2:]
            xi = b[:FFT_N2] - a[FFT_N2:]
        else:
            xr = a[:FFT_N2]
            xi = -a[FFT_N2:]
        out = (_dot(xr.astype(BF16), cc) + _dot(xi.astype(BF16), sc)) * scale
        k1 = pl.program_id(1) * kb + kk
        oa_ref[pl.ds(k1, FFT_N2, stride=n1), :] = out[:, :half]
        ob_ref[pl.ds(k1, FFT_N2, stride=n1), :] = out[:, half:]


def _dft_tables(n):
    k = np.arange(n)
    ang = 2.0 * np.pi * ((k[:, None] * k[None, :]) % n) / n
    return np.cos(ang), np.sin(ang)


def _fft_b_call(zr, zi, n1, batch, seq_len, has_imag):
    c2, s2 = _dft_tables(FFT_N2)
    cs2 = jnp.asarray(np.concatenate([c2, s2], axis=0), F32)
    c64, s64 = _dft_tables(FNET_GROUP_DIM)
    eye = np.eye(BRANCH_W // FNET_GROUP_DIM)
    cc = jnp.asarray(np.kron(eye, c64), F32)
    sc = jnp.asarray(np.kron(eye, s64), F32)
    kb = min(8, n1)
    scale = 1.0 / math.sqrt(seq_len * FNET_GROUP_DIM)
    kern = functools.partial(_fft_b_kernel, kb=kb, n1=n1, scale=scale, has_imag=has_imag)
    zspec = pl.BlockSpec((None, kb, FFT_N2, BRANCH_W), lambda b, i: (b, i, 0, 0))
    half = pl.BlockSpec((seq_len, BRANCH_W // 2), lambda b, i: (b, 0))
    return pl.pallas_call(
        kern,
        grid=(batch, n1 // kb),
        in_specs=[
            zspec, zspec,
            pl.BlockSpec((2 * FFT_N2, FFT_N2), lambda b, i: (0, 0)),
            pl.BlockSpec((BRANCH_W, BRANCH_W), lambda b, i: (0, 0)),
            pl.BlockSpec((BRANCH_W, BRANCH_W), lambda b, i: (0, 0)),
        ],
        out_specs=[half, half],
        out_shape=[jax.ShapeDtypeStruct((batch * seq_len, BRANCH_W // 2), F32)] * 2,
        compiler_params=_cparams("arbitrary", "arbitrary"),
        name="fourier_stage_b",
    )(zr, zi, cs2, cc, sc)


def _fourier_latent(f, batch, seq_len):
    n1 = seq_len // FFT_N2
    wide = FFT_N2 * BRANCH_W
    c1, s1 = _dft_tables(n1)
    n1p = max(n1, 8)
    cs1 = np.zeros((2 * n1p, n1))
    cs1[:n1] = c1
    cs1[n1p:n1p + n1] = -s1
    k1 = np.arange(n1)[:, None]
    l2 = np.arange(FFT_N2)[None, :]
    tw = 2.0 * np.pi * (k1 * l2) / seq_len
    tc = jnp.asarray(np.repeat(np.cos(tw), BRANCH_W, axis=1), F32)
    ts = jnp.asarray(np.repeat(np.sin(tw), BRANCH_W, axis=1), F32)
    cw = min(8192, wide)
    xv = f.reshape(batch, n1, wide)
    spec = pl.BlockSpec((None, n1, cw), lambda b, j: (b, 0, j))
    tspec = pl.BlockSpec((n1, cw), lambda b, j: (0, j))
    zr, zi = pl.pallas_call(
        functools.partial(_fft_a_kernel, n1=n1, n1p=n1p),
        grid=(batch, wide // cw),
        in_specs=[spec, pl.BlockSpec((2 * n1p, n1), lambda b, j: (0, 0)), tspec, tspec],
        out_specs=[spec, spec],
        out_shape=[jax.ShapeDtypeStruct((batch, n1, wide), BF16)] * 2,
        compiler_params=_cparams("arbitrary", "arbitrary"),
        name="fourier_stage_a",
    )(xv, jnp.asarray(cs1, F32), tc, ts)
    zr = zr.reshape(batch, n1, FFT_N2, BRANCH_W)
    zi = zi.reshape(batch, n1, FFT_N2, BRANCH_W)
    return _fft_b_call(zr, zi, n1, batch, seq_len, True)


def _fourier_ctx(f, batch, ctx_len):
    assert ctx_len == FFT_N2
    z = f.reshape(batch, 1, FFT_N2, BRANCH_W)
    return _fft_b_call(z, z, 1, batch, ctx_len, False)


def _s5_tables(a_re, a_im, log_dt, b_re, b_im, c_re, c_im, d_skip, batch):
    t = S5_CHUNK
    g, p, hc = S5_GROUPS, S5_STATE, S5_GROUP_CH
    lam = lax.complex(a_re.astype(F32), a_im.astype(F32))
    dt = jnp.exp(log_dt.astype(F32))[..., None]
    ks = jnp.arange(t + 1, dtype=F32)
    apow = jnp.exp((lam * dt)[..., None] * ks)
    a_bar = apow[..., 1]
    b_bar = ((a_bar - 1.0) / lam)[..., None] * lax.complex(b_re.astype(F32), b_im.astype(F32))
    cm = lax.complex(c_re.astype(F32), c_im.astype(F32))
    kimp = jnp.real(jnp.einsum('dghp,dgpk,dgpj->dgjkh', cm, apow[..., :t], b_bar,
                               precision=lax.Precision.HIGHEST))
    kf, kb = kimp[0], kimp[1]
    kfull = jnp.concatenate([kb[:, :, :0:-1], kf[:, :, :1] + kb[:, :, :1], kf[:, :, 1:]], axis=2)
    kp = kfull.reshape(S5_PAIRS, 2, hc, 2 * t - 1, hc)
    blk = [kp[:, gi] for gi in range(2)]
    zb = jnp.zeros_like(blk[0])
    strip = jnp.concatenate([jnp.stack([blk[0], zb], axis=3), jnp.stack([zb, blk[1]], axis=3)], axis=1)
    strip = strip.reshape(S5_PAIRS, 2 * hc, (2 * t - 1) * 2 * hc)
    strip = jnp.pad(strip, ((0, 0), (0, 0), (0, 2 * hc)))

    wf = jnp.einsum('gpj,gph->gjhp', apow[0][..., t - 1::-1][..., :t], b_bar[0])
    wb = jnp.einsum('gpj,gph->gjhp', apow[1][..., :t], b_bar[1])
    kinds = [jnp.real(wf), jnp.imag(wf), jnp.real(wb), jnp.imag(wb)]

    def we_pair(kd):
        k5 = kd.reshape(S5_PAIRS, 2, t, hc, p)
        z = jnp.zeros_like(k5[:, 0])
        rows = jnp.stack([jnp.concatenate([k5[:, 0], z], axis=-1), jnp.concatenate([z, k5[:, 1]], axis=-1)], axis=2)
        return rows.reshape(S5_PAIRS, 2 * t * hc, 2 * p)

    we = jnp.concatenate([we_pair(kd) for kd in kinds], axis=-1).astype(BF16)

    vf = jnp.einsum('ghp,gpt->gpth', cm[0], apow[0][..., 1:t + 1])
    vb = jnp.einsum('ghp,gpt->gpth', cm[1], apow[1][..., t:0:-1])
    vkinds = [jnp.real(vf), -jnp.imag(vf), jnp.real(vb), -jnp.imag(vb)]

    def v_pair(kd):
        k5 = kd.reshape(S5_PAIRS, 2, p, t, hc)
        z = jnp.zeros_like(k5[:, 0])
        rows = jnp.concatenate([jnp.stack([k5[:, 0], z], axis=3), jnp.stack([z, k5[:, 1]], axis=3)], axis=1)
        return rows.reshape(S5_PAIRS, 2 * p, 2 * t * hc)

    v1 = jnp.concatenate([v_pair(kd) for kd in vkinds], axis=1)
    v = jnp.concatenate([v1, v1], axis=1).astype(BF16)

    def lanes(z):
        return jnp.tile(z.reshape(1, g * p), (1, batch))

    at = apow[..., t]
    a_tab = jnp.concatenate([lanes(jnp.real(at[0])), lanes(jnp.imag(at[0])),
                             lanes(jnp.real(at[1])), lanes(jnp.imag(at[1]))], axis=0)
    dvec = jnp.tile(d_skip.astype(F32).reshape(S5_PAIRS, 1, 2 * hc), (1, t, 1)).reshape(S5_PAIRS, 1, 2 * t * hc)
    return dict(strip=strip, we=we, v=v, a_tab=a_tab, dvec=dvec)


def _s5_pack_kernel(xa_ref, xb_ref, u_ref, *, n_chunks):
    per_half = S5_PAIRS // 2
    for half, x_ref in enumerate((xa_ref, xb_ref)):
        rows = [x_ref[pl.ds(tau, n_chunks, stride=S5_CHUNK), :] for tau in range(S5_CHUNK)]
        for qq in range(per_half):
            pieces = [r[:, qq * 32:(qq + 1) * 32] for r in rows]
            u_ref[half * per_half + qq] = jnp.concatenate(pieces, axis=-1).astype(BF16)


def _s5_unpack_kernel(y_ref, oa_ref, ob_ref, *, n_chunks):
    per_half = S5_PAIRS // 2
    for half, o_ref in enumerate((oa_ref, ob_ref)):
        ys = [y_ref[half * per_half + qq].astype(F32) for qq in range(per_half)]
        for t in range(S5_CHUNK):
            pieces = [y[:, t * 32:(t + 1) * 32] for y in ys]
            o_ref[pl.ds(t, n_chunks, stride=S5_CHUNK), :] = jnp.concatenate(pieces, axis=-1)


def _s5_pack(sa, sb, batch):
    n_chunks = sa.shape[0] // batch // S5_CHUNK
    rows = n_chunks * S5_CHUNK
    cols = 2 * S5_CHUNK * S5_GROUP_CH
    half = pl.BlockSpec((rows, 128), lambda b: (b, 0))
    return pl.pallas_call(
        functools.partial(_s5_pack_kernel, n_chunks=n_chunks),
        grid=(batch,),
        in_specs=[half, half],
        out_specs=pl.BlockSpec((S5_PAIRS, None, n_chunks, cols), lambda b: (0, b, 0, 0)),
        out_shape=jax.ShapeDtypeStruct((S5_PAIRS, batch, n_chunks, cols), BF16),
        compiler_params=_cparams("arbitrary"),
        name="s5_pack",
    )(sa, sb)


def _s5_unpack(y, batch):
    n_chunks = y.shape[2]
    rows = n_chunks * S5_CHUNK
    cols = y.shape[3]
    half = pl.BlockSpec((rows, 128), lambda b: (b, 0))
    return pl.pallas_call(
        functools.partial(_s5_unpack_kernel, n_chunks=n_chunks),
        grid=(batch,),
        in_specs=[pl.BlockSpec((S5_PAIRS, None, n_chunks, cols), lambda b: (0, b, 0, 0))],
        out_specs=[half, half],
        out_shape=[jax.ShapeDtypeStruct((batch * rows, 128), F32)] * 2,
        compiler_params=_cparams("arbitrary"),
        name="s5_unpack",
    )(y)


def _s5_e_kernel(ul_ref, uc_ref, we_ref, ref_, imf_, reb_, imb_):
    u = jnp.concatenate([ul_ref[...], uc_ref[...]], axis=0)
    e = _dot(u, we_ref[...])
    ref_[...] = e[:, 0:128]
    imf_[...] = e[:, 128:256]
    reb_[...] = e[:, 256:384]
    imb_[...] = e[:, 384:512]


def _s5_scan_kernel(a_ref, ref_, imf_, reb_, imb_, prf, pif, prb, pib, *, n_rows, n_ctx):
    afr = a_ref[0:1, :]
    afi = a_ref[1:2, :]
    abr = a_ref[2:3, :]
    abi = a_ref[3:4, :]
    zero = jnp.zeros_like(afr)

    n_lat = n_rows - n_ctx

    def body(s, carry):
        sfr, sfi, sbr, sbi = carry
        nf = jnp.where(s < n_ctx, n_lat + s, s - n_ctx)
        nb = n_rows - 1 - s
        prf[pl.ds(nf, 1), :] = sfr
        pif[pl.ds(nf, 1), :] = sfi
        prb[pl.ds(nb, 1), :] = sbr
        pib[pl.ds(nb, 1), :] = sbi
        efr = ref_[pl.ds(nf, 1), :]
        efi = imf_[pl.ds(nf, 1), :]
        ebr = reb_[pl.ds(nb, 1), :]
        ebi = imb_[pl.ds(nb, 1), :]
        nfr = afr * sfr - afi * sfi + efr
        nfi = afr * sfi + afi * sfr + efi
        nbr = abr * sbr - abi * sbi + ebr
        nbi = abr * sbi + abi * sbr + ebi
        return nfr, nfi, nbr, nbi

    lax.fori_loop(0, n_rows, body, (zero, zero, zero, zero))


def _s5_y_kernel(ul_ref, uc_ref, strip_ref, v_ref, d_ref, prf, pif, prb, pib, yl_ref, yc_ref, m_scr):
    width = 2 * S5_GROUP_CH
    cols = S5_CHUNK * width
    n_lat = yl_ref.shape[0]

    @pl.when(pl.program_id(1) == 0)
    def _():
        strip = strip_ref[...]
        for j in range(S5_CHUNK):
            off = (S5_CHUNK - 1 - j) * width
            win = strip if off == 0 else pltpu.roll(strip, 2 * cols - off, axis=1)
            m_scr[j * width:(j + 1) * width, :] = win[:, :cols].astype(BF16)

    u = jnp.concatenate([ul_ref[...], uc_ref[...]], axis=0)
    y_intra = _dot(u, m_scr[...])
    pcat = jnp.concatenate([prf[...], pif[...], prb[...], pib[...]], axis=-1)
    hi, lo = _split_bf16(pcat)
    y_cross = _dot(jnp.concatenate([hi, lo], axis=-1), v_ref[...])
    y = y_intra + y_cross + d_ref[...] * u.astype(F32)
    yl_ref[...] = y[:n_lat].astype(BF16)
    yc_ref[...] = y[n_lat:].astype(BF16)


def _s5_core(ul, uc, tabs, layer, batch):
    n_lat, n_ctx = ul.shape[2], uc.shape[2]
    n_rows = n_lat + n_ctx
    width = batch * S5_PAIRS * 128
    cols = 2 * S5_CHUNK * S5_GROUP_CH
    ul_spec = pl.BlockSpec((None, None, n_lat, cols), lambda q, b: (q, b, 0, 0))
    uc_spec = pl.BlockSpec((None, None, n_ctx, cols), lambda q, b: (q, b, 0, 0))
    st_spec = pl.BlockSpec((n_rows, 128), lambda q, b: (0, b * S5_PAIRS + q))
    st_shape = jax.ShapeDtypeStruct((n_rows, width), F32)
    e4 = pl.pallas_call(
        _s5_e_kernel,
        grid=(S5_PAIRS, batch),
        in_specs=[ul_spec, uc_spec, pl.BlockSpec((None, None, cols, 512), lambda q, b: (layer, q, 0, 0))],
        out_specs=[st_spec] * 4,
        out_shape=[st_shape] * 4,
        compiler_params=_cparams("arbitrary", "arbitrary"),
        name="s5_chunk_states",
    )(ul, uc, tabs['we'])
    p4 = pl.pallas_call(
        functools.partial(_s5_scan_kernel, n_rows=n_rows, n_ctx=n_ctx),
        out_shape=[st_shape] * 4,
        compiler_params=pltpu.CompilerParams(vmem_limit_bytes=VMEM_LIMIT_BYTES),
        name="s5_state_scan",
    )(tabs['a_tab'][layer], *e4)
    y = pl.pallas_call(
        _s5_y_kernel,
        grid=(S5_PAIRS, batch),
        in_specs=[
            ul_spec, uc_spec,
            pl.BlockSpec((None, None, 2 * S5_GROUP_CH, 2 * cols), lambda q, b: (layer, q, 0, 0)),
            pl.BlockSpec((None, None, cols, cols), lambda q, b: (layer, q, 0, 0)),
            pl.BlockSpec((None, None, 1, cols), lambda q, b: (layer, q, 0, 0)),
            st_spec, st_spec, st_spec, st_spec,
        ],
        out_specs=[ul_spec, uc_spec],
        out_shape=[
            jax.ShapeDtypeStruct((S5_PAIRS, batch, n_lat, cols), BF16),
            jax.ShapeDtypeStruct((S5_PAIRS, batch, n_ctx, cols), BF16),
        ],
        scratch_shapes=[pltpu.VMEM((cols, cols), BF16)],
        compiler_params=_cparams("arbitrary", "arbitrary"),
        name="s5_outputs",
    )(ul, uc, tabs['strip'], tabs['v'], tabs['dvec'], *p4)
    return y


def _s5_mixer(s_lat, s_ctx, tabs, layer, batch):
    ul = _s5_pack(*s_lat, batch)
    uc = _s5_pack(*s_ctx, batch)
    yl, yc = _s5_core(ul, uc, tabs, layer, batch)
    return _s5_unpack(yl, batch), _s5_unpack(yc, batch)


def _ret_tables(ret_decay):
    c = RET_CHUNK
    lg = jax.nn.log_sigmoid(ret_decay.astype(F32))
    lane_h = np.repeat(np.arange(RET_HEADS), RET_DIM)
    lgl = jnp.repeat(lg, RET_DIM, axis=1)
    pos = jnp.arange(c, dtype=F32)[:, None]
    qd = jnp.stack([jnp.exp((pos + 1.0) * lgl[0][None]), jnp.exp((c - pos) * lgl[1][None])])
    kd = jnp.stack([jnp.exp((c - 1.0 - pos) * lgl[0][None]), jnp.exp(pos * lgl[1][None])])
    bmask = jnp.asarray((lane_h[:, None] == lane_h[None, :]).astype(np.float32))
    cd = jnp.exp(c * lgl)[:, :, None] * bmask[None]
    diff = pos - pos.T
    dm = []
    for h in range(RET_HEADS):
        fw = jnp.where(diff >= 0, jnp.exp(jnp.maximum(diff, 0.0) * lg[0, h]), 0.0)
        bw = jnp.where(diff <= 0, jnp.exp(jnp.maximum(-diff, 0.0) * lg[1, h]), 0.0)
        dm.append(fw + bw)
    dm = jnp.concatenate(dm, axis=0)
    return dict(qd=qd, kd=kd, cd=cd, dm=dm)


def _ret_masks():
    lane_h = np.repeat(np.arange(RET_HEADS), RET_DIM)
    bmask = (lane_h[:, None] == lane_h[None, :]).astype(np.float32)
    hmask = (np.arange(RET_HEADS)[:, None] == lane_h[None, :]).astype(np.float32)
    return jnp.asarray(bmask), jnp.asarray(hmask)


def _rope_tables(n_tokens):
    t = np.arange(n_tokens)
    row = (t // GRID_W).astype(np.float64)
    col = (t % GRID_W).astype(np.float64)
    n_freq = RET_DIM // 4
    inv_freq = 1.0 / (ROPE_BASE ** (np.arange(n_freq, dtype=np.float64) / n_freq))
    ang = np.concatenate([row[:, None] * inv_freq, col[:, None] * inv_freq], axis=-1)
    cos = np.cos(ang)
    sin = np.sin(ang)
    cos_t = np.tile(np.concatenate([cos, cos], axis=-1), (1, RET_HEADS))
    sin_t = np.tile(np.concatenate([-sin, sin], axis=-1), (1, RET_HEADS))
    half = RET_DIM // 2
    perm = np.arange(BRANCH_W) ^ half
    swap = np.zeros((BRANCH_W, BRANCH_W), np.float32)
    swap[perm, np.arange(BRANCH_W)] = 1.0
    return jnp.asarray(cos_t, F32), jnp.asarray(sin_t, F32), jnp.asarray(swap, BF16)


def _ret_chunk(q, k, v, s, qd, kd, cd, bmask, dm, hmask, with_intra):
    cross = _dot((q * qd).astype(BF16), s.astype(BF16))
    s_new = cd * s + bmask * _dot_tn((k * kd).astype(BF16), v)
    if not with_intra:
        return cross, s_new
    qb = q.astype(BF16)
    kb = k.astype(BF16)
    qs = jnp.concatenate([qb * hmask[h:h + 1].astype(BF16) for h in range(RET_HEADS)], axis=0)
    scores = _dot_nt(qs, kb) * dm
    ov = _dot(scores.astype(BF16), v)
    c = q.shape[0]
    inner = ov[0:c] * hmask[0:1]
    for h in range(1, RET_HEADS):
        inner = inner + ov[h * c:(h + 1) * c] * hmask[h:h + 1]
    return inner + cross, s_new


def _ret_kernel(qf_ref, kf_ref, vf_ref, qb_ref, kb_ref, vb_ref, qc_ref, kc_ref, vc_ref,
                cosf_ref, sinf_ref, cosb_ref, sinb_ref, swap_ref,
                qd_ref, kd_ref, cd_ref, bm_ref, dm_ref, hm_ref,
                of_ref, ob_ref, ocf_ref, ocb_ref, sf_scr, sb_scr, *, n_chunks, n_ctx_chunks):
    i = pl.program_id(1)
    c = RET_CHUNK
    k_scale = RET_DIM ** -0.5
    bmask = bm_ref[...]
    dm = dm_ref[...]
    hmask = hm_ref[...]
    tabs = [(qd_ref[d], kd_ref[d], cd_ref[d]) for d in range(2)]

    @pl.when(i == 0)
    def _():
        for d, oc_ref, s_scr in ((0, ocf_ref, sf_scr), (1, ocb_ref, sb_scr)):
            qd, kd, cd = tabs[d]
            s = jnp.zeros((BRANCH_W, BRANCH_W), F32)
            order = range(n_ctx_chunks) if d == 0 else range(n_ctx_chunks - 1, -1, -1)
            for cc in order:
                sl = slice(cc * c, (cc + 1) * c)
                o, s = _ret_chunk(qc_ref[sl, :].astype(F32), kc_ref[sl, :].astype(F32) * k_scale, vc_ref[sl, :],
                                  s, qd, kd, cd, bmask, dm, hmask, d == 0)
                oc_ref[sl, :] = o
            s_scr[...] = s

    swap = swap_ref[...]

    def rope(x_ref, cos_ref, sin_ref):
        xb = x_ref[...]
        return xb.astype(F32) * cos_ref[...] + _dot(xb, swap) * sin_ref[...]

    q_f = rope(qf_ref, cosf_ref, sinf_ref)
    k_f = rope(kf_ref, cosf_ref, sinf_ref) * k_scale
    q_b = rope(qb_ref, cosb_ref, sinb_ref)
    k_b = rope(kb_ref, cosb_ref, sinb_ref) * k_scale
    sf = sf_scr[...]
    sb = sb_scr[...]
    for step in range(n_chunks):
        sl = slice(step * c, (step + 1) * c)
        o, sf = _ret_chunk(q_f[sl], k_f[sl], vf_ref[sl, :], sf, *tabs[0], bmask, dm, hmask, True)
        of_ref[sl, :] = o
        cb = n_chunks - 1 - step
        sl = slice(cb * c, (cb + 1) * c)
        o, sb = _ret_chunk(q_b[sl], k_b[sl], vb_ref[sl, :], sb, *tabs[1], bmask, dm, hmask, False)
        ob_ref[sl, :] = o
    sf_scr[...] = sf
    sb_scr[...] = sb


def _retention(proj_l, proj_c, tabs, layer, masks, rope, batch, seq_len, ctx_len):
    n_chunks = 4
    blk = n_chunks * RET_CHUNK
    nblk = seq_len // blk
    cos_t, sin_t, swap = rope

    def lat(col, back):
        if back:
            return pl.BlockSpec((blk, BRANCH_W), lambda b, i: (b * nblk + nblk - 1 - i, col))
        return pl.BlockSpec((blk, BRANCH_W), lambda b, i: (b * nblk + i, col))

    def ctx(col):
        return pl.BlockSpec((ctx_len, BRANCH_W), lambda b, i: (b, col))

    def const(shape):
        return pl.BlockSpec(shape, lambda b, i: (0,) * len(shape))

    def per_layer(shape):
        return pl.BlockSpec((None,) + shape, lambda b, i: (layer,) + (0,) * len(shape))

    tab_f = pl.BlockSpec((blk, BRANCH_W), lambda b, i: (i, 0))
    tab_b = pl.BlockSpec((blk, BRANCH_W), lambda b, i: (nblk - 1 - i, 0))
    kern = functools.partial(_ret_kernel, n_chunks=n_chunks, n_ctx_chunks=ctx_len // RET_CHUNK)
    c = RET_CHUNK
    ctx_out = pl.BlockSpec((ctx_len, BRANCH_W), lambda b, i: (b, 0))
    o_f, o_b, oc_f, oc_b = pl.pallas_call(
        kern,
        grid=(batch, nblk),
        in_specs=[
            lat(COL_RQ, False), lat(COL_RK, False), lat(COL_RV, False),
            lat(COL_RQ, True), lat(COL_RK, True), lat(COL_RV, True),
            ctx(COL_RQ), ctx(COL_RK), ctx(COL_RV),
            tab_f, tab_f, tab_b, tab_b, const((BRANCH_W, BRANCH_W)),
            per_layer((2, c, BRANCH_W)), per_layer((2, c, BRANCH_W)), per_layer((2, BRANCH_W, BRANCH_W)),
            const((BRANCH_W, BRANCH_W)), per_layer((RET_HEADS * c, c)), const((RET_HEADS, BRANCH_W)),
        ],
        out_specs=[lat(0, False), lat(0, True), ctx_out, ctx_out],
        out_shape=[
            jax.ShapeDtypeStruct((batch * seq_len, BRANCH_W), F32),
            jax.ShapeDtypeStruct((batch * seq_len, BRANCH_W), F32),
            jax.ShapeDtypeStruct((batch * ctx_len, BRANCH_W), F32),
            jax.ShapeDtypeStruct((batch * ctx_len, BRANCH_W), F32),
        ],
        scratch_shapes=[pltpu.VMEM((BRANCH_W, BRANCH_W), F32), pltpu.VMEM((BRANCH_W, BRANCH_W), F32)],
        compiler_params=_cparams("arbitrary", "arbitrary"),
        name="retention",
    )(proj_l, proj_l, proj_l, proj_l, proj_l, proj_l, proj_c, proj_c, proj_c,
      cos_t, sin_t, cos_t, sin_t, swap,
      tabs['qd'], tabs['kd'], tabs['cd'], masks[0], tabs['dm'], masks[1])
    return (o_f, o_b), (oc_f, oc_b)


def _na_tables(rpb):
    kr, kw = NA_WIN_ROWS, NA_WIN_COLS
    col = np.arange(GRID_W)
    col_start = np.clip(col - kw // 2, 0, GRID_W - kw)
    in_win = (col[None, :] >= col_start[:, None]) & (col[None, :] < col_start[:, None] + kw)
    dc = np.clip(col[None, :] - col[:, None], -(kw - 1), kw - 1) + (kw - 1)
    pick_c = (dc[:, :, None] == np.arange(2 * kw - 1)[None, None, :]).astype(np.float32)
    by = jnp.einsum('hrc,qkc->hqrk', rpb.astype(F32), jnp.asarray(pick_c), precision=lax.Precision.HIGHEST)
    by = jnp.where(jnp.asarray(in_win)[None, :, None, :], by, NEG_BIG)
    bias = jnp.stack([by[:, :, v:v + kr, :] for v in range(kr)], axis=0)
    return bias.reshape(kr, NA_HEADS * GRID_W, kr * GRID_W)


def _na_head_mask():
    lane_h = np.repeat(np.arange(NA_HEADS), NA_DIM)
    hmask = (np.arange(NA_HEADS)[:, None] == lane_h[None, :]).astype(np.float32)
    return jnp.asarray(hmask, F32)


def _attend(qs, keys, vals, bias, kc, vc):
    s_ctx = _dot_nt(qs, kc)
    m = jnp.max(s_ctx, axis=-1, keepdims=True)
    if keys is not None:
        s_band = _dot_nt(qs, keys) + bias
        m = jnp.maximum(m, jnp.max(s_band, axis=-1, keepdims=True))
        p_band = jnp.exp(s_band - m)
    p_ctx = jnp.exp(s_ctx - m)
    l = jnp.sum(p_ctx, axis=-1, keepdims=True)
    o = _dot(p_ctx.astype(BF16), vc)
    if keys is not None:
        l = l + jnp.sum(p_band, axis=-1, keepdims=True)
        o = o + _dot(p_band.astype(BF16), vals)
    return o / l


def _stack_heads(q, hmask_scaled):
    return jnp.concatenate([q * hmask_scaled[h:h + 1] for h in range(NA_HEADS)], axis=0)


def _unstack_heads(o, hmask, n):
    out = o[0:n] * hmask[0:1]
    for h in range(1, NA_HEADS):
        out = out + o[h * n:(h + 1) * n] * hmask[h:h + 1]
    return out


def _na_kernel(q_ref, k_ref, v_ref, kc_ref, vc_ref, bias_ref, hm_ref, o_ref, *, n_grid_rows):
    i = pl.program_id(1)
    hmask = hm_ref[...]
    hms = (hmask * (NA_DIM ** -0.5)).astype(BF16)
    kc = kc_ref[...]
    vc = vc_ref[...]
    band = NA_WIN_ROWS * GRID_W
    for rr in range(NA_QROWS):
        r = i * NA_QROWS + rr
        rs = jnp.clip(r - NA_WIN_ROWS // 2, 0, n_grid_rows - NA_WIN_ROWS)
        var = rs - r + (NA_WIN_ROWS - 1)
        start = pl.multiple_of(rs * GRID_W, GRID_W)
        keys = k_ref[pl.ds(start, band), :]
        vals = v_ref[pl.ds(start, band), :]
        qs = _stack_heads(q_ref[rr * GRID_W:(rr + 1) * GRID_W, :], hms)
        o = _attend(qs, keys, vals, bias_ref[var], kc, vc)
        o_ref[rr * GRID_W:(rr + 1) * GRID_W, :] = _unstack_heads(o, hmask, GRID_W).astype(BF16)


def _na_ctx_kernel(q_ref, kc_ref, vc_ref, hm_ref, o_ref):
    hmask = hm_ref[...]
    hms = (hmask * (NA_DIM ** -0.5)).astype(BF16)
    n = q_ref.shape[0]
    o = _attend(_stack_heads(q_ref[...], hms), None, None, None, kc_ref[...], vc_ref[...])
    o_ref[...] = _unstack_heads(o, hmask, n).astype(BF16)


def _neighborhood(proj_l, proj_c, bias, layer, hmask, batch, seq_len, ctx_len, need_ctx_out):
    rows = seq_len // GRID_W
    qblk = NA_QROWS * GRID_W
    nq = seq_len // qblk
    out_l = pl.pallas_call(
        functools.partial(_na_kernel, n_grid_rows=rows),
        grid=(batch, nq),
        in_specs=[
            pl.BlockSpec((qblk, BRANCH_W), lambda b, i: (b * nq + i, COL_NQ)),
            pl.BlockSpec((seq_len, BRANCH_W), lambda b, i: (b, COL_NK)),
            pl.BlockSpec((seq_len, BRANCH_W), lambda b, i: (b, COL_NV)),
            pl.BlockSpec((ctx_len, BRANCH_W), lambda b, i: (b, COL_NK)),
            pl.BlockSpec((ctx_len, BRANCH_W), lambda b, i: (b, COL_NV)),
            pl.BlockSpec((None,) + bias.shape[1:], lambda b, i: (layer, 0, 0, 0)),
            pl.BlockSpec(hmask.shape, lambda b, i: (0, 0)),
        ],
        out_specs=pl.BlockSpec((qblk, BRANCH_W), lambda b, i: (b * nq + i, 0)),
        out_shape=jax.ShapeDtypeStruct((batch * seq_len, BRANCH_W), BF16),
        compiler_params=_cparams("arbitrary", "arbitrary"),
        name="neighborhood_attn",
    )(proj_l, proj_l, proj_l, proj_c, proj_c, bias, hmask)
    out_c = None
    if need_ctx_out:
        out_c = pl.pallas_call(
            _na_ctx_kernel,
            grid=(batch,),
            in_specs=[
                pl.BlockSpec((ctx_len, BRANCH_W), lambda b: (b, COL_NQ)),
                pl.BlockSpec((ctx_len, BRANCH_W), lambda b: (b, COL_NK)),
                pl.BlockSpec((ctx_len, BRANCH_W), lambda b: (b, COL_NV)),
                pl.BlockSpec(hmask.shape, lambda b: (0, 0)),
            ],
            out_specs=pl.BlockSpec((ctx_len, BRANCH_W), lambda b: (b, 0)),
            out_shape=jax.ShapeDtypeStruct((batch * ctx_len, BRANCH_W), BF16),
            compiler_params=_cparams("arbitrary"),
            name="context_attn",
        )(proj_c, proj_c, proj_c, hmask)
    return out_l, out_c


def _merge_kernel(x_ref, mod_ref, g_ref, gt0, gt1, gt2, gt3, fa_ref, fb_ref, s5a_ref, s5b_ref, rof_ref, rob_ref, rg_ref, na_ref,
                  wglu_ref, bglu_ref, gn_ref, avg_ref, wb_ref, wo_ref, o_ref, *, tiles_per_mod, mod_base):
    i = pl.program_id(0)
    _, _, gate_a = _mod_rows(mod_ref, i, tiles_per_mod, mod_base, 0)
    z = _gelu_tanh(jnp.concatenate([s5a_ref[...], s5b_ref[...]], axis=-1)).astype(BF16)
    zf = z.astype(F32)
    b_s5 = (zf * _sigmoid(_dot(z, wglu_ref[...]) + bglu_ref[...])).astype(BF16)
    o = rof_ref[...] + rob_ref[...]
    avg = avg_ref[...]
    hi, lo = _split_bf16(o)
    mu = _dot(hi, avg) + _dot(lo, avg)
    dlt = o - mu
    hi, lo = _split_bf16(dlt * dlt)
    var = _dot(hi, avg) + _dot(lo, avg)
    hn = dlt * lax.rsqrt(var + EPS) * gn_ref[...]
    b_ret = (_silu(rg_ref[...].astype(F32)) * hn).astype(BF16)
    b_fnet = jnp.concatenate([fa_ref[...], fb_ref[...]], axis=-1).astype(BF16)
    outs = (b_fnet, b_s5, b_ret, na_ref[...])
    gates = (gt0, gt1, gt2, gt3)
    y = (1.0 + jnp.tanh(gates[0][...].astype(F32))) * _dot(outs[0], wb_ref[0])
    for b in range(1, N_BRANCH):
        y = y + (1.0 + jnp.tanh(gates[b][...].astype(F32))) * _dot(outs[b], wb_ref[b])
    yo = _dot(y.astype(BF16), wo_ref[...])
    o_ref[...] = x_ref[...] + gate_a * _rms(yo, g_ref[...])


def _merge(x, mod, g1, proj, a, s5y, ret_o, na, lw, *, rows_per_mod, mod_base):
    rows, d = x.shape
    tm = min(512, rows)
    nt = rows // tm

    def row(shape, col=0):
        return pl.BlockSpec(shape, lambda i: (i, col))

    def const(arr):
        return pl.BlockSpec(arr.shape, lambda i: (0,) * arr.ndim)

    kern = functools.partial(_merge_kernel, tiles_per_mod=max(rows_per_mod // tm, 1), mod_base=mod_base)
    ins = [x, mod, g1.reshape(1, d), proj, proj, proj, proj, a[0], a[1], s5y[0], s5y[1], ret_o[0], ret_o[1], proj, na,
           lw['w_glu'], lw['b_glu'], lw['ret_gn'], lw['avg'], lw['w_branch'], lw['w_out']]
    specs = [
        row((tm, d)), const(mod), pl.BlockSpec((1, d), lambda i: (0, 0)),
        row((tm, d), 0), row((tm, d), 1), row((tm, d), 2), row((tm, d), 3),
        row((tm, 128)), row((tm, 128)), row((tm, 128)), row((tm, 128)),
        row((tm, BRANCH_W)), row((tm, BRANCH_W)),
        row((tm, BRANCH_W), COL_RG), row((tm, BRANCH_W)),
        const(lw['w_glu']), const(lw['b_glu']), const(lw['ret_gn']), const(lw['avg']),
        const(lw['w_branch']), const(lw['w_out']),
    ]
    return pl.pallas_call(
        kern,
        grid=(nt,),
        in_specs=specs,
        out_specs=row((tm, d)),
        out_shape=jax.ShapeDtypeStruct((rows, d), F32),
        compiler_params=_cparams("arbitrary"),
        name="merge_out",
    )(*ins)


def _ffn_kernel(x_ref, mod_ref, g2_ref, g3_ref, wg_ref, wu_ref, wd_ref, o_ref, h_scr, acc_scr,
                *, tiles_per_mod, mod_base, n_f):
    i = pl.program_id(0)
    f = pl.program_id(1)

    @pl.when(f == 0)
    def _():
        sh, sc, _ = _mod_rows(mod_ref, i, tiles_per_mod, mod_base, 3)
        h_scr[...] = (_rms(x_ref[...], g2_ref[...]) * (1.0 + sc) + sh).astype(BF16)
        acc_scr[...] = jnp.zeros_like(acc_scr)

    h = h_scr[...]
    act = (_silu(_dot(h, wg_ref[...])) * _dot(h, wu_ref[...])).astype(BF16)
    acc_scr[...] += _dot(act, wd_ref[...])

    @pl.when(f == n_f - 1)
    def _():
        _, _, gate_f = _mod_rows(mod_ref, i, tiles_per_mod, mod_base, 3)
        o_ref[...] = x_ref[...] + gate_f * _rms(acc_scr[...], g3_ref[...])


def _ffn_dense(x, mod, g2, g3, wg, wu, wd, *, rows_per_mod, mod_base):
    rows, d = x.shape
    d_ff = wg.shape[1]
    tm = min(512, rows)
    tf = d_ff // 2 if (d_ff // 2) % 128 == 0 else d_ff
    n_f = d_ff // tf
    kern = functools.partial(_ffn_kernel, tiles_per_mod=max(rows_per_mod // tm, 1), mod_base=mod_base, n_f=n_f)
    return pl.pallas_call(
        kern,
        grid=(rows // tm, n_f),
        in_specs=[
            pl.BlockSpec((tm, d), lambda i, f: (i, 0)),
            pl.BlockSpec(mod.shape, lambda i, f: (0, 0)),
            pl.BlockSpec((1, d), lambda i, f: (0, 0)),
            pl.BlockSpec((1, d), lambda i, f: (0, 0)),
            pl.BlockSpec((d, tf), lambda i, f: (0, f)),
            pl.BlockSpec((d, tf), lambda i, f: (0, f)),
            pl.BlockSpec((tf, d), lambda i, f: (f, 0)),
        ],
        out_specs=pl.BlockSpec((tm, d), lambda i, f: (i, 0)),
        out_shape=jax.ShapeDtypeStruct((rows, d), F32),
        scratch_shapes=[pltpu.VMEM((tm, d), BF16), pltpu.VMEM((tm, d), F32)],
        compiler_params=_cparams("arbitrary", "arbitrary"),
        name="ffn_dense",
    )(x, mod, g2.reshape(1, d), g3.reshape(1, d), wg, wu, wd)


def _router_kernel(x_ref, mod_ref, g2_ref, wr_ref, br_ref, tri_ref, h_ref, comb_ref, plan_ref, cnt_ref, cnt_scr,
                   *, tiles_per_mod, mod_base):
    i = pl.program_id(0)

    @pl.when(i == 0)
    def _():
        cnt_scr[...] = jnp.zeros_like(cnt_scr)

    sh, sc, _ = _mod_rows(mod_ref, i, tiles_per_mod, mod_base, 3)
    h = _rms(x_ref[...], g2_ref[...]) * (1.0 + sc) + sh
    h_ref[...] = _pack_pairs(h)
    h_hi, h_lo = _split_bf16(h)
    w_hi, w_lo = _split_bf16(wr_ref[...])
    logits = _dot(h_hi, w_hi) + _dot(h_lo, w_hi) + _dot(h_hi, w_lo) + br_ref[...]
    lane = lax.broadcasted_iota(jnp.int32, logits.shape, 1)
    v1 = jnp.max(logits, axis=-1, keepdims=True)
    i1 = jnp.min(jnp.where(logits == v1, lane, 128), axis=-1, keepdims=True)
    rest = jnp.where(lane == i1, NEG_BIG, logits)
    v2 = jnp.max(rest, axis=-1, keepdims=True)
    i2 = jnp.min(jnp.where(rest == v2, lane, 128), axis=-1, keepdims=True)
    e = jnp.exp(v2 - v1)
    w1 = 1.0 / (1.0 + e)
    w2 = e / (1.0 + e)
    meta = jnp.where(lane == 0, i1.astype(F32), 0.0) + jnp.where(lane == 1, i2.astype(F32), 0.0)
    meta = meta + jnp.where(lane == 2, w1, 0.0) + jnp.where(lane == 3, w2, 0.0)
    member = jnp.where((lane == i1) | (lane == i2), 1.0, 0.0)
    before = _dot(tri_ref[...], member.astype(BF16)) + cnt_scr[...]
    rank1 = jnp.sum(jnp.where(lane == i1, before, 0.0), axis=-1, keepdims=True)
    rank2 = jnp.sum(jnp.where(lane == i2, before, 0.0), axis=-1, keepdims=True)
    cnt_scr[...] += jnp.sum(member, axis=0, keepdims=True)
    cnt_ref[...] = cnt_scr[...]
    meta = meta + jnp.where(lane == 4, rank1, 0.0) + jnp.where(lane == 5, rank2, 0.0)
    comb_ref[...] = meta[:, :MOE_META_W]
    plan_ref[...] = meta.T[:MOE_META_W]


def _router(x, mod, g2, w_router, b_router, *, rows_per_mod, mod_base):
    rows, d = x.shape
    tm = min(512, rows)
    wr = jnp.zeros((d, 128), F32).at[:, :N_EXPERTS].set(w_router)
    br = jnp.full((1, 128), NEG_BIG, F32).at[0, :N_EXPERTS].set(b_router)
    tri = jnp.asarray(np.tril(np.ones((tm, tm), np.float32), -1), BF16)
    kern = functools.partial(_router_kernel, tiles_per_mod=max(rows_per_mod // tm, 1), mod_base=mod_base)
    return pl.pallas_call(
        kern,
        grid=(rows // tm,),
        in_specs=[
            pl.BlockSpec((tm, d), lambda i: (i, 0)),
            pl.BlockSpec(mod.shape, lambda i: (0, 0)),
            pl.BlockSpec((1, d), lambda i: (0, 0)),
            pl.BlockSpec((d, 128), lambda i: (0, 0)),
            pl.BlockSpec((1, 128), lambda i: (0, 0)),
            pl.BlockSpec((tm, tm), lambda i: (0, 0)),
        ],
        out_specs=[
            pl.BlockSpec((tm, d // 2), lambda i: (i, 0)),
            pl.BlockSpec((tm, MOE_META_W), lambda i: (i, 0)),
            pl.BlockSpec((MOE_META_W, tm), lambda i: (0, i)),
            pl.BlockSpec((1, 128), lambda i: (0, 0)),
        ],
        out_shape=[
            jax.ShapeDtypeStruct((rows, d // 2), jnp.int32),
            jax.ShapeDtypeStruct((rows, MOE_META_W), F32),
            jax.ShapeDtypeStruct((MOE_META_W, rows), F32),
            jax.ShapeDtypeStruct((1, 128), F32),
        ],
        scratch_shapes=[pltpu.VMEM((1, 128), F32)],
        compiler_params=_cparams("arbitrary"),
        name="moe_router",
    )(x, mod, g2.reshape(1, d), wr, br, tri)


def _sc_gather(table, idx):
    n_idx = idx.shape[0]
    width = table.shape[1]
    per_worker = n_idx // SC_WORKERS
    chunk_rows = math.gcd(per_worker, SC_GATHER_ROWS)
    n_chunks = per_worker // chunk_rows
    assert per_worker * SC_WORKERS == n_idx and chunk_rows % 8 == 0
    mesh = plsc.VectorSubcoreMesh(core_axis_name="c", subcore_axis_name="s")

    assert n_chunks % 2 == 0
    buf = [pltpu.VMEM((chunk_rows,), jnp.int32), pltpu.VMEM((chunk_rows, width), table.dtype),
           pltpu.SemaphoreType.DMA, pltpu.SemaphoreType.DMA]

    @functools.partial(
        pl.kernel, mesh=mesh,
        out_type=jax.ShapeDtypeStruct((n_idx, width), table.dtype),
        scratch_types=buf + buf,
        name="sc_row_gather",
    )
    def gather(table_hbm, idx_hbm, out_hbm, idx0, rows0, g0, w0, idx1, rows1, g1, w1):
        wid = lax.axis_index("s") * SC_CORES + lax.axis_index("c")
        base = wid * per_worker
        slots = ((idx0, rows0, g0, w0), (idx1, rows1, g1, w1))

        def fetch(j, slot):
            idx_v, rows_v, g, _ = slots[slot]
            pltpu.sync_copy(idx_hbm.at[pl.ds(base + j * chunk_rows, chunk_rows)], idx_v)
            pltpu.make_async_copy(table_hbm.at[idx_v], rows_v, g).start()

        def store(j, slot):
            idx_v, rows_v, g, w = slots[slot]
            pltpu.make_async_copy(table_hbm.at[idx_v], rows_v, g).wait()
            pltpu.make_async_copy(rows_v, out_hbm.at[pl.ds(base + j * chunk_rows, chunk_rows)], w).start()

        def drain(j, slot):
            _, rows_v, _, w = slots[slot]
            pltpu.make_async_copy(rows_v, out_hbm.at[pl.ds(base + j * chunk_rows, chunk_rows)], w).wait()

        fetch(0, 0)

        @pl.loop(0, n_chunks // 2)
        def _(jj):
            j = 2 * jj

            @pl.when(jj > 0)
            def _():
                drain(j - 1, 1)

            fetch(j + 1, 1)
            store(j, 0)

            @pl.when(j + 2 < n_chunks)
            def _():
                drain(j, 0)
                fetch(j + 2, 0)

            store(j + 1, 1)

        drain(n_chunks - 2, 0)
        drain(n_chunks - 1, 1)

    return gather(table, idx)


def _sc_scatter(table, idx, n_out):
    n_idx = idx.shape[0]
    rows, width = table.shape
    per_worker = n_idx // SC_WORKERS
    chunk_rows = math.gcd(per_worker, SC_GATHER_ROWS)
    n_chunks = per_worker // chunk_rows
    assert per_worker * SC_WORKERS == n_idx and chunk_rows % 8 == 0 and rows % per_worker == 0
    mesh = plsc.VectorSubcoreMesh(core_axis_name="c", subcore_axis_name="s")

    assert n_chunks % 2 == 0
    buf = [pltpu.VMEM((chunk_rows,), jnp.int32), pltpu.VMEM((chunk_rows, width), table.dtype),
           pltpu.SemaphoreType.DMA, pltpu.SemaphoreType.DMA]

    @functools.partial(
        pl.kernel, mesh=mesh,
        out_type=jax.ShapeDtypeStruct((n_out, width), table.dtype),
        scratch_types=buf + buf,
        name="sc_row_scatter",
    )
    def scatter(table_hbm, idx_hbm, out_hbm, idx0, rows0, l0, w0, idx1, rows1, l1, w1):
        wid = lax.axis_index("s") * SC_CORES + lax.axis_index("c")
        base = wid * per_worker
        slots = ((idx0, rows0, l0, w0), (idx1, rows1, l1, w1))

        def src(j):
            return table_hbm.at[pl.ds(lax.rem(base + j * chunk_rows, rows), chunk_rows)]

        def fetch(j, slot):
            idx_v, rows_v, l, _ = slots[slot]
            pltpu.sync_copy(idx_hbm.at[pl.ds(base + j * chunk_rows, chunk_rows)], idx_v)
            pltpu.make_async_copy(src(j), rows_v, l).start()

        def store(j, slot):
            idx_v, rows_v, l, w = slots[slot]
            pltpu.make_async_copy(src(j), rows_v, l).wait()
            pltpu.make_async_copy(rows_v, out_hbm.at[idx_v], w).start()

        def drain(slot):
            idx_v, rows_v, _, w = slots[slot]
            pltpu.make_async_copy(rows_v, out_hbm.at[idx_v], w).wait()

        fetch(0, 0)

        @pl.loop(0, n_chunks // 2)
        def _(jj):
            j = 2 * jj

            @pl.when(jj > 0)
            def _():
                drain(1)

            fetch(j + 1, 1)
            store(j, 0)

            @pl.when(j + 2 < n_chunks)
            def _():
                drain(0)
                fetch(j + 2, 0)

            store(j + 1, 1)

        drain(0)
        drain(1)

    return scatter(table, idx)


def _moe_plan(plan, counts_row, rows):
    tile = MOE_ROW_TILE
    n_tiles = (2 * rows) // tile + N_EXPERTS
    n_slots = n_tiles * tile
    counts = counts_row[0, :N_EXPERTS].astype(jnp.int32)
    padded = ((counts + tile - 1) // tile) * tile
    ends = jnp.cumsum(padded)
    starts = ends - padded
    ids = jnp.arange(N_EXPERTS, dtype=F32)[:, None]
    start_f = starts.astype(F32)[:, None]

    def slot(e_row, r_row):
        return jnp.sum(jnp.where(e_row[None, :] == ids, start_f, 0.0), axis=0) + r_row

    pos = jnp.concatenate([slot(plan[0], plan[4]), slot(plan[1], plan[5])])
    tile_start = jnp.arange(n_tiles, dtype=jnp.int32) * tile
    used = tile_start < ends[-1]
    tile_e = jnp.minimum(jnp.sum((tile_start[:, None] >= ends[None, :]).astype(jnp.int32), axis=1), N_EXPERTS - 1)
    last_e = jnp.max(jnp.where(used, tile_e, 0))
    tile_e = jnp.where(used, tile_e, last_e)
    valid_end = jnp.sum((tile_e[:, None] == jnp.arange(N_EXPERTS)[None, :]) * (starts + counts)[None, :], axis=1)
    n_valid = jnp.where(used, jnp.clip(valid_end - tile_start, 0, tile), 0).astype(jnp.int32)
    return pos.astype(jnp.int32), n_slots, tile_e.astype(jnp.int32), n_valid


def _moe_group_kernel(eid_ref, nval_ref, hs_ref, wg_ref, wu_ref, wd_ref, y_ref, h_scr, acc_scr, *, n_f):
    w = pl.program_id(0)
    f = pl.program_id(1)
    nv = nval_ref[w]

    def run(n_rows):
        rows = slice(0, n_rows)

        @pl.when(f == 0)
        def _():
            hv = _unpack_pairs(hs_ref[rows, :])
            row = lax.broadcasted_iota(jnp.int32, hv.shape, 0)
            h_scr[rows, :] = jnp.where(row < nv, hv, 0.0).astype(BF16)
            acc_scr[rows, :] = jnp.zeros((n_rows, acc_scr.shape[1]), F32)

        h = h_scr[rows, :]
        gate = _dot(h, wg_ref[...].astype(BF16))
        up = _dot(h, wu_ref[...].astype(BF16))
        acc_scr[rows, :] += _dot((_silu(gate) * up).astype(BF16), wd_ref[...].astype(BF16))

        @pl.when(f == n_f - 1)
        def _():
            y_ref[rows, :] = _pack_pairs(acc_scr[rows, :])

    half = hs_ref.shape[0] // 2

    @pl.when(nv > half)
    def _():
        run(hs_ref.shape[0])

    @pl.when((nv > 0) & (nv <= half))
    def _():
        run(half)


def _moe_grouped(hs, tile_e, n_valid, wg, wu, wd):
    n_slots = hs.shape[0]
    d = wg.shape[1]
    d_ff = wg.shape[2]
    tile = MOE_ROW_TILE
    tf = MOE_FF_TILE
    n_f = d_ff // tf

    def f_idx(f, nval, w):
        return jnp.where(nval[w] > 0, f, n_f - 1)

    grid_spec = pltpu.PrefetchScalarGridSpec(
        num_scalar_prefetch=2,
        grid=(n_slots // tile, n_f),
        in_specs=[
            pl.BlockSpec((tile, d // 2), lambda w, f, eid, nval: (w, 0)),
            pl.BlockSpec((None, d, tf), lambda w, f, eid, nval: (eid[w], 0, f_idx(f, nval, w))),
            pl.BlockSpec((None, d, tf), lambda w, f, eid, nval: (eid[w], 0, f_idx(f, nval, w))),
            pl.BlockSpec((None, tf, d), lambda w, f, eid, nval: (eid[w], f_idx(f, nval, w), 0)),
        ],
        out_specs=pl.BlockSpec((tile, d // 2), lambda w, f, eid, nval: (w, 0)),
        scratch_shapes=[pltpu.VMEM((tile, d), BF16), pltpu.VMEM((tile, d), F32)],
    )
    return pl.pallas_call(
        functools.partial(_moe_group_kernel, n_f=n_f),
        grid_spec=grid_spec,
        out_shape=jax.ShapeDtypeStruct((n_slots, d // 2), jnp.int32),
        compiler_params=_cparams("arbitrary", "arbitrary"),
        name="moe_experts",
    )(tile_e, n_valid, hs, wg, wu, wd)


def _moe_out_kernel(x_ref, y1_ref, y2_ref, meta_ref, mod_ref, g3_ref, o_ref, *, tiles_per_mod, mod_base):
    i = pl.program_id(0)
    _, _, gate_f = _mod_rows(mod_ref, i, tiles_per_mod, mod_base, 3)
    meta = meta_ref[...]
    y = meta[:, 2:3] * _unpack_pairs(y1_ref[...]) + meta[:, 3:4] * _unpack_pairs(y2_ref[...])
    o_ref[...] = x_ref[...] + gate_f * _rms(y, g3_ref[...])


def _moe_combine(x, yg, meta, mod, g3, *, rows_per_mod, mod_base):
    rows, d = x.shape
    tm = min(512, rows)
    nt = rows // tm
    kern = functools.partial(_moe_out_kernel, tiles_per_mod=max(rows_per_mod // tm, 1), mod_base=mod_base)
    return pl.pallas_call(
        kern,
        grid=(nt,),
        in_specs=[
            pl.BlockSpec((tm, d), lambda i: (i, 0)),
            pl.BlockSpec((tm, d // 2), lambda i: (i, 0)),
            pl.BlockSpec((tm, d // 2), lambda i: (nt + i, 0)),
            pl.BlockSpec((tm, MOE_META_W), lambda i: (i, 0)),
            pl.BlockSpec(mod.shape, lambda i: (0, 0)),
            pl.BlockSpec((1, d), lambda i: (0, 0)),
        ],
        out_specs=pl.BlockSpec((tm, d), lambda i: (i, 0)),
        out_shape=jax.ShapeDtypeStruct((rows, d), F32),
        compiler_params=_cparams("arbitrary"),
        name="moe_combine",
    )(x, yg, yg, meta, mod, g3.reshape(1, d))


def _moe_sparse(x, routed, mod, g3, wg, wu, wd, *, rows_per_mod, mod_base):
    h, meta, plan, counts = routed
    rows = x.shape[0]
    pos, n_slots, tile_e, n_valid = _moe_plan(plan, counts, rows)
    hs = _sc_scatter(h, pos, n_slots)
    ys = _moe_grouped(hs, tile_e, n_valid, wg, wu, wd)
    yg = _sc_gather(ys, pos)
    return _moe_combine(x, yg, meta, mod, g3, rows_per_mod=rows_per_mod, mod_base=mod_base)


def _cast_kernel(w_ref, o_ref, *, scale):
    w = w_ref[...]
    o_ref[...] = (w if scale == 1.0 else w * scale).astype(BF16)


def _cast_bf16(w_stack, layer, scale=1.0):
    squeeze = w_stack.ndim == 3
    w4 = w_stack[:, None] if squeeze else w_stack
    _, n_e, k, n = w4.shape
    bk = min(k, 256)
    out = pl.pallas_call(
        functools.partial(_cast_kernel, scale=scale),
        grid=(n_e, k // bk),
        in_specs=[pl.BlockSpec((None, None, bk, n), lambda e, i: (layer, e, i, 0))],
        out_specs=pl.BlockSpec((None, bk, n), lambda e, i: (e, i, 0)),
        out_shape=jax.ShapeDtypeStruct((n_e, k, n), BF16),
        compiler_params=_cparams("arbitrary", "arbitrary"),
        name="cast_weights",
    )(w4)
    return out[0] if squeeze else out


def _permute_w_in(w_in_stack, layer):
    _, k, n = w_in_stack.shape
    n_blocks = n // BRANCH_W
    shift = 9
    n_gate_blocks = N_BRANCH * D_MODEL // BRANCH_W

    per_step = 5
    assert n_blocks % per_step == 0

    def permute_kernel(*refs):
        o_ref = refs[-1]
        for s, w_ref in enumerate(refs[:-1]):
            scale = jnp.where(pl.program_id(0) * per_step + s < n_gate_blocks, 0.5, 1.0)
            o_ref[:, s * BRANCH_W:(s + 1) * BRANCH_W] = (w_ref[...] * scale).astype(BF16)

    def src(s):
        return pl.BlockSpec((None, k, BRANCH_W), lambda j: (layer, 0, (j * per_step + s + shift) % n_blocks))

    return pl.pallas_call(
        permute_kernel,
        grid=(n_blocks // per_step,),
        in_specs=[src(s) for s in range(per_step)],
        out_specs=pl.BlockSpec((k, per_step * BRANCH_W), lambda j: (0, j)),
        out_shape=jax.ShapeDtypeStruct((k, n), BF16),
        compiler_params=_cparams("arbitrary"),
        name="cast_permute_w_in",
    )(*([w_in_stack] * per_step))


def kernel(x, c, ctx, c_ctx, w_mod, b_mod, norm_g, w_in, s5_a_re, s5_a_im, s5_log_dt, s5_b_re, s5_b_im, s5_c_re, s5_c_im, s5_d, s5_w_glu, s5_b_glu, ret_decay, ret_gn, na_rpb, w_branch, w_out, ffn_w_gate, ffn_w_up, ffn_w_down, moe_w_router, moe_b_router, moe_w_gate, moe_w_up, moe_w_down):
    batch, seq_len, d = x.shape
    ctx_len = ctx.shape[1]
    depth = w_mod.shape[0]
    cond = jnp.concatenate([c, c_ctx[None, :]], axis=0)
    mod_all = _modulation(cond, w_mod, b_mod)
    rope = _rope_tables(seq_len)
    lane_h = np.repeat(np.arange(RET_HEADS), RET_DIM)
    avg = jnp.asarray((lane_h[:, None] == lane_h[None, :]).astype(np.float32) / RET_DIM, BF16)

    xl = x.reshape(batch * seq_len, d)
    xc = ctx.reshape(batch * ctx_len, d)
    lat = dict(rows_per_mod=seq_len, mod_base=0)
    cxt = dict(rows_per_mod=batch * ctx_len, mod_base=batch)

    s5_tabs = jax.vmap(functools.partial(_s5_tables, batch=batch))(
        s5_a_re, s5_a_im, s5_log_dt, s5_b_re, s5_b_im, s5_c_re, s5_c_im, s5_d)
    ret_tabs = jax.vmap(_ret_tables)(ret_decay)
    ret_masks = _ret_masks()
    na_bias = jax.vmap(_na_tables)(na_rpb)
    na_hmask = _na_head_mask()

    for layer in range(depth):
        last = layer == depth - 1
        need_ctx = not last
        mod = mod_all[layer]
        ng = norm_g[layer]
        w_in_bf = _permute_w_in(w_in, layer)
        lw = dict(w_glu=s5_w_glu[layer].astype(BF16), b_glu=s5_b_glu[layer].reshape(1, BRANCH_W).astype(F32),
                  ret_gn=ret_gn[layer].reshape(1, BRANCH_W).astype(F32), avg=avg,
                  w_branch=_cast_bf16(w_branch, layer, 0.5), w_out=_cast_bf16(w_out, layer))

        proj_l, f_l, *s_in_l = _in_proj(xl, mod, ng[0], w_in_bf, **lat)
        proj_c, f_c, *s_in_c = _in_proj(xc, mod, ng[0], w_in_bf, **cxt)

        a_l = _fourier_latent(f_l, batch, seq_len)
        s_l, s_c = _s5_mixer(s_in_l, s_in_c, s5_tabs, layer, batch)
        r_l, r_c = _retention(proj_l, proj_c, ret_tabs, layer, ret_masks, rope, batch, seq_len, ctx_len)
        n_l, n_c = _neighborhood(proj_l, proj_c, na_bias, layer, na_hmask, batch, seq_len, ctx_len, need_ctx)

        xl = _merge(xl, mod, ng[1], proj_l, a_l, s_l, r_l, n_l, lw, **lat)
        if need_ctx:
            a_c = _fourier_ctx(f_c, batch, ctx_len)
            xc = _merge(xc, mod, ng[1], proj_c, a_c, s_c, r_c, n_c, lw, **cxt)

        i = layer // 2
        if layer % 2 == 0:
            wg, wu, wd = _cast_bf16(ffn_w_gate, i), _cast_bf16(ffn_w_up, i), _cast_bf16(ffn_w_down, i)
            xl = _ffn_dense(xl, mod, ng[2], ng[3], wg, wu, wd, **lat)
            if need_ctx:
                xc = _ffn_dense(xc, mod, ng[2], ng[3], wg, wu, wd, **cxt)
        else:
            wg, wu, wd = moe_w_gate[i], moe_w_up[i], moe_w_down[i]
            routed = _router(xl, mod, ng[2], moe_w_router[i], moe_b_router[i], **lat)
            xl = _moe_sparse(xl, routed, mod, ng[3], wg, wu, wd, **lat)
            if need_ctx:
                routed_c = _router(xc, mod, ng[2], moe_w_router[i], moe_b_router[i], **cxt)
                xc = _moe_sparse(xc, routed_c, mod, ng[3], wg, wu, wd, **cxt)
    return xl.reshape(batch, seq_len, d)
```

```python
import functools
import math

import numpy as np
import jax
import jax.numpy as jnp
from jax import lax
from jax.experimental import pallas as pl
from jax.experimental.pallas import tpu as pltpu
from jax.experimental.pallas import tpu_sc as plsc

F32 = jnp.float32
BF16 = jnp.bfloat16

D_MODEL = 1024
BRANCH_W = 256
N_BRANCH = 4
GRID_W = 64
FNET_GROUP_DIM = 64
S5_GROUP_CH = 16
S5_GROUPS = 16
S5_STATE = 64
S5_CHUNK = 32
S5_PAIRS = S5_GROUPS // 2
RET_HEADS = 4
RET_DIM = 64
RET_CHUNK = 128
NA_HEADS = 4
NA_DIM = 64
NA_WIN_ROWS = 8
NA_WIN_COLS = 16
NA_QROWS = 8
ROPE_BASE = 10000.0
N_EXPERTS = 8
EPS = 1e-6
FFT_N2 = 256
NEG_BIG = -1e30
VMEM_LIMIT_BYTES = 50 * 1024 * 1024
SC_CORES = 2
SC_SUBCORES = 16
SC_WORKERS = SC_CORES * SC_SUBCORES
SC_GATHER_ROWS = 64
MOE_ROW_TILE = 1024
MOE_FF_TILE = 512
MOE_META_W = 8

COL_F, COL_S, COL_RQ, COL_RK, COL_RV, COL_RG, COL_NQ, COL_NK, COL_NV = range(16, 25)
IN_W = 9 * BRANCH_W + N_BRANCH * D_MODEL
IN_TN = 1280
IN_F_TILE = (N_BRANCH * D_MODEL) // IN_TN
IN_F_OFF = N_BRANCH * D_MODEL - IN_F_TILE * IN_TN
IN_S_OFF = IN_F_OFF + BRANCH_W


def _cparams(*sem):
    return pltpu.CompilerParams(dimension_semantics=sem, vmem_limit_bytes=VMEM_LIMIT_BYTES)


def _sigmoid(v):
    return 0.5 * jnp.tanh(0.5 * v) + 0.5


def _silu(v):
    return v * _sigmoid(v)


def _gelu_tanh(v):
    return 0.5 * v * (1.0 + jnp.tanh(math.sqrt(2.0 / math.pi) * (v + 0.044715 * (v * v * v))))


def _rms(v, g):
    ms = jnp.mean(v * v, axis=-1, keepdims=True)
    return v * lax.rsqrt(ms + EPS) * g


def _split_bf16(v):
    hi = v.astype(BF16)
    lo = (v - hi.astype(F32)).astype(BF16)
    return hi, lo


def _pack_pairs(v):
    n = v.shape[1] // 2
    lo = lax.bitcast_convert_type(v[:, :n].astype(BF16).astype(F32), jnp.int32)
    hi = lax.bitcast_convert_type(v[:, n:].astype(BF16).astype(F32), jnp.int32)
    return (hi & -65536) | ((lo >> 16) & 65535)


def _unpack_pairs(w):
    lo = lax.bitcast_convert_type(w << 16, F32)
    hi = lax.bitcast_convert_type(w & -65536, F32)
    return jnp.concatenate([lo, hi], axis=-1)


def _dot(a, b):
    return jnp.dot(a, b, preferred_element_type=F32)


def _dot_nt(a, b):
    return lax.dot_general(a, b, (((1,), (1,)), ((), ())), preferred_element_type=F32)


def _dot_tn(a, b):
    return lax.dot_general(a, b, (((0,), (0,)), ((), ())), preferred_element_type=F32)


def _mod_kernel(ct_ref, w_ref, b_ref, o_ref, *, n_cond):
    ct = ct_ref[...]
    s = _silu(ct)
    w = w_ref[...]
    rows = [jnp.sum(w * s[:, r:r + 1], axis=0, keepdims=True) for r in range(n_cond)]
    rows.append(jnp.zeros((8 - n_cond, w.shape[1]), F32))
    o_ref[...] = jnp.concatenate(rows, axis=0) + b_ref[...]


def _modulation(cond, w_mod, b_mod):
    n_layers, d, n = w_mod.shape
    tn = 512
    ct = jnp.zeros((8, d), F32).at[:cond.shape[0]].set(cond).T
    return pl.pallas_call(
        functools.partial(_mod_kernel, n_cond=cond.shape[0]),
        grid=(n_layers, n // tn),
        in_specs=[
            pl.BlockSpec((d, 8), lambda l, j: (0, 0)),
            pl.BlockSpec((None, d, tn), lambda l, j: (l, 0, j)),
            pl.BlockSpec((None, 1, tn), lambda l, j: (l, 0, j)),
        ],
        out_specs=pl.BlockSpec((None, 8, tn), lambda l, j: (l, 0, j)),
        out_shape=jax.ShapeDtypeStruct((n_layers, 8, n), F32),
        compiler_params=_cparams("arbitrary", "arbitrary"),
        name="adaln_mod",
    )(ct, w_mod, b_mod.reshape(n_layers, 1, n))


def _mod_rows(mod_ref, i, tiles_per_mod, mod_base, first):
    r = mod_base + i // tiles_per_mod
    return [mod_ref[pl.ds(r, 1), (first + k) * D_MODEL:(first + k + 1) * D_MODEL] for k in range(3)]


def _in_kernel(x_ref, mod_ref, g_ref, w_ref, proj_ref, f_ref, sa_ref, sb_ref, *, tiles_per_mod, mod_base):
    i = pl.program_id(0)
    sh, sc, _ = _mod_rows(mod_ref, i, tiles_per_mod, mod_base, 0)
    h = (_rms(x_ref[...], g_ref[...]) * (1.0 + sc) + sh).astype(BF16)
    for j in range(IN_W // IN_TN):
        res = _dot(h, w_ref[:, j * IN_TN:(j + 1) * IN_TN])
        proj_ref[:, j * IN_TN:(j + 1) * IN_TN] = res.astype(BF16)
        if j == IN_F_TILE:
            f_ref[...] = res[:, IN_F_OFF:IN_F_OFF + BRANCH_W].astype(BF16)
            sa_ref[...] = res[:, IN_S_OFF:IN_S_OFF + 128]
            sb_ref[...] = res[:, IN_S_OFF + 128:IN_S_OFF + 256]


def _in_proj(x, mod, g, w_bf, *, rows_per_mod, mod_base):
    rows, d = x.shape
    tm = math.gcd(512, rows_per_mod)
    kern = functools.partial(_in_kernel, tiles_per_mod=max(rows_per_mod // tm, 1), mod_base=mod_base)
    return pl.pallas_call(
        kern,
        grid=(rows // tm,),
        in_specs=[
            pl.BlockSpec((tm, d), lambda i: (i, 0)),
            pl.BlockSpec(mod.shape, lambda i: (0, 0)),
            pl.BlockSpec((1, d), lambda i: (0, 0)),
            pl.BlockSpec((d, IN_W), lambda i: (0, 0), pipeline_mode=pl.Buffered(1)),
        ],
        out_specs=[
            pl.BlockSpec((tm, IN_W), lambda i: (i, 0)),
            pl.BlockSpec((tm, BRANCH_W), lambda i: (i, 0)),
            pl.BlockSpec((tm, 128), lambda i: (i, 0)),
            pl.BlockSpec((tm, 128), lambda i: (i, 0)),
        ],
        out_shape=[
            jax.ShapeDtypeStruct((rows, IN_W), BF16),
            jax.ShapeDtypeStruct((rows, BRANCH_W), BF16),
            jax.ShapeDtypeStruct((rows, 128), F32),
            jax.ShapeDtypeStruct((rows, 128), F32),
        ],
        compiler_params=_cparams("arbitrary"),
        name="in_proj",
    )(x, mod, g.reshape(1, d), w_bf)


def _fft_a_kernel(x_ref, cs_ref, tc_ref, ts_ref, zr_ref, zi_ref, *, n1, n1p):
    y = _dot(cs_ref[...].astype(BF16), x_ref[...])
    yr = y[:n1]
    yi = y[n1p:n1p + n1]
    tc = tc_ref[...]
    ts = ts_ref[...]
    zr_ref[...] = (yr * tc + yi * ts).astype(BF16)
    zi_ref[...] = (yi * tc - yr * ts).astype(BF16)


def _fft_b_kernel(zr_ref, zi_ref, cs_ref, cc_ref, sc_ref, oa_ref, ob_ref, *, kb, n1, scale, has_imag):
    cs = cs_ref[...].astype(BF16)
    cc = cc_ref[...].astype(BF16)
    sc = sc_ref[...].astype(BF16)
    half = BRANCH_W // 2
    for kk in range(kb):
        a = _dot(cs, zr_ref[kk])
        if has_imag:
            b = _dot(cs, zi_ref[kk])
            xr = a[:FFT_N2] + b[FFT_N2:]
            xi = b[:FFT_N2] - a[FFT_N2:]
        else:
            xr = a[:FFT_N2]
            xi = -a[FFT_N2:]
        out = (_dot(xr.astype(BF16), cc) + _dot(xi.astype(BF16), sc)) * scale
        k1 = pl.program_id(1) * kb + kk
        oa_ref[pl.ds(k1, FFT_N2, stride=n1), :] = out[:, :half]
        ob_ref[pl.ds(k1, FFT_N2, stride=n1), :] = out[:, half:]


def _dft_tables(n):
    k = np.arange(n)
    ang = 2.0 * np.pi * ((k[:, None] * k[None, :]) % n) / n
    return np.cos(ang), np.sin(ang)


def _fft_b_call(zr, zi, n1, batch, seq_len, has_imag):
    c2, s2 = _dft_tables(FFT_N2)
    cs2 = jnp.asarray(np.concatenate([c2, s2], axis=0), F32)
    c64, s64 = _dft_tables(FNET_GROUP_DIM)
    eye = np.eye(BRANCH_W // FNET_GROUP_DIM)
    cc = jnp.asarray(np.kron(eye, c64), F32)
    sc = jnp.asarray(np.kron(eye, s64), F32)
    kb = min(8, n1)
    scale = 1.0 / math.sqrt(seq_len * FNET_GROUP_DIM)
    kern = functools.partial(_fft_b_kernel, kb=kb, n1=n1, scale=scale, has_imag=has_imag)
    zspec = pl.BlockSpec((None, kb, FFT_N2, BRANCH_W), lambda b, i: (b, i, 0, 0))
    half = pl.BlockSpec((seq_len, BRANCH_W // 2), lambda b, i: (b, 0))
    return pl.pallas_call(
        kern,
        grid=(batch, n1 // kb),
        in_specs=[
            zspec, zspec,
            pl.BlockSpec((2 * FFT_N2, FFT_N2), lambda b, i: (0, 0)),
            pl.BlockSpec((BRANCH_W, BRANCH_W), lambda b, i: (0, 0)),
            pl.BlockSpec((BRANCH_W, BRANCH_W), lambda b, i: (0, 0)),
        ],
        out_specs=[half, half],
        out_shape=[jax.ShapeDtypeStruct((batch * seq_len, BRANCH_W // 2), F32)] * 2,
        compiler_params=_cparams("arbitrary", "arbitrary"),
        name="fourier_stage_b",
    )(zr, zi, cs2, cc, sc)


def _fourier_latent(f, batch, seq_len):
    n1 = seq_len // FFT_N2
    wide = FFT_N2 * BRANCH_W
    c1, s1 = _dft_tables(n1)
    n1p = max(n1, 8)
    cs1 = np.zeros((2 * n1p, n1))
    cs1[:n1] = c1
    cs1[n1p:n1p + n1] = -s1
    k1 = np.arange(n1)[:, None]
    l2 = np.arange(FFT_N2)[None, :]
    tw = 2.0 * np.pi * (k1 * l2) / seq_len
    tc = jnp.asarray(np.repeat(np.cos(tw), BRANCH_W, axis=1), F32)
    ts = jnp.asarray(np.repeat(np.sin(tw), BRANCH_W, axis=1), F32)
    cw = min(8192, wide)
    xv = f.reshape(batch, n1, wide)
    spec = pl.BlockSpec((None, n1, cw), lambda b, j: (b, 0, j))
    tspec = pl.BlockSpec((n1, cw), lambda b, j: (0, j))
    zr, zi = pl.pallas_call(
        functools.partial(_fft_a_kernel, n1=n1, n1p=n1p),
        grid=(batch, wide // cw),
        in_specs=[spec, pl.BlockSpec((2 * n1p, n1), lambda b, j: (0, 0)), tspec, tspec],
        out_specs=[spec, spec],
        out_shape=[jax.ShapeDtypeStruct((batch, n1, wide), BF16)] * 2,
        compiler_params=_cparams("arbitrary", "arbitrary"),
        name="fourier_stage_a",
    )(xv, jnp.asarray(cs1, F32), tc, ts)
    zr = zr.reshape(batch, n1, FFT_N2, BRANCH_W)
    zi = zi.reshape(batch, n1, FFT_N2, BRANCH_W)
    return _fft_b_call(zr, zi, n1, batch, seq_len, True)


def _fourier_ctx(f, batch, ctx_len):
    assert ctx_len == FFT_N2
    z = f.reshape(batch, 1, FFT_N2, BRANCH_W)
    return _fft_b_call(z, z, 1, batch, ctx_len, False)


def _s5_tables(a_re, a_im, log_dt, b_re, b_im, c_re, c_im, d_skip, batch):
    t = S5_CHUNK
    g, p, hc = S5_GROUPS, S5_STATE, S5_GROUP_CH
    lam = lax.complex(a_re.astype(F32), a_im.astype(F32))
    dt = jnp.exp(log_dt.astype(F32))[..., None]
    ks = jnp.arange(t + 1, dtype=F32)
    apow = jnp.exp((lam * dt)[..., None] * ks)
    a_bar = apow[..., 1]
    b_bar = ((a_bar - 1.0) / lam)[..., None] * lax.complex(b_re.astype(F32), b_im.astype(F32))
    cm = lax.complex(c_re.astype(F32), c_im.astype(F32))
    kimp = jnp.real(jnp.einsum('dghp,dgpk,dgpj->dgjkh', cm, apow[..., :t], b_bar,
                               precision=lax.Precision.HIGHEST))
    kf, kb = kimp[0], kimp[1]
    kfull = jnp.concatenate([kb[:, :, :0:-1], kf[:, :, :1] + kb[:, :, :1], kf[:, :, 1:]], axis=2)
    kp = kfull.reshape(S5_PAIRS, 2, hc, 2 * t - 1, hc)
    blk = [kp[:, gi] for gi in range(2)]
    zb = jnp.zeros_like(blk[0])
    strip = jnp.concatenate([jnp.stack([blk[0], zb], axis=3), jnp.stack([zb, blk[1]], axis=3)], axis=1)
    strip = strip.reshape(S5_PAIRS, 2 * hc, (2 * t - 1) * 2 * hc)
    strip = jnp.pad(strip, ((0, 0), (0, 0), (0, 2 * hc)))

    wf = jnp.einsum('gpj,gph->gjhp', apow[0][..., t - 1::-1][..., :t], b_bar[0])
    wb = jnp.einsum('gpj,gph->gjhp', apow[1][..., :t], b_bar[1])
    kinds = [jnp.real(wf), jnp.imag(wf), jnp.real(wb), jnp.imag(wb)]

    def we_pair(kd):
        k5 = kd.reshape(S5_PAIRS, 2, t, hc, p)
        z = jnp.zeros_like(k5[:, 0])
        rows = jnp.stack([jnp.concatenate([k5[:, 0], z], axis=-1), jnp.concatenate([z, k5[:, 1]], axis=-1)], axis=2)
        return rows.reshape(S5_PAIRS, 2 * t * hc, 2 * p)

    we = jnp.concatenate([we_pair(kd) for kd in kinds], axis=-1).astype(BF16)

    vf = jnp.einsum('ghp,gpt->gpth', cm[0], apow[0][..., 1:t + 1])
    vb = jnp.einsum('ghp,gpt->gpth', cm[1], apow[1][..., t:0:-1])
    vkinds = [jnp.real(vf), -jnp.imag(vf), jnp.real(vb), -jnp.imag(vb)]

    def v_pair(kd):
        k5 = kd.reshape(S5_PAIRS, 2, p, t, hc)
        z = jnp.zeros_like(k5[:, 0])
        rows = jnp.concatenate([jnp.stack([k5[:, 0], z], axis=3), jnp.stack([z, k5[:, 1]], axis=3)], axis=1)
        return rows.reshape(S5_PAIRS, 2 * p, 2 * t * hc)

    v1 = jnp.concatenate([v_pair(kd) for kd in vkinds], axis=1)
    v = jnp.concatenate([v1, v1], axis=1).astype(BF16)

    def lanes(z):
        return jnp.tile(z.reshape(1, g * p), (1, batch))

    at = apow[..., t]
    a_tab = jnp.concatenate([lanes(jnp.real(at[0])), lanes(jnp.imag(at[0])),
                             lanes(jnp.real(at[1])), lanes(jnp.imag(at[1]))], axis=0)
    dvec = jnp.tile(d_skip.astype(F32).reshape(S5_PAIRS, 1, 2 * hc), (1, t, 1)).reshape(S5_PAIRS, 1, 2 * t * hc)
    return dict(strip=strip, we=we, v=v, a_tab=a_tab, dvec=dvec)


def _s5_pack_kernel(xa_ref, xb_ref, u_ref, *, n_chunks):
    per_half = S5_PAIRS // 2
    for half, x_ref in enumerate((xa_ref, xb_ref)):
        rows = [x_ref[pl.ds(tau, n_chunks, stride=S5_CHUNK), :] for tau in range(S5_CHUNK)]
        for qq in range(per_half):
            pieces = [r[:, qq * 32:(qq + 1) * 32] for r in rows]
            u_ref[half * per_half + qq] = jnp.concatenate(pieces, axis=-1).astype(BF16)


def _s5_unpack_kernel(y_ref, oa_ref, ob_ref, *, n_chunks):
    per_half = S5_PAIRS // 2
    for half, o_ref in enumerate((oa_ref, ob_ref)):
        ys = [y_ref[half * per_half + qq].astype(F32) for qq in range(per_half)]
        for t in range(S5_CHUNK):
            pieces = [y[:, t * 32:(t + 1) * 32] for y in ys]
            o_ref[pl.ds(t, n_chunks, stride=S5_CHUNK), :] = jnp.concatenate(pieces, axis=-1)


def _s5_pack(sa, sb, batch):
    n_chunks = sa.shape[0] // batch // S5_CHUNK
    rows = n_chunks * S5_CHUNK
    cols = 2 * S5_CHUNK * S5_GROUP_CH
    half = pl.BlockSpec((rows, 128), lambda b: (b, 0))
    return pl.pallas_call(
        functools.partial(_s5_pack_kernel, n_chunks=n_chunks),
        grid=(batch,),
        in_specs=[half, half],
        out_specs=pl.BlockSpec((S5_PAIRS, None, n_chunks, cols), lambda b: (0, b, 0, 0)),
        out_shape=jax.ShapeDtypeStruct((S5_PAIRS, batch, n_chunks, cols), BF16),
        compiler_params=_cparams("arbitrary"),
        name="s5_pack",
    )(sa, sb)


def _s5_unpack(y, batch):
    n_chunks = y.shape[2]
    rows = n_chunks * S5_CHUNK
    cols = y.shape[3]
    half = pl.BlockSpec((rows, 128), lambda b: (b, 0))
    return pl.pallas_call(
        functools.partial(_s5_unpack_kernel, n_chunks=n_chunks),
        grid=(batch,),
        in_specs=[pl.BlockSpec((S5_PAIRS, None, n_chunks, cols), lambda b: (0, b, 0, 0))],
        out_specs=[half, half],
        out_shape=[jax.ShapeDtypeStruct((batch * rows, 128), F32)] * 2,
        compiler_params=_cparams("arbitrary"),
        name="s5_unpack",
    )(y)


def _s5_e_kernel(ul_ref, uc_ref, we_ref, ref_, imf_, reb_, imb_):
    u = jnp.concatenate([ul_ref[...], uc_ref[...]], axis=0)
    e = _dot(u, we_ref[...])
    ref_[...] = e[:, 0:128]
    imf_[...] = e[:, 128:256]
    reb_[...] = e[:, 256:384]
    imb_[...] = e[:, 384:512]


def _s5_scan_kernel(a_ref, ref_, imf_, reb_, imb_, prf, pif, prb, pib, *, n_rows, n_ctx):
    afr = a_ref[0:1, :]
    afi = a_ref[1:2, :]
    abr = a_ref[2:3, :]
    abi = a_ref[3:4, :]
    zero = jnp.zeros_like(afr)

    n_lat = n_rows - n_ctx

    def body(s, carry):
        sfr, sfi, sbr, sbi = carry
        nf = jnp.where(s < n_ctx, n_lat + s, s - n_ctx)
        nb = n_rows - 1 - s
        prf[pl.ds(nf, 1), :] = sfr
        pif[pl.ds(nf, 1), :] = sfi
        prb[pl.ds(nb, 1), :] = sbr
        pib[pl.ds(nb, 1), :] = sbi
        efr = ref_[pl.ds(nf, 1), :]
        efi = imf_[pl.ds(nf, 1), :]
        ebr = reb_[pl.ds(nb, 1), :]
        ebi = imb_[pl.ds(nb, 1), :]
        nfr = afr * sfr - afi * sfi + efr
        nfi = afr * sfi + afi * sfr + efi
        nbr = abr * sbr - abi * sbi + ebr
        nbi = abr * sbi + abi * sbr + ebi
        return nfr, nfi, nbr, nbi

    lax.fori_loop(0, n_rows, body, (zero, zero, zero, zero))


def _s5_y_kernel(ul_ref, uc_ref, strip_ref, v_ref, d_ref, prf, pif, prb, pib, yl_ref, yc_ref, m_scr):
    width = 2 * S5_GROUP_CH
    cols = S5_CHUNK * width
    n_lat = yl_ref.shape[0]

    @pl.when(pl.program_id(1) == 0)
    def _():
        strip = strip_ref[...]
        for j in range(S5_CHUNK):
            off = (S5_CHUNK - 1 - j) * width
            win = strip if off == 0 else pltpu.roll(strip, 2 * cols - off, axis=1)
            m_scr[j * width:(j + 1) * width, :] = win[:, :cols].astype(BF16)

    u = jnp.concatenate([ul_ref[...], uc_ref[...]], axis=0)
    y_intra = _dot(u, m_scr[...])
    pcat = jnp.concatenate([prf[...], pif[...], prb[...], pib[...]], axis=-1)
    hi, lo = _split_bf16(pcat)
    y_cross = _dot(jnp.concatenate([hi, lo], axis=-1), v_ref[...])
    y = y_intra + y_cross + d_ref[...] * u.astype(F32)
    yl_ref[...] = y[:n_lat].astype(BF16)
    yc_ref[...] = y[n_lat:].astype(BF16)


def _s5_core(ul, uc, tabs, layer, batch):
    n_lat, n_ctx = ul.shape[2], uc.shape[2]
    n_rows = n_lat + n_ctx
    width = batch * S5_PAIRS * 128
    cols = 2 * S5_CHUNK * S5_GROUP_CH
    ul_spec = pl.BlockSpec((None, None, n_lat, cols), lambda q, b: (q, b, 0, 0))
    uc_spec = pl.BlockSpec((None, None, n_ctx, cols), lambda q, b: (q, b, 0, 0))
    st_spec = pl.BlockSpec((n_rows, 128), lambda q, b: (0, b * S5_PAIRS + q))
    st_shape = jax.ShapeDtypeStruct((n_rows, width), F32)
    e4 = pl.pallas_call(
        _s5_e_kernel,
        grid=(S5_PAIRS, batch),
        in_specs=[ul_spec, uc_spec, pl.BlockSpec((None, None, cols, 512), lambda q, b: (layer, q, 0, 0))],
        out_specs=[st_spec] * 4,
        out_shape=[st_shape] * 4,
        compiler_params=_cparams("arbitrary", "arbitrary"),
        name="s5_chunk_states",
    )(ul, uc, tabs['we'])
    p4 = pl.pallas_call(
        functools.partial(_s5_scan_kernel, n_rows=n_rows, n_ctx=n_ctx),
        out_shape=[st_shape] * 4,
        compiler_params=pltpu.CompilerParams(vmem_limit_bytes=VMEM_LIMIT_BYTES),
        name="s5_state_scan",
    )(tabs['a_tab'][layer], *e4)
    y = pl.pallas_call(
        _s5_y_kernel,
        grid=(S5_PAIRS, batch),
        in_specs=[
            ul_spec, uc_spec,
            pl.BlockSpec((None, None, 2 * S5_GROUP_CH, 2 * cols), lambda q, b: (layer, q, 0, 0)),
            pl.BlockSpec((None, None, cols, cols), lambda q, b: (layer, q, 0, 0)),
            pl.BlockSpec((None, None, 1, cols), lambda q, b: (layer, q, 0, 0)),
            st_spec, st_spec, st_spec, st_spec,
        ],
        out_specs=[ul_spec, uc_spec],
        out_shape=[
            jax.ShapeDtypeStruct((S5_PAIRS, batch, n_lat, cols), BF16),
            jax.ShapeDtypeStruct((S5_PAIRS, batch, n_ctx, cols), BF16),
        ],
        scratch_shapes=[pltpu.VMEM((cols, cols), BF16)],
        compiler_params=_cparams("arbitrary", "arbitrary"),
        name="s5_outputs",
    )(ul, uc, tabs['strip'], tabs['v'], tabs['dvec'], *p4)
    return y


def _s5_mixer(s_lat, s_ctx, tabs, layer, batch):
    ul = _s5_pack(*s_lat, batch)
    uc = _s5_pack(*s_ctx, batch)
    yl, yc = _s5_core(ul, uc, tabs, layer, batch)
    return _s5_unpack(yl, batch), _s5_unpack(yc, batch)


def _ret_tables(ret_decay):
    c = RET_CHUNK
    lg = jax.nn.log_sigmoid(ret_decay.astype(F32))
    lane_h = np.repeat(np.arange(RET_HEADS), RET_DIM)
    lgl = jnp.repeat(lg, RET_DIM, axis=1)
    pos = jnp.arange(c, dtype=F32)[:, None]
    qd = jnp.stack([jnp.exp((pos + 1.0) * lgl[0][None]), jnp.exp((c - pos) * lgl[1][None])])
    kd = jnp.stack([jnp.exp((c - 1.0 - pos) * lgl[0][None]), jnp.exp(pos * lgl[1][None])])
    bmask = jnp.asarray((lane_h[:, None] == lane_h[None, :]).astype(np.float32))
    cd = jnp.exp(c * lgl)[:, :, None] * bmask[None]
    diff = pos - pos.T
    dm = []
    for h in range(RET_HEADS):
        fw = jnp.where(diff >= 0, jnp.exp(jnp.maximum(diff, 0.0) * lg[0, h]), 0.0)
        bw = jnp.where(diff <= 0, jnp.exp(jnp.maximum(-diff, 0.0) * lg[1, h]), 0.0)
        dm.append(fw + bw)
    dm = jnp.concatenate(dm, axis=0)
    return dict(qd=qd, kd=kd, cd=cd, dm=dm)


def _ret_masks():
    lane_h = np.repeat(np.arange(RET_HEADS), RET_DIM)
    bmask = (lane_h[:, None] == lane_h[None, :]).astype(np.float32)
    hmask = (np.arange(RET_HEADS)[:, None] == lane_h[None, :]).astype(np.float32)
    return jnp.asarray(bmask), jnp.asarray(hmask)


def _rope_tables(n_tokens):
    t = np.arange(n_tokens)
    row = (t // GRID_W).astype(np.float64)
    col = (t % GRID_W).astype(np.float64)
    n_freq = RET_DIM // 4
    inv_freq = 1.0 / (ROPE_BASE ** (np.arange(n_freq, dtype=np.float64) / n_freq))
    ang = np.concatenate([row[:, None] * inv_freq, col[:, None] * inv_freq], axis=-1)
    cos = np.cos(ang)
    sin = np.sin(ang)
    cos_t = np.tile(np.concatenate([cos, cos], axis=-1), (1, RET_HEADS))
    sin_t = np.tile(np.concatenate([-sin, sin], axis=-1), (1, RET_HEADS))
    half = RET_DIM // 2
    perm = np.arange(BRANCH_W) ^ half
    swap = np.zeros((BRANCH_W, BRANCH_W), np.float32)
    swap[perm, np.arange(BRANCH_W)] = 1.0
    return jnp.asarray(cos_t, F32), jnp.asarray(sin_t, F32), jnp.asarray(swap, BF16)


def _ret_chunk(q, k, v, s, qd, kd, cd, bmask, dm, hmask, with_intra):
    cross = _dot((q * qd).astype(BF16), s.astype(BF16))
    s_new = cd * s + bmask * _dot_tn((k * kd).astype(BF16), v)
    if not with_intra:
        return cross, s_new
    qb = q.astype(BF16)
    kb = k.astype(BF16)
    qs = jnp.concatenate([qb * hmask[h:h + 1].astype(BF16) for h in range(RET_HEADS)], axis=0)
    scores = _dot_nt(qs, kb) * dm
    ov = _dot(scores.astype(BF16), v)
    c = q.shape[0]
    inner = ov[0:c] * hmask[0:1]
    for h in range(1, RET_HEADS):
        inner = inner + ov[h * c:(h + 1) * c] * hmask[h:h + 1]
    return inner + cross, s_new


def _ret_kernel(qf_ref, kf_ref, vf_ref, qb_ref, kb_ref, vb_ref, qc_ref, kc_ref, vc_ref,
                cosf_ref, sinf_ref, cosb_ref, sinb_ref, swap_ref,
                qd_ref, kd_ref, cd_ref, bm_ref, dm_ref, hm_ref,
                of_ref, ob_ref, ocf_ref, ocb_ref, sf_scr, sb_scr, *, n_chunks, n_ctx_chunks):
    i = pl.program_id(1)
    c = RET_CHUNK
    k_scale = RET_DIM ** -0.5
    bmask = bm_ref[...]
    dm = dm_ref[...]
    hmask = hm_ref[...]
    tabs = [(qd_ref[d], kd_ref[d], cd_ref[d]) for d in range(2)]

    @pl.when(i == 0)
    def _():
        for d, oc_ref, s_scr in ((0, ocf_ref, sf_scr), (1, ocb_ref, sb_scr)):
            qd, kd, cd = tabs[d]
            s = jnp.zeros((BRANCH_W, BRANCH_W), F32)
            order = range(n_ctx_chunks) if d == 0 else range(n_ctx_chunks - 1, -1, -1)
            for cc in order:
                sl = slice(cc * c, (cc + 1) * c)
                o, s = _ret_chunk(qc_ref[sl, :].astype(F32), kc_ref[sl, :].astype(F32) * k_scale, vc_ref[sl, :],
                                  s, qd, kd, cd, bmask, dm, hmask, d == 0)
                oc_ref[sl, :] = o
            s_scr[...] = s

    swap = swap_ref[...]

    def rope(x_ref, cos_ref, sin_ref):
        xb = x_ref[...]
        return xb.astype(F32) * cos_ref[...] + _dot(xb, swap) * sin_ref[...]

    q_f = rope(qf_ref, cosf_ref, sinf_ref)
    k_f = rope(kf_ref, cosf_ref, sinf_ref) * k_scale
    q_b = rope(qb_ref, cosb_ref, sinb_ref)
    k_b = rope(kb_ref, cosb_ref, sinb_ref) * k_scale
    sf = sf_scr[...]
    sb = sb_scr[...]
    for step in range(n_chunks):
        sl = slice(step * c, (step + 1) * c)
        o, sf = _ret_chunk(q_f[sl], k_f[sl], vf_ref[sl, :], sf, *tabs[0], bmask, dm, hmask, True)
        of_ref[sl, :] = o
        cb = n_chunks - 1 - step
        sl = slice(cb * c, (cb + 1) * c)
        o, sb = _ret_chunk(q_b[sl], k_b[sl], vb_ref[sl, :], sb, *tabs[1], bmask, dm, hmask, False)
        ob_ref[sl, :] = o
    sf_scr[...] = sf
    sb_scr[...] = sb


def _retention(proj_l, proj_c, tabs, layer, masks, rope, batch, seq_len, ctx_len):
    n_chunks = 4
    blk = n_chunks * RET_CHUNK
    nblk = seq_len // blk
    cos_t, sin_t, swap = rope

    def lat(col, back):
        if back:
            return pl.BlockSpec((blk, BRANCH_W), lambda b, i: (b * nblk + nblk - 1 - i, col))
        return pl.BlockSpec((blk, BRANCH_W), lambda b, i: (b * nblk + i, col))

    def ctx(col):
        return pl.BlockSpec((ctx_len, BRANCH_W), lambda b, i: (b, col))

    def const(shape):
        return pl.BlockSpec(shape, lambda b, i: (0,) * len(shape))

    def per_layer(shape):
        return pl.BlockSpec((None,) + shape, lambda b, i: (layer,) + (0,) * len(shape))

    tab_f = pl.BlockSpec((blk, BRANCH_W), lambda b, i: (i, 0))
    tab_b = pl.BlockSpec((blk, BRANCH_W), lambda b, i: (nblk - 1 - i, 0))
    kern = functools.partial(_ret_kernel, n_chunks=n_chunks, n_ctx_chunks=ctx_len // RET_CHUNK)
    c = RET_CHUNK
    ctx_out = pl.BlockSpec((ctx_len, BRANCH_W), lambda b, i: (b, 0))
    o_f, o_b, oc_f, oc_b = pl.pallas_call(
        kern,
        grid=(batch, nblk),
        in_specs=[
            lat(COL_RQ, False), lat(COL_RK, False), lat(COL_RV, False),
            lat(COL_RQ, True), lat(COL_RK, True), lat(COL_RV, True),
            ctx(COL_RQ), ctx(COL_RK), ctx(COL_RV),
            tab_f, tab_f, tab_b, tab_b, const((BRANCH_W, BRANCH_W)),
            per_layer((2, c, BRANCH_W)), per_layer((2, c, BRANCH_W)), per_layer((2, BRANCH_W, BRANCH_W)),
            const((BRANCH_W, BRANCH_W)), per_layer((RET_HEADS * c, c)), const((RET_HEADS, BRANCH_W)),
        ],
        out_specs=[lat(0, False), lat(0, True), ctx_out, ctx_out],
        out_shape=[
            jax.ShapeDtypeStruct((batch * seq_len, BRANCH_W), F32),
            jax.ShapeDtypeStruct((batch * seq_len, BRANCH_W), F32),
            jax.ShapeDtypeStruct((batch * ctx_len, BRANCH_W), F32),
            jax.ShapeDtypeStruct((batch * ctx_len, BRANCH_W), F32),
        ],
        scratch_shapes=[pltpu.VMEM((BRANCH_W, BRANCH_W), F32), pltpu.VMEM((BRANCH_W, BRANCH_W), F32)],
        compiler_params=_cparams("arbitrary", "arbitrary"),
        name="retention",
    )(proj_l, proj_l, proj_l, proj_l, proj_l, proj_l, proj_c, proj_c, proj_c,
      cos_t, sin_t, cos_t, sin_t, swap,
      tabs['qd'], tabs['kd'], tabs['cd'], masks[0], tabs['dm'], masks[1])
    return (o_f, o_b), (oc_f, oc_b)


def _na_tables(rpb):
    kr, kw = NA_WIN_ROWS, NA_WIN_COLS
    col = np.arange(GRID_W)
    col_start = np.clip(col - kw // 2, 0, GRID_W - kw)
    in_win = (col[None, :] >= col_start[:, None]) & (col[None, :] < col_start[:, None] + kw)
    dc = np.clip(col[None, :] - col[:, None], -(kw - 1), kw - 1) + (kw - 1)
    pick_c = (dc[:, :, None] == np.arange(2 * kw - 1)[None, None, :]).astype(np.float32)
    by = jnp.einsum('hrc,qkc->hqrk', rpb.astype(F32), jnp.asarray(pick_c), precision=lax.Precision.HIGHEST)
    by = jnp.where(jnp.asarray(in_win)[None, :, None, :], by, NEG_BIG)
    bias = jnp.stack([by[:, :, v:v + kr, :] for v in range(kr)], axis=0)
    return bias.reshape(kr, NA_HEADS * GRID_W, kr * GRID_W)


def _na_head_mask():
    lane_h = np.repeat(np.arange(NA_HEADS), NA_DIM)
    hmask = (np.arange(NA_HEADS)[:, None] == lane_h[None, :]).astype(np.float32)
    return jnp.asarray(hmask, F32)


def _attend(qs, keys, vals, bias, kc, vc):
    s_ctx = _dot_nt(qs, kc)
    m = jnp.max(s_ctx, axis=-1, keepdims=True)
    if keys is not None:
        s_band = _dot_nt(qs, keys) + bias
        m = jnp.maximum(m, jnp.max(s_band, axis=-1, keepdims=True))
        p_band = jnp.exp(s_band - m)
    p_ctx = jnp.exp(s_ctx - m)
    l = jnp.sum(p_ctx, axis=-1, keepdims=True)
    o = _dot(p_ctx.astype(BF16), vc)
    if keys is not None:
        l = l + jnp.sum(p_band, axis=-1, keepdims=True)
        o = o + _dot(p_band.astype(BF16), vals)
    return o / l


def _stack_heads(q, hmask_scaled):
    return jnp.concatenate([q * hmask_scaled[h:h + 1] for h in range(NA_HEADS)], axis=0)


def _unstack_heads(o, hmask, n):
    out = o[0:n] * hmask[0:1]
    for h in range(1, NA_HEADS):
        out = out + o[h * n:(h + 1) * n] * hmask[h:h + 1]
    return out


def _na_kernel(q_ref, k_ref, v_ref, kc_ref, vc_ref, bias_ref, hm_ref, o_ref, *, n_grid_rows):
    i = pl.program_id(1)
    hmask = hm_ref[...]
    hms = (hmask * (NA_DIM ** -0.5)).astype(BF16)
    kc = kc_ref[...]
    vc = vc_ref[...]
    band = NA_WIN_ROWS * GRID_W
    for rr in range(NA_QROWS):
        r = i * NA_QROWS + rr
        rs = jnp.clip(r - NA_WIN_ROWS // 2, 0, n_grid_rows - NA_WIN_ROWS)
        var = rs - r + (NA_WIN_ROWS - 1)
        start = pl.multiple_of(rs * GRID_W, GRID_W)
        keys = k_ref[pl.ds(start, band), :]
        vals = v_ref[pl.ds(start, band), :]
        qs = _stack_heads(q_ref[rr * GRID_W:(rr + 1) * GRID_W, :], hms)
        o = _attend(qs, keys, vals, bias_ref[var], kc, vc)
        o_ref[rr * GRID_W:(rr + 1) * GRID_W, :] = _unstack_heads(o, hmask, GRID_W).astype(BF16)


def _na_ctx_kernel(q_ref, kc_ref, vc_ref, hm_ref, o_ref):
    hmask = hm_ref[...]
    hms = (hmask * (NA_DIM ** -0.5)).astype(BF16)
    n = q_ref.shape[0]
    o = _attend(_stack_heads(q_ref[...], hms), None, None, None, kc_ref[...], vc_ref[...])
    o_ref[...] = _unstack_heads(o, hmask, n).astype(BF16)


def _neighborhood(proj_l, proj_c, bias, layer, hmask, batch, seq_len, ctx_len, need_ctx_out):
    rows = seq_len // GRID_W
    qblk = NA_QROWS * GRID_W
    nq = seq_len // qblk
    out_l = pl.pallas_call(
        functools.partial(_na_kernel, n_grid_rows=rows),
        grid=(batch, nq),
        in_specs=[
            pl.BlockSpec((qblk, BRANCH_W), lambda b, i: (b * nq + i, COL_NQ)),
            pl.BlockSpec((seq_len, BRANCH_W), lambda b, i: (b, COL_NK)),
            pl.BlockSpec((seq_len, BRANCH_W), lambda b, i: (b, COL_NV)),
            pl.BlockSpec((ctx_len, BRANCH_W), lambda b, i: (b, COL_NK)),
            pl.BlockSpec((ctx_len, BRANCH_W), lambda b, i: (b, COL_NV)),
            pl.BlockSpec((None,) + bias.shape[1:], lambda b, i: (layer, 0, 0, 0)),
            pl.BlockSpec(hmask.shape, lambda b, i: (0, 0)),
        ],
        out_specs=pl.BlockSpec((qblk, BRANCH_W), lambda b, i: (b * nq + i, 0)),
        out_shape=jax.ShapeDtypeStruct((batch * seq_len, BRANCH_W), BF16),
        compiler_params=_cparams("arbitrary", "arbitrary"),
        name="neighborhood_attn",
    )(proj_l, proj_l, proj_l, proj_c, proj_c, bias, hmask)
    out_c = None
    if need_ctx_out:
        out_c = pl.pallas_call(
            _na_ctx_kernel,
            grid=(batch,),
            in_specs=[
                pl.BlockSpec((ctx_len, BRANCH_W), lambda b: (b, COL_NQ)),
                pl.BlockSpec((ctx_len, BRANCH_W), lambda b: (b, COL_NK)),
                pl.BlockSpec((ctx_len, BRANCH_W), lambda b: (b, COL_NV)),
                pl.BlockSpec(hmask.shape, lambda b: (0, 0)),
            ],
            out_specs=pl.BlockSpec((ctx_len, BRANCH_W), lambda b: (b, 0)),
            out_shape=jax.ShapeDtypeStruct((batch * ctx_len, BRANCH_W), BF16),
            compiler_params=_cparams("arbitrary"),
            name="context_attn",
        )(proj_c, proj_c, proj_c, hmask)
    return out_l, out_c


def _merge_kernel(x_ref, mod_ref, g_ref, gt0, gt1, gt2, gt3, fa_ref, fb_ref, s5a_ref, s5b_ref, rof_ref, rob_ref, rg_ref, na_ref,
                  wglu_ref, bglu_ref, gn_ref, avg_ref, wb_ref, wo_ref, o_ref, *, tiles_per_mod, mod_base):
    i = pl.program_id(0)
    _, _, gate_a = _mod_rows(mod_ref, i, tiles_per_mod, mod_base, 0)
    z = _gelu_tanh(jnp.concatenate([s5a_ref[...], s5b_ref[...]], axis=-1)).astype(BF16)
    zf = z.astype(F32)
    b_s5 = (zf * _sigmoid(_dot(z, wglu_ref[...]) + bglu_ref[...])).astype(BF16)
    o = rof_ref[...] + rob_ref[...]
    avg = avg_ref[...]
    hi, lo = _split_bf16(o)
    mu = _dot(hi, avg) + _dot(lo, avg)
    dlt = o - mu
    hi, lo = _split_bf16(dlt * dlt)
    var = _dot(hi, avg) + _dot(lo, avg)
    hn = dlt * lax.rsqrt(var + EPS) * gn_ref[...]
    b_ret = (_silu(rg_ref[...].astype(F32)) * hn).astype(BF16)
    b_fnet = jnp.concatenate([fa_ref[...], fb_ref[...]], axis=-1).astype(BF16)
    outs = (b_fnet, b_s5, b_ret, na_ref[...])
    gates = (gt0, gt1, gt2, gt3)
    y = (1.0 + jnp.tanh(gates[0][...].astype(F32))) * _dot(outs[0], wb_ref[0])
    for b in range(1, N_BRANCH):
        y = y + (1.0 + jnp.tanh(gates[b][...].astype(F32))) * _dot(outs[b], wb_ref[b])
    yo = _dot(y.astype(BF16), wo_ref[...])
    o_ref[...] = x_ref[...] + gate_a * _rms(yo, g_ref[...])


def _merge(x, mod, g1, proj, a, s5y, ret_o, na, lw, *, rows_per_mod, mod_base):
    rows, d = x.shape
    tm = min(512, rows)
    nt = rows // tm

    def row(shape, col=0):
        return pl.BlockSpec(shape, lambda i: (i, col))

    def const(arr):
        return pl.BlockSpec(arr.shape, lambda i: (0,) * arr.ndim)

    kern = functools.partial(_merge_kernel, tiles_per_mod=max(rows_per_mod // tm, 1), mod_base=mod_base)
    ins = [x, mod, g1.reshape(1, d), proj, proj, proj, proj, a[0], a[1], s5y[0], s5y[1], ret_o[0], ret_o[1], proj, na,
           lw['w_glu'], lw['b_glu'], lw['ret_gn'], lw['avg'], lw['w_branch'], lw['w_out']]
    specs = [
        row((tm, d)), const(mod), pl.BlockSpec((1, d), lambda i: (0, 0)),
        row((tm, d), 0), row((tm, d), 1), row((tm, d), 2), row((tm, d), 3),
        row((tm, 128)), row((tm, 128)), row((tm, 128)), row((tm, 128)),
        row((tm, BRANCH_W)), row((tm, BRANCH_W)),
        row((tm, BRANCH_W), COL_RG), row((tm, BRANCH_W)),
        const(lw['w_glu']), const(lw['b_glu']), const(lw['ret_gn']), const(lw['avg']),
        const(lw['w_branch']), const(lw['w_out']),
    ]
    return pl.pallas_call(
        kern,
        grid=(nt,),
        in_specs=specs,
        out_specs=row((tm, d)),
        out_shape=jax.ShapeDtypeStruct((rows, d), F32),
        compiler_params=_cparams("arbitrary"),
        name="merge_out",
    )(*ins)


def _ffn_kernel(x_ref, mod_ref, g2_ref, g3_ref, wg_ref, wu_ref, wd_ref, o_ref, *, tiles_per_mod, mod_base):
    i = pl.program_id(0)
    sh, sc, gate_f = _mod_rows(mod_ref, i, tiles_per_mod, mod_base, 3)
    x = x_ref[...]
    h = (_rms(x, g2_ref[...]) * (1.0 + sc) + sh).astype(BF16)
    act = (_silu(_dot(h, wg_ref[...])) * _dot(h, wu_ref[...])).astype(BF16)
    y = _dot(act, wd_ref[...])
    o_ref[...] = x + gate_f * _rms(y, g3_ref[...])


def _ffn_dense(x, mod, g2, g3, wg, wu, wd, *, rows_per_mod, mod_base):
    rows, d = x.shape
    d_ff = wg.shape[1]
    tm = min(512, rows)
    kern = functools.partial(_ffn_kernel, tiles_per_mod=max(rows_per_mod // tm, 1), mod_base=mod_base)

    def resident(shape):
        return pl.BlockSpec(shape, lambda i: (0, 0), pipeline_mode=pl.Buffered(1))

    return pl.pallas_call(
        kern,
        grid=(rows // tm,),
        in_specs=[
            pl.BlockSpec((tm, d), lambda i: (i, 0)),
            pl.BlockSpec(mod.shape, lambda i: (0, 0)),
            pl.BlockSpec((1, d), lambda i: (0, 0)),
            pl.BlockSpec((1, d), lambda i: (0, 0)),
            resident((d, d_ff)), resident((d, d_ff)), resident((d_ff, d)),
        ],
        out_specs=pl.BlockSpec((tm, d), lambda i: (i, 0)),
        out_shape=jax.ShapeDtypeStruct((rows, d), F32),
        compiler_params=_cparams("arbitrary"),
        name="ffn_dense",
    )(x, mod, g2.reshape(1, d), g3.reshape(1, d), wg, wu, wd)


def _router_kernel(x_ref, mod_ref, g2_ref, wr_ref, br_ref, tri_ref, h_ref, comb_ref, plan_ref, cnt_ref, cnt_scr,
                   *, tiles_per_mod, mod_base):
    i = pl.program_id(0)

    @pl.when(i == 0)
    def _():
        cnt_scr[...] = jnp.zeros_like(cnt_scr)

    sh, sc, _ = _mod_rows(mod_ref, i, tiles_per_mod, mod_base, 3)
    h = _rms(x_ref[...], g2_ref[...]) * (1.0 + sc) + sh
    h_ref[...] = _pack_pairs(h)
    h_hi, h_lo = _split_bf16(h)
    w_hi, w_lo = _split_bf16(wr_ref[...])
    logits = _dot(h_hi, w_hi) + _dot(h_lo, w_hi) + _dot(h_hi, w_lo) + br_ref[...]
    lane = lax.broadcasted_iota(jnp.int32, logits.shape, 1)
    v1 = jnp.max(logits, axis=-1, keepdims=True)
    i1 = jnp.min(jnp.where(logits == v1, lane, 128), axis=-1, keepdims=True)
    rest = jnp.where(lane == i1, NEG_BIG, logits)
    v2 = jnp.max(rest, axis=-1, keepdims=True)
    i2 = jnp.min(jnp.where(rest == v2, lane, 128), axis=-1, keepdims=True)
    e = jnp.exp(v2 - v1)
    w1 = 1.0 / (1.0 + e)
    w2 = e / (1.0 + e)
    meta = jnp.where(lane == 0, i1.astype(F32), 0.0) + jnp.where(lane == 1, i2.astype(F32), 0.0)
    meta = meta + jnp.where(lane == 2, w1, 0.0) + jnp.where(lane == 3, w2, 0.0)
    member = jnp.where((lane == i1) | (lane == i2), 1.0, 0.0)
    before = _dot(tri_ref[...], member.astype(BF16)) + cnt_scr[...]
    rank1 = jnp.sum(jnp.where(lane == i1, before, 0.0), axis=-1, keepdims=True)
    rank2 = jnp.sum(jnp.where(lane == i2, before, 0.0), axis=-1, keepdims=True)
    cnt_scr[...] += jnp.sum(member, axis=0, keepdims=True)
    cnt_ref[...] = cnt_scr[...]
    meta = meta + jnp.where(lane == 4, rank1, 0.0) + jnp.where(lane == 5, rank2, 0.0)
    comb_ref[...] = meta[:, :MOE_META_W]
    plan_ref[...] = meta.T[:MOE_META_W]


def _router(x, mod, g2, w_router, b_router, *, rows_per_mod, mod_base):
    rows, d = x.shape
    tm = min(512, rows)
    wr = jnp.zeros((d, 128), F32).at[:, :N_EXPERTS].set(w_router)
    br = jnp.full((1, 128), NEG_BIG, F32).at[0, :N_EXPERTS].set(b_router)
    tri = jnp.asarray(np.tril(np.ones((tm, tm), np.float32), -1), BF16)
    kern = functools.partial(_router_kernel, tiles_per_mod=max(rows_per_mod // tm, 1), mod_base=mod_base)
    return pl.pallas_call(
        kern,
        grid=(rows // tm,),
        in_specs=[
            pl.BlockSpec((tm, d), lambda i: (i, 0)),
            pl.BlockSpec(mod.shape, lambda i: (0, 0)),
            pl.BlockSpec((1, d), lambda i: (0, 0)),
            pl.BlockSpec((d, 128), lambda i: (0, 0)),
            pl.BlockSpec((1, 128), lambda i: (0, 0)),
            pl.BlockSpec((tm, tm), lambda i: (0, 0)),
        ],
        out_specs=[
            pl.BlockSpec((tm, d // 2), lambda i: (i, 0)),
            pl.BlockSpec((tm, MOE_META_W), lambda i: (i, 0)),
            pl.BlockSpec((MOE_META_W, tm), lambda i: (0, i)),
            pl.BlockSpec((1, 128), lambda i: (0, 0)),
        ],
        out_shape=[
            jax.ShapeDtypeStruct((rows, d // 2), jnp.int32),
            jax.ShapeDtypeStruct((rows, MOE_META_W), F32),
            jax.ShapeDtypeStruct((MOE_META_W, rows), F32),
            jax.ShapeDtypeStruct((1, 128), F32),
        ],
        scratch_shapes=[pltpu.VMEM((1, 128), F32)],
        compiler_params=_cparams("arbitrary"),
        name="moe_router",
    )(x, mod, g2.reshape(1, d), wr, br, tri)


def _sc_gather(table, idx):
    n_idx = idx.shape[0]
    width = table.shape[1]
    per_worker = n_idx // SC_WORKERS
    chunk_rows = math.gcd(per_worker, SC_GATHER_ROWS)
    n_chunks = per_worker // chunk_rows
    assert per_worker * SC_WORKERS == n_idx and chunk_rows % 8 == 0
    mesh = plsc.VectorSubcoreMesh(core_axis_name="c", subcore_axis_name="s")

    assert n_chunks % 2 == 0
    buf = [pltpu.VMEM((chunk_rows,), jnp.int32), pltpu.VMEM((chunk_rows, width), table.dtype),
           pltpu.SemaphoreType.DMA, pltpu.SemaphoreType.DMA]

    @functools.partial(
        pl.kernel, mesh=mesh,
        out_type=jax.ShapeDtypeStruct((n_idx, width), table.dtype),
        scratch_types=buf + buf,
        name="sc_row_gather",
    )
    def gather(table_hbm, idx_hbm, out_hbm, idx0, rows0, g0, w0, idx1, rows1, g1, w1):
        wid = lax.axis_index("s") * SC_CORES + lax.axis_index("c")
        base = wid * per_worker
        slots = ((idx0, rows0, g0, w0), (idx1, rows1, g1, w1))

        def fetch(j, slot):
            idx_v, rows_v, g, _ = slots[slot]
            pltpu.sync_copy(idx_hbm.at[pl.ds(base + j * chunk_rows, chunk_rows)], idx_v)
            pltpu.make_async_copy(table_hbm.at[idx_v], rows_v, g).start()

        def store(j, slot):
            idx_v, rows_v, g, w = slots[slot]
            pltpu.make_async_copy(table_hbm.at[idx_v], rows_v, g).wait()
            pltpu.make_async_copy(rows_v, out_hbm.at[pl.ds(base + j * chunk_rows, chunk_rows)], w).start()

        def drain(j, slot):
            _, rows_v, _, w = slots[slot]
            pltpu.make_async_copy(rows_v, out_hbm.at[pl.ds(base + j * chunk_rows, chunk_rows)], w).wait()

        fetch(0, 0)

        @pl.loop(0, n_chunks // 2)
        def _(jj):
            j = 2 * jj

            @pl.when(jj > 0)
            def _():
                drain(j - 1, 1)

            fetch(j + 1, 1)
            store(j, 0)

            @pl.when(j + 2 < n_chunks)
            def _():
                drain(j, 0)
                fetch(j + 2, 0)

            store(j + 1, 1)

        drain(n_chunks - 2, 0)
        drain(n_chunks - 1, 1)

    return gather(table, idx)


def _sc_scatter(table, idx, n_out):
    n_idx = idx.shape[0]
    rows, width = table.shape
    per_worker = n_idx // SC_WORKERS
    chunk_rows = math.gcd(per_worker, SC_GATHER_ROWS)
    n_chunks = per_worker // chunk_rows
    assert per_worker * SC_WORKERS == n_idx and chunk_rows % 8 == 0 and rows % per_worker == 0
    mesh = plsc.VectorSubcoreMesh(core_axis_name="c", subcore_axis_name="s")

    assert n_chunks % 2 == 0
    buf = [pltpu.VMEM((chunk_rows,), jnp.int32), pltpu.VMEM((chunk_rows, width), table.dtype),
           pltpu.SemaphoreType.DMA, pltpu.SemaphoreType.DMA]

    @functools.partial(
        pl.kernel, mesh=mesh,
        out_type=jax.ShapeDtypeStruct((n_out, width), table.dtype),
        scratch_types=buf + buf,
        name="sc_row_scatter",
    )
    def scatter(table_hbm, idx_hbm, out_hbm, idx0, rows0, l0, w0, idx1, rows1, l1, w1):
        wid = lax.axis_index("s") * SC_CORES + lax.axis_index("c")
        base = wid * per_worker
        slots = ((idx0, rows0, l0, w0), (idx1, rows1, l1, w1))

        def src(j):
            return table_hbm.at[pl.ds(lax.rem(base + j * chunk_rows, rows), chunk_rows)]

        def fetch(j, slot):
            idx_v, rows_v, l, _ = slots[slot]
            pltpu.sync_copy(idx_hbm.at[pl.ds(base + j * chunk_rows, chunk_rows)], idx_v)
            pltpu.make_async_copy(src(j), rows_v, l).start()

        def store(j, slot):
            idx_v, rows_v, l, w = slots[slot]
            pltpu.make_async_copy(src(j), rows_v, l).wait()
            pltpu.make_async_copy(rows_v, out_hbm.at[idx_v], w).start()

        def drain(slot):
            idx_v, rows_v, _, w = slots[slot]
            pltpu.make_async_copy(rows_v, out_hbm.at[idx_v], w).wait()

        fetch(0, 0)

        @pl.loop(0, n_chunks // 2)
        def _(jj):
            j = 2 * jj

            @pl.when(jj > 0)
            def _():
                drain(1)

            fetch(j + 1, 1)
            store(j, 0)

            @pl.when(j + 2 < n_chunks)
            def _():
                drain(0)
                fetch(j + 2, 0)

            store(j + 1, 1)

        drain(0)
        drain(1)

    return scatter(table, idx)


def _moe_plan(plan, counts_row, rows):
    tile = MOE_ROW_TILE
    n_tiles = (2 * rows) // tile + N_EXPERTS
    n_slots = n_tiles * tile
    counts = counts_row[0, :N_EXPERTS].astype(jnp.int32)
    padded = ((counts + tile - 1) // tile) * tile
    ends = jnp.cumsum(padded)
    starts = ends - padded
    ids = jnp.arange(N_EXPERTS, dtype=F32)[:, None]
    start_f = starts.astype(F32)[:, None]

    def slot(e_row, r_row):
        return jnp.sum(jnp.where(e_row[None, :] == ids, start_f, 0.0), axis=0) + r_row

    pos = jnp.concatenate([slot(plan[0], plan[4]), slot(plan[1], plan[5])])
    tile_start = jnp.arange(n_tiles, dtype=jnp.int32) * tile
    used = tile_start < ends[-1]
    tile_e = jnp.minimum(jnp.sum((tile_start[:, None] >= ends[None, :]).astype(jnp.int32), axis=1), N_EXPERTS - 1)
    last_e = jnp.max(jnp.where(used, tile_e, 0))
    tile_e = jnp.where(used, tile_e, last_e)
    valid_end = jnp.sum((tile_e[:, None] == jnp.arange(N_EXPERTS)[None, :]) * (starts + counts)[None, :], axis=1)
    n_valid = jnp.where(used, jnp.clip(valid_end - tile_start, 0, tile), 0).astype(jnp.int32)
    return pos.astype(jnp.int32), n_slots, tile_e.astype(jnp.int32), n_valid


def _moe_group_kernel(eid_ref, nval_ref, hs_ref, wg_ref, wu_ref, wd_ref, y_ref, h_scr, acc_scr, *, n_f):
    w = pl.program_id(0)
    f = pl.program_id(1)
    nv = nval_ref[w]

    def run(n_rows):
        rows = slice(0, n_rows)

        @pl.when(f == 0)
        def _():
            hv = _unpack_pairs(hs_ref[rows, :])
            row = lax.broadcasted_iota(jnp.int32, hv.shape, 0)
            h_scr[rows, :] = jnp.where(row < nv, hv, 0.0).astype(BF16)
            acc_scr[rows, :] = jnp.zeros((n_rows, acc_scr.shape[1]), F32)

        h = h_scr[rows, :]
        gate = _dot(h, wg_ref[...].astype(BF16))
        up = _dot(h, wu_ref[...].astype(BF16))
        acc_scr[rows, :] += _dot((_silu(gate) * up).astype(BF16), wd_ref[...].astype(BF16))

        @pl.when(f == n_f - 1)
        def _():
            y_ref[rows, :] = _pack_pairs(acc_scr[rows, :])

    half = hs_ref.shape[0] // 2

    @pl.when(nv > half)
    def _():
        run(hs_ref.shape[0])

    @pl.when((nv > 0) & (nv <= half))
    def _():
        run(half)


def _moe_grouped(hs, tile_e, n_valid, wg, wu, wd):
    n_slots = hs.shape[0]
    d = wg.shape[1]
    d_ff = wg.shape[2]
    tile = MOE_ROW_TILE
    tf = MOE_FF_TILE
    n_f = d_ff // tf

    def f_idx(f, nval, w):
        return jnp.where(nval[w] > 0, f, n_f - 1)

    grid_spec = pltpu.PrefetchScalarGridSpec(
        num_scalar_prefetch=2,
        grid=(n_slots // tile, n_f),
        in_specs=[
            pl.BlockSpec((tile, d // 2), lambda w, f, eid, nval: (w, 0)),
            pl.BlockSpec((None, d, tf), lambda w, f, eid, nval: (eid[w], 0, f_idx(f, nval, w))),
            pl.BlockSpec((None, d, tf), lambda w, f, eid, nval: (eid[w], 0, f_idx(f, nval, w))),
            pl.BlockSpec((None, tf, d), lambda w, f, eid, nval: (eid[w], f_idx(f, nval, w), 0)),
        ],
        out_specs=pl.BlockSpec((tile, d // 2), lambda w, f, eid, nval: (w, 0)),
        scratch_shapes=[pltpu.VMEM((tile, d), BF16), pltpu.VMEM((tile, d), F32)],
    )
    return pl.pallas_call(
        functools.partial(_moe_group_kernel, n_f=n_f),
        grid_spec=grid_spec,
        out_shape=jax.ShapeDtypeStruct((n_slots, d // 2), jnp.int32),
        compiler_params=_cparams("arbitrary", "arbitrary"),
        name="moe_experts",
    )(tile_e, n_valid, hs, wg, wu, wd)


def _moe_out_kernel(x_ref, y1_ref, y2_ref, meta_ref, mod_ref, g3_ref, o_ref, *, tiles_per_mod, mod_base):
    i = pl.program_id(0)
    _, _, gate_f = _mod_rows(mod_ref, i, tiles_per_mod, mod_base, 3)
    meta = meta_ref[...]
    y = meta[:, 2:3] * _unpack_pairs(y1_ref[...]) + meta[:, 3:4] * _unpack_pairs(y2_ref[...])
    o_ref[...] = x_ref[...] + gate_f * _rms(y, g3_ref[...])


def _moe_combine(x, yg, meta, mod, g3, *, rows_per_mod, mod_base):
    rows, d = x.shape
    tm = min(512, rows)
    nt = rows // tm
    kern = functools.partial(_moe_out_kernel, tiles_per_mod=max(rows_per_mod // tm, 1), mod_base=mod_base)
    return pl.pallas_call(
        kern,
        grid=(nt,),
        in_specs=[
            pl.BlockSpec((tm, d), lambda i: (i, 0)),
            pl.BlockSpec((tm, d // 2), lambda i: (i, 0)),
            pl.BlockSpec((tm, d // 2), lambda i: (nt + i, 0)),
            pl.BlockSpec((tm, MOE_META_W), lambda i: (i, 0)),
            pl.BlockSpec(mod.shape, lambda i: (0, 0)),
            pl.BlockSpec((1, d), lambda i: (0, 0)),
        ],
        out_specs=pl.BlockSpec((tm, d), lambda i: (i, 0)),
        out_shape=jax.ShapeDtypeStruct((rows, d), F32),
        compiler_params=_cparams("arbitrary"),
        name="moe_combine",
    )(x, yg, yg, meta, mod, g3.reshape(1, d))


def _moe_sparse(x, routed, mod, g3, wg, wu, wd, *, rows_per_mod, mod_base):
    h, meta, plan, counts = routed
    rows = x.shape[0]
    pos, n_slots, tile_e, n_valid = _moe_plan(plan, counts, rows)
    hs = _sc_scatter(h, pos, n_slots)
    ys = _moe_grouped(hs, tile_e, n_valid, wg, wu, wd)
    yg = _sc_gather(ys, pos)
    return _moe_combine(x, yg, meta, mod, g3, rows_per_mod=rows_per_mod, mod_base=mod_base)


def _cast_kernel(w_ref, o_ref, *, scale):
    w = w_ref[...]
    o_ref[...] = (w if scale == 1.0 else w * scale).astype(BF16)


def _cast_bf16(w_stack, layer, scale=1.0):
    squeeze = w_stack.ndim == 3
    w4 = w_stack[:, None] if squeeze else w_stack
    _, n_e, k, n = w4.shape
    bk = min(k, 256)
    out = pl.pallas_call(
        functools.partial(_cast_kernel, scale=scale),
        grid=(n_e, k // bk),
        in_specs=[pl.BlockSpec((None, None, bk, n), lambda e, i: (layer, e, i, 0))],
        out_specs=pl.BlockSpec((None, bk, n), lambda e, i: (e, i, 0)),
        out_shape=jax.ShapeDtypeStruct((n_e, k, n), BF16),
        compiler_params=_cparams("arbitrary", "arbitrary"),
        name="cast_weights",
    )(w4)
    return out[0] if squeeze else out


def _permute_w_in(w_in_stack, layer):
    _, k, n = w_in_stack.shape
    n_blocks = n // BRANCH_W
    shift = 9
    n_gate_blocks = N_BRANCH * D_MODEL // BRANCH_W

    per_step = 5
    assert n_blocks % per_step == 0

    def permute_kernel(*refs):
        o_ref = refs[-1]
        for s, w_ref in enumerate(refs[:-1]):
            scale = jnp.where(pl.program_id(0) * per_step + s < n_gate_blocks, 0.5, 1.0)
            o_ref[:, s * BRANCH_W:(s + 1) * BRANCH_W] = (w_ref[...] * scale).astype(BF16)

    def src(s):
        return pl.BlockSpec((None, k, BRANCH_W), lambda j: (layer, 0, (j * per_step + s + shift) % n_blocks))

    return pl.pallas_call(
        permute_kernel,
        grid=(n_blocks // per_step,),
        in_specs=[src(s) for s in range(per_step)],
        out_specs=pl.BlockSpec((k, per_step * BRANCH_W), lambda j: (0, j)),
        out_shape=jax.ShapeDtypeStruct((k, n), BF16),
        compiler_params=_cparams("arbitrary"),
        name="cast_permute_w_in",
    )(*([w_in_stack] * per_step))


def kernel(x, c, ctx, c_ctx, w_mod, b_mod, norm_g, w_in, s5_a_re, s5_a_im, s5_log_dt, s5_b_re, s5_b_im, s5_c_re, s5_c_im, s5_d, s5_w_glu, s5_b_glu, ret_decay, ret_gn, na_rpb, w_branch, w_out, ffn_w_gate, ffn_w_up, ffn_w_down, moe_w_router, moe_b_router, moe_w_gate, moe_w_up, moe_w_down):
    batch, seq_len, d = x.shape
    ctx_len = ctx.shape[1]
    depth = w_mod.shape[0]
    cond = jnp.concatenate([c, c_ctx[None, :]], axis=0)
    mod_all = _modulation(cond, w_mod, b_mod)
    rope = _rope_tables(seq_len)
    lane_h = np.repeat(np.arange(RET_HEADS), RET_DIM)
    avg = jnp.asarray((lane_h[:, None] == lane_h[None, :]).astype(np.float32) / RET_DIM, BF16)

    xl = x.reshape(batch * seq_len, d)
    xc = ctx.reshape(batch * ctx_len, d)
    lat = dict(rows_per_mod=seq_len, mod_base=0)
    cxt = dict(rows_per_mod=batch * ctx_len, mod_base=batch)

    s5_tabs = jax.vmap(functools.partial(_s5_tables, batch=batch))(
        s5_a_re, s5_a_im, s5_log_dt, s5_b_re, s5_b_im, s5_c_re, s5_c_im, s5_d)
    ret_tabs = jax.vmap(_ret_tables)(ret_decay)
    ret_masks = _ret_masks()
    na_bias = jax.vmap(_na_tables)(na_rpb)
    na_hmask = _na_head_mask()

    for layer in range(depth):
        last = layer == depth - 1
        need_ctx = not last
        mod = mod_all[layer]
        ng = norm_g[layer]
        w_in_bf = _permute_w_in(w_in, layer)
        lw = dict(w_glu=s5_w_glu[layer].astype(BF16), b_glu=s5_b_glu[layer].reshape(1, BRANCH_W).astype(F32),
                  ret_gn=ret_gn[layer].reshape(1, BRANCH_W).astype(F32), avg=avg,
                  w_branch=_cast_bf16(w_branch, layer, 0.5), w_out=_cast_bf16(w_out, layer))

        proj_l, f_l, *s_in_l = _in_proj(xl, mod, ng[0], w_in_bf, **lat)
        proj_c, f_c, *s_in_c = _in_proj(xc, mod, ng[0], w_in_bf, **cxt)

        a_l = _fourier_latent(f_l, batch, seq_len)
        s_l, s_c = _s5_mixer(s_in_l, s_in_c, s5_tabs, layer, batch)
        r_l, r_c = _retention(proj_l, proj_c, ret_tabs, layer, ret_masks, rope, batch, seq_len, ctx_len)
        n_l, n_c = _neighborhood(proj_l, proj_c, na_bias, layer, na_hmask, batch, seq_len, ctx_len, need_ctx)

        xl = _merge(xl, mod, ng[1], proj_l, a_l, s_l, r_l, n_l, lw, **lat)
        if need_ctx:
            a_c = _fourier_ctx(f_c, batch, ctx_len)
            xc = _merge(xc, mod, ng[1], proj_c, a_c, s_c, r_c, n_c, lw, **cxt)

        i = layer // 2
        if layer % 2 == 0:
            wg, wu, wd = _cast_bf16(ffn_w_gate, i), _cast_bf16(ffn_w_up, i), _cast_bf16(ffn_w_down, i)
            xl = _ffn_dense(xl, mod, ng[2], ng[3], wg, wu, wd, **lat)
            if need_ctx:
                xc = _ffn_dense(xc, mod, ng[2], ng[3], wg, wu, wd, **cxt)
        else:
            wg, wu, wd = moe_w_gate[i], moe_w_up[i], moe_w_down[i]
            routed = _router(xl, mod, ng[2], moe_w_router[i], moe_b_router[i], **lat)
            xl = _moe_sparse(xl, routed, mod, ng[3], wg, wu, wd, **lat)
            if need_ctx:
                routed_c = _router(xc, mod, ng[2], moe_w_router[i], moe_b_router[i], **cxt)
                xc = _moe_sparse(xc, routed_c, mod, ng[3], wg, wu, wd, **cxt)
    return xl.reshape(batch, seq_len, d)
```

```python
import functools
import math

import numpy as np
import jax
import jax.numpy as jnp
from jax import lax
from jax.experimental import pallas as pl
from jax.experimental.pallas import tpu as pltpu
from jax.experimental.pallas import tpu_sc as plsc

F32 = jnp.float32
BF16 = jnp.bfloat16

D_MODEL = 1024
BRANCH_W = 256
N_BRANCH = 4
GRID_W = 64
FNET_GROUP_DIM = 64
S5_GROUP_CH = 16
S5_GROUPS = 16
S5_STATE = 64
S5_CHUNK = 32
S5_PAIRS = S5_GROUPS // 2
RET_HEADS = 4
RET_DIM = 64
RET_CHUNK = 128
NA_HEADS = 4
NA_DIM = 64
NA_WIN_ROWS = 8
NA_WIN_COLS = 16
NA_QROWS = 8
ROPE_BASE = 10000.0
N_EXPERTS = 8
EPS = 1e-6
FFT_N2 = 256
NEG_BIG = -1e30
VMEM_LIMIT_BYTES = 50 * 1024 * 1024
SC_CORES = 2
SC_SUBCORES = 16
SC_WORKERS = SC_CORES * SC_SUBCORES
SC_GATHER_ROWS = 64
MOE_ROW_TILE = 1024
MOE_FF_TILE = 512
MOE_META_W = 8

COL_F, COL_S, COL_RQ, COL_RK, COL_RV, COL_RG, COL_NQ, COL_NK, COL_NV = range(16, 25)
IN_W = 9 * BRANCH_W + N_BRANCH * D_MODEL
IN_TN = 1280
IN_F_TILE = (N_BRANCH * D_MODEL) // IN_TN
IN_F_OFF = N_BRANCH * D_MODEL - IN_F_TILE * IN_TN
IN_S_OFF = IN_F_OFF + BRANCH_W


def _cparams(*sem):
    return pltpu.CompilerParams(dimension_semantics=sem, vmem_limit_bytes=VMEM_LIMIT_BYTES)


def _sigmoid(v):
    return 0.5 * jnp.tanh(0.5 * v) + 0.5


def _silu(v):
    return v * _sigmoid(v)


def _gelu_tanh(v):
    return 0.5 * v * (1.0 + jnp.tanh(math.sqrt(2.0 / math.pi) * (v + 0.044715 * (v * v * v))))


def _rms(v, g):
    ms = jnp.mean(v * v, axis=-1, keepdims=True)
    return v * lax.rsqrt(ms + EPS) * g


def _split_bf16(v):
    hi = v.astype(BF16)
    lo = (v - hi.astype(F32)).astype(BF16)
    return hi, lo


def _pack_pairs(v):
    n = v.shape[1] // 2
    lo = lax.bitcast_convert_type(v[:, :n].astype(BF16).astype(F32), jnp.int32)
    hi = lax.bitcast_convert_type(v[:, n:].astype(BF16).astype(F32), jnp.int32)
    return (hi & -65536) | ((lo >> 16) & 65535)


def _unpack_pairs(w):
    lo = lax.bitcast_convert_type(w << 16, F32)
    hi = lax.bitcast_convert_type(w & -65536, F32)
    return jnp.concatenate([lo, hi], axis=-1)


def _dot(a, b):
    return jnp.dot(a, b, preferred_element_type=F32)


def _dot_nt(a, b):
    return lax.dot_general(a, b, (((1,), (1,)), ((), ())), preferred_element_type=F32)


def _dot_tn(a, b):
    return lax.dot_general(a, b, (((0,), (0,)), ((), ())), preferred_element_type=F32)


def _mod_kernel(ct_ref, w_ref, b_ref, o_ref, *, n_cond):
    ct = ct_ref[...]
    s = _silu(ct)
    w = w_ref[...]
    rows = [jnp.sum(w * s[:, r:r + 1], axis=0, keepdims=True) for r in range(n_cond)]
    rows.append(jnp.zeros((8 - n_cond, w.shape[1]), F32))
    o_ref[...] = jnp.concatenate(rows, axis=0) + b_ref[...]


def _modulation(cond, w_mod, b_mod):
    n_layers, d, n = w_mod.shape
    tn = 512
    ct = jnp.zeros((8, d), F32).at[:cond.shape[0]].set(cond).T
    return pl.pallas_call(
        functools.partial(_mod_kernel, n_cond=cond.shape[0]),
        grid=(n_layers, n // tn),
        in_specs=[
            pl.BlockSpec((d, 8), lambda l, j: (0, 0)),
            pl.BlockSpec((None, d, tn), lambda l, j: (l, 0, j)),
            pl.BlockSpec((None, 1, tn), lambda l, j: (l, 0, j)),
        ],
        out_specs=pl.BlockSpec((None, 8, tn), lambda l, j: (l, 0, j)),
        out_shape=jax.ShapeDtypeStruct((n_layers, 8, n), F32),
        compiler_params=_cparams("arbitrary", "arbitrary"),
        name="adaln_mod",
    )(ct, w_mod, b_mod.reshape(n_layers, 1, n))


def _mod_rows(mod_ref, i, tiles_per_mod, mod_base, first):
    r = mod_base + i // tiles_per_mod
    return [mod_ref[pl.ds(r, 1), (first + k) * D_MODEL:(first + k + 1) * D_MODEL] for k in range(3)]


def _in_kernel(x_ref, mod_ref, g_ref, w_ref, proj_ref, f_ref, sa_ref, sb_ref, *, tiles_per_mod, mod_base):
    i = pl.program_id(0)
    sh, sc, _ = _mod_rows(mod_ref, i, tiles_per_mod, mod_base, 0)
    h = (_rms(x_ref[...], g_ref[...]) * (1.0 + sc) + sh).astype(BF16)
    for j in range(IN_W // IN_TN):
        res = _dot(h, w_ref[:, j * IN_TN:(j + 1) * IN_TN])
        proj_ref[:, j * IN_TN:(j + 1) * IN_TN] = res.astype(BF16)
        if j == IN_F_TILE:
            f_ref[...] = res[:, IN_F_OFF:IN_F_OFF + BRANCH_W].astype(BF16)
            sa_ref[...] = res[:, IN_S_OFF:IN_S_OFF + 128]
            sb_ref[...] = res[:, IN_S_OFF + 128:IN_S_OFF + 256]


def _in_proj(x, mod, g, w_bf, *, rows_per_mod, mod_base):
    rows, d = x.shape
    tm = math.gcd(512, rows_per_mod)
    kern = functools.partial(_in_kernel, tiles_per_mod=max(rows_per_mod // tm, 1), mod_base=mod_base)
    return pl.pallas_call(
        kern,
        grid=(rows // tm,),
        in_specs=[
            pl.BlockSpec((tm, d), lambda i: (i, 0)),
            pl.BlockSpec(mod.shape, lambda i: (0, 0)),
            pl.BlockSpec((1, d), lambda i: (0, 0)),
            pl.BlockSpec((d, IN_W), lambda i: (0, 0), pipeline_mode=pl.Buffered(1)),
        ],
        out_specs=[
            pl.BlockSpec((tm, IN_W), lambda i: (i, 0)),
            pl.BlockSpec((tm, BRANCH_W), lambda i: (i, 0)),
            pl.BlockSpec((tm, 128), lambda i: (i, 0)),
            pl.BlockSpec((tm, 128), lambda i: (i, 0)),
        ],
        out_shape=[
            jax.ShapeDtypeStruct((rows, IN_W), BF16),
            jax.ShapeDtypeStruct((rows, BRANCH_W), BF16),
            jax.ShapeDtypeStruct((rows, 128), F32),
            jax.ShapeDtypeStruct((rows, 128), F32),
        ],
        compiler_params=_cparams("arbitrary"),
        name="in_proj",
    )(x, mod, g.reshape(1, d), w_bf)


def _fft_a_kernel(x_ref, cs_ref, tc_ref, ts_ref, zr_ref, zi_ref, *, n1, n1p):
    y = _dot(cs_ref[...].astype(BF16), x_ref[...])
    yr = y[:n1]
    yi = y[n1p:n1p + n1]
    tc = tc_ref[...]
    ts = ts_ref[...]
    zr_ref[...] = (yr * tc + yi * ts).astype(BF16)
    zi_ref[...] = (yi * tc - yr * ts).astype(BF16)


def _fft_b_kernel(zr_ref, zi_ref, cs_ref, cc_ref, sc_ref, oa_ref, ob_ref, *, kb, n1, scale, has_imag):
    cs = cs_ref[...].astype(BF16)
    cc = cc_ref[...].astype(BF16)
    sc = sc_ref[...].astype(BF16)
    half = BRANCH_W // 2
    for kk in range(kb):
        a = _dot(cs, zr_ref[kk])
        if has_imag:
            b = _dot(cs, zi_ref[kk])
            xr = a[:FFT_N2] + b[FFT_N2:]
            xi = b[:FFT_N2] - a[FFT_N2:]
        else:
            xr = a[:FFT_N2]
            xi = -a[FFT_N2:]
        out = (_dot(xr.astype(BF16), cc) + _dot(xi.astype(BF16), sc)) * scale
        k1 = pl.program_id(1) * kb + kk
        oa_ref[pl.ds(k1, FFT_N2, stride=n1), :] = out[:, :half]
        ob_ref[pl.ds(k1, FFT_N2, stride=n1), :] = out[:, half:]


def _dft_tables(n):
    k = np.arange(n)
    ang = 2.0 * np.pi * ((k[:, None] * k[None, :]) % n) / n
    return np.cos(ang), np.sin(ang)


def _fft_b_call(zr, zi, n1, batch, seq_len, has_imag):
    c2, s2 = _dft_tables(FFT_N2)
    cs2 = jnp.asarray(np.concatenate([c2, s2], axis=0), F32)
    c64, s64 = _dft_tables(FNET_GROUP_DIM)
    eye = np.eye(BRANCH_W // FNET_GROUP_DIM)
    cc = jnp.asarray(np.kron(eye, c64), F32)
    sc = jnp.asarray(np.kron(eye, s64), F32)
    kb = min(8, n1)
    scale = 1.0 / math.sqrt(seq_len * FNET_GROUP_DIM)
    kern = functools.partial(_fft_b_kernel, kb=kb, n1=n1, scale=scale, has_imag=has_imag)
    zspec = pl.BlockSpec((None, kb, FFT_N2, BRANCH_W), lambda b, i: (b, i, 0, 0))
    half = pl.BlockSpec((seq_len, BRANCH_W // 2), lambda b, i: (b, 0))
    return pl.pallas_call(
        kern,
        grid=(batch, n1 // kb),
        in_specs=[
            zspec, zspec,
            pl.BlockSpec((2 * FFT_N2, FFT_N2), lambda b, i: (0, 0)),
            pl.BlockSpec((BRANCH_W, BRANCH_W), lambda b, i: (0, 0)),
            pl.BlockSpec((BRANCH_W, BRANCH_W), lambda b, i: (0, 0)),
        ],
        out_specs=[half, half],
        out_shape=[jax.ShapeDtypeStruct((batch * seq_len, BRANCH_W // 2), F32)] * 2,
        compiler_params=_cparams("arbitrary", "arbitrary"),
        name="fourier_stage_b",
    )(zr, zi, cs2, cc, sc)


def _fourier_latent(f, batch, seq_len):
    n1 = seq_len // FFT_N2
    wide = FFT_N2 * BRANCH_W
    c1, s1 = _dft_tables(n1)
    n1p = max(n1, 8)
    cs1 = np.zeros((2 * n1p, n1))
    cs1[:n1] = c1
    cs1[n1p:n1p + n1] = -s1
    k1 = np.arange(n1)[:, None]
    l2 = np.arange(FFT_N2)[None, :]
    tw = 2.0 * np.pi * (k1 * l2) / seq_len
    tc = jnp.asarray(np.repeat(np.cos(tw), BRANCH_W, axis=1), F32)
    ts = jnp.asarray(np.repeat(np.sin(tw), BRANCH_W, axis=1), F32)
    cw = min(8192, wide)
    xv = f.reshape(batch, n1, wide)
    spec = pl.BlockSpec((None, n1, cw), lambda b, j: (b, 0, j))
    tspec = pl.BlockSpec((n1, cw), lambda b, j: (0, j))
    zr, zi = pl.pallas_call(
        functools.partial(_fft_a_kernel, n1=n1, n1p=n1p),
        grid=(batch, wide // cw),
        in_specs=[spec, pl.BlockSpec((2 * n1p, n1), lambda b, j: (0, 0)), tspec, tspec],
        out_specs=[spec, spec],
        out_shape=[jax.ShapeDtypeStruct((batch, n1, wide), BF16)] * 2,
        compiler_params=_cparams("arbitrary", "arbitrary"),
        name="fourier_stage_a",
    )(xv, jnp.asarray(cs1, F32), tc, ts)
    zr = zr.reshape(batch, n1, FFT_N2, BRANCH_W)
    zi = zi.reshape(batch, n1, FFT_N2, BRANCH_W)
    return _fft_b_call(zr, zi, n1, batch, seq_len, True)


def _fourier_ctx(f, batch, ctx_len):
    assert ctx_len == FFT_N2
    z = f.reshape(batch, 1, FFT_N2, BRANCH_W)
    return _fft_b_call(z, z, 1, batch, ctx_len, False)


def _s5_tables(a_re, a_im, log_dt, b_re, b_im, c_re, c_im, d_skip, batch):
    t = S5_CHUNK
    g, p, hc = S5_GROUPS, S5_STATE, S5_GROUP_CH
    lam = lax.complex(a_re.astype(F32), a_im.astype(F32))
    dt = jnp.exp(log_dt.astype(F32))[..., None]
    ks = jnp.arange(t + 1, dtype=F32)
    apow = jnp.exp((lam * dt)[..., None] * ks)
    a_bar = apow[..., 1]
    b_bar = ((a_bar - 1.0) / lam)[..., None] * lax.complex(b_re.astype(F32), b_im.astype(F32))
    cm = lax.complex(c_re.astype(F32), c_im.astype(F32))
    kimp = jnp.real(jnp.einsum('dghp,dgpk,dgpj->dgjkh', cm, apow[..., :t], b_bar,
                               precision=lax.Precision.HIGHEST))
    kf, kb = kimp[0], kimp[1]
    kfull = jnp.concatenate([kb[:, :, :0:-1], kf[:, :, :1] + kb[:, :, :1], kf[:, :, 1:]], axis=2)
    kp = kfull.reshape(S5_PAIRS, 2, hc, 2 * t - 1, hc)
    blk = [kp[:, gi] for gi in range(2)]
    zb = jnp.zeros_like(blk[0])
    strip = jnp.concatenate([jnp.stack([blk[0], zb], axis=3), jnp.stack([zb, blk[1]], axis=3)], axis=1)
    strip = strip.reshape(S5_PAIRS, 2 * hc, (2 * t - 1) * 2 * hc)
    strip = jnp.pad(strip, ((0, 0), (0, 0), (0, 2 * hc)))

    wf = jnp.einsum('gpj,gph->gjhp', apow[0][..., t - 1::-1][..., :t], b_bar[0])
    wb = jnp.einsum('gpj,gph->gjhp', apow[1][..., :t], b_bar[1])
    kinds = [jnp.real(wf), jnp.imag(wf), jnp.real(wb), jnp.imag(wb)]

    def we_pair(kd):
        k5 = kd.reshape(S5_PAIRS, 2, t, hc, p)
        z = jnp.zeros_like(k5[:, 0])
        rows = jnp.stack([jnp.concatenate([k5[:, 0], z], axis=-1), jnp.concatenate([z, k5[:, 1]], axis=-1)], axis=2)
        return rows.reshape(S5_PAIRS, 2 * t * hc, 2 * p)

    we = jnp.concatenate([we_pair(kd) for kd in kinds], axis=-1).astype(BF16)

    vf = jnp.einsum('ghp,gpt->gpth', cm[0], apow[0][..., 1:t + 1])
    vb = jnp.einsum('ghp,gpt->gpth', cm[1], apow[1][..., t:0:-1])
    vkinds = [jnp.real(vf), -jnp.imag(vf), jnp.real(vb), -jnp.imag(vb)]

    def v_pair(kd):
        k5 = kd.reshape(S5_PAIRS, 2, p, t, hc)
        z = jnp.zeros_like(k5[:, 0])
        rows = jnp.concatenate([jnp.stack([k5[:, 0], z], axis=3), jnp.stack([z, k5[:, 1]], axis=3)], axis=1)
        return rows.reshape(S5_PAIRS, 2 * p, 2 * t * hc)

    v1 = jnp.concatenate([v_pair(kd) for kd in vkinds], axis=1)
    v = jnp.concatenate([v1, v1], axis=1).astype(BF16)

    def lanes(z):
        return jnp.tile(z.reshape(1, g * p), (1, batch))

    at = apow[..., t]
    a_tab = jnp.concatenate([lanes(jnp.real(at[0])), lanes(jnp.imag(at[0])),
                             lanes(jnp.real(at[1])), lanes(jnp.imag(at[1]))], axis=0)
    dvec = jnp.tile(d_skip.astype(F32).reshape(S5_PAIRS, 1, 2 * hc), (1, t, 1)).reshape(S5_PAIRS, 1, 2 * t * hc)
    return dict(strip=strip, we=we, v=v, a_tab=a_tab, dvec=dvec)


def _s5_pack_kernel(xa_ref, xb_ref, u_ref, *, n_chunks):
    per_half = S5_PAIRS // 2
    for half, x_ref in enumerate((xa_ref, xb_ref)):
        rows = [x_ref[pl.ds(tau, n_chunks, stride=S5_CHUNK), :] for tau in range(S5_CHUNK)]
        for qq in range(per_half):
            pieces = [r[:, qq * 32:(qq + 1) * 32] for r in rows]
            u_ref[half * per_half + qq] = jnp.concatenate(pieces, axis=-1).astype(BF16)


def _s5_unpack_kernel(y_ref, oa_ref, ob_ref, *, n_chunks):
    per_half = S5_PAIRS // 2
    for half, o_ref in enumerate((oa_ref, ob_ref)):
        ys = [y_ref[half * per_half + qq].astype(F32) for qq in range(per_half)]
        for t in range(S5_CHUNK):
            pieces = [y[:, t * 32:(t + 1) * 32] for y in ys]
            o_ref[pl.ds(t, n_chunks, stride=S5_CHUNK), :] = jnp.concatenate(pieces, axis=-1)


def _s5_pack(sa, sb, batch):
    n_chunks = sa.shape[0] // batch // S5_CHUNK
    rows = n_chunks * S5_CHUNK
    cols = 2 * S5_CHUNK * S5_GROUP_CH
    half = pl.BlockSpec((rows, 128), lambda b: (b, 0))
    return pl.pallas_call(
        functools.partial(_s5_pack_kernel, n_chunks=n_chunks),
        grid=(batch,),
        in_specs=[half, half],
        out_specs=pl.BlockSpec((S5_PAIRS, None, n_chunks, cols), lambda b: (0, b, 0, 0)),
        out_shape=jax.ShapeDtypeStruct((S5_PAIRS, batch, n_chunks, cols), BF16),
        compiler_params=_cparams("arbitrary"),
        name="s5_pack",
    )(sa, sb)


def _s5_unpack(y, batch):
    n_chunks = y.shape[2]
    rows = n_chunks * S5_CHUNK
    cols = y.shape[3]
    half = pl.BlockSpec((rows, 128), lambda b: (b, 0))
    return pl.pallas_call(
        functools.partial(_s5_unpack_kernel, n_chunks=n_chunks),
        grid=(batch,),
        in_specs=[pl.BlockSpec((S5_PAIRS, None, n_chunks, cols), lambda b: (0, b, 0, 0))],
        out_specs=[half, half],
        out_shape=[jax.ShapeDtypeStruct((batch * rows, 128), F32)] * 2,
        compiler_params=_cparams("arbitrary"),
        name="s5_unpack",
    )(y)


def _s5_e_kernel(ul_ref, uc_ref, we_ref, ref_, imf_, reb_, imb_):
    u = jnp.concatenate([ul_ref[...], uc_ref[...]], axis=0)
    e = _dot(u, we_ref[...])
    ref_[...] = e[:, 0:128]
    imf_[...] = e[:, 128:256]
    reb_[...] = e[:, 256:384]
    imb_[...] = e[:, 384:512]


def _s5_scan_kernel(a_ref, ref_, imf_, reb_, imb_, prf, pif, prb, pib, *, n_rows, n_ctx):
    afr = a_ref[0:1, :]
    afi = a_ref[1:2, :]
    abr = a_ref[2:3, :]
    abi = a_ref[3:4, :]
    zero = jnp.zeros_like(afr)

    n_lat = n_rows - n_ctx

    def body(s, carry):
        sfr, sfi, sbr, sbi = carry
        nf = jnp.where(s < n_ctx, n_lat + s, s - n_ctx)
        nb = n_rows - 1 - s
        prf[pl.ds(nf, 1), :] = sfr
        pif[pl.ds(nf, 1), :] = sfi
        prb[pl.ds(nb, 1), :] = sbr
        pib[pl.ds(nb, 1), :] = sbi
        efr = ref_[pl.ds(nf, 1), :]
        efi = imf_[pl.ds(nf, 1), :]
        ebr = reb_[pl.ds(nb, 1), :]
        ebi = imb_[pl.ds(nb, 1), :]
        nfr = afr * sfr - afi * sfi + efr
        nfi = afr * sfi + afi * sfr + efi
        nbr = abr * sbr - abi * sbi + ebr
        nbi = abr * sbi + abi * sbr + ebi
        return nfr, nfi, nbr, nbi

    lax.fori_loop(0, n_rows, body, (zero, zero, zero, zero))


def _s5_y_kernel(ul_ref, uc_ref, strip_ref, v_ref, d_ref, prf, pif, prb, pib, yl_ref, yc_ref, m_scr):
    width = 2 * S5_GROUP_CH
    cols = S5_CHUNK * width
    n_lat = yl_ref.shape[0]

    @pl.when(pl.program_id(1) == 0)
    def _():
        strip = strip_ref[...]
        for j in range(S5_CHUNK):
            off = (S5_CHUNK - 1 - j) * width
            win = strip if off == 0 else pltpu.roll(strip, 2 * cols - off, axis=1)
            m_scr[j * width:(j + 1) * width, :] = win[:, :cols].astype(BF16)

    u = jnp.concatenate([ul_ref[...], uc_ref[...]], axis=0)
    y_intra = _dot(u, m_scr[...])
    pcat = jnp.concatenate([prf[...], pif[...], prb[...], pib[...]], axis=-1)
    hi, lo = _split_bf16(pcat)
    y_cross = _dot(jnp.concatenate([hi, lo], axis=-1), v_ref[...])
    y = y_intra + y_cross + d_ref[...] * u.astype(F32)
    yl_ref[...] = y[:n_lat].astype(BF16)
    yc_ref[...] = y[n_lat:].astype(BF16)


def _s5_core(ul, uc, tabs, layer, batch):
    n_lat, n_ctx = ul.shape[2], uc.shape[2]
    n_rows = n_lat + n_ctx
    width = batch * S5_PAIRS * 128
    cols = 2 * S5_CHUNK * S5_GROUP_CH
    ul_spec = pl.BlockSpec((None, None, n_lat, cols), lambda q, b: (q, b, 0, 0))
    uc_spec = pl.BlockSpec((None, None, n_ctx, cols), lambda q, b: (q, b, 0, 0))
    st_spec = pl.BlockSpec((n_rows, 128), lambda q, b: (0, b * S5_PAIRS + q))
    st_shape = jax.ShapeDtypeStruct((n_rows, width), F32)
    e4 = pl.pallas_call(
        _s5_e_kernel,
        grid=(S5_PAIRS, batch),
        in_specs=[ul_spec, uc_spec, pl.BlockSpec((None, None, cols, 512), lambda q, b: (layer, q, 0, 0))],
        out_specs=[st_spec] * 4,
        out_shape=[st_shape] * 4,
        compiler_params=_cparams("arbitrary", "arbitrary"),
        name="s5_chunk_states",
    )(ul, uc, tabs['we'])
    p4 = pl.pallas_call(
        functools.partial(_s5_scan_kernel, n_rows=n_rows, n_ctx=n_ctx),
        out_shape=[st_shape] * 4,
        compiler_params=pltpu.CompilerParams(vmem_limit_bytes=VMEM_LIMIT_BYTES),
        name="s5_state_scan",
    )(tabs['a_tab'][layer], *e4)
    y = pl.pallas_call(
        _s5_y_kernel,
        grid=(S5_PAIRS, batch),
        in_specs=[
            ul_spec, uc_spec,
            pl.BlockSpec((None, None, 2 * S5_GROUP_CH, 2 * cols), lambda q, b: (layer, q, 0, 0)),
            pl.BlockSpec((None, None, cols, cols), lambda q, b: (layer, q, 0, 0)),
            pl.BlockSpec((None, None, 1, cols), lambda q, b: (layer, q, 0, 0)),
            st_spec, st_spec, st_spec, st_spec,
        ],
        out_specs=[ul_spec, uc_spec],
        out_shape=[
            jax.ShapeDtypeStruct((S5_PAIRS, batch, n_lat, cols), BF16),
            jax.ShapeDtypeStruct((S5_PAIRS, batch, n_ctx, cols), BF16),
        ],
        scratch_shapes=[pltpu.VMEM((cols, cols), BF16)],
        compiler_params=_cparams("arbitrary", "arbitrary"),
        name="s5_outputs",
    )(ul, uc, tabs['strip'], tabs['v'], tabs['dvec'], *p4)
    return y


def _s5_mixer(s_lat, s_ctx, tabs, layer, batch):
    ul = _s5_pack(*s_lat, batch)
    uc = _s5_pack(*s_ctx, batch)
    yl, yc = _s5_core(ul, uc, tabs, layer, batch)
    return _s5_unpack(yl, batch), _s5_unpack(yc, batch)


def _ret_tables(ret_decay):
    c = RET_CHUNK
    lg = jax.nn.log_sigmoid(ret_decay.astype(F32))
    lane_h = np.repeat(np.arange(RET_HEADS), RET_DIM)
    lgl = jnp.repeat(lg, RET_DIM, axis=1)
    pos = jnp.arange(c, dtype=F32)[:, None]
    qd = jnp.stack([jnp.exp((pos + 1.0) * lgl[0][None]), jnp.exp((c - pos) * lgl[1][None])])
    kd = jnp.stack([jnp.exp((c - 1.0 - pos) * lgl[0][None]), jnp.exp(pos * lgl[1][None])])
    bmask = jnp.asarray((lane_h[:, None] == lane_h[None, :]).astype(np.float32))
    cd = jnp.exp(c * lgl)[:, :, None] * bmask[None]
    diff = pos - pos.T
    dm = []
    for h in range(RET_HEADS):
        fw = jnp.where(diff >= 0, jnp.exp(jnp.maximum(diff, 0.0) * lg[0, h]), 0.0)
        bw = jnp.where(diff <= 0, jnp.exp(jnp.maximum(-diff, 0.0) * lg[1, h]), 0.0)
        dm.append(fw + bw)
    dm = jnp.concatenate(dm, axis=0)
    return dict(qd=qd, kd=kd, cd=cd, dm=dm)


def _ret_masks():
    lane_h = np.repeat(np.arange(RET_HEADS), RET_DIM)
    bmask = (lane_h[:, None] == lane_h[None, :]).astype(np.float32)
    hmask = (np.arange(RET_HEADS)[:, None] == lane_h[None, :]).astype(np.float32)
    return jnp.asarray(bmask), jnp.asarray(hmask)


def _rope_tables(n_tokens):
    t = np.arange(n_tokens)
    row = (t // GRID_W).astype(np.float64)
    col = (t % GRID_W).astype(np.float64)
    n_freq = RET_DIM // 4
    inv_freq = 1.0 / (ROPE_BASE ** (np.arange(n_freq, dtype=np.float64) / n_freq))
    ang = np.concatenate([row[:, None] * inv_freq, col[:, None] * inv_freq], axis=-1)
    cos = np.cos(ang)
    sin = np.sin(ang)
    cos_t = np.tile(np.concatenate([cos, cos], axis=-1), (1, RET_HEADS))
    sin_t = np.tile(np.concatenate([-sin, sin], axis=-1), (1, RET_HEADS))
    half = RET_DIM // 2
    perm = np.arange(BRANCH_W) ^ half
    swap = np.zeros((BRANCH_W, BRANCH_W), np.float32)
    swap[perm, np.arange(BRANCH_W)] = 1.0
    return jnp.asarray(cos_t, F32), jnp.asarray(sin_t, F32), jnp.asarray(swap, BF16)


def _ret_chunk(q, k, v, s, qd, kd, cd, bmask, dm, hmask, with_intra):
    cross = _dot((q * qd).astype(BF16), s.astype(BF16))
    s_new = cd * s + bmask * _dot_tn((k * kd).astype(BF16), v)
    if not with_intra:
        return cross, s_new
    qb = q.astype(BF16)
    kb = k.astype(BF16)
    qs = jnp.concatenate([qb * hmask[h:h + 1].astype(BF16) for h in range(RET_HEADS)], axis=0)
    scores = _dot_nt(qs, kb) * dm
    ov = _dot(scores.astype(BF16), v)
    c = q.shape[0]
    inner = ov[0:c] * hmask[0:1]
    for h in range(1, RET_HEADS):
        inner = inner + ov[h * c:(h + 1) * c] * hmask[h:h + 1]
    return inner + cross, s_new


def _ret_kernel(qf_ref, kf_ref, vf_ref, qb_ref, kb_ref, vb_ref, qc_ref, kc_ref, vc_ref,
                cosf_ref, sinf_ref, cosb_ref, sinb_ref, swap_ref,
                qd_ref, kd_ref, cd_ref, bm_ref, dm_ref, hm_ref,
                of_ref, ob_ref, ocf_ref, ocb_ref, sf_scr, sb_scr, *, n_chunks, n_ctx_chunks):
    i = pl.program_id(1)
    c = RET_CHUNK
    k_scale = RET_DIM ** -0.5
    bmask = bm_ref[...]
    dm = dm_ref[...]
    hmask = hm_ref[...]
    tabs = [(qd_ref[d], kd_ref[d], cd_ref[d]) for d in range(2)]

    @pl.when(i == 0)
    def _():
        for d, oc_ref, s_scr in ((0, ocf_ref, sf_scr), (1, ocb_ref, sb_scr)):
            qd, kd, cd = tabs[d]
            s = jnp.zeros((BRANCH_W, BRANCH_W), F32)
            order = range(n_ctx_chunks) if d == 0 else range(n_ctx_chunks - 1, -1, -1)
            for cc in order:
                sl = slice(cc * c, (cc + 1) * c)
                o, s = _ret_chunk(qc_ref[sl, :].astype(F32), kc_ref[sl, :].astype(F32) * k_scale, vc_ref[sl, :],
                                  s, qd, kd, cd, bmask, dm, hmask, d == 0)
                oc_ref[sl, :] = o
            s_scr[...] = s

    swap = swap_ref[...]

    def rope(x_ref, cos_ref, sin_ref):
        xb = x_ref[...]
        return xb.astype(F32) * cos_ref[...] + _dot(xb, swap) * sin_ref[...]

    q_f = rope(qf_ref, cosf_ref, sinf_ref)
    k_f = rope(kf_ref, cosf_ref, sinf_ref) * k_scale
    q_b = rope(qb_ref, cosb_ref, sinb_ref)
    k_b = rope(kb_ref, cosb_ref, sinb_ref) * k_scale
    sf = sf_scr[...]
    sb = sb_scr[...]
    for step in range(n_chunks):
        sl = slice(step * c, (step + 1) * c)
        o, sf = _ret_chunk(q_f[sl], k_f[sl], vf_ref[sl, :], sf, *tabs[0], bmask, dm, hmask, True)
        of_ref[sl, :] = o
        cb = n_chunks - 1 - step
        sl = slice(cb * c, (cb + 1) * c)
        o, sb = _ret_chunk(q_b[sl], k_b[sl], vb_ref[sl, :], sb, *tabs[1], bmask, dm, hmask, False)
        ob_ref[sl, :] = o
    sf_scr[...] = sf
    sb_scr[...] = sb


def _retention(proj_l, proj_c, tabs, layer, masks, rope, batch, seq_len, ctx_len):
    n_chunks = 4
    blk = n_chunks * RET_CHUNK
    nblk = seq_len // blk
    cos_t, sin_t, swap = rope

    def lat(col, back):
        if back:
            return pl.BlockSpec((blk, BRANCH_W), lambda b, i: (b * nblk + nblk - 1 - i, col))
        return pl.BlockSpec((blk, BRANCH_W), lambda b, i: (b * nblk + i, col))

    def ctx(col):
        return pl.BlockSpec((ctx_len, BRANCH_W), lambda b, i: (b, col))

    def const(shape):
        return pl.BlockSpec(shape, lambda b, i: (0,) * len(shape))

    def per_layer(shape):
        return pl.BlockSpec((None,) + shape, lambda b, i: (layer,) + (0,) * len(shape))

    tab_f = pl.BlockSpec((blk, BRANCH_W), lambda b, i: (i, 0))
    tab_b = pl.BlockSpec((blk, BRANCH_W), lambda b, i: (nblk - 1 - i, 0))
    kern = functools.partial(_ret_kernel, n_chunks=n_chunks, n_ctx_chunks=ctx_len // RET_CHUNK)
    c = RET_CHUNK
    ctx_out = pl.BlockSpec((ctx_len, BRANCH_W), lambda b, i: (b, 0))
    o_f, o_b, oc_f, oc_b = pl.pallas_call(
        kern,
        grid=(batch, nblk),
        in_specs=[
            lat(COL_RQ, False), lat(COL_RK, False), lat(COL_RV, False),
            lat(COL_RQ, True), lat(COL_RK, True), lat(COL_RV, True),
            ctx(COL_RQ), ctx(COL_RK), ctx(COL_RV),
            tab_f, tab_f, tab_b, tab_b, const((BRANCH_W, BRANCH_W)),
            per_layer((2, c, BRANCH_W)), per_layer((2, c, BRANCH_W)), per_layer((2, BRANCH_W, BRANCH_W)),
            const((BRANCH_W, BRANCH_W)), per_layer((RET_HEADS * c, c)), const((RET_HEADS, BRANCH_W)),
        ],
        out_specs=[lat(0, False), lat(0, True), ctx_out, ctx_out],
        out_shape=[
            jax.ShapeDtypeStruct((batch * seq_len, BRANCH_W), F32),
            jax.ShapeDtypeStruct((batch * seq_len, BRANCH_W), F32),
            jax.ShapeDtypeStruct((batch * ctx_len, BRANCH_W), F32),
            jax.ShapeDtypeStruct((batch * ctx_len, BRANCH_W), F32),
        ],
        scratch_shapes=[pltpu.VMEM((BRANCH_W, BRANCH_W), F32), pltpu.VMEM((BRANCH_W, BRANCH_W), F32)],
        compiler_params=_cparams("arbitrary", "arbitrary"),
        name="retention",
    )(proj_l, proj_l, proj_l, proj_l, proj_l, proj_l, proj_c, proj_c, proj_c,
      cos_t, sin_t, cos_t, sin_t, swap,
      tabs['qd'], tabs['kd'], tabs['cd'], masks[0], tabs['dm'], masks[1])
    return (o_f, o_b), (oc_f, oc_b)


def _na_tables(rpb):
    kr, kw = NA_WIN_ROWS, NA_WIN_COLS
    col = np.arange(GRID_W)
    col_start = np.clip(col - kw // 2, 0, GRID_W - kw)
    in_win = (col[None, :] >= col_start[:, None]) & (col[None, :] < col_start[:, None] + kw)
    dc = np.clip(col[None, :] - col[:, None], -(kw - 1), kw - 1) + (kw - 1)
    pick_c = (dc[:, :, None] == np.arange(2 * kw - 1)[None, None, :]).astype(np.float32)
    by = jnp.einsum('hrc,qkc->hqrk', rpb.astype(F32), jnp.asarray(pick_c), precision=lax.Precision.HIGHEST)
    by = jnp.where(jnp.asarray(in_win)[None, :, None, :], by, NEG_BIG)
    bias = jnp.stack([by[:, :, v:v + kr, :] for v in range(kr)], axis=0)
    return bias.reshape(kr, NA_HEADS * GRID_W, kr * GRID_W)


def _na_head_mask():
    lane_h = np.repeat(np.arange(NA_HEADS), NA_DIM)
    hmask = (np.arange(NA_HEADS)[:, None] == lane_h[None, :]).astype(np.float32)
    return jnp.asarray(hmask, F32)


def _attend(qs, keys, vals, bias, kc, vc):
    s_ctx = _dot_nt(qs, kc)
    m = jnp.max(s_ctx, axis=-1, keepdims=True)
    if keys is not None:
        s_band = _dot_nt(qs, keys) + bias
        m = jnp.maximum(m, jnp.max(s_band, axis=-1, keepdims=True))
        p_band = jnp.exp(s_band - m)
    p_ctx = jnp.exp(s_ctx - m)
    l = jnp.sum(p_ctx, axis=-1, keepdims=True)
    o = _dot(p_ctx.astype(BF16), vc)
    if keys is not None:
        l = l + jnp.sum(p_band, axis=-1, keepdims=True)
        o = o + _dot(p_band.astype(BF16), vals)
    return o / l


def _stack_heads(q, hmask_scaled):
    return jnp.concatenate([q * hmask_scaled[h:h + 1] for h in range(NA_HEADS)], axis=0)


def _unstack_heads(o, hmask, n):
    out = o[0:n] * hmask[0:1]
    for h in range(1, NA_HEADS):
        out = out + o[h * n:(h + 1) * n] * hmask[h:h + 1]
    return out


def _na_kernel(q_ref, k_ref, v_ref, kc_ref, vc_ref, bias_ref, hm_ref, o_ref, *, n_grid_rows):
    i = pl.program_id(1)
    hmask = hm_ref[...]
    hms = (hmask * (NA_DIM ** -0.5)).astype(BF16)
    kc = kc_ref[...]
    vc = vc_ref[...]
    band = NA_WIN_ROWS * GRID_W
    for rr in range(NA_QROWS):
        r = i * NA_QROWS + rr
        rs = jnp.clip(r - NA_WIN_ROWS // 2, 0, n_grid_rows - NA_WIN_ROWS)
        var = rs - r + (NA_WIN_ROWS - 1)
        start = pl.multiple_of(rs * GRID_W, GRID_W)
        keys = k_ref[pl.ds(start, band), :]
        vals = v_ref[pl.ds(start, band), :]
        qs = _stack_heads(q_ref[rr * GRID_W:(rr + 1) * GRID_W, :], hms)
        o = _attend(qs, keys, vals, bias_ref[var], kc, vc)
        o_ref[rr * GRID_W:(rr + 1) * GRID_W, :] = _unstack_heads(o, hmask, GRID_W).astype(BF16)


def _na_ctx_kernel(q_ref, kc_ref, vc_ref, hm_ref, o_ref):
    hmask = hm_ref[...]
    hms = (hmask * (NA_DIM ** -0.5)).astype(BF16)
    n = q_ref.shape[0]
    o = _attend(_stack_heads(q_ref[...], hms), None, None, None, kc_ref[...], vc_ref[...])
    o_ref[...] = _unstack_heads(o, hmask, n).astype(BF16)


def _neighborhood(proj_l, proj_c, bias, layer, hmask, batch, seq_len, ctx_len, need_ctx_out):
    rows = seq_len // GRID_W
    qblk = NA_QROWS * GRID_W
    nq = seq_len // qblk
    out_l = pl.pallas_call(
        functools.partial(_na_kernel, n_grid_rows=rows),
        grid=(batch, nq),
        in_specs=[
            pl.BlockSpec((qblk, BRANCH_W), lambda b, i: (b * nq + i, COL_NQ)),
            pl.BlockSpec((seq_len, BRANCH_W), lambda b, i: (b, COL_NK)),
            pl.BlockSpec((seq_len, BRANCH_W), lambda b, i: (b, COL_NV)),
            pl.BlockSpec((ctx_len, BRANCH_W), lambda b, i: (b, COL_NK)),
            pl.BlockSpec((ctx_len, BRANCH_W), lambda b, i: (b, COL_NV)),
            pl.BlockSpec((None,) + bias.shape[1:], lambda b, i: (layer, 0, 0, 0)),
            pl.BlockSpec(hmask.shape, lambda b, i: (0, 0)),
        ],
        out_specs=pl.BlockSpec((qblk, BRANCH_W), lambda b, i: (b * nq + i, 0)),
        out_shape=jax.ShapeDtypeStruct((batch * seq_len, BRANCH_W), BF16),
        compiler_params=_cparams("arbitrary", "arbitrary"),
        name="neighborhood_attn",
    )(proj_l, proj_l, proj_l, proj_c, proj_c, bias, hmask)
    out_c = None
    if need_ctx_out:
        out_c = pl.pallas_call(
            _na_ctx_kernel,
            grid=(batch,),
            in_specs=[
                pl.BlockSpec((ctx_len, BRANCH_W), lambda b: (b, COL_NQ)),
                pl.BlockSpec((ctx_len, BRANCH_W), lambda b: (b, COL_NK)),
                pl.BlockSpec((ctx_len, BRANCH_W), lambda b: (b, COL_NV)),
                pl.BlockSpec(hmask.shape, lambda b: (0, 0)),
            ],
            out_specs=pl.BlockSpec((ctx_len, BRANCH_W), lambda b: (b, 0)),
            out_shape=jax.ShapeDtypeStruct((batch * ctx_len, BRANCH_W), BF16),
            compiler_params=_cparams("arbitrary"),
            name="context_attn",
        )(proj_c, proj_c, proj_c, hmask)
    return out_l, out_c


def _merge_kernel(x_ref, mod_ref, g_ref, gt0, gt1, gt2, gt3, fa_ref, fb_ref, s5a_ref, s5b_ref, rof_ref, rob_ref, rg_ref, na_ref,
                  wglu_ref, bglu_ref, gn_ref, avg_ref, wb_ref, wo_ref, o_ref, *, tiles_per_mod, mod_base):
    i = pl.program_id(0)
    _, _, gate_a = _mod_rows(mod_ref, i, tiles_per_mod, mod_base, 0)
    z = _gelu_tanh(jnp.concatenate([s5a_ref[...], s5b_ref[...]], axis=-1)).astype(BF16)
    zf = z.astype(F32)
    b_s5 = (zf * _sigmoid(_dot(z, wglu_ref[...]) + bglu_ref[...])).astype(BF16)
    o = rof_ref[...] + rob_ref[...]
    avg = avg_ref[...]
    hi, lo = _split_bf16(o)
    mu = _dot(hi, avg) + _dot(lo, avg)
    dlt = o - mu
    hi, lo = _split_bf16(dlt * dlt)
    var = _dot(hi, avg) + _dot(lo, avg)
    hn = dlt * lax.rsqrt(var + EPS) * gn_ref[...]
    b_ret = (_silu(rg_ref[...].astype(F32)) * hn).astype(BF16)
    b_fnet = jnp.concatenate([fa_ref[...], fb_ref[...]], axis=-1).astype(BF16)
    outs = (b_fnet, b_s5, b_ret, na_ref[...])
    gates = (gt0, gt1, gt2, gt3)
    y = (1.0 + jnp.tanh(gates[0][...].astype(F32))) * _dot(outs[0], wb_ref[0])
    for b in range(1, N_BRANCH):
        y = y + (1.0 + jnp.tanh(gates[b][...].astype(F32))) * _dot(outs[b], wb_ref[b])
    yo = _dot(y.astype(BF16), wo_ref[...])
    o_ref[...] = x_ref[...] + gate_a * _rms(yo, g_ref[...])


def _merge(x, mod, g1, proj, a, s5y, ret_o, na, lw, *, rows_per_mod, mod_base):
    rows, d = x.shape
    tm = min(512, rows)
    nt = rows // tm

    def row(shape, col=0):
        return pl.BlockSpec(shape, lambda i: (i, col))

    def const(arr):
        return pl.BlockSpec(arr.shape, lambda i: (0,) * arr.ndim)

    kern = functools.partial(_merge_kernel, tiles_per_mod=max(rows_per_mod // tm, 1), mod_base=mod_base)
    ins = [x, mod, g1.reshape(1, d), proj, proj, proj, proj, a[0], a[1], s5y[0], s5y[1], ret_o[0], ret_o[1], proj, na,
           lw['w_glu'], lw['b_glu'], lw['ret_gn'], lw['avg'], lw['w_branch'], lw['w_out']]
    specs = [
        row((tm, d)), const(mod), pl.BlockSpec((1, d), lambda i: (0, 0)),
        row((tm, d), 0), row((tm, d), 1), row((tm, d), 2), row((tm, d), 3),
        row((tm, 128)), row((tm, 128)), row((tm, 128)), row((tm, 128)),
        row((tm, BRANCH_W)), row((tm, BRANCH_W)),
        row((tm, BRANCH_W), COL_RG), row((tm, BRANCH_W)),
        const(lw['w_glu']), const(lw['b_glu']), const(lw['ret_gn']), const(lw['avg']),
        const(lw['w_branch']), const(lw['w_out']),
    ]
    return pl.pallas_call(
        kern,
        grid=(nt,),
        in_specs=specs,
        out_specs=row((tm, d)),
        out_shape=jax.ShapeDtypeStruct((rows, d), F32),
        compiler_params=_cparams("arbitrary"),
        name="merge_out",
    )(*ins)


def _ffn_kernel(x_ref, mod_ref, g2_ref, g3_ref, wg_ref, wu_ref, wd_ref, o_ref, *, tiles_per_mod, mod_base):
    i = pl.program_id(0)
    sh, sc, gate_f = _mod_rows(mod_ref, i, tiles_per_mod, mod_base, 3)
    x = x_ref[...]
    h = (_rms(x, g2_ref[...]) * (1.0 + sc) + sh).astype(BF16)
    act = (_silu(_dot(h, wg_ref[...])) * _dot(h, wu_ref[...])).astype(BF16)
    y = _dot(act, wd_ref[...])
    o_ref[...] = x + gate_f * _rms(y, g3_ref[...])


def _ffn_dense(x, mod, g2, g3, wg, wu, wd, *, rows_per_mod, mod_base):
    rows, d = x.shape
    d_ff = wg.shape[1]
    tm = min(512, rows)
    kern = functools.partial(_ffn_kernel, tiles_per_mod=max(rows_per_mod // tm, 1), mod_base=mod_base)

    def resident(shape):
        return pl.BlockSpec(shape, lambda i: (0, 0), pipeline_mode=pl.Buffered(1))

    return pl.pallas_call(
        kern,
        grid=(rows // tm,),
        in_specs=[
            pl.BlockSpec((tm, d), lambda i: (i, 0)),
            pl.BlockSpec(mod.shape, lambda i: (0, 0)),
            pl.BlockSpec((1, d), lambda i: (0, 0)),
            pl.BlockSpec((1, d), lambda i: (0, 0)),
            resident((d, d_ff)), resident((d, d_ff)), resident((d_ff, d)),
        ],
        out_specs=pl.BlockSpec((tm, d), lambda i: (i, 0)),
        out_shape=jax.ShapeDtypeStruct((rows, d), F32),
        compiler_params=_cparams("arbitrary"),
        name="ffn_dense",
    )(x, mod, g2.reshape(1, d), g3.reshape(1, d), wg, wu, wd)


def _router_kernel(x_ref, mod_ref, g2_ref, wr_ref, br_ref, tri_ref, h_ref, comb_ref, plan_ref, cnt_ref, cnt_scr,
                   *, tiles_per_mod, mod_base):
    i = pl.program_id(0)

    @pl.when(i == 0)
    def _():
        cnt_scr[...] = jnp.zeros_like(cnt_scr)

    sh, sc, _ = _mod_rows(mod_ref, i, tiles_per_mod, mod_base, 3)
    h = _rms(x_ref[...], g2_ref[...]) * (1.0 + sc) + sh
    h_ref[...] = _pack_pairs(h)
    h_hi, h_lo = _split_bf16(h)
    w_hi, w_lo = _split_bf16(wr_ref[...])
    logits = _dot(h_hi, w_hi) + _dot(h_lo, w_hi) + _dot(h_hi, w_lo) + br_ref[...]
    lane = lax.broadcasted_iota(jnp.int32, logits.shape, 1)
    v1 = jnp.max(logits, axis=-1, keepdims=True)
    i1 = jnp.min(jnp.where(logits == v1, lane, 128), axis=-1, keepdims=True)
    rest = jnp.where(lane == i1, NEG_BIG, logits)
    v2 = jnp.max(rest, axis=-1, keepdims=True)
    i2 = jnp.min(jnp.where(rest == v2, lane, 128), axis=-1, keepdims=True)
    e = jnp.exp(v2 - v1)
    w1 = 1.0 / (1.0 + e)
    w2 = e / (1.0 + e)
    meta = jnp.where(lane == 0, i1.astype(F32), 0.0) + jnp.where(lane == 1, i2.astype(F32), 0.0)
    meta = meta + jnp.where(lane == 2, w1, 0.0) + jnp.where(lane == 3, w2, 0.0)
    member = jnp.where((lane == i1) | (lane == i2), 1.0, 0.0)
    before = _dot(tri_ref[...], member.astype(BF16)) + cnt_scr[...]
    rank1 = jnp.sum(jnp.where(lane == i1, before, 0.0), axis=-1, keepdims=True)
    rank2 = jnp.sum(jnp.where(lane == i2, before, 0.0), axis=-1, keepdims=True)
    cnt_scr[...] += jnp.sum(member, axis=0, keepdims=True)
    cnt_ref[...] = cnt_scr[...]
    meta = meta + jnp.where(lane == 4, rank1, 0.0) + jnp.where(lane == 5, rank2, 0.0)
    comb_ref[...] = meta[:, :MOE_META_W]
    plan_ref[...] = meta.T[:MOE_META_W]


def _router(x, mod, g2, w_router, b_router, *, rows_per_mod, mod_base):
    rows, d = x.shape
    tm = min(512, rows)
    wr = jnp.zeros((d, 128), F32).at[:, :N_EXPERTS].set(w_router)
    br = jnp.full((1, 128), NEG_BIG, F32).at[0, :N_EXPERTS].set(b_router)
    tri = jnp.asarray(np.tril(np.ones((tm, tm), np.float32), -1), BF16)
    kern = functools.partial(_router_kernel, tiles_per_mod=max(rows_per_mod // tm, 1), mod_base=mod_base)
    return pl.pallas_call(
        kern,
        grid=(rows // tm,),
        in_specs=[
            pl.BlockSpec((tm, d), lambda i: (i, 0)),
            pl.BlockSpec(mod.shape, lambda i: (0, 0)),
            pl.BlockSpec((1, d), lambda i: (0, 0)),
            pl.BlockSpec((d, 128), lambda i: (0, 0)),
            pl.BlockSpec((1, 128), lambda i: (0, 0)),
            pl.BlockSpec((tm, tm), lambda i: (0, 0)),
        ],
        out_specs=[
            pl.BlockSpec((tm, d // 2), lambda i: (i, 0)),
            pl.BlockSpec((tm, MOE_META_W), lambda i: (i, 0)),
            pl.BlockSpec((MOE_META_W, tm), lambda i: (0, i)),
            pl.BlockSpec((1, 128), lambda i: (0, 0)),
        ],
        out_shape=[
            jax.ShapeDtypeStruct((rows, d // 2), jnp.int32),
            jax.ShapeDtypeStruct((rows, MOE_META_W), F32),
            jax.ShapeDtypeStruct((MOE_META_W, rows), F32),
            jax.ShapeDtypeStruct((1, 128), F32),
        ],
        scratch_shapes=[pltpu.VMEM((1, 128), F32)],
        compiler_params=_cparams("arbitrary"),
        name="moe_router",
    )(x, mod, g2.reshape(1, d), wr, br, tri)


def _sc_gather(table, idx):
    n_idx = idx.shape[0]
    width = table.shape[1]
    per_worker = n_idx // SC_WORKERS
    chunk_rows = math.gcd(per_worker, SC_GATHER_ROWS)
    n_chunks = per_worker // chunk_rows
    assert per_worker * SC_WORKERS == n_idx and chunk_rows % 8 == 0
    mesh = plsc.VectorSubcoreMesh(core_axis_name="c", subcore_axis_name="s")

    assert n_chunks % 2 == 0
    buf = [pltpu.VMEM((chunk_rows,), jnp.int32), pltpu.VMEM((chunk_rows, width), table.dtype),
           pltpu.SemaphoreType.DMA, pltpu.SemaphoreType.DMA]

    @functools.partial(
        pl.kernel, mesh=mesh,
        out_type=jax.ShapeDtypeStruct((n_idx, width), table.dtype),
        scratch_types=buf + buf,
        name="sc_row_gather",
    )
    def gather(table_hbm, idx_hbm, out_hbm, idx0, rows0, g0, w0, idx1, rows1, g1, w1):
        wid = lax.axis_index("s") * SC_CORES + lax.axis_index("c")
        base = wid * per_worker
        slots = ((idx0, rows0, g0, w0), (idx1, rows1, g1, w1))

        def fetch(j, slot):
            idx_v, rows_v, g, _ = slots[slot]
            pltpu.sync_copy(idx_hbm.at[pl.ds(base + j * chunk_rows, chunk_rows)], idx_v)
            pltpu.make_async_copy(table_hbm.at[idx_v], rows_v, g).start()

        def store(j, slot):
            idx_v, rows_v, g, w = slots[slot]
            pltpu.make_async_copy(table_hbm.at[idx_v], rows_v, g).wait()
            pltpu.make_async_copy(rows_v, out_hbm.at[pl.ds(base + j * chunk_rows, chunk_rows)], w).start()

        def drain(j, slot):
            _, rows_v, _, w = slots[slot]
            pltpu.make_async_copy(rows_v, out_hbm.at[pl.ds(base + j * chunk_rows, chunk_rows)], w).wait()

        fetch(0, 0)

        @pl.loop(0, n_chunks // 2)
        def _(jj):
            j = 2 * jj

            @pl.when(jj > 0)
            def _():
                drain(j - 1, 1)

            fetch(j + 1, 1)
            store(j, 0)

            @pl.when(j + 2 < n_chunks)
            def _():
                drain(j, 0)
                fetch(j + 2, 0)

            store(j + 1, 1)

        drain(n_chunks - 2, 0)
        drain(n_chunks - 1, 1)

    return gather(table, idx)


def _sc_scatter(table, idx, n_out):
    n_idx = idx.shape[0]
    rows, width = table.shape
    per_worker = n_idx // SC_WORKERS
    chunk_rows = math.gcd(per_worker, SC_GATHER_ROWS)
    n_chunks = per_worker // chunk_rows
    assert per_worker * SC_WORKERS == n_idx and chunk_rows % 8 == 0 and rows % per_worker == 0
    mesh = plsc.VectorSubcoreMesh(core_axis_name="c", subcore_axis_name="s")

    assert n_chunks % 2 == 0
    buf = [pltpu.VMEM((chunk_rows,), jnp.int32), pltpu.VMEM((chunk_rows, width), table.dtype),
           pltpu.SemaphoreType.DMA, pltpu.SemaphoreType.DMA]

    @functools.partial(
        pl.kernel, mesh=mesh,
        out_type=jax.ShapeDtypeStruct((n_out, width), table.dtype),
        scratch_types=buf + buf,
        name="sc_row_scatter",
    )
    def scatter(table_hbm, idx_hbm, out_hbm, idx0, rows0, l0, w0, idx1, rows1, l1, w1):
        wid = lax.axis_index("s") * SC_CORES + lax.axis_index("c")
        base = wid * per_worker
        slots = ((idx0, rows0, l0, w0), (idx1, rows1, l1, w1))

        def src(j):
            return table_hbm.at[pl.ds(lax.rem(base + j * chunk_rows, rows), chunk_rows)]

        def fetch(j, slot):
            idx_v, rows_v, l, _ = slots[slot]
            pltpu.sync_copy(idx_hbm.at[pl.ds(base + j * chunk_rows, chunk_rows)], idx_v)
            pltpu.make_async_copy(src(j), rows_v, l).start()

        def store(j, slot):
            idx_v, rows_v, l, w = slots[slot]
            pltpu.make_async_copy(src(j), rows_v, l).wait()
            pltpu.make_async_copy(rows_v, out_hbm.at[idx_v], w).start()

        def drain(slot):
            idx_v, rows_v, _, w = slots[slot]
            pltpu.make_async_copy(rows_v, out_hbm.at[idx_v], w).wait()

        fetch(0, 0)

        @pl.loop(0, n_chunks // 2)
        def _(jj):
            j = 2 * jj

            @pl.when(jj > 0)
            def _():
                drain(1)

            fetch(j + 1, 1)
            store(j, 0)

            @pl.when(j + 2 < n_chunks)
            def _():
                drain(0)
                fetch(j + 2, 0)

            store(j + 1, 1)

        drain(0)
        drain(1)

    return scatter(table, idx)


def _moe_plan(plan, counts_row, rows):
    tile = MOE_ROW_TILE
    n_tiles = (2 * rows) // tile + N_EXPERTS
    n_slots = n_tiles * tile
    counts = counts_row[0, :N_EXPERTS].astype(jnp.int32)
    padded = ((counts + tile - 1) // tile) * tile
    ends = jnp.cumsum(padded)
    starts = ends - padded
    ids = jnp.arange(N_EXPERTS, dtype=F32)[:, None]
    start_f = starts.astype(F32)[:, None]

    def slot(e_row, r_row):
        return jnp.sum(jnp.where(e_row[None, :] == ids, start_f, 0.0), axis=0) + r_row

    pos = jnp.concatenate([slot(plan[0], plan[4]), slot(plan[1], plan[5])])
    tile_start = jnp.arange(n_tiles, dtype=jnp.int32) * tile
    used = tile_start < ends[-1]
    tile_e = jnp.minimum(jnp.sum((tile_start[:, None] >= ends[None, :]).astype(jnp.int32), axis=1), N_EXPERTS - 1)
    last_e = jnp.max(jnp.where(used, tile_e, 0))
    tile_e = jnp.where(used, tile_e, last_e)
    valid_end = jnp.sum((tile_e[:, None] == jnp.arange(N_EXPERTS)[None, :]) * (starts + counts)[None, :], axis=1)
    n_valid = jnp.where(used, jnp.clip(valid_end - tile_start, 0, tile), 0).astype(jnp.int32)
    return pos.astype(jnp.int32), n_slots, tile_e.astype(jnp.int32), n_valid


def _moe_group_kernel(eid_ref, nval_ref, hs_ref, wg_ref, wu_ref, wd_ref, y_ref, acc_scr, *, n_f):
    w = pl.program_id(0)
    f = pl.program_id(1)
    nv = nval_ref[w]

    def run(n_rows):
        rows = slice(0, n_rows)

        hv = _unpack_pairs(hs_ref[rows, :])
        row = lax.broadcasted_iota(jnp.int32, hv.shape, 0)
        h = jnp.where(row < nv, hv, 0.0).astype(BF16)
        gate = _dot(h, wg_ref[...].astype(BF16))
        up = _dot(h, wu_ref[...].astype(BF16))
        part = _dot((_silu(gate) * up).astype(BF16), wd_ref[...].astype(BF16))
        acc = jnp.where(f == 0, 0.0, acc_scr[rows, :]) + part
        acc_scr[rows, :] = acc
        y_ref[rows, :] = _pack_pairs(acc)

    half = hs_ref.shape[0] // 2

    @pl.when(nv > half)
    def _():
        run(hs_ref.shape[0])

    @pl.when((nv > 0) & (nv <= half))
    def _():
        run(half)


def _moe_grouped(hs, tile_e, n_valid, wg, wu, wd):
    n_slots = hs.shape[0]
    d = wg.shape[1]
    d_ff = wg.shape[2]
    tile = MOE_ROW_TILE
    tf = MOE_FF_TILE
    n_f = d_ff // tf

    def f_idx(f, nval, w):
        return jnp.where(nval[w] > 0, f, n_f - 1)

    grid_spec = pltpu.PrefetchScalarGridSpec(
        num_scalar_prefetch=2,
        grid=(n_slots // tile, n_f),
        in_specs=[
            pl.BlockSpec((tile, d // 2), lambda w, f, eid, nval: (w, 0)),
            pl.BlockSpec((None, d, tf), lambda w, f, eid, nval: (eid[w], 0, f_idx(f, nval, w))),
            pl.BlockSpec((None, d, tf), lambda w, f, eid, nval: (eid[w], 0, f_idx(f, nval, w))),
            pl.BlockSpec((None, tf, d), lambda w, f, eid, nval: (eid[w], f_idx(f, nval, w), 0)),
        ],
        out_specs=pl.BlockSpec((tile, d // 2), lambda w, f, eid, nval: (w, 0)),
        scratch_shapes=[pltpu.VMEM((tile, d), F32)],
    )
    return pl.pallas_call(
        functools.partial(_moe_group_kernel, n_f=n_f),
        grid_spec=grid_spec,
        out_shape=jax.ShapeDtypeStruct((n_slots, d // 2), jnp.int32),
        compiler_params=_cparams("arbitrary", "arbitrary"),
        name="moe_experts",
    )(tile_e, n_valid, hs, wg, wu, wd)


def _moe_out_kernel(x_ref, y1_ref, y2_ref, meta_ref, mod_ref, g3_ref, o_ref, *, tiles_per_mod, mod_base):
    i = pl.program_id(0)
    _, _, gate_f = _mod_rows(mod_ref, i, tiles_per_mod, mod_base, 3)
    meta = meta_ref[...]
    y = meta[:, 2:3] * _unpack_pairs(y1_ref[...]) + meta[:, 3:4] * _unpack_pairs(y2_ref[...])
    o_ref[...] = x_ref[...] + gate_f * _rms(y, g3_ref[...])


def _moe_combine(x, yg, meta, mod, g3, *, rows_per_mod, mod_base):
    rows, d = x.shape
    tm = min(512, rows)
    nt = rows // tm
    kern = functools.partial(_moe_out_kernel, tiles_per_mod=max(rows_per_mod // tm, 1), mod_base=mod_base)
    return pl.pallas_call(
        kern,
        grid=(nt,),
        in_specs=[
            pl.BlockSpec((tm, d), lambda i: (i, 0)),
            pl.BlockSpec((tm, d // 2), lambda i: (i, 0)),
            pl.BlockSpec((tm, d // 2), lambda i: (nt + i, 0)),
            pl.BlockSpec((tm, MOE_META_W), lambda i: (i, 0)),
            pl.BlockSpec(mod.shape, lambda i: (0, 0)),
            pl.BlockSpec((1, d), lambda i: (0, 0)),
        ],
        out_specs=pl.BlockSpec((tm, d), lambda i: (i, 0)),
        out_shape=jax.ShapeDtypeStruct((rows, d), F32),
        compiler_params=_cparams("arbitrary"),
        name="moe_combine",
    )(x, yg, yg, meta, mod, g3.reshape(1, d))


def _moe_sparse(x, routed, mod, g3, wg, wu, wd, *, rows_per_mod, mod_base):
    h, meta, plan, counts = routed
    rows = x.shape[0]
    pos, n_slots, tile_e, n_valid = _moe_plan(plan, counts, rows)
    hs = _sc_scatter(h, pos, n_slots)
    ys = _moe_grouped(hs, tile_e, n_valid, wg, wu, wd)
    yg = _sc_gather(ys, pos)
    return _moe_combine(x, yg, meta, mod, g3, rows_per_mod=rows_per_mod, mod_base=mod_base)


def _cast_kernel(w_ref, o_ref, *, scale):
    w = w_ref[...]
    o_ref[...] = (w if scale == 1.0 else w * scale).astype(BF16)


def _cast_bf16(w_stack, layer, scale=1.0):
    squeeze = w_stack.ndim == 3
    w4 = w_stack[:, None] if squeeze else w_stack
    _, n_e, k, n = w4.shape
    bk = min(k, 256)
    out = pl.pallas_call(
        functools.partial(_cast_kernel, scale=scale),
        grid=(n_e, k // bk),
        in_specs=[pl.BlockSpec((None, None, bk, n), lambda e, i: (layer, e, i, 0))],
        out_specs=pl.BlockSpec((None, bk, n), lambda e, i: (e, i, 0)),
        out_shape=jax.ShapeDtypeStruct((n_e, k, n), BF16),
        compiler_params=_cparams("arbitrary", "arbitrary"),
        name="cast_weights",
    )(w4)
    return out[0] if squeeze else out


def _permute_w_in(w_in_stack, layer):
    _, k, n = w_in_stack.shape
    n_blocks = n // BRANCH_W
    shift = 9
    n_gate_blocks = N_BRANCH * D_MODEL // BRANCH_W

    per_step = 5
    assert n_blocks % per_step == 0

    def permute_kernel(*refs):
        o_ref = refs[-1]
        for s, w_ref in enumerate(refs[:-1]):
            scale = jnp.where(pl.program_id(0) * per_step + s < n_gate_blocks, 0.5, 1.0)
            o_ref[:, s * BRANCH_W:(s + 1) * BRANCH_W] = (w_ref[...] * scale).astype(BF16)

    def src(s):
        return pl.BlockSpec((None, k, BRANCH_W), lambda j: (layer, 0, (j * per_step + s + shift) % n_blocks))

    return pl.pallas_call(
        permute_kernel,
        grid=(n_blocks // per_step,),
        in_specs=[src(s) for s in range(per_step)],
        out_specs=pl.BlockSpec((k, per_step * BRANCH_W), lambda j: (0, j)),
        out_shape=jax.ShapeDtypeStruct((k, n), BF16),
        compiler_params=_cparams("arbitrary"),
        name="cast_permute_w_in",
    )(*([w_in_stack] * per_step))


def kernel(x, c, ctx, c_ctx, w_mod, b_mod, norm_g, w_in, s5_a_re, s5_a_im, s5_log_dt, s5_b_re, s5_b_im, s5_c_re, s5_c_im, s5_d, s5_w_glu, s5_b_glu, ret_decay, ret_gn, na_rpb, w_branch, w_out, ffn_w_gate, ffn_w_up, ffn_w_down, moe_w_router, moe_b_router, moe_w_gate, moe_w_up, moe_w_down):
    batch, seq_len, d = x.shape
    ctx_len = ctx.shape[1]
    depth = w_mod.shape[0]
    cond = jnp.concatenate([c, c_ctx[None, :]], axis=0)
    mod_all = _modulation(cond, w_mod, b_mod)
    rope = _rope_tables(seq_len)
    lane_h = np.repeat(np.arange(RET_HEADS), RET_DIM)
    avg = jnp.asarray((lane_h[:, None] == lane_h[None, :]).astype(np.float32) / RET_DIM, BF16)

    xl = x.reshape(batch * seq_len, d)
    xc = ctx.reshape(batch * ctx_len, d)
    lat = dict(rows_per_mod=seq_len, mod_base=0)
    cxt = dict(rows_per_mod=batch * ctx_len, mod_base=batch)

    s5_tabs = jax.vmap(functools.partial(_s5_tables, batch=batch))(
        s5_a_re, s5_a_im, s5_log_dt, s5_b_re, s5_b_im, s5_c_re, s5_c_im, s5_d)
    ret_tabs = jax.vmap(_ret_tables)(ret_decay)
    ret_masks = _ret_masks()
    na_bias = jax.vmap(_na_tables)(na_rpb)
    na_hmask = _na_head_mask()

    for layer in range(depth):
        last = layer == depth - 1
        need_ctx = not last
        mod = mod_all[layer]
        ng = norm_g[layer]
        w_in_bf = _permute_w_in(w_in, layer)
        lw = dict(w_glu=s5_w_glu[layer].astype(BF16), b_glu=s5_b_glu[layer].reshape(1, BRANCH_W).astype(F32),
                  ret_gn=ret_gn[layer].reshape(1, BRANCH_W).astype(F32), avg=avg,
                  w_branch=_cast_bf16(w_branch, layer, 0.5), w_out=_cast_bf16(w_out, layer))

        proj_l, f_l, *s_in_l = _in_proj(xl, mod, ng[0], w_in_bf, **lat)
        proj_c, f_c, *s_in_c = _in_proj(xc, mod, ng[0], w_in_bf, **cxt)

        a_l = _fourier_latent(f_l, batch, seq_len)
        s_l, s_c = _s5_mixer(s_in_l, s_in_c, s5_tabs, layer, batch)
        r_l, r_c = _retention(proj_l, proj_c, ret_tabs, layer, ret_masks, rope, batch, seq_len, ctx_len)
        n_l, n_c = _neighborhood(proj_l, proj_c, na_bias, layer, na_hmask, batch, seq_len, ctx_len, need_ctx)

        xl = _merge(xl, mod, ng[1], proj_l, a_l, s_l, r_l, n_l, lw, **lat)
        if need_ctx:
            a_c = _fourier_ctx(f_c, batch, ctx_len)
            xc = _merge(xc, mod, ng[1], proj_c, a_c, s_c, r_c, n_c, lw, **cxt)

        i = layer // 2
        if layer % 2 == 0:
            wg, wu, wd = _cast_bf16(ffn_w_gate, i), _cast_bf16(ffn_w_up, i), _cast_bf16(ffn_w_down, i)
            xl = _ffn_dense(xl, mod, ng[2], ng[3], wg, wu, wd, **lat)
            if need_ctx:
                xc = _ffn_dense(xc, mod, ng[2], ng[3], wg, wu, wd, **cxt)
        else:
            wg, wu, wd = moe_w_gate[i], moe_w_up[i], moe_w_down[i]
            routed = _router(xl, mod, ng[2], moe_w_router[i], moe_b_router[i], **lat)
            xl = _moe_sparse(xl, routed, mod, ng[3], wg, wu, wd, **lat)
            if need_ctx:
                routed_c = _router(xc, mod, ng[2], moe_w_router[i], moe_b_router[i], **cxt)
                xc = _moe_sparse(xc, routed_c, mod, ng[3], wg, wu, wd, **cxt)
    return xl.reshape(batch, seq_len, d)
```

```python
import functools
import math

import numpy as np
import jax
import jax.numpy as jnp
from jax import lax
from jax.experimental import pallas as pl
from jax.experimental.pallas import tpu as pltpu
from jax.experimental.pallas import tpu_sc as plsc

F32 = jnp.float32
BF16 = jnp.bfloat16

D_MODEL = 1024
BRANCH_W = 256
N_BRANCH = 4
GRID_W = 64
FNET_GROUP_DIM = 64
S5_GROUP_CH = 16
S5_GROUPS = 16
S5_STATE = 64
S5_CHUNK = 32
S5_PAIRS = S5_GROUPS // 2
RET_HEADS = 4
RET_DIM = 64
RET_CHUNK = 128
NA_HEADS = 4
NA_DIM = 64
NA_WIN_ROWS = 8
NA_WIN_COLS = 16
NA_QROWS = 8
ROPE_BASE = 10000.0
N_EXPERTS = 8
EPS = 1e-6
FFT_N2 = 256
NEG_BIG = -1e30
VMEM_LIMIT_BYTES = 50 * 1024 * 1024
SC_CORES = 2
SC_SUBCORES = 16
SC_WORKERS = SC_CORES * SC_SUBCORES
SC_GATHER_ROWS = 64
MOE_ROW_TILE = 1024
MOE_FF_TILE = 512
MOE_W_SPLIT = 2
MOE_META_W = 8

COL_F, COL_S, COL_RQ, COL_RK, COL_RV, COL_RG, COL_NQ, COL_NK, COL_NV = range(16, 25)
IN_W = 9 * BRANCH_W + N_BRANCH * D_MODEL
IN_TN = 1280
IN_F_TILE = (N_BRANCH * D_MODEL) // IN_TN
IN_F_OFF = N_BRANCH * D_MODEL - IN_F_TILE * IN_TN
IN_S_OFF = IN_F_OFF + BRANCH_W


def _cparams(*sem):
    return pltpu.CompilerParams(dimension_semantics=sem, vmem_limit_bytes=VMEM_LIMIT_BYTES)


def _sigmoid(v):
    return 0.5 * jnp.tanh(0.5 * v) + 0.5


def _silu(v):
    return v * _sigmoid(v)


def _gelu_tanh(v):
    return 0.5 * v * (1.0 + jnp.tanh(math.sqrt(2.0 / math.pi) * (v + 0.044715 * (v * v * v))))


def _rms(v, g):
    ms = jnp.mean(v * v, axis=-1, keepdims=True)
    return v * lax.rsqrt(ms + EPS) * g


def _split_bf16(v):
    hi = v.astype(BF16)
    lo = (v - hi.astype(F32)).astype(BF16)
    return hi, lo


def _pack_pairs(v):
    n = v.shape[1] // 2
    lo = lax.bitcast_convert_type(v[:, :n].astype(BF16).astype(F32), jnp.int32)
    hi = lax.bitcast_convert_type(v[:, n:].astype(BF16).astype(F32), jnp.int32)
    return (hi & -65536) | ((lo >> 16) & 65535)


def _unpack_pairs(w):
    lo = lax.bitcast_convert_type(w << 16, F32)
    hi = lax.bitcast_convert_type(w & -65536, F32)
    return jnp.concatenate([lo, hi], axis=-1)


def _dot(a, b):
    return jnp.dot(a, b, preferred_element_type=F32)


def _dot_nt(a, b):
    return lax.dot_general(a, b, (((1,), (1,)), ((), ())), preferred_element_type=F32)


def _dot_tn(a, b):
    return lax.dot_general(a, b, (((0,), (0,)), ((), ())), preferred_element_type=F32)


def _mod_kernel(ct_ref, w_ref, b_ref, o_ref, *, n_cond):
    ct = ct_ref[...]
    s = _silu(ct)
    w = w_ref[...]
    rows = [jnp.sum(w * s[:, r:r + 1], axis=0, keepdims=True) for r in range(n_cond)]
    rows.append(jnp.zeros((8 - n_cond, w.shape[1]), F32))
    o_ref[...] = jnp.concatenate(rows, axis=0) + b_ref[...]


def _modulation(cond, w_mod, b_mod):
    n_layers, d, n = w_mod.shape
    tn = 512
    ct = jnp.zeros((8, d), F32).at[:cond.shape[0]].set(cond).T
    return pl.pallas_call(
        functools.partial(_mod_kernel, n_cond=cond.shape[0]),
        grid=(n_layers, n // tn),
        in_specs=[
            pl.BlockSpec((d, 8), lambda l, j: (0, 0)),
            pl.BlockSpec((None, d, tn), lambda l, j: (l, 0, j)),
            pl.BlockSpec((None, 1, tn), lambda l, j: (l, 0, j)),
        ],
        out_specs=pl.BlockSpec((None, 8, tn), lambda l, j: (l, 0, j)),
        out_shape=jax.ShapeDtypeStruct((n_layers, 8, n), F32),
        compiler_params=_cparams("arbitrary", "arbitrary"),
        name="adaln_mod",
    )(ct, w_mod, b_mod.reshape(n_layers, 1, n))


def _mod_rows(mod_ref, i, tiles_per_mod, mod_base, first):
    r = mod_base + i // tiles_per_mod
    return [mod_ref[pl.ds(r, 1), (first + k) * D_MODEL:(first + k + 1) * D_MODEL] for k in range(3)]


def _in_kernel(x_ref, mod_ref, g_ref, w_ref, proj_ref, f_ref, sa_ref, sb_ref, *, tiles_per_mod, mod_base):
    i = pl.program_id(0)
    sh, sc, _ = _mod_rows(mod_ref, i, tiles_per_mod, mod_base, 0)
    h = (_rms(x_ref[...], g_ref[...]) * (1.0 + sc) + sh).astype(BF16)
    for j in range(IN_W // IN_TN):
        res = _dot(h, w_ref[:, j * IN_TN:(j + 1) * IN_TN])
        proj_ref[:, j * IN_TN:(j + 1) * IN_TN] = res.astype(BF16)
        if j == IN_F_TILE:
            f_ref[...] = res[:, IN_F_OFF:IN_F_OFF + BRANCH_W].astype(BF16)
            sa_ref[...] = res[:, IN_S_OFF:IN_S_OFF + 128]
            sb_ref[...] = res[:, IN_S_OFF + 128:IN_S_OFF + 256]


def _in_proj(x, mod, g, w_bf, *, rows_per_mod, mod_base):
    rows, d = x.shape
    tm = math.gcd(512, rows_per_mod)
    kern = functools.partial(_in_kernel, tiles_per_mod=max(rows_per_mod // tm, 1), mod_base=mod_base)
    return pl.pallas_call(
        kern,
        grid=(rows // tm,),
        in_specs=[
            pl.BlockSpec((tm, d), lambda i: (i, 0)),
            pl.BlockSpec(mod.shape, lambda i: (0, 0)),
            pl.BlockSpec((1, d), lambda i: (0, 0)),
            pl.BlockSpec((d, IN_W), lambda i: (0, 0), pipeline_mode=pl.Buffered(1)),
        ],
        out_specs=[
            pl.BlockSpec((tm, IN_W), lambda i: (i, 0)),
            pl.BlockSpec((tm, BRANCH_W), lambda i: (i, 0)),
            pl.BlockSpec((tm, 128), lambda i: (i, 0)),
            pl.BlockSpec((tm, 128), lambda i: (i, 0)),
        ],
        out_shape=[
            jax.ShapeDtypeStruct((rows, IN_W), BF16),
            jax.ShapeDtypeStruct((rows, BRANCH_W), BF16),
            jax.ShapeDtypeStruct((rows, 128), F32),
            jax.ShapeDtypeStruct((rows, 128), F32),
        ],
        compiler_params=_cparams("arbitrary"),
        name="in_proj",
    )(x, mod, g.reshape(1, d), w_bf)


def _fft_a_kernel(x_ref, cs_ref, tc_ref, ts_ref, zr_ref, zi_ref, *, n1, n1p):
    y = _dot(cs_ref[...].astype(BF16), x_ref[...])
    yr = y[:n1]
    yi = y[n1p:n1p + n1]
    tc = tc_ref[...]
    ts = ts_ref[...]
    zr_ref[...] = (yr * tc + yi * ts).astype(BF16)
    zi_ref[...] = (yi * tc - yr * ts).astype(BF16)


def _fft_b_kernel(zr_ref, zi_ref, cs_ref, cc_ref, sc_ref, oa_ref, ob_ref, *, kb, n1, scale, has_imag):
    cs = cs_ref[...].astype(BF16)
    cc = cc_ref[...].astype(BF16)
    sc = sc_ref[...].astype(BF16)
    half = BRANCH_W // 2
    for kk in range(kb):
        a = _dot(cs, zr_ref[kk])
        if has_imag:
            b = _dot(cs, zi_ref[kk])
            xr = a[:FFT_N2] + b[FFT_N2:]
            xi = b[:FFT_N2] - a[FFT_N2:]
        else:
            xr = a[:FFT_N2]
            xi = -a[FFT_N2:]
        out = (_dot(xr.astype(BF16), cc) + _dot(xi.astype(BF16), sc)) * scale
        k1 = pl.program_id(1) * kb + kk
        oa_ref[pl.ds(k1, FFT_N2, stride=n1), :] = out[:, :half]
        ob_ref[pl.ds(k1, FFT_N2, stride=n1), :] = out[:, half:]


def _dft_tables(n):
    k = np.arange(n)
    ang = 2.0 * np.pi * ((k[:, None] * k[None, :]) % n) / n
    return np.cos(ang), np.sin(ang)


def _fft_b_call(zr, zi, n1, batch, seq_len, has_imag):
    c2, s2 = _dft_tables(FFT_N2)
    cs2 = jnp.asarray(np.concatenate([c2, s2], axis=0), F32)
    c64, s64 = _dft_tables(FNET_GROUP_DIM)
    eye = np.eye(BRANCH_W // FNET_GROUP_DIM)
    cc = jnp.asarray(np.kron(eye, c64), F32)
    sc = jnp.asarray(np.kron(eye, s64), F32)
    kb = min(8, n1)
    scale = 1.0 / math.sqrt(seq_len * FNET_GROUP_DIM)
    kern = functools.partial(_fft_b_kernel, kb=kb, n1=n1, scale=scale, has_imag=has_imag)
    zspec = pl.BlockSpec((None, kb, FFT_N2, BRANCH_W), lambda b, i: (b, i, 0, 0))
    half = pl.BlockSpec((seq_len, BRANCH_W // 2), lambda b, i: (b, 0))
    return pl.pallas_call(
        kern,
        grid=(batch, n1 // kb),
        in_specs=[
            zspec, zspec,
            pl.BlockSpec((2 * FFT_N2, FFT_N2), lambda b, i: (0, 0)),
            pl.BlockSpec((BRANCH_W, BRANCH_W), lambda b, i: (0, 0)),
            pl.BlockSpec((BRANCH_W, BRANCH_W), lambda b, i: (0, 0)),
        ],
        out_specs=[half, half],
        out_shape=[jax.ShapeDtypeStruct((batch * seq_len, BRANCH_W // 2), F32)] * 2,
        compiler_params=_cparams("arbitrary", "arbitrary"),
        name="fourier_stage_b",
    )(zr, zi, cs2, cc, sc)


def _fourier_latent(f, batch, seq_len):
    n1 = seq_len // FFT_N2
    wide = FFT_N2 * BRANCH_W
    c1, s1 = _dft_tables(n1)
    n1p = max(n1, 8)
    cs1 = np.zeros((2 * n1p, n1))
    cs1[:n1] = c1
    cs1[n1p:n1p + n1] = -s1
    k1 = np.arange(n1)[:, None]
    l2 = np.arange(FFT_N2)[None, :]
    tw = 2.0 * np.pi * (k1 * l2) / seq_len
    tc = jnp.asarray(np.repeat(np.cos(tw), BRANCH_W, axis=1), F32)
    ts = jnp.asarray(np.repeat(np.sin(tw), BRANCH_W, axis=1), F32)
    cw = min(8192, wide)
    xv = f.reshape(batch, n1, wide)
    spec = pl.BlockSpec((None, n1, cw), lambda b, j: (b, 0, j))
    tspec = pl.BlockSpec((n1, cw), lambda b, j: (0, j))
    zr, zi = pl.pallas_call(
        functools.partial(_fft_a_kernel, n1=n1, n1p=n1p),
        grid=(batch, wide // cw),
        in_specs=[spec, pl.BlockSpec((2 * n1p, n1), lambda b, j: (0, 0)), tspec, tspec],
        out_specs=[spec, spec],
        out_shape=[jax.ShapeDtypeStruct((batch, n1, wide), BF16)] * 2,
        compiler_params=_cparams("arbitrary", "arbitrary"),
        name="fourier_stage_a",
    )(xv, jnp.asarray(cs1, F32), tc, ts)
    zr = zr.reshape(batch, n1, FFT_N2, BRANCH_W)
    zi = zi.reshape(batch, n1, FFT_N2, BRANCH_W)
    return _fft_b_call(zr, zi, n1, batch, seq_len, True)


def _fourier_ctx(f, batch, ctx_len):
    assert ctx_len == FFT_N2
    z = f.reshape(batch, 1, FFT_N2, BRANCH_W)
    return _fft_b_call(z, z, 1, batch, ctx_len, False)


def _s5_tables(a_re, a_im, log_dt, b_re, b_im, c_re, c_im, d_skip, batch):
    t = S5_CHUNK
    g, p, hc = S5_GROUPS, S5_STATE, S5_GROUP_CH
    lam = lax.complex(a_re.astype(F32), a_im.astype(F32))
    dt = jnp.exp(log_dt.astype(F32))[..., None]
    ks = jnp.arange(t + 1, dtype=F32)
    apow = jnp.exp((lam * dt)[..., None] * ks)
    a_bar = apow[..., 1]
    b_bar = ((a_bar - 1.0) / lam)[..., None] * lax.complex(b_re.astype(F32), b_im.astype(F32))
    cm = lax.complex(c_re.astype(F32), c_im.astype(F32))
    kimp = jnp.real(jnp.einsum('dghp,dgpk,dgpj->dgjkh', cm, apow[..., :t], b_bar,
                               precision=lax.Precision.HIGHEST))
    kf, kb = kimp[0], kimp[1]
    kfull = jnp.concatenate([kb[:, :, :0:-1], kf[:, :, :1] + kb[:, :, :1], kf[:, :, 1:]], axis=2)
    kp = kfull.reshape(S5_PAIRS, 2, hc, 2 * t - 1, hc)
    blk = [kp[:, gi] for gi in range(2)]
    zb = jnp.zeros_like(blk[0])
    strip = jnp.concatenate([jnp.stack([blk[0], zb], axis=3), jnp.stack([zb, blk[1]], axis=3)], axis=1)
    strip = strip.reshape(S5_PAIRS, 2 * hc, (2 * t - 1) * 2 * hc)
    strip = jnp.pad(strip, ((0, 0), (0, 0), (0, 2 * hc)))

    wf = jnp.einsum('gpj,gph->gjhp', apow[0][..., t - 1::-1][..., :t], b_bar[0])
    wb = jnp.einsum('gpj,gph->gjhp', apow[1][..., :t], b_bar[1])
    kinds = [jnp.real(wf), jnp.imag(wf), jnp.real(wb), jnp.imag(wb)]

    def we_pair(kd):
        k5 = kd.reshape(S5_PAIRS, 2, t, hc, p)
        z = jnp.zeros_like(k5[:, 0])
        rows = jnp.stack([jnp.concatenate([k5[:, 0], z], axis=-1), jnp.concatenate([z, k5[:, 1]], axis=-1)], axis=2)
        return rows.reshape(S5_PAIRS, 2 * t * hc, 2 * p)

    we = jnp.concatenate([we_pair(kd) for kd in kinds], axis=-1).astype(BF16)

    vf = jnp.einsum('ghp,gpt->gpth', cm[0], apow[0][..., 1:t + 1])
    vb = jnp.einsum('ghp,gpt->gpth', cm[1], apow[1][..., t:0:-1])
    vkinds = [jnp.real(vf), -jnp.imag(vf), jnp.real(vb), -jnp.imag(vb)]

    def v_pair(kd):
        k5 = kd.reshape(S5_PAIRS, 2, p, t, hc)
        z = jnp.zeros_like(k5[:, 0])
        rows = jnp.concatenate([jnp.stack([k5[:, 0], z], axis=3), jnp.stack([z, k5[:, 1]], axis=3)], axis=1)
        return rows.reshape(S5_PAIRS, 2 * p, 2 * t * hc)

    v1 = jnp.concatenate([v_pair(kd) for kd in vkinds], axis=1)
    v = jnp.concatenate([v1, v1], axis=1).astype(BF16)

    def lanes(z):
        return jnp.tile(z.reshape(1, g * p), (1, batch))

    at = apow[..., t]
    a_tab = jnp.concatenate([lanes(jnp.real(at[0])), lanes(jnp.imag(at[0])),
                             lanes(jnp.real(at[1])), lanes(jnp.imag(at[1]))], axis=0)
    dvec = jnp.tile(d_skip.astype(F32).reshape(S5_PAIRS, 1, 2 * hc), (1, t, 1)).reshape(S5_PAIRS, 1, 2 * t * hc)
    return dict(strip=strip, we=we, v=v, a_tab=a_tab, dvec=dvec)


def _s5_pack_kernel(xa_ref, xb_ref, u_ref, *, n_chunks):
    per_half = S5_PAIRS // 2
    for half, x_ref in enumerate((xa_ref, xb_ref)):
        rows = [x_ref[pl.ds(tau, n_chunks, stride=S5_CHUNK), :] for tau in range(S5_CHUNK)]
        for qq in range(per_half):
            pieces = [r[:, qq * 32:(qq + 1) * 32] for r in rows]
            u_ref[half * per_half + qq] = jnp.concatenate(pieces, axis=-1).astype(BF16)


def _s5_unpack_kernel(y_ref, oa_ref, ob_ref, *, n_chunks):
    per_half = S5_PAIRS // 2
    for half, o_ref in enumerate((oa_ref, ob_ref)):
        ys = [y_ref[half * per_half + qq].astype(F32) for qq in range(per_half)]
        for t in range(S5_CHUNK):
            pieces = [y[:, t * 32:(t + 1) * 32] for y in ys]
            o_ref[pl.ds(t, n_chunks, stride=S5_CHUNK), :] = jnp.concatenate(pieces, axis=-1)


def _s5_pack(sa, sb, batch):
    n_chunks = sa.shape[0] // batch // S5_CHUNK
    rows = n_chunks * S5_CHUNK
    cols = 2 * S5_CHUNK * S5_GROUP_CH
    half = pl.BlockSpec((rows, 128), lambda b: (b, 0))
    return pl.pallas_call(
        functools.partial(_s5_pack_kernel, n_chunks=n_chunks),
        grid=(batch,),
        in_specs=[half, half],
        out_specs=pl.BlockSpec((S5_PAIRS, None, n_chunks, cols), lambda b: (0, b, 0, 0)),
        out_shape=jax.ShapeDtypeStruct((S5_PAIRS, batch, n_chunks, cols), BF16),
        compiler_params=_cparams("arbitrary"),
        name="s5_pack",
    )(sa, sb)


def _s5_unpack(y, batch):
    n_chunks = y.shape[2]
    rows = n_chunks * S5_CHUNK
    cols = y.shape[3]
    half = pl.BlockSpec((rows, 128), lambda b: (b, 0))
    return pl.pallas_call(
        functools.partial(_s5_unpack_kernel, n_chunks=n_chunks),
        grid=(batch,),
        in_specs=[pl.BlockSpec((S5_PAIRS, None, n_chunks, cols), lambda b: (0, b, 0, 0))],
        out_specs=[half, half],
        out_shape=[jax.ShapeDtypeStruct((batch * rows, 128), F32)] * 2,
        compiler_params=_cparams("arbitrary"),
        name="s5_unpack",
    )(y)


def _s5_e_kernel(ul_ref, uc_ref, we_ref, ref_, imf_, reb_, imb_):
    u = jnp.concatenate([ul_ref[...], uc_ref[...]], axis=0)
    e = _dot(u, we_ref[...])
    ref_[...] = e[:, 0:128]
    imf_[...] = e[:, 128:256]
    reb_[...] = e[:, 256:384]
    imb_[...] = e[:, 384:512]


def _s5_scan_kernel(a_ref, ref_, imf_, reb_, imb_, prf, pif, prb, pib, *, n_rows, n_ctx):
    afr = a_ref[0:1, :]
    afi = a_ref[1:2, :]
    abr = a_ref[2:3, :]
    abi = a_ref[3:4, :]
    zero = jnp.zeros_like(afr)

    n_lat = n_rows - n_ctx

    def body(s, carry):
        sfr, sfi, sbr, sbi = carry
        nf = jnp.where(s < n_ctx, n_lat + s, s - n_ctx)
        nb = n_rows - 1 - s
        prf[pl.ds(nf, 1), :] = sfr
        pif[pl.ds(nf, 1), :] = sfi
        prb[pl.ds(nb, 1), :] = sbr
        pib[pl.ds(nb, 1), :] = sbi
        efr = ref_[pl.ds(nf, 1), :]
        efi = imf_[pl.ds(nf, 1), :]
        ebr = reb_[pl.ds(nb, 1), :]
        ebi = imb_[pl.ds(nb, 1), :]
        nfr = afr * sfr - afi * sfi + efr
        nfi = afr * sfi + afi * sfr + efi
        nbr = abr * sbr - abi * sbi + ebr
        nbi = abr * sbi + abi * sbr + ebi
        return nfr, nfi, nbr, nbi

    lax.fori_loop(0, n_rows, body, (zero, zero, zero, zero))


def _s5_y_kernel(ul_ref, uc_ref, strip_ref, v_ref, d_ref, prf, pif, prb, pib, yl_ref, yc_ref, m_scr):
    width = 2 * S5_GROUP_CH
    cols = S5_CHUNK * width
    n_lat = yl_ref.shape[0]

    @pl.when(pl.program_id(1) == 0)
    def _():
        strip = strip_ref[...]
        for j in range(S5_CHUNK):
            off = (S5_CHUNK - 1 - j) * width
            win = strip if off == 0 else pltpu.roll(strip, 2 * cols - off, axis=1)
            m_scr[j * width:(j + 1) * width, :] = win[:, :cols].astype(BF16)

    u = jnp.concatenate([ul_ref[...], uc_ref[...]], axis=0)
    y_intra = _dot(u, m_scr[...])
    pcat = jnp.concatenate([prf[...], pif[...], prb[...], pib[...]], axis=-1)
    hi, lo = _split_bf16(pcat)
    y_cross = _dot(jnp.concatenate([hi, lo], axis=-1), v_ref[...])
    y = y_intra + y_cross + d_ref[...] * u.astype(F32)
    yl_ref[...] = y[:n_lat].astype(BF16)
    yc_ref[...] = y[n_lat:].astype(BF16)


def _s5_core(ul, uc, tabs, layer, batch):
    n_lat, n_ctx = ul.shape[2], uc.shape[2]
    n_rows = n_lat + n_ctx
    width = batch * S5_PAIRS * 128
    cols = 2 * S5_CHUNK * S5_GROUP_CH
    ul_spec = pl.BlockSpec((None, None, n_lat, cols), lambda q, b: (q, b, 0, 0))
    uc_spec = pl.BlockSpec((None, None, n_ctx, cols), lambda q, b: (q, b, 0, 0))
    st_spec = pl.BlockSpec((n_rows, 128), lambda q, b: (0, b * S5_PAIRS + q))
    st_shape = jax.ShapeDtypeStruct((n_rows, width), F32)
    e4 = pl.pallas_call(
        _s5_e_kernel,
        grid=(S5_PAIRS, batch),
        in_specs=[ul_spec, uc_spec, pl.BlockSpec((None, None, cols, 512), lambda q, b: (layer, q, 0, 0))],
        out_specs=[st_spec] * 4,
        out_shape=[st_shape] * 4,
        compiler_params=_cparams("arbitrary", "arbitrary"),
        name="s5_chunk_states",
    )(ul, uc, tabs['we'])
    p4 = pl.pallas_call(
        functools.partial(_s5_scan_kernel, n_rows=n_rows, n_ctx=n_ctx),
        out_shape=[st_shape] * 4,
        compiler_params=pltpu.CompilerParams(vmem_limit_bytes=VMEM_LIMIT_BYTES),
        name="s5_state_scan",
    )(tabs['a_tab'][layer], *e4)
    y = pl.pallas_call(
        _s5_y_kernel,
        grid=(S5_PAIRS, batch),
        in_specs=[
            ul_spec, uc_spec,
            pl.BlockSpec((None, None, 2 * S5_GROUP_CH, 2 * cols), lambda q, b: (layer, q, 0, 0)),
            pl.BlockSpec((None, None, cols, cols), lambda q, b: (layer, q, 0, 0)),
            pl.BlockSpec((None, None, 1, cols), lambda q, b: (layer, q, 0, 0)),
            st_spec, st_spec, st_spec, st_spec,
        ],
        out_specs=[ul_spec, uc_spec],
        out_shape=[
            jax.ShapeDtypeStruct((S5_PAIRS, batch, n_lat, cols), BF16),
            jax.ShapeDtypeStruct((S5_PAIRS, batch, n_ctx, cols), BF16),
        ],
        scratch_shapes=[pltpu.VMEM((cols, cols), BF16)],
        compiler_params=_cparams("arbitrary", "arbitrary"),
        name="s5_outputs",
    )(ul, uc, tabs['strip'], tabs['v'], tabs['dvec'], *p4)
    return y


def _s5_mixer(s_lat, s_ctx, tabs, layer, batch):
    ul = _s5_pack(*s_lat, batch)
    uc = _s5_pack(*s_ctx, batch)
    yl, yc = _s5_core(ul, uc, tabs, layer, batch)
    return _s5_unpack(yl, batch), _s5_unpack(yc, batch)


def _ret_tables(ret_decay):
    c = RET_CHUNK
    lg = jax.nn.log_sigmoid(ret_decay.astype(F32))
    lane_h = np.repeat(np.arange(RET_HEADS), RET_DIM)
    lgl = jnp.repeat(lg, RET_DIM, axis=1)
    pos = jnp.arange(c, dtype=F32)[:, None]
    qd = jnp.stack([jnp.exp((pos + 1.0) * lgl[0][None]), jnp.exp((c - pos) * lgl[1][None])])
    kd = jnp.stack([jnp.exp((c - 1.0 - pos) * lgl[0][None]), jnp.exp(pos * lgl[1][None])])
    bmask = jnp.asarray((lane_h[:, None] == lane_h[None, :]).astype(np.float32))
    cd = jnp.exp(c * lgl)[:, :, None] * bmask[None]
    diff = pos - pos.T
    dm = []
    for h in range(RET_HEADS):
        fw = jnp.where(diff >= 0, jnp.exp(jnp.maximum(diff, 0.0) * lg[0, h]), 0.0)
        bw = jnp.where(diff <= 0, jnp.exp(jnp.maximum(-diff, 0.0) * lg[1, h]), 0.0)
        dm.append(fw + bw)
    dm = jnp.concatenate(dm, axis=0)
    return dict(qd=qd, kd=kd, cd=cd, dm=dm)


def _ret_masks():
    lane_h = np.repeat(np.arange(RET_HEADS), RET_DIM)
    bmask = (lane_h[:, None] == lane_h[None, :]).astype(np.float32)
    hmask = (np.arange(RET_HEADS)[:, None] == lane_h[None, :]).astype(np.float32)
    return jnp.asarray(bmask), jnp.asarray(hmask)


def _rope_tables(n_tokens):
    t = np.arange(n_tokens)
    row = (t // GRID_W).astype(np.float64)
    col = (t % GRID_W).astype(np.float64)
    n_freq = RET_DIM // 4
    inv_freq = 1.0 / (ROPE_BASE ** (np.arange(n_freq, dtype=np.float64) / n_freq))
    ang = np.concatenate([row[:, None] * inv_freq, col[:, None] * inv_freq], axis=-1)
    cos = np.cos(ang)
    sin = np.sin(ang)
    cos_t = np.tile(np.concatenate([cos, cos], axis=-1), (1, RET_HEADS))
    sin_t = np.tile(np.concatenate([-sin, sin], axis=-1), (1, RET_HEADS))
    half = RET_DIM // 2
    perm = np.arange(BRANCH_W) ^ half
    swap = np.zeros((BRANCH_W, BRANCH_W), np.float32)
    swap[perm, np.arange(BRANCH_W)] = 1.0
    return jnp.asarray(cos_t, F32), jnp.asarray(sin_t, F32), jnp.asarray(swap, BF16)


def _ret_chunk(q, k, v, s, qd, kd, cd, bmask, dm, hmask, with_intra):
    cross = _dot((q * qd).astype(BF16), s.astype(BF16))
    s_new = cd * s + bmask * _dot_tn((k * kd).astype(BF16), v)
    if not with_intra:
        return cross, s_new
    qb = q.astype(BF16)
    kb = k.astype(BF16)
    qs = jnp.concatenate([qb * hmask[h:h + 1].astype(BF16) for h in range(RET_HEADS)], axis=0)
    scores = _dot_nt(qs, kb) * dm
    ov = _dot(scores.astype(BF16), v)
    c = q.shape[0]
    inner = ov[0:c] * hmask[0:1]
    for h in range(1, RET_HEADS):
        inner = inner + ov[h * c:(h + 1) * c] * hmask[h:h + 1]
    return inner + cross, s_new


def _ret_kernel(qf_ref, kf_ref, vf_ref, qb_ref, kb_ref, vb_ref, qc_ref, kc_ref, vc_ref,
                cosf_ref, sinf_ref, cosb_ref, sinb_ref, swap_ref,
                qd_ref, kd_ref, cd_ref, bm_ref, dm_ref, hm_ref,
                of_ref, ob_ref, ocf_ref, ocb_ref, sf_scr, sb_scr, *, n_chunks, n_ctx_chunks):
    i = pl.program_id(1)
    c = RET_CHUNK
    k_scale = RET_DIM ** -0.5
    bmask = bm_ref[...]
    dm = dm_ref[...]
    hmask = hm_ref[...]
    tabs = [(qd_ref[d], kd_ref[d], cd_ref[d]) for d in range(2)]

    @pl.when(i == 0)
    def _():
        for d, oc_ref, s_scr in ((0, ocf_ref, sf_scr), (1, ocb_ref, sb_scr)):
            qd, kd, cd = tabs[d]
            s = jnp.zeros((BRANCH_W, BRANCH_W), F32)
            order = range(n_ctx_chunks) if d == 0 else range(n_ctx_chunks - 1, -1, -1)
            for cc in order:
                sl = slice(cc * c, (cc + 1) * c)
                o, s = _ret_chunk(qc_ref[sl, :].astype(F32), kc_ref[sl, :].astype(F32) * k_scale, vc_ref[sl, :],
                                  s, qd, kd, cd, bmask, dm, hmask, d == 0)
                oc_ref[sl, :] = o
            s_scr[...] = s

    swap = swap_ref[...]

    def rope(x_ref, cos_ref, sin_ref):
        xb = x_ref[...]
        return xb.astype(F32) * cos_ref[...] + _dot(xb, swap) * sin_ref[...]

    q_f = rope(qf_ref, cosf_ref, sinf_ref)
    k_f = rope(kf_ref, cosf_ref, sinf_ref) * k_scale
    q_b = rope(qb_ref, cosb_ref, sinb_ref)
    k_b = rope(kb_ref, cosb_ref, sinb_ref) * k_scale
    sf = sf_scr[...]
    sb = sb_scr[...]
    for step in range(n_chunks):
        sl = slice(step * c, (step + 1) * c)
        o, sf = _ret_chunk(q_f[sl], k_f[sl], vf_ref[sl, :], sf, *tabs[0], bmask, dm, hmask, True)
        of_ref[sl, :] = o
        cb = n_chunks - 1 - step
        sl = slice(cb * c, (cb + 1) * c)
        o, sb = _ret_chunk(q_b[sl], k_b[sl], vb_ref[sl, :], sb, *tabs[1], bmask, dm, hmask, False)
        ob_ref[sl, :] = o
    sf_scr[...] = sf
    sb_scr[...] = sb


def _retention(proj_l, proj_c, tabs, layer, masks, rope, batch, seq_len, ctx_len):
    n_chunks = 4
    blk = n_chunks * RET_CHUNK
    nblk = seq_len // blk
    cos_t, sin_t, swap = rope

    def lat(col, back):
        if back:
            return pl.BlockSpec((blk, BRANCH_W), lambda b, i: (b * nblk + nblk - 1 - i, col))
        return pl.BlockSpec((blk, BRANCH_W), lambda b, i: (b * nblk + i, col))

    def ctx(col):
        return pl.BlockSpec((ctx_len, BRANCH_W), lambda b, i: (b, col))

    def const(shape):
        return pl.BlockSpec(shape, lambda b, i: (0,) * len(shape))

    def per_layer(shape):
        return pl.BlockSpec((None,) + shape, lambda b, i: (layer,) + (0,) * len(shape))

    tab_f = pl.BlockSpec((blk, BRANCH_W), lambda b, i: (i, 0))
    tab_b = pl.BlockSpec((blk, BRANCH_W), lambda b, i: (nblk - 1 - i, 0))
    kern = functools.partial(_ret_kernel, n_chunks=n_chunks, n_ctx_chunks=ctx_len // RET_CHUNK)
    c = RET_CHUNK
    ctx_out = pl.BlockSpec((ctx_len, BRANCH_W), lambda b, i: (b, 0))
    o_f, o_b, oc_f, oc_b = pl.pallas_call(
        kern,
        grid=(batch, nblk),
        in_specs=[
            lat(COL_RQ, False), lat(COL_RK, False), lat(COL_RV, False),
            lat(COL_RQ, True), lat(COL_RK, True), lat(COL_RV, True),
            ctx(COL_RQ), ctx(COL_RK), ctx(COL_RV),
            tab_f, tab_f, tab_b, tab_b, const((BRANCH_W, BRANCH_W)),
            per_layer((2, c, BRANCH_W)), per_layer((2, c, BRANCH_W)), per_layer((2, BRANCH_W, BRANCH_W)),
            const((BRANCH_W, BRANCH_W)), per_layer((RET_HEADS * c, c)), const((RET_HEADS, BRANCH_W)),
        ],
        out_specs=[lat(0, False), lat(0, True), ctx_out, ctx_out],
        out_shape=[
            jax.ShapeDtypeStruct((batch * seq_len, BRANCH_W), F32),
            jax.ShapeDtypeStruct((batch * seq_len, BRANCH_W), F32),
            jax.ShapeDtypeStruct((batch * ctx_len, BRANCH_W), F32),
            jax.ShapeDtypeStruct((batch * ctx_len, BRANCH_W), F32),
        ],
        scratch_shapes=[pltpu.VMEM((BRANCH_W, BRANCH_W), F32), pltpu.VMEM((BRANCH_W, BRANCH_W), F32)],
        compiler_params=_cparams("arbitrary", "arbitrary"),
        name="retention",
    )(proj_l, proj_l, proj_l, proj_l, proj_l, proj_l, proj_c, proj_c, proj_c,
      cos_t, sin_t, cos_t, sin_t, swap,
      tabs['qd'], tabs['kd'], tabs['cd'], masks[0], tabs['dm'], masks[1])
    return (o_f, o_b), (oc_f, oc_b)


def _na_tables(rpb):
    kr, kw = NA_WIN_ROWS, NA_WIN_COLS
    col = np.arange(GRID_W)
    col_start = np.clip(col - kw // 2, 0, GRID_W - kw)
    in_win = (col[None, :] >= col_start[:, None]) & (col[None, :] < col_start[:, None] + kw)
    dc = np.clip(col[None, :] - col[:, None], -(kw - 1), kw - 1) + (kw - 1)
    pick_c = (dc[:, :, None] == np.arange(2 * kw - 1)[None, None, :]).astype(np.float32)
    by = jnp.einsum('hrc,qkc->hqrk', rpb.astype(F32), jnp.asarray(pick_c), precision=lax.Precision.HIGHEST)
    by = jnp.where(jnp.asarray(in_win)[None, :, None, :], by, NEG_BIG)
    bias = jnp.stack([by[:, :, v:v + kr, :] for v in range(kr)], axis=0)
    return bias.reshape(kr, NA_HEADS * GRID_W, kr * GRID_W)


def _na_head_mask():
    lane_h = np.repeat(np.arange(NA_HEADS), NA_DIM)
    hmask = (np.arange(NA_HEADS)[:, None] == lane_h[None, :]).astype(np.float32)
    return jnp.asarray(hmask, F32)


def _attend(qs, keys, vals, bias, kc, vc):
    s_ctx = _dot_nt(qs, kc)
    m = jnp.max(s_ctx, axis=-1, keepdims=True)
    if keys is not None:
        s_band = _dot_nt(qs, keys) + bias
        m = jnp.maximum(m, jnp.max(s_band, axis=-1, keepdims=True))
        p_band = jnp.exp(s_band - m)
    p_ctx = jnp.exp(s_ctx - m)
    l = jnp.sum(p_ctx, axis=-1, keepdims=True)
    o = _dot(p_ctx.astype(BF16), vc)
    if keys is not None:
        l = l + jnp.sum(p_band, axis=-1, keepdims=True)
        o = o + _dot(p_band.astype(BF16), vals)
    return o / l


def _stack_heads(q, hmask_scaled):
    return jnp.concatenate([q * hmask_scaled[h:h + 1] for h in range(NA_HEADS)], axis=0)


def _unstack_heads(o, hmask, n):
    out = o[0:n] * hmask[0:1]
    for h in range(1, NA_HEADS):
        out = out + o[h * n:(h + 1) * n] * hmask[h:h + 1]
    return out


def _na_kernel(q_ref, k_ref, v_ref, kc_ref, vc_ref, bias_ref, hm_ref, o_ref, *, n_grid_rows):
    i = pl.program_id(1)
    hmask = hm_ref[...]
    hms = (hmask * (NA_DIM ** -0.5)).astype(BF16)
    kc = kc_ref[...]
    vc = vc_ref[...]
    band = NA_WIN_ROWS * GRID_W
    for rr in range(NA_QROWS):
        r = i * NA_QROWS + rr
        rs = jnp.clip(r - NA_WIN_ROWS // 2, 0, n_grid_rows - NA_WIN_ROWS)
        var = rs - r + (NA_WIN_ROWS - 1)
        start = pl.multiple_of(rs * GRID_W, GRID_W)
        keys = k_ref[pl.ds(start, band), :]
        vals = v_ref[pl.ds(start, band), :]
        qs = _stack_heads(q_ref[rr * GRID_W:(rr + 1) * GRID_W, :], hms)
        o = _attend(qs, keys, vals, bias_ref[var], kc, vc)
        o_ref[rr * GRID_W:(rr + 1) * GRID_W, :] = _unstack_heads(o, hmask, GRID_W).astype(BF16)


def _na_ctx_kernel(q_ref, kc_ref, vc_ref, hm_ref, o_ref):
    hmask = hm_ref[...]
    hms = (hmask * (NA_DIM ** -0.5)).astype(BF16)
    n = q_ref.shape[0]
    o = _attend(_stack_heads(q_ref[...], hms), None, None, None, kc_ref[...], vc_ref[...])
    o_ref[...] = _unstack_heads(o, hmask, n).astype(BF16)


def _neighborhood(proj_l, proj_c, bias, layer, hmask, batch, seq_len, ctx_len, need_ctx_out):
    rows = seq_len // GRID_W
    qblk = NA_QROWS * GRID_W
    nq = seq_len // qblk
    out_l = pl.pallas_call(
        functools.partial(_na_kernel, n_grid_rows=rows),
        grid=(batch, nq),
        in_specs=[
            pl.BlockSpec((qblk, BRANCH_W), lambda b, i: (b * nq + i, COL_NQ)),
            pl.BlockSpec((seq_len, BRANCH_W), lambda b, i: (b, COL_NK)),
            pl.BlockSpec((seq_len, BRANCH_W), lambda b, i: (b, COL_NV)),
            pl.BlockSpec((ctx_len, BRANCH_W), lambda b, i: (b, COL_NK)),
            pl.BlockSpec((ctx_len, BRANCH_W), lambda b, i: (b, COL_NV)),
            pl.BlockSpec((None,) + bias.shape[1:], lambda b, i: (layer, 0, 0, 0)),
            pl.BlockSpec(hmask.shape, lambda b, i: (0, 0)),
        ],
        out_specs=pl.BlockSpec((qblk, BRANCH_W), lambda b, i: (b * nq + i, 0)),
        out_shape=jax.ShapeDtypeStruct((batch * seq_len, BRANCH_W), BF16),
        compiler_params=_cparams("arbitrary", "arbitrary"),
        name="neighborhood_attn",
    )(proj_l, proj_l, proj_l, proj_c, proj_c, bias, hmask)
    out_c = None
    if need_ctx_out:
        out_c = pl.pallas_call(
            _na_ctx_kernel,
            grid=(batch,),
            in_specs=[
                pl.BlockSpec((ctx_len, BRANCH_W), lambda b: (b, COL_NQ)),
                pl.BlockSpec((ctx_len, BRANCH_W), lambda b: (b, COL_NK)),
                pl.BlockSpec((ctx_len, BRANCH_W), lambda b: (b, COL_NV)),
                pl.BlockSpec(hmask.shape, lambda b: (0, 0)),
            ],
            out_specs=pl.BlockSpec((ctx_len, BRANCH_W), lambda b: (b, 0)),
            out_shape=jax.ShapeDtypeStruct((batch * ctx_len, BRANCH_W), BF16),
            compiler_params=_cparams("arbitrary"),
            name="context_attn",
        )(proj_c, proj_c, proj_c, hmask)
    return out_l, out_c


def _merge_kernel(x_ref, mod_ref, g_ref, gt0, gt1, gt2, gt3, fa_ref, fb_ref, s5a_ref, s5b_ref, rof_ref, rob_ref, rg_ref, na_ref,
                  wglu_ref, bglu_ref, gn_ref, avg_ref, wb_ref, wo_ref, o_ref, *, tiles_per_mod, mod_base):
    i = pl.program_id(0)
    _, _, gate_a = _mod_rows(mod_ref, i, tiles_per_mod, mod_base, 0)
    z = _gelu_tanh(jnp.concatenate([s5a_ref[...], s5b_ref[...]], axis=-1)).astype(BF16)
    zf = z.astype(F32)
    b_s5 = (zf * _sigmoid(_dot(z, wglu_ref[...]) + bglu_ref[...])).astype(BF16)
    o = rof_ref[...] + rob_ref[...]
    avg = avg_ref[...]
    hi, lo = _split_bf16(o)
    mu = _dot(hi, avg) + _dot(lo, avg)
    dlt = o - mu
    hi, lo = _split_bf16(dlt * dlt)
    var = _dot(hi, avg) + _dot(lo, avg)
    hn = dlt * lax.rsqrt(var + EPS) * gn_ref[...]
    b_ret = (_silu(rg_ref[...].astype(F32)) * hn).astype(BF16)
    b_fnet = jnp.concatenate([fa_ref[...], fb_ref[...]], axis=-1).astype(BF16)
    outs = (b_fnet, b_s5, b_ret, na_ref[...])
    gates = (gt0, gt1, gt2, gt3)
    y = (1.0 + jnp.tanh(gates[0][...].astype(F32))) * _dot(outs[0], wb_ref[0])
    for b in range(1, N_BRANCH):
        y = y + (1.0 + jnp.tanh(gates[b][...].astype(F32))) * _dot(outs[b], wb_ref[b])
    yo = _dot(y.astype(BF16), wo_ref[...])
    o_ref[...] = x_ref[...] + gate_a * _rms(yo, g_ref[...])


def _merge(x, mod, g1, proj, a, s5y, ret_o, na, lw, *, rows_per_mod, mod_base):
    rows, d = x.shape
    tm = min(512, rows)
    nt = rows // tm

    def row(shape, col=0):
        return pl.BlockSpec(shape, lambda i: (i, col))

    def const(arr):
        return pl.BlockSpec(arr.shape, lambda i: (0,) * arr.ndim)

    kern = functools.partial(_merge_kernel, tiles_per_mod=max(rows_per_mod // tm, 1), mod_base=mod_base)
    ins = [x, mod, g1.reshape(1, d), proj, proj, proj, proj, a[0], a[1], s5y[0], s5y[1], ret_o[0], ret_o[1], proj, na,
           lw['w_glu'], lw['b_glu'], lw['ret_gn'], lw['avg'], lw['w_branch'], lw['w_out']]
    specs = [
        row((tm, d)), const(mod), pl.BlockSpec((1, d), lambda i: (0, 0)),
        row((tm, d), 0), row((tm, d), 1), row((tm, d), 2), row((tm, d), 3),
        row((tm, 128)), row((tm, 128)), row((tm, 128)), row((tm, 128)),
        row((tm, BRANCH_W)), row((tm, BRANCH_W)),
        row((tm, BRANCH_W), COL_RG), row((tm, BRANCH_W)),
        const(lw['w_glu']), const(lw['b_glu']), const(lw['ret_gn']), const(lw['avg']),
        const(lw['w_branch']), const(lw['w_out']),
    ]
    return pl.pallas_call(
        kern,
        grid=(nt,),
        in_specs=specs,
        out_specs=row((tm, d)),
        out_shape=jax.ShapeDtypeStruct((rows, d), F32),
        compiler_params=_cparams("arbitrary"),
        name="merge_out",
    )(*ins)


def _ffn_kernel(x_ref, mod_ref, g2_ref, g3_ref, wg_ref, wu_ref, wd_ref, o_ref, *, tiles_per_mod, mod_base):
    i = pl.program_id(0)
    sh, sc, gate_f = _mod_rows(mod_ref, i, tiles_per_mod, mod_base, 3)
    x = x_ref[...]
    h = (_rms(x, g2_ref[...]) * (1.0 + sc) + sh).astype(BF16)
    act = (_silu(_dot(h, wg_ref[...])) * _dot(h, wu_ref[...])).astype(BF16)
    y = _dot(act, wd_ref[...])
    o_ref[...] = x + gate_f * _rms(y, g3_ref[...])


def _ffn_dense(x, mod, g2, g3, wg, wu, wd, *, rows_per_mod, mod_base):
    rows, d = x.shape
    d_ff = wg.shape[1]
    tm = min(512, rows)
    kern = functools.partial(_ffn_kernel, tiles_per_mod=max(rows_per_mod // tm, 1), mod_base=mod_base)

    def resident(shape):
        return pl.BlockSpec(shape, lambda i: (0, 0), pipeline_mode=pl.Buffered(1))

    return pl.pallas_call(
        kern,
        grid=(rows // tm,),
        in_specs=[
            pl.BlockSpec((tm, d), lambda i: (i, 0)),
            pl.BlockSpec(mod.shape, lambda i: (0, 0)),
            pl.BlockSpec((1, d), lambda i: (0, 0)),
            pl.BlockSpec((1, d), lambda i: (0, 0)),
            resident((d, d_ff)), resident((d, d_ff)), resident((d_ff, d)),
        ],
        out_specs=pl.BlockSpec((tm, d), lambda i: (i, 0)),
        out_shape=jax.ShapeDtypeStruct((rows, d), F32),
        compiler_params=_cparams("arbitrary"),
        name="ffn_dense",
    )(x, mod, g2.reshape(1, d), g3.reshape(1, d), wg, wu, wd)


def _router_kernel(x_ref, mod_ref, g2_ref, wr_ref, br_ref, tri_ref, h_ref, comb_ref, plan_ref, cnt_ref, cnt_scr,
                   *, tiles_per_mod, mod_base):
    i = pl.program_id(0)

    @pl.when(i == 0)
    def _():
        cnt_scr[...] = jnp.zeros_like(cnt_scr)

    sh, sc, _ = _mod_rows(mod_ref, i, tiles_per_mod, mod_base, 3)
    h = _rms(x_ref[...], g2_ref[...]) * (1.0 + sc) + sh
    h_ref[...] = _pack_pairs(h)
    h_hi, h_lo = _split_bf16(h)
    w_hi, w_lo = _split_bf16(wr_ref[...])
    logits = _dot(h_hi, w_hi) + _dot(h_lo, w_hi) + _dot(h_hi, w_lo) + br_ref[...]
    lane = lax.broadcasted_iota(jnp.int32, logits.shape, 1)
    v1 = jnp.max(logits, axis=-1, keepdims=True)
    i1 = jnp.min(jnp.where(logits == v1, lane, 128), axis=-1, keepdims=True)
    rest = jnp.where(lane == i1, NEG_BIG, logits)
    v2 = jnp.max(rest, axis=-1, keepdims=True)
    i2 = jnp.min(jnp.where(rest == v2, lane, 128), axis=-1, keepdims=True)
    e = jnp.exp(v2 - v1)
    w1 = 1.0 / (1.0 + e)
    w2 = e / (1.0 + e)
    meta = jnp.where(lane == 0, i1.astype(F32), 0.0) + jnp.where(lane == 1, i2.astype(F32), 0.0)
    meta = meta + jnp.where(lane == 2, w1, 0.0) + jnp.where(lane == 3, w2, 0.0)
    member = jnp.where((lane == i1) | (lane == i2), 1.0, 0.0)
    before = _dot(tri_ref[...], member.astype(BF16)) + cnt_scr[...]
    rank1 = jnp.sum(jnp.where(lane == i1, before, 0.0), axis=-1, keepdims=True)
    rank2 = jnp.sum(jnp.where(lane == i2, before, 0.0), axis=-1, keepdims=True)
    cnt_scr[...] += jnp.sum(member, axis=0, keepdims=True)
    cnt_ref[...] = cnt_scr[...]
    meta = meta + jnp.where(lane == 4, rank1, 0.0) + jnp.where(lane == 5, rank2, 0.0)
    comb_ref[...] = meta[:, :MOE_META_W]
    plan_ref[...] = meta.T[:MOE_META_W]


def _router(x, mod, g2, w_router, b_router, *, rows_per_mod, mod_base):
    rows, d = x.shape
    tm = min(512, rows)
    wr = jnp.zeros((d, 128), F32).at[:, :N_EXPERTS].set(w_router)
    br = jnp.full((1, 128), NEG_BIG, F32).at[0, :N_EXPERTS].set(b_router)
    tri = jnp.asarray(np.tril(np.ones((tm, tm), np.float32), -1), BF16)
    kern = functools.partial(_router_kernel, tiles_per_mod=max(rows_per_mod // tm, 1), mod_base=mod_base)
    return pl.pallas_call(
        kern,
        grid=(rows // tm,),
        in_specs=[
            pl.BlockSpec((tm, d), lambda i: (i, 0)),
            pl.BlockSpec(mod.shape, lambda i: (0, 0)),
            pl.BlockSpec((1, d), lambda i: (0, 0)),
            pl.BlockSpec((d, 128), lambda i: (0, 0)),
            pl.BlockSpec((1, 128), lambda i: (0, 0)),
            pl.BlockSpec((tm, tm), lambda i: (0, 0)),
        ],
        out_specs=[
            pl.BlockSpec((tm, d // 2), lambda i: (i, 0)),
            pl.BlockSpec((tm, MOE_META_W), lambda i: (i, 0)),
            pl.BlockSpec((MOE_META_W, tm), lambda i: (0, i)),
            pl.BlockSpec((1, 128), lambda i: (0, 0)),
        ],
        out_shape=[
            jax.ShapeDtypeStruct((rows, d // 2), jnp.int32),
            jax.ShapeDtypeStruct((rows, MOE_META_W), F32),
            jax.ShapeDtypeStruct((MOE_META_W, rows), F32),
            jax.ShapeDtypeStruct((1, 128), F32),
        ],
        scratch_shapes=[pltpu.VMEM((1, 128), F32)],
        compiler_params=_cparams("arbitrary"),
        name="moe_router",
    )(x, mod, g2.reshape(1, d), wr, br, tri)


def _sc_gather(table, idx):
    n_idx = idx.shape[0]
    width = table.shape[1]
    per_worker = n_idx // SC_WORKERS
    chunk_rows = math.gcd(per_worker, SC_GATHER_ROWS)
    n_chunks = per_worker // chunk_rows
    assert per_worker * SC_WORKERS == n_idx and chunk_rows % 8 == 0
    mesh = plsc.VectorSubcoreMesh(core_axis_name="c", subcore_axis_name="s")

    assert n_chunks % 2 == 0
    buf = [pltpu.VMEM((chunk_rows,), jnp.int32), pltpu.VMEM((chunk_rows, width), table.dtype),
           pltpu.SemaphoreType.DMA, pltpu.SemaphoreType.DMA]

    @functools.partial(
        pl.kernel, mesh=mesh,
        out_type=jax.ShapeDtypeStruct((n_idx, width), table.dtype),
        scratch_types=buf + buf,
        name="sc_row_gather",
    )
    def gather(table_hbm, idx_hbm, out_hbm, idx0, rows0, g0, w0, idx1, rows1, g1, w1):
        wid = lax.axis_index("s") * SC_CORES + lax.axis_index("c")
        base = wid * per_worker
        slots = ((idx0, rows0, g0, w0), (idx1, rows1, g1, w1))

        def fetch(j, slot):
            idx_v, rows_v, g, _ = slots[slot]
            pltpu.sync_copy(idx_hbm.at[pl.ds(base + j * chunk_rows, chunk_rows)], idx_v)
            pltpu.make_async_copy(table_hbm.at[idx_v], rows_v, g).start()

        def store(j, slot):
            idx_v, rows_v, g, w = slots[slot]
            pltpu.make_async_copy(table_hbm.at[idx_v], rows_v, g).wait()
            pltpu.make_async_copy(rows_v, out_hbm.at[pl.ds(base + j * chunk_rows, chunk_rows)], w).start()

        def drain(j, slot):
            _, rows_v, _, w = slots[slot]
            pltpu.make_async_copy(rows_v, out_hbm.at[pl.ds(base + j * chunk_rows, chunk_rows)], w).wait()

        fetch(0, 0)

        @pl.loop(0, n_chunks // 2)
        def _(jj):
            j = 2 * jj

            @pl.when(jj > 0)
            def _():
                drain(j - 1, 1)

            fetch(j + 1, 1)
            store(j, 0)

            @pl.when(j + 2 < n_chunks)
            def _():
                drain(j, 0)
                fetch(j + 2, 0)

            store(j + 1, 1)

        drain(n_chunks - 2, 0)
        drain(n_chunks - 1, 1)

    return gather(table, idx)


def _sc_scatter(table, idx, n_out):
    n_idx = idx.shape[0]
    rows, width = table.shape
    per_worker = n_idx // SC_WORKERS
    chunk_rows = math.gcd(per_worker, SC_GATHER_ROWS)
    n_chunks = per_worker // chunk_rows
    assert per_worker * SC_WORKERS == n_idx and chunk_rows % 8 == 0 and rows % per_worker == 0
    mesh = plsc.VectorSubcoreMesh(core_axis_name="c", subcore_axis_name="s")

    assert n_chunks % 2 == 0
    buf = [pltpu.VMEM((chunk_rows,), jnp.int32), pltpu.VMEM((chunk_rows, width), table.dtype),
           pltpu.SemaphoreType.DMA, pltpu.SemaphoreType.DMA]

    @functools.partial(
        pl.kernel, mesh=mesh,
        out_type=jax.ShapeDtypeStruct((n_out, width), table.dtype),
        scratch_types=buf + buf,
        name="sc_row_scatter",
    )
    def scatter(table_hbm, idx_hbm, out_hbm, idx0, rows0, l0, w0, idx1, rows1, l1, w1):
        wid = lax.axis_index("s") * SC_CORES + lax.axis_index("c")
        base = wid * per_worker
        slots = ((idx0, rows0, l0, w0), (idx1, rows1, l1, w1))

        def src(j):
            return table_hbm.at[pl.ds(lax.rem(base + j * chunk_rows, rows), chunk_rows)]

        def fetch(j, slot):
            idx_v, rows_v, l, _ = slots[slot]
            pltpu.sync_copy(idx_hbm.at[pl.ds(base + j * chunk_rows, chunk_rows)], idx_v)
            pltpu.make_async_copy(src(j), rows_v, l).start()

        def store(j, slot):
            idx_v, rows_v, l, w = slots[slot]
            pltpu.make_async_copy(src(j), rows_v, l).wait()
            pltpu.make_async_copy(rows_v, out_hbm.at[idx_v], w).start()

        def drain(slot):
            idx_v, rows_v, _, w = slots[slot]
            pltpu.make_async_copy(rows_v, out_hbm.at[idx_v], w).wait()

        fetch(0, 0)

        @pl.loop(0, n_chunks // 2)
        def _(jj):
            j = 2 * jj

            @pl.when(jj > 0)
            def _():
                drain(1)

            fetch(j + 1, 1)
            store(j, 0)

            @pl.when(j + 2 < n_chunks)
            def _():
                drain(0)
                fetch(j + 2, 0)

            store(j + 1, 1)

        drain(0)
        drain(1)

    return scatter(table, idx)


def _moe_plan(plan, counts_row, rows):
    tile = MOE_ROW_TILE
    n_tiles = (2 * rows) // tile + N_EXPERTS
    n_slots = n_tiles * tile
    counts = counts_row[0, :N_EXPERTS].astype(jnp.int32)
    padded = ((counts + tile - 1) // tile) * tile
    ends = jnp.cumsum(padded)
    starts = ends - padded
    ids = jnp.arange(N_EXPERTS, dtype=F32)[:, None]
    start_f = starts.astype(F32)[:, None]

    def slot(e_row, r_row):
        return jnp.sum(jnp.where(e_row[None, :] == ids, start_f, 0.0), axis=0) + r_row

    pos = jnp.concatenate([slot(plan[0], plan[4]), slot(plan[1], plan[5])])
    tile_start = jnp.arange(n_tiles, dtype=jnp.int32) * tile
    used = tile_start < ends[-1]
    tile_e = jnp.minimum(jnp.sum((tile_start[:, None] >= ends[None, :]).astype(jnp.int32), axis=1), N_EXPERTS - 1)
    last_e = jnp.max(jnp.where(used, tile_e, 0))
    tile_e = jnp.where(used, tile_e, last_e)
    valid_end = jnp.sum((tile_e[:, None] == jnp.arange(N_EXPERTS)[None, :]) * (starts + counts)[None, :], axis=1)
    n_valid = jnp.where(used, jnp.clip(valid_end - tile_start, 0, tile), 0).astype(jnp.int32)
    return pos.astype(jnp.int32), n_slots, tile_e.astype(jnp.int32), n_valid


def _moe_group_kernel(eid_ref, nval_ref, hs_ref, *refs, n_f):
    wg_refs = refs[:MOE_W_SPLIT]
    wu_refs = refs[MOE_W_SPLIT:2 * MOE_W_SPLIT]
    wd_refs = refs[2 * MOE_W_SPLIT:3 * MOE_W_SPLIT]
    y_ref, acc_scr = refs[3 * MOE_W_SPLIT:]
    w = pl.program_id(0)
    f = pl.program_id(1)
    nv = nval_ref[w]

    def run(n_rows):
        rows = slice(0, n_rows)

        hv = _unpack_pairs(hs_ref[rows, :])
        row = lax.broadcasted_iota(jnp.int32, hv.shape, 0)
        h = jnp.where(row < nv, hv, 0.0).astype(BF16)
        gate = jnp.concatenate([_dot(h, r[...].astype(BF16)) for r in wg_refs], axis=-1)
        up = jnp.concatenate([_dot(h, r[...].astype(BF16)) for r in wu_refs], axis=-1)
        act = (_silu(gate) * up).astype(BF16)
        piece = act.shape[1] // MOE_W_SPLIT
        part = _dot(act[:, :piece], wd_refs[0][...].astype(BF16))
        for s in range(1, MOE_W_SPLIT):
            part = part + _dot(act[:, s * piece:(s + 1) * piece], wd_refs[s][...].astype(BF16))
        acc = jnp.where(f == 0, 0.0, acc_scr[rows, :]) + part
        acc_scr[rows, :] = acc
        y_ref[rows, :] = _pack_pairs(acc)

    half = hs_ref.shape[0] // 2

    @pl.when(nv > half)
    def _():
        run(hs_ref.shape[0])

    @pl.when((nv > 0) & (nv <= half))
    def _():
        run(half)


def _moe_grouped(hs, tile_e, n_valid, wg, wu, wd):
    n_slots = hs.shape[0]
    d = wg.shape[1]
    d_ff = wg.shape[2]
    tile = MOE_ROW_TILE
    tf = MOE_FF_TILE
    n_f = d_ff // tf

    def f_idx(f, nval, w):
        return jnp.where(nval[w] > 0, f, n_f - 1)

    grid_spec = pltpu.PrefetchScalarGridSpec(
        num_scalar_prefetch=2,
        grid=(n_slots // tile, n_f),
        in_specs=[
            pl.BlockSpec((tile, d // 2), lambda w, f, eid, nval: (w, 0)),
            *[pl.BlockSpec((None, d, tf // MOE_W_SPLIT),
                           lambda w, f, eid, nval, s=s: (eid[w], 0, f_idx(f, nval, w) * MOE_W_SPLIT + s))
              for _ in range(2) for s in range(MOE_W_SPLIT)],
            *[pl.BlockSpec((None, tf // MOE_W_SPLIT, d),
                           lambda w, f, eid, nval, s=s: (eid[w], f_idx(f, nval, w) * MOE_W_SPLIT + s, 0))
              for s in range(MOE_W_SPLIT)],
        ],
        out_specs=pl.BlockSpec((tile, d // 2), lambda w, f, eid, nval: (w, 0)),
        scratch_shapes=[pltpu.VMEM((tile, d), F32)],
    )
    return pl.pallas_call(
        functools.partial(_moe_group_kernel, n_f=n_f),
        grid_spec=grid_spec,
        out_shape=jax.ShapeDtypeStruct((n_slots, d // 2), jnp.int32),
        compiler_params=_cparams("arbitrary", "arbitrary"),
        name="moe_experts",
    )(tile_e, n_valid, hs, *([wg] * MOE_W_SPLIT), *([wu] * MOE_W_SPLIT), *([wd] * MOE_W_SPLIT))


def _moe_out_kernel(x_ref, y1_ref, y2_ref, meta_ref, mod_ref, g3_ref, o_ref, *, tiles_per_mod, mod_base):
    i = pl.program_id(0)
    _, _, gate_f = _mod_rows(mod_ref, i, tiles_per_mod, mod_base, 3)
    meta = meta_ref[...]
    y = meta[:, 2:3] * _unpack_pairs(y1_ref[...]) + meta[:, 3:4] * _unpack_pairs(y2_ref[...])
    o_ref[...] = x_ref[...] + gate_f * _rms(y, g3_ref[...])


def _moe_combine(x, yg, meta, mod, g3, *, rows_per_mod, mod_base):
    rows, d = x.shape
    tm = min(512, rows)
    nt = rows // tm
    kern = functools.partial(_moe_out_kernel, tiles_per_mod=max(rows_per_mod // tm, 1), mod_base=mod_base)
    return pl.pallas_call(
        kern,
        grid=(nt,),
        in_specs=[
            pl.BlockSpec((tm, d), lambda i: (i, 0)),
            pl.BlockSpec((tm, d // 2), lambda i: (i, 0)),
            pl.BlockSpec((tm, d // 2), lambda i: (nt + i, 0)),
            pl.BlockSpec((tm, MOE_META_W), lambda i: (i, 0)),
            pl.BlockSpec(mod.shape, lambda i: (0, 0)),
            pl.BlockSpec((1, d), lambda i: (0, 0)),
        ],
        out_specs=pl.BlockSpec((tm, d), lambda i: (i, 0)),
        out_shape=jax.ShapeDtypeStruct((rows, d), F32),
        compiler_params=_cparams("arbitrary"),
        name="moe_combine",
    )(x, yg, yg, meta, mod, g3.reshape(1, d))


def _moe_sparse(x, routed, mod, g3, wg, wu, wd, *, rows_per_mod, mod_base):
    h, meta, plan, counts = routed
    rows = x.shape[0]
    pos, n_slots, tile_e, n_valid = _moe_plan(plan, counts, rows)
    hs = _sc_scatter(h, pos, n_slots)
    ys = _moe_grouped(hs, tile_e, n_valid, wg, wu, wd)
    yg = _sc_gather(ys, pos)
    return _moe_combine(x, yg, meta, mod, g3, rows_per_mod=rows_per_mod, mod_base=mod_base)


def _cast_kernel(w_ref, o_ref, *, scale):
    w = w_ref[...]
    o_ref[...] = (w if scale == 1.0 else w * scale).astype(BF16)


def _cast_bf16(w_stack, layer, scale=1.0):
    squeeze = w_stack.ndim == 3
    w4 = w_stack[:, None] if squeeze else w_stack
    _, n_e, k, n = w4.shape
    bk = min(k, 256)
    out = pl.pallas_call(
        functools.partial(_cast_kernel, scale=scale),
        grid=(n_e, k // bk),
        in_specs=[pl.BlockSpec((None, None, bk, n), lambda e, i: (layer, e, i, 0))],
        out_specs=pl.BlockSpec((None, bk, n), lambda e, i: (e, i, 0)),
        out_shape=jax.ShapeDtypeStruct((n_e, k, n), BF16),
        compiler_params=_cparams("arbitrary", "arbitrary"),
        name="cast_weights",
    )(w4)
    return out[0] if squeeze else out


def _permute_w_in(w_in_stack, layer):
    _, k, n = w_in_stack.shape
    n_blocks = n // BRANCH_W
    shift = 9
    n_gate_blocks = N_BRANCH * D_MODEL // BRANCH_W

    per_step = 5
    assert n_blocks % per_step == 0

    def permute_kernel(*refs):
        o_ref = refs[-1]
        for s, w_ref in enumerate(refs[:-1]):
            scale = jnp.where(pl.program_id(0) * per_step + s < n_gate_blocks, 0.5, 1.0)
            o_ref[:, s * BRANCH_W:(s + 1) * BRANCH_W] = (w_ref[...] * scale).astype(BF16)

    def src(s):
        return pl.BlockSpec((None, k, BRANCH_W), lambda j: (layer, 0, (j * per_step + s + shift) % n_blocks))

    return pl.pallas_call(
        permute_kernel,
        grid=(n_blocks // per_step,),
        in_specs=[src(s) for s in range(per_step)],
        out_specs=pl.BlockSpec((k, per_step * BRANCH_W), lambda j: (0, j)),
        out_shape=jax.ShapeDtypeStruct((k, n), BF16),
        compiler_params=_cparams("arbitrary"),
        name="cast_permute_w_in",
    )(*([w_in_stack] * per_step))


def kernel(x, c, ctx, c_ctx, w_mod, b_mod, norm_g, w_in, s5_a_re, s5_a_im, s5_log_dt, s5_b_re, s5_b_im, s5_c_re, s5_c_im, s5_d, s5_w_glu, s5_b_glu, ret_decay, ret_gn, na_rpb, w_branch, w_out, ffn_w_gate, ffn_w_up, ffn_w_down, moe_w_router, moe_b_router, moe_w_gate, moe_w_up, moe_w_down):
    batch, seq_len, d = x.shape
    ctx_len = ctx.shape[1]
    depth = w_mod.shape[0]
    cond = jnp.concatenate([c, c_ctx[None, :]], axis=0)
    mod_all = _modulation(cond, w_mod, b_mod)
    rope = _rope_tables(seq_len)
    lane_h = np.repeat(np.arange(RET_HEADS), RET_DIM)
    avg = jnp.asarray((lane_h[:, None] == lane_h[None, :]).astype(np.float32) / RET_DIM, BF16)

    xl = x.reshape(batch * seq_len, d)
    xc = ctx.reshape(batch * ctx_len, d)
    lat = dict(rows_per_mod=seq_len, mod_base=0)
    cxt = dict(rows_per_mod=batch * ctx_len, mod_base=batch)

    s5_tabs = jax.vmap(functools.partial(_s5_tables, batch=batch))(
        s5_a_re, s5_a_im, s5_log_dt, s5_b_re, s5_b_im, s5_c_re, s5_c_im, s5_d)
    ret_tabs = jax.vmap(_ret_tables)(ret_decay)
    ret_masks = _ret_masks()
    na_bias = jax.vmap(_na_tables)(na_rpb)
    na_hmask = _na_head_mask()

    for layer in range(depth):
        last = layer == depth - 1
        need_ctx = not last
        mod = mod_all[layer]
        ng = norm_g[layer]
        w_in_bf = _permute_w_in(w_in, layer)
        lw = dict(w_glu=s5_w_glu[layer].astype(BF16), b_glu=s5_b_glu[layer].reshape(1, BRANCH_W).astype(F32),
                  ret_gn=ret_gn[layer].reshape(1, BRANCH_W).astype(F32), avg=avg,
                  w_branch=_cast_bf16(w_branch, layer, 0.5), w_out=_cast_bf16(w_out, layer))

        proj_l, f_l, *s_in_l = _in_proj(xl, mod, ng[0], w_in_bf, **lat)
        proj_c, f_c, *s_in_c = _in_proj(xc, mod, ng[0], w_in_bf, **cxt)

        a_l = _fourier_latent(f_l, batch, seq_len)
        s_l, s_c = _s5_mixer(s_in_l, s_in_c, s5_tabs, layer, batch)
        r_l, r_c = _retention(proj_l, proj_c, ret_tabs, layer, ret_masks, rope, batch, seq_len, ctx_len)
        n_l, n_c = _neighborhood(proj_l, proj_c, na_bias, layer, na_hmask, batch, seq_len, ctx_len, need_ctx)

        xl = _merge(xl, mod, ng[1], proj_l, a_l, s_l, r_l, n_l, lw, **lat)
        if need_ctx:
            a_c = _fourier_ctx(f_c, batch, ctx_len)
            xc = _merge(xc, mod, ng[1], proj_c, a_c, s_c, r_c, n_c, lw, **cxt)

        i = layer // 2
        if layer % 2 == 0:
            wg, wu, wd = _cast_bf16(ffn_w_gate, i), _cast_bf16(ffn_w_up, i), _cast_bf16(ffn_w_down, i)
            xl = _ffn_dense(xl, mod, ng[2], ng[3], wg, wu, wd, **lat)
            if need_ctx:
                xc = _ffn_dense(xc, mod, ng[2], ng[3], wg, wu, wd, **cxt)
        else:
            wg, wu, wd = moe_w_gate[i], moe_w_up[i], moe_w_down[i]
            routed = _router(xl, mod, ng[2], moe_w_router[i], moe_b_router[i], **lat)
            xl = _moe_sparse(xl, routed, mod, ng[3], wg, wu, wd, **lat)
            if need_ctx:
                routed_c = _router(xc, mod, ng[2], moe_w_router[i], moe_b_router[i], **cxt)
                xc = _moe_sparse(xc, routed_c, mod, ng[3], wg, wu, wd, **cxt)
    return xl.reshape(batch, seq_len, d)
```

```python
import functools
import math

import numpy as np
import jax
import jax.numpy as jnp
from jax import lax
from jax.experimental import pallas as pl
from jax.experimental.pallas import tpu as pltpu
from jax.experimental.pallas import tpu_sc as plsc

F32 = jnp.float32
BF16 = jnp.bfloat16

D_MODEL = 1024
BRANCH_W = 256
N_BRANCH = 4
GRID_W = 64
FNET_GROUP_DIM = 64
S5_GROUP_CH = 16
S5_GROUPS = 16
S5_STATE = 64
S5_CHUNK = 32
S5_PAIRS = S5_GROUPS // 2
RET_HEADS = 4
RET_DIM = 64
RET_CHUNK = 128
NA_HEADS = 4
NA_DIM = 64
NA_WIN_ROWS = 8
NA_WIN_COLS = 16
NA_QROWS = 8
ROPE_BASE = 10000.0
N_EXPERTS = 8
EPS = 1e-6
FFT_N2 = 256
NEG_BIG = -1e30
VMEM_LIMIT_BYTES = 50 * 1024 * 1024
SC_CORES = 2
SC_SUBCORES = 16
SC_WORKERS = SC_CORES * SC_SUBCORES
SC_GATHER_ROWS = 64
MOE_ROW_TILE = 2048
MOE_SUB_ROWS = 1024
MOE_FF_TILE = 512
MOE_META_W = 8

COL_F, COL_S, COL_RQ, COL_RK, COL_RV, COL_RG, COL_NQ, COL_NK, COL_NV = range(16, 25)
IN_W = 9 * BRANCH_W + N_BRANCH * D_MODEL
IN_TN = 1280
IN_F_TILE = (N_BRANCH * D_MODEL) // IN_TN
IN_F_OFF = N_BRANCH * D_MODEL - IN_F_TILE * IN_TN
IN_S_OFF = IN_F_OFF + BRANCH_W


def _cparams(*sem):
    return pltpu.CompilerParams(dimension_semantics=sem, vmem_limit_bytes=VMEM_LIMIT_BYTES)


def _sigmoid(v):
    return 0.5 * jnp.tanh(0.5 * v) + 0.5


def _silu(v):
    return v * _sigmoid(v)


def _gelu_tanh(v):
    return 0.5 * v * (1.0 + jnp.tanh(math.sqrt(2.0 / math.pi) * (v + 0.044715 * (v * v * v))))


def _rms(v, g):
    ms = jnp.mean(v * v, axis=-1, keepdims=True)
    return v * lax.rsqrt(ms + EPS) * g


def _split_bf16(v):
    hi = v.astype(BF16)
    lo = (v - hi.astype(F32)).astype(BF16)
    return hi, lo


def _pack_pairs(v):
    n = v.shape[1] // 2
    lo = lax.bitcast_convert_type(v[:, :n].astype(BF16).astype(F32), jnp.int32)
    hi = lax.bitcast_convert_type(v[:, n:].astype(BF16).astype(F32), jnp.int32)
    return (hi & -65536) | ((lo >> 16) & 65535)


def _unpack_pairs(w):
    lo = lax.bitcast_convert_type(w << 16, F32)
    hi = lax.bitcast_convert_type(w & -65536, F32)
    return jnp.concatenate([lo, hi], axis=-1)


def _dot(a, b):
    return jnp.dot(a, b, preferred_element_type=F32)


def _dot_nt(a, b):
    return lax.dot_general(a, b, (((1,), (1,)), ((), ())), preferred_element_type=F32)


def _dot_tn(a, b):
    return lax.dot_general(a, b, (((0,), (0,)), ((), ())), preferred_element_type=F32)


def _mod_kernel(ct_ref, w_ref, b_ref, o_ref, *, n_cond):
    ct = ct_ref[...]
    s = _silu(ct)
    w = w_ref[...]
    rows = [jnp.sum(w * s[:, r:r + 1], axis=0, keepdims=True) for r in range(n_cond)]
    rows.append(jnp.zeros((8 - n_cond, w.shape[1]), F32))
    o_ref[...] = jnp.concatenate(rows, axis=0) + b_ref[...]


def _modulation(cond, w_mod, b_mod):
    n_layers, d, n = w_mod.shape
    tn = 512
    ct = jnp.zeros((8, d), F32).at[:cond.shape[0]].set(cond).T
    return pl.pallas_call(
        functools.partial(_mod_kernel, n_cond=cond.shape[0]),
        grid=(n_layers, n // tn),
        in_specs=[
            pl.BlockSpec((d, 8), lambda l, j: (0, 0)),
            pl.BlockSpec((None, d, tn), lambda l, j: (l, 0, j)),
            pl.BlockSpec((None, 1, tn), lambda l, j: (l, 0, j)),
        ],
        out_specs=pl.BlockSpec((None, 8, tn), lambda l, j: (l, 0, j)),
        out_shape=jax.ShapeDtypeStruct((n_layers, 8, n), F32),
        compiler_params=_cparams("arbitrary", "arbitrary"),
        name="adaln_mod",
    )(ct, w_mod, b_mod.reshape(n_layers, 1, n))


def _mod_rows(mod_ref, i, tiles_per_mod, mod_base, first):
    r = mod_base + i // tiles_per_mod
    return [mod_ref[pl.ds(r, 1), (first + k) * D_MODEL:(first + k + 1) * D_MODEL] for k in range(3)]


def _in_kernel(x_ref, mod_ref, g_ref, w_ref, proj_ref, f_ref, sa_ref, sb_ref, *, tiles_per_mod, mod_base):
    i = pl.program_id(0)
    sh, sc, _ = _mod_rows(mod_ref, i, tiles_per_mod, mod_base, 0)
    h = (_rms(x_ref[...], g_ref[...]) * (1.0 + sc) + sh).astype(BF16)
    for j in range(IN_W // IN_TN):
        res = _dot(h, w_ref[:, j * IN_TN:(j + 1) * IN_TN])
        proj_ref[:, j * IN_TN:(j + 1) * IN_TN] = res.astype(BF16)
        if j == IN_F_TILE:
            f_ref[...] = res[:, IN_F_OFF:IN_F_OFF + BRANCH_W].astype(BF16)
            sa_ref[...] = res[:, IN_S_OFF:IN_S_OFF + 128]
            sb_ref[...] = res[:, IN_S_OFF + 128:IN_S_OFF + 256]


def _in_proj(x, mod, g, w_bf, *, rows_per_mod, mod_base):
    rows, d = x.shape
    tm = math.gcd(512, rows_per_mod)
    kern = functools.partial(_in_kernel, tiles_per_mod=max(rows_per_mod // tm, 1), mod_base=mod_base)
    return pl.pallas_call(
        kern,
        grid=(rows // tm,),
        in_specs=[
            pl.BlockSpec((tm, d), lambda i: (i, 0)),
            pl.BlockSpec(mod.shape, lambda i: (0, 0)),
            pl.BlockSpec((1, d), lambda i: (0, 0)),
            pl.BlockSpec((d, IN_W), lambda i: (0, 0), pipeline_mode=pl.Buffered(1)),
        ],
        out_specs=[
            pl.BlockSpec((tm, IN_W), lambda i: (i, 0)),
            pl.BlockSpec((tm, BRANCH_W), lambda i: (i, 0)),
            pl.BlockSpec((tm, 128), lambda i: (i, 0)),
            pl.BlockSpec((tm, 128), lambda i: (i, 0)),
        ],
        out_shape=[
            jax.ShapeDtypeStruct((rows, IN_W), BF16),
            jax.ShapeDtypeStruct((rows, BRANCH_W), BF16),
            jax.ShapeDtypeStruct((rows, 128), F32),
            jax.ShapeDtypeStruct((rows, 128), F32),
        ],
        compiler_params=_cparams("arbitrary"),
        name="in_proj",
    )(x, mod, g.reshape(1, d), w_bf)


def _fft_a_kernel(x_ref, cs_ref, tc_ref, ts_ref, zr_ref, zi_ref, *, n1, n1p):
    y = _dot(cs_ref[...].astype(BF16), x_ref[...])
    yr = y[:n1]
    yi = y[n1p:n1p + n1]
    tc = tc_ref[...]
    ts = ts_ref[...]
    zr_ref[...] = (yr * tc + yi * ts).astype(BF16)
    zi_ref[...] = (yi * tc - yr * ts).astype(BF16)


def _fft_b_kernel(zr_ref, zi_ref, cs_ref, cc_ref, sc_ref, oa_ref, ob_ref, *, kb, n1, scale, has_imag):
    cs = cs_ref[...].astype(BF16)
    cc = cc_ref[...].astype(BF16)
    sc = sc_ref[...].astype(BF16)
    half = BRANCH_W // 2
    for kk in range(kb):
        a = _dot(cs, zr_ref[kk])
        if has_imag:
            b = _dot(cs, zi_ref[kk])
            xr = a[:FFT_N2] + b[FFT_N2:]
            xi = b[:FFT_N2] - a[FFT_N2:]
        else:
            xr = a[:FFT_N2]
            xi = -a[FFT_N2:]
        out = (_dot(xr.astype(BF16), cc) + _dot(xi.astype(BF16), sc)) * scale
        k1 = pl.program_id(1) * kb + kk
        oa_ref[pl.ds(k1, FFT_N2, stride=n1), :] = out[:, :half]
        ob_ref[pl.ds(k1, FFT_N2, stride=n1), :] = out[:, half:]


def _dft_tables(n):
    k = np.arange(n)
    ang = 2.0 * np.pi * ((k[:, None] * k[None, :]) % n) / n
    return np.cos(ang), np.sin(ang)


def _fft_b_call(zr, zi, n1, batch, seq_len, has_imag):
    c2, s2 = _dft_tables(FFT_N2)
    cs2 = jnp.asarray(np.concatenate([c2, s2], axis=0), F32)
    c64, s64 = _dft_tables(FNET_GROUP_DIM)
    eye = np.eye(BRANCH_W // FNET_GROUP_DIM)
    cc = jnp.asarray(np.kron(eye, c64), F32)
    sc = jnp.asarray(np.kron(eye, s64), F32)
    kb = min(8, n1)
    scale = 1.0 / math.sqrt(seq_len * FNET_GROUP_DIM)
    kern = functools.partial(_fft_b_kernel, kb=kb, n1=n1, scale=scale, has_imag=has_imag)
    zspec = pl.BlockSpec((None, kb, FFT_N2, BRANCH_W), lambda b, i: (b, i, 0, 0))
    half = pl.BlockSpec((seq_len, BRANCH_W // 2), lambda b, i: (b, 0))
    return pl.pallas_call(
        kern,
        grid=(batch, n1 // kb),
        in_specs=[
            zspec, zspec,
            pl.BlockSpec((2 * FFT_N2, FFT_N2), lambda b, i: (0, 0)),
            pl.BlockSpec((BRANCH_W, BRANCH_W), lambda b, i: (0, 0)),
            pl.BlockSpec((BRANCH_W, BRANCH_W), lambda b, i: (0, 0)),
        ],
        out_specs=[half, half],
        out_shape=[jax.ShapeDtypeStruct((batch * seq_len, BRANCH_W // 2), F32)] * 2,
        compiler_params=_cparams("arbitrary", "arbitrary"),
        name="fourier_stage_b",
    )(zr, zi, cs2, cc, sc)


def _fourier_latent(f, batch, seq_len):
    n1 = seq_len // FFT_N2
    wide = FFT_N2 * BRANCH_W
    c1, s1 = _dft_tables(n1)
    n1p = max(n1, 8)
    cs1 = np.zeros((2 * n1p, n1))
    cs1[:n1] = c1
    cs1[n1p:n1p + n1] = -s1
    k1 = np.arange(n1)[:, None]
    l2 = np.arange(FFT_N2)[None, :]
    tw = 2.0 * np.pi * (k1 * l2) / seq_len
    tc = jnp.asarray(np.repeat(np.cos(tw), BRANCH_W, axis=1), F32)
    ts = jnp.asarray(np.repeat(np.sin(tw), BRANCH_W, axis=1), F32)
    cw = min(8192, wide)
    xv = f.reshape(batch, n1, wide)
    spec = pl.BlockSpec((None, n1, cw), lambda b, j: (b, 0, j))
    tspec = pl.BlockSpec((n1, cw), lambda b, j: (0, j))
    zr, zi = pl.pallas_call(
        functools.partial(_fft_a_kernel, n1=n1, n1p=n1p),
        grid=(batch, wide // cw),
        in_specs=[spec, pl.BlockSpec((2 * n1p, n1), lambda b, j: (0, 0)), tspec, tspec],
        out_specs=[spec, spec],
        out_shape=[jax.ShapeDtypeStruct((batch, n1, wide), BF16)] * 2,
        compiler_params=_cparams("arbitrary", "arbitrary"),
        name="fourier_stage_a",
    )(xv, jnp.asarray(cs1, F32), tc, ts)
    zr = zr.reshape(batch, n1, FFT_N2, BRANCH_W)
    zi = zi.reshape(batch, n1, FFT_N2, BRANCH_W)
    return _fft_b_call(zr, zi, n1, batch, seq_len, True)


def _fourier_ctx(f, batch, ctx_len):
    assert ctx_len == FFT_N2
    z = f.reshape(batch, 1, FFT_N2, BRANCH_W)
    return _fft_b_call(z, z, 1, batch, ctx_len, False)


def _s5_tables(a_re, a_im, log_dt, b_re, b_im, c_re, c_im, d_skip, batch):
    t = S5_CHUNK
    g, p, hc = S5_GROUPS, S5_STATE, S5_GROUP_CH
    lam = lax.complex(a_re.astype(F32), a_im.astype(F32))
    dt = jnp.exp(log_dt.astype(F32))[..., None]
    ks = jnp.arange(t + 1, dtype=F32)
    apow = jnp.exp((lam * dt)[..., None] * ks)
    a_bar = apow[..., 1]
    b_bar = ((a_bar - 1.0) / lam)[..., None] * lax.complex(b_re.astype(F32), b_im.astype(F32))
    cm = lax.complex(c_re.astype(F32), c_im.astype(F32))
    kimp = jnp.real(jnp.einsum('dghp,dgpk,dgpj->dgjkh', cm, apow[..., :t], b_bar,
                               precision=lax.Precision.HIGHEST))
    kf, kb = kimp[0], kimp[1]
    kfull = jnp.concatenate([kb[:, :, :0:-1], kf[:, :, :1] + kb[:, :, :1], kf[:, :, 1:]], axis=2)
    kp = kfull.reshape(S5_PAIRS, 2, hc, 2 * t - 1, hc)
    blk = [kp[:, gi] for gi in range(2)]
    zb = jnp.zeros_like(blk[0])
    strip = jnp.concatenate([jnp.stack([blk[0], zb], axis=3), jnp.stack([zb, blk[1]], axis=3)], axis=1)
    strip = strip.reshape(S5_PAIRS, 2 * hc, (2 * t - 1) * 2 * hc)
    strip = jnp.pad(strip, ((0, 0), (0, 0), (0, 2 * hc)))

    wf = jnp.einsum('gpj,gph->gjhp', apow[0][..., t - 1::-1][..., :t], b_bar[0])
    wb = jnp.einsum('gpj,gph->gjhp', apow[1][..., :t], b_bar[1])
    kinds = [jnp.real(wf), jnp.imag(wf), jnp.real(wb), jnp.imag(wb)]

    def we_pair(kd):
        k5 = kd.reshape(S5_PAIRS, 2, t, hc, p)
        z = jnp.zeros_like(k5[:, 0])
        rows = jnp.stack([jnp.concatenate([k5[:, 0], z], axis=-1), jnp.concatenate([z, k5[:, 1]], axis=-1)], axis=2)
        return rows.reshape(S5_PAIRS, 2 * t * hc, 2 * p)

    we = jnp.concatenate([we_pair(kd) for kd in kinds], axis=-1).astype(BF16)

    vf = jnp.einsum('ghp,gpt->gpth', cm[0], apow[0][..., 1:t + 1])
    vb = jnp.einsum('ghp,gpt->gpth', cm[1], apow[1][..., t:0:-1])
    vkinds = [jnp.real(vf), -jnp.imag(vf), jnp.real(vb), -jnp.imag(vb)]

    def v_pair(kd):
        k5 = kd.reshape(S5_PAIRS, 2, p, t, hc)
        z = jnp.zeros_like(k5[:, 0])
        rows = jnp.concatenate([jnp.stack([k5[:, 0], z], axis=3), jnp.stack([z, k5[:, 1]], axis=3)], axis=1)
        return rows.reshape(S5_PAIRS, 2 * p, 2 * t * hc)

    v1 = jnp.concatenate([v_pair(kd) for kd in vkinds], axis=1)
    v = jnp.concatenate([v1, v1], axis=1).astype(BF16)

    def lanes(z):
        return jnp.tile(z.reshape(1, g * p), (1, batch))

    at = apow[..., t]
    a_tab = jnp.concatenate([lanes(jnp.real(at[0])), lanes(jnp.imag(at[0])),
                             lanes(jnp.real(at[1])), lanes(jnp.imag(at[1]))], axis=0)
    dvec = jnp.tile(d_skip.astype(F32).reshape(S5_PAIRS, 1, 2 * hc), (1, t, 1)).reshape(S5_PAIRS, 1, 2 * t * hc)
    return dict(strip=strip, we=we, v=v, a_tab=a_tab, dvec=dvec)


def _s5_pack_kernel(xa_ref, xb_ref, u_ref, *, n_chunks):
    per_half = S5_PAIRS // 2
    for half, x_ref in enumerate((xa_ref, xb_ref)):
        rows = [x_ref[pl.ds(tau, n_chunks, stride=S5_CHUNK), :] for tau in range(S5_CHUNK)]
        for qq in range(per_half):
            pieces = [r[:, qq * 32:(qq + 1) * 32] for r in rows]
            u_ref[half * per_half + qq] = jnp.concatenate(pieces, axis=-1).astype(BF16)


def _s5_unpack_kernel(y_ref, oa_ref, ob_ref, *, n_chunks):
    per_half = S5_PAIRS // 2
    for half, o_ref in enumerate((oa_ref, ob_ref)):
        ys = [y_ref[half * per_half + qq].astype(F32) for qq in range(per_half)]
        for t in range(S5_CHUNK):
            pieces = [y[:, t * 32:(t + 1) * 32] for y in ys]
            o_ref[pl.ds(t, n_chunks, stride=S5_CHUNK), :] = jnp.concatenate(pieces, axis=-1)


def _s5_pack(sa, sb, batch):
    n_chunks = sa.shape[0] // batch // S5_CHUNK
    rows = n_chunks * S5_CHUNK
    cols = 2 * S5_CHUNK * S5_GROUP_CH
    half = pl.BlockSpec((rows, 128), lambda b: (b, 0))
    return pl.pallas_call(
        functools.partial(_s5_pack_kernel, n_chunks=n_chunks),
        grid=(batch,),
        in_specs=[half, half],
        out_specs=pl.BlockSpec((S5_PAIRS, None, n_chunks, cols), lambda b: (0, b, 0, 0)),
        out_shape=jax.ShapeDtypeStruct((S5_PAIRS, batch, n_chunks, cols), BF16),
        compiler_params=_cparams("arbitrary"),
        name="s5_pack",
    )(sa, sb)


def _s5_unpack(y, batch):
    n_chunks = y.shape[2]
    rows = n_chunks * S5_CHUNK
    cols = y.shape[3]
    half = pl.BlockSpec((rows, 128), lambda b: (b, 0))
    return pl.pallas_call(
        functools.partial(_s5_unpack_kernel, n_chunks=n_chunks),
        grid=(batch,),
        in_specs=[pl.BlockSpec((S5_PAIRS, None, n_chunks, cols), lambda b: (0, b, 0, 0))],
        out_specs=[half, half],
        out_shape=[jax.ShapeDtypeStruct((batch * rows, 128), F32)] * 2,
        compiler_params=_cparams("arbitrary"),
        name="s5_unpack",
    )(y)


def _s5_e_kernel(ul_ref, uc_ref, we_ref, ref_, imf_, reb_, imb_):
    u = jnp.concatenate([ul_ref[...], uc_ref[...]], axis=0)
    e = _dot(u, we_ref[...])
    ref_[...] = e[:, 0:128]
    imf_[...] = e[:, 128:256]
    reb_[...] = e[:, 256:384]
    imb_[...] = e[:, 384:512]


def _s5_scan_kernel(a_ref, ref_, imf_, reb_, imb_, prf, pif, prb, pib, *, n_rows, n_ctx):
    afr = a_ref[0:1, :]
    afi = a_ref[1:2, :]
    abr = a_ref[2:3, :]
    abi = a_ref[3:4, :]
    zero = jnp.zeros_like(afr)

    n_lat = n_rows - n_ctx

    def body(s, carry):
        sfr, sfi, sbr, sbi = carry
        nf = jnp.where(s < n_ctx, n_lat + s, s - n_ctx)
        nb = n_rows - 1 - s
        prf[pl.ds(nf, 1), :] = sfr
        pif[pl.ds(nf, 1), :] = sfi
        prb[pl.ds(nb, 1), :] = sbr
        pib[pl.ds(nb, 1), :] = sbi
        efr = ref_[pl.ds(nf, 1), :]
        efi = imf_[pl.ds(nf, 1), :]
        ebr = reb_[pl.ds(nb, 1), :]
        ebi = imb_[pl.ds(nb, 1), :]
        nfr = afr * sfr - afi * sfi + efr
        nfi = afr * sfi + afi * sfr + efi
        nbr = abr * sbr - abi * sbi + ebr
        nbi = abr * sbi + abi * sbr + ebi
        return nfr, nfi, nbr, nbi

    lax.fori_loop(0, n_rows, body, (zero, zero, zero, zero))


def _s5_y_kernel(ul_ref, uc_ref, strip_ref, v_ref, d_ref, prf, pif, prb, pib, yl_ref, yc_ref, m_scr):
    width = 2 * S5_GROUP_CH
    cols = S5_CHUNK * width
    n_lat = yl_ref.shape[0]

    @pl.when(pl.program_id(1) == 0)
    def _():
        strip = strip_ref[...]
        for j in range(S5_CHUNK):
            off = (S5_CHUNK - 1 - j) * width
            win = strip if off == 0 else pltpu.roll(strip, 2 * cols - off, axis=1)
            m_scr[j * width:(j + 1) * width, :] = win[:, :cols].astype(BF16)

    u = jnp.concatenate([ul_ref[...], uc_ref[...]], axis=0)
    y_intra = _dot(u, m_scr[...])
    pcat = jnp.concatenate([prf[...], pif[...], prb[...], pib[...]], axis=-1)
    hi, lo = _split_bf16(pcat)
    y_cross = _dot(jnp.concatenate([hi, lo], axis=-1), v_ref[...])
    y = y_intra + y_cross + d_ref[...] * u.astype(F32)
    yl_ref[...] = y[:n_lat].astype(BF16)
    yc_ref[...] = y[n_lat:].astype(BF16)


def _s5_core(ul, uc, tabs, layer, batch):
    n_lat, n_ctx = ul.shape[2], uc.shape[2]
    n_rows = n_lat + n_ctx
    width = batch * S5_PAIRS * 128
    cols = 2 * S5_CHUNK * S5_GROUP_CH
    ul_spec = pl.BlockSpec((None, None, n_lat, cols), lambda q, b: (q, b, 0, 0))
    uc_spec = pl.BlockSpec((None, None, n_ctx, cols), lambda q, b: (q, b, 0, 0))
    st_spec = pl.BlockSpec((n_rows, 128), lambda q, b: (0, b * S5_PAIRS + q))
    st_shape = jax.ShapeDtypeStruct((n_rows, width), F32)
    e4 = pl.pallas_call(
        _s5_e_kernel,
        grid=(S5_PAIRS, batch),
        in_specs=[ul_spec, uc_spec, pl.BlockSpec((None, None, cols, 512), lambda q, b: (layer, q, 0, 0))],
        out_specs=[st_spec] * 4,
        out_shape=[st_shape] * 4,
        compiler_params=_cparams("arbitrary", "arbitrary"),
        name="s5_chunk_states",
    )(ul, uc, tabs['we'])
    p4 = pl.pallas_call(
        functools.partial(_s5_scan_kernel, n_rows=n_rows, n_ctx=n_ctx),
        out_shape=[st_shape] * 4,
        compiler_params=pltpu.CompilerParams(vmem_limit_bytes=VMEM_LIMIT_BYTES),
        name="s5_state_scan",
    )(tabs['a_tab'][layer], *e4)
    y = pl.pallas_call(
        _s5_y_kernel,
        grid=(S5_PAIRS, batch),
        in_specs=[
            ul_spec, uc_spec,
            pl.BlockSpec((None, None, 2 * S5_GROUP_CH, 2 * cols), lambda q, b: (layer, q, 0, 0)),
            pl.BlockSpec((None, None, cols, cols), lambda q, b: (layer, q, 0, 0)),
            pl.BlockSpec((None, None, 1, cols), lambda q, b: (layer, q, 0, 0)),
            st_spec, st_spec, st_spec, st_spec,
        ],
        out_specs=[ul_spec, uc_spec],
        out_shape=[
            jax.ShapeDtypeStruct((S5_PAIRS, batch, n_lat, cols), BF16),
            jax.ShapeDtypeStruct((S5_PAIRS, batch, n_ctx, cols), BF16),
        ],
        scratch_shapes=[pltpu.VMEM((cols, cols), BF16)],
        compiler_params=_cparams("arbitrary", "arbitrary"),
        name="s5_outputs",
    )(ul, uc, tabs['strip'], tabs['v'], tabs['dvec'], *p4)
    return y


def _s5_mixer(s_lat, s_ctx, tabs, layer, batch):
    ul = _s5_pack(*s_lat, batch)
    uc = _s5_pack(*s_ctx, batch)
    yl, yc = _s5_core(ul, uc, tabs, layer, batch)
    return _s5_unpack(yl, batch), _s5_unpack(yc, batch)


def _ret_tables(ret_decay):
    c = RET_CHUNK
    lg = jax.nn.log_sigmoid(ret_decay.astype(F32))
    lane_h = np.repeat(np.arange(RET_HEADS), RET_DIM)
    lgl = jnp.repeat(lg, RET_DIM, axis=1)
    pos = jnp.arange(c, dtype=F32)[:, None]
    qd = jnp.stack([jnp.exp((pos + 1.0) * lgl[0][None]), jnp.exp((c - pos) * lgl[1][None])])
    kd = jnp.stack([jnp.exp((c - 1.0 - pos) * lgl[0][None]), jnp.exp(pos * lgl[1][None])])
    bmask = jnp.asarray((lane_h[:, None] == lane_h[None, :]).astype(np.float32))
    cd = jnp.exp(c * lgl)[:, :, None] * bmask[None]
    diff = pos - pos.T
    dm = []
    for h in range(RET_HEADS):
        fw = jnp.where(diff >= 0, jnp.exp(jnp.maximum(diff, 0.0) * lg[0, h]), 0.0)
        bw = jnp.where(diff <= 0, jnp.exp(jnp.maximum(-diff, 0.0) * lg[1, h]), 0.0)
        dm.append(fw + bw)
    dm = jnp.concatenate(dm, axis=0)
    return dict(qd=qd, kd=kd, cd=cd, dm=dm)


def _ret_masks():
    lane_h = np.repeat(np.arange(RET_HEADS), RET_DIM)
    bmask = (lane_h[:, None] == lane_h[None, :]).astype(np.float32)
    hmask = (np.arange(RET_HEADS)[:, None] == lane_h[None, :]).astype(np.float32)
    return jnp.asarray(bmask), jnp.asarray(hmask)


def _rope_tables(n_tokens):
    t = np.arange(n_tokens)
    row = (t // GRID_W).astype(np.float64)
    col = (t % GRID_W).astype(np.float64)
    n_freq = RET_DIM // 4
    inv_freq = 1.0 / (ROPE_BASE ** (np.arange(n_freq, dtype=np.float64) / n_freq))
    ang = np.concatenate([row[:, None] * inv_freq, col[:, None] * inv_freq], axis=-1)
    cos = np.cos(ang)
    sin = np.sin(ang)
    cos_t = np.tile(np.concatenate([cos, cos], axis=-1), (1, RET_HEADS))
    sin_t = np.tile(np.concatenate([-sin, sin], axis=-1), (1, RET_HEADS))
    half = RET_DIM // 2
    perm = np.arange(BRANCH_W) ^ half
    swap = np.zeros((BRANCH_W, BRANCH_W), np.float32)
    swap[perm, np.arange(BRANCH_W)] = 1.0
    return jnp.asarray(cos_t, F32), jnp.asarray(sin_t, F32), jnp.asarray(swap, BF16)


def _ret_chunk(q, k, v, s, qd, kd, cd, bmask, dm, hmask, with_intra):
    cross = _dot((q * qd).astype(BF16), s.astype(BF16))
    s_new = cd * s + bmask * _dot_tn((k * kd).astype(BF16), v)
    if not with_intra:
        return cross, s_new
    qb = q.astype(BF16)
    kb = k.astype(BF16)
    qs = jnp.concatenate([qb * hmask[h:h + 1].astype(BF16) for h in range(RET_HEADS)], axis=0)
    scores = _dot_nt(qs, kb) * dm
    ov = _dot(scores.astype(BF16), v)
    c = q.shape[0]
    inner = ov[0:c] * hmask[0:1]
    for h in range(1, RET_HEADS):
        inner = inner + ov[h * c:(h + 1) * c] * hmask[h:h + 1]
    return inner + cross, s_new


def _ret_kernel(qf_ref, kf_ref, vf_ref, qb_ref, kb_ref, vb_ref, qc_ref, kc_ref, vc_ref,
                cosf_ref, sinf_ref, cosb_ref, sinb_ref, swap_ref,
                qd_ref, kd_ref, cd_ref, bm_ref, dm_ref, hm_ref,
                of_ref, ob_ref, ocf_ref, ocb_ref, sf_scr, sb_scr, *, n_chunks, n_ctx_chunks):
    i = pl.program_id(1)
    c = RET_CHUNK
    k_scale = RET_DIM ** -0.5
    bmask = bm_ref[...]
    dm = dm_ref[...]
    hmask = hm_ref[...]
    tabs = [(qd_ref[d], kd_ref[d], cd_ref[d]) for d in range(2)]

    @pl.when(i == 0)
    def _():
        for d, oc_ref, s_scr in ((0, ocf_ref, sf_scr), (1, ocb_ref, sb_scr)):
            qd, kd, cd = tabs[d]
            s = jnp.zeros((BRANCH_W, BRANCH_W), F32)
            order = range(n_ctx_chunks) if d == 0 else range(n_ctx_chunks - 1, -1, -1)
            for cc in order:
                sl = slice(cc * c, (cc + 1) * c)
                o, s = _ret_chunk(qc_ref[sl, :].astype(F32), kc_ref[sl, :].astype(F32) * k_scale, vc_ref[sl, :],
                                  s, qd, kd, cd, bmask, dm, hmask, d == 0)
                oc_ref[sl, :] = o
            s_scr[...] = s

    swap = swap_ref[...]

    def rope(x_ref, cos_ref, sin_ref):
        xb = x_ref[...]
        return xb.astype(F32) * cos_ref[...] + _dot(xb, swap) * sin_ref[...]

    q_f = rope(qf_ref, cosf_ref, sinf_ref)
    k_f = rope(kf_ref, cosf_ref, sinf_ref) * k_scale
    q_b = rope(qb_ref, cosb_ref, sinb_ref)
    k_b = rope(kb_ref, cosb_ref, sinb_ref) * k_scale
    sf = sf_scr[...]
    sb = sb_scr[...]
    for step in range(n_chunks):
        sl = slice(step * c, (step + 1) * c)
        o, sf = _ret_chunk(q_f[sl], k_f[sl], vf_ref[sl, :], sf, *tabs[0], bmask, dm, hmask, True)
        of_ref[sl, :] = o
        cb = n_chunks - 1 - step
        sl = slice(cb * c, (cb + 1) * c)
        o, sb = _ret_chunk(q_b[sl], k_b[sl], vb_ref[sl, :], sb, *tabs[1], bmask, dm, hmask, False)
        ob_ref[sl, :] = o
    sf_scr[...] = sf
    sb_scr[...] = sb


def _retention(proj_l, proj_c, tabs, layer, masks, rope, batch, seq_len, ctx_len):
    n_chunks = 4
    blk = n_chunks * RET_CHUNK
    nblk = seq_len // blk
    cos_t, sin_t, swap = rope

    def lat(col, back):
        if back:
            return pl.BlockSpec((blk, BRANCH_W), lambda b, i: (b * nblk + nblk - 1 - i, col))
        return pl.BlockSpec((blk, BRANCH_W), lambda b, i: (b * nblk + i, col))

    def ctx(col):
        return pl.BlockSpec((ctx_len, BRANCH_W), lambda b, i: (b, col))

    def const(shape):
        return pl.BlockSpec(shape, lambda b, i: (0,) * len(shape))

    def per_layer(shape):
        return pl.BlockSpec((None,) + shape, lambda b, i: (layer,) + (0,) * len(shape))

    tab_f = pl.BlockSpec((blk, BRANCH_W), lambda b, i: (i, 0))
    tab_b = pl.BlockSpec((blk, BRANCH_W), lambda b, i: (nblk - 1 - i, 0))
    kern = functools.partial(_ret_kernel, n_chunks=n_chunks, n_ctx_chunks=ctx_len // RET_CHUNK)
    c = RET_CHUNK
    ctx_out = pl.BlockSpec((ctx_len, BRANCH_W), lambda b, i: (b, 0))
    o_f, o_b, oc_f, oc_b = pl.pallas_call(
        kern,
        grid=(batch, nblk),
        in_specs=[
            lat(COL_RQ, False), lat(COL_RK, False), lat(COL_RV, False),
            lat(COL_RQ, True), lat(COL_RK, True), lat(COL_RV, True),
            ctx(COL_RQ), ctx(COL_RK), ctx(COL_RV),
            tab_f, tab_f, tab_b, tab_b, const((BRANCH_W, BRANCH_W)),
            per_layer((2, c, BRANCH_W)), per_layer((2, c, BRANCH_W)), per_layer((2, BRANCH_W, BRANCH_W)),
            const((BRANCH_W, BRANCH_W)), per_layer((RET_HEADS * c, c)), const((RET_HEADS, BRANCH_W)),
        ],
        out_specs=[lat(0, False), lat(0, True), ctx_out, ctx_out],
        out_shape=[
            jax.ShapeDtypeStruct((batch * seq_len, BRANCH_W), F32),
            jax.ShapeDtypeStruct((batch * seq_len, BRANCH_W), F32),
            jax.ShapeDtypeStruct((batch * ctx_len, BRANCH_W), F32),
            jax.ShapeDtypeStruct((batch * ctx_len, BRANCH_W), F32),
        ],
        scratch_shapes=[pltpu.VMEM((BRANCH_W, BRANCH_W), F32), pltpu.VMEM((BRANCH_W, BRANCH_W), F32)],
        compiler_params=_cparams("arbitrary", "arbitrary"),
        name="retention",
    )(proj_l, proj_l, proj_l, proj_l, proj_l, proj_l, proj_c, proj_c, proj_c,
      cos_t, sin_t, cos_t, sin_t, swap,
      tabs['qd'], tabs['kd'], tabs['cd'], masks[0], tabs['dm'], masks[1])
    return (o_f, o_b), (oc_f, oc_b)


def _na_tables(rpb):
    kr, kw = NA_WIN_ROWS, NA_WIN_COLS
    col = np.arange(GRID_W)
    col_start = np.clip(col - kw // 2, 0, GRID_W - kw)
    in_win = (col[None, :] >= col_start[:, None]) & (col[None, :] < col_start[:, None] + kw)
    dc = np.clip(col[None, :] - col[:, None], -(kw - 1), kw - 1) + (kw - 1)
    pick_c = (dc[:, :, None] == np.arange(2 * kw - 1)[None, None, :]).astype(np.float32)
    by = jnp.einsum('hrc,qkc->hqrk', rpb.astype(F32), jnp.asarray(pick_c), precision=lax.Precision.HIGHEST)
    by = jnp.where(jnp.asarray(in_win)[None, :, None, :], by, NEG_BIG)
    bias = jnp.stack([by[:, :, v:v + kr, :] for v in range(kr)], axis=0)
    return bias.reshape(kr, NA_HEADS * GRID_W, kr * GRID_W)


def _na_head_mask():
    lane_h = np.repeat(np.arange(NA_HEADS), NA_DIM)
    hmask = (np.arange(NA_HEADS)[:, None] == lane_h[None, :]).astype(np.float32)
    return jnp.asarray(hmask, F32)


def _attend(qs, keys, vals, bias, kc, vc):
    s_ctx = _dot_nt(qs, kc)
    m = jnp.max(s_ctx, axis=-1, keepdims=True)
    if keys is not None:
        s_band = _dot_nt(qs, keys) + bias
        m = jnp.maximum(m, jnp.max(s_band, axis=-1, keepdims=True))
        p_band = jnp.exp(s_band - m)
    p_ctx = jnp.exp(s_ctx - m)
    l = jnp.sum(p_ctx, axis=-1, keepdims=True)
    o = _dot(p_ctx.astype(BF16), vc)
    if keys is not None:
        l = l + jnp.sum(p_band, axis=-1, keepdims=True)
        o = o + _dot(p_band.astype(BF16), vals)
    return o / l


def _stack_heads(q, hmask_scaled):
    return jnp.concatenate([q * hmask_scaled[h:h + 1] for h in range(NA_HEADS)], axis=0)


def _unstack_heads(o, hmask, n):
    out = o[0:n] * hmask[0:1]
    for h in range(1, NA_HEADS):
        out = out + o[h * n:(h + 1) * n] * hmask[h:h + 1]
    return out


def _na_kernel(q_ref, k_ref, v_ref, kc_ref, vc_ref, bias_ref, hm_ref, o_ref, *, n_grid_rows):
    i = pl.program_id(1)
    hmask = hm_ref[...]
    hms = (hmask * (NA_DIM ** -0.5)).astype(BF16)
    kc = kc_ref[...]
    vc = vc_ref[...]
    band = NA_WIN_ROWS * GRID_W
    for rr in range(NA_QROWS):
        r = i * NA_QROWS + rr
        rs = jnp.clip(r - NA_WIN_ROWS // 2, 0, n_grid_rows - NA_WIN_ROWS)
        var = rs - r + (NA_WIN_ROWS - 1)
        start = pl.multiple_of(rs * GRID_W, GRID_W)
        keys = k_ref[pl.ds(start, band), :]
        vals = v_ref[pl.ds(start, band), :]
        qs = _stack_heads(q_ref[rr * GRID_W:(rr + 1) * GRID_W, :], hms)
        o = _attend(qs, keys, vals, bias_ref[var], kc, vc)
        o_ref[rr * GRID_W:(rr + 1) * GRID_W, :] = _unstack_heads(o, hmask, GRID_W).astype(BF16)


def _na_ctx_kernel(q_ref, kc_ref, vc_ref, hm_ref, o_ref):
    hmask = hm_ref[...]
    hms = (hmask * (NA_DIM ** -0.5)).astype(BF16)
    n = q_ref.shape[0]
    o = _attend(_stack_heads(q_ref[...], hms), None, None, None, kc_ref[...], vc_ref[...])
    o_ref[...] = _unstack_heads(o, hmask, n).astype(BF16)


def _neighborhood(proj_l, proj_c, bias, layer, hmask, batch, seq_len, ctx_len, need_ctx_out):
    rows = seq_len // GRID_W
    qblk = NA_QROWS * GRID_W
    nq = seq_len // qblk
    out_l = pl.pallas_call(
        functools.partial(_na_kernel, n_grid_rows=rows),
        grid=(batch, nq),
        in_specs=[
            pl.BlockSpec((qblk, BRANCH_W), lambda b, i: (b * nq + i, COL_NQ)),
            pl.BlockSpec((seq_len, BRANCH_W), lambda b, i: (b, COL_NK)),
            pl.BlockSpec((seq_len, BRANCH_W), lambda b, i: (b, COL_NV)),
            pl.BlockSpec((ctx_len, BRANCH_W), lambda b, i: (b, COL_NK)),
            pl.BlockSpec((ctx_len, BRANCH_W), lambda b, i: (b, COL_NV)),
            pl.BlockSpec((None,) + bias.shape[1:], lambda b, i: (layer, 0, 0, 0)),
            pl.BlockSpec(hmask.shape, lambda b, i: (0, 0)),
        ],
        out_specs=pl.BlockSpec((qblk, BRANCH_W), lambda b, i: (b * nq + i, 0)),
        out_shape=jax.ShapeDtypeStruct((batch * seq_len, BRANCH_W), BF16),
        compiler_params=_cparams("arbitrary", "arbitrary"),
        name="neighborhood_attn",
    )(proj_l, proj_l, proj_l, proj_c, proj_c, bias, hmask)
    out_c = None
    if need_ctx_out:
        out_c = pl.pallas_call(
            _na_ctx_kernel,
            grid=(batch,),
            in_specs=[
                pl.BlockSpec((ctx_len, BRANCH_W), lambda b: (b, COL_NQ)),
                pl.BlockSpec((ctx_len, BRANCH_W), lambda b: (b, COL_NK)),
                pl.BlockSpec((ctx_len, BRANCH_W), lambda b: (b, COL_NV)),
                pl.BlockSpec(hmask.shape, lambda b: (0, 0)),
            ],
            out_specs=pl.BlockSpec((ctx_len, BRANCH_W), lambda b: (b, 0)),
            out_shape=jax.ShapeDtypeStruct((batch * ctx_len, BRANCH_W), BF16),
            compiler_params=_cparams("arbitrary"),
            name="context_attn",
        )(proj_c, proj_c, proj_c, hmask)
    return out_l, out_c


def _merge_kernel(x_ref, mod_ref, g_ref, gt0, gt1, gt2, gt3, fa_ref, fb_ref, s5a_ref, s5b_ref, rof_ref, rob_ref, rg_ref, na_ref,
                  wglu_ref, bglu_ref, gn_ref, avg_ref, wb_ref, wo_ref, o_ref, *, tiles_per_mod, mod_base):
    i = pl.program_id(0)
    _, _, gate_a = _mod_rows(mod_ref, i, tiles_per_mod, mod_base, 0)
    z = _gelu_tanh(jnp.concatenate([s5a_ref[...], s5b_ref[...]], axis=-1)).astype(BF16)
    zf = z.astype(F32)
    b_s5 = (zf * _sigmoid(_dot(z, wglu_ref[...]) + bglu_ref[...])).astype(BF16)
    o = rof_ref[...] + rob_ref[...]
    avg = avg_ref[...]
    hi, lo = _split_bf16(o)
    mu = _dot(hi, avg) + _dot(lo, avg)
    dlt = o - mu
    hi, lo = _split_bf16(dlt * dlt)
    var = _dot(hi, avg) + _dot(lo, avg)
    hn = dlt * lax.rsqrt(var + EPS) * gn_ref[...]
    b_ret = (_silu(rg_ref[...].astype(F32)) * hn).astype(BF16)
    b_fnet = jnp.concatenate([fa_ref[...], fb_ref[...]], axis=-1).astype(BF16)
    outs = (b_fnet, b_s5, b_ret, na_ref[...])
    gates = (gt0, gt1, gt2, gt3)
    y = (1.0 + jnp.tanh(gates[0][...].astype(F32))) * _dot(outs[0], wb_ref[0])
    for b in range(1, N_BRANCH):
        y = y + (1.0 + jnp.tanh(gates[b][...].astype(F32))) * _dot(outs[b], wb_ref[b])
    yo = _dot(y.astype(BF16), wo_ref[...])
    o_ref[...] = x_ref[...] + gate_a * _rms(yo, g_ref[...])


def _merge(x, mod, g1, proj, a, s5y, ret_o, na, lw, *, rows_per_mod, mod_base):
    rows, d = x.shape
    tm = min(512, rows)
    nt = rows // tm

    def row(shape, col=0):
        return pl.BlockSpec(shape, lambda i: (i, col))

    def const(arr):
        return pl.BlockSpec(arr.shape, lambda i: (0,) * arr.ndim)

    kern = functools.partial(_merge_kernel, tiles_per_mod=max(rows_per_mod // tm, 1), mod_base=mod_base)
    ins = [x, mod, g1.reshape(1, d), proj, proj, proj, proj, a[0], a[1], s5y[0], s5y[1], ret_o[0], ret_o[1], proj, na,
           lw['w_glu'], lw['b_glu'], lw['ret_gn'], lw['avg'], lw['w_branch'], lw['w_out']]
    specs = [
        row((tm, d)), const(mod), pl.BlockSpec((1, d), lambda i: (0, 0)),
        row((tm, d), 0), row((tm, d), 1), row((tm, d), 2), row((tm, d), 3),
        row((tm, 128)), row((tm, 128)), row((tm, 128)), row((tm, 128)),
        row((tm, BRANCH_W)), row((tm, BRANCH_W)),
        row((tm, BRANCH_W), COL_RG), row((tm, BRANCH_W)),
        const(lw['w_glu']), const(lw['b_glu']), const(lw['ret_gn']), const(lw['avg']),
        const(lw['w_branch']), const(lw['w_out']),
    ]
    return pl.pallas_call(
        kern,
        grid=(nt,),
        in_specs=specs,
        out_specs=row((tm, d)),
        out_shape=jax.ShapeDtypeStruct((rows, d), F32),
        compiler_params=_cparams("arbitrary"),
        name="merge_out",
    )(*ins)


def _ffn_kernel(x_ref, mod_ref, g2_ref, g3_ref, wg_ref, wu_ref, wd_ref, o_ref, *, tiles_per_mod, mod_base):
    i = pl.program_id(0)
    sh, sc, gate_f = _mod_rows(mod_ref, i, tiles_per_mod, mod_base, 3)
    x = x_ref[...]
    h = (_rms(x, g2_ref[...]) * (1.0 + sc) + sh).astype(BF16)
    act = (_silu(_dot(h, wg_ref[...])) * _dot(h, wu_ref[...])).astype(BF16)
    y = _dot(act, wd_ref[...])
    o_ref[...] = x + gate_f * _rms(y, g3_ref[...])


def _ffn_dense(x, mod, g2, g3, wg, wu, wd, *, rows_per_mod, mod_base):
    rows, d = x.shape
    d_ff = wg.shape[1]
    tm = min(512, rows)
    kern = functools.partial(_ffn_kernel, tiles_per_mod=max(rows_per_mod // tm, 1), mod_base=mod_base)

    def resident(shape):
        return pl.BlockSpec(shape, lambda i: (0, 0), pipeline_mode=pl.Buffered(1))

    return pl.pallas_call(
        kern,
        grid=(rows // tm,),
        in_specs=[
            pl.BlockSpec((tm, d), lambda i: (i, 0)),
            pl.BlockSpec(mod.shape, lambda i: (0, 0)),
            pl.BlockSpec((1, d), lambda i: (0, 0)),
            pl.BlockSpec((1, d), lambda i: (0, 0)),
            resident((d, d_ff)), resident((d, d_ff)), resident((d_ff, d)),
        ],
        out_specs=pl.BlockSpec((tm, d), lambda i: (i, 0)),
        out_shape=jax.ShapeDtypeStruct((rows, d), F32),
        compiler_params=_cparams("arbitrary"),
        name="ffn_dense",
    )(x, mod, g2.reshape(1, d), g3.reshape(1, d), wg, wu, wd)


def _router_kernel(x_ref, mod_ref, g2_ref, wr_ref, br_ref, tri_ref, h_ref, comb_ref, plan_ref, cnt_ref, cnt_scr,
                   *, tiles_per_mod, mod_base):
    i = pl.program_id(0)

    @pl.when(i == 0)
    def _():
        cnt_scr[...] = jnp.zeros_like(cnt_scr)

    sh, sc, _ = _mod_rows(mod_ref, i, tiles_per_mod, mod_base, 3)
    h = _rms(x_ref[...], g2_ref[...]) * (1.0 + sc) + sh
    h_ref[...] = _pack_pairs(h)
    h_hi, h_lo = _split_bf16(h)
    w_hi, w_lo = _split_bf16(wr_ref[...])
    logits = _dot(h_hi, w_hi) + _dot(h_lo, w_hi) + _dot(h_hi, w_lo) + br_ref[...]
    lane = lax.broadcasted_iota(jnp.int32, logits.shape, 1)
    v1 = jnp.max(logits, axis=-1, keepdims=True)
    i1 = jnp.min(jnp.where(logits == v1, lane, 128), axis=-1, keepdims=True)
    rest = jnp.where(lane == i1, NEG_BIG, logits)
    v2 = jnp.max(rest, axis=-1, keepdims=True)
    i2 = jnp.min(jnp.where(rest == v2, lane, 128), axis=-1, keepdims=True)
    e = jnp.exp(v2 - v1)
    w1 = 1.0 / (1.0 + e)
    w2 = e / (1.0 + e)
    meta = jnp.where(lane == 0, i1.astype(F32), 0.0) + jnp.where(lane == 1, i2.astype(F32), 0.0)
    meta = meta + jnp.where(lane == 2, w1, 0.0) + jnp.where(lane == 3, w2, 0.0)
    member = jnp.where((lane == i1) | (lane == i2), 1.0, 0.0)
    before = _dot(tri_ref[...], member.astype(BF16)) + cnt_scr[...]
    rank1 = jnp.sum(jnp.where(lane == i1, before, 0.0), axis=-1, keepdims=True)
    rank2 = jnp.sum(jnp.where(lane == i2, before, 0.0), axis=-1, keepdims=True)
    cnt_scr[...] += jnp.sum(member, axis=0, keepdims=True)
    cnt_ref[...] = cnt_scr[...]
    meta = meta + jnp.where(lane == 4, rank1, 0.0) + jnp.where(lane == 5, rank2, 0.0)
    comb_ref[...] = meta[:, :MOE_META_W]
    plan_ref[...] = meta.T[:MOE_META_W]


def _router(x, mod, g2, w_router, b_router, *, rows_per_mod, mod_base):
    rows, d = x.shape
    tm = min(512, rows)
    wr = jnp.zeros((d, 128), F32).at[:, :N_EXPERTS].set(w_router)
    br = jnp.full((1, 128), NEG_BIG, F32).at[0, :N_EXPERTS].set(b_router)
    tri = jnp.asarray(np.tril(np.ones((tm, tm), np.float32), -1), BF16)
    kern = functools.partial(_router_kernel, tiles_per_mod=max(rows_per_mod // tm, 1), mod_base=mod_base)
    return pl.pallas_call(
        kern,
        grid=(rows // tm,),
        in_specs=[
            pl.BlockSpec((tm, d), lambda i: (i, 0)),
            pl.BlockSpec(mod.shape, lambda i: (0, 0)),
            pl.BlockSpec((1, d), lambda i: (0, 0)),
            pl.BlockSpec((d, 128), lambda i: (0, 0)),
            pl.BlockSpec((1, 128), lambda i: (0, 0)),
            pl.BlockSpec((tm, tm), lambda i: (0, 0)),
        ],
        out_specs=[
            pl.BlockSpec((tm, d // 2), lambda i: (i, 0)),
            pl.BlockSpec((tm, MOE_META_W), lambda i: (i, 0)),
            pl.BlockSpec((MOE_META_W, tm), lambda i: (0, i)),
            pl.BlockSpec((1, 128), lambda i: (0, 0)),
        ],
        out_shape=[
            jax.ShapeDtypeStruct((rows, d // 2), jnp.int32),
            jax.ShapeDtypeStruct((rows, MOE_META_W), F32),
            jax.ShapeDtypeStruct((MOE_META_W, rows), F32),
            jax.ShapeDtypeStruct((1, 128), F32),
        ],
        scratch_shapes=[pltpu.VMEM((1, 128), F32)],
        compiler_params=_cparams("arbitrary"),
        name="moe_router",
    )(x, mod, g2.reshape(1, d), wr, br, tri)


def _sc_gather(table, idx):
    n_idx = idx.shape[0]
    width = table.shape[1]
    per_worker = n_idx // SC_WORKERS
    chunk_rows = math.gcd(per_worker, SC_GATHER_ROWS)
    n_chunks = per_worker // chunk_rows
    assert per_worker * SC_WORKERS == n_idx and chunk_rows % 8 == 0
    mesh = plsc.VectorSubcoreMesh(core_axis_name="c", subcore_axis_name="s")

    assert n_chunks % 2 == 0
    buf = [pltpu.VMEM((chunk_rows,), jnp.int32), pltpu.VMEM((chunk_rows, width), table.dtype),
           pltpu.SemaphoreType.DMA, pltpu.SemaphoreType.DMA]

    @functools.partial(
        pl.kernel, mesh=mesh,
        out_type=jax.ShapeDtypeStruct((n_idx, width), table.dtype),
        scratch_types=buf + buf,
        name="sc_row_gather",
    )
    def gather(table_hbm, idx_hbm, out_hbm, idx0, rows0, g0, w0, idx1, rows1, g1, w1):
        wid = lax.axis_index("s") * SC_CORES + lax.axis_index("c")
        base = wid * per_worker
        slots = ((idx0, rows0, g0, w0), (idx1, rows1, g1, w1))

        def fetch(j, slot):
            idx_v, rows_v, g, _ = slots[slot]
            pltpu.sync_copy(idx_hbm.at[pl.ds(base + j * chunk_rows, chunk_rows)], idx_v)
            pltpu.make_async_copy(table_hbm.at[idx_v], rows_v, g).start()

        def store(j, slot):
            idx_v, rows_v, g, w = slots[slot]
            pltpu.make_async_copy(table_hbm.at[idx_v], rows_v, g).wait()
            pltpu.make_async_copy(rows_v, out_hbm.at[pl.ds(base + j * chunk_rows, chunk_rows)], w).start()

        def drain(j, slot):
            _, rows_v, _, w = slots[slot]
            pltpu.make_async_copy(rows_v, out_hbm.at[pl.ds(base + j * chunk_rows, chunk_rows)], w).wait()

        fetch(0, 0)

        @pl.loop(0, n_chunks // 2)
        def _(jj):
            j = 2 * jj

            @pl.when(jj > 0)
            def _():
                drain(j - 1, 1)

            fetch(j + 1, 1)
            store(j, 0)

            @pl.when(j + 2 < n_chunks)
            def _():
                drain(j, 0)
                fetch(j + 2, 0)

            store(j + 1, 1)

        drain(n_chunks - 2, 0)
        drain(n_chunks - 1, 1)

    return gather(table, idx)


def _sc_scatter(table, idx, n_out):
    n_idx = idx.shape[0]
    rows, width = table.shape
    per_worker = n_idx // SC_WORKERS
    chunk_rows = math.gcd(per_worker, SC_GATHER_ROWS)
    n_chunks = per_worker // chunk_rows
    assert per_worker * SC_WORKERS == n_idx and chunk_rows % 8 == 0 and rows % per_worker == 0
    mesh = plsc.VectorSubcoreMesh(core_axis_name="c", subcore_axis_name="s")

    assert n_chunks % 2 == 0
    buf = [pltpu.VMEM((chunk_rows,), jnp.int32), pltpu.VMEM((chunk_rows, width), table.dtype),
           pltpu.SemaphoreType.DMA, pltpu.SemaphoreType.DMA]

    @functools.partial(
        pl.kernel, mesh=mesh,
        out_type=jax.ShapeDtypeStruct((n_out, width), table.dtype),
        scratch_types=buf + buf,
        name="sc_row_scatter",
    )
    def scatter(table_hbm, idx_hbm, out_hbm, idx0, rows0, l0, w0, idx1, rows1, l1, w1):
        wid = lax.axis_index("s") * SC_CORES + lax.axis_index("c")
        base = wid * per_worker
        slots = ((idx0, rows0, l0, w0), (idx1, rows1, l1, w1))

        def src(j):
            return table_hbm.at[pl.ds(lax.rem(base + j * chunk_rows, rows), chunk_rows)]

        def fetch(j, slot):
            idx_v, rows_v, l, _ = slots[slot]
            pltpu.sync_copy(idx_hbm.at[pl.ds(base + j * chunk_rows, chunk_rows)], idx_v)
            pltpu.make_async_copy(src(j), rows_v, l).start()

        def store(j, slot):
            idx_v, rows_v, l, w = slots[slot]
            pltpu.make_async_copy(src(j), rows_v, l).wait()
            pltpu.make_async_copy(rows_v, out_hbm.at[idx_v], w).start()

        def drain(slot):
            idx_v, rows_v, _, w = slots[slot]
            pltpu.make_async_copy(rows_v, out_hbm.at[idx_v], w).wait()

        fetch(0, 0)

        @pl.loop(0, n_chunks // 2)
        def _(jj):
            j = 2 * jj

            @pl.when(jj > 0)
            def _():
                drain(1)

            fetch(j + 1, 1)
            store(j, 0)

            @pl.when(j + 2 < n_chunks)
            def _():
                drain(0)
                fetch(j + 2, 0)

            store(j + 1, 1)

        drain(0)
        drain(1)

    return scatter(table, idx)


def _moe_plan(plan, counts_row, rows):
    tile = MOE_ROW_TILE
    n_tiles = (2 * rows) // tile + N_EXPERTS
    n_slots = n_tiles * tile
    counts = counts_row[0, :N_EXPERTS].astype(jnp.int32)
    padded = ((counts + tile - 1) // tile) * tile
    ends = jnp.cumsum(padded)
    starts = ends - padded
    ids = jnp.arange(N_EXPERTS, dtype=F32)[:, None]
    start_f = starts.astype(F32)[:, None]

    def slot(e_row, r_row):
        return jnp.sum(jnp.where(e_row[None, :] == ids, start_f, 0.0), axis=0) + r_row

    pos = jnp.concatenate([slot(plan[0], plan[4]), slot(plan[1], plan[5])])
    tile_start = jnp.arange(n_tiles, dtype=jnp.int32) * tile
    used = tile_start < ends[-1]
    tile_e = jnp.minimum(jnp.sum((tile_start[:, None] >= ends[None, :]).astype(jnp.int32), axis=1), N_EXPERTS - 1)
    last_e = jnp.max(jnp.where(used, tile_e, 0))
    tile_e = jnp.where(used, tile_e, last_e)
    valid_end = jnp.sum((tile_e[:, None] == jnp.arange(N_EXPERTS)[None, :]) * (starts + counts)[None, :], axis=1)
    n_valid = jnp.where(used, jnp.clip(valid_end - tile_start, 0, tile), 0).astype(jnp.int32)
    return pos.astype(jnp.int32), n_slots, tile_e.astype(jnp.int32), n_valid


def _moe_group_kernel(eid_ref, nval_ref, hs_ref, wg_ref, wu_ref, wd_ref, y_ref, acc_scr, *, n_f):
    w = pl.program_id(0)
    f = pl.program_id(1)
    nv = nval_ref[w]

    def run(n_rows):
        wg = wg_ref[...].astype(BF16)
        wu = wu_ref[...].astype(BF16)
        wd = wd_ref[...].astype(BF16)
        for r0 in range(0, n_rows, MOE_SUB_ROWS):
            rows = slice(r0, r0 + MOE_SUB_ROWS)
            hv = _unpack_pairs(hs_ref[rows, :])
            row = r0 + lax.broadcasted_iota(jnp.int32, hv.shape, 0)
            h = jnp.where(row < nv, hv, 0.0).astype(BF16)
            part = _dot((_silu(_dot(h, wg)) * _dot(h, wu)).astype(BF16), wd)
            acc = jnp.where(f == 0, 0.0, acc_scr[rows, :]) + part
            acc_scr[rows, :] = acc
            y_ref[rows, :] = _pack_pairs(acc)

    half = hs_ref.shape[0] // 2

    @pl.when(nv > half)
    def _():
        run(hs_ref.shape[0])

    @pl.when((nv > 0) & (nv <= half))
    def _():
        run(half)


def _moe_grouped(hs, tile_e, n_valid, wg, wu, wd):
    n_slots = hs.shape[0]
    d = wg.shape[1]
    d_ff = wg.shape[2]
    tile = MOE_ROW_TILE
    tf = MOE_FF_TILE
    n_f = d_ff // tf

    def f_idx(f, nval, w):
        return jnp.where(nval[w] > 0, f, n_f - 1)

    grid_spec = pltpu.PrefetchScalarGridSpec(
        num_scalar_prefetch=2,
        grid=(n_slots // tile, n_f),
        in_specs=[
            pl.BlockSpec((tile, d // 2), lambda w, f, eid, nval: (w, 0)),
            pl.BlockSpec((None, d, tf), lambda w, f, eid, nval: (eid[w], 0, f_idx(f, nval, w))),
            pl.BlockSpec((None, d, tf), lambda w, f, eid, nval: (eid[w], 0, f_idx(f, nval, w))),
            pl.BlockSpec((None, tf, d), lambda w, f, eid, nval: (eid[w], f_idx(f, nval, w), 0)),
        ],
        out_specs=pl.BlockSpec((tile, d // 2), lambda w, f, eid, nval: (w, 0)),
        scratch_shapes=[pltpu.VMEM((tile, d), F32)],
    )
    return pl.pallas_call(
        functools.partial(_moe_group_kernel, n_f=n_f),
        grid_spec=grid_spec,
        out_shape=jax.ShapeDtypeStruct((n_slots, d // 2), jnp.int32),
        compiler_params=_cparams("arbitrary", "arbitrary"),
        name="moe_experts",
    )(tile_e, n_valid, hs, wg, wu, wd)


def _moe_out_kernel(x_ref, y1_ref, y2_ref, meta_ref, mod_ref, g3_ref, o_ref, *, tiles_per_mod, mod_base):
    i = pl.program_id(0)
    _, _, gate_f = _mod_rows(mod_ref, i, tiles_per_mod, mod_base, 3)
    meta = meta_ref[...]
    y = meta[:, 2:3] * _unpack_pairs(y1_ref[...]) + meta[:, 3:4] * _unpack_pairs(y2_ref[...])
    o_ref[...] = x_ref[...] + gate_f * _rms(y, g3_ref[...])


def _moe_combine(x, yg, meta, mod, g3, *, rows_per_mod, mod_base):
    rows, d = x.shape
    tm = min(512, rows)
    nt = rows // tm
    kern = functools.partial(_moe_out_kernel, tiles_per_mod=max(rows_per_mod // tm, 1), mod_base=mod_base)
    return pl.pallas_call(
        kern,
        grid=(nt,),
        in_specs=[
            pl.BlockSpec((tm, d), lambda i: (i, 0)),
            pl.BlockSpec((tm, d // 2), lambda i: (i, 0)),
            pl.BlockSpec((tm, d // 2), lambda i: (nt + i, 0)),
            pl.BlockSpec((tm, MOE_META_W), lambda i: (i, 0)),
            pl.BlockSpec(mod.shape, lambda i: (0, 0)),
            pl.BlockSpec((1, d), lambda i: (0, 0)),
        ],
        out_specs=pl.BlockSpec((tm, d), lambda i: (i, 0)),
        out_shape=jax.ShapeDtypeStruct((rows, d), F32),
        compiler_params=_cparams("arbitrary"),
        name="moe_combine",
    )(x, yg, yg, meta, mod, g3.reshape(1, d))


def _moe_sparse(x, routed, mod, g3, wg, wu, wd, *, rows_per_mod, mod_base):
    h, meta, plan, counts = routed
    rows = x.shape[0]
    pos, n_slots, tile_e, n_valid = _moe_plan(plan, counts, rows)
    hs = _sc_scatter(h, pos, n_slots)
    ys = _moe_grouped(hs, tile_e, n_valid, wg, wu, wd)
    yg = _sc_gather(ys, pos)
    return _moe_combine(x, yg, meta, mod, g3, rows_per_mod=rows_per_mod, mod_base=mod_base)


def _cast_kernel(w_ref, o_ref, *, scale):
    w = w_ref[...]
    o_ref[...] = (w if scale == 1.0 else w * scale).astype(BF16)


def _cast_bf16(w_stack, layer, scale=1.0):
    squeeze = w_stack.ndim == 3
    w4 = w_stack[:, None] if squeeze else w_stack
    _, n_e, k, n = w4.shape
    bk = min(k, 256)
    out = pl.pallas_call(
        functools.partial(_cast_kernel, scale=scale),
        grid=(n_e, k // bk),
        in_specs=[pl.BlockSpec((None, None, bk, n), lambda e, i: (layer, e, i, 0))],
        out_specs=pl.BlockSpec((None, bk, n), lambda e, i: (e, i, 0)),
        out_shape=jax.ShapeDtypeStruct((n_e, k, n), BF16),
        compiler_params=_cparams("arbitrary", "arbitrary"),
        name="cast_weights",
    )(w4)
    return out[0] if squeeze else out


def _permute_w_in(w_in_stack, layer):
    _, k, n = w_in_stack.shape
    n_blocks = n // BRANCH_W
    shift = 9
    n_gate_blocks = N_BRANCH * D_MODEL // BRANCH_W

    per_step = 5
    assert n_blocks % per_step == 0

    def permute_kernel(*refs):
        o_ref = refs[-1]
        for s, w_ref in enumerate(refs[:-1]):
            scale = jnp.where(pl.program_id(0) * per_step + s < n_gate_blocks, 0.5, 1.0)
            o_ref[:, s * BRANCH_W:(s + 1) * BRANCH_W] = (w_ref[...] * scale).astype(BF16)

    def src(s):
        return pl.BlockSpec((None, k, BRANCH_W), lambda j: (layer, 0, (j * per_step + s + shift) % n_blocks))

    return pl.pallas_call(
        permute_kernel,
        grid=(n_blocks // per_step,),
        in_specs=[src(s) for s in range(per_step)],
        out_specs=pl.BlockSpec((k, per_step * BRANCH_W), lambda j: (0, j)),
        out_shape=jax.ShapeDtypeStruct((k, n), BF16),
        compiler_params=_cparams("arbitrary"),
        name="cast_permute_w_in",
    )(*([w_in_stack] * per_step))


def kernel(x, c, ctx, c_ctx, w_mod, b_mod, norm_g, w_in, s5_a_re, s5_a_im, s5_log_dt, s5_b_re, s5_b_im, s5_c_re, s5_c_im, s5_d, s5_w_glu, s5_b_glu, ret_decay, ret_gn, na_rpb, w_branch, w_out, ffn_w_gate, ffn_w_up, ffn_w_down, moe_w_router, moe_b_router, moe_w_gate, moe_w_up, moe_w_down):
    batch, seq_len, d = x.shape
    ctx_len = ctx.shape[1]
    depth = w_mod.shape[0]
    cond = jnp.concatenate([c, c_ctx[None, :]], axis=0)
    mod_all = _modulation(cond, w_mod, b_mod)
    rope = _rope_tables(seq_len)
    lane_h = np.repeat(np.arange(RET_HEADS), RET_DIM)
    avg = jnp.asarray((lane_h[:, None] == lane_h[None, :]).astype(np.float32) / RET_DIM, BF16)

    xl = x.reshape(batch * seq_len, d)
    xc = ctx.reshape(batch * ctx_len, d)
    lat = dict(rows_per_mod=seq_len, mod_base=0)
    cxt = dict(rows_per_mod=batch * ctx_len, mod_base=batch)

    s5_tabs = jax.vmap(functools.partial(_s5_tables, batch=batch))(
        s5_a_re, s5_a_im, s5_log_dt, s5_b_re, s5_b_im, s5_c_re, s5_c_im, s5_d)
    ret_tabs = jax.vmap(_ret_tables)(ret_decay)
    ret_masks = _ret_masks()
    na_bias = jax.vmap(_na_tables)(na_rpb)
    na_hmask = _na_head_mask()

    for layer in range(depth):
        last = layer == depth - 1
        need_ctx = not last
        mod = mod_all[layer]
        ng = norm_g[layer]
        w_in_bf = _permute_w_in(w_in, layer)
        lw = dict(w_glu=s5_w_glu[layer].astype(BF16), b_glu=s5_b_glu[layer].reshape(1, BRANCH_W).astype(F32),
                  ret_gn=ret_gn[layer].reshape(1, BRANCH_W).astype(F32), avg=avg,
                  w_branch=_cast_bf16(w_branch, layer, 0.5), w_out=_cast_bf16(w_out, layer))

        proj_l, f_l, *s_in_l = _in_proj(xl, mod, ng[0], w_in_bf, **lat)
        proj_c, f_c, *s_in_c = _in_proj(xc, mod, ng[0], w_in_bf, **cxt)

        a_l = _fourier_latent(f_l, batch, seq_len)
        s_l, s_c = _s5_mixer(s_in_l, s_in_c, s5_tabs, layer, batch)
        r_l, r_c = _retention(proj_l, proj_c, ret_tabs, layer, ret_masks, rope, batch, seq_len, ctx_len)
        n_l, n_c = _neighborhood(proj_l, proj_c, na_bias, layer, na_hmask, batch, seq_len, ctx_len, need_ctx)

        xl = _merge(xl, mod, ng[1], proj_l, a_l, s_l, r_l, n_l, lw, **lat)
        if need_ctx:
            a_c = _fourier_ctx(f_c, batch, ctx_len)
            xc = _merge(xc, mod, ng[1], proj_c, a_c, s_c, r_c, n_c, lw, **cxt)

        i = layer // 2
        if layer % 2 == 0:
            wg, wu, wd = _cast_bf16(ffn_w_gate, i), _cast_bf16(ffn_w_up, i), _cast_bf16(ffn_w_down, i)
            xl = _ffn_dense(xl, mod, ng[2], ng[3], wg, wu, wd, **lat)
            if need_ctx:
                xc = _ffn_dense(xc, mod, ng[2], ng[3], wg, wu, wd, **cxt)
        else:
            wg, wu, wd = moe_w_gate[i], moe_w_up[i], moe_w_down[i]
            routed = _router(xl, mod, ng[2], moe_w_router[i], moe_b_router[i], **lat)
            xl = _moe_sparse(xl, routed, mod, ng[3], wg, wu, wd, **lat)
            if need_ctx:
                routed_c = _router(xc, mod, ng[2], moe_w_router[i], moe_b_router[i], **cxt)
                xc = _moe_sparse(xc, routed_c, mod, ng[3], wg, wu, wd, **cxt)
    return xl.reshape(batch, seq_len, d)
```

```python
import functools
import math

import numpy as np
import jax
import jax.numpy as jnp
from jax import lax
from jax.experimental import pallas as pl
from jax.experimental.pallas import tpu as pltpu
from jax.experimental.pallas import tpu_sc as plsc

F32 = jnp.float32
BF16 = jnp.bfloat16

D_MODEL = 1024
BRANCH_W = 256
N_BRANCH = 4
GRID_W = 64
FNET_GROUP_DIM = 64
S5_GROUP_CH = 16
S5_GROUPS = 16
S5_STATE = 64
S5_CHUNK = 32
S5_PAIRS = S5_GROUPS // 2
RET_HEADS = 4
RET_DIM = 64
RET_CHUNK = 128
NA_HEADS = 4
NA_DIM = 64
NA_WIN_ROWS = 8
NA_WIN_COLS = 16
NA_QROWS = 8
ROPE_BASE = 10000.0
N_EXPERTS = 8
EPS = 1e-6
FFT_N2 = 256
NEG_BIG = -1e30
VMEM_LIMIT_BYTES = 50 * 1024 * 1024
SC_CORES = 2
SC_SUBCORES = 16
SC_WORKERS = SC_CORES * SC_SUBCORES
SC_GATHER_ROWS = 64
MOE_ROW_TILE = 2048
MOE_SUB_ROWS = 512
MOE_FF_TILE = 512
MOE_META_W = 8

COL_F, COL_S, COL_RQ, COL_RK, COL_RV, COL_RG, COL_NQ, COL_NK, COL_NV = range(16, 25)
IN_W = 9 * BRANCH_W + N_BRANCH * D_MODEL
IN_TN = 1280
IN_F_TILE = (N_BRANCH * D_MODEL) // IN_TN
IN_F_OFF = N_BRANCH * D_MODEL - IN_F_TILE * IN_TN
IN_S_OFF = IN_F_OFF + BRANCH_W


def _cparams(*sem):
    return pltpu.CompilerParams(dimension_semantics=sem, vmem_limit_bytes=VMEM_LIMIT_BYTES)


def _sigmoid(v):
    return 0.5 * jnp.tanh(0.5 * v) + 0.5


def _silu(v):
    return v * _sigmoid(v)


def _gelu_tanh(v):
    return 0.5 * v * (1.0 + jnp.tanh(math.sqrt(2.0 / math.pi) * (v + 0.044715 * (v * v * v))))


def _rms(v, g):
    ms = jnp.mean(v * v, axis=-1, keepdims=True)
    return v * lax.rsqrt(ms + EPS) * g


def _split_bf16(v):
    hi = v.astype(BF16)
    lo = (v - hi.astype(F32)).astype(BF16)
    return hi, lo


def _pack_pairs(v):
    n = v.shape[1] // 2
    lo = lax.bitcast_convert_type(v[:, :n].astype(BF16).astype(F32), jnp.int32)
    hi = lax.bitcast_convert_type(v[:, n:].astype(BF16).astype(F32), jnp.int32)
    return (hi & -65536) | ((lo >> 16) & 65535)


def _unpack_pairs(w):
    lo = lax.bitcast_convert_type(w << 16, F32)
    hi = lax.bitcast_convert_type(w & -65536, F32)
    return jnp.concatenate([lo, hi], axis=-1)


def _dot(a, b):
    return jnp.dot(a, b, preferred_element_type=F32)


def _dot_nt(a, b):
    return lax.dot_general(a, b, (((1,), (1,)), ((), ())), preferred_element_type=F32)


def _dot_tn(a, b):
    return lax.dot_general(a, b, (((0,), (0,)), ((), ())), preferred_element_type=F32)


def _mod_kernel(ct_ref, w_ref, b_ref, o_ref, *, n_cond):
    ct = ct_ref[...]
    s = _silu(ct)
    w = w_ref[...]
    rows = [jnp.sum(w * s[:, r:r + 1], axis=0, keepdims=True) for r in range(n_cond)]
    rows.append(jnp.zeros((8 - n_cond, w.shape[1]), F32))
    o_ref[...] = jnp.concatenate(rows, axis=0) + b_ref[...]


def _modulation(cond, w_mod, b_mod):
    n_layers, d, n = w_mod.shape
    tn = 512
    ct = jnp.zeros((8, d), F32).at[:cond.shape[0]].set(cond).T
    return pl.pallas_call(
        functools.partial(_mod_kernel, n_cond=cond.shape[0]),
        grid=(n_layers, n // tn),
        in_specs=[
            pl.BlockSpec((d, 8), lambda l, j: (0, 0)),
            pl.BlockSpec((None, d, tn), lambda l, j: (l, 0, j)),
            pl.BlockSpec((None, 1, tn), lambda l, j: (l, 0, j)),
        ],
        out_specs=pl.BlockSpec((None, 8, tn), lambda l, j: (l, 0, j)),
        out_shape=jax.ShapeDtypeStruct((n_layers, 8, n), F32),
        compiler_params=_cparams("arbitrary", "arbitrary"),
        name="adaln_mod",
    )(ct, w_mod, b_mod.reshape(n_layers, 1, n))


def _mod_rows(mod_ref, i, tiles_per_mod, mod_base, first):
    r = mod_base + i // tiles_per_mod
    return [mod_ref[pl.ds(r, 1), (first + k) * D_MODEL:(first + k + 1) * D_MODEL] for k in range(3)]


def _in_kernel(x_ref, mod_ref, g_ref, w_ref, proj_ref, f_ref, sa_ref, sb_ref, *, tiles_per_mod, mod_base):
    i = pl.program_id(0)
    sh, sc, _ = _mod_rows(mod_ref, i, tiles_per_mod, mod_base, 0)
    h = (_rms(x_ref[...], g_ref[...]) * (1.0 + sc) + sh).astype(BF16)
    for j in range(IN_W // IN_TN):
        res = _dot(h, w_ref[:, j * IN_TN:(j + 1) * IN_TN])
        proj_ref[:, j * IN_TN:(j + 1) * IN_TN] = res.astype(BF16)
        if j == IN_F_TILE:
            f_ref[...] = res[:, IN_F_OFF:IN_F_OFF + BRANCH_W].astype(BF16)
            sa_ref[...] = res[:, IN_S_OFF:IN_S_OFF + 128]
            sb_ref[...] = res[:, IN_S_OFF + 128:IN_S_OFF + 256]


def _in_proj(x, mod, g, w_bf, *, rows_per_mod, mod_base):
    rows, d = x.shape
    tm = math.gcd(512, rows_per_mod)
    kern = functools.partial(_in_kernel, tiles_per_mod=max(rows_per_mod // tm, 1), mod_base=mod_base)
    return pl.pallas_call(
        kern,
        grid=(rows // tm,),
        in_specs=[
            pl.BlockSpec((tm, d), lambda i: (i, 0)),
            pl.BlockSpec(mod.shape, lambda i: (0, 0)),
            pl.BlockSpec((1, d), lambda i: (0, 0)),
            pl.BlockSpec((d, IN_W), lambda i: (0, 0), pipeline_mode=pl.Buffered(1)),
        ],
        out_specs=[
            pl.BlockSpec((tm, IN_W), lambda i: (i, 0)),
            pl.BlockSpec((tm, BRANCH_W), lambda i: (i, 0)),
            pl.BlockSpec((tm, 128), lambda i: (i, 0)),
            pl.BlockSpec((tm, 128), lambda i: (i, 0)),
        ],
        out_shape=[
            jax.ShapeDtypeStruct((rows, IN_W), BF16),
            jax.ShapeDtypeStruct((rows, BRANCH_W), BF16),
            jax.ShapeDtypeStruct((rows, 128), F32),
            jax.ShapeDtypeStruct((rows, 128), F32),
        ],
        compiler_params=_cparams("arbitrary"),
        name="in_proj",
    )(x, mod, g.reshape(1, d), w_bf)


def _fft_a_kernel(x_ref, cs_ref, tc_ref, ts_ref, zr_ref, zi_ref, *, n1, n1p):
    y = _dot(cs_ref[...].astype(BF16), x_ref[...])
    yr = y[:n1]
    yi = y[n1p:n1p + n1]
    tc = tc_ref[...]
    ts = ts_ref[...]
    zr_ref[...] = (yr * tc + yi * ts).astype(BF16)
    zi_ref[...] = (yi * tc - yr * ts).astype(BF16)


def _fft_b_kernel(zr_ref, zi_ref, cs_ref, cc_ref, sc_ref, oa_ref, ob_ref, *, kb, n1, scale, has_imag):
    cs = cs_ref[...].astype(BF16)
    cc = cc_ref[...].astype(BF16)
    sc = sc_ref[...].astype(BF16)
    half = BRANCH_W // 2
    for kk in range(kb):
        a = _dot(cs, zr_ref[kk])
        if has_imag:
            b = _dot(cs, zi_ref[kk])
            xr = a[:FFT_N2] + b[FFT_N2:]
            xi = b[:FFT_N2] - a[FFT_N2:]
        else:
            xr = a[:FFT_N2]
            xi = -a[FFT_N2:]
        out = (_dot(xr.astype(BF16), cc) + _dot(xi.astype(BF16), sc)) * scale
        k1 = pl.program_id(1) * kb + kk
        oa_ref[pl.ds(k1, FFT_N2, stride=n1), :] = out[:, :half]
        ob_ref[pl.ds(k1, FFT_N2, stride=n1), :] = out[:, half:]


def _dft_tables(n):
    k = np.arange(n)
    ang = 2.0 * np.pi * ((k[:, None] * k[None, :]) % n) / n
    return np.cos(ang), np.sin(ang)


def _fft_b_call(zr, zi, n1, batch, seq_len, has_imag):
    c2, s2 = _dft_tables(FFT_N2)
    cs2 = jnp.asarray(np.concatenate([c2, s2], axis=0), F32)
    c64, s64 = _dft_tables(FNET_GROUP_DIM)
    eye = np.eye(BRANCH_W // FNET_GROUP_DIM)
    cc = jnp.asarray(np.kron(eye, c64), F32)
    sc = jnp.asarray(np.kron(eye, s64), F32)
    kb = min(8, n1)
    scale = 1.0 / math.sqrt(seq_len * FNET_GROUP_DIM)
    kern = functools.partial(_fft_b_kernel, kb=kb, n1=n1, scale=scale, has_imag=has_imag)
    zspec = pl.BlockSpec((None, kb, FFT_N2, BRANCH_W), lambda b, i: (b, i, 0, 0))
    half = pl.BlockSpec((seq_len, BRANCH_W // 2), lambda b, i: (b, 0))
    return pl.pallas_call(
        kern,
        grid=(batch, n1 // kb),
        in_specs=[
            zspec, zspec,
            pl.BlockSpec((2 * FFT_N2, FFT_N2), lambda b, i: (0, 0)),
            pl.BlockSpec((BRANCH_W, BRANCH_W), lambda b, i: (0, 0)),
            pl.BlockSpec((BRANCH_W, BRANCH_W), lambda b, i: (0, 0)),
        ],
        out_specs=[half, half],
        out_shape=[jax.ShapeDtypeStruct((batch * seq_len, BRANCH_W // 2), F32)] * 2,
        compiler_params=_cparams("arbitrary", "arbitrary"),
        name="fourier_stage_b",
    )(zr, zi, cs2, cc, sc)


def _fourier_latent(f, batch, seq_len):
    n1 = seq_len // FFT_N2
    wide = FFT_N2 * BRANCH_W
    c1, s1 = _dft_tables(n1)
    n1p = max(n1, 8)
    cs1 = np.zeros((2 * n1p, n1))
    cs1[:n1] = c1
    cs1[n1p:n1p + n1] = -s1
    k1 = np.arange(n1)[:, None]
    l2 = np.arange(FFT_N2)[None, :]
    tw = 2.0 * np.pi * (k1 * l2) / seq_len
    tc = jnp.asarray(np.repeat(np.cos(tw), BRANCH_W, axis=1), F32)
    ts = jnp.asarray(np.repeat(np.sin(tw), BRANCH_W, axis=1), F32)
    cw = min(8192, wide)
    xv = f.reshape(batch, n1, wide)
    spec = pl.BlockSpec((None, n1, cw), lambda b, j: (b, 0, j))
    tspec = pl.BlockSpec((n1, cw), lambda b, j: (0, j))
    zr, zi = pl.pallas_call(
        functools.partial(_fft_a_kernel, n1=n1, n1p=n1p),
        grid=(batch, wide // cw),
        in_specs=[spec, pl.BlockSpec((2 * n1p, n1), lambda b, j: (0, 0)), tspec, tspec],
        out_specs=[spec, spec],
        out_shape=[jax.ShapeDtypeStruct((batch, n1, wide), BF16)] * 2,
        compiler_params=_cparams("arbitrary", "arbitrary"),
        name="fourier_stage_a",
    )(xv, jnp.asarray(cs1, F32), tc, ts)
    zr = zr.reshape(batch, n1, FFT_N2, BRANCH_W)
    zi = zi.reshape(batch, n1, FFT_N2, BRANCH_W)
    return _fft_b_call(zr, zi, n1, batch, seq_len, True)


def _fourier_ctx(f, batch, ctx_len):
    assert ctx_len == FFT_N2
    z = f.reshape(batch, 1, FFT_N2, BRANCH_W)
    return _fft_b_call(z, z, 1, batch, ctx_len, False)


def _s5_tables(a_re, a_im, log_dt, b_re, b_im, c_re, c_im, d_skip, batch):
    t = S5_CHUNK
    g, p, hc = S5_GROUPS, S5_STATE, S5_GROUP_CH
    lam = lax.complex(a_re.astype(F32), a_im.astype(F32))
    dt = jnp.exp(log_dt.astype(F32))[..., None]
    ks = jnp.arange(t + 1, dtype=F32)
    apow = jnp.exp((lam * dt)[..., None] * ks)
    a_bar = apow[..., 1]
    b_bar = ((a_bar - 1.0) / lam)[..., None] * lax.complex(b_re.astype(F32), b_im.astype(F32))
    cm = lax.complex(c_re.astype(F32), c_im.astype(F32))
    kimp = jnp.real(jnp.einsum('dghp,dgpk,dgpj->dgjkh', cm, apow[..., :t], b_bar,
                               precision=lax.Precision.HIGHEST))
    kf, kb = kimp[0], kimp[1]
    kfull = jnp.concatenate([kb[:, :, :0:-1], kf[:, :, :1] + kb[:, :, :1], kf[:, :, 1:]], axis=2)
    kp = kfull.reshape(S5_PAIRS, 2, hc, 2 * t - 1, hc)
    blk = [kp[:, gi] for gi in range(2)]
    zb = jnp.zeros_like(blk[0])
    strip = jnp.concatenate([jnp.stack([blk[0], zb], axis=3), jnp.stack([zb, blk[1]], axis=3)], axis=1)
    strip = strip.reshape(S5_PAIRS, 2 * hc, (2 * t - 1) * 2 * hc)
    strip = jnp.pad(strip, ((0, 0), (0, 0), (0, 2 * hc)))

    wf = jnp.einsum('gpj,gph->gjhp', apow[0][..., t - 1::-1][..., :t], b_bar[0])
    wb = jnp.einsum('gpj,gph->gjhp', apow[1][..., :t], b_bar[1])
    kinds = [jnp.real(wf), jnp.imag(wf), jnp.real(wb), jnp.imag(wb)]

    def we_pair(kd):
        k5 = kd.reshape(S5_PAIRS, 2, t, hc, p)
        z = jnp.zeros_like(k5[:, 0])
        rows = jnp.stack([jnp.concatenate([k5[:, 0], z], axis=-1), jnp.concatenate([z, k5[:, 1]], axis=-1)], axis=2)
        return rows.reshape(S5_PAIRS, 2 * t * hc, 2 * p)

    we = jnp.concatenate([we_pair(kd) for kd in kinds], axis=-1).astype(BF16)

    vf = jnp.einsum('ghp,gpt->gpth', cm[0], apow[0][..., 1:t + 1])
    vb = jnp.einsum('ghp,gpt->gpth', cm[1], apow[1][..., t:0:-1])
    vkinds = [jnp.real(vf), -jnp.imag(vf), jnp.real(vb), -jnp.imag(vb)]

    def v_pair(kd):
        k5 = kd.reshape(S5_PAIRS, 2, p, t, hc)
        z = jnp.zeros_like(k5[:, 0])
        rows = jnp.concatenate([jnp.stack([k5[:, 0], z], axis=3), jnp.stack([z, k5[:, 1]], axis=3)], axis=1)
        return rows.reshape(S5_PAIRS, 2 * p, 2 * t * hc)

    v1 = jnp.concatenate([v_pair(kd) for kd in vkinds], axis=1)
    v = jnp.concatenate([v1, v1], axis=1).astype(BF16)

    def lanes(z):
        return jnp.tile(z.reshape(1, g * p), (1, batch))

    at = apow[..., t]
    a_tab = jnp.concatenate([lanes(jnp.real(at[0])), lanes(jnp.imag(at[0])),
                             lanes(jnp.real(at[1])), lanes(jnp.imag(at[1]))], axis=0)
    dvec = jnp.tile(d_skip.astype(F32).reshape(S5_PAIRS, 1, 2 * hc), (1, t, 1)).reshape(S5_PAIRS, 1, 2 * t * hc)
    return dict(strip=strip, we=we, v=v, a_tab=a_tab, dvec=dvec)


def _s5_pack_kernel(xa_ref, xb_ref, u_ref, *, n_chunks):
    per_half = S5_PAIRS // 2
    for half, x_ref in enumerate((xa_ref, xb_ref)):
        rows = [x_ref[pl.ds(tau, n_chunks, stride=S5_CHUNK), :] for tau in range(S5_CHUNK)]
        for qq in range(per_half):
            pieces = [r[:, qq * 32:(qq + 1) * 32] for r in rows]
            u_ref[half * per_half + qq] = jnp.concatenate(pieces, axis=-1).astype(BF16)


def _s5_unpack_kernel(y_ref, oa_ref, ob_ref, *, n_chunks):
    per_half = S5_PAIRS // 2
    for half, o_ref in enumerate((oa_ref, ob_ref)):
        ys = [y_ref[half * per_half + qq].astype(F32) for qq in range(per_half)]
        for t in range(S5_CHUNK):
            pieces = [y[:, t * 32:(t + 1) * 32] for y in ys]
            o_ref[pl.ds(t, n_chunks, stride=S5_CHUNK), :] = jnp.concatenate(pieces, axis=-1)


def _s5_pack(sa, sb, batch):
    n_chunks = sa.shape[0] // batch // S5_CHUNK
    rows = n_chunks * S5_CHUNK
    cols = 2 * S5_CHUNK * S5_GROUP_CH
    half = pl.BlockSpec((rows, 128), lambda b: (b, 0))
    return pl.pallas_call(
        functools.partial(_s5_pack_kernel, n_chunks=n_chunks),
        grid=(batch,),
        in_specs=[half, half],
        out_specs=pl.BlockSpec((S5_PAIRS, None, n_chunks, cols), lambda b: (0, b, 0, 0)),
        out_shape=jax.ShapeDtypeStruct((S5_PAIRS, batch, n_chunks, cols), BF16),
        compiler_params=_cparams("arbitrary"),
        name="s5_pack",
    )(sa, sb)


def _s5_unpack(y, batch):
    n_chunks = y.shape[2]
    rows = n_chunks * S5_CHUNK
    cols = y.shape[3]
    half = pl.BlockSpec((rows, 128), lambda b: (b, 0))
    return pl.pallas_call(
        functools.partial(_s5_unpack_kernel, n_chunks=n_chunks),
        grid=(batch,),
        in_specs=[pl.BlockSpec((S5_PAIRS, None, n_chunks, cols), lambda b: (0, b, 0, 0))],
        out_specs=[half, half],
        out_shape=[jax.ShapeDtypeStruct((batch * rows, 128), F32)] * 2,
        compiler_params=_cparams("arbitrary"),
        name="s5_unpack",
    )(y)


def _s5_e_kernel(ul_ref, uc_ref, we_ref, ref_, imf_, reb_, imb_):
    u = jnp.concatenate([ul_ref[...], uc_ref[...]], axis=0)
    e = _dot(u, we_ref[...])
    ref_[...] = e[:, 0:128]
    imf_[...] = e[:, 128:256]
    reb_[...] = e[:, 256:384]
    imb_[...] = e[:, 384:512]


def _s5_scan_kernel(a_ref, ref_, imf_, reb_, imb_, prf, pif, prb, pib, *, n_rows, n_ctx):
    afr = a_ref[0:1, :]
    afi = a_ref[1:2, :]
    abr = a_ref[2:3, :]
    abi = a_ref[3:4, :]
    zero = jnp.zeros_like(afr)

    n_lat = n_rows - n_ctx

    def body(s, carry):
        sfr, sfi, sbr, sbi = carry
        nf = jnp.where(s < n_ctx, n_lat + s, s - n_ctx)
        nb = n_rows - 1 - s
        prf[pl.ds(nf, 1), :] = sfr
        pif[pl.ds(nf, 1), :] = sfi
        prb[pl.ds(nb, 1), :] = sbr
        pib[pl.ds(nb, 1), :] = sbi
        efr = ref_[pl.ds(nf, 1), :]
        efi = imf_[pl.ds(nf, 1), :]
        ebr = reb_[pl.ds(nb, 1), :]
        ebi = imb_[pl.ds(nb, 1), :]
        nfr = afr * sfr - afi * sfi + efr
        nfi = afr * sfi + afi * sfr + efi
        nbr = abr * sbr - abi * sbi + ebr
        nbi = abr * sbi + abi * sbr + ebi
        return nfr, nfi, nbr, nbi

    lax.fori_loop(0, n_rows, body, (zero, zero, zero, zero))


def _s5_y_kernel(ul_ref, uc_ref, strip_ref, v_ref, d_ref, prf, pif, prb, pib, yl_ref, yc_ref, m_scr):
    width = 2 * S5_GROUP_CH
    cols = S5_CHUNK * width
    n_lat = yl_ref.shape[0]

    @pl.when(pl.program_id(1) == 0)
    def _():
        strip = strip_ref[...]
        for j in range(S5_CHUNK):
            off = (S5_CHUNK - 1 - j) * width
            win = strip if off == 0 else pltpu.roll(strip, 2 * cols - off, axis=1)
            m_scr[j * width:(j + 1) * width, :] = win[:, :cols].astype(BF16)

    u = jnp.concatenate([ul_ref[...], uc_ref[...]], axis=0)
    y_intra = _dot(u, m_scr[...])
    pcat = jnp.concatenate([prf[...], pif[...], prb[...], pib[...]], axis=-1)
    hi, lo = _split_bf16(pcat)
    y_cross = _dot(jnp.concatenate([hi, lo], axis=-1), v_ref[...])
    y = y_intra + y_cross + d_ref[...] * u.astype(F32)
    yl_ref[...] = y[:n_lat].astype(BF16)
    yc_ref[...] = y[n_lat:].astype(BF16)


def _s5_core(ul, uc, tabs, layer, batch):
    n_lat, n_ctx = ul.shape[2], uc.shape[2]
    n_rows = n_lat + n_ctx
    width = batch * S5_PAIRS * 128
    cols = 2 * S5_CHUNK * S5_GROUP_CH
    ul_spec = pl.BlockSpec((None, None, n_lat, cols), lambda q, b: (q, b, 0, 0))
    uc_spec = pl.BlockSpec((None, None, n_ctx, cols), lambda q, b: (q, b, 0, 0))
    st_spec = pl.BlockSpec((n_rows, 128), lambda q, b: (0, b * S5_PAIRS + q))
    st_shape = jax.ShapeDtypeStruct((n_rows, width), F32)
    e4 = pl.pallas_call(
        _s5_e_kernel,
        grid=(S5_PAIRS, batch),
        in_specs=[ul_spec, uc_spec, pl.BlockSpec((None, None, cols, 512), lambda q, b: (layer, q, 0, 0))],
        out_specs=[st_spec] * 4,
        out_shape=[st_shape] * 4,
        compiler_params=_cparams("arbitrary", "arbitrary"),
        name="s5_chunk_states",
    )(ul, uc, tabs['we'])
    p4 = pl.pallas_call(
        functools.partial(_s5_scan_kernel, n_rows=n_rows, n_ctx=n_ctx),
        out_shape=[st_shape] * 4,
        compiler_params=pltpu.CompilerParams(vmem_limit_bytes=VMEM_LIMIT_BYTES),
        name="s5_state_scan",
    )(tabs['a_tab'][layer], *e4)
    y = pl.pallas_call(
        _s5_y_kernel,
        grid=(S5_PAIRS, batch),
        in_specs=[
            ul_spec, uc_spec,
            pl.BlockSpec((None, None, 2 * S5_GROUP_CH, 2 * cols), lambda q, b: (layer, q, 0, 0)),
            pl.BlockSpec((None, None, cols, cols), lambda q, b: (layer, q, 0, 0)),
            pl.BlockSpec((None, None, 1, cols), lambda q, b: (layer, q, 0, 0)),
            st_spec, st_spec, st_spec, st_spec,
        ],
        out_specs=[ul_spec, uc_spec],
        out_shape=[
            jax.ShapeDtypeStruct((S5_PAIRS, batch, n_lat, cols), BF16),
            jax.ShapeDtypeStruct((S5_PAIRS, batch, n_ctx, cols), BF16),
        ],
        scratch_shapes=[pltpu.VMEM((cols, cols), BF16)],
        compiler_params=_cparams("arbitrary", "arbitrary"),
        name="s5_outputs",
    )(ul, uc, tabs['strip'], tabs['v'], tabs['dvec'], *p4)
    return y


def _s5_mixer(s_lat, s_ctx, tabs, layer, batch):
    ul = _s5_pack(*s_lat, batch)
    uc = _s5_pack(*s_ctx, batch)
    yl, yc = _s5_core(ul, uc, tabs, layer, batch)
    return _s5_unpack(yl, batch), _s5_unpack(yc, batch)


def _ret_tables(ret_decay):
    c = RET_CHUNK
    lg = jax.nn.log_sigmoid(ret_decay.astype(F32))
    lane_h = np.repeat(np.arange(RET_HEADS), RET_DIM)
    lgl = jnp.repeat(lg, RET_DIM, axis=1)
    pos = jnp.arange(c, dtype=F32)[:, None]
    qd = jnp.stack([jnp.exp((pos + 1.0) * lgl[0][None]), jnp.exp((c - pos) * lgl[1][None])])
    kd = jnp.stack([jnp.exp((c - 1.0 - pos) * lgl[0][None]), jnp.exp(pos * lgl[1][None])])
    bmask = jnp.asarray((lane_h[:, None] == lane_h[None, :]).astype(np.float32))
    cd = jnp.exp(c * lgl)[:, :, None] * bmask[None]
    diff = pos - pos.T
    dm = []
    for h in range(RET_HEADS):
        fw = jnp.where(diff >= 0, jnp.exp(jnp.maximum(diff, 0.0) * lg[0, h]), 0.0)
        bw = jnp.where(diff <= 0, jnp.exp(jnp.maximum(-diff, 0.0) * lg[1, h]), 0.0)
        dm.append(fw + bw)
    dm = jnp.concatenate(dm, axis=0)
    return dict(qd=qd, kd=kd, cd=cd, dm=dm)


def _ret_masks():
    lane_h = np.repeat(np.arange(RET_HEADS), RET_DIM)
    bmask = (lane_h[:, None] == lane_h[None, :]).astype(np.float32)
    hmask = (np.arange(RET_HEADS)[:, None] == lane_h[None, :]).astype(np.float32)
    return jnp.asarray(bmask), jnp.asarray(hmask)


def _rope_tables(n_tokens):
    t = np.arange(n_tokens)
    row = (t // GRID_W).astype(np.float64)
    col = (t % GRID_W).astype(np.float64)
    n_freq = RET_DIM // 4
    inv_freq = 1.0 / (ROPE_BASE ** (np.arange(n_freq, dtype=np.float64) / n_freq))
    ang = np.concatenate([row[:, None] * inv_freq, col[:, None] * inv_freq], axis=-1)
    cos = np.cos(ang)
    sin = np.sin(ang)
    cos_t = np.tile(np.concatenate([cos, cos], axis=-1), (1, RET_HEADS))
    sin_t = np.tile(np.concatenate([-sin, sin], axis=-1), (1, RET_HEADS))
    half = RET_DIM // 2
    perm = np.arange(BRANCH_W) ^ half
    swap = np.zeros((BRANCH_W, BRANCH_W), np.float32)
    swap[perm, np.arange(BRANCH_W)] = 1.0
    return jnp.asarray(cos_t, F32), jnp.asarray(sin_t, F32), jnp.asarray(swap, BF16)


def _ret_chunk(q, k, v, s, qd, kd, cd, bmask, dm, hmask, with_intra):
    cross = _dot((q * qd).astype(BF16), s.astype(BF16))
    s_new = cd * s + bmask * _dot_tn((k * kd).astype(BF16), v)
    if not with_intra:
        return cross, s_new
    qb = q.astype(BF16)
    kb = k.astype(BF16)
    qs = jnp.concatenate([qb * hmask[h:h + 1].astype(BF16) for h in range(RET_HEADS)], axis=0)
    scores = _dot_nt(qs, kb) * dm
    ov = _dot(scores.astype(BF16), v)
    c = q.shape[0]
    inner = ov[0:c] * hmask[0:1]
    for h in range(1, RET_HEADS):
        inner = inner + ov[h * c:(h + 1) * c] * hmask[h:h + 1]
    return inner + cross, s_new


def _ret_kernel(qf_ref, kf_ref, vf_ref, qb_ref, kb_ref, vb_ref, qc_ref, kc_ref, vc_ref,
                cosf_ref, sinf_ref, cosb_ref, sinb_ref, swap_ref,
                qd_ref, kd_ref, cd_ref, bm_ref, dm_ref, hm_ref,
                of_ref, ob_ref, ocf_ref, ocb_ref, sf_scr, sb_scr, *, n_chunks, n_ctx_chunks):
    i = pl.program_id(1)
    c = RET_CHUNK
    k_scale = RET_DIM ** -0.5
    bmask = bm_ref[...]
    dm = dm_ref[...]
    hmask = hm_ref[...]
    tabs = [(qd_ref[d], kd_ref[d], cd_ref[d]) for d in range(2)]

    @pl.when(i == 0)
    def _():
        for d, oc_ref, s_scr in ((0, ocf_ref, sf_scr), (1, ocb_ref, sb_scr)):
            qd, kd, cd = tabs[d]
            s = jnp.zeros((BRANCH_W, BRANCH_W), F32)
            order = range(n_ctx_chunks) if d == 0 else range(n_ctx_chunks - 1, -1, -1)
            for cc in order:
                sl = slice(cc * c, (cc + 1) * c)
                o, s = _ret_chunk(qc_ref[sl, :].astype(F32), kc_ref[sl, :].astype(F32) * k_scale, vc_ref[sl, :],
                                  s, qd, kd, cd, bmask, dm, hmask, d == 0)
                oc_ref[sl, :] = o
            s_scr[...] = s

    swap = swap_ref[...]

    def rope(x_ref, cos_ref, sin_ref):
        xb = x_ref[...]
        return xb.astype(F32) * cos_ref[...] + _dot(xb, swap) * sin_ref[...]

    q_f = rope(qf_ref, cosf_ref, sinf_ref)
    k_f = rope(kf_ref, cosf_ref, sinf_ref) * k_scale
    q_b = rope(qb_ref, cosb_ref, sinb_ref)
    k_b = rope(kb_ref, cosb_ref, sinb_ref) * k_scale
    sf = sf_scr[...]
    sb = sb_scr[...]
    for step in range(n_chunks):
        sl = slice(step * c, (step + 1) * c)
        o, sf = _ret_chunk(q_f[sl], k_f[sl], vf_ref[sl, :], sf, *tabs[0], bmask, dm, hmask, True)
        of_ref[sl, :] = o
        cb = n_chunks - 1 - step
        sl = slice(cb * c, (cb + 1) * c)
        o, sb = _ret_chunk(q_b[sl], k_b[sl], vb_ref[sl, :], sb, *tabs[1], bmask, dm, hmask, False)
        ob_ref[sl, :] = o
    sf_scr[...] = sf
    sb_scr[...] = sb


def _retention(proj_l, proj_c, tabs, layer, masks, rope, batch, seq_len, ctx_len):
    n_chunks = 4
    blk = n_chunks * RET_CHUNK
    nblk = seq_len // blk
    cos_t, sin_t, swap = rope

    def lat(col, back):
        if back:
            return pl.BlockSpec((blk, BRANCH_W), lambda b, i: (b * nblk + nblk - 1 - i, col))
        return pl.BlockSpec((blk, BRANCH_W), lambda b, i: (b * nblk + i, col))

    def ctx(col):
        return pl.BlockSpec((ctx_len, BRANCH_W), lambda b, i: (b, col))

    def const(shape):
        return pl.BlockSpec(shape, lambda b, i: (0,) * len(shape))

    def per_layer(shape):
        return pl.BlockSpec((None,) + shape, lambda b, i: (layer,) + (0,) * len(shape))

    tab_f = pl.BlockSpec((blk, BRANCH_W), lambda b, i: (i, 0))
    tab_b = pl.BlockSpec((blk, BRANCH_W), lambda b, i: (nblk - 1 - i, 0))
    kern = functools.partial(_ret_kernel, n_chunks=n_chunks, n_ctx_chunks=ctx_len // RET_CHUNK)
    c = RET_CHUNK
    ctx_out = pl.BlockSpec((ctx_len, BRANCH_W), lambda b, i: (b, 0))
    o_f, o_b, oc_f, oc_b = pl.pallas_call(
        kern,
        grid=(batch, nblk),
        in_specs=[
            lat(COL_RQ, False), lat(COL_RK, False), lat(COL_RV, False),
            lat(COL_RQ, True), lat(COL_RK, True), lat(COL_RV, True),
            ctx(COL_RQ), ctx(COL_RK), ctx(COL_RV),
            tab_f, tab_f, tab_b, tab_b, const((BRANCH_W, BRANCH_W)),
            per_layer((2, c, BRANCH_W)), per_layer((2, c, BRANCH_W)), per_layer((2, BRANCH_W, BRANCH_W)),
            const((BRANCH_W, BRANCH_W)), per_layer((RET_HEADS * c, c)), const((RET_HEADS, BRANCH_W)),
        ],
        out_specs=[lat(0, False), lat(0, True), ctx_out, ctx_out],
        out_shape=[
            jax.ShapeDtypeStruct((batch * seq_len, BRANCH_W), F32),
            jax.ShapeDtypeStruct((batch * seq_len, BRANCH_W), F32),
            jax.ShapeDtypeStruct((batch * ctx_len, BRANCH_W), F32),
            jax.ShapeDtypeStruct((batch * ctx_len, BRANCH_W), F32),
        ],
        scratch_shapes=[pltpu.VMEM((BRANCH_W, BRANCH_W), F32), pltpu.VMEM((BRANCH_W, BRANCH_W), F32)],
        compiler_params=_cparams("arbitrary", "arbitrary"),
        name="retention",
    )(proj_l, proj_l, proj_l, proj_l, proj_l, proj_l, proj_c, proj_c, proj_c,
      cos_t, sin_t, cos_t, sin_t, swap,
      tabs['qd'], tabs['kd'], tabs['cd'], masks[0], tabs['dm'], masks[1])
    return (o_f, o_b), (oc_f, oc_b)


def _na_tables(rpb):
    kr, kw = NA_WIN_ROWS, NA_WIN_COLS
    col = np.arange(GRID_W)
    col_start = np.clip(col - kw // 2, 0, GRID_W - kw)
    in_win = (col[None, :] >= col_start[:, None]) & (col[None, :] < col_start[:, None] + kw)
    dc = np.clip(col[None, :] - col[:, None], -(kw - 1), kw - 1) + (kw - 1)
    pick_c = (dc[:, :, None] == np.arange(2 * kw - 1)[None, None, :]).astype(np.float32)
    by = jnp.einsum('hrc,qkc->hqrk', rpb.astype(F32), jnp.asarray(pick_c), precision=lax.Precision.HIGHEST)
    by = jnp.where(jnp.asarray(in_win)[None, :, None, :], by, NEG_BIG)
    bias = jnp.stack([by[:, :, v:v + kr, :] for v in range(kr)], axis=0)
    return bias.reshape(kr, NA_HEADS * GRID_W, kr * GRID_W)


def _na_head_mask():
    lane_h = np.repeat(np.arange(NA_HEADS), NA_DIM)
    hmask = (np.arange(NA_HEADS)[:, None] == lane_h[None, :]).astype(np.float32)
    return jnp.asarray(hmask, F32)


def _attend(qs, keys, vals, bias, kc, vc):
    s_ctx = _dot_nt(qs, kc)
    m = jnp.max(s_ctx, axis=-1, keepdims=True)
    if keys is not None:
        s_band = _dot_nt(qs, keys) + bias
        m = jnp.maximum(m, jnp.max(s_band, axis=-1, keepdims=True))
        p_band = jnp.exp(s_band - m)
    p_ctx = jnp.exp(s_ctx - m)
    l = jnp.sum(p_ctx, axis=-1, keepdims=True)
    o = _dot(p_ctx.astype(BF16), vc)
    if keys is not None:
        l = l + jnp.sum(p_band, axis=-1, keepdims=True)
        o = o + _dot(p_band.astype(BF16), vals)
    return o / l


def _stack_heads(q, hmask_scaled):
    return jnp.concatenate([q * hmask_scaled[h:h + 1] for h in range(NA_HEADS)], axis=0)


def _unstack_heads(o, hmask, n):
    out = o[0:n] * hmask[0:1]
    for h in range(1, NA_HEADS):
        out = out + o[h * n:(h + 1) * n] * hmask[h:h + 1]
    return out


def _na_kernel(q_ref, k_ref, v_ref, kc_ref, vc_ref, bias_ref, hm_ref, o_ref, *, n_grid_rows):
    i = pl.program_id(1)
    hmask = hm_ref[...]
    hms = (hmask * (NA_DIM ** -0.5)).astype(BF16)
    kc = kc_ref[...]
    vc = vc_ref[...]
    band = NA_WIN_ROWS * GRID_W
    for rr in range(NA_QROWS):
        r = i * NA_QROWS + rr
        rs = jnp.clip(r - NA_WIN_ROWS // 2, 0, n_grid_rows - NA_WIN_ROWS)
        var = rs - r + (NA_WIN_ROWS - 1)
        start = pl.multiple_of(rs * GRID_W, GRID_W)
        keys = k_ref[pl.ds(start, band), :]
        vals = v_ref[pl.ds(start, band), :]
        qs = _stack_heads(q_ref[rr * GRID_W:(rr + 1) * GRID_W, :], hms)
        o = _attend(qs, keys, vals, bias_ref[var], kc, vc)
        o_ref[rr * GRID_W:(rr + 1) * GRID_W, :] = _unstack_heads(o, hmask, GRID_W).astype(BF16)


def _na_ctx_kernel(q_ref, kc_ref, vc_ref, hm_ref, o_ref):
    hmask = hm_ref[...]
    hms = (hmask * (NA_DIM ** -0.5)).astype(BF16)
    n = q_ref.shape[0]
    o = _attend(_stack_heads(q_ref[...], hms), None, None, None, kc_ref[...], vc_ref[...])
    o_ref[...] = _unstack_heads(o, hmask, n).astype(BF16)


def _neighborhood(proj_l, proj_c, bias, layer, hmask, batch, seq_len, ctx_len, need_ctx_out):
    rows = seq_len // GRID_W
    qblk = NA_QROWS * GRID_W
    nq = seq_len // qblk
    out_l = pl.pallas_call(
        functools.partial(_na_kernel, n_grid_rows=rows),
        grid=(batch, nq),
        in_specs=[
            pl.BlockSpec((qblk, BRANCH_W), lambda b, i: (b * nq + i, COL_NQ)),
            pl.BlockSpec((seq_len, BRANCH_W), lambda b, i: (b, COL_NK)),
            pl.BlockSpec((seq_len, BRANCH_W), lambda b, i: (b, COL_NV)),
            pl.BlockSpec((ctx_len, BRANCH_W), lambda b, i: (b, COL_NK)),
            pl.BlockSpec((ctx_len, BRANCH_W), lambda b, i: (b, COL_NV)),
            pl.BlockSpec((None,) + bias.shape[1:], lambda b, i: (layer, 0, 0, 0)),
            pl.BlockSpec(hmask.shape, lambda b, i: (0, 0)),
        ],
        out_specs=pl.BlockSpec((qblk, BRANCH_W), lambda b, i: (b * nq + i, 0)),
        out_shape=jax.ShapeDtypeStruct((batch * seq_len, BRANCH_W), BF16),
        compiler_params=_cparams("arbitrary", "arbitrary"),
        name="neighborhood_attn",
    )(proj_l, proj_l, proj_l, proj_c, proj_c, bias, hmask)
    out_c = None
    if need_ctx_out:
        out_c = pl.pallas_call(
            _na_ctx_kernel,
            grid=(batch,),
            in_specs=[
                pl.BlockSpec((ctx_len, BRANCH_W), lambda b: (b, COL_NQ)),
                pl.BlockSpec((ctx_len, BRANCH_W), lambda b: (b, COL_NK)),
                pl.BlockSpec((ctx_len, BRANCH_W), lambda b: (b, COL_NV)),
                pl.BlockSpec(hmask.shape, lambda b: (0, 0)),
            ],
            out_specs=pl.BlockSpec((ctx_len, BRANCH_W), lambda b: (b, 0)),
            out_shape=jax.ShapeDtypeStruct((batch * ctx_len, BRANCH_W), BF16),
            compiler_params=_cparams("arbitrary"),
            name="context_attn",
        )(proj_c, proj_c, proj_c, hmask)
    return out_l, out_c


def _merge_kernel(x_ref, mod_ref, g_ref, gt0, gt1, gt2, gt3, fa_ref, fb_ref, s5a_ref, s5b_ref, rof_ref, rob_ref, rg_ref, na_ref,
                  wglu_ref, bglu_ref, gn_ref, avg_ref, wb_ref, wo_ref, o_ref, *, tiles_per_mod, mod_base):
    i = pl.program_id(0)
    _, _, gate_a = _mod_rows(mod_ref, i, tiles_per_mod, mod_base, 0)
    z = _gelu_tanh(jnp.concatenate([s5a_ref[...], s5b_ref[...]], axis=-1)).astype(BF16)
    zf = z.astype(F32)
    b_s5 = (zf * _sigmoid(_dot(z, wglu_ref[...]) + bglu_ref[...])).astype(BF16)
    o = rof_ref[...] + rob_ref[...]
    avg = avg_ref[...]
    hi, lo = _split_bf16(o)
    mu = _dot(hi, avg) + _dot(lo, avg)
    dlt = o - mu
    hi, lo = _split_bf16(dlt * dlt)
    var = _dot(hi, avg) + _dot(lo, avg)
    hn = dlt * lax.rsqrt(var + EPS) * gn_ref[...]
    b_ret = (_silu(rg_ref[...].astype(F32)) * hn).astype(BF16)
    b_fnet = jnp.concatenate([fa_ref[...], fb_ref[...]], axis=-1).astype(BF16)
    outs = (b_fnet, b_s5, b_ret, na_ref[...])
    gates = (gt0, gt1, gt2, gt3)
    y = (1.0 + jnp.tanh(gates[0][...].astype(F32))) * _dot(outs[0], wb_ref[0])
    for b in range(1, N_BRANCH):
        y = y + (1.0 + jnp.tanh(gates[b][...].astype(F32))) * _dot(outs[b], wb_ref[b])
    yo = _dot(y.astype(BF16), wo_ref[...])
    o_ref[...] = x_ref[...] + gate_a * _rms(yo, g_ref[...])


def _merge(x, mod, g1, proj, a, s5y, ret_o, na, lw, *, rows_per_mod, mod_base):
    rows, d = x.shape
    tm = min(512, rows)
    nt = rows // tm

    def row(shape, col=0):
        return pl.BlockSpec(shape, lambda i: (i, col))

    def const(arr):
        return pl.BlockSpec(arr.shape, lambda i: (0,) * arr.ndim)

    kern = functools.partial(_merge_kernel, tiles_per_mod=max(rows_per_mod // tm, 1), mod_base=mod_base)
    ins = [x, mod, g1.reshape(1, d), proj, proj, proj, proj, a[0], a[1], s5y[0], s5y[1], ret_o[0], ret_o[1], proj, na,
           lw['w_glu'], lw['b_glu'], lw['ret_gn'], lw['avg'], lw['w_branch'], lw['w_out']]
    specs = [
        row((tm, d)), const(mod), pl.BlockSpec((1, d), lambda i: (0, 0)),
        row((tm, d), 0), row((tm, d), 1), row((tm, d), 2), row((tm, d), 3),
        row((tm, 128)), row((tm, 128)), row((tm, 128)), row((tm, 128)),
        row((tm, BRANCH_W)), row((tm, BRANCH_W)),
        row((tm, BRANCH_W), COL_RG), row((tm, BRANCH_W)),
        const(lw['w_glu']), const(lw['b_glu']), const(lw['ret_gn']), const(lw['avg']),
        const(lw['w_branch']), const(lw['w_out']),
    ]
    return pl.pallas_call(
        kern,
        grid=(nt,),
        in_specs=specs,
        out_specs=row((tm, d)),
        out_shape=jax.ShapeDtypeStruct((rows, d), F32),
        compiler_params=_cparams("arbitrary"),
        name="merge_out",
    )(*ins)


def _ffn_kernel(x_ref, mod_ref, g2_ref, g3_ref, wg_ref, wu_ref, wd_ref, o_ref, *, tiles_per_mod, mod_base):
    i = pl.program_id(0)
    sh, sc, gate_f = _mod_rows(mod_ref, i, tiles_per_mod, mod_base, 3)
    x = x_ref[...]
    h = (_rms(x, g2_ref[...]) * (1.0 + sc) + sh).astype(BF16)
    act = (_silu(_dot(h, wg_ref[...])) * _dot(h, wu_ref[...])).astype(BF16)
    y = _dot(act, wd_ref[...])
    o_ref[...] = x + gate_f * _rms(y, g3_ref[...])


def _ffn_dense(x, mod, g2, g3, wg, wu, wd, *, rows_per_mod, mod_base):
    rows, d = x.shape
    d_ff = wg.shape[1]
    tm = min(512, rows)
    kern = functools.partial(_ffn_kernel, tiles_per_mod=max(rows_per_mod // tm, 1), mod_base=mod_base)

    def resident(shape):
        return pl.BlockSpec(shape, lambda i: (0, 0), pipeline_mode=pl.Buffered(1))

    return pl.pallas_call(
        kern,
        grid=(rows // tm,),
        in_specs=[
            pl.BlockSpec((tm, d), lambda i: (i, 0)),
            pl.BlockSpec(mod.shape, lambda i: (0, 0)),
            pl.BlockSpec((1, d), lambda i: (0, 0)),
            pl.BlockSpec((1, d), lambda i: (0, 0)),
            resident((d, d_ff)), resident((d, d_ff)), resident((d_ff, d)),
        ],
        out_specs=pl.BlockSpec((tm, d), lambda i: (i, 0)),
        out_shape=jax.ShapeDtypeStruct((rows, d), F32),
        compiler_params=_cparams("arbitrary"),
        name="ffn_dense",
    )(x, mod, g2.reshape(1, d), g3.reshape(1, d), wg, wu, wd)


def _router_kernel(x_ref, mod_ref, g2_ref, wr_ref, br_ref, tri_ref, h_ref, comb_ref, plan_ref, cnt_ref, cnt_scr,
                   *, tiles_per_mod, mod_base):
    i = pl.program_id(0)

    @pl.when(i == 0)
    def _():
        cnt_scr[...] = jnp.zeros_like(cnt_scr)

    sh, sc, _ = _mod_rows(mod_ref, i, tiles_per_mod, mod_base, 3)
    h = _rms(x_ref[...], g2_ref[...]) * (1.0 + sc) + sh
    h_ref[...] = _pack_pairs(h)
    h_hi, h_lo = _split_bf16(h)
    w_hi, w_lo = _split_bf16(wr_ref[...])
    logits = _dot(h_hi, w_hi) + _dot(h_lo, w_hi) + _dot(h_hi, w_lo) + br_ref[...]
    lane = lax.broadcasted_iota(jnp.int32, logits.shape, 1)
    v1 = jnp.max(logits, axis=-1, keepdims=True)
    i1 = jnp.min(jnp.where(logits == v1, lane, 128), axis=-1, keepdims=True)
    rest = jnp.where(lane == i1, NEG_BIG, logits)
    v2 = jnp.max(rest, axis=-1, keepdims=True)
    i2 = jnp.min(jnp.where(rest == v2, lane, 128), axis=-1, keepdims=True)
    e = jnp.exp(v2 - v1)
    w1 = 1.0 / (1.0 + e)
    w2 = e / (1.0 + e)
    meta = jnp.where(lane == 0, i1.astype(F32), 0.0) + jnp.where(lane == 1, i2.astype(F32), 0.0)
    meta = meta + jnp.where(lane == 2, w1, 0.0) + jnp.where(lane == 3, w2, 0.0)
    member = jnp.where((lane == i1) | (lane == i2), 1.0, 0.0)
    before = _dot(tri_ref[...], member.astype(BF16)) + cnt_scr[...]
    rank1 = jnp.sum(jnp.where(lane == i1, before, 0.0), axis=-1, keepdims=True)
    rank2 = jnp.sum(jnp.where(lane == i2, before, 0.0), axis=-1, keepdims=True)
    cnt_scr[...] += jnp.sum(member, axis=0, keepdims=True)
    cnt_ref[...] = cnt_scr[...]
    meta = meta + jnp.where(lane == 4, rank1, 0.0) + jnp.where(lane == 5, rank2, 0.0)
    comb_ref[...] = meta[:, :MOE_META_W]
    plan_ref[...] = meta.T[:MOE_META_W]


def _router(x, mod, g2, w_router, b_router, *, rows_per_mod, mod_base):
    rows, d = x.shape
    tm = min(512, rows)
    wr = jnp.zeros((d, 128), F32).at[:, :N_EXPERTS].set(w_router)
    br = jnp.full((1, 128), NEG_BIG, F32).at[0, :N_EXPERTS].set(b_router)
    tri = jnp.asarray(np.tril(np.ones((tm, tm), np.float32), -1), BF16)
    kern = functools.partial(_router_kernel, tiles_per_mod=max(rows_per_mod // tm, 1), mod_base=mod_base)
    return pl.pallas_call(
        kern,
        grid=(rows // tm,),
        in_specs=[
            pl.BlockSpec((tm, d), lambda i: (i, 0)),
            pl.BlockSpec(mod.shape, lambda i: (0, 0)),
            pl.BlockSpec((1, d), lambda i: (0, 0)),
            pl.BlockSpec((d, 128), lambda i: (0, 0)),
            pl.BlockSpec((1, 128), lambda i: (0, 0)),
            pl.BlockSpec((tm, tm), lambda i: (0, 0)),
        ],
        out_specs=[
            pl.BlockSpec((tm, d // 2), lambda i: (i, 0)),
            pl.BlockSpec((tm, MOE_META_W), lambda i: (i, 0)),
            pl.BlockSpec((MOE_META_W, tm), lambda i: (0, i)),
            pl.BlockSpec((1, 128), lambda i: (0, 0)),
        ],
        out_shape=[
            jax.ShapeDtypeStruct((rows, d // 2), jnp.int32),
            jax.ShapeDtypeStruct((rows, MOE_META_W), F32),
            jax.ShapeDtypeStruct((MOE_META_W, rows), F32),
            jax.ShapeDtypeStruct((1, 128), F32),
        ],
        scratch_shapes=[pltpu.VMEM((1, 128), F32)],
        compiler_params=_cparams("arbitrary"),
        name="moe_router",
    )(x, mod, g2.reshape(1, d), wr, br, tri)


def _sc_gather(table, idx):
    n_idx = idx.shape[0]
    width = table.shape[1]
    per_worker = n_idx // SC_WORKERS
    chunk_rows = math.gcd(per_worker, SC_GATHER_ROWS)
    n_chunks = per_worker // chunk_rows
    assert per_worker * SC_WORKERS == n_idx and chunk_rows % 8 == 0
    mesh = plsc.VectorSubcoreMesh(core_axis_name="c", subcore_axis_name="s")

    assert n_chunks % 2 == 0
    buf = [pltpu.VMEM((chunk_rows,), jnp.int32), pltpu.VMEM((chunk_rows, width), table.dtype),
           pltpu.SemaphoreType.DMA, pltpu.SemaphoreType.DMA]

    @functools.partial(
        pl.kernel, mesh=mesh,
        out_type=jax.ShapeDtypeStruct((n_idx, width), table.dtype),
        scratch_types=buf + buf,
        name="sc_row_gather",
    )
    def gather(table_hbm, idx_hbm, out_hbm, idx0, rows0, g0, w0, idx1, rows1, g1, w1):
        wid = lax.axis_index("s") * SC_CORES + lax.axis_index("c")
        base = wid * per_worker
        slots = ((idx0, rows0, g0, w0), (idx1, rows1, g1, w1))

        def fetch(j, slot):
            idx_v, rows_v, g, _ = slots[slot]
            pltpu.sync_copy(idx_hbm.at[pl.ds(base + j * chunk_rows, chunk_rows)], idx_v)
            pltpu.make_async_copy(table_hbm.at[idx_v], rows_v, g).start()

        def store(j, slot):
            idx_v, rows_v, g, w = slots[slot]
            pltpu.make_async_copy(table_hbm.at[idx_v], rows_v, g).wait()
            pltpu.make_async_copy(rows_v, out_hbm.at[pl.ds(base + j * chunk_rows, chunk_rows)], w).start()

        def drain(j, slot):
            _, rows_v, _, w = slots[slot]
            pltpu.make_async_copy(rows_v, out_hbm.at[pl.ds(base + j * chunk_rows, chunk_rows)], w).wait()

        fetch(0, 0)

        @pl.loop(0, n_chunks // 2)
        def _(jj):
            j = 2 * jj

            @pl.when(jj > 0)
            def _():
                drain(j - 1, 1)

            fetch(j + 1, 1)
            store(j, 0)

            @pl.when(j + 2 < n_chunks)
            def _():
                drain(j, 0)
                fetch(j + 2, 0)

            store(j + 1, 1)

        drain(n_chunks - 2, 0)
        drain(n_chunks - 1, 1)

    return gather(table, idx)


def _sc_scatter(table, idx, n_out):
    n_idx = idx.shape[0]
    rows, width = table.shape
    per_worker = n_idx // SC_WORKERS
    chunk_rows = math.gcd(per_worker, SC_GATHER_ROWS)
    n_chunks = per_worker // chunk_rows
    assert per_worker * SC_WORKERS == n_idx and chunk_rows % 8 == 0 and rows % per_worker == 0
    mesh = plsc.VectorSubcoreMesh(core_axis_name="c", subcore_axis_name="s")

    assert n_chunks % 2 == 0
    buf = [pltpu.VMEM((chunk_rows,), jnp.int32), pltpu.VMEM((chunk_rows, width), table.dtype),
           pltpu.SemaphoreType.DMA, pltpu.SemaphoreType.DMA]

    @functools.partial(
        pl.kernel, mesh=mesh,
        out_type=jax.ShapeDtypeStruct((n_out, width), table.dtype),
        scratch_types=buf + buf,
        name="sc_row_scatter",
    )
    def scatter(table_hbm, idx_hbm, out_hbm, idx0, rows0, l0, w0, idx1, rows1, l1, w1):
        wid = lax.axis_index("s") * SC_CORES + lax.axis_index("c")
        base = wid * per_worker
        slots = ((idx0, rows0, l0, w0), (idx1, rows1, l1, w1))

        def src(j):
            return table_hbm.at[pl.ds(lax.rem(base + j * chunk_rows, rows), chunk_rows)]

        def fetch(j, slot):
            idx_v, rows_v, l, _ = slots[slot]
            pltpu.sync_copy(idx_hbm.at[pl.ds(base + j * chunk_rows, chunk_rows)], idx_v)
            pltpu.make_async_copy(src(j), rows_v, l).start()

        def store(j, slot):
            idx_v, rows_v, l, w = slots[slot]
            pltpu.make_async_copy(src(j), rows_v, l).wait()
            pltpu.make_async_copy(rows_v, out_hbm.at[idx_v], w).start()

        def drain(slot):
            idx_v, rows_v, _, w = slots[slot]
            pltpu.make_async_copy(rows_v, out_hbm.at[idx_v], w).wait()

        fetch(0, 0)

        @pl.loop(0, n_chunks // 2)
        def _(jj):
            j = 2 * jj

            @pl.when(jj > 0)
            def _():
                drain(1)

            fetch(j + 1, 1)
            store(j, 0)

            @pl.when(j + 2 < n_chunks)
            def _():
                drain(0)
                fetch(j + 2, 0)

            store(j + 1, 1)

        drain(0)
        drain(1)

    return scatter(table, idx)


def _moe_plan(plan, counts_row, rows):
    tile = MOE_ROW_TILE
    n_tiles = (2 * rows) // tile + N_EXPERTS
    n_slots = n_tiles * tile
    counts = counts_row[0, :N_EXPERTS].astype(jnp.int32)
    padded = ((counts + tile - 1) // tile) * tile
    ends = jnp.cumsum(padded)
    starts = ends - padded
    ids = jnp.arange(N_EXPERTS, dtype=F32)[:, None]
    start_f = starts.astype(F32)[:, None]

    def slot(e_row, r_row):
        return jnp.sum(jnp.where(e_row[None, :] == ids, start_f, 0.0), axis=0) + r_row

    pos = jnp.concatenate([slot(plan[0], plan[4]), slot(plan[1], plan[5])])
    tile_start = jnp.arange(n_tiles, dtype=jnp.int32) * tile
    used = tile_start < ends[-1]
    tile_e = jnp.minimum(jnp.sum((tile_start[:, None] >= ends[None, :]).astype(jnp.int32), axis=1), N_EXPERTS - 1)
    last_e = jnp.max(jnp.where(used, tile_e, 0))
    tile_e = jnp.where(used, tile_e, last_e)
    valid_end = jnp.sum((tile_e[:, None] == jnp.arange(N_EXPERTS)[None, :]) * (starts + counts)[None, :], axis=1)
    n_valid = jnp.where(used, jnp.clip(valid_end - tile_start, 0, tile), 0).astype(jnp.int32)
    return pos.astype(jnp.int32), n_slots, tile_e.astype(jnp.int32), n_valid


def _moe_group_kernel(eid_ref, nval_ref, hs_ref, wg_ref, wu_ref, wd_ref, y_ref, acc_scr, *, n_f):
    w = pl.program_id(0)
    f = pl.program_id(1)
    nv = nval_ref[w]

    def run(n_rows):
        wg = wg_ref[...].astype(BF16)
        wu = wu_ref[...].astype(BF16)
        wd = wd_ref[...].astype(BF16)
        for r0 in range(0, n_rows, MOE_SUB_ROWS):
            rows = slice(r0, r0 + MOE_SUB_ROWS)
            hv = _unpack_pairs(hs_ref[rows, :])
            row = r0 + lax.broadcasted_iota(jnp.int32, hv.shape, 0)
            h = jnp.where(row < nv, hv, 0.0).astype(BF16)
            part = _dot((_silu(_dot(h, wg)) * _dot(h, wu)).astype(BF16), wd)
            acc = jnp.where(f == 0, 0.0, acc_scr[rows, :]) + part
            acc_scr[rows, :] = acc
            y_ref[rows, :] = _pack_pairs(acc)

    for groups in range(1, hs_ref.shape[0] // MOE_SUB_ROWS + 1):
        @pl.when((nv > (groups - 1) * MOE_SUB_ROWS) & (nv <= groups * MOE_SUB_ROWS))
        def _(groups=groups):
            run(groups * MOE_SUB_ROWS)


def _moe_grouped(hs, tile_e, n_valid, wg, wu, wd):
    n_slots = hs.shape[0]
    d = wg.shape[1]
    d_ff = wg.shape[2]
    tile = MOE_ROW_TILE
    tf = MOE_FF_TILE
    n_f = d_ff // tf

    def f_idx(f, nval, w):
        return jnp.where(nval[w] > 0, f, n_f - 1)

    grid_spec = pltpu.PrefetchScalarGridSpec(
        num_scalar_prefetch=2,
        grid=(n_slots // tile, n_f),
        in_specs=[
            pl.BlockSpec((tile, d // 2), lambda w, f, eid, nval: (w, 0)),
            pl.BlockSpec((None, d, tf), lambda w, f, eid, nval: (eid[w], 0, f_idx(f, nval, w))),
            pl.BlockSpec((None, d, tf), lambda w, f, eid, nval: (eid[w], 0, f_idx(f, nval, w))),
            pl.BlockSpec((None, tf, d), lambda w, f, eid, nval: (eid[w], f_idx(f, nval, w), 0)),
        ],
        out_specs=pl.BlockSpec((tile, d // 2), lambda w, f, eid, nval: (w, 0)),
        scratch_shapes=[pltpu.VMEM((tile, d), F32)],
    )
    return pl.pallas_call(
        functools.partial(_moe_group_kernel, n_f=n_f),
        grid_spec=grid_spec,
        out_shape=jax.ShapeDtypeStruct((n_slots, d // 2), jnp.int32),
        compiler_params=_cparams("arbitrary", "arbitrary"),
        name="moe_experts",
    )(tile_e, n_valid, hs, wg, wu, wd)


def _moe_out_kernel(x_ref, y1_ref, y2_ref, meta_ref, mod_ref, g3_ref, o_ref, *, tiles_per_mod, mod_base):
    i = pl.program_id(0)
    _, _, gate_f = _mod_rows(mod_ref, i, tiles_per_mod, mod_base, 3)
    meta = meta_ref[...]
    y = meta[:, 2:3] * _unpack_pairs(y1_ref[...]) + meta[:, 3:4] * _unpack_pairs(y2_ref[...])
    o_ref[...] = x_ref[...] + gate_f * _rms(y, g3_ref[...])


def _moe_combine(x, yg, meta, mod, g3, *, rows_per_mod, mod_base):
    rows, d = x.shape
    tm = min(512, rows)
    nt = rows // tm
    kern = functools.partial(_moe_out_kernel, tiles_per_mod=max(rows_per_mod // tm, 1), mod_base=mod_base)
    return pl.pallas_call(
        kern,
        grid=(nt,),
        in_specs=[
            pl.BlockSpec((tm, d), lambda i: (i, 0)),
            pl.BlockSpec((tm, d // 2), lambda i: (i, 0)),
            pl.BlockSpec((tm, d // 2), lambda i: (nt + i, 0)),
            pl.BlockSpec((tm, MOE_META_W), lambda i: (i, 0)),
            pl.BlockSpec(mod.shape, lambda i: (0, 0)),
            pl.BlockSpec((1, d), lambda i: (0, 0)),
        ],
        out_specs=pl.BlockSpec((tm, d), lambda i: (i, 0)),
        out_shape=jax.ShapeDtypeStruct((rows, d), F32),
        compiler_params=_cparams("arbitrary"),
        name="moe_combine",
    )(x, yg, yg, meta, mod, g3.reshape(1, d))


def _moe_sparse(x, routed, mod, g3, wg, wu, wd, *, rows_per_mod, mod_base):
    h, meta, plan, counts = routed
    rows = x.shape[0]
    pos, n_slots, tile_e, n_valid = _moe_plan(plan, counts, rows)
    hs = _sc_scatter(h, pos, n_slots)
    ys = _moe_grouped(hs, tile_e, n_valid, wg, wu, wd)
    yg = _sc_gather(ys, pos)
    return _moe_combine(x, yg, meta, mod, g3, rows_per_mod=rows_per_mod, mod_base=mod_base)


def _cast_kernel(w_ref, o_ref, *, scale):
    w = w_ref[...]
    o_ref[...] = (w if scale == 1.0 else w * scale).astype(BF16)


def _cast_bf16(w_stack, layer, scale=1.0):
    squeeze = w_stack.ndim == 3
    w4 = w_stack[:, None] if squeeze else w_stack
    _, n_e, k, n = w4.shape
    bk = min(k, 256)
    out = pl.pallas_call(
        functools.partial(_cast_kernel, scale=scale),
        grid=(n_e, k // bk),
        in_specs=[pl.BlockSpec((None, None, bk, n), lambda e, i: (layer, e, i, 0))],
        out_specs=pl.BlockSpec((None, bk, n), lambda e, i: (e, i, 0)),
        out_shape=jax.ShapeDtypeStruct((n_e, k, n), BF16),
        compiler_params=_cparams("arbitrary", "arbitrary"),
        name="cast_weights",
    )(w4)
    return out[0] if squeeze else out


def _permute_w_in(w_in_stack, layer):
    _, k, n = w_in_stack.shape
    n_blocks = n // BRANCH_W
    shift = 9
    n_gate_blocks = N_BRANCH * D_MODEL // BRANCH_W

    per_step = 5
    assert n_blocks % per_step == 0

    def permute_kernel(*refs):
        o_ref = refs[-1]
        for s, w_ref in enumerate(refs[:-1]):
            scale = jnp.where(pl.program_id(0) * per_step + s < n_gate_blocks, 0.5, 1.0)
            o_ref[:, s * BRANCH_W:(s + 1) * BRANCH_W] = (w_ref[...] * scale).astype(BF16)

    def src(s):
        return pl.BlockSpec((None, k, BRANCH_W), lambda j: (layer, 0, (j * per_step + s + shift) % n_blocks))

    return pl.pallas_call(
        permute_kernel,
        grid=(n_blocks // per_step,),
        in_specs=[src(s) for s in range(per_step)],
        out_specs=pl.BlockSpec((k, per_step * BRANCH_W), lambda j: (0, j)),
        out_shape=jax.ShapeDtypeStruct((k, n), BF16),
        compiler_params=_cparams("arbitrary"),
        name="cast_permute_w_in",
    )(*([w_in_stack] * per_step))


def kernel(x, c, ctx, c_ctx, w_mod, b_mod, norm_g, w_in, s5_a_re, s5_a_im, s5_log_dt, s5_b_re, s5_b_im, s5_c_re, s5_c_im, s5_d, s5_w_glu, s5_b_glu, ret_decay, ret_gn, na_rpb, w_branch, w_out, ffn_w_gate, ffn_w_up, ffn_w_down, moe_w_router, moe_b_router, moe_w_gate, moe_w_up, moe_w_down):
    batch, seq_len, d = x.shape
    ctx_len = ctx.shape[1]
    depth = w_mod.shape[0]
    cond = jnp.concatenate([c, c_ctx[None, :]], axis=0)
    mod_all = _modulation(cond, w_mod, b_mod)
    rope = _rope_tables(seq_len)
    lane_h = np.repeat(np.arange(RET_HEADS), RET_DIM)
    avg = jnp.asarray((lane_h[:, None] == lane_h[None, :]).astype(np.float32) / RET_DIM, BF16)

    xl = x.reshape(batch * seq_len, d)
    xc = ctx.reshape(batch * ctx_len, d)
    lat = dict(rows_per_mod=seq_len, mod_base=0)
    cxt = dict(rows_per_mod=batch * ctx_len, mod_base=batch)

    s5_tabs = jax.vmap(functools.partial(_s5_tables, batch=batch))(
        s5_a_re, s5_a_im, s5_log_dt, s5_b_re, s5_b_im, s5_c_re, s5_c_im, s5_d)
    ret_tabs = jax.vmap(_ret_tables)(ret_decay)
    ret_masks = _ret_masks()
    na_bias = jax.vmap(_na_tables)(na_rpb)
    na_hmask = _na_head_mask()

    for layer in range(depth):
        last = layer == depth - 1
        need_ctx = not last
        mod = mod_all[layer]
        ng = norm_g[layer]
        w_in_bf = _permute_w_in(w_in, layer)
        lw = dict(w_glu=s5_w_glu[layer].astype(BF16), b_glu=s5_b_glu[layer].reshape(1, BRANCH_W).astype(F32),
                  ret_gn=ret_gn[layer].reshape(1, BRANCH_W).astype(F32), avg=avg,
                  w_branch=_cast_bf16(w_branch, layer, 0.5), w_out=_cast_bf16(w_out, layer))

        proj_l, f_l, *s_in_l = _in_proj(xl, mod, ng[0], w_in_bf, **lat)
        proj_c, f_c, *s_in_c = _in_proj(xc, mod, ng[0], w_in_bf, **cxt)

        a_l = _fourier_latent(f_l, batch, seq_len)
        s_l, s_c = _s5_mixer(s_in_l, s_in_c, s5_tabs, layer, batch)
        r_l, r_c = _retention(proj_l, proj_c, ret_tabs, layer, ret_masks, rope, batch, seq_len, ctx_len)
        n_l, n_c = _neighborhood(proj_l, proj_c, na_bias, layer, na_hmask, batch, seq_len, ctx_len, need_ctx)

        xl = _merge(xl, mod, ng[1], proj_l, a_l, s_l, r_l, n_l, lw, **lat)
        if need_ctx:
            a_c = _fourier_ctx(f_c, batch, ctx_len)
            xc = _merge(xc, mod, ng[1], proj_c, a_c, s_c, r_c, n_c, lw, **cxt)

        i = layer // 2
        if layer % 2 == 0:
            wg, wu, wd = _cast_bf16(ffn_w_gate, i), _cast_bf16(ffn_w_up, i), _cast_bf16(ffn_w_down, i)
            xl = _ffn_dense(xl, mod, ng[2], ng[3], wg, wu, wd, **lat)
            if need_ctx:
                xc = _ffn_dense(xc, mod, ng[2], ng[3], wg, wu, wd, **cxt)
        else:
            wg, wu, wd = moe_w_gate[i], moe_w_up[i], moe_w_down[i]
            routed = _router(xl, mod, ng[2], moe_w_router[i], moe_b_router[i], **lat)
            xl = _moe_sparse(xl, routed, mod, ng[3], wg, wu, wd, **lat)
            if need_ctx:
                routed_c = _router(xc, mod, ng[2], moe_w_router[i], moe_b_router[i], **cxt)
                xc = _moe_sparse(xc, routed_c, mod, ng[3], wg, wu, wd, **cxt)
    return xl.reshape(batch, seq_len, d)
```

```python
import functools
import math

import numpy as np
import jax
import jax.numpy as jnp
from jax import lax
from jax.experimental import pallas as pl
from jax.experimental.pallas import tpu as pltpu
from jax.experimental.pallas import tpu_sc as plsc

F32 = jnp.float32
BF16 = jnp.bfloat16

D_MODEL = 1024
BRANCH_W = 256
N_BRANCH = 4
GRID_W = 64
FNET_GROUP_DIM = 64
S5_GROUP_CH = 16
S5_GROUPS = 16
S5_STATE = 64
S5_CHUNK = 32
S5_PAIRS = S5_GROUPS // 2
RET_HEADS = 4
RET_DIM = 64
RET_CHUNK = 128
NA_HEADS = 4
NA_DIM = 64
NA_WIN_ROWS = 8
NA_WIN_COLS = 16
NA_QROWS = 16
ROPE_BASE = 10000.0
N_EXPERTS = 8
EPS = 1e-6
FFT_N2 = 256
NEG_BIG = -1e30
VMEM_LIMIT_BYTES = 50 * 1024 * 1024
SC_CORES = 2
SC_SUBCORES = 16
SC_WORKERS = SC_CORES * SC_SUBCORES
SC_GATHER_ROWS = 64
MOE_ROW_TILE = 2048
MOE_SUB_ROWS = 512
MOE_FF_TILE = 512
MOE_META_W = 8

COL_F, COL_S, COL_RQ, COL_RK, COL_RV, COL_RG, COL_NQ, COL_NK, COL_NV = range(16, 25)
IN_W = 9 * BRANCH_W + N_BRANCH * D_MODEL
IN_TN = 1280
IN_F_TILE = (N_BRANCH * D_MODEL) // IN_TN
IN_F_OFF = N_BRANCH * D_MODEL - IN_F_TILE * IN_TN
IN_S_OFF = IN_F_OFF + BRANCH_W


def _cparams(*sem):
    return pltpu.CompilerParams(dimension_semantics=sem, vmem_limit_bytes=VMEM_LIMIT_BYTES)


def _sigmoid(v):
    return 0.5 * jnp.tanh(0.5 * v) + 0.5


def _silu(v):
    return v * _sigmoid(v)


def _gelu_tanh(v):
    return 0.5 * v * (1.0 + jnp.tanh(math.sqrt(2.0 / math.pi) * (v + 0.044715 * (v * v * v))))


def _rms(v, g):
    ms = jnp.mean(v * v, axis=-1, keepdims=True)
    return v * lax.rsqrt(ms + EPS) * g


def _split_bf16(v):
    hi = v.astype(BF16)
    lo = (v - hi.astype(F32)).astype(BF16)
    return hi, lo


def _pack_pairs(v):
    n = v.shape[1] // 2
    lo = lax.bitcast_convert_type(v[:, :n].astype(BF16).astype(F32), jnp.int32)
    hi = lax.bitcast_convert_type(v[:, n:].astype(BF16).astype(F32), jnp.int32)
    return (hi & -65536) | ((lo >> 16) & 65535)


def _unpack_pairs(w):
    lo = lax.bitcast_convert_type(w << 16, F32)
    hi = lax.bitcast_convert_type(w & -65536, F32)
    return jnp.concatenate([lo, hi], axis=-1)


def _dot(a, b):
    return jnp.dot(a, b, preferred_element_type=F32)


def _dot_nt(a, b):
    return lax.dot_general(a, b, (((1,), (1,)), ((), ())), preferred_element_type=F32)


def _dot_tn(a, b):
    return lax.dot_general(a, b, (((0,), (0,)), ((), ())), preferred_element_type=F32)


def _mod_kernel(ct_ref, w_ref, b_ref, o_ref, *, n_cond):
    ct = ct_ref[...]
    s = _silu(ct)
    w = w_ref[...]
    rows = [jnp.sum(w * s[:, r:r + 1], axis=0, keepdims=True) for r in range(n_cond)]
    rows.append(jnp.zeros((8 - n_cond, w.shape[1]), F32))
    o_ref[...] = jnp.concatenate(rows, axis=0) + b_ref[...]


def _modulation(cond, w_mod, b_mod):
    n_layers, d, n = w_mod.shape
    tn = 1536
    ct = jnp.zeros((8, d), F32).at[:cond.shape[0]].set(cond).T
    return pl.pallas_call(
        functools.partial(_mod_kernel, n_cond=cond.shape[0]),
        grid=(n_layers, n // tn),
        in_specs=[
            pl.BlockSpec((d, 8), lambda l, j: (0, 0)),
            pl.BlockSpec((None, d, tn), lambda l, j: (l, 0, j)),
            pl.BlockSpec((None, 1, tn), lambda l, j: (l, 0, j)),
        ],
        out_specs=pl.BlockSpec((None, 8, tn), lambda l, j: (l, 0, j)),
        out_shape=jax.ShapeDtypeStruct((n_layers, 8, n), F32),
        compiler_params=_cparams("arbitrary", "arbitrary"),
        name="adaln_mod",
    )(ct, w_mod, b_mod.reshape(n_layers, 1, n))


def _mod_rows(mod_ref, i, tiles_per_mod, mod_base, first):
    r = mod_base + i // tiles_per_mod
    return [mod_ref[pl.ds(r, 1), (first + k) * D_MODEL:(first + k + 1) * D_MODEL] for k in range(3)]


def _in_kernel(x_ref, mod_ref, g_ref, w_ref, proj_ref, f_ref, sa_ref, sb_ref, *, tiles_per_mod, mod_base):
    i = pl.program_id(0)
    sh, sc, _ = _mod_rows(mod_ref, i, tiles_per_mod, mod_base, 0)
    h = (_rms(x_ref[...], g_ref[...]) * (1.0 + sc) + sh).astype(BF16)
    for j in range(IN_W // IN_TN):
        res = _dot(h, w_ref[:, j * IN_TN:(j + 1) * IN_TN])
        proj_ref[:, j * IN_TN:(j + 1) * IN_TN] = res.astype(BF16)
        if j == IN_F_TILE:
            f_ref[...] = res[:, IN_F_OFF:IN_F_OFF + BRANCH_W].astype(BF16)
            sa_ref[...] = res[:, IN_S_OFF:IN_S_OFF + 128]
            sb_ref[...] = res[:, IN_S_OFF + 128:IN_S_OFF + 256]


def _in_proj(x, mod, g, w_bf, *, rows_per_mod, mod_base):
    rows, d = x.shape
    tm = math.gcd(512, rows_per_mod)
    kern = functools.partial(_in_kernel, tiles_per_mod=max(rows_per_mod // tm, 1), mod_base=mod_base)
    return pl.pallas_call(
        kern,
        grid=(rows // tm,),
        in_specs=[
            pl.BlockSpec((tm, d), lambda i: (i, 0)),
            pl.BlockSpec(mod.shape, lambda i: (0, 0)),
            pl.BlockSpec((1, d), lambda i: (0, 0)),
            pl.BlockSpec((d, IN_W), lambda i: (0, 0), pipeline_mode=pl.Buffered(1)),
        ],
        out_specs=[
            pl.BlockSpec((tm, IN_W), lambda i: (i, 0)),
            pl.BlockSpec((tm, BRANCH_W), lambda i: (i, 0)),
            pl.BlockSpec((tm, 128), lambda i: (i, 0)),
            pl.BlockSpec((tm, 128), lambda i: (i, 0)),
        ],
        out_shape=[
            jax.ShapeDtypeStruct((rows, IN_W), BF16),
            jax.ShapeDtypeStruct((rows, BRANCH_W), BF16),
            jax.ShapeDtypeStruct((rows, 128), F32),
            jax.ShapeDtypeStruct((rows, 128), F32),
        ],
        compiler_params=_cparams("arbitrary"),
        name="in_proj",
    )(x, mod, g.reshape(1, d), w_bf)


def _fft_a_kernel(x_ref, cs_ref, tc_ref, ts_ref, zr_ref, zi_ref, *, n1, n1p):
    y = _dot(cs_ref[...].astype(BF16), x_ref[...])
    yr = y[:n1]
    yi = y[n1p:n1p + n1]
    tc = tc_ref[...]
    ts = ts_ref[...]
    zr_ref[...] = (yr * tc + yi * ts).astype(BF16)
    zi_ref[...] = (yi * tc - yr * ts).astype(BF16)


def _fft_b_kernel(zr_ref, zi_ref, cs_ref, cc_ref, sc_ref, oa_ref, ob_ref, *, kb, n1, scale, has_imag):
    cs = cs_ref[...].astype(BF16)
    cc = cc_ref[...].astype(BF16)
    sc = sc_ref[...].astype(BF16)
    half = BRANCH_W // 2
    for kk in range(kb):
        a = _dot(cs, zr_ref[kk])
        if has_imag:
            b = _dot(cs, zi_ref[kk])
            xr = a[:FFT_N2] + b[FFT_N2:]
            xi = b[:FFT_N2] - a[FFT_N2:]
        else:
            xr = a[:FFT_N2]
            xi = -a[FFT_N2:]
        out = (_dot(xr.astype(BF16), cc) + _dot(xi.astype(BF16), sc)) * scale
        k1 = pl.program_id(1) * kb + kk
        oa_ref[pl.ds(k1, FFT_N2, stride=n1), :] = out[:, :half]
        ob_ref[pl.ds(k1, FFT_N2, stride=n1), :] = out[:, half:]


def _dft_tables(n):
    k = np.arange(n)
    ang = 2.0 * np.pi * ((k[:, None] * k[None, :]) % n) / n
    return np.cos(ang), np.sin(ang)


def _fft_b_call(zr, zi, n1, batch, seq_len, has_imag):
    c2, s2 = _dft_tables(FFT_N2)
    cs2 = jnp.asarray(np.concatenate([c2, s2], axis=0), F32)
    c64, s64 = _dft_tables(FNET_GROUP_DIM)
    eye = np.eye(BRANCH_W // FNET_GROUP_DIM)
    cc = jnp.asarray(np.kron(eye, c64), F32)
    sc = jnp.asarray(np.kron(eye, s64), F32)
    kb = min(8, n1)
    scale = 1.0 / math.sqrt(seq_len * FNET_GROUP_DIM)
    kern = functools.partial(_fft_b_kernel, kb=kb, n1=n1, scale=scale, has_imag=has_imag)
    zspec = pl.BlockSpec((None, kb, FFT_N2, BRANCH_W), lambda b, i: (b, i, 0, 0))
    half = pl.BlockSpec((seq_len, BRANCH_W // 2), lambda b, i: (b, 0))
    return pl.pallas_call(
        kern,
        grid=(batch, n1 // kb),
        in_specs=[
            zspec, zspec,
            pl.BlockSpec((2 * FFT_N2, FFT_N2), lambda b, i: (0, 0)),
            pl.BlockSpec((BRANCH_W, BRANCH_W), lambda b, i: (0, 0)),
            pl.BlockSpec((BRANCH_W, BRANCH_W), lambda b, i: (0, 0)),
        ],
        out_specs=[half, half],
        out_shape=[jax.ShapeDtypeStruct((batch * seq_len, BRANCH_W // 2), F32)] * 2,
        compiler_params=_cparams("arbitrary", "arbitrary"),
        name="fourier_stage_b",
    )(zr, zi, cs2, cc, sc)


def _fourier_latent(f, batch, seq_len):
    n1 = seq_len // FFT_N2
    wide = FFT_N2 * BRANCH_W
    c1, s1 = _dft_tables(n1)
    n1p = max(n1, 8)
    cs1 = np.zeros((2 * n1p, n1))
    cs1[:n1] = c1
    cs1[n1p:n1p + n1] = -s1
    k1 = np.arange(n1)[:, None]
    l2 = np.arange(FFT_N2)[None, :]
    tw = 2.0 * np.pi * (k1 * l2) / seq_len
    tc = jnp.asarray(np.repeat(np.cos(tw), BRANCH_W, axis=1), F32)
    ts = jnp.asarray(np.repeat(np.sin(tw), BRANCH_W, axis=1), F32)
    cw = min(8192, wide)
    xv = f.reshape(batch, n1, wide)
    spec = pl.BlockSpec((None, n1, cw), lambda b, j: (b, 0, j))
    tspec = pl.BlockSpec((n1, cw), lambda b, j: (0, j))
    zr, zi = pl.pallas_call(
        functools.partial(_fft_a_kernel, n1=n1, n1p=n1p),
        grid=(batch, wide // cw),
        in_specs=[spec, pl.BlockSpec((2 * n1p, n1), lambda b, j: (0, 0)), tspec, tspec],
        out_specs=[spec, spec],
        out_shape=[jax.ShapeDtypeStruct((batch, n1, wide), BF16)] * 2,
        compiler_params=_cparams("arbitrary", "arbitrary"),
        name="fourier_stage_a",
    )(xv, jnp.asarray(cs1, F32), tc, ts)
    zr = zr.reshape(batch, n1, FFT_N2, BRANCH_W)
    zi = zi.reshape(batch, n1, FFT_N2, BRANCH_W)
    return _fft_b_call(zr, zi, n1, batch, seq_len, True)


def _fourier_ctx(f, batch, ctx_len):
    assert ctx_len == FFT_N2
    z = f.reshape(batch, 1, FFT_N2, BRANCH_W)
    return _fft_b_call(z, z, 1, batch, ctx_len, False)


def _s5_tables(a_re, a_im, log_dt, b_re, b_im, c_re, c_im, d_skip, batch):
    t = S5_CHUNK
    g, p, hc = S5_GROUPS, S5_STATE, S5_GROUP_CH
    lam = lax.complex(a_re.astype(F32), a_im.astype(F32))
    dt = jnp.exp(log_dt.astype(F32))[..., None]
    ks = jnp.arange(t + 1, dtype=F32)
    apow = jnp.exp((lam * dt)[..., None] * ks)
    a_bar = apow[..., 1]
    b_bar = ((a_bar - 1.0) / lam)[..., None] * lax.complex(b_re.astype(F32), b_im.astype(F32))
    cm = lax.complex(c_re.astype(F32), c_im.astype(F32))
    lagv = jnp.arange(-(t - 1), t, dtype=F32)
    ldt = lam * dt
    pw_f = jnp.where(lagv >= 0, jnp.exp(ldt[0][..., None] * jnp.maximum(lagv, 0.0)), 0.0)
    pw_b = jnp.where(lagv <= 0, jnp.exp(ldt[1][..., None] * jnp.maximum(-lagv, 0.0)), 0.0)
    kfull = jnp.real(jnp.einsum('ghp,gpl,gpj->gjlh',
                                jnp.concatenate([cm[0], cm[1]], axis=-1),
                                jnp.concatenate([pw_f, pw_b], axis=1),
                                jnp.concatenate([b_bar[0], b_bar[1]], axis=1),
                                precision=lax.Precision.HIGHEST))
    kp = kfull.reshape(S5_PAIRS, 2, hc, 2 * t - 1, hc)
    blk = [kp[:, gi] for gi in range(2)]
    zb = jnp.zeros_like(blk[0])
    strip = jnp.concatenate([jnp.stack([blk[0], zb], axis=3), jnp.stack([zb, blk[1]], axis=3)], axis=1)
    strip = strip.reshape(S5_PAIRS, 2 * hc, (2 * t - 1) * 2 * hc)
    strip = jnp.pad(strip, ((0, 0), (0, 0), (0, 2 * hc)))

    wf = jnp.einsum('gpj,gph->gjhp', apow[0][..., t - 1::-1][..., :t], b_bar[0])
    wb = jnp.einsum('gpj,gph->gjhp', apow[1][..., :t], b_bar[1])
    kinds = [jnp.real(wf), jnp.imag(wf), jnp.real(wb), jnp.imag(wb)]

    def we_pair(kd):
        k5 = kd.reshape(S5_PAIRS, 2, t, hc, p)
        z = jnp.zeros_like(k5[:, 0])
        rows = jnp.stack([jnp.concatenate([k5[:, 0], z], axis=-1), jnp.concatenate([z, k5[:, 1]], axis=-1)], axis=2)
        return rows.reshape(S5_PAIRS, 2 * t * hc, 2 * p)

    we = jnp.concatenate([we_pair(kd) for kd in kinds], axis=-1).astype(BF16)

    vf = jnp.einsum('ghp,gpt->gpth', cm[0], apow[0][..., 1:t + 1])
    vb = jnp.einsum('ghp,gpt->gpth', cm[1], apow[1][..., t:0:-1])
    vkinds = [jnp.real(vf), -jnp.imag(vf), jnp.real(vb), -jnp.imag(vb)]

    def v_pair(kd):
        k5 = kd.reshape(S5_PAIRS, 2, p, t, hc)
        z = jnp.zeros_like(k5[:, 0])
        rows = jnp.concatenate([jnp.stack([k5[:, 0], z], axis=3), jnp.stack([z, k5[:, 1]], axis=3)], axis=1)
        return rows.reshape(S5_PAIRS, 2 * p, 2 * t * hc)

    v1 = jnp.concatenate([v_pair(kd) for kd in vkinds], axis=1)
    v = jnp.concatenate([v1, v1], axis=1).astype(BF16)

    def lanes(z):
        return jnp.tile(z.reshape(1, g * p), (1, batch))

    at = apow[..., t]
    a_tab = jnp.concatenate([lanes(jnp.real(at[0])), lanes(jnp.imag(at[0])),
                             lanes(jnp.real(at[1])), lanes(jnp.imag(at[1]))], axis=0)
    dvec = jnp.tile(d_skip.astype(F32).reshape(S5_PAIRS, 1, 2 * hc), (1, t, 1)).reshape(S5_PAIRS, 1, 2 * t * hc)
    return dict(strip=strip, we=we, v=v, a_tab=a_tab, dvec=dvec)


def _s5_pack_kernel(xa_ref, xb_ref, u_ref, *, n_chunks):
    per_half = S5_PAIRS // 2
    for half, x_ref in enumerate((xa_ref, xb_ref)):
        rows = [x_ref[pl.ds(tau, n_chunks, stride=S5_CHUNK), :] for tau in range(S5_CHUNK)]
        for qq in range(per_half):
            pieces = [r[:, qq * 32:(qq + 1) * 32] for r in rows]
            u_ref[half * per_half + qq] = jnp.concatenate(pieces, axis=-1).astype(BF16)


def _s5_unpack_kernel(y_ref, oa_ref, ob_ref, *, n_chunks):
    per_half = S5_PAIRS // 2
    for half, o_ref in enumerate((oa_ref, ob_ref)):
        ys = [y_ref[half * per_half + qq].astype(F32) for qq in range(per_half)]
        for t in range(S5_CHUNK):
            pieces = [y[:, t * 32:(t + 1) * 32] for y in ys]
            o_ref[pl.ds(t, n_chunks, stride=S5_CHUNK), :] = jnp.concatenate(pieces, axis=-1)


def _s5_pack(sa, sb, batch):
    n_chunks = sa.shape[0] // batch // S5_CHUNK
    rows = n_chunks * S5_CHUNK
    cols = 2 * S5_CHUNK * S5_GROUP_CH
    half = pl.BlockSpec((rows, 128), lambda b: (b, 0))
    return pl.pallas_call(
        functools.partial(_s5_pack_kernel, n_chunks=n_chunks),
        grid=(batch,),
        in_specs=[half, half],
        out_specs=pl.BlockSpec((S5_PAIRS, None, n_chunks, cols), lambda b: (0, b, 0, 0)),
        out_shape=jax.ShapeDtypeStruct((S5_PAIRS, batch, n_chunks, cols), BF16),
        compiler_params=_cparams("arbitrary"),
        name="s5_pack",
    )(sa, sb)


def _s5_unpack(y, batch):
    n_chunks = y.shape[2]
    rows = n_chunks * S5_CHUNK
    cols = y.shape[3]
    half = pl.BlockSpec((rows, 128), lambda b: (b, 0))
    return pl.pallas_call(
        functools.partial(_s5_unpack_kernel, n_chunks=n_chunks),
        grid=(batch,),
        in_specs=[pl.BlockSpec((S5_PAIRS, None, n_chunks, cols), lambda b: (0, b, 0, 0))],
        out_specs=[half, half],
        out_shape=[jax.ShapeDtypeStruct((batch * rows, 128), F32)] * 2,
        compiler_params=_cparams("arbitrary"),
        name="s5_unpack",
    )(y)


def _s5_e_kernel(ul_ref, uc_ref, we_ref, ref_, imf_, reb_, imb_):
    u = jnp.concatenate([ul_ref[...], uc_ref[...]], axis=0)
    e = _dot(u, we_ref[...])
    ref_[...] = e[:, 0:128]
    imf_[...] = e[:, 128:256]
    reb_[...] = e[:, 256:384]
    imb_[...] = e[:, 384:512]


def _s5_scan_kernel(a_ref, ref_, imf_, reb_, imb_, prf, pif, prb, pib, *, n_rows, n_ctx):
    afr = a_ref[0:1, :]
    afi = a_ref[1:2, :]
    abr = a_ref[2:3, :]
    abi = a_ref[3:4, :]
    zero = jnp.zeros_like(afr)

    n_lat = n_rows - n_ctx

    def body(s, carry):
        sfr, sfi, sbr, sbi = carry
        nf = jnp.where(s < n_ctx, n_lat + s, s - n_ctx)
        nb = n_rows - 1 - s
        prf[pl.ds(nf, 1), :] = sfr
        pif[pl.ds(nf, 1), :] = sfi
        prb[pl.ds(nb, 1), :] = sbr
        pib[pl.ds(nb, 1), :] = sbi
        efr = ref_[pl.ds(nf, 1), :]
        efi = imf_[pl.ds(nf, 1), :]
        ebr = reb_[pl.ds(nb, 1), :]
        ebi = imb_[pl.ds(nb, 1), :]
        nfr = afr * sfr - afi * sfi + efr
        nfi = afr * sfi + afi * sfr + efi
        nbr = abr * sbr - abi * sbi + ebr
        nbi = abr * sbi + abi * sbr + ebi
        return nfr, nfi, nbr, nbi

    lax.fori_loop(0, n_rows, body, (zero, zero, zero, zero))


def _s5_y_kernel(ul_ref, uc_ref, strip_ref, v_ref, d_ref, prf, pif, prb, pib, yl_ref, yc_ref, m_scr):
    width = 2 * S5_GROUP_CH
    cols = S5_CHUNK * width
    n_lat = yl_ref.shape[0]

    @pl.when(pl.program_id(1) == 0)
    def _():
        strip = strip_ref[...]
        for j in range(S5_CHUNK):
            off = (S5_CHUNK - 1 - j) * width
            win = strip if off == 0 else pltpu.roll(strip, 2 * cols - off, axis=1)
            m_scr[j * width:(j + 1) * width, :] = win[:, :cols].astype(BF16)

    u = jnp.concatenate([ul_ref[...], uc_ref[...]], axis=0)
    y_intra = _dot(u, m_scr[...])
    pcat = jnp.concatenate([prf[...], pif[...], prb[...], pib[...]], axis=-1)
    hi, lo = _split_bf16(pcat)
    y_cross = _dot(jnp.concatenate([hi, lo], axis=-1), v_ref[...])
    y = y_intra + y_cross + d_ref[...] * u.astype(F32)
    yl_ref[...] = y[:n_lat].astype(BF16)
    yc_ref[...] = y[n_lat:].astype(BF16)


def _s5_core(ul, uc, tabs, layer, batch):
    n_lat, n_ctx = ul.shape[2], uc.shape[2]
    n_rows = n_lat + n_ctx
    width = batch * S5_PAIRS * 128
    cols = 2 * S5_CHUNK * S5_GROUP_CH
    ul_spec = pl.BlockSpec((None, None, n_lat, cols), lambda q, b: (q, b, 0, 0))
    uc_spec = pl.BlockSpec((None, None, n_ctx, cols), lambda q, b: (q, b, 0, 0))
    st_spec = pl.BlockSpec((n_rows, 128), lambda q, b: (0, b * S5_PAIRS + q))
    st_shape = jax.ShapeDtypeStruct((n_rows, width), F32)
    e4 = pl.pallas_call(
        _s5_e_kernel,
        grid=(S5_PAIRS, batch),
        in_specs=[ul_spec, uc_spec, pl.BlockSpec((None, None, cols, 512), lambda q, b: (layer, q, 0, 0))],
        out_specs=[st_spec] * 4,
        out_shape=[st_shape] * 4,
        compiler_params=_cparams("arbitrary", "arbitrary"),
        name="s5_chunk_states",
    )(ul, uc, tabs['we'])
    p4 = pl.pallas_call(
        functools.partial(_s5_scan_kernel, n_rows=n_rows, n_ctx=n_ctx),
        out_shape=[st_shape] * 4,
        compiler_params=pltpu.CompilerParams(vmem_limit_bytes=VMEM_LIMIT_BYTES),
        name="s5_state_scan",
    )(tabs['a_tab'][layer], *e4)
    y = pl.pallas_call(
        _s5_y_kernel,
        grid=(S5_PAIRS, batch),
        in_specs=[
            ul_spec, uc_spec,
            pl.BlockSpec((None, None, 2 * S5_GROUP_CH, 2 * cols), lambda q, b: (layer, q, 0, 0)),
            pl.BlockSpec((None, None, cols, cols), lambda q, b: (layer, q, 0, 0)),
            pl.BlockSpec((None, None, 1, cols), lambda q, b: (layer, q, 0, 0)),
            st_spec, st_spec, st_spec, st_spec,
        ],
        out_specs=[ul_spec, uc_spec],
        out_shape=[
            jax.ShapeDtypeStruct((S5_PAIRS, batch, n_lat, cols), BF16),
            jax.ShapeDtypeStruct((S5_PAIRS, batch, n_ctx, cols), BF16),
        ],
        scratch_shapes=[pltpu.VMEM((cols, cols), BF16)],
        compiler_params=_cparams("arbitrary", "arbitrary"),
        name="s5_outputs",
    )(ul, uc, tabs['strip'], tabs['v'], tabs['dvec'], *p4)
    return y


def _s5_mixer(s_lat, s_ctx, tabs, layer, batch):
    ul = _s5_pack(*s_lat, batch)
    uc = _s5_pack(*s_ctx, batch)
    yl, yc = _s5_core(ul, uc, tabs, layer, batch)
    return _s5_unpack(yl, batch), _s5_unpack(yc, batch)


def _ret_tables(ret_decay):
    c = RET_CHUNK
    lg = jax.nn.log_sigmoid(ret_decay.astype(F32))
    lane_h = np.repeat(np.arange(RET_HEADS), RET_DIM)
    lgl = jnp.repeat(lg, RET_DIM, axis=1)
    pos = jnp.arange(c, dtype=F32)[:, None]
    qd = jnp.stack([jnp.exp((pos + 1.0) * lgl[0][None]), jnp.exp((c - pos) * lgl[1][None])])
    kd = jnp.stack([jnp.exp((c - 1.0 - pos) * lgl[0][None]), jnp.exp(pos * lgl[1][None])])
    bmask = jnp.asarray((lane_h[:, None] == lane_h[None, :]).astype(np.float32))
    cd = jnp.exp(c * lgl)[:, :, None] * bmask[None]
    diff = pos - pos.T
    dm = []
    for h in range(RET_HEADS):
        fw = jnp.where(diff >= 0, jnp.exp(jnp.maximum(diff, 0.0) * lg[0, h]), 0.0)
        bw = jnp.where(diff <= 0, jnp.exp(jnp.maximum(-diff, 0.0) * lg[1, h]), 0.0)
        dm.append(fw + bw)
    dm = jnp.concatenate(dm, axis=0)
    return dict(qd=qd, kd=kd, cd=cd, dm=dm)


def _ret_masks():
    lane_h = np.repeat(np.arange(RET_HEADS), RET_DIM)
    bmask = (lane_h[:, None] == lane_h[None, :]).astype(np.float32)
    hmask = (np.arange(RET_HEADS)[:, None] == lane_h[None, :]).astype(np.float32)
    return jnp.asarray(bmask), jnp.asarray(hmask)


def _rope_tables(n_tokens):
    t = np.arange(n_tokens)
    row = (t // GRID_W).astype(np.float64)
    col = (t % GRID_W).astype(np.float64)
    n_freq = RET_DIM // 4
    inv_freq = 1.0 / (ROPE_BASE ** (np.arange(n_freq, dtype=np.float64) / n_freq))
    ang = np.concatenate([row[:, None] * inv_freq, col[:, None] * inv_freq], axis=-1)
    cos = np.cos(ang)
    sin = np.sin(ang)
    cos_t = np.tile(np.concatenate([cos, cos], axis=-1), (1, RET_HEADS))
    sin_t = np.tile(np.concatenate([-sin, sin], axis=-1), (1, RET_HEADS))
    half = RET_DIM // 2
    perm = np.arange(BRANCH_W) ^ half
    swap = np.zeros((BRANCH_W, BRANCH_W), np.float32)
    swap[perm, np.arange(BRANCH_W)] = 1.0
    return jnp.asarray(cos_t, F32), jnp.asarray(sin_t, F32), jnp.asarray(swap, BF16)


def _ret_chunk(q, k, v, s, qd, kd, cd, bmask, dm, hmask, with_intra):
    cross = _dot((q * qd).astype(BF16), s.astype(BF16))
    s_new = cd * s + bmask * _dot_tn((k * kd).astype(BF16), v)
    if not with_intra:
        return cross, s_new
    qb = q.astype(BF16)
    kb = k.astype(BF16)
    qs = jnp.concatenate([qb * hmask[h:h + 1].astype(BF16) for h in range(RET_HEADS)], axis=0)
    scores = _dot_nt(qs, kb) * dm
    ov = _dot(scores.astype(BF16), v)
    c = q.shape[0]
    inner = ov[0:c] * hmask[0:1]
    for h in range(1, RET_HEADS):
        inner = inner + ov[h * c:(h + 1) * c] * hmask[h:h + 1]
    return inner + cross, s_new


def _ret_kernel(qf_ref, kf_ref, vf_ref, qb_ref, kb_ref, vb_ref, qc_ref, kc_ref, vc_ref,
                cosf_ref, sinf_ref, cosb_ref, sinb_ref, swap_ref,
                qd_ref, kd_ref, cd_ref, bm_ref, dm_ref, hm_ref,
                of_ref, ob_ref, ocf_ref, ocb_ref, sf_scr, sb_scr, *, n_chunks, n_ctx_chunks):
    i = pl.program_id(1)
    c = RET_CHUNK
    k_scale = RET_DIM ** -0.5
    bmask = bm_ref[...]
    dm = dm_ref[...]
    hmask = hm_ref[...]
    tabs = [(qd_ref[d], kd_ref[d], cd_ref[d]) for d in range(2)]

    @pl.when(i == 0)
    def _():
        for d, oc_ref, s_scr in ((0, ocf_ref, sf_scr), (1, ocb_ref, sb_scr)):
            qd, kd, cd = tabs[d]
            s = jnp.zeros((BRANCH_W, BRANCH_W), F32)
            order = range(n_ctx_chunks) if d == 0 else range(n_ctx_chunks - 1, -1, -1)
            for cc in order:
                sl = slice(cc * c, (cc + 1) * c)
                o, s = _ret_chunk(qc_ref[sl, :].astype(F32), kc_ref[sl, :].astype(F32) * k_scale, vc_ref[sl, :],
                                  s, qd, kd, cd, bmask, dm, hmask, d == 0)
                oc_ref[sl, :] = o
            s_scr[...] = s

    swap = swap_ref[...]

    def rope(x_ref, cos_ref, sin_ref):
        xb = x_ref[...]
        return xb.astype(F32) * cos_ref[...] + _dot(xb, swap) * sin_ref[...]

    q_f = rope(qf_ref, cosf_ref, sinf_ref)
    k_f = rope(kf_ref, cosf_ref, sinf_ref) * k_scale
    q_b = rope(qb_ref, cosb_ref, sinb_ref)
    k_b = rope(kb_ref, cosb_ref, sinb_ref) * k_scale
    sf = sf_scr[...]
    sb = sb_scr[...]
    for step in range(n_chunks):
        sl = slice(step * c, (step + 1) * c)
        o, sf = _ret_chunk(q_f[sl], k_f[sl], vf_ref[sl, :], sf, *tabs[0], bmask, dm, hmask, True)
        of_ref[sl, :] = o
        cb = n_chunks - 1 - step
        sl = slice(cb * c, (cb + 1) * c)
        o, sb = _ret_chunk(q_b[sl], k_b[sl], vb_ref[sl, :], sb, *tabs[1], bmask, dm, hmask, False)
        ob_ref[sl, :] = o
    sf_scr[...] = sf
    sb_scr[...] = sb


def _retention(proj_l, proj_c, tabs, layer, masks, rope, batch, seq_len, ctx_len):
    n_chunks = 4
    blk = n_chunks * RET_CHUNK
    nblk = seq_len // blk
    cos_t, sin_t, swap = rope

    def lat(col, back):
        if back:
            return pl.BlockSpec((blk, BRANCH_W), lambda b, i: (b * nblk + nblk - 1 - i, col))
        return pl.BlockSpec((blk, BRANCH_W), lambda b, i: (b * nblk + i, col))

    def ctx(col):
        return pl.BlockSpec((ctx_len, BRANCH_W), lambda b, i: (b, col))

    def const(shape):
        return pl.BlockSpec(shape, lambda b, i: (0,) * len(shape))

    def per_layer(shape):
        return pl.BlockSpec((None,) + shape, lambda b, i: (layer,) + (0,) * len(shape))

    tab_f = pl.BlockSpec((blk, BRANCH_W), lambda b, i: (i, 0))
    tab_b = pl.BlockSpec((blk, BRANCH_W), lambda b, i: (nblk - 1 - i, 0))
    kern = functools.partial(_ret_kernel, n_chunks=n_chunks, n_ctx_chunks=ctx_len // RET_CHUNK)
    c = RET_CHUNK
    ctx_out = pl.BlockSpec((ctx_len, BRANCH_W), lambda b, i: (b, 0))
    o_f, o_b, oc_f, oc_b = pl.pallas_call(
        kern,
        grid=(batch, nblk),
        in_specs=[
            lat(COL_RQ, False), lat(COL_RK, False), lat(COL_RV, False),
            lat(COL_RQ, True), lat(COL_RK, True), lat(COL_RV, True),
            ctx(COL_RQ), ctx(COL_RK), ctx(COL_RV),
            tab_f, tab_f, tab_b, tab_b, const((BRANCH_W, BRANCH_W)),
            per_layer((2, c, BRANCH_W)), per_layer((2, c, BRANCH_W)), per_layer((2, BRANCH_W, BRANCH_W)),
            const((BRANCH_W, BRANCH_W)), per_layer((RET_HEADS * c, c)), const((RET_HEADS, BRANCH_W)),
        ],
        out_specs=[lat(0, False), lat(0, True), ctx_out, ctx_out],
        out_shape=[
            jax.ShapeDtypeStruct((batch * seq_len, BRANCH_W), F32),
            jax.ShapeDtypeStruct((batch * seq_len, BRANCH_W), F32),
            jax.ShapeDtypeStruct((batch * ctx_len, BRANCH_W), F32),
            jax.ShapeDtypeStruct((batch * ctx_len, BRANCH_W), F32),
        ],
        scratch_shapes=[pltpu.VMEM((BRANCH_W, BRANCH_W), F32), pltpu.VMEM((BRANCH_W, BRANCH_W), F32)],
        compiler_params=_cparams("arbitrary", "arbitrary"),
        name="retention",
    )(proj_l, proj_l, proj_l, proj_l, proj_l, proj_l, proj_c, proj_c, proj_c,
      cos_t, sin_t, cos_t, sin_t, swap,
      tabs['qd'], tabs['kd'], tabs['cd'], masks[0], tabs['dm'], masks[1])
    return (o_f, o_b), (oc_f, oc_b)


def _na_tables(rpb):
    kr, kw = NA_WIN_ROWS, NA_WIN_COLS
    col = np.arange(GRID_W)
    col_start = np.clip(col - kw // 2, 0, GRID_W - kw)
    in_win = (col[None, :] >= col_start[:, None]) & (col[None, :] < col_start[:, None] + kw)
    dc = np.clip(col[None, :] - col[:, None], -(kw - 1), kw - 1) + (kw - 1)
    pick_c = (dc[:, :, None] == np.arange(2 * kw - 1)[None, None, :]).astype(np.float32)
    by = jnp.einsum('hrc,qkc->hqrk', rpb.astype(F32), jnp.asarray(pick_c), precision=lax.Precision.HIGHEST)
    by = jnp.where(jnp.asarray(in_win)[None, :, None, :], by, NEG_BIG)
    bias = jnp.stack([by[:, :, v:v + kr, :] for v in range(kr)], axis=0)
    return bias.reshape(kr, NA_HEADS * GRID_W, kr * GRID_W)


def _na_head_mask():
    lane_h = np.repeat(np.arange(NA_HEADS), NA_DIM)
    hmask = (np.arange(NA_HEADS)[:, None] == lane_h[None, :]).astype(np.float32)
    return jnp.asarray(hmask, F32)


def _attend(qs, keys, vals, bias, kc, vc):
    s_ctx = _dot_nt(qs, kc)
    m = jnp.max(s_ctx, axis=-1, keepdims=True)
    if keys is not None:
        s_band = _dot_nt(qs, keys) + bias
        m = jnp.maximum(m, jnp.max(s_band, axis=-1, keepdims=True))
        p_band = jnp.exp(s_band - m)
    p_ctx = jnp.exp(s_ctx - m)
    l = jnp.sum(p_ctx, axis=-1, keepdims=True)
    o = _dot(p_ctx.astype(BF16), vc)
    if keys is not None:
        l = l + jnp.sum(p_band, axis=-1, keepdims=True)
        o = o + _dot(p_band.astype(BF16), vals)
    return o / l


def _stack_heads(q, hmask_scaled):
    return jnp.concatenate([q * hmask_scaled[h:h + 1] for h in range(NA_HEADS)], axis=0)


def _unstack_heads(o, hmask, n):
    out = o[0:n] * hmask[0:1]
    for h in range(1, NA_HEADS):
        out = out + o[h * n:(h + 1) * n] * hmask[h:h + 1]
    return out


def _na_kernel(q_ref, k_ref, v_ref, kc_ref, vc_ref, bias_ref, hm_ref, o_ref, *, n_grid_rows):
    i = pl.program_id(1)
    hmask = hm_ref[...]
    hms = (hmask * (NA_DIM ** -0.5)).astype(BF16)
    kc = kc_ref[...]
    vc = vc_ref[...]
    band = NA_WIN_ROWS * GRID_W
    for rr in range(NA_QROWS):
        r = i * NA_QROWS + rr
        rs = jnp.clip(r - NA_WIN_ROWS // 2, 0, n_grid_rows - NA_WIN_ROWS)
        var = rs - r + (NA_WIN_ROWS - 1)
        start = pl.multiple_of(rs * GRID_W, GRID_W)
        keys = k_ref[pl.ds(start, band), :]
        vals = v_ref[pl.ds(start, band), :]
        qs = _stack_heads(q_ref[rr * GRID_W:(rr + 1) * GRID_W, :], hms)
        o = _attend(qs, keys, vals, bias_ref[var], kc, vc)
        o_ref[rr * GRID_W:(rr + 1) * GRID_W, :] = _unstack_heads(o, hmask, GRID_W).astype(BF16)


def _na_ctx_kernel(q_ref, kc_ref, vc_ref, hm_ref, o_ref):
    hmask = hm_ref[...]
    hms = (hmask * (NA_DIM ** -0.5)).astype(BF16)
    n = q_ref.shape[0]
    o = _attend(_stack_heads(q_ref[...], hms), None, None, None, kc_ref[...], vc_ref[...])
    o_ref[...] = _unstack_heads(o, hmask, n).astype(BF16)


def _neighborhood(proj_l, proj_c, bias, layer, hmask, batch, seq_len, ctx_len, need_ctx_out):
    rows = seq_len // GRID_W
    qblk = NA_QROWS * GRID_W
    nq = seq_len // qblk
    out_l = pl.pallas_call(
        functools.partial(_na_kernel, n_grid_rows=rows),
        grid=(batch, nq),
        in_specs=[
            pl.BlockSpec((qblk, BRANCH_W), lambda b, i: (b * nq + i, COL_NQ)),
            pl.BlockSpec((seq_len, BRANCH_W), lambda b, i: (b, COL_NK)),
            pl.BlockSpec((seq_len, BRANCH_W), lambda b, i: (b, COL_NV)),
            pl.BlockSpec((ctx_len, BRANCH_W), lambda b, i: (b, COL_NK)),
            pl.BlockSpec((ctx_len, BRANCH_W), lambda b, i: (b, COL_NV)),
            pl.BlockSpec((None,) + bias.shape[1:], lambda b, i: (layer, 0, 0, 0)),
            pl.BlockSpec(hmask.shape, lambda b, i: (0, 0)),
        ],
        out_specs=pl.BlockSpec((qblk, BRANCH_W), lambda b, i: (b * nq + i, 0)),
        out_shape=jax.ShapeDtypeStruct((batch * seq_len, BRANCH_W), BF16),
        compiler_params=_cparams("arbitrary", "arbitrary"),
        name="neighborhood_attn",
    )(proj_l, proj_l, proj_l, proj_c, proj_c, bias, hmask)
    out_c = None
    if need_ctx_out:
        out_c = pl.pallas_call(
            _na_ctx_kernel,
            grid=(batch,),
            in_specs=[
                pl.BlockSpec((ctx_len, BRANCH_W), lambda b: (b, COL_NQ)),
                pl.BlockSpec((ctx_len, BRANCH_W), lambda b: (b, COL_NK)),
                pl.BlockSpec((ctx_len, BRANCH_W), lambda b: (b, COL_NV)),
                pl.BlockSpec(hmask.shape, lambda b: (0, 0)),
            ],
            out_specs=pl.BlockSpec((ctx_len, BRANCH_W), lambda b: (b, 0)),
            out_shape=jax.ShapeDtypeStruct((batch * ctx_len, BRANCH_W), BF16),
            compiler_params=_cparams("arbitrary"),
            name="context_attn",
        )(proj_c, proj_c, proj_c, hmask)
    return out_l, out_c


def _merge_kernel(x_ref, mod_ref, g_ref, gt0, gt1, gt2, gt3, fa_ref, fb_ref, s5a_ref, s5b_ref, rof_ref, rob_ref, rg_ref, na_ref,
                  wglu_ref, bglu_ref, gn_ref, avg_ref, wb_ref, wo_ref, o_ref, *, tiles_per_mod, mod_base):
    i = pl.program_id(0)
    _, _, gate_a = _mod_rows(mod_ref, i, tiles_per_mod, mod_base, 0)
    z = _gelu_tanh(jnp.concatenate([s5a_ref[...], s5b_ref[...]], axis=-1)).astype(BF16)
    zf = z.astype(F32)
    b_s5 = (zf * _sigmoid(_dot(z, wglu_ref[...]) + bglu_ref[...])).astype(BF16)
    o = rof_ref[...] + rob_ref[...]
    avg = avg_ref[...]
    hi, lo = _split_bf16(o)
    mu = _dot(hi, avg) + _dot(lo, avg)
    dlt = o - mu
    hi, lo = _split_bf16(dlt * dlt)
    var = _dot(hi, avg) + _dot(lo, avg)
    hn = dlt * lax.rsqrt(var + EPS) * gn_ref[...]
    b_ret = (_silu(rg_ref[...].astype(F32)) * hn).astype(BF16)
    b_fnet = jnp.concatenate([fa_ref[...], fb_ref[...]], axis=-1).astype(BF16)
    outs = (b_fnet, b_s5, b_ret, na_ref[...])
    gates = (gt0, gt1, gt2, gt3)
    y = (1.0 + jnp.tanh(gates[0][...].astype(F32))) * _dot(outs[0], wb_ref[0])
    for b in range(1, N_BRANCH):
        y = y + (1.0 + jnp.tanh(gates[b][...].astype(F32))) * _dot(outs[b], wb_ref[b])
    yo = _dot(y.astype(BF16), wo_ref[...])
    o_ref[...] = x_ref[...] + gate_a * _rms(yo, g_ref[...])


def _merge(x, mod, g1, proj, a, s5y, ret_o, na, lw, *, rows_per_mod, mod_base):
    rows, d = x.shape
    tm = min(512, rows)
    nt = rows // tm

    def row(shape, col=0):
        return pl.BlockSpec(shape, lambda i: (i, col))

    def const(arr):
        return pl.BlockSpec(arr.shape, lambda i: (0,) * arr.ndim)

    kern = functools.partial(_merge_kernel, tiles_per_mod=max(rows_per_mod // tm, 1), mod_base=mod_base)
    ins = [x, mod, g1.reshape(1, d), proj, proj, proj, proj, a[0], a[1], s5y[0], s5y[1], ret_o[0], ret_o[1], proj, na,
           lw['w_glu'], lw['b_glu'], lw['ret_gn'], lw['avg'], lw['w_branch'], lw['w_out']]
    specs = [
        row((tm, d)), const(mod), pl.BlockSpec((1, d), lambda i: (0, 0)),
        row((tm, d), 0), row((tm, d), 1), row((tm, d), 2), row((tm, d), 3),
        row((tm, 128)), row((tm, 128)), row((tm, 128)), row((tm, 128)),
        row((tm, BRANCH_W)), row((tm, BRANCH_W)),
        row((tm, BRANCH_W), COL_RG), row((tm, BRANCH_W)),
        const(lw['w_glu']), const(lw['b_glu']), const(lw['ret_gn']), const(lw['avg']),
        const(lw['w_branch']), const(lw['w_out']),
    ]
    return pl.pallas_call(
        kern,
        grid=(nt,),
        in_specs=specs,
        out_specs=row((tm, d)),
        out_shape=jax.ShapeDtypeStruct((rows, d), F32),
        compiler_params=_cparams("arbitrary"),
        name="merge_out",
    )(*ins)


def _ffn_kernel(x_ref, mod_ref, g2_ref, g3_ref, wg_ref, wu_ref, wd_ref, o_ref, *, tiles_per_mod, mod_base):
    i = pl.program_id(0)
    sh, sc, gate_f = _mod_rows(mod_ref, i, tiles_per_mod, mod_base, 3)
    x = x_ref[...]
    h = (_rms(x, g2_ref[...]) * (1.0 + sc) + sh).astype(BF16)
    act = (_silu(_dot(h, wg_ref[...])) * _dot(h, wu_ref[...])).astype(BF16)
    y = _dot(act, wd_ref[...])
    o_ref[...] = x + gate_f * _rms(y, g3_ref[...])


def _ffn_dense(x, mod, g2, g3, wg, wu, wd, *, rows_per_mod, mod_base):
    rows, d = x.shape
    d_ff = wg.shape[1]
    tm = min(512, rows)
    kern = functools.partial(_ffn_kernel, tiles_per_mod=max(rows_per_mod // tm, 1), mod_base=mod_base)

    def resident(shape):
        return pl.BlockSpec(shape, lambda i: (0, 0), pipeline_mode=pl.Buffered(1))

    return pl.pallas_call(
        kern,
        grid=(rows // tm,),
        in_specs=[
            pl.BlockSpec((tm, d), lambda i: (i, 0)),
            pl.BlockSpec(mod.shape, lambda i: (0, 0)),
            pl.BlockSpec((1, d), lambda i: (0, 0)),
            pl.BlockSpec((1, d), lambda i: (0, 0)),
            resident((d, d_ff)), resident((d, d_ff)), resident((d_ff, d)),
        ],
        out_specs=pl.BlockSpec((tm, d), lambda i: (i, 0)),
        out_shape=jax.ShapeDtypeStruct((rows, d), F32),
        compiler_params=_cparams("arbitrary"),
        name="ffn_dense",
    )(x, mod, g2.reshape(1, d), g3.reshape(1, d), wg, wu, wd)


def _router_kernel(x_ref, mod_ref, g2_ref, wr_ref, br_ref, tri_ref, h_ref, comb_ref, plan_ref, cnt_ref, cnt_scr,
                   *, tiles_per_mod, mod_base):
    i = pl.program_id(0)

    @pl.when(i == 0)
    def _():
        cnt_scr[...] = jnp.zeros_like(cnt_scr)

    sh, sc, _ = _mod_rows(mod_ref, i, tiles_per_mod, mod_base, 3)
    h = _rms(x_ref[...], g2_ref[...]) * (1.0 + sc) + sh
    h_ref[...] = _pack_pairs(h)
    h_hi, h_lo = _split_bf16(h)
    w_hi, w_lo = _split_bf16(wr_ref[...])
    logits = _dot(h_hi, w_hi) + _dot(h_lo, w_hi) + _dot(h_hi, w_lo) + br_ref[...]
    lane = lax.broadcasted_iota(jnp.int32, logits.shape, 1)
    v1 = jnp.max(logits, axis=-1, keepdims=True)
    i1 = jnp.min(jnp.where(logits == v1, lane, 128), axis=-1, keepdims=True)
    rest = jnp.where(lane == i1, NEG_BIG, logits)
    v2 = jnp.max(rest, axis=-1, keepdims=True)
    i2 = jnp.min(jnp.where(rest == v2, lane, 128), axis=-1, keepdims=True)
    e = jnp.exp(v2 - v1)
    w1 = 1.0 / (1.0 + e)
    w2 = e / (1.0 + e)
    meta = jnp.where(lane == 0, i1.astype(F32), 0.0) + jnp.where(lane == 1, i2.astype(F32), 0.0)
    meta = meta + jnp.where(lane == 2, w1, 0.0) + jnp.where(lane == 3, w2, 0.0)
    member = jnp.where((lane == i1) | (lane == i2), 1.0, 0.0)
    before = _dot(tri_ref[...], member.astype(BF16)) + cnt_scr[...]
    rank1 = jnp.sum(jnp.where(lane == i1, before, 0.0), axis=-1, keepdims=True)
    rank2 = jnp.sum(jnp.where(lane == i2, before, 0.0), axis=-1, keepdims=True)
    cnt_scr[...] += jnp.sum(member, axis=0, keepdims=True)
    cnt_ref[...] = cnt_scr[...]
    meta = meta + jnp.where(lane == 4, rank1, 0.0) + jnp.where(lane == 5, rank2, 0.0)
    comb_ref[...] = meta[:, :MOE_META_W]
    plan_ref[...] = meta.T[:MOE_META_W]


def _router(x, mod, g2, w_router, b_router, *, rows_per_mod, mod_base):
    rows, d = x.shape
    tm = min(512, rows)
    wr = jnp.zeros((d, 128), F32).at[:, :N_EXPERTS].set(w_router)
    br = jnp.full((1, 128), NEG_BIG, F32).at[0, :N_EXPERTS].set(b_router)
    tri = jnp.asarray(np.tril(np.ones((tm, tm), np.float32), -1), BF16)
    kern = functools.partial(_router_kernel, tiles_per_mod=max(rows_per_mod // tm, 1), mod_base=mod_base)
    return pl.pallas_call(
        kern,
        grid=(rows // tm,),
        in_specs=[
            pl.BlockSpec((tm, d), lambda i: (i, 0)),
            pl.BlockSpec(mod.shape, lambda i: (0, 0)),
            pl.BlockSpec((1, d), lambda i: (0, 0)),
            pl.BlockSpec((d, 128), lambda i: (0, 0)),
            pl.BlockSpec((1, 128), lambda i: (0, 0)),
            pl.BlockSpec((tm, tm), lambda i: (0, 0)),
        ],
        out_specs=[
            pl.BlockSpec((tm, d // 2), lambda i: (i, 0)),
            pl.BlockSpec((tm, MOE_META_W), lambda i: (i, 0)),
            pl.BlockSpec((MOE_META_W, tm), lambda i: (0, i)),
            pl.BlockSpec((1, 128), lambda i: (0, 0)),
        ],
        out_shape=[
            jax.ShapeDtypeStruct((rows, d // 2), jnp.int32),
            jax.ShapeDtypeStruct((rows, MOE_META_W), F32),
            jax.ShapeDtypeStruct((MOE_META_W, rows), F32),
            jax.ShapeDtypeStruct((1, 128), F32),
        ],
        scratch_shapes=[pltpu.VMEM((1, 128), F32)],
        compiler_params=_cparams("arbitrary"),
        name="moe_router",
    )(x, mod, g2.reshape(1, d), wr, br, tri)


def _sc_gather(table, idx):
    n_idx = idx.shape[0]
    width = table.shape[1]
    per_worker = n_idx // SC_WORKERS
    chunk_rows = math.gcd(per_worker, SC_GATHER_ROWS)
    n_chunks = per_worker // chunk_rows
    assert per_worker * SC_WORKERS == n_idx and chunk_rows % 8 == 0
    mesh = plsc.VectorSubcoreMesh(core_axis_name="c", subcore_axis_name="s")

    assert n_chunks % 2 == 0
    buf = [pltpu.VMEM((chunk_rows,), jnp.int32), pltpu.VMEM((chunk_rows, width), table.dtype),
           pltpu.SemaphoreType.DMA, pltpu.SemaphoreType.DMA]

    @functools.partial(
        pl.kernel, mesh=mesh,
        out_type=jax.ShapeDtypeStruct((n_idx, width), table.dtype),
        scratch_types=buf + buf,
        name="sc_row_gather",
    )
    def gather(table_hbm, idx_hbm, out_hbm, idx0, rows0, g0, w0, idx1, rows1, g1, w1):
        wid = lax.axis_index("s") * SC_CORES + lax.axis_index("c")
        base = wid * per_worker
        slots = ((idx0, rows0, g0, w0), (idx1, rows1, g1, w1))

        def fetch(j, slot):
            idx_v, rows_v, g, _ = slots[slot]
            pltpu.sync_copy(idx_hbm.at[pl.ds(base + j * chunk_rows, chunk_rows)], idx_v)
            pltpu.make_async_copy(table_hbm.at[idx_v], rows_v, g).start()

        def store(j, slot):
            idx_v, rows_v, g, w = slots[slot]
            pltpu.make_async_copy(table_hbm.at[idx_v], rows_v, g).wait()
            pltpu.make_async_copy(rows_v, out_hbm.at[pl.ds(base + j * chunk_rows, chunk_rows)], w).start()

        def drain(j, slot):
            _, rows_v, _, w = slots[slot]
            pltpu.make_async_copy(rows_v, out_hbm.at[pl.ds(base + j * chunk_rows, chunk_rows)], w).wait()

        fetch(0, 0)

        @pl.loop(0, n_chunks // 2)
        def _(jj):
            j = 2 * jj

            @pl.when(jj > 0)
            def _():
                drain(j - 1, 1)

            fetch(j + 1, 1)
            store(j, 0)

            @pl.when(j + 2 < n_chunks)
            def _():
                drain(j, 0)
                fetch(j + 2, 0)

            store(j + 1, 1)

        drain(n_chunks - 2, 0)
        drain(n_chunks - 1, 1)

    return gather(table, idx)


def _sc_scatter(table, idx, n_out):
    n_idx = idx.shape[0]
    rows, width = table.shape
    per_worker = n_idx // SC_WORKERS
    chunk_rows = math.gcd(per_worker, SC_GATHER_ROWS)
    n_chunks = per_worker // chunk_rows
    assert per_worker * SC_WORKERS == n_idx and chunk_rows % 8 == 0 and rows % per_worker == 0
    mesh = plsc.VectorSubcoreMesh(core_axis_name="c", subcore_axis_name="s")

    assert n_chunks % 2 == 0
    buf = [pltpu.VMEM((chunk_rows,), jnp.int32), pltpu.VMEM((chunk_rows, width), table.dtype),
           pltpu.SemaphoreType.DMA, pltpu.SemaphoreType.DMA]

    @functools.partial(
        pl.kernel, mesh=mesh,
        out_type=jax.ShapeDtypeStruct((n_out, width), table.dtype),
        scratch_types=buf + buf,
        name="sc_row_scatter",
    )
    def scatter(table_hbm, idx_hbm, out_hbm, idx0, rows0, l0, w0, idx1, rows1, l1, w1):
        wid = lax.axis_index("s") * SC_CORES + lax.axis_index("c")
        base = wid * per_worker
        slots = ((idx0, rows0, l0, w0), (idx1, rows1, l1, w1))

        def src(j):
            return table_hbm.at[pl.ds(lax.rem(base + j * chunk_rows, rows), chunk_rows)]

        def fetch(j, slot):
            idx_v, rows_v, l, _ = slots[slot]
            pltpu.sync_copy(idx_hbm.at[pl.ds(base + j * chunk_rows, chunk_rows)], idx_v)
            pltpu.make_async_copy(src(j), rows_v, l).start()

        def store(j, slot):
            idx_v, rows_v, l, w = slots[slot]
            pltpu.make_async_copy(src(j), rows_v, l).wait()
            pltpu.make_async_copy(rows_v, out_hbm.at[idx_v], w).start()

        def drain(slot):
            idx_v, rows_v, _, w = slots[slot]
            pltpu.make_async_copy(rows_v, out_hbm.at[idx_v], w).wait()

        fetch(0, 0)

        @pl.loop(0, n_chunks // 2)
        def _(jj):
            j = 2 * jj

            @pl.when(jj > 0)
            def _():
                drain(1)

            fetch(j + 1, 1)
            store(j, 0)

            @pl.when(j + 2 < n_chunks)
            def _():
                drain(0)
                fetch(j + 2, 0)

            store(j + 1, 1)

        drain(0)
        drain(1)

    return scatter(table, idx)


def _moe_plan(plan, counts_row, rows):
    tile = MOE_ROW_TILE
    n_tiles = (2 * rows) // tile + N_EXPERTS
    n_slots = n_tiles * tile
    counts = counts_row[0, :N_EXPERTS].astype(jnp.int32)
    padded = ((counts + tile - 1) // tile) * tile
    ends = jnp.cumsum(padded)
    starts = ends - padded
    ids = jnp.arange(N_EXPERTS, dtype=F32)[:, None]
    start_f = starts.astype(F32)[:, None]

    def slot(e_row, r_row):
        return jnp.sum(jnp.where(e_row[None, :] == ids, start_f, 0.0), axis=0) + r_row

    pos = jnp.concatenate([slot(plan[0], plan[4]), slot(plan[1], plan[5])])
    tile_start = jnp.arange(n_tiles, dtype=jnp.int32) * tile
    used = tile_start < ends[-1]
    tile_e = jnp.minimum(jnp.sum((tile_start[:, None] >= ends[None, :]).astype(jnp.int32), axis=1), N_EXPERTS - 1)
    last_e = jnp.max(jnp.where(used, tile_e, 0))
    tile_e = jnp.where(used, tile_e, last_e)
    valid_end = jnp.sum((tile_e[:, None] == jnp.arange(N_EXPERTS)[None, :]) * (starts + counts)[None, :], axis=1)
    n_valid = jnp.where(used, jnp.clip(valid_end - tile_start, 0, tile), 0).astype(jnp.int32)
    return pos.astype(jnp.int32), n_slots, tile_e.astype(jnp.int32), n_valid


def _moe_group_kernel(eid_ref, nval_ref, hs_ref, wg_ref, wu_ref, wd_ref, y_ref, acc_scr, *, n_f):
    w = pl.program_id(0)
    f = pl.program_id(1)
    nv = nval_ref[w]

    def run(n_rows):
        wg = wg_ref[...].astype(BF16)
        wu = wu_ref[...].astype(BF16)
        wd = wd_ref[...].astype(BF16)
        for r0 in range(0, n_rows, MOE_SUB_ROWS):
            rows = slice(r0, r0 + MOE_SUB_ROWS)
            hv = _unpack_pairs(hs_ref[rows, :])
            row = r0 + lax.broadcasted_iota(jnp.int32, hv.shape, 0)
            h = jnp.where(row < nv, hv, 0.0).astype(BF16)
            part = _dot((_silu(_dot(h, wg)) * _dot(h, wu)).astype(BF16), wd)
            acc = jnp.where(f == 0, 0.0, acc_scr[rows, :]) + part
            acc_scr[rows, :] = acc
            y_ref[rows, :] = _pack_pairs(acc)

    for groups in range(1, hs_ref.shape[0] // MOE_SUB_ROWS + 1):
        @pl.when((nv > (groups - 1) * MOE_SUB_ROWS) & (nv <= groups * MOE_SUB_ROWS))
        def _(groups=groups):
            run(groups * MOE_SUB_ROWS)


def _moe_grouped(hs, tile_e, n_valid, wg, wu, wd):
    n_slots = hs.shape[0]
    d = wg.shape[1]
    d_ff = wg.shape[2]
    tile = MOE_ROW_TILE
    tf = MOE_FF_TILE
    n_f = d_ff // tf

    def f_idx(f, nval, w):
        return jnp.where(nval[w] > 0, f, n_f - 1)

    grid_spec = pltpu.PrefetchScalarGridSpec(
        num_scalar_prefetch=2,
        grid=(n_slots // tile, n_f),
        in_specs=[
            pl.BlockSpec((tile, d // 2), lambda w, f, eid, nval: (w, 0)),
            pl.BlockSpec((None, d, tf), lambda w, f, eid, nval: (eid[w], 0, f_idx(f, nval, w))),
            pl.BlockSpec((None, d, tf), lambda w, f, eid, nval: (eid[w], 0, f_idx(f, nval, w))),
            pl.BlockSpec((None, tf, d), lambda w, f, eid, nval: (eid[w], f_idx(f, nval, w), 0)),
        ],
        out_specs=pl.BlockSpec((tile, d // 2), lambda w, f, eid, nval: (w, 0)),
        scratch_shapes=[pltpu.VMEM((tile, d), F32)],
    )
    return pl.pallas_call(
        functools.partial(_moe_group_kernel, n_f=n_f),
        grid_spec=grid_spec,
        out_shape=jax.ShapeDtypeStruct((n_slots, d // 2), jnp.int32),
        compiler_params=_cparams("arbitrary", "arbitrary"),
        name="moe_experts",
    )(tile_e, n_valid, hs, wg, wu, wd)


def _moe_out_kernel(x_ref, y1_ref, y2_ref, meta_ref, mod_ref, g3_ref, o_ref, *, tiles_per_mod, mod_base):
    i = pl.program_id(0)
    _, _, gate_f = _mod_rows(mod_ref, i, tiles_per_mod, mod_base, 3)
    meta = meta_ref[...]
    y = meta[:, 2:3] * _unpack_pairs(y1_ref[...]) + meta[:, 3:4] * _unpack_pairs(y2_ref[...])
    o_ref[...] = x_ref[...] + gate_f * _rms(y, g3_ref[...])


def _moe_combine(x, yg, meta, mod, g3, *, rows_per_mod, mod_base):
    rows, d = x.shape
    tm = min(512, rows)
    nt = rows // tm
    kern = functools.partial(_moe_out_kernel, tiles_per_mod=max(rows_per_mod // tm, 1), mod_base=mod_base)
    return pl.pallas_call(
        kern,
        grid=(nt,),
        in_specs=[
            pl.BlockSpec((tm, d), lambda i: (i, 0)),
            pl.BlockSpec((tm, d // 2), lambda i: (i, 0)),
            pl.BlockSpec((tm, d // 2), lambda i: (nt + i, 0)),
            pl.BlockSpec((tm, MOE_META_W), lambda i: (i, 0)),
            pl.BlockSpec(mod.shape, lambda i: (0, 0)),
            pl.BlockSpec((1, d), lambda i: (0, 0)),
        ],
        out_specs=pl.BlockSpec((tm, d), lambda i: (i, 0)),
        out_shape=jax.ShapeDtypeStruct((rows, d), F32),
        compiler_params=_cparams("arbitrary"),
        name="moe_combine",
    )(x, yg, yg, meta, mod, g3.reshape(1, d))


def _moe_sparse(x, routed, mod, g3, wg, wu, wd, *, rows_per_mod, mod_base):
    h, meta, plan, counts = routed
    rows = x.shape[0]
    pos, n_slots, tile_e, n_valid = _moe_plan(plan, counts, rows)
    hs = _sc_scatter(h, pos, n_slots)
    ys = _moe_grouped(hs, tile_e, n_valid, wg, wu, wd)
    yg = _sc_gather(ys, pos)
    return _moe_combine(x, yg, meta, mod, g3, rows_per_mod=rows_per_mod, mod_base=mod_base)


def _cast_kernel(w_ref, o_ref, *, scale):
    w = w_ref[...]
    o_ref[...] = (w if scale == 1.0 else w * scale).astype(BF16)


def _cast_bf16(w_stack, layer, scale=1.0):
    squeeze = w_stack.ndim == 3
    w4 = w_stack[:, None] if squeeze else w_stack
    _, n_e, k, n = w4.shape
    bk = min(k, 256)
    out = pl.pallas_call(
        functools.partial(_cast_kernel, scale=scale),
        grid=(n_e, k // bk),
        in_specs=[pl.BlockSpec((None, None, bk, n), lambda e, i: (layer, e, i, 0))],
        out_specs=pl.BlockSpec((None, bk, n), lambda e, i: (e, i, 0)),
        out_shape=jax.ShapeDtypeStruct((n_e, k, n), BF16),
        compiler_params=_cparams("arbitrary", "arbitrary"),
        name="cast_weights",
    )(w4)
    return out[0] if squeeze else out


def _permute_w_in(w_in_stack, layer):
    _, k, n = w_in_stack.shape
    n_blocks = n // BRANCH_W
    shift = 9
    n_gate_blocks = N_BRANCH * D_MODEL // BRANCH_W

    per_step = 5
    assert n_blocks % per_step == 0

    def permute_kernel(*refs):
        o_ref = refs[-1]
        for s, w_ref in enumerate(refs[:-1]):
            scale = jnp.where(pl.program_id(0) * per_step + s < n_gate_blocks, 0.5, 1.0)
            o_ref[:, s * BRANCH_W:(s + 1) * BRANCH_W] = (w_ref[...] * scale).astype(BF16)

    def src(s):
        return pl.BlockSpec((None, k, BRANCH_W), lambda j: (layer, 0, (j * per_step + s + shift) % n_blocks))

    return pl.pallas_call(
        permute_kernel,
        grid=(n_blocks // per_step,),
        in_specs=[src(s) for s in range(per_step)],
        out_specs=pl.BlockSpec((k, per_step * BRANCH_W), lambda j: (0, j)),
        out_shape=jax.ShapeDtypeStruct((k, n), BF16),
        compiler_params=_cparams("arbitrary"),
        name="cast_permute_w_in",
    )(*([w_in_stack] * per_step))


def kernel(x, c, ctx, c_ctx, w_mod, b_mod, norm_g, w_in, s5_a_re, s5_a_im, s5_log_dt, s5_b_re, s5_b_im, s5_c_re, s5_c_im, s5_d, s5_w_glu, s5_b_glu, ret_decay, ret_gn, na_rpb, w_branch, w_out, ffn_w_gate, ffn_w_up, ffn_w_down, moe_w_router, moe_b_router, moe_w_gate, moe_w_up, moe_w_down):
    batch, seq_len, d = x.shape
    ctx_len = ctx.shape[1]
    depth = w_mod.shape[0]
    cond = jnp.concatenate([c, c_ctx[None, :]], axis=0)
    mod_all = _modulation(cond, w_mod, b_mod)
    rope = _rope_tables(seq_len)
    lane_h = np.repeat(np.arange(RET_HEADS), RET_DIM)
    avg = jnp.asarray((lane_h[:, None] == lane_h[None, :]).astype(np.float32) / RET_DIM, BF16)

    xl = x.reshape(batch * seq_len, d)
    xc = ctx.reshape(batch * ctx_len, d)
    lat = dict(rows_per_mod=seq_len, mod_base=0)
    cxt = dict(rows_per_mod=batch * ctx_len, mod_base=batch)

    s5_tabs = jax.vmap(functools.partial(_s5_tables, batch=batch))(
        s5_a_re, s5_a_im, s5_log_dt, s5_b_re, s5_b_im, s5_c_re, s5_c_im, s5_d)
    ret_tabs = jax.vmap(_ret_tables)(ret_decay)
    ret_masks = _ret_masks()
    na_bias = jax.vmap(_na_tables)(na_rpb)
    na_hmask = _na_head_mask()

    for layer in range(depth):
        last = layer == depth - 1
        need_ctx = not last
        mod = mod_all[layer]
        ng = norm_g[layer]
        w_in_bf = _permute_w_in(w_in, layer)
        lw = dict(w_glu=s5_w_glu[layer].astype(BF16), b_glu=s5_b_glu[layer].reshape(1, BRANCH_W).astype(F32),
                  ret_gn=ret_gn[layer].reshape(1, BRANCH_W).astype(F32), avg=avg,
                  w_branch=_cast_bf16(w_branch, layer, 0.5), w_out=_cast_bf16(w_out, layer))

        proj_l, f_l, *s_in_l = _in_proj(xl, mod, ng[0], w_in_bf, **lat)
        proj_c, f_c, *s_in_c = _in_proj(xc, mod, ng[0], w_in_bf, **cxt)

        a_l = _fourier_latent(f_l, batch, seq_len)
        s_l, s_c = _s5_mixer(s_in_l, s_in_c, s5_tabs, layer, batch)
        r_l, r_c = _retention(proj_l, proj_c, ret_tabs, layer, ret_masks, rope, batch, seq_len, ctx_len)
        n_l, n_c = _neighborhood(proj_l, proj_c, na_bias, layer, na_hmask, batch, seq_len, ctx_len, need_ctx)

        xl = _merge(xl, mod, ng[1], proj_l, a_l, s_l, r_l, n_l, lw, **lat)
        if need_ctx:
            a_c = _fourier_ctx(f_c, batch, ctx_len)
            xc = _merge(xc, mod, ng[1], proj_c, a_c, s_c, r_c, n_c, lw, **cxt)

        i = layer // 2
        if layer % 2 == 0:
            wg, wu, wd = _cast_bf16(ffn_w_gate, i), _cast_bf16(ffn_w_up, i), _cast_bf16(ffn_w_down, i)
            xl = _ffn_dense(xl, mod, ng[2], ng[3], wg, wu, wd, **lat)
            if need_ctx:
                xc = _ffn_dense(xc, mod, ng[2], ng[3], wg, wu, wd, **cxt)
        else:
            wg, wu, wd = moe_w_gate[i], moe_w_up[i], moe_w_down[i]
            routed = _router(xl, mod, ng[2], moe_w_router[i], moe_b_router[i], **lat)
            xl = _moe_sparse(xl, routed, mod, ng[3], wg, wu, wd, **lat)
            if need_ctx:
                routed_c = _router(xc, mod, ng[2], moe_w_router[i], moe_b_router[i], **cxt)
                xc = _moe_sparse(xc, routed_c, mod, ng[3], wg, wu, wd, **cxt)
    return xl.reshape(batch, seq_len, d)
```

```python
import functools
import math

import numpy as np
import jax
import jax.numpy as jnp
from jax import lax
from jax.experimental import pallas as pl
from jax.experimental.pallas import tpu as pltpu
from jax.experimental.pallas import tpu_sc as plsc

F32 = jnp.float32
BF16 = jnp.bfloat16

D_MODEL = 1024
BRANCH_W = 256
N_BRANCH = 4
GRID_W = 64
FNET_GROUP_DIM = 64
S5_GROUP_CH = 16
S5_GROUPS = 16
S5_STATE = 64
S5_CHUNK = 32
S5_PAIRS = S5_GROUPS // 2
RET_HEADS = 4
RET_DIM = 64
RET_CHUNK = 128
NA_HEADS = 4
NA_DIM = 64
NA_WIN_ROWS = 8
NA_WIN_COLS = 16
NA_QROWS = 16
ROPE_BASE = 10000.0
N_EXPERTS = 8
EPS = 1e-6
FFT_N2 = 256
NEG_BIG = -1e30
VMEM_LIMIT_BYTES = 50 * 1024 * 1024
SC_CORES = 2
SC_SUBCORES = 16
SC_WORKERS = SC_CORES * SC_SUBCORES
SC_GATHER_ROWS = 64
MOE_ROW_TILE = 2048
MOE_SUB_ROWS = 512
MOE_FF_TILE = 512
MOE_META_W = 8

COL_F, COL_S, COL_RQ, COL_RK, COL_RV, COL_RG, COL_NQ, COL_NK, COL_NV = range(16, 25)
IN_W = 9 * BRANCH_W + N_BRANCH * D_MODEL
IN_TN = 1280
IN_F_TILE = (N_BRANCH * D_MODEL) // IN_TN
IN_F_OFF = N_BRANCH * D_MODEL - IN_F_TILE * IN_TN
IN_S_OFF = IN_F_OFF + BRANCH_W


def _cparams(*sem):
    return pltpu.CompilerParams(dimension_semantics=sem, vmem_limit_bytes=VMEM_LIMIT_BYTES)


def _sigmoid(v):
    return 0.5 * jnp.tanh(0.5 * v) + 0.5


def _silu(v):
    return v * _sigmoid(v)


def _gelu_tanh(v):
    return 0.5 * v * (1.0 + jnp.tanh(math.sqrt(2.0 / math.pi) * (v + 0.044715 * (v * v * v))))


def _rms(v, g):
    ms = jnp.mean(v * v, axis=-1, keepdims=True)
    return v * lax.rsqrt(ms + EPS) * g


def _split_bf16(v):
    hi = v.astype(BF16)
    lo = (v - hi.astype(F32)).astype(BF16)
    return hi, lo


def _pack_pairs(v):
    n = v.shape[1] // 2
    lo = lax.bitcast_convert_type(v[:, :n].astype(BF16).astype(F32), jnp.int32)
    hi = lax.bitcast_convert_type(v[:, n:].astype(BF16).astype(F32), jnp.int32)
    return (hi & -65536) | ((lo >> 16) & 65535)


def _unpack_pairs(w):
    lo = lax.bitcast_convert_type(w << 16, F32)
    hi = lax.bitcast_convert_type(w & -65536, F32)
    return jnp.concatenate([lo, hi], axis=-1)


def _dot(a, b):
    return jnp.dot(a, b, preferred_element_type=F32)


def _dot_nt(a, b):
    return lax.dot_general(a, b, (((1,), (1,)), ((), ())), preferred_element_type=F32)


def _dot_tn(a, b):
    return lax.dot_general(a, b, (((0,), (0,)), ((), ())), preferred_element_type=F32)


def _mod_kernel(ct_ref, w_ref, b_ref, o_ref, *, n_cond):
    ct = ct_ref[...]
    s = _silu(ct)
    w = w_ref[...]
    rows = [jnp.sum(w * s[:, r:r + 1], axis=0, keepdims=True) for r in range(n_cond)]
    rows.append(jnp.zeros((8 - n_cond, w.shape[1]), F32))
    o_ref[...] = jnp.concatenate(rows, axis=0) + b_ref[...]


def _modulation(cond, w_mod, b_mod):
    n_layers, d, n = w_mod.shape
    tn = 1536
    ct = jnp.zeros((8, d), F32).at[:cond.shape[0]].set(cond).T
    return pl.pallas_call(
        functools.partial(_mod_kernel, n_cond=cond.shape[0]),
        grid=(n_layers, n // tn),
        in_specs=[
            pl.BlockSpec((d, 8), lambda l, j: (0, 0)),
            pl.BlockSpec((None, d, tn), lambda l, j: (l, 0, j)),
            pl.BlockSpec((None, 1, tn), lambda l, j: (l, 0, j)),
        ],
        out_specs=pl.BlockSpec((None, 8, tn), lambda l, j: (l, 0, j)),
        out_shape=jax.ShapeDtypeStruct((n_layers, 8, n), F32),
        compiler_params=_cparams("arbitrary", "arbitrary"),
        name="adaln_mod",
    )(ct, w_mod, b_mod.reshape(n_layers, 1, n))


def _mod_rows(mod_ref, i, tiles_per_mod, mod_base, first):
    r = mod_base + i // tiles_per_mod
    return [mod_ref[pl.ds(r, 1), (first + k) * D_MODEL:(first + k + 1) * D_MODEL] for k in range(3)]


def _in_kernel(x_ref, mod_ref, g_ref, w_ref, proj_ref, f_ref, sa_ref, sb_ref, *, tiles_per_mod, mod_base):
    i = pl.program_id(0)
    sh, sc, _ = _mod_rows(mod_ref, i, tiles_per_mod, mod_base, 0)
    h = (_rms(x_ref[...], g_ref[...]) * (1.0 + sc) + sh).astype(BF16)
    for j in range(IN_W // IN_TN):
        res = _dot(h, w_ref[:, j * IN_TN:(j + 1) * IN_TN])
        proj_ref[:, j * IN_TN:(j + 1) * IN_TN] = res.astype(BF16)
        if j == IN_F_TILE:
            f_ref[...] = res[:, IN_F_OFF:IN_F_OFF + BRANCH_W].astype(BF16)
            sa_ref[...] = res[:, IN_S_OFF:IN_S_OFF + 128]
            sb_ref[...] = res[:, IN_S_OFF + 128:IN_S_OFF + 256]


def _in_proj(x, mod, g, w_bf, *, rows_per_mod, mod_base):
    rows, d = x.shape
    tm = math.gcd(512, rows_per_mod)
    kern = functools.partial(_in_kernel, tiles_per_mod=max(rows_per_mod // tm, 1), mod_base=mod_base)
    return pl.pallas_call(
        kern,
        grid=(rows // tm,),
        in_specs=[
            pl.BlockSpec((tm, d), lambda i: (i, 0)),
            pl.BlockSpec(mod.shape, lambda i: (0, 0)),
            pl.BlockSpec((1, d), lambda i: (0, 0)),
            pl.BlockSpec((d, IN_W), lambda i: (0, 0), pipeline_mode=pl.Buffered(1)),
        ],
        out_specs=[
            pl.BlockSpec((tm, IN_W), lambda i: (i, 0)),
            pl.BlockSpec((tm, BRANCH_W), lambda i: (i, 0)),
            pl.BlockSpec((tm, 128), lambda i: (i, 0)),
            pl.BlockSpec((tm, 128), lambda i: (i, 0)),
        ],
        out_shape=[
            jax.ShapeDtypeStruct((rows, IN_W), BF16),
            jax.ShapeDtypeStruct((rows, BRANCH_W), BF16),
            jax.ShapeDtypeStruct((rows, 128), F32),
            jax.ShapeDtypeStruct((rows, 128), F32),
        ],
        compiler_params=_cparams("arbitrary"),
        name="in_proj",
    )(x, mod, g.reshape(1, d), w_bf)


def _fft_a_kernel(x_ref, cs_ref, tc_ref, ts_ref, zr_ref, zi_ref, *, n1, n1p):
    y = _dot(cs_ref[...].astype(BF16), x_ref[...])
    yr = y[:n1]
    yi = y[n1p:n1p + n1]
    tc = tc_ref[...]
    ts = ts_ref[...]
    zr_ref[...] = (yr * tc + yi * ts).astype(BF16)
    zi_ref[...] = (yi * tc - yr * ts).astype(BF16)


def _fft_b_kernel(zr_ref, zi_ref, cs_ref, cc_ref, sc_ref, oa_ref, ob_ref, *, kb, n1, scale, has_imag):
    cs = cs_ref[...].astype(BF16)
    cc = cc_ref[...].astype(BF16)
    sc = sc_ref[...].astype(BF16)
    half = BRANCH_W // 2
    for kk in range(kb):
        a = _dot(cs, zr_ref[kk])
        if has_imag:
            b = _dot(cs, zi_ref[kk])
            xr = a[:FFT_N2] + b[FFT_N2:]
            xi = b[:FFT_N2] - a[FFT_N2:]
        else:
            xr = a[:FFT_N2]
            xi = -a[FFT_N2:]
        out = (_dot(xr.astype(BF16), cc) + _dot(xi.astype(BF16), sc)) * scale
        k1 = pl.program_id(1) * kb + kk
        oa_ref[pl.ds(k1, FFT_N2, stride=n1), :] = out[:, :half]
        ob_ref[pl.ds(k1, FFT_N2, stride=n1), :] = out[:, half:]


def _dft_tables(n):
    k = np.arange(n)
    ang = 2.0 * np.pi * ((k[:, None] * k[None, :]) % n) / n
    return np.cos(ang), np.sin(ang)


def _fft_b_call(zr, zi, n1, batch, seq_len, has_imag):
    c2, s2 = _dft_tables(FFT_N2)
    cs2 = jnp.asarray(np.concatenate([c2, s2], axis=0), F32)
    c64, s64 = _dft_tables(FNET_GROUP_DIM)
    eye = np.eye(BRANCH_W // FNET_GROUP_DIM)
    cc = jnp.asarray(np.kron(eye, c64), F32)
    sc = jnp.asarray(np.kron(eye, s64), F32)
    kb = min(8, n1)
    scale = 1.0 / math.sqrt(seq_len * FNET_GROUP_DIM)
    kern = functools.partial(_fft_b_kernel, kb=kb, n1=n1, scale=scale, has_imag=has_imag)
    zspec = pl.BlockSpec((None, kb, FFT_N2, BRANCH_W), lambda b, i: (b, i, 0, 0))
    half = pl.BlockSpec((seq_len, BRANCH_W // 2), lambda b, i: (b, 0))
    return pl.pallas_call(
        kern,
        grid=(batch, n1 // kb),
        in_specs=[
            zspec, zspec,
            pl.BlockSpec((2 * FFT_N2, FFT_N2), lambda b, i: (0, 0)),
            pl.BlockSpec((BRANCH_W, BRANCH_W), lambda b, i: (0, 0)),
            pl.BlockSpec((BRANCH_W, BRANCH_W), lambda b, i: (0, 0)),
        ],
        out_specs=[half, half],
        out_shape=[jax.ShapeDtypeStruct((batch * seq_len, BRANCH_W // 2), F32)] * 2,
        compiler_params=_cparams("arbitrary", "arbitrary"),
        name="fourier_stage_b",
    )(zr, zi, cs2, cc, sc)


def _fourier_latent(f, batch, seq_len):
    n1 = seq_len // FFT_N2
    wide = FFT_N2 * BRANCH_W
    c1, s1 = _dft_tables(n1)
    n1p = max(n1, 8)
    cs1 = np.zeros((2 * n1p, n1))
    cs1[:n1] = c1
    cs1[n1p:n1p + n1] = -s1
    k1 = np.arange(n1)[:, None]
    l2 = np.arange(FFT_N2)[None, :]
    tw = 2.0 * np.pi * (k1 * l2) / seq_len
    tc = jnp.asarray(np.repeat(np.cos(tw), BRANCH_W, axis=1), F32)
    ts = jnp.asarray(np.repeat(np.sin(tw), BRANCH_W, axis=1), F32)
    cw = min(8192, wide)
    xv = f.reshape(batch, n1, wide)
    spec = pl.BlockSpec((None, n1, cw), lambda b, j: (b, 0, j))
    tspec = pl.BlockSpec((n1, cw), lambda b, j: (0, j))
    zr, zi = pl.pallas_call(
        functools.partial(_fft_a_kernel, n1=n1, n1p=n1p),
        grid=(batch, wide // cw),
        in_specs=[spec, pl.BlockSpec((2 * n1p, n1), lambda b, j: (0, 0)), tspec, tspec],
        out_specs=[spec, spec],
        out_shape=[jax.ShapeDtypeStruct((batch, n1, wide), BF16)] * 2,
        compiler_params=_cparams("arbitrary", "arbitrary"),
        name="fourier_stage_a",
    )(xv, jnp.asarray(cs1, F32), tc, ts)
    zr = zr.reshape(batch, n1, FFT_N2, BRANCH_W)
    zi = zi.reshape(batch, n1, FFT_N2, BRANCH_W)
    return _fft_b_call(zr, zi, n1, batch, seq_len, True)


def _fourier_ctx(f, batch, ctx_len):
    assert ctx_len == FFT_N2
    z = f.reshape(batch, 1, FFT_N2, BRANCH_W)
    return _fft_b_call(z, z, 1, batch, ctx_len, False)


def _s5_tables(a_re, a_im, log_dt, b_re, b_im, c_re, c_im, d_skip, batch):
    t = S5_CHUNK
    g, p, hc = S5_GROUPS, S5_STATE, S5_GROUP_CH
    lam = lax.complex(a_re.astype(F32), a_im.astype(F32))
    dt = jnp.exp(log_dt.astype(F32))[..., None]
    ks = jnp.arange(t + 1, dtype=F32)
    apow = jnp.exp((lam * dt)[..., None] * ks)
    a_bar = apow[..., 1]
    b_bar = ((a_bar - 1.0) / lam)[..., None] * lax.complex(b_re.astype(F32), b_im.astype(F32))
    cm = lax.complex(c_re.astype(F32), c_im.astype(F32))
    lagv = jnp.arange(-(t - 1), t, dtype=F32)
    ldt = lam * dt
    pw_f = jnp.where(lagv >= 0, jnp.exp(ldt[0][..., None] * jnp.maximum(lagv, 0.0)), 0.0)
    pw_b = jnp.where(lagv <= 0, jnp.exp(ldt[1][..., None] * jnp.maximum(-lagv, 0.0)), 0.0)
    kfull = jnp.real(jnp.einsum('ghp,gpl,gpj->gjlh',
                                jnp.concatenate([cm[0], cm[1]], axis=-1),
                                jnp.concatenate([pw_f, pw_b], axis=1),
                                jnp.concatenate([b_bar[0], b_bar[1]], axis=1),
                                precision=lax.Precision.HIGHEST))
    kp = kfull.reshape(S5_PAIRS, 2, hc, 2 * t - 1, hc)
    blk = [kp[:, gi] for gi in range(2)]
    zb = jnp.zeros_like(blk[0])
    strip = jnp.concatenate([jnp.stack([blk[0], zb], axis=3), jnp.stack([zb, blk[1]], axis=3)], axis=1)
    strip = strip.reshape(S5_PAIRS, 2 * hc, (2 * t - 1) * 2 * hc)
    strip = jnp.pad(strip, ((0, 0), (0, 0), (0, 2 * hc)))

    wf = jnp.einsum('gpj,gph->gjhp', apow[0][..., t - 1::-1][..., :t], b_bar[0])
    wb = jnp.einsum('gpj,gph->gjhp', apow[1][..., :t], b_bar[1])
    kinds = [jnp.real(wf), jnp.imag(wf), jnp.real(wb), jnp.imag(wb)]

    def we_pair(kd):
        k5 = kd.reshape(S5_PAIRS, 2, t, hc, p)
        z = jnp.zeros_like(k5[:, 0])
        rows = jnp.stack([jnp.concatenate([k5[:, 0], z], axis=-1), jnp.concatenate([z, k5[:, 1]], axis=-1)], axis=2)
        return rows.reshape(S5_PAIRS, 2 * t * hc, 2 * p)

    we = jnp.concatenate([we_pair(kd) for kd in kinds], axis=-1).astype(BF16)

    vf = jnp.einsum('ghp,gpt->gpth', cm[0], apow[0][..., 1:t + 1])
    vb = jnp.einsum('ghp,gpt->gpth', cm[1], apow[1][..., t:0:-1])
    vkinds = [jnp.real(vf), -jnp.imag(vf), jnp.real(vb), -jnp.imag(vb)]

    def v_pair(kd):
        k5 = kd.reshape(S5_PAIRS, 2, p, t, hc)
        z = jnp.zeros_like(k5[:, 0])
        rows = jnp.concatenate([jnp.stack([k5[:, 0], z], axis=3), jnp.stack([z, k5[:, 1]], axis=3)], axis=1)
        return rows.reshape(S5_PAIRS, 2 * p, 2 * t * hc)

    v1 = jnp.concatenate([v_pair(kd) for kd in vkinds], axis=1)
    v = jnp.concatenate([v1, v1], axis=1).astype(BF16)

    def lanes(z):
        return jnp.tile(z.reshape(1, g * p), (1, batch))

    at = apow[..., t]
    a_tab = jnp.concatenate([lanes(jnp.real(at[0])), lanes(jnp.imag(at[0])),
                             lanes(jnp.real(at[1])), lanes(jnp.imag(at[1]))], axis=0)
    dvec = jnp.tile(d_skip.astype(F32).reshape(S5_PAIRS, 1, 2 * hc), (1, t, 1)).reshape(S5_PAIRS, 1, 2 * t * hc)
    return dict(strip=strip, we=we, v=v, a_tab=a_tab, dvec=dvec)


def _s5_pack_kernel(xa_ref, xb_ref, u_ref, *, n_chunks):
    per_half = S5_PAIRS // 2
    for half, x_ref in enumerate((xa_ref, xb_ref)):
        rows = [x_ref[pl.ds(tau, n_chunks, stride=S5_CHUNK), :] for tau in range(S5_CHUNK)]
        for qq in range(per_half):
            pieces = [r[:, qq * 32:(qq + 1) * 32] for r in rows]
            u_ref[half * per_half + qq] = jnp.concatenate(pieces, axis=-1).astype(BF16)


def _s5_unpack_kernel(y_ref, oa_ref, ob_ref, *, n_chunks):
    per_half = S5_PAIRS // 2
    for half, o_ref in enumerate((oa_ref, ob_ref)):
        ys = [y_ref[half * per_half + qq].astype(F32) for qq in range(per_half)]
        for t in range(S5_CHUNK):
            pieces = [y[:, t * 32:(t + 1) * 32] for y in ys]
            o_ref[pl.ds(t, n_chunks, stride=S5_CHUNK), :] = jnp.concatenate(pieces, axis=-1)


def _s5_pack(sa, sb, batch):
    n_chunks = sa.shape[0] // batch // S5_CHUNK
    rows = n_chunks * S5_CHUNK
    cols = 2 * S5_CHUNK * S5_GROUP_CH
    half = pl.BlockSpec((rows, 128), lambda b: (b, 0))
    return pl.pallas_call(
        functools.partial(_s5_pack_kernel, n_chunks=n_chunks),
        grid=(batch,),
        in_specs=[half, half],
        out_specs=pl.BlockSpec((S5_PAIRS, None, n_chunks, cols), lambda b: (0, b, 0, 0)),
        out_shape=jax.ShapeDtypeStruct((S5_PAIRS, batch, n_chunks, cols), BF16),
        compiler_params=_cparams("arbitrary"),
        name="s5_pack",
    )(sa, sb)


def _s5_unpack(y, batch):
    n_chunks = y.shape[2]
    rows = n_chunks * S5_CHUNK
    cols = y.shape[3]
    half = pl.BlockSpec((rows, 128), lambda b: (b, 0))
    return pl.pallas_call(
        functools.partial(_s5_unpack_kernel, n_chunks=n_chunks),
        grid=(batch,),
        in_specs=[pl.BlockSpec((S5_PAIRS, None, n_chunks, cols), lambda b: (0, b, 0, 0))],
        out_specs=[half, half],
        out_shape=[jax.ShapeDtypeStruct((batch * rows, 128), F32)] * 2,
        compiler_params=_cparams("arbitrary"),
        name="s5_unpack",
    )(y)


def _s5_e_kernel(ul_ref, uc_ref, we_ref, ref_, imf_, reb_, imb_):
    u = jnp.concatenate([ul_ref[...], uc_ref[...]], axis=0)
    e = _dot(u, we_ref[...])
    ref_[...] = e[:, 0:128]
    imf_[...] = e[:, 128:256]
    reb_[...] = e[:, 256:384]
    imb_[...] = e[:, 384:512]


def _s5_scan_kernel(a_ref, ref_, imf_, reb_, imb_, prf, pif, prb, pib, *, n_rows, n_ctx):
    afr = a_ref[0:1, :]
    afi = a_ref[1:2, :]
    abr = a_ref[2:3, :]
    abi = a_ref[3:4, :]
    zero = jnp.zeros_like(afr)

    n_lat = n_rows - n_ctx

    def body(s, carry):
        sfr, sfi, sbr, sbi = carry
        nf = jnp.where(s < n_ctx, n_lat + s, s - n_ctx)
        nb = n_rows - 1 - s
        prf[pl.ds(nf, 1), :] = sfr
        pif[pl.ds(nf, 1), :] = sfi
        prb[pl.ds(nb, 1), :] = sbr
        pib[pl.ds(nb, 1), :] = sbi
        efr = ref_[pl.ds(nf, 1), :]
        efi = imf_[pl.ds(nf, 1), :]
        ebr = reb_[pl.ds(nb, 1), :]
        ebi = imb_[pl.ds(nb, 1), :]
        nfr = afr * sfr - afi * sfi + efr
        nfi = afr * sfi + afi * sfr + efi
        nbr = abr * sbr - abi * sbi + ebr
        nbi = abr * sbi + abi * sbr + ebi
        return nfr, nfi, nbr, nbi

    lax.fori_loop(0, n_rows, body, (zero, zero, zero, zero))


def _s5_y_kernel(ul_ref, uc_ref, strip_ref, v_ref, d_ref, prf, pif, prb, pib, yl_ref, yc_ref, m_scr):
    width = 2 * S5_GROUP_CH
    cols = S5_CHUNK * width
    n_lat = yl_ref.shape[0]

    @pl.when(pl.program_id(1) == 0)
    def _():
        strip = strip_ref[...]
        for j in range(S5_CHUNK):
            off = (S5_CHUNK - 1 - j) * width
            win = strip if off == 0 else pltpu.roll(strip, 2 * cols - off, axis=1)
            m_scr[j * width:(j + 1) * width, :] = win[:, :cols].astype(BF16)

    u = jnp.concatenate([ul_ref[...], uc_ref[...]], axis=0)
    y_intra = _dot(u, m_scr[...])
    pcat = jnp.concatenate([prf[...], pif[...], prb[...], pib[...]], axis=-1)
    hi, lo = _split_bf16(pcat)
    y_cross = _dot(jnp.concatenate([hi, lo], axis=-1), v_ref[...])
    y = y_intra + y_cross + d_ref[...] * u.astype(F32)
    yl_ref[...] = y[:n_lat].astype(BF16)
    yc_ref[...] = y[n_lat:].astype(BF16)


def _s5_core(ul, uc, tabs, layer, batch):
    n_lat, n_ctx = ul.shape[2], uc.shape[2]
    n_rows = n_lat + n_ctx
    width = batch * S5_PAIRS * 128
    cols = 2 * S5_CHUNK * S5_GROUP_CH
    ul_spec = pl.BlockSpec((None, None, n_lat, cols), lambda q, b: (q, b, 0, 0))
    uc_spec = pl.BlockSpec((None, None, n_ctx, cols), lambda q, b: (q, b, 0, 0))
    st_spec = pl.BlockSpec((n_rows, 128), lambda q, b: (0, b * S5_PAIRS + q))
    st_shape = jax.ShapeDtypeStruct((n_rows, width), F32)
    e4 = pl.pallas_call(
        _s5_e_kernel,
        grid=(S5_PAIRS, batch),
        in_specs=[ul_spec, uc_spec, pl.BlockSpec((None, None, cols, 512), lambda q, b: (layer, q, 0, 0))],
        out_specs=[st_spec] * 4,
        out_shape=[st_shape] * 4,
        compiler_params=_cparams("arbitrary", "arbitrary"),
        name="s5_chunk_states",
    )(ul, uc, tabs['we'])
    p4 = pl.pallas_call(
        functools.partial(_s5_scan_kernel, n_rows=n_rows, n_ctx=n_ctx),
        out_shape=[st_shape] * 4,
        compiler_params=pltpu.CompilerParams(vmem_limit_bytes=VMEM_LIMIT_BYTES),
        name="s5_state_scan",
    )(tabs['a_tab'][layer], *e4)
    y = pl.pallas_call(
        _s5_y_kernel,
        grid=(S5_PAIRS, batch),
        in_specs=[
            ul_spec, uc_spec,
            pl.BlockSpec((None, None, 2 * S5_GROUP_CH, 2 * cols), lambda q, b: (layer, q, 0, 0)),
            pl.BlockSpec((None, None, cols, cols), lambda q, b: (layer, q, 0, 0)),
            pl.BlockSpec((None, None, 1, cols), lambda q, b: (layer, q, 0, 0)),
            st_spec, st_spec, st_spec, st_spec,
        ],
        out_specs=[ul_spec, uc_spec],
        out_shape=[
            jax.ShapeDtypeStruct((S5_PAIRS, batch, n_lat, cols), BF16),
            jax.ShapeDtypeStruct((S5_PAIRS, batch, n_ctx, cols), BF16),
        ],
        scratch_shapes=[pltpu.VMEM((cols, cols), BF16)],
        compiler_params=_cparams("arbitrary", "arbitrary"),
        name="s5_outputs",
    )(ul, uc, tabs['strip'], tabs['v'], tabs['dvec'], *p4)
    return y


def _s5_mixer(s_lat, s_ctx, tabs, layer, batch):
    ul = _s5_pack(*s_lat, batch)
    uc = _s5_pack(*s_ctx, batch)
    yl, yc = _s5_core(ul, uc, tabs, layer, batch)
    return _s5_unpack(yl, batch), _s5_unpack(yc, batch)


def _ret_tables(ret_decay):
    c = RET_CHUNK
    lg = jax.nn.log_sigmoid(ret_decay.astype(F32))
    lane_h = np.repeat(np.arange(RET_HEADS), RET_DIM)
    lgl = jnp.repeat(lg, RET_DIM, axis=1)
    pos = jnp.arange(c, dtype=F32)[:, None]
    qd = jnp.stack([jnp.exp((pos + 1.0) * lgl[0][None]), jnp.exp((c - pos) * lgl[1][None])])
    kd = jnp.stack([jnp.exp((c - 1.0 - pos) * lgl[0][None]), jnp.exp(pos * lgl[1][None])])
    bmask = jnp.asarray((lane_h[:, None] == lane_h[None, :]).astype(np.float32))
    cd = jnp.exp(c * lgl)[:, :, None] * bmask[None]
    diff = pos - pos.T
    dm = []
    for h in range(RET_HEADS):
        fw = jnp.where(diff >= 0, jnp.exp(jnp.maximum(diff, 0.0) * lg[0, h]), 0.0)
        bw = jnp.where(diff <= 0, jnp.exp(jnp.maximum(-diff, 0.0) * lg[1, h]), 0.0)
        dm.append(fw + bw)
    dm = jnp.concatenate(dm, axis=0)
    return dict(qd=qd, kd=kd, cd=cd, dm=dm)


def _ret_masks():
    lane_h = np.repeat(np.arange(RET_HEADS), RET_DIM)
    bmask = (lane_h[:, None] == lane_h[None, :]).astype(np.float32)
    hmask = (np.arange(RET_HEADS)[:, None] == lane_h[None, :]).astype(np.float32)
    return jnp.asarray(bmask), jnp.asarray(hmask)


def _rope_tables(n_tokens):
    t = np.arange(n_tokens)
    row = (t // GRID_W).astype(np.float64)
    col = (t % GRID_W).astype(np.float64)
    n_freq = RET_DIM // 4
    inv_freq = 1.0 / (ROPE_BASE ** (np.arange(n_freq, dtype=np.float64) / n_freq))
    ang = np.concatenate([row[:, None] * inv_freq, col[:, None] * inv_freq], axis=-1)
    cos = np.cos(ang)
    sin = np.sin(ang)
    cos_t = np.tile(np.concatenate([cos, cos], axis=-1), (1, RET_HEADS))
    sin_t = np.tile(np.concatenate([-sin, sin], axis=-1), (1, RET_HEADS))
    half = RET_DIM // 2
    perm = np.arange(BRANCH_W) ^ half
    swap = np.zeros((BRANCH_W, BRANCH_W), np.float32)
    swap[perm, np.arange(BRANCH_W)] = 1.0
    return jnp.asarray(cos_t, F32), jnp.asarray(sin_t, F32), jnp.asarray(swap, BF16)


def _ret_chunk(q, k, v, s, qd, kd, cd, bmask, dm, hmask, with_intra):
    cross = _dot((q * qd).astype(BF16), s.astype(BF16))
    s_new = cd * s + bmask * _dot_tn((k * kd).astype(BF16), v)
    if not with_intra:
        return cross, s_new
    qb = q.astype(BF16)
    kb = k.astype(BF16)
    qs = jnp.concatenate([qb * hmask[h:h + 1].astype(BF16) for h in range(RET_HEADS)], axis=0)
    scores = _dot_nt(qs, kb) * dm
    ov = _dot(scores.astype(BF16), v)
    c = q.shape[0]
    inner = ov[0:c] * hmask[0:1]
    for h in range(1, RET_HEADS):
        inner = inner + ov[h * c:(h + 1) * c] * hmask[h:h + 1]
    return inner + cross, s_new


def _ret_kernel(qf_ref, kf_ref, vf_ref, qb_ref, kb_ref, vb_ref, qc_ref, kc_ref, vc_ref,
                cosf_ref, sinf_ref, cosb_ref, sinb_ref, swap_ref,
                qd_ref, kd_ref, cd_ref, bm_ref, dm_ref, hm_ref,
                of_ref, ob_ref, ocf_ref, ocb_ref, sf_scr, sb_scr, *, n_chunks, n_ctx_chunks):
    i = pl.program_id(1)
    c = RET_CHUNK
    k_scale = RET_DIM ** -0.5
    bmask = bm_ref[...]
    dm = dm_ref[...]
    hmask = hm_ref[...]
    tabs = [(qd_ref[d], kd_ref[d], cd_ref[d]) for d in range(2)]

    @pl.when(i == 0)
    def _():
        for d, oc_ref, s_scr in ((0, ocf_ref, sf_scr), (1, ocb_ref, sb_scr)):
            qd, kd, cd = tabs[d]
            s = jnp.zeros((BRANCH_W, BRANCH_W), F32)
            order = range(n_ctx_chunks) if d == 0 else range(n_ctx_chunks - 1, -1, -1)
            for cc in order:
                sl = slice(cc * c, (cc + 1) * c)
                o, s = _ret_chunk(qc_ref[sl, :].astype(F32), kc_ref[sl, :].astype(F32) * k_scale, vc_ref[sl, :],
                                  s, qd, kd, cd, bmask, dm, hmask, d == 0)
                oc_ref[sl, :] = o
            s_scr[...] = s

    swap = swap_ref[...]

    def rope(x_ref, cos_ref, sin_ref):
        xb = x_ref[...]
        return xb.astype(F32) * cos_ref[...] + _dot(xb, swap) * sin_ref[...]

    q_f = rope(qf_ref, cosf_ref, sinf_ref)
    k_f = rope(kf_ref, cosf_ref, sinf_ref) * k_scale
    q_b = rope(qb_ref, cosb_ref, sinb_ref)
    k_b = rope(kb_ref, cosb_ref, sinb_ref) * k_scale
    sf = sf_scr[...]
    sb = sb_scr[...]
    for step in range(n_chunks):
        sl = slice(step * c, (step + 1) * c)
        o, sf = _ret_chunk(q_f[sl], k_f[sl], vf_ref[sl, :], sf, *tabs[0], bmask, dm, hmask, True)
        of_ref[sl, :] = o
        cb = n_chunks - 1 - step
        sl = slice(cb * c, (cb + 1) * c)
        o, sb = _ret_chunk(q_b[sl], k_b[sl], vb_ref[sl, :], sb, *tabs[1], bmask, dm, hmask, False)
        ob_ref[sl, :] = o
    sf_scr[...] = sf
    sb_scr[...] = sb


def _retention(proj_l, proj_c, tabs, layer, masks, rope, batch, seq_len, ctx_len):
    n_chunks = 8
    blk = n_chunks * RET_CHUNK
    nblk = seq_len // blk
    cos_t, sin_t, swap = rope

    def lat(col, back):
        if back:
            return pl.BlockSpec((blk, BRANCH_W), lambda b, i: (b * nblk + nblk - 1 - i, col))
        return pl.BlockSpec((blk, BRANCH_W), lambda b, i: (b * nblk + i, col))

    def ctx(col):
        return pl.BlockSpec((ctx_len, BRANCH_W), lambda b, i: (b, col))

    def const(shape):
        return pl.BlockSpec(shape, lambda b, i: (0,) * len(shape))

    def per_layer(shape):
        return pl.BlockSpec((None,) + shape, lambda b, i: (layer,) + (0,) * len(shape))

    tab_f = pl.BlockSpec((blk, BRANCH_W), lambda b, i: (i, 0))
    tab_b = pl.BlockSpec((blk, BRANCH_W), lambda b, i: (nblk - 1 - i, 0))
    kern = functools.partial(_ret_kernel, n_chunks=n_chunks, n_ctx_chunks=ctx_len // RET_CHUNK)
    c = RET_CHUNK
    ctx_out = pl.BlockSpec((ctx_len, BRANCH_W), lambda b, i: (b, 0))
    o_f, o_b, oc_f, oc_b = pl.pallas_call(
        kern,
        grid=(batch, nblk),
        in_specs=[
            lat(COL_RQ, False), lat(COL_RK, False), lat(COL_RV, False),
            lat(COL_RQ, True), lat(COL_RK, True), lat(COL_RV, True),
            ctx(COL_RQ), ctx(COL_RK), ctx(COL_RV),
            tab_f, tab_f, tab_b, tab_b, const((BRANCH_W, BRANCH_W)),
            per_layer((2, c, BRANCH_W)), per_layer((2, c, BRANCH_W)), per_layer((2, BRANCH_W, BRANCH_W)),
            const((BRANCH_W, BRANCH_W)), per_layer((RET_HEADS * c, c)), const((RET_HEADS, BRANCH_W)),
        ],
        out_specs=[lat(0, False), lat(0, True), ctx_out, ctx_out],
        out_shape=[
            jax.ShapeDtypeStruct((batch * seq_len, BRANCH_W), F32),
            jax.ShapeDtypeStruct((batch * seq_len, BRANCH_W), F32),
            jax.ShapeDtypeStruct((batch * ctx_len, BRANCH_W), F32),
            jax.ShapeDtypeStruct((batch * ctx_len, BRANCH_W), F32),
        ],
        scratch_shapes=[pltpu.VMEM((BRANCH_W, BRANCH_W), F32), pltpu.VMEM((BRANCH_W, BRANCH_W), F32)],
        compiler_params=_cparams("arbitrary", "arbitrary"),
        name="retention",
    )(proj_l, proj_l, proj_l, proj_l, proj_l, proj_l, proj_c, proj_c, proj_c,
      cos_t, sin_t, cos_t, sin_t, swap,
      tabs['qd'], tabs['kd'], tabs['cd'], masks[0], tabs['dm'], masks[1])
    return (o_f, o_b), (oc_f, oc_b)


def _na_tables(rpb):
    kr, kw = NA_WIN_ROWS, NA_WIN_COLS
    col = np.arange(GRID_W)
    col_start = np.clip(col - kw // 2, 0, GRID_W - kw)
    in_win = (col[None, :] >= col_start[:, None]) & (col[None, :] < col_start[:, None] + kw)
    dc = np.clip(col[None, :] - col[:, None], -(kw - 1), kw - 1) + (kw - 1)
    pick_c = (dc[:, :, None] == np.arange(2 * kw - 1)[None, None, :]).astype(np.float32)
    by = jnp.einsum('hrc,qkc->hqrk', rpb.astype(F32), jnp.asarray(pick_c), precision=lax.Precision.HIGHEST)
    by = jnp.where(jnp.asarray(in_win)[None, :, None, :], by, NEG_BIG)
    bias = jnp.stack([by[:, :, v:v + kr, :] for v in range(kr)], axis=0)
    return bias.reshape(kr, NA_HEADS * GRID_W, kr * GRID_W)


def _na_head_mask():
    lane_h = np.repeat(np.arange(NA_HEADS), NA_DIM)
    hmask = (np.arange(NA_HEADS)[:, None] == lane_h[None, :]).astype(np.float32)
    return jnp.asarray(hmask, F32)


def _attend(qs, keys, vals, bias, kc, vc):
    s_ctx = _dot_nt(qs, kc)
    m = jnp.max(s_ctx, axis=-1, keepdims=True)
    if keys is not None:
        s_band = _dot_nt(qs, keys) + bias
        m = jnp.maximum(m, jnp.max(s_band, axis=-1, keepdims=True))
        p_band = jnp.exp(s_band - m)
    p_ctx = jnp.exp(s_ctx - m)
    l = jnp.sum(p_ctx, axis=-1, keepdims=True)
    o = _dot(p_ctx.astype(BF16), vc)
    if keys is not None:
        l = l + jnp.sum(p_band, axis=-1, keepdims=True)
        o = o + _dot(p_band.astype(BF16), vals)
    return o / l


def _stack_heads(q, hmask_scaled):
    return jnp.concatenate([q * hmask_scaled[h:h + 1] for h in range(NA_HEADS)], axis=0)


def _unstack_heads(o, hmask, n):
    out = o[0:n] * hmask[0:1]
    for h in range(1, NA_HEADS):
        out = out + o[h * n:(h + 1) * n] * hmask[h:h + 1]
    return out


def _na_kernel(q_ref, k_ref, v_ref, kc_ref, vc_ref, bias_ref, hm_ref, o_ref, *, n_grid_rows):
    i = pl.program_id(1)
    hmask = hm_ref[...]
    hms = (hmask * (NA_DIM ** -0.5)).astype(BF16)
    kc = kc_ref[...]
    vc = vc_ref[...]
    band = NA_WIN_ROWS * GRID_W
    for rr in range(NA_QROWS):
        r = i * NA_QROWS + rr
        rs = jnp.clip(r - NA_WIN_ROWS // 2, 0, n_grid_rows - NA_WIN_ROWS)
        var = rs - r + (NA_WIN_ROWS - 1)
        start = pl.multiple_of(rs * GRID_W, GRID_W)
        keys = k_ref[pl.ds(start, band), :]
        vals = v_ref[pl.ds(start, band), :]
        qs = _stack_heads(q_ref[rr * GRID_W:(rr + 1) * GRID_W, :], hms)
        o = _attend(qs, keys, vals, bias_ref[var], kc, vc)
        o_ref[rr * GRID_W:(rr + 1) * GRID_W, :] = _unstack_heads(o, hmask, GRID_W).astype(BF16)


def _na_ctx_kernel(q_ref, kc_ref, vc_ref, hm_ref, o_ref):
    hmask = hm_ref[...]
    hms = (hmask * (NA_DIM ** -0.5)).astype(BF16)
    n = q_ref.shape[0]
    o = _attend(_stack_heads(q_ref[...], hms), None, None, None, kc_ref[...], vc_ref[...])
    o_ref[...] = _unstack_heads(o, hmask, n).astype(BF16)


def _neighborhood(proj_l, proj_c, bias, layer, hmask, batch, seq_len, ctx_len, need_ctx_out):
    rows = seq_len // GRID_W
    qblk = NA_QROWS * GRID_W
    nq = seq_len // qblk
    out_l = pl.pallas_call(
        functools.partial(_na_kernel, n_grid_rows=rows),
        grid=(batch, nq),
        in_specs=[
            pl.BlockSpec((qblk, BRANCH_W), lambda b, i: (b * nq + i, COL_NQ)),
            pl.BlockSpec((seq_len, BRANCH_W), lambda b, i: (b, COL_NK)),
            pl.BlockSpec((seq_len, BRANCH_W), lambda b, i: (b, COL_NV)),
            pl.BlockSpec((ctx_len, BRANCH_W), lambda b, i: (b, COL_NK)),
            pl.BlockSpec((ctx_len, BRANCH_W), lambda b, i: (b, COL_NV)),
            pl.BlockSpec((None,) + bias.shape[1:], lambda b, i: (layer, 0, 0, 0)),
            pl.BlockSpec(hmask.shape, lambda b, i: (0, 0)),
        ],
        out_specs=pl.BlockSpec((qblk, BRANCH_W), lambda b, i: (b * nq + i, 0)),
        out_shape=jax.ShapeDtypeStruct((batch * seq_len, BRANCH_W), BF16),
        compiler_params=_cparams("arbitrary", "arbitrary"),
        name="neighborhood_attn",
    )(proj_l, proj_l, proj_l, proj_c, proj_c, bias, hmask)
    out_c = None
    if need_ctx_out:
        out_c = pl.pallas_call(
            _na_ctx_kernel,
            grid=(batch,),
            in_specs=[
                pl.BlockSpec((ctx_len, BRANCH_W), lambda b: (b, COL_NQ)),
                pl.BlockSpec((ctx_len, BRANCH_W), lambda b: (b, COL_NK)),
                pl.BlockSpec((ctx_len, BRANCH_W), lambda b: (b, COL_NV)),
                pl.BlockSpec(hmask.shape, lambda b: (0, 0)),
            ],
            out_specs=pl.BlockSpec((ctx_len, BRANCH_W), lambda b: (b, 0)),
            out_shape=jax.ShapeDtypeStruct((batch * ctx_len, BRANCH_W), BF16),
            compiler_params=_cparams("arbitrary"),
            name="context_attn",
        )(proj_c, proj_c, proj_c, hmask)
    return out_l, out_c


def _merge_kernel(x_ref, mod_ref, g_ref, gt0, gt1, gt2, gt3, fa_ref, fb_ref, s5a_ref, s5b_ref, rof_ref, rob_ref, rg_ref, na_ref,
                  wglu_ref, bglu_ref, gn_ref, avg_ref, wb_ref, wo_ref, o_ref, *, tiles_per_mod, mod_base):
    i = pl.program_id(0)
    _, _, gate_a = _mod_rows(mod_ref, i, tiles_per_mod, mod_base, 0)
    z = _gelu_tanh(jnp.concatenate([s5a_ref[...], s5b_ref[...]], axis=-1)).astype(BF16)
    zf = z.astype(F32)
    b_s5 = (zf * _sigmoid(_dot(z, wglu_ref[...]) + bglu_ref[...])).astype(BF16)
    o = rof_ref[...] + rob_ref[...]
    avg = avg_ref[...]
    hi, lo = _split_bf16(o)
    mu = _dot(hi, avg) + _dot(lo, avg)
    dlt = o - mu
    hi, lo = _split_bf16(dlt * dlt)
    var = _dot(hi, avg) + _dot(lo, avg)
    hn = dlt * lax.rsqrt(var + EPS) * gn_ref[...]
    b_ret = (_silu(rg_ref[...].astype(F32)) * hn).astype(BF16)
    b_fnet = jnp.concatenate([fa_ref[...], fb_ref[...]], axis=-1).astype(BF16)
    outs = (b_fnet, b_s5, b_ret, na_ref[...])
    gates = (gt0, gt1, gt2, gt3)
    y = (1.0 + jnp.tanh(gates[0][...].astype(F32))) * _dot(outs[0], wb_ref[0])
    for b in range(1, N_BRANCH):
        y = y + (1.0 + jnp.tanh(gates[b][...].astype(F32))) * _dot(outs[b], wb_ref[b])
    yo = _dot(y.astype(BF16), wo_ref[...])
    o_ref[...] = x_ref[...] + gate_a * _rms(yo, g_ref[...])


def _merge(x, mod, g1, proj, a, s5y, ret_o, na, lw, *, rows_per_mod, mod_base):
    rows, d = x.shape
    tm = min(512, rows)
    nt = rows // tm

    def row(shape, col=0):
        return pl.BlockSpec(shape, lambda i: (i, col))

    def const(arr):
        return pl.BlockSpec(arr.shape, lambda i: (0,) * arr.ndim)

    kern = functools.partial(_merge_kernel, tiles_per_mod=max(rows_per_mod // tm, 1), mod_base=mod_base)
    ins = [x, mod, g1.reshape(1, d), proj, proj, proj, proj, a[0], a[1], s5y[0], s5y[1], ret_o[0], ret_o[1], proj, na,
           lw['w_glu'], lw['b_glu'], lw['ret_gn'], lw['avg'], lw['w_branch'], lw['w_out']]
    specs = [
        row((tm, d)), const(mod), pl.BlockSpec((1, d), lambda i: (0, 0)),
        row((tm, d), 0), row((tm, d), 1), row((tm, d), 2), row((tm, d), 3),
        row((tm, 128)), row((tm, 128)), row((tm, 128)), row((tm, 128)),
        row((tm, BRANCH_W)), row((tm, BRANCH_W)),
        row((tm, BRANCH_W), COL_RG), row((tm, BRANCH_W)),
        const(lw['w_glu']), const(lw['b_glu']), const(lw['ret_gn']), const(lw['avg']),
        const(lw['w_branch']), const(lw['w_out']),
    ]
    return pl.pallas_call(
        kern,
        grid=(nt,),
        in_specs=specs,
        out_specs=row((tm, d)),
        out_shape=jax.ShapeDtypeStruct((rows, d), F32),
        compiler_params=_cparams("arbitrary"),
        name="merge_out",
    )(*ins)


def _ffn_kernel(x_ref, mod_ref, g2_ref, g3_ref, wg_ref, wu_ref, wd_ref, o_ref, *, tiles_per_mod, mod_base):
    i = pl.program_id(0)
    sh, sc, gate_f = _mod_rows(mod_ref, i, tiles_per_mod, mod_base, 3)
    x = x_ref[...]
    h = (_rms(x, g2_ref[...]) * (1.0 + sc) + sh).astype(BF16)
    act = (_silu(_dot(h, wg_ref[...])) * _dot(h, wu_ref[...])).astype(BF16)
    y = _dot(act, wd_ref[...])
    o_ref[...] = x + gate_f * _rms(y, g3_ref[...])


def _ffn_dense(x, mod, g2, g3, wg, wu, wd, *, rows_per_mod, mod_base):
    rows, d = x.shape
    d_ff = wg.shape[1]
    tm = min(512, rows)
    kern = functools.partial(_ffn_kernel, tiles_per_mod=max(rows_per_mod // tm, 1), mod_base=mod_base)

    def resident(shape):
        return pl.BlockSpec(shape, lambda i: (0, 0), pipeline_mode=pl.Buffered(1))

    return pl.pallas_call(
        kern,
        grid=(rows // tm,),
        in_specs=[
            pl.BlockSpec((tm, d), lambda i: (i, 0)),
            pl.BlockSpec(mod.shape, lambda i: (0, 0)),
            pl.BlockSpec((1, d), lambda i: (0, 0)),
            pl.BlockSpec((1, d), lambda i: (0, 0)),
            resident((d, d_ff)), resident((d, d_ff)), resident((d_ff, d)),
        ],
        out_specs=pl.BlockSpec((tm, d), lambda i: (i, 0)),
        out_shape=jax.ShapeDtypeStruct((rows, d), F32),
        compiler_params=_cparams("arbitrary"),
        name="ffn_dense",
    )(x, mod, g2.reshape(1, d), g3.reshape(1, d), wg, wu, wd)


def _router_kernel(x_ref, mod_ref, g2_ref, wr_ref, br_ref, tri_ref, h_ref, comb_ref, plan_ref, cnt_ref, cnt_scr,
                   *, tiles_per_mod, mod_base):
    i = pl.program_id(0)

    @pl.when(i == 0)
    def _():
        cnt_scr[...] = jnp.zeros_like(cnt_scr)

    sh, sc, _ = _mod_rows(mod_ref, i, tiles_per_mod, mod_base, 3)
    h = _rms(x_ref[...], g2_ref[...]) * (1.0 + sc) + sh
    h_ref[...] = _pack_pairs(h)
    h_hi, h_lo = _split_bf16(h)
    w_hi, w_lo = _split_bf16(wr_ref[...])
    logits = _dot(h_hi, w_hi) + _dot(h_lo, w_hi) + _dot(h_hi, w_lo) + br_ref[...]
    lane = lax.broadcasted_iota(jnp.int32, logits.shape, 1)
    v1 = jnp.max(logits, axis=-1, keepdims=True)
    i1 = jnp.min(jnp.where(logits == v1, lane, 128), axis=-1, keepdims=True)
    rest = jnp.where(lane == i1, NEG_BIG, logits)
    v2 = jnp.max(rest, axis=-1, keepdims=True)
    i2 = jnp.min(jnp.where(rest == v2, lane, 128), axis=-1, keepdims=True)
    e = jnp.exp(v2 - v1)
    w1 = 1.0 / (1.0 + e)
    w2 = e / (1.0 + e)
    meta = jnp.where(lane == 0, i1.astype(F32), 0.0) + jnp.where(lane == 1, i2.astype(F32), 0.0)
    meta = meta + jnp.where(lane == 2, w1, 0.0) + jnp.where(lane == 3, w2, 0.0)
    member = jnp.where((lane == i1) | (lane == i2), 1.0, 0.0)
    before = _dot(tri_ref[...], member.astype(BF16)) + cnt_scr[...]
    rank1 = jnp.sum(jnp.where(lane == i1, before, 0.0), axis=-1, keepdims=True)
    rank2 = jnp.sum(jnp.where(lane == i2, before, 0.0), axis=-1, keepdims=True)
    cnt_scr[...] += jnp.sum(member, axis=0, keepdims=True)
    cnt_ref[...] = cnt_scr[...]
    meta = meta + jnp.where(lane == 4, rank1, 0.0) + jnp.where(lane == 5, rank2, 0.0)
    comb_ref[...] = meta[:, :MOE_META_W]
    plan_ref[...] = meta.T[:MOE_META_W]


def _router(x, mod, g2, w_router, b_router, *, rows_per_mod, mod_base):
    rows, d = x.shape
    tm = min(512, rows)
    wr = jnp.zeros((d, 128), F32).at[:, :N_EXPERTS].set(w_router)
    br = jnp.full((1, 128), NEG_BIG, F32).at[0, :N_EXPERTS].set(b_router)
    tri = jnp.asarray(np.tril(np.ones((tm, tm), np.float32), -1), BF16)
    kern = functools.partial(_router_kernel, tiles_per_mod=max(rows_per_mod // tm, 1), mod_base=mod_base)
    return pl.pallas_call(
        kern,
        grid=(rows // tm,),
        in_specs=[
            pl.BlockSpec((tm, d), lambda i: (i, 0)),
            pl.BlockSpec(mod.shape, lambda i: (0, 0)),
            pl.BlockSpec((1, d), lambda i: (0, 0)),
            pl.BlockSpec((d, 128), lambda i: (0, 0)),
            pl.BlockSpec((1, 128), lambda i: (0, 0)),
            pl.BlockSpec((tm, tm), lambda i: (0, 0)),
        ],
        out_specs=[
            pl.BlockSpec((tm, d // 2), lambda i: (i, 0)),
            pl.BlockSpec((tm, MOE_META_W), lambda i: (i, 0)),
            pl.BlockSpec((MOE_META_W, tm), lambda i: (0, i)),
            pl.BlockSpec((1, 128), lambda i: (0, 0)),
        ],
        out_shape=[
            jax.ShapeDtypeStruct((rows, d // 2), jnp.int32),
            jax.ShapeDtypeStruct((rows, MOE_META_W), F32),
            jax.ShapeDtypeStruct((MOE_META_W, rows), F32),
            jax.ShapeDtypeStruct((1, 128), F32),
        ],
        scratch_shapes=[pltpu.VMEM((1, 128), F32)],
        compiler_params=_cparams("arbitrary"),
        name="moe_router",
    )(x, mod, g2.reshape(1, d), wr, br, tri)


def _sc_gather(table, idx):
    n_idx = idx.shape[0]
    width = table.shape[1]
    per_worker = n_idx // SC_WORKERS
    chunk_rows = math.gcd(per_worker, SC_GATHER_ROWS)
    n_chunks = per_worker // chunk_rows
    assert per_worker * SC_WORKERS == n_idx and chunk_rows % 8 == 0
    mesh = plsc.VectorSubcoreMesh(core_axis_name="c", subcore_axis_name="s")

    assert n_chunks % 2 == 0
    buf = [pltpu.VMEM((chunk_rows,), jnp.int32), pltpu.VMEM((chunk_rows, width), table.dtype),
           pltpu.SemaphoreType.DMA, pltpu.SemaphoreType.DMA]

    @functools.partial(
        pl.kernel, mesh=mesh,
        out_type=jax.ShapeDtypeStruct((n_idx, width), table.dtype),
        scratch_types=buf + buf,
        name="sc_row_gather",
    )
    def gather(table_hbm, idx_hbm, out_hbm, idx0, rows0, g0, w0, idx1, rows1, g1, w1):
        wid = lax.axis_index("s") * SC_CORES + lax.axis_index("c")
        base = wid * per_worker
        slots = ((idx0, rows0, g0, w0), (idx1, rows1, g1, w1))

        def fetch(j, slot):
            idx_v, rows_v, g, _ = slots[slot]
            pltpu.sync_copy(idx_hbm.at[pl.ds(base + j * chunk_rows, chunk_rows)], idx_v)
            pltpu.make_async_copy(table_hbm.at[idx_v], rows_v, g).start()

        def store(j, slot):
            idx_v, rows_v, g, w = slots[slot]
            pltpu.make_async_copy(table_hbm.at[idx_v], rows_v, g).wait()
            pltpu.make_async_copy(rows_v, out_hbm.at[pl.ds(base + j * chunk_rows, chunk_rows)], w).start()

        def drain(j, slot):
            _, rows_v, _, w = slots[slot]
            pltpu.make_async_copy(rows_v, out_hbm.at[pl.ds(base + j * chunk_rows, chunk_rows)], w).wait()

        fetch(0, 0)

        @pl.loop(0, n_chunks // 2)
        def _(jj):
            j = 2 * jj

            @pl.when(jj > 0)
            def _():
                drain(j - 1, 1)

            fetch(j + 1, 1)
            store(j, 0)

            @pl.when(j + 2 < n_chunks)
            def _():
                drain(j, 0)
                fetch(j + 2, 0)

            store(j + 1, 1)

        drain(n_chunks - 2, 0)
        drain(n_chunks - 1, 1)

    return gather(table, idx)


def _sc_scatter(table, idx, n_out):
    n_idx = idx.shape[0]
    rows, width = table.shape
    per_worker = n_idx // SC_WORKERS
    chunk_rows = math.gcd(per_worker, SC_GATHER_ROWS)
    n_chunks = per_worker // chunk_rows
    assert per_worker * SC_WORKERS == n_idx and chunk_rows % 8 == 0 and rows % per_worker == 0
    mesh = plsc.VectorSubcoreMesh(core_axis_name="c", subcore_axis_name="s")

    assert n_chunks % 2 == 0
    buf = [pltpu.VMEM((chunk_rows,), jnp.int32), pltpu.VMEM((chunk_rows, width), table.dtype),
           pltpu.SemaphoreType.DMA, pltpu.SemaphoreType.DMA]

    @functools.partial(
        pl.kernel, mesh=mesh,
        out_type=jax.ShapeDtypeStruct((n_out, width), table.dtype),
        scratch_types=buf + buf,
        name="sc_row_scatter",
    )
    def scatter(table_hbm, idx_hbm, out_hbm, idx0, rows0, l0, w0, idx1, rows1, l1, w1):
        wid = lax.axis_index("s") * SC_CORES + lax.axis_index("c")
        base = wid * per_worker
        slots = ((idx0, rows0, l0, w0), (idx1, rows1, l1, w1))

        def src(j):
            return table_hbm.at[pl.ds(lax.rem(base + j * chunk_rows, rows), chunk_rows)]

        def fetch(j, slot):
            idx_v, rows_v, l, _ = slots[slot]
            pltpu.sync_copy(idx_hbm.at[pl.ds(base + j * chunk_rows, chunk_rows)], idx_v)
            pltpu.make_async_copy(src(j), rows_v, l).start()

        def store(j, slot):
            idx_v, rows_v, l, w = slots[slot]
            pltpu.make_async_copy(src(j), rows_v, l).wait()
            pltpu.make_async_copy(rows_v, out_hbm.at[idx_v], w).start()

        def drain(slot):
            idx_v, rows_v, _, w = slots[slot]
            pltpu.make_async_copy(rows_v, out_hbm.at[idx_v], w).wait()

        fetch(0, 0)

        @pl.loop(0, n_chunks // 2)
        def _(jj):
            j = 2 * jj

            @pl.when(jj > 0)
            def _():
                drain(1)

            fetch(j + 1, 1)
            store(j, 0)

            @pl.when(j + 2 < n_chunks)
            def _():
                drain(0)
                fetch(j + 2, 0)

            store(j + 1, 1)

        drain(0)
        drain(1)

    return scatter(table, idx)


def _moe_plan(plan, counts_row, rows):
    tile = MOE_ROW_TILE
    n_tiles = (2 * rows) // tile + N_EXPERTS
    n_slots = n_tiles * tile
    counts = counts_row[0, :N_EXPERTS].astype(jnp.int32)
    padded = ((counts + tile - 1) // tile) * tile
    ends = jnp.cumsum(padded)
    starts = ends - padded
    ids = jnp.arange(N_EXPERTS, dtype=F32)[:, None]
    start_f = starts.astype(F32)[:, None]

    def slot(e_row, r_row):
        return jnp.sum(jnp.where(e_row[None, :] == ids, start_f, 0.0), axis=0) + r_row

    pos = jnp.concatenate([slot(plan[0], plan[4]), slot(plan[1], plan[5])])
    tile_start = jnp.arange(n_tiles, dtype=jnp.int32) * tile
    used = tile_start < ends[-1]
    tile_e = jnp.minimum(jnp.sum((tile_start[:, None] >= ends[None, :]).astype(jnp.int32), axis=1), N_EXPERTS - 1)
    last_e = jnp.max(jnp.where(used, tile_e, 0))
    tile_e = jnp.where(used, tile_e, last_e)
    valid_end = jnp.sum((tile_e[:, None] == jnp.arange(N_EXPERTS)[None, :]) * (starts + counts)[None, :], axis=1)
    n_valid = jnp.where(used, jnp.clip(valid_end - tile_start, 0, tile), 0).astype(jnp.int32)
    return pos.astype(jnp.int32), n_slots, tile_e.astype(jnp.int32), n_valid


def _moe_group_kernel(eid_ref, nval_ref, hs_ref, wg_ref, wu_ref, wd_ref, y_ref, acc_scr, *, n_f):
    w = pl.program_id(0)
    f = pl.program_id(1)
    nv = nval_ref[w]

    def run(n_rows):
        wg = wg_ref[...].astype(BF16)
        wu = wu_ref[...].astype(BF16)
        wd = wd_ref[...].astype(BF16)
        for r0 in range(0, n_rows, MOE_SUB_ROWS):
            rows = slice(r0, r0 + MOE_SUB_ROWS)
            hv = _unpack_pairs(hs_ref[rows, :])
            row = r0 + lax.broadcasted_iota(jnp.int32, hv.shape, 0)
            h = jnp.where(row < nv, hv, 0.0).astype(BF16)
            part = _dot((_silu(_dot(h, wg)) * _dot(h, wu)).astype(BF16), wd)
            acc = jnp.where(f == 0, 0.0, acc_scr[rows, :]) + part
            acc_scr[rows, :] = acc
            y_ref[rows, :] = _pack_pairs(acc)

    for groups in range(1, hs_ref.shape[0] // MOE_SUB_ROWS + 1):
        @pl.when((nv > (groups - 1) * MOE_SUB_ROWS) & (nv <= groups * MOE_SUB_ROWS))
        def _(groups=groups):
            run(groups * MOE_SUB_ROWS)


def _moe_grouped(hs, tile_e, n_valid, wg, wu, wd):
    n_slots = hs.shape[0]
    d = wg.shape[1]
    d_ff = wg.shape[2]
    tile = MOE_ROW_TILE
    tf = MOE_FF_TILE
    n_f = d_ff // tf

    def f_idx(f, nval, w):
        return jnp.where(nval[w] > 0, f, n_f - 1)

    grid_spec = pltpu.PrefetchScalarGridSpec(
        num_scalar_prefetch=2,
        grid=(n_slots // tile, n_f),
        in_specs=[
            pl.BlockSpec((tile, d // 2), lambda w, f, eid, nval: (w, 0)),
            pl.BlockSpec((None, d, tf), lambda w, f, eid, nval: (eid[w], 0, f_idx(f, nval, w))),
            pl.BlockSpec((None, d, tf), lambda w, f, eid, nval: (eid[w], 0, f_idx(f, nval, w))),
            pl.BlockSpec((None, tf, d), lambda w, f, eid, nval: (eid[w], f_idx(f, nval, w), 0)),
        ],
        out_specs=pl.BlockSpec((tile, d // 2), lambda w, f, eid, nval: (w, 0)),
        scratch_shapes=[pltpu.VMEM((tile, d), F32)],
    )
    return pl.pallas_call(
        functools.partial(_moe_group_kernel, n_f=n_f),
        grid_spec=grid_spec,
        out_shape=jax.ShapeDtypeStruct((n_slots, d // 2), jnp.int32),
        compiler_params=_cparams("arbitrary", "arbitrary"),
        name="moe_experts",
    )(tile_e, n_valid, hs, wg, wu, wd)


def _moe_out_kernel(x_ref, y1_ref, y2_ref, meta_ref, mod_ref, g3_ref, o_ref, *, tiles_per_mod, mod_base):
    i = pl.program_id(0)
    _, _, gate_f = _mod_rows(mod_ref, i, tiles_per_mod, mod_base, 3)
    meta = meta_ref[...]
    y = meta[:, 2:3] * _unpack_pairs(y1_ref[...]) + meta[:, 3:4] * _unpack_pairs(y2_ref[...])
    o_ref[...] = x_ref[...] + gate_f * _rms(y, g3_ref[...])


def _moe_combine(x, yg, meta, mod, g3, *, rows_per_mod, mod_base):
    rows, d = x.shape
    tm = math.gcd(1024, rows_per_mod)
    nt = rows // tm
    kern = functools.partial(_moe_out_kernel, tiles_per_mod=max(rows_per_mod // tm, 1), mod_base=mod_base)
    return pl.pallas_call(
        kern,
        grid=(nt,),
        in_specs=[
            pl.BlockSpec((tm, d), lambda i: (i, 0)),
            pl.BlockSpec((tm, d // 2), lambda i: (i, 0)),
            pl.BlockSpec((tm, d // 2), lambda i: (nt + i, 0)),
            pl.BlockSpec((tm, MOE_META_W), lambda i: (i, 0)),
            pl.BlockSpec(mod.shape, lambda i: (0, 0)),
            pl.BlockSpec((1, d), lambda i: (0, 0)),
        ],
        out_specs=pl.BlockSpec((tm, d), lambda i: (i, 0)),
        out_shape=jax.ShapeDtypeStruct((rows, d), F32),
        compiler_params=_cparams("arbitrary"),
        name="moe_combine",
    )(x, yg, yg, meta, mod, g3.reshape(1, d))


def _moe_sparse(x, routed, mod, g3, wg, wu, wd, *, rows_per_mod, mod_base):
    h, meta, plan, counts = routed
    rows = x.shape[0]
    pos, n_slots, tile_e, n_valid = _moe_plan(plan, counts, rows)
    hs = _sc_scatter(h, pos, n_slots)
    ys = _moe_grouped(hs, tile_e, n_valid, wg, wu, wd)
    yg = _sc_gather(ys, pos)
    return _moe_combine(x, yg, meta, mod, g3, rows_per_mod=rows_per_mod, mod_base=mod_base)


def _cast_kernel(w_ref, o_ref, *, scale):
    w = w_ref[...]
    o_ref[...] = (w if scale == 1.0 else w * scale).astype(BF16)


def _cast_bf16(w_stack, layer, scale=1.0):
    squeeze = w_stack.ndim == 3
    w4 = w_stack[:, None] if squeeze else w_stack
    _, n_e, k, n = w4.shape
    bk = min(k, 256)
    out = pl.pallas_call(
        functools.partial(_cast_kernel, scale=scale),
        grid=(n_e, k // bk),
        in_specs=[pl.BlockSpec((None, None, bk, n), lambda e, i: (layer, e, i, 0))],
        out_specs=pl.BlockSpec((None, bk, n), lambda e, i: (e, i, 0)),
        out_shape=jax.ShapeDtypeStruct((n_e, k, n), BF16),
        compiler_params=_cparams("arbitrary", "arbitrary"),
        name="cast_weights",
    )(w4)
    return out[0] if squeeze else out


def _permute_w_in(w_in_stack, layer):
    _, k, n = w_in_stack.shape
    n_blocks = n // BRANCH_W
    shift = 9
    n_gate_blocks = N_BRANCH * D_MODEL // BRANCH_W

    per_step = 5
    assert n_blocks % per_step == 0

    def permute_kernel(*refs):
        o_ref = refs[-1]
        for s, w_ref in enumerate(refs[:-1]):
            scale = jnp.where(pl.program_id(0) * per_step + s < n_gate_blocks, 0.5, 1.0)
            o_ref[:, s * BRANCH_W:(s + 1) * BRANCH_W] = (w_ref[...] * scale).astype(BF16)

    def src(s):
        return pl.BlockSpec((None, k, BRANCH_W), lambda j: (layer, 0, (j * per_step + s + shift) % n_blocks))

    return pl.pallas_call(
        permute_kernel,
        grid=(n_blocks // per_step,),
        in_specs=[src(s) for s in range(per_step)],
        out_specs=pl.BlockSpec((k, per_step * BRANCH_W), lambda j: (0, j)),
        out_shape=jax.ShapeDtypeStruct((k, n), BF16),
        compiler_params=_cparams("arbitrary"),
        name="cast_permute_w_in",
    )(*([w_in_stack] * per_step))


def kernel(x, c, ctx, c_ctx, w_mod, b_mod, norm_g, w_in, s5_a_re, s5_a_im, s5_log_dt, s5_b_re, s5_b_im, s5_c_re, s5_c_im, s5_d, s5_w_glu, s5_b_glu, ret_decay, ret_gn, na_rpb, w_branch, w_out, ffn_w_gate, ffn_w_up, ffn_w_down, moe_w_router, moe_b_router, moe_w_gate, moe_w_up, moe_w_down):
    batch, seq_len, d = x.shape
    ctx_len = ctx.shape[1]
    depth = w_mod.shape[0]
    cond = jnp.concatenate([c, c_ctx[None, :]], axis=0)
    mod_all = _modulation(cond, w_mod, b_mod)
    rope = _rope_tables(seq_len)
    lane_h = np.repeat(np.arange(RET_HEADS), RET_DIM)
    avg = jnp.asarray((lane_h[:, None] == lane_h[None, :]).astype(np.float32) / RET_DIM, BF16)

    xl = x.reshape(batch * seq_len, d)
    xc = ctx.reshape(batch * ctx_len, d)
    lat = dict(rows_per_mod=seq_len, mod_base=0)
    cxt = dict(rows_per_mod=batch * ctx_len, mod_base=batch)

    s5_tabs = jax.vmap(functools.partial(_s5_tables, batch=batch))(
        s5_a_re, s5_a_im, s5_log_dt, s5_b_re, s5_b_im, s5_c_re, s5_c_im, s5_d)
    ret_tabs = jax.vmap(_ret_tables)(ret_decay)
    ret_masks = _ret_masks()
    na_bias = jax.vmap(_na_tables)(na_rpb)
    na_hmask = _na_head_mask()

    for layer in range(depth):
        last = layer == depth - 1
        need_ctx = not last
        mod = mod_all[layer]
        ng = norm_g[layer]
        w_in_bf = _permute_w_in(w_in, layer)
        lw = dict(w_glu=s5_w_glu[layer].astype(BF16), b_glu=s5_b_glu[layer].reshape(1, BRANCH_W).astype(F32),
                  ret_gn=ret_gn[layer].reshape(1, BRANCH_W).astype(F32), avg=avg,
                  w_branch=_cast_bf16(w_branch, layer, 0.5), w_out=_cast_bf16(w_out, layer))

        proj_l, f_l, *s_in_l = _in_proj(xl, mod, ng[0], w_in_bf, **lat)
        proj_c, f_c, *s_in_c = _in_proj(xc, mod, ng[0], w_in_bf, **cxt)

        a_l = _fourier_latent(f_l, batch, seq_len)
        s_l, s_c = _s5_mixer(s_in_l, s_in_c, s5_tabs, layer, batch)
        r_l, r_c = _retention(proj_l, proj_c, ret_tabs, layer, ret_masks, rope, batch, seq_len, ctx_len)
        n_l, n_c = _neighborhood(proj_l, proj_c, na_bias, layer, na_hmask, batch, seq_len, ctx_len, need_ctx)

        xl = _merge(xl, mod, ng[1], proj_l, a_l, s_l, r_l, n_l, lw, **lat)
        if need_ctx:
            a_c = _fourier_ctx(f_c, batch, ctx_len)
            xc = _merge(xc, mod, ng[1], proj_c, a_c, s_c, r_c, n_c, lw, **cxt)

        i = layer // 2
        if layer % 2 == 0:
            wg, wu, wd = _cast_bf16(ffn_w_gate, i), _cast_bf16(ffn_w_up, i), _cast_bf16(ffn_w_down, i)
            xl = _ffn_dense(xl, mod, ng[2], ng[3], wg, wu, wd, **lat)
            if need_ctx:
                xc = _ffn_dense(xc, mod, ng[2], ng[3], wg, wu, wd, **cxt)
        else:
            wg, wu, wd = moe_w_gate[i], moe_w_up[i], moe_w_down[i]
            routed = _router(xl, mod, ng[2], moe_w_router[i], moe_b_router[i], **lat)
            xl = _moe_sparse(xl, routed, mod, ng[3], wg, wu, wd, **lat)
            if need_ctx:
                routed_c = _router(xc, mod, ng[2], moe_w_router[i], moe_b_router[i], **cxt)
                xc = _moe_sparse(xc, routed_c, mod, ng[3], wg, wu, wd, **cxt)
    return xl.reshape(batch, seq_len, d)
```

```python
import functools
import math

import numpy as np
import jax
import jax.numpy as jnp
from jax import lax
from jax.experimental import pallas as pl
from jax.experimental.pallas import tpu as pltpu
from jax.experimental.pallas import tpu_sc as plsc

F32 = jnp.float32
BF16 = jnp.bfloat16

D_MODEL = 1024
BRANCH_W = 256
N_BRANCH = 4
GRID_W = 64
FNET_GROUP_DIM = 64
S5_GROUP_CH = 16
S5_GROUPS = 16
S5_STATE = 64
S5_CHUNK = 32
S5_PAIRS = S5_GROUPS // 2
RET_HEADS = 4
RET_DIM = 64
RET_CHUNK = 128
NA_HEADS = 4
NA_DIM = 64
NA_WIN_ROWS = 8
NA_WIN_COLS = 16
NA_QROWS = 16
ROPE_BASE = 10000.0
N_EXPERTS = 8
EPS = 1e-6
FFT_N2 = 256
NEG_BIG = -1e30
VMEM_LIMIT_BYTES = 50 * 1024 * 1024
SC_CORES = 2
SC_SUBCORES = 16
SC_WORKERS = SC_CORES * SC_SUBCORES
SC_GATHER_ROWS = 64
MOE_ROW_TILE = 2048
MOE_SUB_ROWS = 512
MOE_FF_TILE = 512
MOE_META_W = 8

COL_F, COL_S, COL_RQ, COL_RK, COL_RV, COL_RG, COL_NQ, COL_NK, COL_NV = range(16, 25)
IN_W = 9 * BRANCH_W + N_BRANCH * D_MODEL
IN_TN = 1280
IN_F_TILE = (N_BRANCH * D_MODEL) // IN_TN
IN_F_OFF = N_BRANCH * D_MODEL - IN_F_TILE * IN_TN
IN_S_OFF = IN_F_OFF + BRANCH_W


def _cparams(*sem):
    return pltpu.CompilerParams(dimension_semantics=sem, vmem_limit_bytes=VMEM_LIMIT_BYTES)


def _sigmoid(v):
    return 0.5 * jnp.tanh(0.5 * v) + 0.5


def _silu(v):
    return v * _sigmoid(v)


def _gelu_tanh(v):
    return 0.5 * v * (1.0 + jnp.tanh(math.sqrt(2.0 / math.pi) * (v + 0.044715 * (v * v * v))))


def _rms(v, g):
    ms = jnp.mean(v * v, axis=-1, keepdims=True)
    return v * lax.rsqrt(ms + EPS) * g


def _split_bf16(v):
    hi = v.astype(BF16)
    lo = (v - hi.astype(F32)).astype(BF16)
    return hi, lo


def _pack_pairs(v):
    n = v.shape[1] // 2
    lo = lax.bitcast_convert_type(v[:, :n].astype(BF16).astype(F32), jnp.int32)
    hi = lax.bitcast_convert_type(v[:, n:].astype(BF16).astype(F32), jnp.int32)
    return (hi & -65536) | ((lo >> 16) & 65535)


def _unpack_pairs(w):
    lo = lax.bitcast_convert_type(w << 16, F32)
    hi = lax.bitcast_convert_type(w & -65536, F32)
    return jnp.concatenate([lo, hi], axis=-1)


def _dot(a, b):
    return jnp.dot(a, b, preferred_element_type=F32)


def _dot_nt(a, b):
    return lax.dot_general(a, b, (((1,), (1,)), ((), ())), preferred_element_type=F32)


def _dot_tn(a, b):
    return lax.dot_general(a, b, (((0,), (0,)), ((), ())), preferred_element_type=F32)


def _mod_kernel(ct_ref, w_ref, b_ref, o_ref, *, n_cond):
    ct = ct_ref[...]
    s = _silu(ct)
    w = w_ref[...]
    rows = [jnp.sum(w * s[:, r:r + 1], axis=0, keepdims=True) for r in range(n_cond)]
    rows.append(jnp.zeros((8 - n_cond, w.shape[1]), F32))
    o_ref[...] = jnp.concatenate(rows, axis=0) + b_ref[...]


def _modulation(cond, w_mod, b_mod):
    n_layers, d, n = w_mod.shape
    tn = 1536
    ct = jnp.zeros((8, d), F32).at[:cond.shape[0]].set(cond).T
    return pl.pallas_call(
        functools.partial(_mod_kernel, n_cond=cond.shape[0]),
        grid=(n_layers, n // tn),
        in_specs=[
            pl.BlockSpec((d, 8), lambda l, j: (0, 0)),
            pl.BlockSpec((None, d, tn), lambda l, j: (l, 0, j)),
            pl.BlockSpec((None, 1, tn), lambda l, j: (l, 0, j)),
        ],
        out_specs=pl.BlockSpec((None, 8, tn), lambda l, j: (l, 0, j)),
        out_shape=jax.ShapeDtypeStruct((n_layers, 8, n), F32),
        compiler_params=_cparams("arbitrary", "arbitrary"),
        name="adaln_mod",
    )(ct, w_mod, b_mod.reshape(n_layers, 1, n))


def _mod_rows(mod_ref, i, tiles_per_mod, mod_base, first):
    r = mod_base + i // tiles_per_mod
    return [mod_ref[pl.ds(r, 1), (first + k) * D_MODEL:(first + k + 1) * D_MODEL] for k in range(3)]


def _in_kernel(x_ref, mod_ref, g_ref, w_ref, proj_ref, f_ref, sa_ref, sb_ref, *, tiles_per_mod, mod_base):
    i = pl.program_id(0)
    sh, sc, _ = _mod_rows(mod_ref, i, tiles_per_mod, mod_base, 0)
    h = (_rms(x_ref[...], g_ref[...]) * (1.0 + sc) + sh).astype(BF16)
    for j in range(IN_W // IN_TN):
        res = _dot(h, w_ref[:, j * IN_TN:(j + 1) * IN_TN])
        proj_ref[:, j * IN_TN:(j + 1) * IN_TN] = res.astype(BF16)
        if j == IN_F_TILE:
            f_ref[...] = res[:, IN_F_OFF:IN_F_OFF + BRANCH_W].astype(BF16)
            sa_ref[...] = res[:, IN_S_OFF:IN_S_OFF + 128]
            sb_ref[...] = res[:, IN_S_OFF + 128:IN_S_OFF + 256]


def _in_proj(x, mod, g, w_bf, *, rows_per_mod, mod_base):
    rows, d = x.shape
    tm = math.gcd(512, rows_per_mod)
    kern = functools.partial(_in_kernel, tiles_per_mod=max(rows_per_mod // tm, 1), mod_base=mod_base)
    return pl.pallas_call(
        kern,
        grid=(rows // tm,),
        in_specs=[
            pl.BlockSpec((tm, d), lambda i: (i, 0)),
            pl.BlockSpec(mod.shape, lambda i: (0, 0)),
            pl.BlockSpec((1, d), lambda i: (0, 0)),
            pl.BlockSpec((d, IN_W), lambda i: (0, 0), pipeline_mode=pl.Buffered(1)),
        ],
        out_specs=[
            pl.BlockSpec((tm, IN_W), lambda i: (i, 0)),
            pl.BlockSpec((tm, BRANCH_W), lambda i: (i, 0)),
            pl.BlockSpec((tm, 128), lambda i: (i, 0)),
            pl.BlockSpec((tm, 128), lambda i: (i, 0)),
        ],
        out_shape=[
            jax.ShapeDtypeStruct((rows, IN_W), BF16),
            jax.ShapeDtypeStruct((rows, BRANCH_W), BF16),
            jax.ShapeDtypeStruct((rows, 128), F32),
            jax.ShapeDtypeStruct((rows, 128), F32),
        ],
        compiler_params=_cparams("arbitrary"),
        name="in_proj",
    )(x, mod, g.reshape(1, d), w_bf)


def _fft_a_kernel(x_ref, cs_ref, tc_ref, ts_ref, zr_ref, zi_ref, *, n1, n1p):
    y = _dot(cs_ref[...].astype(BF16), x_ref[...])
    yr = y[:n1]
    yi = y[n1p:n1p + n1]
    tc = tc_ref[...]
    ts = ts_ref[...]
    zr_ref[...] = (yr * tc + yi * ts).astype(BF16)
    zi_ref[...] = (yi * tc - yr * ts).astype(BF16)


def _fft_b_kernel(zr_ref, zi_ref, cs_ref, cc_ref, sc_ref, oa_ref, ob_ref, *, kb, n1, scale, has_imag):
    cs = cs_ref[...].astype(BF16)
    cc = cc_ref[...].astype(BF16)
    sc = sc_ref[...].astype(BF16)
    half = BRANCH_W // 2
    for kk in range(kb):
        a = _dot(cs, zr_ref[kk])
        if has_imag:
            b = _dot(cs, zi_ref[kk])
            xr = a[:FFT_N2] + b[FFT_N2:]
            xi = b[:FFT_N2] - a[FFT_N2:]
        else:
            xr = a[:FFT_N2]
            xi = -a[FFT_N2:]
        out = (_dot(xr.astype(BF16), cc) + _dot(xi.astype(BF16), sc)) * scale
        k1 = pl.program_id(1) * kb + kk
        oa_ref[pl.ds(k1, FFT_N2, stride=n1), :] = out[:, :half]
        ob_ref[pl.ds(k1, FFT_N2, stride=n1), :] = out[:, half:]


def _dft_tables(n):
    k = np.arange(n)
    ang = 2.0 * np.pi * ((k[:, None] * k[None, :]) % n) / n
    return np.cos(ang), np.sin(ang)


def _fft_b_call(zr, zi, n1, batch, seq_len, has_imag):
    c2, s2 = _dft_tables(FFT_N2)
    cs2 = jnp.asarray(np.concatenate([c2, s2], axis=0), F32)
    c64, s64 = _dft_tables(FNET_GROUP_DIM)
    eye = np.eye(BRANCH_W // FNET_GROUP_DIM)
    cc = jnp.asarray(np.kron(eye, c64), F32)
    sc = jnp.asarray(np.kron(eye, s64), F32)
    kb = min(8, n1)
    scale = 1.0 / math.sqrt(seq_len * FNET_GROUP_DIM)
    kern = functools.partial(_fft_b_kernel, kb=kb, n1=n1, scale=scale, has_imag=has_imag)
    zspec = pl.BlockSpec((None, kb, FFT_N2, BRANCH_W), lambda b, i: (b, i, 0, 0))
    half = pl.BlockSpec((seq_len, BRANCH_W // 2), lambda b, i: (b, 0))
    return pl.pallas_call(
        kern,
        grid=(batch, n1 // kb),
        in_specs=[
            zspec, zspec,
            pl.BlockSpec((2 * FFT_N2, FFT_N2), lambda b, i: (0, 0)),
            pl.BlockSpec((BRANCH_W, BRANCH_W), lambda b, i: (0, 0)),
            pl.BlockSpec((BRANCH_W, BRANCH_W), lambda b, i: (0, 0)),
        ],
        out_specs=[half, half],
        out_shape=[jax.ShapeDtypeStruct((batch * seq_len, BRANCH_W // 2), F32)] * 2,
        compiler_params=_cparams("arbitrary", "arbitrary"),
        name="fourier_stage_b",
    )(zr, zi, cs2, cc, sc)


def _fourier_latent(f, batch, seq_len):
    n1 = seq_len // FFT_N2
    wide = FFT_N2 * BRANCH_W
    c1, s1 = _dft_tables(n1)
    n1p = max(n1, 8)
    cs1 = np.zeros((2 * n1p, n1))
    cs1[:n1] = c1
    cs1[n1p:n1p + n1] = -s1
    k1 = np.arange(n1)[:, None]
    l2 = np.arange(FFT_N2)[None, :]
    tw = 2.0 * np.pi * (k1 * l2) / seq_len
    tc = jnp.asarray(np.repeat(np.cos(tw), BRANCH_W, axis=1), F32)
    ts = jnp.asarray(np.repeat(np.sin(tw), BRANCH_W, axis=1), F32)
    cw = min(8192, wide)
    xv = f.reshape(batch, n1, wide)
    spec = pl.BlockSpec((None, n1, cw), lambda b, j: (b, 0, j))
    tspec = pl.BlockSpec((n1, cw), lambda b, j: (0, j))
    zr, zi = pl.pallas_call(
        functools.partial(_fft_a_kernel, n1=n1, n1p=n1p),
        grid=(batch, wide // cw),
        in_specs=[spec, pl.BlockSpec((2 * n1p, n1), lambda b, j: (0, 0)), tspec, tspec],
        out_specs=[spec, spec],
        out_shape=[jax.ShapeDtypeStruct((batch, n1, wide), BF16)] * 2,
        compiler_params=_cparams("arbitrary", "arbitrary"),
        name="fourier_stage_a",
    )(xv, jnp.asarray(cs1, F32), tc, ts)
    zr = zr.reshape(batch, n1, FFT_N2, BRANCH_W)
    zi = zi.reshape(batch, n1, FFT_N2, BRANCH_W)
    return _fft_b_call(zr, zi, n1, batch, seq_len, True)


def _fourier_ctx(f, batch, ctx_len):
    assert ctx_len == FFT_N2
    z = f.reshape(batch, 1, FFT_N2, BRANCH_W)
    return _fft_b_call(z, z, 1, batch, ctx_len, False)


def _s5_tables(a_re, a_im, log_dt, b_re, b_im, c_re, c_im, d_skip, batch):
    t = S5_CHUNK
    g, p, hc = S5_GROUPS, S5_STATE, S5_GROUP_CH
    lam = lax.complex(a_re.astype(F32), a_im.astype(F32))
    dt = jnp.exp(log_dt.astype(F32))[..., None]
    ks = jnp.arange(t + 1, dtype=F32)
    apow = jnp.exp((lam * dt)[..., None] * ks)
    a_bar = apow[..., 1]
    b_bar = ((a_bar - 1.0) / lam)[..., None] * lax.complex(b_re.astype(F32), b_im.astype(F32))
    cm = lax.complex(c_re.astype(F32), c_im.astype(F32))
    lagv = jnp.arange(-(t - 1), t, dtype=F32)
    ldt = lam * dt
    pw_f = jnp.where(lagv >= 0, jnp.exp(ldt[0][..., None] * jnp.maximum(lagv, 0.0)), 0.0)
    pw_b = jnp.where(lagv <= 0, jnp.exp(ldt[1][..., None] * jnp.maximum(-lagv, 0.0)), 0.0)
    kfull = jnp.real(jnp.einsum('ghp,gpl,gpj->gjlh',
                                jnp.concatenate([cm[0], cm[1]], axis=-1),
                                jnp.concatenate([pw_f, pw_b], axis=1),
                                jnp.concatenate([b_bar[0], b_bar[1]], axis=1),
                                precision=lax.Precision.HIGHEST))
    kp = kfull.reshape(S5_PAIRS, 2, hc, 2 * t - 1, hc)
    blk = [kp[:, gi] for gi in range(2)]
    zb = jnp.zeros_like(blk[0])
    strip = jnp.concatenate([jnp.stack([blk[0], zb], axis=3), jnp.stack([zb, blk[1]], axis=3)], axis=1)
    strip = strip.reshape(S5_PAIRS, 2 * hc, (2 * t - 1) * 2 * hc)
    strip = jnp.pad(strip, ((0, 0), (0, 0), (0, 2 * hc)))

    wf = jnp.einsum('gpj,gph->gjhp', apow[0][..., t - 1::-1][..., :t], b_bar[0])
    wb = jnp.einsum('gpj,gph->gjhp', apow[1][..., :t], b_bar[1])
    kinds = [jnp.real(wf), jnp.imag(wf), jnp.real(wb), jnp.imag(wb)]

    def we_pair(kd):
        k5 = kd.reshape(S5_PAIRS, 2, t, hc, p)
        z = jnp.zeros_like(k5[:, 0])
        rows = jnp.stack([jnp.concatenate([k5[:, 0], z], axis=-1), jnp.concatenate([z, k5[:, 1]], axis=-1)], axis=2)
        return rows.reshape(S5_PAIRS, 2 * t * hc, 2 * p)

    we = jnp.concatenate([we_pair(kd) for kd in kinds], axis=-1).astype(BF16)

    vf = jnp.einsum('ghp,gpt->gpth', cm[0], apow[0][..., 1:t + 1])
    vb = jnp.einsum('ghp,gpt->gpth', cm[1], apow[1][..., t:0:-1])
    vkinds = [jnp.real(vf), -jnp.imag(vf), jnp.real(vb), -jnp.imag(vb)]

    def v_pair(kd):
        k5 = kd.reshape(S5_PAIRS, 2, p, t, hc)
        z = jnp.zeros_like(k5[:, 0])
        rows = jnp.concatenate([jnp.stack([k5[:, 0], z], axis=3), jnp.stack([z, k5[:, 1]], axis=3)], axis=1)
        return rows.reshape(S5_PAIRS, 2 * p, 2 * t * hc)

    v1 = jnp.concatenate([v_pair(kd) for kd in vkinds], axis=1)
    v = jnp.concatenate([v1, v1], axis=1).astype(BF16)

    def lanes(z):
        return jnp.tile(z.reshape(1, g * p), (1, batch))

    at = apow[..., t]
    a_tab = jnp.concatenate([lanes(jnp.real(at[0])), lanes(jnp.imag(at[0])),
                             lanes(jnp.real(at[1])), lanes(jnp.imag(at[1]))], axis=0)
    dvec = jnp.tile(d_skip.astype(F32).reshape(S5_PAIRS, 1, 2 * hc), (1, t, 1)).reshape(S5_PAIRS, 1, 2 * t * hc)
    return dict(strip=strip, we=we, v=v, a_tab=a_tab, dvec=dvec)


def _s5_pack_kernel(xa_ref, xb_ref, u_ref, *, n_chunks):
    per_half = S5_PAIRS // 2
    for half, x_ref in enumerate((xa_ref, xb_ref)):
        rows = [x_ref[pl.ds(tau, n_chunks, stride=S5_CHUNK), :] for tau in range(S5_CHUNK)]
        for qq in range(per_half):
            pieces = [r[:, qq * 32:(qq + 1) * 32] for r in rows]
            u_ref[half * per_half + qq] = jnp.concatenate(pieces, axis=-1).astype(BF16)


def _s5_unpack_kernel(y_ref, oa_ref, ob_ref, *, n_chunks):
    per_half = S5_PAIRS // 2
    for half, o_ref in enumerate((oa_ref, ob_ref)):
        ys = [y_ref[half * per_half + qq].astype(F32) for qq in range(per_half)]
        for t in range(S5_CHUNK):
            pieces = [y[:, t * 32:(t + 1) * 32] for y in ys]
            o_ref[pl.ds(t, n_chunks, stride=S5_CHUNK), :] = jnp.concatenate(pieces, axis=-1)


def _s5_pack(sa, sb, batch):
    n_chunks = sa.shape[0] // batch // S5_CHUNK
    rows = n_chunks * S5_CHUNK
    cols = 2 * S5_CHUNK * S5_GROUP_CH
    half = pl.BlockSpec((rows, 128), lambda b: (b, 0))
    return pl.pallas_call(
        functools.partial(_s5_pack_kernel, n_chunks=n_chunks),
        grid=(batch,),
        in_specs=[half, half],
        out_specs=pl.BlockSpec((S5_PAIRS, None, n_chunks, cols), lambda b: (0, b, 0, 0)),
        out_shape=jax.ShapeDtypeStruct((S5_PAIRS, batch, n_chunks, cols), BF16),
        compiler_params=_cparams("arbitrary"),
        name="s5_pack",
    )(sa, sb)


def _s5_unpack(y, batch):
    n_chunks = y.shape[2]
    rows = n_chunks * S5_CHUNK
    cols = y.shape[3]
    half = pl.BlockSpec((rows, 128), lambda b: (b, 0))
    return pl.pallas_call(
        functools.partial(_s5_unpack_kernel, n_chunks=n_chunks),
        grid=(batch,),
        in_specs=[pl.BlockSpec((S5_PAIRS, None, n_chunks, cols), lambda b: (0, b, 0, 0))],
        out_specs=[half, half],
        out_shape=[jax.ShapeDtypeStruct((batch * rows, 128), F32)] * 2,
        compiler_params=_cparams("arbitrary"),
        name="s5_unpack",
    )(y)


def _s5_e_kernel(ul_ref, uc_ref, we_ref, ref_, imf_, reb_, imb_):
    u = jnp.concatenate([ul_ref[...], uc_ref[...]], axis=0)
    e = _dot(u, we_ref[...])
    ref_[...] = e[:, 0:128]
    imf_[...] = e[:, 128:256]
    reb_[...] = e[:, 256:384]
    imb_[...] = e[:, 384:512]


def _s5_scan_kernel(a_ref, ref_, imf_, reb_, imb_, prf, pif, prb, pib, *, n_rows, n_ctx):
    afr = a_ref[0:1, :]
    afi = a_ref[1:2, :]
    abr = a_ref[2:3, :]
    abi = a_ref[3:4, :]
    zero = jnp.zeros_like(afr)

    n_lat = n_rows - n_ctx

    def body(s, carry):
        sfr, sfi, sbr, sbi = carry
        nf = jnp.where(s < n_ctx, n_lat + s, s - n_ctx)
        nb = n_rows - 1 - s
        prf[pl.ds(nf, 1), :] = sfr
        pif[pl.ds(nf, 1), :] = sfi
        prb[pl.ds(nb, 1), :] = sbr
        pib[pl.ds(nb, 1), :] = sbi
        efr = ref_[pl.ds(nf, 1), :]
        efi = imf_[pl.ds(nf, 1), :]
        ebr = reb_[pl.ds(nb, 1), :]
        ebi = imb_[pl.ds(nb, 1), :]
        nfr = afr * sfr - afi * sfi + efr
        nfi = afr * sfi + afi * sfr + efi
        nbr = abr * sbr - abi * sbi + ebr
        nbi = abr * sbi + abi * sbr + ebi
        return nfr, nfi, nbr, nbi

    lax.fori_loop(0, n_rows, body, (zero, zero, zero, zero))


def _s5_y_kernel(ul_ref, uc_ref, strip_ref, v_ref, d_ref, prf, pif, prb, pib, yl_ref, yc_ref, m_scr):
    width = 2 * S5_GROUP_CH
    cols = S5_CHUNK * width
    n_lat = yl_ref.shape[0]

    @pl.when(pl.program_id(1) == 0)
    def _():
        strip = strip_ref[...]
        for j in range(S5_CHUNK):
            off = (S5_CHUNK - 1 - j) * width
            win = strip if off == 0 else pltpu.roll(strip, 2 * cols - off, axis=1)
            m_scr[j * width:(j + 1) * width, :] = win[:, :cols].astype(BF16)

    u = jnp.concatenate([ul_ref[...], uc_ref[...]], axis=0)
    y_intra = _dot(u, m_scr[...])
    pcat = jnp.concatenate([prf[...], pif[...], prb[...], pib[...]], axis=-1)
    hi, lo = _split_bf16(pcat)
    y_cross = _dot(jnp.concatenate([hi, lo], axis=-1), v_ref[...])
    y = y_intra + y_cross + d_ref[...] * u.astype(F32)
    yl_ref[...] = y[:n_lat].astype(BF16)
    yc_ref[...] = y[n_lat:].astype(BF16)


def _s5_core(ul, uc, tabs, layer, batch):
    n_lat, n_ctx = ul.shape[2], uc.shape[2]
    n_rows = n_lat + n_ctx
    width = batch * S5_PAIRS * 128
    cols = 2 * S5_CHUNK * S5_GROUP_CH
    ul_spec = pl.BlockSpec((None, None, n_lat, cols), lambda q, b: (q, b, 0, 0))
    uc_spec = pl.BlockSpec((None, None, n_ctx, cols), lambda q, b: (q, b, 0, 0))
    st_spec = pl.BlockSpec((n_rows, 128), lambda q, b: (0, b * S5_PAIRS + q))
    st_shape = jax.ShapeDtypeStruct((n_rows, width), F32)
    e4 = pl.pallas_call(
        _s5_e_kernel,
        grid=(S5_PAIRS, batch),
        in_specs=[ul_spec, uc_spec, pl.BlockSpec((None, None, cols, 512), lambda q, b: (layer, q, 0, 0))],
        out_specs=[st_spec] * 4,
        out_shape=[st_shape] * 4,
        compiler_params=_cparams("arbitrary", "arbitrary"),
        name="s5_chunk_states",
    )(ul, uc, tabs['we'])
    p4 = pl.pallas_call(
        functools.partial(_s5_scan_kernel, n_rows=n_rows, n_ctx=n_ctx),
        out_shape=[st_shape] * 4,
        compiler_params=pltpu.CompilerParams(vmem_limit_bytes=VMEM_LIMIT_BYTES),
        name="s5_state_scan",
    )(tabs['a_tab'][layer], *e4)
    y = pl.pallas_call(
        _s5_y_kernel,
        grid=(S5_PAIRS, batch),
        in_specs=[
            ul_spec, uc_spec,
            pl.BlockSpec((None, None, 2 * S5_GROUP_CH, 2 * cols), lambda q, b: (layer, q, 0, 0)),
            pl.BlockSpec((None, None, cols, cols), lambda q, b: (layer, q, 0, 0)),
            pl.BlockSpec((None, None, 1, cols), lambda q, b: (layer, q, 0, 0)),
            st_spec, st_spec, st_spec, st_spec,
        ],
        out_specs=[ul_spec, uc_spec],
        out_shape=[
            jax.ShapeDtypeStruct((S5_PAIRS, batch, n_lat, cols), BF16),
            jax.ShapeDtypeStruct((S5_PAIRS, batch, n_ctx, cols), BF16),
        ],
        scratch_shapes=[pltpu.VMEM((cols, cols), BF16)],
        compiler_params=_cparams("arbitrary", "arbitrary"),
        name="s5_outputs",
    )(ul, uc, tabs['strip'], tabs['v'], tabs['dvec'], *p4)
    return y


def _s5_mixer(s_lat, s_ctx, tabs, layer, batch):
    ul = _s5_pack(*s_lat, batch)
    uc = _s5_pack(*s_ctx, batch)
    yl, yc = _s5_core(ul, uc, tabs, layer, batch)
    return _s5_unpack(yl, batch), _s5_unpack(yc, batch)


def _ret_tables(ret_decay):
    c = RET_CHUNK
    lg = jax.nn.log_sigmoid(ret_decay.astype(F32))
    lane_h = np.repeat(np.arange(RET_HEADS), RET_DIM)
    lgl = jnp.repeat(lg, RET_DIM, axis=1)
    pos = jnp.arange(c, dtype=F32)[:, None]
    qd = jnp.stack([jnp.exp((pos + 1.0) * lgl[0][None]), jnp.exp((c - pos) * lgl[1][None])])
    kd = jnp.stack([jnp.exp((c - 1.0 - pos) * lgl[0][None]), jnp.exp(pos * lgl[1][None])])
    bmask = jnp.asarray((lane_h[:, None] == lane_h[None, :]).astype(np.float32))
    cd = jnp.exp(c * lgl)[:, :, None] * bmask[None]
    diff = pos - pos.T
    dm = []
    for h in range(RET_HEADS):
        fw = jnp.where(diff >= 0, jnp.exp(jnp.maximum(diff, 0.0) * lg[0, h]), 0.0)
        bw = jnp.where(diff <= 0, jnp.exp(jnp.maximum(-diff, 0.0) * lg[1, h]), 0.0)
        dm.append(fw + bw)
    dm = jnp.concatenate(dm, axis=0)
    return dict(qd=qd, kd=kd, cd=cd, dm=dm)


def _ret_masks():
    lane_h = np.repeat(np.arange(RET_HEADS), RET_DIM)
    bmask = (lane_h[:, None] == lane_h[None, :]).astype(np.float32)
    hmask = (np.arange(RET_HEADS)[:, None] == lane_h[None, :]).astype(np.float32)
    return jnp.asarray(bmask), jnp.asarray(hmask)


def _rope_tables(n_tokens):
    t = np.arange(n_tokens)
    row = (t // GRID_W).astype(np.float64)
    col = (t % GRID_W).astype(np.float64)
    n_freq = RET_DIM // 4
    inv_freq = 1.0 / (ROPE_BASE ** (np.arange(n_freq, dtype=np.float64) / n_freq))
    ang = np.concatenate([row[:, None] * inv_freq, col[:, None] * inv_freq], axis=-1)
    cos = np.cos(ang)
    sin = np.sin(ang)
    cos_t = np.tile(np.concatenate([cos, cos], axis=-1), (1, RET_HEADS))
    sin_t = np.tile(np.concatenate([-sin, sin], axis=-1), (1, RET_HEADS))
    half = RET_DIM // 2
    perm = np.arange(BRANCH_W) ^ half
    swap = np.zeros((BRANCH_W, BRANCH_W), np.float32)
    swap[perm, np.arange(BRANCH_W)] = 1.0
    return jnp.asarray(cos_t, F32), jnp.asarray(sin_t, F32), jnp.asarray(swap, BF16)


def _ret_chunk(q, k, v, s, qd, kd, cd, bmask, dm, hmask, with_intra):
    cross = _dot((q * qd).astype(BF16), s.astype(BF16))
    s_new = cd * s + bmask * _dot_tn((k * kd).astype(BF16), v)
    if not with_intra:
        return cross, s_new
    qb = q.astype(BF16)
    kb = k.astype(BF16)
    qs = jnp.concatenate([qb * hmask[h:h + 1].astype(BF16) for h in range(RET_HEADS)], axis=0)
    scores = _dot_nt(qs, kb) * dm
    ov = _dot(scores.astype(BF16), v)
    c = q.shape[0]
    inner = ov[0:c] * hmask[0:1]
    for h in range(1, RET_HEADS):
        inner = inner + ov[h * c:(h + 1) * c] * hmask[h:h + 1]
    return inner + cross, s_new


def _ret_kernel(qf_ref, kf_ref, vf_ref, qb_ref, kb_ref, vb_ref, qc_ref, kc_ref, vc_ref,
                cosf_ref, sinf_ref, cosb_ref, sinb_ref, swap_ref,
                qd_ref, kd_ref, cd_ref, bm_ref, dm_ref, hm_ref,
                of_ref, ob_ref, ocf_ref, ocb_ref, sf_scr, sb_scr, *, n_chunks, n_ctx_chunks):
    i = pl.program_id(1)
    c = RET_CHUNK
    k_scale = RET_DIM ** -0.5
    bmask = bm_ref[...]
    dm = dm_ref[...]
    hmask = hm_ref[...]
    tabs = [(qd_ref[d], kd_ref[d], cd_ref[d]) for d in range(2)]

    @pl.when(i == 0)
    def _():
        for d, oc_ref, s_scr in ((0, ocf_ref, sf_scr), (1, ocb_ref, sb_scr)):
            qd, kd, cd = tabs[d]
            s = jnp.zeros((BRANCH_W, BRANCH_W), F32)
            order = range(n_ctx_chunks) if d == 0 else range(n_ctx_chunks - 1, -1, -1)
            for cc in order:
                sl = slice(cc * c, (cc + 1) * c)
                o, s = _ret_chunk(qc_ref[sl, :].astype(F32), kc_ref[sl, :].astype(F32) * k_scale, vc_ref[sl, :],
                                  s, qd, kd, cd, bmask, dm, hmask, d == 0)
                oc_ref[sl, :] = o
            s_scr[...] = s

    swap = swap_ref[...]

    def rope(x_ref, cos_ref, sin_ref):
        xb = x_ref[...]
        return xb.astype(F32) * cos_ref[...] + _dot(xb, swap) * sin_ref[...]

    q_f = rope(qf_ref, cosf_ref, sinf_ref)
    k_f = rope(kf_ref, cosf_ref, sinf_ref) * k_scale
    q_b = rope(qb_ref, cosb_ref, sinb_ref)
    k_b = rope(kb_ref, cosb_ref, sinb_ref) * k_scale
    sf = sf_scr[...]
    sb = sb_scr[...]
    for step in range(n_chunks):
        sl = slice(step * c, (step + 1) * c)
        o, sf = _ret_chunk(q_f[sl], k_f[sl], vf_ref[sl, :], sf, *tabs[0], bmask, dm, hmask, True)
        of_ref[sl, :] = o
        cb = n_chunks - 1 - step
        sl = slice(cb * c, (cb + 1) * c)
        o, sb = _ret_chunk(q_b[sl], k_b[sl], vb_ref[sl, :], sb, *tabs[1], bmask, dm, hmask, False)
        ob_ref[sl, :] = o
    sf_scr[...] = sf
    sb_scr[...] = sb


def _retention(proj_l, proj_c, tabs, layer, masks, rope, batch, seq_len, ctx_len):
    n_chunks = 8
    blk = n_chunks * RET_CHUNK
    nblk = seq_len // blk
    cos_t, sin_t, swap = rope

    def lat(col, back):
        if back:
            return pl.BlockSpec((blk, BRANCH_W), lambda b, i: (b * nblk + nblk - 1 - i, col))
        return pl.BlockSpec((blk, BRANCH_W), lambda b, i: (b * nblk + i, col))

    def ctx(col):
        return pl.BlockSpec((ctx_len, BRANCH_W), lambda b, i: (b, col))

    def const(shape):
        return pl.BlockSpec(shape, lambda b, i: (0,) * len(shape))

    def per_layer(shape):
        return pl.BlockSpec((None,) + shape, lambda b, i: (layer,) + (0,) * len(shape))

    tab_f = pl.BlockSpec((blk, BRANCH_W), lambda b, i: (i, 0))
    tab_b = pl.BlockSpec((blk, BRANCH_W), lambda b, i: (nblk - 1 - i, 0))
    kern = functools.partial(_ret_kernel, n_chunks=n_chunks, n_ctx_chunks=ctx_len // RET_CHUNK)
    c = RET_CHUNK
    ctx_out = pl.BlockSpec((ctx_len, BRANCH_W), lambda b, i: (b, 0))
    o_f, o_b, oc_f, oc_b = pl.pallas_call(
        kern,
        grid=(batch, nblk),
        in_specs=[
            lat(COL_RQ, False), lat(COL_RK, False), lat(COL_RV, False),
            lat(COL_RQ, True), lat(COL_RK, True), lat(COL_RV, True),
            ctx(COL_RQ), ctx(COL_RK), ctx(COL_RV),
            tab_f, tab_f, tab_b, tab_b, const((BRANCH_W, BRANCH_W)),
            per_layer((2, c, BRANCH_W)), per_layer((2, c, BRANCH_W)), per_layer((2, BRANCH_W, BRANCH_W)),
            const((BRANCH_W, BRANCH_W)), per_layer((RET_HEADS * c, c)), const((RET_HEADS, BRANCH_W)),
        ],
        out_specs=[lat(0, False), lat(0, True), ctx_out, ctx_out],
        out_shape=[
            jax.ShapeDtypeStruct((batch * seq_len, BRANCH_W), F32),
            jax.ShapeDtypeStruct((batch * seq_len, BRANCH_W), F32),
            jax.ShapeDtypeStruct((batch * ctx_len, BRANCH_W), F32),
            jax.ShapeDtypeStruct((batch * ctx_len, BRANCH_W), F32),
        ],
        scratch_shapes=[pltpu.VMEM((BRANCH_W, BRANCH_W), F32), pltpu.VMEM((BRANCH_W, BRANCH_W), F32)],
        compiler_params=_cparams("arbitrary", "arbitrary"),
        name="retention",
    )(proj_l, proj_l, proj_l, proj_l, proj_l, proj_l, proj_c, proj_c, proj_c,
      cos_t, sin_t, cos_t, sin_t, swap,
      tabs['qd'], tabs['kd'], tabs['cd'], masks[0], tabs['dm'], masks[1])
    return (o_f, o_b), (oc_f, oc_b)


def _na_tables(rpb):
    kr, kw = NA_WIN_ROWS, NA_WIN_COLS
    col = np.arange(GRID_W)
    col_start = np.clip(col - kw // 2, 0, GRID_W - kw)
    in_win = (col[None, :] >= col_start[:, None]) & (col[None, :] < col_start[:, None] + kw)
    dc = np.clip(col[None, :] - col[:, None], -(kw - 1), kw - 1) + (kw - 1)
    pick_c = (dc[:, :, None] == np.arange(2 * kw - 1)[None, None, :]).astype(np.float32)
    by = jnp.einsum('hrc,qkc->hqrk', rpb.astype(F32), jnp.asarray(pick_c), precision=lax.Precision.HIGHEST)
    by = jnp.where(jnp.asarray(in_win)[None, :, None, :], by, NEG_BIG)
    bias = jnp.stack([by[:, :, v:v + kr, :] for v in range(kr)], axis=0)
    return bias.reshape(kr, NA_HEADS * GRID_W, kr * GRID_W)


def _na_head_mask():
    lane_h = np.repeat(np.arange(NA_HEADS), NA_DIM)
    hmask = (np.arange(NA_HEADS)[:, None] == lane_h[None, :]).astype(np.float32)
    return jnp.asarray(hmask, F32)


def _attend(qs, keys, vals, bias, kc, vc):
    s_ctx = _dot_nt(qs, kc)
    m = jnp.max(s_ctx, axis=-1, keepdims=True)
    if keys is not None:
        s_band = _dot_nt(qs, keys) + bias
        m = jnp.maximum(m, jnp.max(s_band, axis=-1, keepdims=True))
        p_band = jnp.exp(s_band - m)
    p_ctx = jnp.exp(s_ctx - m)
    l = jnp.sum(p_ctx, axis=-1, keepdims=True)
    o = _dot(p_ctx.astype(BF16), vc)
    if keys is not None:
        l = l + jnp.sum(p_band, axis=-1, keepdims=True)
        o = o + _dot(p_band.astype(BF16), vals)
    return o / l


def _stack_heads(q, hmask_scaled):
    return jnp.concatenate([q * hmask_scaled[h:h + 1] for h in range(NA_HEADS)], axis=0)


def _unstack_heads(o, hmask, n):
    out = o[0:n] * hmask[0:1]
    for h in range(1, NA_HEADS):
        out = out + o[h * n:(h + 1) * n] * hmask[h:h + 1]
    return out


def _na_kernel(q_ref, k_ref, v_ref, kc_ref, vc_ref, bias_ref, hm_ref, o_ref, *, n_grid_rows):
    i = pl.program_id(1)
    hmask = hm_ref[...]
    hms = (hmask * (NA_DIM ** -0.5)).astype(BF16)
    kc = kc_ref[...]
    vc = vc_ref[...]
    band = NA_WIN_ROWS * GRID_W
    for rr in range(NA_QROWS):
        r = i * NA_QROWS + rr
        rs = jnp.clip(r - NA_WIN_ROWS // 2, 0, n_grid_rows - NA_WIN_ROWS)
        var = rs - r + (NA_WIN_ROWS - 1)
        start = pl.multiple_of(rs * GRID_W, GRID_W)
        keys = k_ref[pl.ds(start, band), :]
        vals = v_ref[pl.ds(start, band), :]
        qs = _stack_heads(q_ref[rr * GRID_W:(rr + 1) * GRID_W, :], hms)
        o = _attend(qs, keys, vals, bias_ref[var], kc, vc)
        o_ref[rr * GRID_W:(rr + 1) * GRID_W, :] = _unstack_heads(o, hmask, GRID_W).astype(BF16)


def _na_ctx_kernel(q_ref, kc_ref, vc_ref, hm_ref, o_ref):
    hmask = hm_ref[...]
    hms = (hmask * (NA_DIM ** -0.5)).astype(BF16)
    n = q_ref.shape[0]
    o = _attend(_stack_heads(q_ref[...], hms), None, None, None, kc_ref[...], vc_ref[...])
    o_ref[...] = _unstack_heads(o, hmask, n).astype(BF16)


def _neighborhood(proj_l, proj_c, bias, layer, hmask, batch, seq_len, ctx_len, need_ctx_out):
    rows = seq_len // GRID_W
    qblk = NA_QROWS * GRID_W
    nq = seq_len // qblk
    out_l = pl.pallas_call(
        functools.partial(_na_kernel, n_grid_rows=rows),
        grid=(batch, nq),
        in_specs=[
            pl.BlockSpec((qblk, BRANCH_W), lambda b, i: (b * nq + i, COL_NQ)),
            pl.BlockSpec((seq_len, BRANCH_W), lambda b, i: (b, COL_NK)),
            pl.BlockSpec((seq_len, BRANCH_W), lambda b, i: (b, COL_NV)),
            pl.BlockSpec((ctx_len, BRANCH_W), lambda b, i: (b, COL_NK)),
            pl.BlockSpec((ctx_len, BRANCH_W), lambda b, i: (b, COL_NV)),
            pl.BlockSpec((None,) + bias.shape[1:], lambda b, i: (layer, 0, 0, 0)),
            pl.BlockSpec(hmask.shape, lambda b, i: (0, 0)),
        ],
        out_specs=pl.BlockSpec((qblk, BRANCH_W), lambda b, i: (b * nq + i, 0)),
        out_shape=jax.ShapeDtypeStruct((batch * seq_len, BRANCH_W), BF16),
        compiler_params=_cparams("arbitrary", "arbitrary"),
        name="neighborhood_attn",
    )(proj_l, proj_l, proj_l, proj_c, proj_c, bias, hmask)
    out_c = None
    if need_ctx_out:
        out_c = pl.pallas_call(
            _na_ctx_kernel,
            grid=(batch,),
            in_specs=[
                pl.BlockSpec((ctx_len, BRANCH_W), lambda b: (b, COL_NQ)),
                pl.BlockSpec((ctx_len, BRANCH_W), lambda b: (b, COL_NK)),
                pl.BlockSpec((ctx_len, BRANCH_W), lambda b: (b, COL_NV)),
                pl.BlockSpec(hmask.shape, lambda b: (0, 0)),
            ],
            out_specs=pl.BlockSpec((ctx_len, BRANCH_W), lambda b: (b, 0)),
            out_shape=jax.ShapeDtypeStruct((batch * ctx_len, BRANCH_W), BF16),
            compiler_params=_cparams("arbitrary"),
            name="context_attn",
        )(proj_c, proj_c, proj_c, hmask)
    return out_l, out_c


def _merge_kernel(x_ref, mod_ref, g_ref, gt0, gt1, gt2, gt3, fa_ref, fb_ref, s5a_ref, s5b_ref, rof_ref, rob_ref, rg_ref, na_ref,
                  wglu_ref, bglu_ref, gn_ref, avg_ref, wb_ref, wo_ref, o_ref, *, tiles_per_mod, mod_base):
    i = pl.program_id(0)
    _, _, gate_a = _mod_rows(mod_ref, i, tiles_per_mod, mod_base, 0)
    z = _gelu_tanh(jnp.concatenate([s5a_ref[...], s5b_ref[...]], axis=-1)).astype(BF16)
    zf = z.astype(F32)
    b_s5 = (zf * _sigmoid(_dot(z, wglu_ref[...]) + bglu_ref[...])).astype(BF16)
    o = rof_ref[...] + rob_ref[...]
    avg = avg_ref[...]
    hi, lo = _split_bf16(o)
    mu = _dot(hi, avg) + _dot(lo, avg)
    dlt = o - mu
    hi, lo = _split_bf16(dlt * dlt)
    var = _dot(hi, avg) + _dot(lo, avg)
    hn = dlt * lax.rsqrt(var + EPS) * gn_ref[...]
    b_ret = (_silu(rg_ref[...].astype(F32)) * hn).astype(BF16)
    b_fnet = jnp.concatenate([fa_ref[...], fb_ref[...]], axis=-1).astype(BF16)
    outs = (b_fnet, b_s5, b_ret, na_ref[...])
    gates = (gt0, gt1, gt2, gt3)
    y = (1.0 + jnp.tanh(gates[0][...].astype(F32))) * _dot(outs[0], wb_ref[0])
    for b in range(1, N_BRANCH):
        y = y + (1.0 + jnp.tanh(gates[b][...].astype(F32))) * _dot(outs[b], wb_ref[b])
    yo = _dot(y.astype(BF16), wo_ref[...])
    o_ref[...] = x_ref[...] + gate_a * _rms(yo, g_ref[...])


def _merge(x, mod, g1, proj, a, s5y, ret_o, na, lw, *, rows_per_mod, mod_base):
    rows, d = x.shape
    tm = min(512, rows)
    nt = rows // tm

    def row(shape, col=0):
        return pl.BlockSpec(shape, lambda i: (i, col))

    def const(arr):
        return pl.BlockSpec(arr.shape, lambda i: (0,) * arr.ndim)

    kern = functools.partial(_merge_kernel, tiles_per_mod=max(rows_per_mod // tm, 1), mod_base=mod_base)
    ins = [x, mod, g1.reshape(1, d), proj, proj, proj, proj, a[0], a[1], s5y[0], s5y[1], ret_o[0], ret_o[1], proj, na,
           lw['w_glu'], lw['b_glu'], lw['ret_gn'], lw['avg'], lw['w_branch'], lw['w_out']]
    specs = [
        row((tm, d)), const(mod), pl.BlockSpec((1, d), lambda i: (0, 0)),
        row((tm, d), 0), row((tm, d), 1), row((tm, d), 2), row((tm, d), 3),
        row((tm, 128)), row((tm, 128)), row((tm, 128)), row((tm, 128)),
        row((tm, BRANCH_W)), row((tm, BRANCH_W)),
        row((tm, BRANCH_W), COL_RG), row((tm, BRANCH_W)),
        const(lw['w_glu']), const(lw['b_glu']), const(lw['ret_gn']), const(lw['avg']),
        const(lw['w_branch']), const(lw['w_out']),
    ]
    return pl.pallas_call(
        kern,
        grid=(nt,),
        in_specs=specs,
        out_specs=row((tm, d)),
        out_shape=jax.ShapeDtypeStruct((rows, d), F32),
        compiler_params=_cparams("arbitrary"),
        name="merge_out",
    )(*ins)


def _ffn_kernel(x_ref, mod_ref, g2_ref, g3_ref, wg_ref, wu_ref, wd_ref, o_ref, *, tiles_per_mod, mod_base):
    i = pl.program_id(0)
    sh, sc, gate_f = _mod_rows(mod_ref, i, tiles_per_mod, mod_base, 3)
    x = x_ref[...]
    h = (_rms(x, g2_ref[...]) * (1.0 + sc) + sh).astype(BF16)
    act = (_silu(_dot(h, wg_ref[...])) * _dot(h, wu_ref[...])).astype(BF16)
    y = _dot(act, wd_ref[...])
    o_ref[...] = x + gate_f * _rms(y, g3_ref[...])


def _ffn_dense(x, mod, g2, g3, wg, wu, wd, *, rows_per_mod, mod_base):
    rows, d = x.shape
    d_ff = wg.shape[1]
    tm = min(512, rows)
    kern = functools.partial(_ffn_kernel, tiles_per_mod=max(rows_per_mod // tm, 1), mod_base=mod_base)

    def resident(shape):
        return pl.BlockSpec(shape, lambda i: (0, 0), pipeline_mode=pl.Buffered(1))

    return pl.pallas_call(
        kern,
        grid=(rows // tm,),
        in_specs=[
            pl.BlockSpec((tm, d), lambda i: (i, 0)),
            pl.BlockSpec(mod.shape, lambda i: (0, 0)),
            pl.BlockSpec((1, d), lambda i: (0, 0)),
            pl.BlockSpec((1, d), lambda i: (0, 0)),
            resident((d, d_ff)), resident((d, d_ff)), resident((d_ff, d)),
        ],
        out_specs=pl.BlockSpec((tm, d), lambda i: (i, 0)),
        out_shape=jax.ShapeDtypeStruct((rows, d), F32),
        compiler_params=_cparams("arbitrary"),
        name="ffn_dense",
    )(x, mod, g2.reshape(1, d), g3.reshape(1, d), wg, wu, wd)


def _router_kernel(x_ref, mod_ref, g2_ref, wr_ref, br_ref, tri_ref, h_ref, comb_ref, plan_ref, cnt_ref, cnt_scr,
                   *, tiles_per_mod, mod_base):
    i = pl.program_id(0)

    @pl.when(i == 0)
    def _():
        cnt_scr[...] = jnp.zeros_like(cnt_scr)

    sh, sc, _ = _mod_rows(mod_ref, i, tiles_per_mod, mod_base, 3)
    h = _rms(x_ref[...], g2_ref[...]) * (1.0 + sc) + sh
    h_ref[...] = _pack_pairs(h)
    h_hi, h_lo = _split_bf16(h)
    w_hi, w_lo = _split_bf16(wr_ref[...])
    logits = _dot(h_hi, w_hi) + _dot(h_lo, w_hi) + _dot(h_hi, w_lo) + br_ref[...]
    lane = lax.broadcasted_iota(jnp.int32, logits.shape, 1)
    v1 = jnp.max(logits, axis=-1, keepdims=True)
    i1 = jnp.min(jnp.where(logits == v1, lane, 128), axis=-1, keepdims=True)
    rest = jnp.where(lane == i1, NEG_BIG, logits)
    v2 = jnp.max(rest, axis=-1, keepdims=True)
    i2 = jnp.min(jnp.where(rest == v2, lane, 128), axis=-1, keepdims=True)
    e = jnp.exp(v2 - v1)
    w1 = 1.0 / (1.0 + e)
    w2 = e / (1.0 + e)
    meta = jnp.where(lane == 0, i1.astype(F32), 0.0) + jnp.where(lane == 1, i2.astype(F32), 0.0)
    meta = meta + jnp.where(lane == 2, w1, 0.0) + jnp.where(lane == 3, w2, 0.0)
    member = jnp.where((lane == i1) | (lane == i2), 1.0, 0.0)
    before = _dot(tri_ref[...], member.astype(BF16)) + cnt_scr[...]
    rank1 = jnp.sum(jnp.where(lane == i1, before, 0.0), axis=-1, keepdims=True)
    rank2 = jnp.sum(jnp.where(lane == i2, before, 0.0), axis=-1, keepdims=True)
    cnt_scr[...] += jnp.sum(member, axis=0, keepdims=True)
    cnt_ref[...] = cnt_scr[...]
    meta = meta + jnp.where(lane == 4, rank1, 0.0) + jnp.where(lane == 5, rank2, 0.0)
    comb_ref[...] = meta[:, :MOE_META_W]
    plan_ref[...] = meta.T[:MOE_META_W]


def _router(x, mod, g2, w_router, b_router, *, rows_per_mod, mod_base):
    rows, d = x.shape
    tm = min(512, rows)
    wr = jnp.zeros((d, 128), F32).at[:, :N_EXPERTS].set(w_router)
    br = jnp.full((1, 128), NEG_BIG, F32).at[0, :N_EXPERTS].set(b_router)
    tri = jnp.asarray(np.tril(np.ones((tm, tm), np.float32), -1), BF16)
    kern = functools.partial(_router_kernel, tiles_per_mod=max(rows_per_mod // tm, 1), mod_base=mod_base)
    return pl.pallas_call(
        kern,
        grid=(rows // tm,),
        in_specs=[
            pl.BlockSpec((tm, d), lambda i: (i, 0)),
            pl.BlockSpec(mod.shape, lambda i: (0, 0)),
            pl.BlockSpec((1, d), lambda i: (0, 0)),
            pl.BlockSpec((d, 128), lambda i: (0, 0)),
            pl.BlockSpec((1, 128), lambda i: (0, 0)),
            pl.BlockSpec((tm, tm), lambda i: (0, 0)),
        ],
        out_specs=[
            pl.BlockSpec((tm, d // 2), lambda i: (i, 0)),
            pl.BlockSpec((tm, MOE_META_W), lambda i: (i, 0)),
            pl.BlockSpec((MOE_META_W, tm), lambda i: (0, i)),
            pl.BlockSpec((1, 128), lambda i: (0, 0)),
        ],
        out_shape=[
            jax.ShapeDtypeStruct((rows, d // 2), jnp.int32),
            jax.ShapeDtypeStruct((rows, MOE_META_W), F32),
            jax.ShapeDtypeStruct((MOE_META_W, rows), F32),
            jax.ShapeDtypeStruct((1, 128), F32),
        ],
        scratch_shapes=[pltpu.VMEM((1, 128), F32)],
        compiler_params=_cparams("arbitrary"),
        name="moe_router",
    )(x, mod, g2.reshape(1, d), wr, br, tri)


def _sc_gather(table, idx):
    n_idx = idx.shape[0]
    width = table.shape[1]
    per_worker = n_idx // SC_WORKERS
    chunk_rows = math.gcd(per_worker, SC_GATHER_ROWS)
    n_chunks = per_worker // chunk_rows
    assert per_worker * SC_WORKERS == n_idx and chunk_rows % 8 == 0
    mesh = plsc.VectorSubcoreMesh(core_axis_name="c", subcore_axis_name="s")

    assert n_chunks % 2 == 0
    buf = [pltpu.VMEM((chunk_rows,), jnp.int32), pltpu.VMEM((chunk_rows, width), table.dtype),
           pltpu.SemaphoreType.DMA, pltpu.SemaphoreType.DMA]

    @functools.partial(
        pl.kernel, mesh=mesh,
        out_type=jax.ShapeDtypeStruct((n_idx, width), table.dtype),
        scratch_types=buf + buf,
        name="sc_row_gather",
    )
    def gather(table_hbm, idx_hbm, out_hbm, idx0, rows0, g0, w0, idx1, rows1, g1, w1):
        wid = lax.axis_index("s") * SC_CORES + lax.axis_index("c")
        base = wid * per_worker
        slots = ((idx0, rows0, g0, w0), (idx1, rows1, g1, w1))

        def fetch(j, slot):
            idx_v, rows_v, g, _ = slots[slot]
            pltpu.sync_copy(idx_hbm.at[pl.ds(base + j * chunk_rows, chunk_rows)], idx_v)
            pltpu.make_async_copy(table_hbm.at[idx_v], rows_v, g).start()

        def store(j, slot):
            idx_v, rows_v, g, w = slots[slot]
            pltpu.make_async_copy(table_hbm.at[idx_v], rows_v, g).wait()
            pltpu.make_async_copy(rows_v, out_hbm.at[pl.ds(base + j * chunk_rows, chunk_rows)], w).start()

        def drain(j, slot):
            _, rows_v, _, w = slots[slot]
            pltpu.make_async_copy(rows_v, out_hbm.at[pl.ds(base + j * chunk_rows, chunk_rows)], w).wait()

        fetch(0, 0)

        @pl.loop(0, n_chunks // 2)
        def _(jj):
            j = 2 * jj

            @pl.when(jj > 0)
            def _():
                drain(j - 1, 1)

            fetch(j + 1, 1)
            store(j, 0)

            @pl.when(j + 2 < n_chunks)
            def _():
                drain(j, 0)
                fetch(j + 2, 0)

            store(j + 1, 1)

        drain(n_chunks - 2, 0)
        drain(n_chunks - 1, 1)

    return gather(table, idx)


def _sc_scatter(table, idx, n_out):
    n_idx = idx.shape[0]
    rows, width = table.shape
    per_worker = n_idx // SC_WORKERS
    chunk_rows = math.gcd(per_worker, SC_GATHER_ROWS)
    n_chunks = per_worker // chunk_rows
    assert per_worker * SC_WORKERS == n_idx and chunk_rows % 8 == 0 and rows % per_worker == 0
    mesh = plsc.VectorSubcoreMesh(core_axis_name="c", subcore_axis_name="s")

    assert n_chunks % 2 == 0
    buf = [pltpu.VMEM((chunk_rows,), jnp.int32), pltpu.VMEM((chunk_rows, width), table.dtype),
           pltpu.SemaphoreType.DMA, pltpu.SemaphoreType.DMA]

    @functools.partial(
        pl.kernel, mesh=mesh,
        out_type=jax.ShapeDtypeStruct((n_out, width), table.dtype),
        scratch_types=buf + buf,
        name="sc_row_scatter",
    )
    def scatter(table_hbm, idx_hbm, out_hbm, idx0, rows0, l0, w0, idx1, rows1, l1, w1):
        wid = lax.axis_index("s") * SC_CORES + lax.axis_index("c")
        base = wid * per_worker
        slots = ((idx0, rows0, l0, w0), (idx1, rows1, l1, w1))

        def src(j):
            return table_hbm.at[pl.ds(lax.rem(base + j * chunk_rows, rows), chunk_rows)]

        def fetch(j, slot):
            idx_v, rows_v, l, _ = slots[slot]
            pltpu.sync_copy(idx_hbm.at[pl.ds(base + j * chunk_rows, chunk_rows)], idx_v)
            pltpu.make_async_copy(src(j), rows_v, l).start()

        def store(j, slot):
            idx_v, rows_v, l, w = slots[slot]
            pltpu.make_async_copy(src(j), rows_v, l).wait()
            pltpu.make_async_copy(rows_v, out_hbm.at[idx_v], w).start()

        def drain(slot):
            idx_v, rows_v, _, w = slots[slot]
            pltpu.make_async_copy(rows_v, out_hbm.at[idx_v], w).wait()

        fetch(0, 0)

        @pl.loop(0, n_chunks // 2)
        def _(jj):
            j = 2 * jj

            @pl.when(jj > 0)
            def _():
                drain(1)

            fetch(j + 1, 1)
            store(j, 0)

            @pl.when(j + 2 < n_chunks)
            def _():
                drain(0)
                fetch(j + 2, 0)

            store(j + 1, 1)

        drain(0)
        drain(1)

    return scatter(table, idx)


def _moe_plan(plan, counts_row, rows):
    tile = MOE_ROW_TILE
    n_tiles = (2 * rows) // tile + N_EXPERTS
    n_slots = n_tiles * tile
    counts = counts_row[0, :N_EXPERTS].astype(jnp.int32)
    padded = ((counts + tile - 1) // tile) * tile
    ends = jnp.cumsum(padded)
    starts = ends - padded
    ids = jnp.arange(N_EXPERTS, dtype=F32)[:, None]
    start_f = starts.astype(F32)[:, None]

    def slot(e_row, r_row):
        return jnp.sum(jnp.where(e_row[None, :] == ids, start_f, 0.0), axis=0) + r_row

    pos = jnp.concatenate([slot(plan[0], plan[4]), slot(plan[1], plan[5])])
    tile_start = jnp.arange(n_tiles, dtype=jnp.int32) * tile
    used = tile_start < ends[-1]
    tile_e = jnp.minimum(jnp.sum((tile_start[:, None] >= ends[None, :]).astype(jnp.int32), axis=1), N_EXPERTS - 1)
    last_e = jnp.max(jnp.where(used, tile_e, 0))
    tile_e = jnp.where(used, tile_e, last_e)
    valid_end = jnp.sum((tile_e[:, None] == jnp.arange(N_EXPERTS)[None, :]) * (starts + counts)[None, :], axis=1)
    n_valid = jnp.where(used, jnp.clip(valid_end - tile_start, 0, tile), 0).astype(jnp.int32)
    return pos.astype(jnp.int32), n_slots, tile_e.astype(jnp.int32), n_valid


def _moe_group_kernel(eid_ref, nval_ref, hs_ref, wg_ref, wu_ref, wd_ref, y_ref, acc_scr, *, n_f):
    w = pl.program_id(0)
    f = pl.program_id(1)
    nv = nval_ref[w]

    def run(n_rows):
        wg = wg_ref[...].astype(BF16)
        wu = wu_ref[...].astype(BF16)
        wd = wd_ref[...].astype(BF16)
        for r0 in range(0, n_rows, MOE_SUB_ROWS):
            rows = slice(r0, r0 + MOE_SUB_ROWS)
            hv = _unpack_pairs(hs_ref[rows, :])
            row = r0 + lax.broadcasted_iota(jnp.int32, hv.shape, 0)
            h = jnp.where(row < nv, hv, 0.0).astype(BF16)
            part = _dot((_silu(_dot(h, wg)) * _dot(h, wu)).astype(BF16), wd)
            acc = jnp.where(f == 0, 0.0, acc_scr[rows, :]) + part
            acc_scr[rows, :] = acc
            y_ref[rows, :] = _pack_pairs(acc)

    for groups in range(1, hs_ref.shape[0] // MOE_SUB_ROWS + 1):
        @pl.when((nv > (groups - 1) * MOE_SUB_ROWS) & (nv <= groups * MOE_SUB_ROWS))
        def _(groups=groups):
            run(groups * MOE_SUB_ROWS)


def _moe_grouped(hs, tile_e, n_valid, wg, wu, wd):
    n_slots = hs.shape[0]
    d = wg.shape[1]
    d_ff = wg.shape[2]
    tile = MOE_ROW_TILE
    tf = MOE_FF_TILE
    n_f = d_ff // tf

    def f_idx(f, nval, w):
        return jnp.where(nval[w] > 0, f, n_f - 1)

    grid_spec = pltpu.PrefetchScalarGridSpec(
        num_scalar_prefetch=2,
        grid=(n_slots // tile, n_f),
        in_specs=[
            pl.BlockSpec((tile, d // 2), lambda w, f, eid, nval: (w, 0)),
            pl.BlockSpec((None, d, tf), lambda w, f, eid, nval: (eid[w], 0, f_idx(f, nval, w))),
            pl.BlockSpec((None, d, tf), lambda w, f, eid, nval: (eid[w], 0, f_idx(f, nval, w))),
            pl.BlockSpec((None, tf, d), lambda w, f, eid, nval: (eid[w], f_idx(f, nval, w), 0)),
        ],
        out_specs=pl.BlockSpec((tile, d // 2), lambda w, f, eid, nval: (w, 0)),
        scratch_shapes=[pltpu.VMEM((tile, d), F32)],
    )
    return pl.pallas_call(
        functools.partial(_moe_group_kernel, n_f=n_f),
        grid_spec=grid_spec,
        out_shape=jax.ShapeDtypeStruct((n_slots, d // 2), jnp.int32),
        compiler_params=_cparams("arbitrary", "arbitrary"),
        name="moe_experts",
    )(tile_e, n_valid, hs, wg, wu, wd)


def _moe_out_kernel(x_ref, y1_ref, y2_ref, meta_ref, mod_ref, g3_ref, *refs, tiles_per_mod, mod_base, first_tile):
    o_ref = refs[-1]
    i = first_tile + pl.program_id(0)
    _, _, gate_f = _mod_rows(mod_ref, i, tiles_per_mod, mod_base, 3)
    meta = meta_ref[...]
    y = meta[:, 2:3] * _unpack_pairs(y1_ref[...]) + meta[:, 3:4] * _unpack_pairs(y2_ref[...])
    o_ref[...] = x_ref[...] + gate_f * _rms(y, g3_ref[...])


def _moe_combine(x, yg, meta, mod, g3, part, n_parts, prev, *, rows_per_mod, mod_base):
    rows, d = x.shape
    tm = math.gcd(1024, rows_per_mod)
    nt = rows // tm // n_parts
    t0 = part * nt
    kern = functools.partial(_moe_out_kernel, tiles_per_mod=max(rows_per_mod // tm, 1), mod_base=mod_base,
                             first_tile=t0)
    ins = [x, yg, yg, meta, mod, g3.reshape(1, d)]
    specs = [
        pl.BlockSpec((tm, d), lambda i: (t0 + i, 0)),
        pl.BlockSpec((tm, d // 2), lambda i: (i, 0)),
        pl.BlockSpec((tm, d // 2), lambda i: (nt + i, 0)),
        pl.BlockSpec((tm, MOE_META_W), lambda i: (t0 + i, 0)),
        pl.BlockSpec(mod.shape, lambda i: (0, 0)),
        pl.BlockSpec((1, d), lambda i: (0, 0)),
    ]
    aliases = {}
    if prev is not None:
        ins.append(prev)
        specs.append(pl.BlockSpec(memory_space=pl.ANY))
        aliases = {len(ins) - 1: 0}
    return pl.pallas_call(
        kern,
        grid=(nt,),
        in_specs=specs,
        out_specs=pl.BlockSpec((tm, d), lambda i: (t0 + i, 0)),
        out_shape=jax.ShapeDtypeStruct((rows, d), F32),
        input_output_aliases=aliases,
        compiler_params=_cparams("arbitrary"),
        name="moe_combine",
    )(*ins)


def _moe_sparse(x, routed, mod, g3, wg, wu, wd, *, rows_per_mod, mod_base):
    h, meta, plan, counts = routed
    rows = x.shape[0]
    pos, n_slots, tile_e, n_valid = _moe_plan(plan, counts, rows)
    hs = _sc_scatter(h, pos, n_slots)
    ys = _moe_grouped(hs, tile_e, n_valid, wg, wu, wd)
    n_parts = 2 if (rows // 2) % (2 * SC_WORKERS * SC_GATHER_ROWS) == 0 else 1
    part_rows = rows // n_parts
    out = None
    for part in range(n_parts):
        sl = slice(part * part_rows, (part + 1) * part_rows)
        idx = jnp.concatenate([pos[:rows][sl], pos[rows:][sl]])
        yg = _sc_gather(ys, idx)
        out = _moe_combine(x, yg, meta, mod, g3, part, n_parts, out, rows_per_mod=rows_per_mod, mod_base=mod_base)
    return out


def _cast_kernel(w_ref, o_ref, *, scale):
    w = w_ref[...]
    o_ref[...] = (w if scale == 1.0 else w * scale).astype(BF16)


def _cast_bf16(w_stack, layer, scale=1.0):
    squeeze = w_stack.ndim == 3
    w4 = w_stack[:, None] if squeeze else w_stack
    _, n_e, k, n = w4.shape
    bk = min(k, 256)
    out = pl.pallas_call(
        functools.partial(_cast_kernel, scale=scale),
        grid=(n_e, k // bk),
        in_specs=[pl.BlockSpec((None, None, bk, n), lambda e, i: (layer, e, i, 0))],
        out_specs=pl.BlockSpec((None, bk, n), lambda e, i: (e, i, 0)),
        out_shape=jax.ShapeDtypeStruct((n_e, k, n), BF16),
        compiler_params=_cparams("arbitrary", "arbitrary"),
        name="cast_weights",
    )(w4)
    return out[0] if squeeze else out


def _permute_w_in(w_in_stack, layer):
    _, k, n = w_in_stack.shape
    n_blocks = n // BRANCH_W
    shift = 9
    n_gate_blocks = N_BRANCH * D_MODEL // BRANCH_W

    per_step = 5
    assert n_blocks % per_step == 0

    def permute_kernel(*refs):
        o_ref = refs[-1]
        for s, w_ref in enumerate(refs[:-1]):
            scale = jnp.where(pl.program_id(0) * per_step + s < n_gate_blocks, 0.5, 1.0)
            o_ref[:, s * BRANCH_W:(s + 1) * BRANCH_W] = (w_ref[...] * scale).astype(BF16)

    def src(s):
        return pl.BlockSpec((None, k, BRANCH_W), lambda j: (layer, 0, (j * per_step + s + shift) % n_blocks))

    return pl.pallas_call(
        permute_kernel,
        grid=(n_blocks // per_step,),
        in_specs=[src(s) for s in range(per_step)],
        out_specs=pl.BlockSpec((k, per_step * BRANCH_W), lambda j: (0, j)),
        out_shape=jax.ShapeDtypeStruct((k, n), BF16),
        compiler_params=_cparams("arbitrary"),
        name="cast_permute_w_in",
    )(*([w_in_stack] * per_step))


def kernel(x, c, ctx, c_ctx, w_mod, b_mod, norm_g, w_in, s5_a_re, s5_a_im, s5_log_dt, s5_b_re, s5_b_im, s5_c_re, s5_c_im, s5_d, s5_w_glu, s5_b_glu, ret_decay, ret_gn, na_rpb, w_branch, w_out, ffn_w_gate, ffn_w_up, ffn_w_down, moe_w_router, moe_b_router, moe_w_gate, moe_w_up, moe_w_down):
    batch, seq_len, d = x.shape
    ctx_len = ctx.shape[1]
    depth = w_mod.shape[0]
    cond = jnp.concatenate([c, c_ctx[None, :]], axis=0)
    mod_all = _modulation(cond, w_mod, b_mod)
    rope = _rope_tables(seq_len)
    lane_h = np.repeat(np.arange(RET_HEADS), RET_DIM)
    avg = jnp.asarray((lane_h[:, None] == lane_h[None, :]).astype(np.float32) / RET_DIM, BF16)

    xl = x.reshape(batch * seq_len, d)
    xc = ctx.reshape(batch * ctx_len, d)
    lat = dict(rows_per_mod=seq_len, mod_base=0)
    cxt = dict(rows_per_mod=batch * ctx_len, mod_base=batch)

    s5_tabs = jax.vmap(functools.partial(_s5_tables, batch=batch))(
        s5_a_re, s5_a_im, s5_log_dt, s5_b_re, s5_b_im, s5_c_re, s5_c_im, s5_d)
    ret_tabs = jax.vmap(_ret_tables)(ret_decay)
    ret_masks = _ret_masks()
    na_bias = jax.vmap(_na_tables)(na_rpb)
    na_hmask = _na_head_mask()

    for layer in range(depth):
        last = layer == depth - 1
        need_ctx = not last
        mod = mod_all[layer]
        ng = norm_g[layer]
        w_in_bf = _permute_w_in(w_in, layer)
        lw = dict(w_glu=s5_w_glu[layer].astype(BF16), b_glu=s5_b_glu[layer].reshape(1, BRANCH_W).astype(F32),
                  ret_gn=ret_gn[layer].reshape(1, BRANCH_W).astype(F32), avg=avg,
                  w_branch=_cast_bf16(w_branch, layer, 0.5), w_out=_cast_bf16(w_out, layer))

        proj_l, f_l, *s_in_l = _in_proj(xl, mod, ng[0], w_in_bf, **lat)
        proj_c, f_c, *s_in_c = _in_proj(xc, mod, ng[0], w_in_bf, **cxt)

        a_l = _fourier_latent(f_l, batch, seq_len)
        s_l, s_c = _s5_mixer(s_in_l, s_in_c, s5_tabs, layer, batch)
        r_l, r_c = _retention(proj_l, proj_c, ret_tabs, layer, ret_masks, rope, batch, seq_len, ctx_len)
        n_l, n_c = _neighborhood(proj_l, proj_c, na_bias, layer, na_hmask, batch, seq_len, ctx_len, need_ctx)

        xl = _merge(xl, mod, ng[1], proj_l, a_l, s_l, r_l, n_l, lw, **lat)
        if need_ctx:
            a_c = _fourier_ctx(f_c, batch, ctx_len)
            xc = _merge(xc, mod, ng[1], proj_c, a_c, s_c, r_c, n_c, lw, **cxt)

        i = layer // 2
        if layer % 2 == 0:
            wg, wu, wd = _cast_bf16(ffn_w_gate, i), _cast_bf16(ffn_w_up, i), _cast_bf16(ffn_w_down, i)
            xl = _ffn_dense(xl, mod, ng[2], ng[3], wg, wu, wd, **lat)
            if need_ctx:
                xc = _ffn_dense(xc, mod, ng[2], ng[3], wg, wu, wd, **cxt)
        else:
            wg, wu, wd = moe_w_gate[i], moe_w_up[i], moe_w_down[i]
            routed = _router(xl, mod, ng[2], moe_w_router[i], moe_b_router[i], **lat)
            xl = _moe_sparse(xl, routed, mod, ng[3], wg, wu, wd, **lat)
            if need_ctx:
                routed_c = _router(xc, mod, ng[2], moe_w_router[i], moe_b_router[i], **cxt)
                xc = _moe_sparse(xc, routed_c, mod, ng[3], wg, wu, wd, **cxt)
    return xl.reshape(batch, seq_len, d)
```

```python
import functools
import math

import numpy as np
import jax
import jax.numpy as jnp
from jax import lax
from jax.experimental import pallas as pl
from jax.experimental.pallas import tpu as pltpu
from jax.experimental.pallas import tpu_sc as plsc

F32 = jnp.float32
BF16 = jnp.bfloat16

D_MODEL = 1024
BRANCH_W = 256
N_BRANCH = 4
GRID_W = 64
FNET_GROUP_DIM = 64
S5_GROUP_CH = 16
S5_GROUPS = 16
S5_STATE = 64
S5_CHUNK = 32
S5_PAIRS = S5_GROUPS // 2
RET_HEADS = 4
RET_DIM = 64
RET_CHUNK = 128
NA_HEADS = 4
NA_DIM = 64
NA_WIN_ROWS = 8
NA_WIN_COLS = 16
NA_QROWS = 16
ROPE_BASE = 10000.0
N_EXPERTS = 8
EPS = 1e-6
FFT_N2 = 256
NEG_BIG = -1e30
VMEM_LIMIT_BYTES = 50 * 1024 * 1024
SC_CORES = 2
SC_SUBCORES = 16
SC_WORKERS = SC_CORES * SC_SUBCORES
SC_GATHER_ROWS = 64
MOE_ROW_TILE = 2048
MOE_SUB_ROWS = 512
MOE_FF_TILE = 512
MOE_META_W = 8

COL_F, COL_S, COL_RQ, COL_RK, COL_RV, COL_RG, COL_NQ, COL_NK, COL_NV = range(16, 25)
IN_W = 9 * BRANCH_W + N_BRANCH * D_MODEL
IN_TN = 1280
IN_F_TILE = (N_BRANCH * D_MODEL) // IN_TN
IN_F_OFF = N_BRANCH * D_MODEL - IN_F_TILE * IN_TN
IN_S_OFF = IN_F_OFF + BRANCH_W


def _cparams(*sem):
    return pltpu.CompilerParams(dimension_semantics=sem, vmem_limit_bytes=VMEM_LIMIT_BYTES)


def _sigmoid(v):
    return 0.5 * jnp.tanh(0.5 * v) + 0.5


def _silu(v):
    return v * _sigmoid(v)


def _gelu_tanh(v):
    return 0.5 * v * (1.0 + jnp.tanh(math.sqrt(2.0 / math.pi) * (v + 0.044715 * (v * v * v))))


def _rms(v, g):
    ms = jnp.mean(v * v, axis=-1, keepdims=True)
    return v * lax.rsqrt(ms + EPS) * g


def _split_bf16(v):
    hi = v.astype(BF16)
    lo = (v - hi.astype(F32)).astype(BF16)
    return hi, lo


def _pack_pairs(v):
    n = v.shape[1] // 2
    lo = lax.bitcast_convert_type(v[:, :n].astype(BF16).astype(F32), jnp.int32)
    hi = lax.bitcast_convert_type(v[:, n:].astype(BF16).astype(F32), jnp.int32)
    return (hi & -65536) | ((lo >> 16) & 65535)


def _unpack_pairs(w):
    lo = lax.bitcast_convert_type(w << 16, F32)
    hi = lax.bitcast_convert_type(w & -65536, F32)
    return jnp.concatenate([lo, hi], axis=-1)


def _dot(a, b):
    return jnp.dot(a, b, preferred_element_type=F32)


def _dot_nt(a, b):
    return lax.dot_general(a, b, (((1,), (1,)), ((), ())), preferred_element_type=F32)


def _dot_tn(a, b):
    return lax.dot_general(a, b, (((0,), (0,)), ((), ())), preferred_element_type=F32)


def _mod_kernel(ct_ref, w_ref, b_ref, o_ref, *, n_cond):
    ct = ct_ref[...]
    s = _silu(ct)
    w = w_ref[...]
    rows = [jnp.sum(w * s[:, r:r + 1], axis=0, keepdims=True) for r in range(n_cond)]
    rows.append(jnp.zeros((8 - n_cond, w.shape[1]), F32))
    o_ref[...] = jnp.concatenate(rows, axis=0) + b_ref[...]


def _modulation(cond, w_mod, b_mod):
    n_layers, d, n = w_mod.shape
    tn = 1536
    ct = jnp.zeros((8, d), F32).at[:cond.shape[0]].set(cond).T
    return pl.pallas_call(
        functools.partial(_mod_kernel, n_cond=cond.shape[0]),
        grid=(n_layers, n // tn),
        in_specs=[
            pl.BlockSpec((d, 8), lambda l, j: (0, 0)),
            pl.BlockSpec((None, d, tn), lambda l, j: (l, 0, j)),
            pl.BlockSpec((None, 1, tn), lambda l, j: (l, 0, j)),
        ],
        out_specs=pl.BlockSpec((None, 8, tn), lambda l, j: (l, 0, j)),
        out_shape=jax.ShapeDtypeStruct((n_layers, 8, n), F32),
        compiler_params=_cparams("arbitrary", "arbitrary"),
        name="adaln_mod",
    )(ct, w_mod, b_mod.reshape(n_layers, 1, n))


def _mod_rows(mod_ref, i, tiles_per_mod, mod_base, first):
    r = mod_base + i // tiles_per_mod
    return [mod_ref[pl.ds(r, 1), (first + k) * D_MODEL:(first + k + 1) * D_MODEL] for k in range(3)]


def _in_kernel(x_ref, mod_ref, g_ref, w_ref, proj_ref, f_ref, sa_ref, sb_ref, *, tiles_per_mod, mod_base):
    i = pl.program_id(0)
    sh, sc, _ = _mod_rows(mod_ref, i, tiles_per_mod, mod_base, 0)
    h = (_rms(x_ref[...], g_ref[...]) * (1.0 + sc) + sh).astype(BF16)
    for j in range(IN_W // IN_TN):
        res = _dot(h, w_ref[:, j * IN_TN:(j + 1) * IN_TN])
        proj_ref[:, j * IN_TN:(j + 1) * IN_TN] = res.astype(BF16)
        if j == IN_F_TILE:
            f_ref[...] = res[:, IN_F_OFF:IN_F_OFF + BRANCH_W].astype(BF16)
            sa_ref[...] = res[:, IN_S_OFF:IN_S_OFF + 128]
            sb_ref[...] = res[:, IN_S_OFF + 128:IN_S_OFF + 256]


def _in_proj(x, mod, g, w_bf, *, rows_per_mod, mod_base):
    rows, d = x.shape
    tm = math.gcd(512, rows_per_mod)
    kern = functools.partial(_in_kernel, tiles_per_mod=max(rows_per_mod // tm, 1), mod_base=mod_base)
    return pl.pallas_call(
        kern,
        grid=(rows // tm,),
        in_specs=[
            pl.BlockSpec((tm, d), lambda i: (i, 0)),
            pl.BlockSpec(mod.shape, lambda i: (0, 0)),
            pl.BlockSpec((1, d), lambda i: (0, 0)),
            pl.BlockSpec((d, IN_W), lambda i: (0, 0), pipeline_mode=pl.Buffered(1)),
        ],
        out_specs=[
            pl.BlockSpec((tm, IN_W), lambda i: (i, 0)),
            pl.BlockSpec((tm, BRANCH_W), lambda i: (i, 0)),
            pl.BlockSpec((tm, 128), lambda i: (i, 0)),
            pl.BlockSpec((tm, 128), lambda i: (i, 0)),
        ],
        out_shape=[
            jax.ShapeDtypeStruct((rows, IN_W), BF16),
            jax.ShapeDtypeStruct((rows, BRANCH_W), BF16),
            jax.ShapeDtypeStruct((rows, 128), F32),
            jax.ShapeDtypeStruct((rows, 128), F32),
        ],
        compiler_params=_cparams("arbitrary"),
        name="in_proj",
    )(x, mod, g.reshape(1, d), w_bf)


def _fft_a_kernel(x_ref, cs_ref, tc_ref, ts_ref, zr_ref, zi_ref, *, n1, n1p):
    y = _dot(cs_ref[...].astype(BF16), x_ref[...])
    yr = y[:n1]
    yi = y[n1p:n1p + n1]
    tc = tc_ref[...]
    ts = ts_ref[...]
    zr_ref[...] = (yr * tc + yi * ts).astype(BF16)
    zi_ref[...] = (yi * tc - yr * ts).astype(BF16)


def _fft_b_kernel(zr_ref, zi_ref, cs_ref, cc_ref, sc_ref, oa_ref, ob_ref, *, kb, n1, scale, has_imag):
    cs = cs_ref[...].astype(BF16)
    cc = cc_ref[...].astype(BF16)
    sc = sc_ref[...].astype(BF16)
    half = BRANCH_W // 2
    for kk in range(kb):
        a = _dot(cs, zr_ref[kk])
        if has_imag:
            b = _dot(cs, zi_ref[kk])
            xr = a[:FFT_N2] + b[FFT_N2:]
            xi = b[:FFT_N2] - a[FFT_N2:]
        else:
            xr = a[:FFT_N2]
            xi = -a[FFT_N2:]
        out = (_dot(xr.astype(BF16), cc) + _dot(xi.astype(BF16), sc)) * scale
        k1 = pl.program_id(1) * kb + kk
        oa_ref[pl.ds(k1, FFT_N2, stride=n1), :] = out[:, :half]
        ob_ref[pl.ds(k1, FFT_N2, stride=n1), :] = out[:, half:]


def _dft_tables(n):
    k = np.arange(n)
    ang = 2.0 * np.pi * ((k[:, None] * k[None, :]) % n) / n
    return np.cos(ang), np.sin(ang)


def _fft_b_call(zr, zi, n1, batch, seq_len, has_imag):
    c2, s2 = _dft_tables(FFT_N2)
    cs2 = jnp.asarray(np.concatenate([c2, s2], axis=0), F32)
    c64, s64 = _dft_tables(FNET_GROUP_DIM)
    eye = np.eye(BRANCH_W // FNET_GROUP_DIM)
    cc = jnp.asarray(np.kron(eye, c64), F32)
    sc = jnp.asarray(np.kron(eye, s64), F32)
    kb = min(8, n1)
    scale = 1.0 / math.sqrt(seq_len * FNET_GROUP_DIM)
    kern = functools.partial(_fft_b_kernel, kb=kb, n1=n1, scale=scale, has_imag=has_imag)
    zspec = pl.BlockSpec((None, kb, FFT_N2, BRANCH_W), lambda b, i: (b, i, 0, 0))
    half = pl.BlockSpec((seq_len, BRANCH_W // 2), lambda b, i: (b, 0))
    return pl.pallas_call(
        kern,
        grid=(batch, n1 // kb),
        in_specs=[
            zspec, zspec,
            pl.BlockSpec((2 * FFT_N2, FFT_N2), lambda b, i: (0, 0)),
            pl.BlockSpec((BRANCH_W, BRANCH_W), lambda b, i: (0, 0)),
            pl.BlockSpec((BRANCH_W, BRANCH_W), lambda b, i: (0, 0)),
        ],
        out_specs=[half, half],
        out_shape=[jax.ShapeDtypeStruct((batch * seq_len, BRANCH_W // 2), F32)] * 2,
        compiler_params=_cparams("arbitrary", "arbitrary"),
        name="fourier_stage_b",
    )(zr, zi, cs2, cc, sc)


def _fourier_latent(f, batch, seq_len):
    n1 = seq_len // FFT_N2
    wide = FFT_N2 * BRANCH_W
    c1, s1 = _dft_tables(n1)
    n1p = max(n1, 8)
    cs1 = np.zeros((2 * n1p, n1))
    cs1[:n1] = c1
    cs1[n1p:n1p + n1] = -s1
    k1 = np.arange(n1)[:, None]
    l2 = np.arange(FFT_N2)[None, :]
    tw = 2.0 * np.pi * (k1 * l2) / seq_len
    tc = jnp.asarray(np.repeat(np.cos(tw), BRANCH_W, axis=1), F32)
    ts = jnp.asarray(np.repeat(np.sin(tw), BRANCH_W, axis=1), F32)
    cw = min(8192, wide)
    xv = f.reshape(batch, n1, wide)
    spec = pl.BlockSpec((None, n1, cw), lambda b, j: (b, 0, j))
    tspec = pl.BlockSpec((n1, cw), lambda b, j: (0, j))
    zr, zi = pl.pallas_call(
        functools.partial(_fft_a_kernel, n1=n1, n1p=n1p),
        grid=(batch, wide // cw),
        in_specs=[spec, pl.BlockSpec((2 * n1p, n1), lambda b, j: (0, 0)), tspec, tspec],
        out_specs=[spec, spec],
        out_shape=[jax.ShapeDtypeStruct((batch, n1, wide), BF16)] * 2,
        compiler_params=_cparams("arbitrary", "arbitrary"),
        name="fourier_stage_a",
    )(xv, jnp.asarray(cs1, F32), tc, ts)
    zr = zr.reshape(batch, n1, FFT_N2, BRANCH_W)
    zi = zi.reshape(batch, n1, FFT_N2, BRANCH_W)
    return _fft_b_call(zr, zi, n1, batch, seq_len, True)


def _fourier_ctx(f, batch, ctx_len):
    assert ctx_len == FFT_N2
    z = f.reshape(batch, 1, FFT_N2, BRANCH_W)
    return _fft_b_call(z, z, 1, batch, ctx_len, False)


def _s5_tables(a_re, a_im, log_dt, b_re, b_im, c_re, c_im, d_skip, batch):
    t = S5_CHUNK
    g, p, hc = S5_GROUPS, S5_STATE, S5_GROUP_CH
    lam = lax.complex(a_re.astype(F32), a_im.astype(F32))
    dt = jnp.exp(log_dt.astype(F32))[..., None]
    ks = jnp.arange(t + 1, dtype=F32)
    apow = jnp.exp((lam * dt)[..., None] * ks)
    a_bar = apow[..., 1]
    b_bar = ((a_bar - 1.0) / lam)[..., None] * lax.complex(b_re.astype(F32), b_im.astype(F32))
    cm = lax.complex(c_re.astype(F32), c_im.astype(F32))
    lagv = jnp.arange(-(t - 1), t, dtype=F32)
    ldt = lam * dt
    pw_f = jnp.where(lagv >= 0, jnp.exp(ldt[0][..., None] * jnp.maximum(lagv, 0.0)), 0.0)
    pw_b = jnp.where(lagv <= 0, jnp.exp(ldt[1][..., None] * jnp.maximum(-lagv, 0.0)), 0.0)
    kfull = jnp.real(jnp.einsum('ghp,gpl,gpj->gjlh',
                                jnp.concatenate([cm[0], cm[1]], axis=-1),
                                jnp.concatenate([pw_f, pw_b], axis=1),
                                jnp.concatenate([b_bar[0], b_bar[1]], axis=1),
                                precision=lax.Precision.HIGHEST))
    kp = kfull.reshape(S5_PAIRS, 2, hc, 2 * t - 1, hc)
    blk = [kp[:, gi] for gi in range(2)]
    zb = jnp.zeros_like(blk[0])
    strip = jnp.concatenate([jnp.stack([blk[0], zb], axis=3), jnp.stack([zb, blk[1]], axis=3)], axis=1)
    strip = strip.reshape(S5_PAIRS, 2 * hc, (2 * t - 1) * 2 * hc)
    strip = jnp.pad(strip, ((0, 0), (0, 0), (0, 2 * hc)))

    apow_kp = jnp.exp(ks[:, None] * ldt[:, :, None, :])
    b_hp = jnp.swapaxes(b_bar, -1, -2)
    wf = apow_kp[0][:, t - 1::-1, None, :] * b_hp[0][:, None, :, :]
    wb = apow_kp[1][:, :t, None, :] * b_hp[1][:, None, :, :]
    kinds = [jnp.real(wf), jnp.imag(wf), jnp.real(wb), jnp.imag(wb)]

    def we_pair(kd):
        k5 = kd.reshape(S5_PAIRS, 2, t, hc, p)
        z = jnp.zeros_like(k5[:, 0])
        rows = jnp.stack([jnp.concatenate([k5[:, 0], z], axis=-1), jnp.concatenate([z, k5[:, 1]], axis=-1)], axis=2)
        return rows.reshape(S5_PAIRS, 2 * t * hc, 2 * p)

    we = jnp.concatenate([we_pair(kd) for kd in kinds], axis=-1).astype(BF16)

    c_ph = jnp.swapaxes(cm, -1, -2)
    vf = apow[0][:, :, 1:t + 1, None] * c_ph[0][:, :, None, :]
    vb = apow[1][:, :, t:0:-1, None] * c_ph[1][:, :, None, :]
    vkinds = [jnp.real(vf), -jnp.imag(vf), jnp.real(vb), -jnp.imag(vb)]

    def v_pair(kd):
        k5 = kd.reshape(S5_PAIRS, 2, p, t, hc)
        z = jnp.zeros_like(k5[:, 0])
        rows = jnp.concatenate([jnp.stack([k5[:, 0], z], axis=3), jnp.stack([z, k5[:, 1]], axis=3)], axis=1)
        return rows.reshape(S5_PAIRS, 2 * p, 2 * t * hc)

    v1 = jnp.concatenate([v_pair(kd) for kd in vkinds], axis=1)
    v = jnp.concatenate([v1, v1], axis=1).astype(BF16)

    def lanes(z):
        return jnp.tile(z.reshape(1, g * p), (1, batch))

    at = apow[..., t]
    a_tab = jnp.concatenate([lanes(jnp.real(at[0])), lanes(jnp.imag(at[0])),
                             lanes(jnp.real(at[1])), lanes(jnp.imag(at[1]))], axis=0)
    dvec = jnp.tile(d_skip.astype(F32).reshape(S5_PAIRS, 1, 2 * hc), (1, t, 1)).reshape(S5_PAIRS, 1, 2 * t * hc)
    return dict(strip=strip, we=we, v=v, a_tab=a_tab, dvec=dvec)


def _s5_pack_kernel(xa_ref, xb_ref, u_ref, *, n_chunks):
    per_half = S5_PAIRS // 2
    for half, x_ref in enumerate((xa_ref, xb_ref)):
        rows = [x_ref[pl.ds(tau, n_chunks, stride=S5_CHUNK), :] for tau in range(S5_CHUNK)]
        for qq in range(per_half):
            pieces = [r[:, qq * 32:(qq + 1) * 32] for r in rows]
            u_ref[half * per_half + qq] = jnp.concatenate(pieces, axis=-1).astype(BF16)


def _s5_unpack_kernel(y_ref, oa_ref, ob_ref, *, n_chunks):
    per_half = S5_PAIRS // 2
    for half, o_ref in enumerate((oa_ref, ob_ref)):
        ys = [y_ref[half * per_half + qq].astype(F32) for qq in range(per_half)]
        for t in range(S5_CHUNK):
            pieces = [y[:, t * 32:(t + 1) * 32] for y in ys]
            o_ref[pl.ds(t, n_chunks, stride=S5_CHUNK), :] = jnp.concatenate(pieces, axis=-1)


def _s5_pack(sa, sb, batch):
    n_chunks = sa.shape[0] // batch // S5_CHUNK
    rows = n_chunks * S5_CHUNK
    cols = 2 * S5_CHUNK * S5_GROUP_CH
    half = pl.BlockSpec((rows, 128), lambda b: (b, 0))
    return pl.pallas_call(
        functools.partial(_s5_pack_kernel, n_chunks=n_chunks),
        grid=(batch,),
        in_specs=[half, half],
        out_specs=pl.BlockSpec((S5_PAIRS, None, n_chunks, cols), lambda b: (0, b, 0, 0)),
        out_shape=jax.ShapeDtypeStruct((S5_PAIRS, batch, n_chunks, cols), BF16),
        compiler_params=_cparams("arbitrary"),
        name="s5_pack",
    )(sa, sb)


def _s5_unpack(y, batch):
    n_chunks = y.shape[2]
    rows = n_chunks * S5_CHUNK
    cols = y.shape[3]
    half = pl.BlockSpec((rows, 128), lambda b: (b, 0))
    return pl.pallas_call(
        functools.partial(_s5_unpack_kernel, n_chunks=n_chunks),
        grid=(batch,),
        in_specs=[pl.BlockSpec((S5_PAIRS, None, n_chunks, cols), lambda b: (0, b, 0, 0))],
        out_specs=[half, half],
        out_shape=[jax.ShapeDtypeStruct((batch * rows, 128), F32)] * 2,
        compiler_params=_cparams("arbitrary"),
        name="s5_unpack",
    )(y)


def _s5_e_kernel(ul_ref, uc_ref, we_ref, ref_, imf_, reb_, imb_):
    u = jnp.concatenate([ul_ref[...], uc_ref[...]], axis=0)
    e = _dot(u, we_ref[...])
    ref_[...] = e[:, 0:128]
    imf_[...] = e[:, 128:256]
    reb_[...] = e[:, 256:384]
    imb_[...] = e[:, 384:512]


def _s5_scan_kernel(a_ref, ref_, imf_, reb_, imb_, prf, pif, prb, pib, *, n_rows, n_ctx):
    afr = a_ref[0:1, :]
    afi = a_ref[1:2, :]
    abr = a_ref[2:3, :]
    abi = a_ref[3:4, :]
    zero = jnp.zeros_like(afr)

    n_lat = n_rows - n_ctx

    def body(s, carry):
        sfr, sfi, sbr, sbi = carry
        nf = jnp.where(s < n_ctx, n_lat + s, s - n_ctx)
        nb = n_rows - 1 - s
        prf[pl.ds(nf, 1), :] = sfr
        pif[pl.ds(nf, 1), :] = sfi
        prb[pl.ds(nb, 1), :] = sbr
        pib[pl.ds(nb, 1), :] = sbi
        efr = ref_[pl.ds(nf, 1), :]
        efi = imf_[pl.ds(nf, 1), :]
        ebr = reb_[pl.ds(nb, 1), :]
        ebi = imb_[pl.ds(nb, 1), :]
        nfr = afr * sfr - afi * sfi + efr
        nfi = afr * sfi + afi * sfr + efi
        nbr = abr * sbr - abi * sbi + ebr
        nbi = abr * sbi + abi * sbr + ebi
        return nfr, nfi, nbr, nbi

    lax.fori_loop(0, n_rows, body, (zero, zero, zero, zero))


def _s5_y_kernel(ul_ref, uc_ref, strip_ref, v_ref, d_ref, prf, pif, prb, pib, yl_ref, yc_ref, m_scr):
    width = 2 * S5_GROUP_CH
    cols = S5_CHUNK * width
    n_lat = yl_ref.shape[0]

    @pl.when(pl.program_id(1) == 0)
    def _():
        strip = strip_ref[...]
        for j in range(S5_CHUNK):
            off = (S5_CHUNK - 1 - j) * width
            win = strip if off == 0 else pltpu.roll(strip, 2 * cols - off, axis=1)
            m_scr[j * width:(j + 1) * width, :] = win[:, :cols].astype(BF16)

    u = jnp.concatenate([ul_ref[...], uc_ref[...]], axis=0)
    y_intra = _dot(u, m_scr[...])
    pcat = jnp.concatenate([prf[...], pif[...], prb[...], pib[...]], axis=-1)
    hi, lo = _split_bf16(pcat)
    y_cross = _dot(jnp.concatenate([hi, lo], axis=-1), v_ref[...])
    y = y_intra + y_cross + d_ref[...] * u.astype(F32)
    yl_ref[...] = y[:n_lat].astype(BF16)
    yc_ref[...] = y[n_lat:].astype(BF16)


def _s5_core(ul, uc, tabs, layer, batch):
    n_lat, n_ctx = ul.shape[2], uc.shape[2]
    n_rows = n_lat + n_ctx
    width = batch * S5_PAIRS * 128
    cols = 2 * S5_CHUNK * S5_GROUP_CH
    ul_spec = pl.BlockSpec((None, None, n_lat, cols), lambda q, b: (q, b, 0, 0))
    uc_spec = pl.BlockSpec((None, None, n_ctx, cols), lambda q, b: (q, b, 0, 0))
    st_spec = pl.BlockSpec((n_rows, 128), lambda q, b: (0, b * S5_PAIRS + q))
    st_shape = jax.ShapeDtypeStruct((n_rows, width), F32)
    e4 = pl.pallas_call(
        _s5_e_kernel,
        grid=(S5_PAIRS, batch),
        in_specs=[ul_spec, uc_spec, pl.BlockSpec((None, None, cols, 512), lambda q, b: (layer, q, 0, 0))],
        out_specs=[st_spec] * 4,
        out_shape=[st_shape] * 4,
        compiler_params=_cparams("arbitrary", "arbitrary"),
        name="s5_chunk_states",
    )(ul, uc, tabs['we'])
    p4 = pl.pallas_call(
        functools.partial(_s5_scan_kernel, n_rows=n_rows, n_ctx=n_ctx),
        out_shape=[st_shape] * 4,
        compiler_params=pltpu.CompilerParams(vmem_limit_bytes=VMEM_LIMIT_BYTES),
        name="s5_state_scan",
    )(tabs['a_tab'][layer], *e4)
    y = pl.pallas_call(
        _s5_y_kernel,
        grid=(S5_PAIRS, batch),
        in_specs=[
            ul_spec, uc_spec,
            pl.BlockSpec((None, None, 2 * S5_GROUP_CH, 2 * cols), lambda q, b: (layer, q, 0, 0)),
            pl.BlockSpec((None, None, cols, cols), lambda q, b: (layer, q, 0, 0)),
            pl.BlockSpec((None, None, 1, cols), lambda q, b: (layer, q, 0, 0)),
            st_spec, st_spec, st_spec, st_spec,
        ],
        out_specs=[ul_spec, uc_spec],
        out_shape=[
            jax.ShapeDtypeStruct((S5_PAIRS, batch, n_lat, cols), BF16),
            jax.ShapeDtypeStruct((S5_PAIRS, batch, n_ctx, cols), BF16),
        ],
        scratch_shapes=[pltpu.VMEM((cols, cols), BF16)],
        compiler_params=_cparams("arbitrary", "arbitrary"),
        name="s5_outputs",
    )(ul, uc, tabs['strip'], tabs['v'], tabs['dvec'], *p4)
    return y


def _s5_mixer(s_lat, s_ctx, tabs, layer, batch):
    ul = _s5_pack(*s_lat, batch)
    uc = _s5_pack(*s_ctx, batch)
    yl, yc = _s5_core(ul, uc, tabs, layer, batch)
    return _s5_unpack(yl, batch), _s5_unpack(yc, batch)


def _ret_tables(ret_decay):
    c = RET_CHUNK
    lg = jax.nn.log_sigmoid(ret_decay.astype(F32))
    lane_h = np.repeat(np.arange(RET_HEADS), RET_DIM)
    lgl = jnp.repeat(lg, RET_DIM, axis=1)
    pos = jnp.arange(c, dtype=F32)[:, None]
    qd = jnp.stack([jnp.exp((pos + 1.0) * lgl[0][None]), jnp.exp((c - pos) * lgl[1][None])])
    kd = jnp.stack([jnp.exp((c - 1.0 - pos) * lgl[0][None]), jnp.exp(pos * lgl[1][None])])
    bmask = jnp.asarray((lane_h[:, None] == lane_h[None, :]).astype(np.float32))
    cd = jnp.exp(c * lgl)[:, :, None] * bmask[None]
    diff = pos - pos.T
    dm = []
    for h in range(RET_HEADS):
        fw = jnp.where(diff >= 0, jnp.exp(jnp.maximum(diff, 0.0) * lg[0, h]), 0.0)
        bw = jnp.where(diff <= 0, jnp.exp(jnp.maximum(-diff, 0.0) * lg[1, h]), 0.0)
        dm.append(fw + bw)
    dm = jnp.concatenate(dm, axis=0)
    return dict(qd=qd, kd=kd, cd=cd, dm=dm)


def _ret_masks():
    lane_h = np.repeat(np.arange(RET_HEADS), RET_DIM)
    bmask = (lane_h[:, None] == lane_h[None, :]).astype(np.float32)
    hmask = (np.arange(RET_HEADS)[:, None] == lane_h[None, :]).astype(np.float32)
    return jnp.asarray(bmask), jnp.asarray(hmask)


def _rope_tables(n_tokens):
    t = np.arange(n_tokens)
    row = (t // GRID_W).astype(np.float64)
    col = (t % GRID_W).astype(np.float64)
    n_freq = RET_DIM // 4
    inv_freq = 1.0 / (ROPE_BASE ** (np.arange(n_freq, dtype=np.float64) / n_freq))
    ang = np.concatenate([row[:, None] * inv_freq, col[:, None] * inv_freq], axis=-1)
    cos = np.cos(ang)
    sin = np.sin(ang)
    cos_t = np.tile(np.concatenate([cos, cos], axis=-1), (1, RET_HEADS))
    sin_t = np.tile(np.concatenate([-sin, sin], axis=-1), (1, RET_HEADS))
    half = RET_DIM // 2
    perm = np.arange(BRANCH_W) ^ half
    swap = np.zeros((BRANCH_W, BRANCH_W), np.float32)
    swap[perm, np.arange(BRANCH_W)] = 1.0
    return jnp.asarray(cos_t, F32), jnp.asarray(sin_t, F32), jnp.asarray(swap, BF16)


def _ret_chunk(q, k, v, s, qd, kd, cd, bmask, dm, hmask, with_intra):
    cross = _dot((q * qd).astype(BF16), s.astype(BF16))
    s_new = cd * s + bmask * _dot_tn((k * kd).astype(BF16), v)
    if not with_intra:
        return cross, s_new
    qb = q.astype(BF16)
    kb = k.astype(BF16)
    qs = jnp.concatenate([qb * hmask[h:h + 1].astype(BF16) for h in range(RET_HEADS)], axis=0)
    scores = _dot_nt(qs, kb) * dm
    ov = _dot(scores.astype(BF16), v)
    c = q.shape[0]
    inner = ov[0:c] * hmask[0:1]
    for h in range(1, RET_HEADS):
        inner = inner + ov[h * c:(h + 1) * c] * hmask[h:h + 1]
    return inner + cross, s_new


def _ret_kernel(qf_ref, kf_ref, vf_ref, qb_ref, kb_ref, vb_ref, qc_ref, kc_ref, vc_ref,
                cosf_ref, sinf_ref, cosb_ref, sinb_ref, swap_ref,
                qd_ref, kd_ref, cd_ref, bm_ref, dm_ref, hm_ref,
                of_ref, ob_ref, ocf_ref, ocb_ref, sf_scr, sb_scr, *, n_chunks, n_ctx_chunks):
    i = pl.program_id(1)
    c = RET_CHUNK
    k_scale = RET_DIM ** -0.5
    bmask = bm_ref[...]
    dm = dm_ref[...]
    hmask = hm_ref[...]
    tabs = [(qd_ref[d], kd_ref[d], cd_ref[d]) for d in range(2)]

    @pl.when(i == 0)
    def _():
        for d, oc_ref, s_scr in ((0, ocf_ref, sf_scr), (1, ocb_ref, sb_scr)):
            qd, kd, cd = tabs[d]
            s = jnp.zeros((BRANCH_W, BRANCH_W), F32)
            order = range(n_ctx_chunks) if d == 0 else range(n_ctx_chunks - 1, -1, -1)
            for cc in order:
                sl = slice(cc * c, (cc + 1) * c)
                o, s = _ret_chunk(qc_ref[sl, :].astype(F32), kc_ref[sl, :].astype(F32) * k_scale, vc_ref[sl, :],
                                  s, qd, kd, cd, bmask, dm, hmask, d == 0)
                oc_ref[sl, :] = o
            s_scr[...] = s

    swap = swap_ref[...]

    def rope(x_ref, cos_ref, sin_ref):
        xb = x_ref[...]
        return xb.astype(F32) * cos_ref[...] + _dot(xb, swap) * sin_ref[...]

    q_f = rope(qf_ref, cosf_ref, sinf_ref)
    k_f = rope(kf_ref, cosf_ref, sinf_ref) * k_scale
    q_b = rope(qb_ref, cosb_ref, sinb_ref)
    k_b = rope(kb_ref, cosb_ref, sinb_ref) * k_scale
    sf = sf_scr[...]
    sb = sb_scr[...]
    for step in range(n_chunks):
        sl = slice(step * c, (step + 1) * c)
        o, sf = _ret_chunk(q_f[sl], k_f[sl], vf_ref[sl, :], sf, *tabs[0], bmask, dm, hmask, True)
        of_ref[sl, :] = o
        cb = n_chunks - 1 - step
        sl = slice(cb * c, (cb + 1) * c)
        o, sb = _ret_chunk(q_b[sl], k_b[sl], vb_ref[sl, :], sb, *tabs[1], bmask, dm, hmask, False)
        ob_ref[sl, :] = o
    sf_scr[...] = sf
    sb_scr[...] = sb


def _retention(proj_l, proj_c, tabs, layer, masks, rope, batch, seq_len, ctx_len):
    n_chunks = 8
    blk = n_chunks * RET_CHUNK
    nblk = seq_len // blk
    cos_t, sin_t, swap = rope

    def lat(col, back):
        if back:
            return pl.BlockSpec((blk, BRANCH_W), lambda b, i: (b * nblk + nblk - 1 - i, col))
        return pl.BlockSpec((blk, BRANCH_W), lambda b, i: (b * nblk + i, col))

    def ctx(col):
        return pl.BlockSpec((ctx_len, BRANCH_W), lambda b, i: (b, col))

    def const(shape):
        return pl.BlockSpec(shape, lambda b, i: (0,) * len(shape))

    def per_layer(shape):
        return pl.BlockSpec((None,) + shape, lambda b, i: (layer,) + (0,) * len(shape))

    tab_f = pl.BlockSpec((blk, BRANCH_W), lambda b, i: (i, 0))
    tab_b = pl.BlockSpec((blk, BRANCH_W), lambda b, i: (nblk - 1 - i, 0))
    kern = functools.partial(_ret_kernel, n_chunks=n_chunks, n_ctx_chunks=ctx_len // RET_CHUNK)
    c = RET_CHUNK
    ctx_out = pl.BlockSpec((ctx_len, BRANCH_W), lambda b, i: (b, 0))
    o_f, o_b, oc_f, oc_b = pl.pallas_call(
        kern,
        grid=(batch, nblk),
        in_specs=[
            lat(COL_RQ, False), lat(COL_RK, False), lat(COL_RV, False),
            lat(COL_RQ, True), lat(COL_RK, True), lat(COL_RV, True),
            ctx(COL_RQ), ctx(COL_RK), ctx(COL_RV),
            tab_f, tab_f, tab_b, tab_b, const((BRANCH_W, BRANCH_W)),
            per_layer((2, c, BRANCH_W)), per_layer((2, c, BRANCH_W)), per_layer((2, BRANCH_W, BRANCH_W)),
            const((BRANCH_W, BRANCH_W)), per_layer((RET_HEADS * c, c)), const((RET_HEADS, BRANCH_W)),
        ],
        out_specs=[lat(0, False), lat(0, True), ctx_out, ctx_out],
        out_shape=[
            jax.ShapeDtypeStruct((batch * seq_len, BRANCH_W), F32),
            jax.ShapeDtypeStruct((batch * seq_len, BRANCH_W), F32),
            jax.ShapeDtypeStruct((batch * ctx_len, BRANCH_W), F32),
            jax.ShapeDtypeStruct((batch * ctx_len, BRANCH_W), F32),
        ],
        scratch_shapes=[pltpu.VMEM((BRANCH_W, BRANCH_W), F32), pltpu.VMEM((BRANCH_W, BRANCH_W), F32)],
        compiler_params=_cparams("arbitrary", "arbitrary"),
        name="retention",
    )(proj_l, proj_l, proj_l, proj_l, proj_l, proj_l, proj_c, proj_c, proj_c,
      cos_t, sin_t, cos_t, sin_t, swap,
      tabs['qd'], tabs['kd'], tabs['cd'], masks[0], tabs['dm'], masks[1])
    return (o_f, o_b), (oc_f, oc_b)


def _na_tables(rpb):
    kr, kw = NA_WIN_ROWS, NA_WIN_COLS
    col = np.arange(GRID_W)
    col_start = np.clip(col - kw // 2, 0, GRID_W - kw)
    in_win = (col[None, :] >= col_start[:, None]) & (col[None, :] < col_start[:, None] + kw)
    dc = np.clip(col[None, :] - col[:, None], -(kw - 1), kw - 1) + (kw - 1)
    pick_c = (dc[:, :, None] == np.arange(2 * kw - 1)[None, None, :]).astype(np.float32)
    by = jnp.einsum('hrc,qkc->hqrk', rpb.astype(F32), jnp.asarray(pick_c), precision=lax.Precision.HIGHEST)
    by = jnp.where(jnp.asarray(in_win)[None, :, None, :], by, NEG_BIG)
    bias = jnp.stack([by[:, :, v:v + kr, :] for v in range(kr)], axis=0)
    return bias.reshape(kr, NA_HEADS * GRID_W, kr * GRID_W)


def _na_head_mask():
    lane_h = np.repeat(np.arange(NA_HEADS), NA_DIM)
    hmask = (np.arange(NA_HEADS)[:, None] == lane_h[None, :]).astype(np.float32)
    return jnp.asarray(hmask, F32)


def _attend(qs, keys, vals, bias, kc, vc):
    s_ctx = _dot_nt(qs, kc)
    m = jnp.max(s_ctx, axis=-1, keepdims=True)
    if keys is not None:
        s_band = _dot_nt(qs, keys) + bias
        m = jnp.maximum(m, jnp.max(s_band, axis=-1, keepdims=True))
        p_band = jnp.exp(s_band - m)
    p_ctx = jnp.exp(s_ctx - m)
    l = jnp.sum(p_ctx, axis=-1, keepdims=True)
    o = _dot(p_ctx.astype(BF16), vc)
    if keys is not None:
        l = l + jnp.sum(p_band, axis=-1, keepdims=True)
        o = o + _dot(p_band.astype(BF16), vals)
    return o / l


def _stack_heads(q, hmask_scaled):
    return jnp.concatenate([q * hmask_scaled[h:h + 1] for h in range(NA_HEADS)], axis=0)


def _unstack_heads(o, hmask, n):
    out = o[0:n] * hmask[0:1]
    for h in range(1, NA_HEADS):
        out = out + o[h * n:(h + 1) * n] * hmask[h:h + 1]
    return out


def _na_kernel(q_ref, k_ref, v_ref, kc_ref, vc_ref, bias_ref, hm_ref, o_ref, *, n_grid_rows):
    i = pl.program_id(1)
    hmask = hm_ref[...]
    hms = (hmask * (NA_DIM ** -0.5)).astype(BF16)
    kc = kc_ref[...]
    vc = vc_ref[...]
    band = NA_WIN_ROWS * GRID_W
    for rr in range(NA_QROWS):
        r = i * NA_QROWS + rr
        rs = jnp.clip(r - NA_WIN_ROWS // 2, 0, n_grid_rows - NA_WIN_ROWS)
        var = rs - r + (NA_WIN_ROWS - 1)
        start = pl.multiple_of(rs * GRID_W, GRID_W)
        keys = k_ref[pl.ds(start, band), :]
        vals = v_ref[pl.ds(start, band), :]
        qs = _stack_heads(q_ref[rr * GRID_W:(rr + 1) * GRID_W, :], hms)
        o = _attend(qs, keys, vals, bias_ref[var], kc, vc)
        o_ref[rr * GRID_W:(rr + 1) * GRID_W, :] = _unstack_heads(o, hmask, GRID_W).astype(BF16)


def _na_ctx_kernel(q_ref, kc_ref, vc_ref, hm_ref, o_ref):
    hmask = hm_ref[...]
    hms = (hmask * (NA_DIM ** -0.5)).astype(BF16)
    n = q_ref.shape[0]
    o = _attend(_stack_heads(q_ref[...], hms), None, None, None, kc_ref[...], vc_ref[...])
    o_ref[...] = _unstack_heads(o, hmask, n).astype(BF16)


def _neighborhood(proj_l, proj_c, bias, layer, hmask, batch, seq_len, ctx_len, need_ctx_out):
    rows = seq_len // GRID_W
    qblk = NA_QROWS * GRID_W
    nq = seq_len // qblk
    out_l = pl.pallas_call(
        functools.partial(_na_kernel, n_grid_rows=rows),
        grid=(batch, nq),
        in_specs=[
            pl.BlockSpec((qblk, BRANCH_W), lambda b, i: (b * nq + i, COL_NQ)),
            pl.BlockSpec((seq_len, BRANCH_W), lambda b, i: (b, COL_NK)),
            pl.BlockSpec((seq_len, BRANCH_W), lambda b, i: (b, COL_NV)),
            pl.BlockSpec((ctx_len, BRANCH_W), lambda b, i: (b, COL_NK)),
            pl.BlockSpec((ctx_len, BRANCH_W), lambda b, i: (b, COL_NV)),
            pl.BlockSpec((None,) + bias.shape[1:], lambda b, i: (layer, 0, 0, 0)),
            pl.BlockSpec(hmask.shape, lambda b, i: (0, 0)),
        ],
        out_specs=pl.BlockSpec((qblk, BRANCH_W), lambda b, i: (b * nq + i, 0)),
        out_shape=jax.ShapeDtypeStruct((batch * seq_len, BRANCH_W), BF16),
        compiler_params=_cparams("arbitrary", "arbitrary"),
        name="neighborhood_attn",
    )(proj_l, proj_l, proj_l, proj_c, proj_c, bias, hmask)
    out_c = None
    if need_ctx_out:
        out_c = pl.pallas_call(
            _na_ctx_kernel,
            grid=(batch,),
            in_specs=[
                pl.BlockSpec((ctx_len, BRANCH_W), lambda b: (b, COL_NQ)),
                pl.BlockSpec((ctx_len, BRANCH_W), lambda b: (b, COL_NK)),
                pl.BlockSpec((ctx_len, BRANCH_W), lambda b: (b, COL_NV)),
                pl.BlockSpec(hmask.shape, lambda b: (0, 0)),
            ],
            out_specs=pl.BlockSpec((ctx_len, BRANCH_W), lambda b: (b, 0)),
            out_shape=jax.ShapeDtypeStruct((batch * ctx_len, BRANCH_W), BF16),
            compiler_params=_cparams("arbitrary"),
            name="context_attn",
        )(proj_c, proj_c, proj_c, hmask)
    return out_l, out_c


def _merge_kernel(x_ref, mod_ref, g_ref, gt0, gt1, gt2, gt3, fa_ref, fb_ref, s5a_ref, s5b_ref, rof_ref, rob_ref, rg_ref, na_ref,
                  wglu_ref, bglu_ref, gn_ref, avg_ref, wb_ref, wo_ref, o_ref, *, tiles_per_mod, mod_base):
    i = pl.program_id(0)
    _, _, gate_a = _mod_rows(mod_ref, i, tiles_per_mod, mod_base, 0)
    z = _gelu_tanh(jnp.concatenate([s5a_ref[...], s5b_ref[...]], axis=-1)).astype(BF16)
    zf = z.astype(F32)
    b_s5 = (zf * _sigmoid(_dot(z, wglu_ref[...]) + bglu_ref[...])).astype(BF16)
    o = rof_ref[...] + rob_ref[...]
    avg = avg_ref[...]
    hi, lo = _split_bf16(o)
    mu = _dot(hi, avg) + _dot(lo, avg)
    dlt = o - mu
    hi, lo = _split_bf16(dlt * dlt)
    var = _dot(hi, avg) + _dot(lo, avg)
    hn = dlt * lax.rsqrt(var + EPS) * gn_ref[...]
    b_ret = (_silu(rg_ref[...].astype(F32)) * hn).astype(BF16)
    b_fnet = jnp.concatenate([fa_ref[...], fb_ref[...]], axis=-1).astype(BF16)
    outs = (b_fnet, b_s5, b_ret, na_ref[...])
    gates = (gt0, gt1, gt2, gt3)
    y = (1.0 + jnp.tanh(gates[0][...].astype(F32))) * _dot(outs[0], wb_ref[0])
    for b in range(1, N_BRANCH):
        y = y + (1.0 + jnp.tanh(gates[b][...].astype(F32))) * _dot(outs[b], wb_ref[b])
    yo = _dot(y.astype(BF16), wo_ref[...])
    o_ref[...] = x_ref[...] + gate_a * _rms(yo, g_ref[...])


def _merge(x, mod, g1, proj, a, s5y, ret_o, na, lw, *, rows_per_mod, mod_base):
    rows, d = x.shape
    tm = min(512, rows)
    nt = rows // tm

    def row(shape, col=0):
        return pl.BlockSpec(shape, lambda i: (i, col))

    def const(arr):
        return pl.BlockSpec(arr.shape, lambda i: (0,) * arr.ndim)

    kern = functools.partial(_merge_kernel, tiles_per_mod=max(rows_per_mod // tm, 1), mod_base=mod_base)
    ins = [x, mod, g1.reshape(1, d), proj, proj, proj, proj, a[0], a[1], s5y[0], s5y[1], ret_o[0], ret_o[1], proj, na,
           lw['w_glu'], lw['b_glu'], lw['ret_gn'], lw['avg'], lw['w_branch'], lw['w_out']]
    specs = [
        row((tm, d)), const(mod), pl.BlockSpec((1, d), lambda i: (0, 0)),
        row((tm, d), 0), row((tm, d), 1), row((tm, d), 2), row((tm, d), 3),
        row((tm, 128)), row((tm, 128)), row((tm, 128)), row((tm, 128)),
        row((tm, BRANCH_W)), row((tm, BRANCH_W)),
        row((tm, BRANCH_W), COL_RG), row((tm, BRANCH_W)),
        const(lw['w_glu']), const(lw['b_glu']), const(lw['ret_gn']), const(lw['avg']),
        const(lw['w_branch']), const(lw['w_out']),
    ]
    return pl.pallas_call(
        kern,
        grid=(nt,),
        in_specs=specs,
        out_specs=row((tm, d)),
        out_shape=jax.ShapeDtypeStruct((rows, d), F32),
        compiler_params=_cparams("arbitrary"),
        name="merge_out",
    )(*ins)


def _ffn_kernel(x_ref, mod_ref, g2_ref, g3_ref, wg_ref, wu_ref, wd_ref, o_ref, *, tiles_per_mod, mod_base):
    i = pl.program_id(0)
    sh, sc, gate_f = _mod_rows(mod_ref, i, tiles_per_mod, mod_base, 3)
    x = x_ref[...]
    h = (_rms(x, g2_ref[...]) * (1.0 + sc) + sh).astype(BF16)
    act = (_silu(_dot(h, wg_ref[...])) * _dot(h, wu_ref[...])).astype(BF16)
    y = _dot(act, wd_ref[...])
    o_ref[...] = x + gate_f * _rms(y, g3_ref[...])


def _ffn_dense(x, mod, g2, g3, wg, wu, wd, *, rows_per_mod, mod_base):
    rows, d = x.shape
    d_ff = wg.shape[1]
    tm = min(512, rows)
    kern = functools.partial(_ffn_kernel, tiles_per_mod=max(rows_per_mod // tm, 1), mod_base=mod_base)

    def resident(shape):
        return pl.BlockSpec(shape, lambda i: (0, 0), pipeline_mode=pl.Buffered(1))

    return pl.pallas_call(
        kern,
        grid=(rows // tm,),
        in_specs=[
            pl.BlockSpec((tm, d), lambda i: (i, 0)),
            pl.BlockSpec(mod.shape, lambda i: (0, 0)),
            pl.BlockSpec((1, d), lambda i: (0, 0)),
            pl.BlockSpec((1, d), lambda i: (0, 0)),
            resident((d, d_ff)), resident((d, d_ff)), resident((d_ff, d)),
        ],
        out_specs=pl.BlockSpec((tm, d), lambda i: (i, 0)),
        out_shape=jax.ShapeDtypeStruct((rows, d), F32),
        compiler_params=_cparams("arbitrary"),
        name="ffn_dense",
    )(x, mod, g2.reshape(1, d), g3.reshape(1, d), wg, wu, wd)


def _router_kernel(x_ref, mod_ref, g2_ref, wr_ref, br_ref, tri_ref, h_ref, comb_ref, plan_ref, cnt_ref, cnt_scr,
                   *, tiles_per_mod, mod_base):
    i = pl.program_id(0)

    @pl.when(i == 0)
    def _():
        cnt_scr[...] = jnp.zeros_like(cnt_scr)

    sh, sc, _ = _mod_rows(mod_ref, i, tiles_per_mod, mod_base, 3)
    h = _rms(x_ref[...], g2_ref[...]) * (1.0 + sc) + sh
    h_ref[...] = _pack_pairs(h)
    h_hi, h_lo = _split_bf16(h)
    w_hi, w_lo = _split_bf16(wr_ref[...])
    logits = _dot(h_hi, w_hi) + _dot(h_lo, w_hi) + _dot(h_hi, w_lo) + br_ref[...]
    lane = lax.broadcasted_iota(jnp.int32, logits.shape, 1)
    v1 = jnp.max(logits, axis=-1, keepdims=True)
    i1 = jnp.min(jnp.where(logits == v1, lane, 128), axis=-1, keepdims=True)
    rest = jnp.where(lane == i1, NEG_BIG, logits)
    v2 = jnp.max(rest, axis=-1, keepdims=True)
    i2 = jnp.min(jnp.where(rest == v2, lane, 128), axis=-1, keepdims=True)
    e = jnp.exp(v2 - v1)
    w1 = 1.0 / (1.0 + e)
    w2 = e / (1.0 + e)
    meta = jnp.where(lane == 0, i1.astype(F32), 0.0) + jnp.where(lane == 1, i2.astype(F32), 0.0)
    meta = meta + jnp.where(lane == 2, w1, 0.0) + jnp.where(lane == 3, w2, 0.0)
    member = jnp.where((lane == i1) | (lane == i2), 1.0, 0.0)
    before = _dot(tri_ref[...], member.astype(BF16)) + cnt_scr[...]
    rank1 = jnp.sum(jnp.where(lane == i1, before, 0.0), axis=-1, keepdims=True)
    rank2 = jnp.sum(jnp.where(lane == i2, before, 0.0), axis=-1, keepdims=True)
    cnt_scr[...] += jnp.sum(member, axis=0, keepdims=True)
    cnt_ref[...] = cnt_scr[...]
    meta = meta + jnp.where(lane == 4, rank1, 0.0) + jnp.where(lane == 5, rank2, 0.0)
    comb_ref[...] = meta[:, :MOE_META_W]
    plan_ref[...] = meta.T[:MOE_META_W]


def _router(x, mod, g2, w_router, b_router, *, rows_per_mod, mod_base):
    rows, d = x.shape
    tm = min(512, rows)
    wr = jnp.zeros((d, 128), F32).at[:, :N_EXPERTS].set(w_router)
    br = jnp.full((1, 128), NEG_BIG, F32).at[0, :N_EXPERTS].set(b_router)
    tri = jnp.asarray(np.tril(np.ones((tm, tm), np.float32), -1), BF16)
    kern = functools.partial(_router_kernel, tiles_per_mod=max(rows_per_mod // tm, 1), mod_base=mod_base)
    return pl.pallas_call(
        kern,
        grid=(rows // tm,),
        in_specs=[
            pl.BlockSpec((tm, d), lambda i: (i, 0)),
            pl.BlockSpec(mod.shape, lambda i: (0, 0)),
            pl.BlockSpec((1, d), lambda i: (0, 0)),
            pl.BlockSpec((d, 128), lambda i: (0, 0)),
            pl.BlockSpec((1, 128), lambda i: (0, 0)),
            pl.BlockSpec((tm, tm), lambda i: (0, 0)),
        ],
        out_specs=[
            pl.BlockSpec((tm, d // 2), lambda i: (i, 0)),
            pl.BlockSpec((tm, MOE_META_W), lambda i: (i, 0)),
            pl.BlockSpec((MOE_META_W, tm), lambda i: (0, i)),
            pl.BlockSpec((1, 128), lambda i: (0, 0)),
        ],
        out_shape=[
            jax.ShapeDtypeStruct((rows, d // 2), jnp.int32),
            jax.ShapeDtypeStruct((rows, MOE_META_W), F32),
            jax.ShapeDtypeStruct((MOE_META_W, rows), F32),
            jax.ShapeDtypeStruct((1, 128), F32),
        ],
        scratch_shapes=[pltpu.VMEM((1, 128), F32)],
        compiler_params=_cparams("arbitrary"),
        name="moe_router",
    )(x, mod, g2.reshape(1, d), wr, br, tri)


def _sc_gather(table, idx):
    n_idx = idx.shape[0]
    width = table.shape[1]
    per_worker = n_idx // SC_WORKERS
    chunk_rows = math.gcd(per_worker, SC_GATHER_ROWS)
    n_chunks = per_worker // chunk_rows
    assert per_worker * SC_WORKERS == n_idx and chunk_rows % 8 == 0
    mesh = plsc.VectorSubcoreMesh(core_axis_name="c", subcore_axis_name="s")

    assert n_chunks % 2 == 0
    buf = [pltpu.VMEM((chunk_rows,), jnp.int32), pltpu.VMEM((chunk_rows, width), table.dtype),
           pltpu.SemaphoreType.DMA, pltpu.SemaphoreType.DMA]

    @functools.partial(
        pl.kernel, mesh=mesh,
        out_type=jax.ShapeDtypeStruct((n_idx, width), table.dtype),
        scratch_types=buf + buf,
        name="sc_row_gather",
    )
    def gather(table_hbm, idx_hbm, out_hbm, idx0, rows0, g0, w0, idx1, rows1, g1, w1):
        wid = lax.axis_index("s") * SC_CORES + lax.axis_index("c")
        base = wid * per_worker
        slots = ((idx0, rows0, g0, w0), (idx1, rows1, g1, w1))

        def fetch(j, slot):
            idx_v, rows_v, g, _ = slots[slot]
            pltpu.sync_copy(idx_hbm.at[pl.ds(base + j * chunk_rows, chunk_rows)], idx_v)
            pltpu.make_async_copy(table_hbm.at[idx_v], rows_v, g).start()

        def store(j, slot):
            idx_v, rows_v, g, w = slots[slot]
            pltpu.make_async_copy(table_hbm.at[idx_v], rows_v, g).wait()
            pltpu.make_async_copy(rows_v, out_hbm.at[pl.ds(base + j * chunk_rows, chunk_rows)], w).start()

        def drain(j, slot):
            _, rows_v, _, w = slots[slot]
            pltpu.make_async_copy(rows_v, out_hbm.at[pl.ds(base + j * chunk_rows, chunk_rows)], w).wait()

        fetch(0, 0)

        @pl.loop(0, n_chunks // 2)
        def _(jj):
            j = 2 * jj

            @pl.when(jj > 0)
            def _():
                drain(j - 1, 1)

            fetch(j + 1, 1)
            store(j, 0)

            @pl.when(j + 2 < n_chunks)
            def _():
                drain(j, 0)
                fetch(j + 2, 0)

            store(j + 1, 1)

        drain(n_chunks - 2, 0)
        drain(n_chunks - 1, 1)

    return gather(table, idx)


def _sc_scatter(table, idx, n_out):
    n_idx = idx.shape[0]
    rows, width = table.shape
    per_worker = n_idx // SC_WORKERS
    chunk_rows = math.gcd(per_worker, SC_GATHER_ROWS)
    n_chunks = per_worker // chunk_rows
    assert per_worker * SC_WORKERS == n_idx and chunk_rows % 8 == 0 and rows % per_worker == 0
    mesh = plsc.VectorSubcoreMesh(core_axis_name="c", subcore_axis_name="s")

    assert n_chunks % 2 == 0
    buf = [pltpu.VMEM((chunk_rows,), jnp.int32), pltpu.VMEM((chunk_rows, width), table.dtype),
           pltpu.SemaphoreType.DMA, pltpu.SemaphoreType.DMA]

    @functools.partial(
        pl.kernel, mesh=mesh,
        out_type=jax.ShapeDtypeStruct((n_out, width), table.dtype),
        scratch_types=buf + buf,
        name="sc_row_scatter",
    )
    def scatter(table_hbm, idx_hbm, out_hbm, idx0, rows0, l0, w0, idx1, rows1, l1, w1):
        wid = lax.axis_index("s") * SC_CORES + lax.axis_index("c")
        base = wid * per_worker
        slots = ((idx0, rows0, l0, w0), (idx1, rows1, l1, w1))

        def src(j):
            return table_hbm.at[pl.ds(lax.rem(base + j * chunk_rows, rows), chunk_rows)]

        def fetch(j, slot):
            idx_v, rows_v, l, _ = slots[slot]
            pltpu.sync_copy(idx_hbm.at[pl.ds(base + j * chunk_rows, chunk_rows)], idx_v)
            pltpu.make_async_copy(src(j), rows_v, l).start()

        def store(j, slot):
            idx_v, rows_v, l, w = slots[slot]
            pltpu.make_async_copy(src(j), rows_v, l).wait()
            pltpu.make_async_copy(rows_v, out_hbm.at[idx_v], w).start()

        def drain(slot):
            idx_v, rows_v, _, w = slots[slot]
            pltpu.make_async_copy(rows_v, out_hbm.at[idx_v], w).wait()

        fetch(0, 0)

        @pl.loop(0, n_chunks // 2)
        def _(jj):
            j = 2 * jj

            @pl.when(jj > 0)
            def _():
                drain(1)

            fetch(j + 1, 1)
            store(j, 0)

            @pl.when(j + 2 < n_chunks)
            def _():
                drain(0)
                fetch(j + 2, 0)

            store(j + 1, 1)

        drain(0)
        drain(1)

    return scatter(table, idx)


def _moe_plan(plan, counts_row, rows):
    tile = MOE_ROW_TILE
    n_tiles = (2 * rows) // tile + N_EXPERTS
    n_slots = n_tiles * tile
    counts = counts_row[0, :N_EXPERTS].astype(jnp.int32)
    padded = ((counts + tile - 1) // tile) * tile
    ends = jnp.cumsum(padded)
    starts = ends - padded
    ids = jnp.arange(N_EXPERTS, dtype=F32)[:, None]
    start_f = starts.astype(F32)[:, None]

    def slot(e_row, r_row):
        return jnp.sum(jnp.where(e_row[None, :] == ids, start_f, 0.0), axis=0) + r_row

    pos = jnp.concatenate([slot(plan[0], plan[4]), slot(plan[1], plan[5])])
    tile_start = jnp.arange(n_tiles, dtype=jnp.int32) * tile
    used = tile_start < ends[-1]
    tile_e = jnp.minimum(jnp.sum((tile_start[:, None] >= ends[None, :]).astype(jnp.int32), axis=1), N_EXPERTS - 1)
    last_e = jnp.max(jnp.where(used, tile_e, 0))
    tile_e = jnp.where(used, tile_e, last_e)
    valid_end = jnp.sum((tile_e[:, None] == jnp.arange(N_EXPERTS)[None, :]) * (starts + counts)[None, :], axis=1)
    n_valid = jnp.where(used, jnp.clip(valid_end - tile_start, 0, tile), 0).astype(jnp.int32)
    return pos.astype(jnp.int32), n_slots, tile_e.astype(jnp.int32), n_valid


def _moe_group_kernel(eid_ref, nval_ref, hs_ref, wg_ref, wu_ref, wd_ref, y_ref, acc_scr, *, n_f):
    w = pl.program_id(0)
    f = pl.program_id(1)
    nv = nval_ref[w]

    def run(n_rows):
        wg = wg_ref[...].astype(BF16)
        wu = wu_ref[...].astype(BF16)
        wd = wd_ref[...].astype(BF16)
        for r0 in range(0, n_rows, MOE_SUB_ROWS):
            rows = slice(r0, r0 + MOE_SUB_ROWS)
            hv = _unpack_pairs(hs_ref[rows, :])
            row = r0 + lax.broadcasted_iota(jnp.int32, hv.shape, 0)
            h = jnp.where(row < nv, hv, 0.0).astype(BF16)
            part = _dot((_silu(_dot(h, wg)) * _dot(h, wu)).astype(BF16), wd)
            acc = jnp.where(f == 0, 0.0, acc_scr[rows, :]) + part
            acc_scr[rows, :] = acc
            y_ref[rows, :] = _pack_pairs(acc)

    for groups in range(1, hs_ref.shape[0] // MOE_SUB_ROWS + 1):
        @pl.when((nv > (groups - 1) * MOE_SUB_ROWS) & (nv <= groups * MOE_SUB_ROWS))
        def _(groups=groups):
            run(groups * MOE_SUB_ROWS)


def _moe_grouped(hs, tile_e, n_valid, wg, wu, wd):
    n_slots = hs.shape[0]
    d = wg.shape[1]
    d_ff = wg.shape[2]
    tile = MOE_ROW_TILE
    tf = MOE_FF_TILE
    n_f = d_ff // tf

    def f_idx(f, nval, w):
        return jnp.where(nval[w] > 0, f, n_f - 1)

    grid_spec = pltpu.PrefetchScalarGridSpec(
        num_scalar_prefetch=2,
        grid=(n_slots // tile, n_f),
        in_specs=[
            pl.BlockSpec((tile, d // 2), lambda w, f, eid, nval: (w, 0)),
            pl.BlockSpec((None, d, tf), lambda w, f, eid, nval: (eid[w], 0, f_idx(f, nval, w))),
            pl.BlockSpec((None, d, tf), lambda w, f, eid, nval: (eid[w], 0, f_idx(f, nval, w))),
            pl.BlockSpec((None, tf, d), lambda w, f, eid, nval: (eid[w], f_idx(f, nval, w), 0)),
        ],
        out_specs=pl.BlockSpec((tile, d // 2), lambda w, f, eid, nval: (w, 0)),
        scratch_shapes=[pltpu.VMEM((tile, d), F32)],
    )
    return pl.pallas_call(
        functools.partial(_moe_group_kernel, n_f=n_f),
        grid_spec=grid_spec,
        out_shape=jax.ShapeDtypeStruct((n_slots, d // 2), jnp.int32),
        compiler_params=_cparams("arbitrary", "arbitrary"),
        name="moe_experts",
    )(tile_e, n_valid, hs, wg, wu, wd)


def _moe_out_kernel(x_ref, y1_ref, y2_ref, meta_ref, mod_ref, g3_ref, o_ref, *, tiles_per_mod, mod_base):
    i = pl.program_id(0)
    _, _, gate_f = _mod_rows(mod_ref, i, tiles_per_mod, mod_base, 3)
    meta = meta_ref[...]
    y = meta[:, 2:3] * _unpack_pairs(y1_ref[...]) + meta[:, 3:4] * _unpack_pairs(y2_ref[...])
    o_ref[...] = x_ref[...] + gate_f * _rms(y, g3_ref[...])


def _moe_combine(x, yg, meta, mod, g3, *, rows_per_mod, mod_base):
    rows, d = x.shape
    tm = math.gcd(1024, rows_per_mod)
    nt = rows // tm
    kern = functools.partial(_moe_out_kernel, tiles_per_mod=max(rows_per_mod // tm, 1), mod_base=mod_base)
    return pl.pallas_call(
        kern,
        grid=(nt,),
        in_specs=[
            pl.BlockSpec((tm, d), lambda i: (i, 0)),
            pl.BlockSpec((tm, d // 2), lambda i: (i, 0)),
            pl.BlockSpec((tm, d // 2), lambda i: (nt + i, 0)),
            pl.BlockSpec((tm, MOE_META_W), lambda i: (i, 0)),
            pl.BlockSpec(mod.shape, lambda i: (0, 0)),
            pl.BlockSpec((1, d), lambda i: (0, 0)),
        ],
        out_specs=pl.BlockSpec((tm, d), lambda i: (i, 0)),
        out_shape=jax.ShapeDtypeStruct((rows, d), F32),
        compiler_params=_cparams("arbitrary"),
        name="moe_combine",
    )(x, yg, yg, meta, mod, g3.reshape(1, d))


def _moe_sparse(x, routed, mod, g3, wg, wu, wd, *, rows_per_mod, mod_base):
    h, meta, plan, counts = routed
    rows = x.shape[0]
    pos, n_slots, tile_e, n_valid = _moe_plan(plan, counts, rows)
    hs = _sc_scatter(h, pos, n_slots)
    ys = _moe_grouped(hs, tile_e, n_valid, wg, wu, wd)
    yg = _sc_gather(ys, pos)
    return _moe_combine(x, yg, meta, mod, g3, rows_per_mod=rows_per_mod, mod_base=mod_base)


def _cast_kernel(w_ref, o_ref, *, scale):
    w = w_ref[...]
    o_ref[...] = (w if scale == 1.0 else w * scale).astype(BF16)


def _cast_bf16(w_stack, layer, scale=1.0):
    squeeze = w_stack.ndim == 3
    w4 = w_stack[:, None] if squeeze else w_stack
    _, n_e, k, n = w4.shape
    bk = min(k, 256)
    out = pl.pallas_call(
        functools.partial(_cast_kernel, scale=scale),
        grid=(n_e, k // bk),
        in_specs=[pl.BlockSpec((None, None, bk, n), lambda e, i: (layer, e, i, 0))],
        out_specs=pl.BlockSpec((None, bk, n), lambda e, i: (e, i, 0)),
        out_shape=jax.ShapeDtypeStruct((n_e, k, n), BF16),
        compiler_params=_cparams("arbitrary", "arbitrary"),
        name="cast_weights",
    )(w4)
    return out[0] if squeeze else out


def _permute_w_in(w_in_stack, layer):
    _, k, n = w_in_stack.shape
    n_blocks = n // BRANCH_W
    shift = 9
    n_gate_blocks = N_BRANCH * D_MODEL // BRANCH_W

    per_step = 5
    assert n_blocks % per_step == 0

    def permute_kernel(*refs):
        o_ref = refs[-1]
        for s, w_ref in enumerate(refs[:-1]):
            scale = jnp.where(pl.program_id(0) * per_step + s < n_gate_blocks, 0.5, 1.0)
            o_ref[:, s * BRANCH_W:(s + 1) * BRANCH_W] = (w_ref[...] * scale).astype(BF16)

    def src(s):
        return pl.BlockSpec((None, k, BRANCH_W), lambda j: (layer, 0, (j * per_step + s + shift) % n_blocks))

    return pl.pallas_call(
        permute_kernel,
        grid=(n_blocks // per_step,),
        in_specs=[src(s) for s in range(per_step)],
        out_specs=pl.BlockSpec((k, per_step * BRANCH_W), lambda j: (0, j)),
        out_shape=jax.ShapeDtypeStruct((k, n), BF16),
        compiler_params=_cparams("arbitrary"),
        name="cast_permute_w_in",
    )(*([w_in_stack] * per_step))


def kernel(x, c, ctx, c_ctx, w_mod, b_mod, norm_g, w_in, s5_a_re, s5_a_im, s5_log_dt, s5_b_re, s5_b_im, s5_c_re, s5_c_im, s5_d, s5_w_glu, s5_b_glu, ret_decay, ret_gn, na_rpb, w_branch, w_out, ffn_w_gate, ffn_w_up, ffn_w_down, moe_w_router, moe_b_router, moe_w_gate, moe_w_up, moe_w_down):
    batch, seq_len, d = x.shape
    ctx_len = ctx.shape[1]
    depth = w_mod.shape[0]
    cond = jnp.concatenate([c, c_ctx[None, :]], axis=0)
    mod_all = _modulation(cond, w_mod, b_mod)
    rope = _rope_tables(seq_len)
    lane_h = np.repeat(np.arange(RET_HEADS), RET_DIM)
    avg = jnp.asarray((lane_h[:, None] == lane_h[None, :]).astype(np.float32) / RET_DIM, BF16)

    xl = x.reshape(batch * seq_len, d)
    xc = ctx.reshape(batch * ctx_len, d)
    lat = dict(rows_per_mod=seq_len, mod_base=0)
    cxt = dict(rows_per_mod=batch * ctx_len, mod_base=batch)

    s5_tabs = jax.vmap(functools.partial(_s5_tables, batch=batch))(
        s5_a_re, s5_a_im, s5_log_dt, s5_b_re, s5_b_im, s5_c_re, s5_c_im, s5_d)
    ret_tabs = jax.vmap(_ret_tables)(ret_decay)
    ret_masks = _ret_masks()
    na_bias = jax.vmap(_na_tables)(na_rpb)
    na_hmask = _na_head_mask()

    for layer in range(depth):
        last = layer == depth - 1
        need_ctx = not last
        mod = mod_all[layer]
        ng = norm_g[layer]
        w_in_bf = _permute_w_in(w_in, layer)
        lw = dict(w_glu=s5_w_glu[layer].astype(BF16), b_glu=s5_b_glu[layer].reshape(1, BRANCH_W).astype(F32),
                  ret_gn=ret_gn[layer].reshape(1, BRANCH_W).astype(F32), avg=avg,
                  w_branch=_cast_bf16(w_branch, layer, 0.5), w_out=_cast_bf16(w_out, layer))

        proj_l, f_l, *s_in_l = _in_proj(xl, mod, ng[0], w_in_bf, **lat)
        proj_c, f_c, *s_in_c = _in_proj(xc, mod, ng[0], w_in_bf, **cxt)

        a_l = _fourier_latent(f_l, batch, seq_len)
        s_l, s_c = _s5_mixer(s_in_l, s_in_c, s5_tabs, layer, batch)
        r_l, r_c = _retention(proj_l, proj_c, ret_tabs, layer, ret_masks, rope, batch, seq_len, ctx_len)
        n_l, n_c = _neighborhood(proj_l, proj_c, na_bias, layer, na_hmask, batch, seq_len, ctx_len, need_ctx)

        xl = _merge(xl, mod, ng[1], proj_l, a_l, s_l, r_l, n_l, lw, **lat)
        if need_ctx:
            a_c = _fourier_ctx(f_c, batch, ctx_len)
            xc = _merge(xc, mod, ng[1], proj_c, a_c, s_c, r_c, n_c, lw, **cxt)

        i = layer // 2
        if layer % 2 == 0:
            wg, wu, wd = _cast_bf16(ffn_w_gate, i), _cast_bf16(ffn_w_up, i), _cast_bf16(ffn_w_down, i)
            xl = _ffn_dense(xl, mod, ng[2], ng[3], wg, wu, wd, **lat)
            if need_ctx:
                xc = _ffn_dense(xc, mod, ng[2], ng[3], wg, wu, wd, **cxt)
        else:
            wg, wu, wd = moe_w_gate[i], moe_w_up[i], moe_w_down[i]
            routed = _router(xl, mod, ng[2], moe_w_router[i], moe_b_router[i], **lat)
            xl = _moe_sparse(xl, routed, mod, ng[3], wg, wu, wd, **lat)
            if need_ctx:
                routed_c = _router(xc, mod, ng[2], moe_w_router[i], moe_b_router[i], **cxt)
                xc = _moe_sparse(xc, routed_c, mod, ng[3], wg, wu, wd, **cxt)
    return xl.reshape(batch, seq_len, d)
```

```python
import functools
import math

import numpy as np
import jax
import jax.numpy as jnp
from jax import lax
from jax.experimental import pallas as pl
from jax.experimental.pallas import tpu as pltpu
from jax.experimental.pallas import tpu_sc as plsc

F32 = jnp.float32
BF16 = jnp.bfloat16

D_MODEL = 1024
BRANCH_W = 256
N_BRANCH = 4
GRID_W = 64
FNET_GROUP_DIM = 64
S5_GROUP_CH = 16
S5_GROUPS = 16
S5_STATE = 64
S5_CHUNK = 32
S5_PAIRS = S5_GROUPS // 2
RET_HEADS = 4
RET_DIM = 64
RET_CHUNK = 128
NA_HEADS = 4
NA_DIM = 64
NA_WIN_ROWS = 8
NA_WIN_COLS = 16
NA_QROWS = 16
ROPE_BASE = 10000.0
N_EXPERTS = 8
EPS = 1e-6
FFT_N2 = 256
NEG_BIG = -1e30
VMEM_LIMIT_BYTES = 50 * 1024 * 1024
SC_CORES = 2
SC_SUBCORES = 16
SC_WORKERS = SC_CORES * SC_SUBCORES
SC_GATHER_ROWS = 64
MOE_ROW_TILE = 2048
MOE_SUB_ROWS = 512
MOE_FF_TILE = 512
MOE_META_W = 8

COL_F, COL_S, COL_RQ, COL_RK, COL_RV, COL_RG, COL_NQ, COL_NK, COL_NV = range(16, 25)
IN_W = 9 * BRANCH_W + N_BRANCH * D_MODEL
IN_TN = 1280
IN_F_TILE = (N_BRANCH * D_MODEL) // IN_TN
IN_F_OFF = N_BRANCH * D_MODEL - IN_F_TILE * IN_TN
IN_S_OFF = IN_F_OFF + BRANCH_W


def _cparams(*sem):
    return pltpu.CompilerParams(dimension_semantics=sem, vmem_limit_bytes=VMEM_LIMIT_BYTES)


def _sigmoid(v):
    return 0.5 * jnp.tanh(0.5 * v) + 0.5


def _silu(v):
    return v * _sigmoid(v)


def _gelu_tanh(v):
    return 0.5 * v * (1.0 + jnp.tanh(math.sqrt(2.0 / math.pi) * (v + 0.044715 * (v * v * v))))


def _rms(v, g):
    ms = jnp.mean(v * v, axis=-1, keepdims=True)
    return v * lax.rsqrt(ms + EPS) * g


def _split_bf16(v):
    hi = v.astype(BF16)
    lo = (v - hi.astype(F32)).astype(BF16)
    return hi, lo


def _pack_pairs(v):
    n = v.shape[1] // 2
    lo = lax.bitcast_convert_type(v[:, :n].astype(BF16).astype(F32), jnp.int32)
    hi = lax.bitcast_convert_type(v[:, n:].astype(BF16).astype(F32), jnp.int32)
    return (hi & -65536) | ((lo >> 16) & 65535)


def _unpack_pairs(w):
    lo = lax.bitcast_convert_type(w << 16, F32)
    hi = lax.bitcast_convert_type(w & -65536, F32)
    return jnp.concatenate([lo, hi], axis=-1)


def _dot(a, b):
    return jnp.dot(a, b, preferred_element_type=F32)


def _dot_nt(a, b):
    return lax.dot_general(a, b, (((1,), (1,)), ((), ())), preferred_element_type=F32)


def _dot_tn(a, b):
    return lax.dot_general(a, b, (((0,), (0,)), ((), ())), preferred_element_type=F32)


def _mod_kernel(ct_ref, w_ref, b_ref, o_ref, *, n_cond):
    ct = ct_ref[...]
    s = _silu(ct)
    w = w_ref[...]
    rows = [jnp.sum(w * s[:, r:r + 1], axis=0, keepdims=True) for r in range(n_cond)]
    rows.append(jnp.zeros((8 - n_cond, w.shape[1]), F32))
    o_ref[...] = jnp.concatenate(rows, axis=0) + b_ref[...]


def _modulation(cond, w_mod, b_mod):
    n_layers, d, n = w_mod.shape
    tn = 1536
    ct = jnp.zeros((8, d), F32).at[:cond.shape[0]].set(cond).T
    return pl.pallas_call(
        functools.partial(_mod_kernel, n_cond=cond.shape[0]),
        grid=(n_layers, n // tn),
        in_specs=[
            pl.BlockSpec((d, 8), lambda l, j: (0, 0)),
            pl.BlockSpec((None, d, tn), lambda l, j: (l, 0, j)),
            pl.BlockSpec((None, 1, tn), lambda l, j: (l, 0, j)),
        ],
        out_specs=pl.BlockSpec((None, 8, tn), lambda l, j: (l, 0, j)),
        out_shape=jax.ShapeDtypeStruct((n_layers, 8, n), F32),
        compiler_params=_cparams("arbitrary", "arbitrary"),
        name="adaln_mod",
    )(ct, w_mod, b_mod.reshape(n_layers, 1, n))


def _mod_rows(mod_ref, i, tiles_per_mod, mod_base, first):
    r = mod_base + i // tiles_per_mod
    return [mod_ref[pl.ds(r, 1), (first + k) * D_MODEL:(first + k + 1) * D_MODEL] for k in range(3)]


def _in_kernel(x_ref, mod_ref, g_ref, w_ref, proj_ref, f_ref, sa_ref, sb_ref, *, tiles_per_mod, mod_base):
    i = pl.program_id(0)
    sh, sc, _ = _mod_rows(mod_ref, i, tiles_per_mod, mod_base, 0)
    h = (_rms(x_ref[...], g_ref[...]) * (1.0 + sc) + sh).astype(BF16)
    for j in range(IN_W // IN_TN):
        res = _dot(h, w_ref[:, j * IN_TN:(j + 1) * IN_TN])
        proj_ref[:, j * IN_TN:(j + 1) * IN_TN] = res.astype(BF16)
        if j == IN_F_TILE:
            f_ref[...] = res[:, IN_F_OFF:IN_F_OFF + BRANCH_W].astype(BF16)
            sa_ref[...] = res[:, IN_S_OFF:IN_S_OFF + 128]
            sb_ref[...] = res[:, IN_S_OFF + 128:IN_S_OFF + 256]


def _in_proj(x, mod, g, w_bf, *, rows_per_mod, mod_base):
    rows, d = x.shape
    tm = math.gcd(512, rows_per_mod)
    kern = functools.partial(_in_kernel, tiles_per_mod=max(rows_per_mod // tm, 1), mod_base=mod_base)
    return pl.pallas_call(
        kern,
        grid=(rows // tm,),
        in_specs=[
            pl.BlockSpec((tm, d), lambda i: (i, 0)),
            pl.BlockSpec(mod.shape, lambda i: (0, 0)),
            pl.BlockSpec((1, d), lambda i: (0, 0)),
            pl.BlockSpec((d, IN_W), lambda i: (0, 0), pipeline_mode=pl.Buffered(1)),
        ],
        out_specs=[
            pl.BlockSpec((tm, IN_W), lambda i: (i, 0)),
            pl.BlockSpec((tm, BRANCH_W), lambda i: (i, 0)),
            pl.BlockSpec((tm, 128), lambda i: (i, 0)),
            pl.BlockSpec((tm, 128), lambda i: (i, 0)),
        ],
        out_shape=[
            jax.ShapeDtypeStruct((rows, IN_W), BF16),
            jax.ShapeDtypeStruct((rows, BRANCH_W), BF16),
            jax.ShapeDtypeStruct((rows, 128), F32),
            jax.ShapeDtypeStruct((rows, 128), F32),
        ],
        compiler_params=_cparams("arbitrary"),
        name="in_proj",
    )(x, mod, g.reshape(1, d), w_bf)


def _fft_a_kernel(x_ref, cs_ref, tc_ref, ts_ref, zr_ref, zi_ref, *, n1, n1p):
    y = _dot(cs_ref[...].astype(BF16), x_ref[...])
    yr = y[:n1]
    yi = y[n1p:n1p + n1]
    tc = tc_ref[...]
    ts = ts_ref[...]
    zr_ref[...] = (yr * tc + yi * ts).astype(BF16)
    zi_ref[...] = (yi * tc - yr * ts).astype(BF16)


def _fft_b_kernel(zr_ref, zi_ref, cs_ref, cc_ref, sc_ref, oa_ref, ob_ref, *, kb, n1, scale, has_imag):
    cs = cs_ref[...].astype(BF16)
    cc = cc_ref[...].astype(BF16)
    sc = sc_ref[...].astype(BF16)
    half = BRANCH_W // 2
    for kk in range(kb):
        a = _dot(cs, zr_ref[kk])
        if has_imag:
            b = _dot(cs, zi_ref[kk])
            xr = a[:FFT_N2] + b[FFT_N2:]
            xi = b[:FFT_N2] - a[FFT_N2:]
        else:
            xr = a[:FFT_N2]
            xi = -a[FFT_N2:]
        out = (_dot(xr.astype(BF16), cc) + _dot(xi.astype(BF16), sc)) * scale
        k1 = pl.program_id(1) * kb + kk
        oa_ref[pl.ds(k1, FFT_N2, stride=n1), :] = out[:, :half]
        ob_ref[pl.ds(k1, FFT_N2, stride=n1), :] = out[:, half:]


def _dft_tables(n):
    k = np.arange(n)
    ang = 2.0 * np.pi * ((k[:, None] * k[None, :]) % n) / n
    return np.cos(ang), np.sin(ang)


def _fft_b_call(zr, zi, n1, batch, seq_len, has_imag):
    c2, s2 = _dft_tables(FFT_N2)
    cs2 = jnp.asarray(np.concatenate([c2, s2], axis=0), F32)
    c64, s64 = _dft_tables(FNET_GROUP_DIM)
    eye = np.eye(BRANCH_W // FNET_GROUP_DIM)
    cc = jnp.asarray(np.kron(eye, c64), F32)
    sc = jnp.asarray(np.kron(eye, s64), F32)
    kb = min(8, n1)
    scale = 1.0 / math.sqrt(seq_len * FNET_GROUP_DIM)
    kern = functools.partial(_fft_b_kernel, kb=kb, n1=n1, scale=scale, has_imag=has_imag)
    zspec = pl.BlockSpec((None, kb, FFT_N2, BRANCH_W), lambda b, i: (b, i, 0, 0))
    half = pl.BlockSpec((seq_len, BRANCH_W // 2), lambda b, i: (b, 0))
    return pl.pallas_call(
        kern,
        grid=(batch, n1 // kb),
        in_specs=[
            zspec, zspec,
            pl.BlockSpec((2 * FFT_N2, FFT_N2), lambda b, i: (0, 0)),
            pl.BlockSpec((BRANCH_W, BRANCH_W), lambda b, i: (0, 0)),
            pl.BlockSpec((BRANCH_W, BRANCH_W), lambda b, i: (0, 0)),
        ],
        out_specs=[half, half],
        out_shape=[jax.ShapeDtypeStruct((batch * seq_len, BRANCH_W // 2), F32)] * 2,
        compiler_params=_cparams("arbitrary", "arbitrary"),
        name="fourier_stage_b",
    )(zr, zi, cs2, cc, sc)


def _fourier_latent(f, batch, seq_len):
    n1 = seq_len // FFT_N2
    wide = FFT_N2 * BRANCH_W
    c1, s1 = _dft_tables(n1)
    n1p = max(n1, 8)
    cs1 = np.zeros((2 * n1p, n1))
    cs1[:n1] = c1
    cs1[n1p:n1p + n1] = -s1
    k1 = np.arange(n1)[:, None]
    l2 = np.arange(FFT_N2)[None, :]
    tw = 2.0 * np.pi * (k1 * l2) / seq_len
    tc = jnp.asarray(np.repeat(np.cos(tw), BRANCH_W, axis=1), F32)
    ts = jnp.asarray(np.repeat(np.sin(tw), BRANCH_W, axis=1), F32)
    cw = min(8192, wide)
    xv = f.reshape(batch, n1, wide)
    spec = pl.BlockSpec((None, n1, cw), lambda b, j: (b, 0, j))
    tspec = pl.BlockSpec((n1, cw), lambda b, j: (0, j))
    zr, zi = pl.pallas_call(
        functools.partial(_fft_a_kernel, n1=n1, n1p=n1p),
        grid=(batch, wide // cw),
        in_specs=[spec, pl.BlockSpec((2 * n1p, n1), lambda b, j: (0, 0)), tspec, tspec],
        out_specs=[spec, spec],
        out_shape=[jax.ShapeDtypeStruct((batch, n1, wide), BF16)] * 2,
        compiler_params=_cparams("arbitrary", "arbitrary"),
        name="fourier_stage_a",
    )(xv, jnp.asarray(cs1, F32), tc, ts)
    zr = zr.reshape(batch, n1, FFT_N2, BRANCH_W)
    zi = zi.reshape(batch, n1, FFT_N2, BRANCH_W)
    return _fft_b_call(zr, zi, n1, batch, seq_len, True)


def _fourier_ctx(f, batch, ctx_len):
    assert ctx_len == FFT_N2
    z = f.reshape(batch, 1, FFT_N2, BRANCH_W)
    return _fft_b_call(z, z, 1, batch, ctx_len, False)


def _s5_tables(a_re, a_im, log_dt, b_re, b_im, c_re, c_im, d_skip, batch):
    t = S5_CHUNK
    g, p, hc = S5_GROUPS, S5_STATE, S5_GROUP_CH
    lam = lax.complex(a_re.astype(F32), a_im.astype(F32))
    dt = jnp.exp(log_dt.astype(F32))[..., None]
    ks = jnp.arange(t + 1, dtype=F32)
    apow = jnp.exp((lam * dt)[..., None] * ks)
    a_bar = apow[..., 1]
    b_bar = ((a_bar - 1.0) / lam)[..., None] * lax.complex(b_re.astype(F32), b_im.astype(F32))
    cm = lax.complex(c_re.astype(F32), c_im.astype(F32))
    lagv = jnp.arange(-(t - 1), t, dtype=F32)
    ldt = lam * dt
    pw_f = jnp.where(lagv >= 0, jnp.exp(ldt[0][..., None] * jnp.maximum(lagv, 0.0)), 0.0)
    pw_b = jnp.where(lagv <= 0, jnp.exp(ldt[1][..., None] * jnp.maximum(-lagv, 0.0)), 0.0)
    kfull = jnp.real(jnp.einsum('ghp,gpl,gpj->gjlh',
                                jnp.concatenate([cm[0], cm[1]], axis=-1),
                                jnp.concatenate([pw_f, pw_b], axis=1),
                                jnp.concatenate([b_bar[0], b_bar[1]], axis=1),
                                precision=lax.Precision.HIGHEST))
    kp = kfull.reshape(S5_PAIRS, 2, hc, 2 * t - 1, hc)
    same = jnp.eye(2, dtype=F32)
    strip = kp[:, :, :, :, None, :] * same[None, :, None, None, :, None]
    strip = strip.reshape(S5_PAIRS, 2 * hc, (2 * t - 1) * 2 * hc)
    strip = jnp.pad(strip, ((0, 0), (0, 0), (0, 2 * hc)))

    apow_kp = jnp.exp(ks[:, None] * ldt[:, :, None, :])
    b_hp = jnp.swapaxes(b_bar, -1, -2)
    wf = apow_kp[0][:, t - 1::-1, None, :] * b_hp[0][:, None, :, :]
    wb = apow_kp[1][:, :t, None, :] * b_hp[1][:, None, :, :]
    kinds = [jnp.real(wf), jnp.imag(wf), jnp.real(wb), jnp.imag(wb)]

    def we_pair(kd):
        k5 = jnp.moveaxis(kd.reshape(S5_PAIRS, 2, t, hc, p), 1, 2)
        rows = k5[:, :, :, :, None, :] * same[None, None, :, None, :, None]
        return rows.reshape(S5_PAIRS, 2 * t * hc, 2 * p)

    we = jnp.concatenate([we_pair(kd) for kd in kinds], axis=-1).astype(BF16)

    c_ph = jnp.swapaxes(cm, -1, -2)
    vf = apow[0][:, :, 1:t + 1, None] * c_ph[0][:, :, None, :]
    vb = apow[1][:, :, t:0:-1, None] * c_ph[1][:, :, None, :]
    vkinds = [jnp.real(vf), -jnp.imag(vf), jnp.real(vb), -jnp.imag(vb)]

    def v_pair(kd):
        k5 = kd.reshape(S5_PAIRS, 2, p, t, hc)
        rows = k5[:, :, :, :, None, :] * same[None, :, None, None, :, None]
        return rows.reshape(S5_PAIRS, 2 * p, 2 * t * hc)

    v1 = jnp.concatenate([v_pair(kd) for kd in vkinds], axis=1)
    v = jnp.concatenate([v1, v1], axis=1).astype(BF16)

    def lanes(z):
        return jnp.tile(z.reshape(1, g * p), (1, batch))

    at = apow[..., t]
    a_tab = jnp.concatenate([lanes(jnp.real(at[0])), lanes(jnp.imag(at[0])),
                             lanes(jnp.real(at[1])), lanes(jnp.imag(at[1]))], axis=0)
    dvec = jnp.tile(d_skip.astype(F32).reshape(S5_PAIRS, 1, 2 * hc), (1, t, 1)).reshape(S5_PAIRS, 1, 2 * t * hc)
    return dict(strip=strip, we=we, v=v, a_tab=a_tab, dvec=dvec)


def _s5_pack_kernel(xa_ref, xb_ref, u_ref, *, n_chunks):
    per_half = S5_PAIRS // 2
    for half, x_ref in enumerate((xa_ref, xb_ref)):
        rows = [x_ref[pl.ds(tau, n_chunks, stride=S5_CHUNK), :] for tau in range(S5_CHUNK)]
        for qq in range(per_half):
            pieces = [r[:, qq * 32:(qq + 1) * 32] for r in rows]
            u_ref[half * per_half + qq] = jnp.concatenate(pieces, axis=-1).astype(BF16)


def _s5_unpack_kernel(y_ref, oa_ref, ob_ref, *, n_chunks):
    per_half = S5_PAIRS // 2
    for half, o_ref in enumerate((oa_ref, ob_ref)):
        ys = [y_ref[half * per_half + qq].astype(F32) for qq in range(per_half)]
        for t in range(S5_CHUNK):
            pieces = [y[:, t * 32:(t + 1) * 32] for y in ys]
            o_ref[pl.ds(t, n_chunks, stride=S5_CHUNK), :] = jnp.concatenate(pieces, axis=-1)


def _s5_pack(sa, sb, batch):
    n_chunks = sa.shape[0] // batch // S5_CHUNK
    rows = n_chunks * S5_CHUNK
    cols = 2 * S5_CHUNK * S5_GROUP_CH
    half = pl.BlockSpec((rows, 128), lambda b: (b, 0))
    return pl.pallas_call(
        functools.partial(_s5_pack_kernel, n_chunks=n_chunks),
        grid=(batch,),
        in_specs=[half, half],
        out_specs=pl.BlockSpec((S5_PAIRS, None, n_chunks, cols), lambda b: (0, b, 0, 0)),
        out_shape=jax.ShapeDtypeStruct((S5_PAIRS, batch, n_chunks, cols), BF16),
        compiler_params=_cparams("arbitrary"),
        name="s5_pack",
    )(sa, sb)


def _s5_unpack(y, batch):
    n_chunks = y.shape[2]
    rows = n_chunks * S5_CHUNK
    cols = y.shape[3]
    half = pl.BlockSpec((rows, 128), lambda b: (b, 0))
    return pl.pallas_call(
        functools.partial(_s5_unpack_kernel, n_chunks=n_chunks),
        grid=(batch,),
        in_specs=[pl.BlockSpec((S5_PAIRS, None, n_chunks, cols), lambda b: (0, b, 0, 0))],
        out_specs=[half, half],
        out_shape=[jax.ShapeDtypeStruct((batch * rows, 128), F32)] * 2,
        compiler_params=_cparams("arbitrary"),
        name="s5_unpack",
    )(y)


def _s5_e_kernel(ul_ref, uc_ref, we_ref, ref_, imf_, reb_, imb_):
    u = jnp.concatenate([ul_ref[...], uc_ref[...]], axis=0)
    e = _dot(u, we_ref[...])
    ref_[...] = e[:, 0:128]
    imf_[...] = e[:, 128:256]
    reb_[...] = e[:, 256:384]
    imb_[...] = e[:, 384:512]


def _s5_scan_kernel(a_ref, ref_, imf_, reb_, imb_, prf, pif, prb, pib, *, n_rows, n_ctx):
    afr = a_ref[0:1, :]
    afi = a_ref[1:2, :]
    abr = a_ref[2:3, :]
    abi = a_ref[3:4, :]
    zero = jnp.zeros_like(afr)

    n_lat = n_rows - n_ctx

    def body(s, carry):
        sfr, sfi, sbr, sbi = carry
        nf = jnp.where(s < n_ctx, n_lat + s, s - n_ctx)
        nb = n_rows - 1 - s
        prf[pl.ds(nf, 1), :] = sfr
        pif[pl.ds(nf, 1), :] = sfi
        prb[pl.ds(nb, 1), :] = sbr
        pib[pl.ds(nb, 1), :] = sbi
        efr = ref_[pl.ds(nf, 1), :]
        efi = imf_[pl.ds(nf, 1), :]
        ebr = reb_[pl.ds(nb, 1), :]
        ebi = imb_[pl.ds(nb, 1), :]
        nfr = afr * sfr - afi * sfi + efr
        nfi = afr * sfi + afi * sfr + efi
        nbr = abr * sbr - abi * sbi + ebr
        nbi = abr * sbi + abi * sbr + ebi
        return nfr, nfi, nbr, nbi

    lax.fori_loop(0, n_rows, body, (zero, zero, zero, zero))


def _s5_y_kernel(ul_ref, uc_ref, strip_ref, v_ref, d_ref, prf, pif, prb, pib, yl_ref, yc_ref, m_scr):
    width = 2 * S5_GROUP_CH
    cols = S5_CHUNK * width
    n_lat = yl_ref.shape[0]

    @pl.when(pl.program_id(1) == 0)
    def _():
        strip = strip_ref[...]
        for j in range(S5_CHUNK):
            off = (S5_CHUNK - 1 - j) * width
            win = strip if off == 0 else pltpu.roll(strip, 2 * cols - off, axis=1)
            m_scr[j * width:(j + 1) * width, :] = win[:, :cols].astype(BF16)

    u = jnp.concatenate([ul_ref[...], uc_ref[...]], axis=0)
    y_intra = _dot(u, m_scr[...])
    pcat = jnp.concatenate([prf[...], pif[...], prb[...], pib[...]], axis=-1)
    hi, lo = _split_bf16(pcat)
    y_cross = _dot(jnp.concatenate([hi, lo], axis=-1), v_ref[...])
    y = y_intra + y_cross + d_ref[...] * u.astype(F32)
    yl_ref[...] = y[:n_lat].astype(BF16)
    yc_ref[...] = y[n_lat:].astype(BF16)


def _s5_core(ul, uc, tabs, layer, batch):
    n_lat, n_ctx = ul.shape[2], uc.shape[2]
    n_rows = n_lat + n_ctx
    width = batch * S5_PAIRS * 128
    cols = 2 * S5_CHUNK * S5_GROUP_CH
    ul_spec = pl.BlockSpec((None, None, n_lat, cols), lambda q, b: (q, b, 0, 0))
    uc_spec = pl.BlockSpec((None, None, n_ctx, cols), lambda q, b: (q, b, 0, 0))
    st_spec = pl.BlockSpec((n_rows, 128), lambda q, b: (0, b * S5_PAIRS + q))
    st_shape = jax.ShapeDtypeStruct((n_rows, width), F32)
    e4 = pl.pallas_call(
        _s5_e_kernel,
        grid=(S5_PAIRS, batch),
        in_specs=[ul_spec, uc_spec, pl.BlockSpec((None, None, cols, 512), lambda q, b: (layer, q, 0, 0))],
        out_specs=[st_spec] * 4,
        out_shape=[st_shape] * 4,
        compiler_params=_cparams("arbitrary", "arbitrary"),
        name="s5_chunk_states",
    )(ul, uc, tabs['we'])
    p4 = pl.pallas_call(
        functools.partial(_s5_scan_kernel, n_rows=n_rows, n_ctx=n_ctx),
        out_shape=[st_shape] * 4,
        compiler_params=pltpu.CompilerParams(vmem_limit_bytes=VMEM_LIMIT_BYTES),
        name="s5_state_scan",
    )(tabs['a_tab'][layer], *e4)
    y = pl.pallas_call(
        _s5_y_kernel,
        grid=(S5_PAIRS, batch),
        in_specs=[
            ul_spec, uc_spec,
            pl.BlockSpec((None, None, 2 * S5_GROUP_CH, 2 * cols), lambda q, b: (layer, q, 0, 0)),
            pl.BlockSpec((None, None, cols, cols), lambda q, b: (layer, q, 0, 0)),
            pl.BlockSpec((None, None, 1, cols), lambda q, b: (layer, q, 0, 0)),
            st_spec, st_spec, st_spec, st_spec,
        ],
        out_specs=[ul_spec, uc_spec],
        out_shape=[
            jax.ShapeDtypeStruct((S5_PAIRS, batch, n_lat, cols), BF16),
            jax.ShapeDtypeStruct((S5_PAIRS, batch, n_ctx, cols), BF16),
        ],
        scratch_shapes=[pltpu.VMEM((cols, cols), BF16)],
        compiler_params=_cparams("arbitrary", "arbitrary"),
        name="s5_outputs",
    )(ul, uc, tabs['strip'], tabs['v'], tabs['dvec'], *p4)
    return y


def _s5_mixer(s_lat, s_ctx, tabs, layer, batch):
    ul = _s5_pack(*s_lat, batch)
    uc = _s5_pack(*s_ctx, batch)
    yl, yc = _s5_core(ul, uc, tabs, layer, batch)
    return _s5_unpack(yl, batch), _s5_unpack(yc, batch)


def _ret_tables(ret_decay):
    c = RET_CHUNK
    lg = jax.nn.log_sigmoid(ret_decay.astype(F32))
    lane_h = np.repeat(np.arange(RET_HEADS), RET_DIM)
    lgl = jnp.repeat(lg, RET_DIM, axis=1)
    pos = jnp.arange(c, dtype=F32)[:, None]
    qd = jnp.stack([jnp.exp((pos + 1.0) * lgl[0][None]), jnp.exp((c - pos) * lgl[1][None])])
    kd = jnp.stack([jnp.exp((c - 1.0 - pos) * lgl[0][None]), jnp.exp(pos * lgl[1][None])])
    bmask = jnp.asarray((lane_h[:, None] == lane_h[None, :]).astype(np.float32))
    cd = jnp.exp(c * lgl)[:, :, None] * bmask[None]
    diff = pos - pos.T
    dm = []
    for h in range(RET_HEADS):
        fw = jnp.where(diff >= 0, jnp.exp(jnp.maximum(diff, 0.0) * lg[0, h]), 0.0)
        bw = jnp.where(diff <= 0, jnp.exp(jnp.maximum(-diff, 0.0) * lg[1, h]), 0.0)
        dm.append(fw + bw)
    dm = jnp.concatenate(dm, axis=0)
    return dict(qd=qd, kd=kd, cd=cd, dm=dm)


def _ret_masks():
    lane_h = np.repeat(np.arange(RET_HEADS), RET_DIM)
    bmask = (lane_h[:, None] == lane_h[None, :]).astype(np.float32)
    hmask = (np.arange(RET_HEADS)[:, None] == lane_h[None, :]).astype(np.float32)
    return jnp.asarray(bmask), jnp.asarray(hmask)


def _rope_tables(n_tokens):
    t = np.arange(n_tokens)
    row = (t // GRID_W).astype(np.float64)
    col = (t % GRID_W).astype(np.float64)
    n_freq = RET_DIM // 4
    inv_freq = 1.0 / (ROPE_BASE ** (np.arange(n_freq, dtype=np.float64) / n_freq))
    ang = np.concatenate([row[:, None] * inv_freq, col[:, None] * inv_freq], axis=-1)
    cos = np.cos(ang)
    sin = np.sin(ang)
    cos_t = np.tile(np.concatenate([cos, cos], axis=-1), (1, RET_HEADS))
    sin_t = np.tile(np.concatenate([-sin, sin], axis=-1), (1, RET_HEADS))
    half = RET_DIM // 2
    perm = np.arange(BRANCH_W) ^ half
    swap = np.zeros((BRANCH_W, BRANCH_W), np.float32)
    swap[perm, np.arange(BRANCH_W)] = 1.0
    return jnp.asarray(cos_t, F32), jnp.asarray(sin_t, F32), jnp.asarray(swap, BF16)


def _ret_chunk(q, k, v, s, qd, kd, cd, bmask, dm, hmask, with_intra):
    cross = _dot((q * qd).astype(BF16), s.astype(BF16))
    s_new = cd * s + bmask * _dot_tn((k * kd).astype(BF16), v)
    if not with_intra:
        return cross, s_new
    qb = q.astype(BF16)
    kb = k.astype(BF16)
    qs = jnp.concatenate([qb * hmask[h:h + 1].astype(BF16) for h in range(RET_HEADS)], axis=0)
    scores = _dot_nt(qs, kb) * dm
    ov = _dot(scores.astype(BF16), v)
    c = q.shape[0]
    inner = ov[0:c] * hmask[0:1]
    for h in range(1, RET_HEADS):
        inner = inner + ov[h * c:(h + 1) * c] * hmask[h:h + 1]
    return inner + cross, s_new


def _ret_kernel(qf_ref, kf_ref, vf_ref, qb_ref, kb_ref, vb_ref, qc_ref, kc_ref, vc_ref,
                cosf_ref, sinf_ref, cosb_ref, sinb_ref, swap_ref,
                qd_ref, kd_ref, cd_ref, bm_ref, dm_ref, hm_ref,
                of_ref, ob_ref, ocf_ref, ocb_ref, sf_scr, sb_scr, *, n_chunks, n_ctx_chunks):
    i = pl.program_id(1)
    c = RET_CHUNK
    k_scale = RET_DIM ** -0.5
    bmask = bm_ref[...]
    dm = dm_ref[...]
    hmask = hm_ref[...]
    tabs = [(qd_ref[d], kd_ref[d], cd_ref[d]) for d in range(2)]

    @pl.when(i == 0)
    def _():
        for d, oc_ref, s_scr in ((0, ocf_ref, sf_scr), (1, ocb_ref, sb_scr)):
            qd, kd, cd = tabs[d]
            s = jnp.zeros((BRANCH_W, BRANCH_W), F32)
            order = range(n_ctx_chunks) if d == 0 else range(n_ctx_chunks - 1, -1, -1)
            for cc in order:
                sl = slice(cc * c, (cc + 1) * c)
                o, s = _ret_chunk(qc_ref[sl, :].astype(F32), kc_ref[sl, :].astype(F32) * k_scale, vc_ref[sl, :],
                                  s, qd, kd, cd, bmask, dm, hmask, d == 0)
                oc_ref[sl, :] = o
            s_scr[...] = s

    swap = swap_ref[...]

    def rope(x_ref, cos_ref, sin_ref):
        xb = x_ref[...]
        return xb.astype(F32) * cos_ref[...] + _dot(xb, swap) * sin_ref[...]

    q_f = rope(qf_ref, cosf_ref, sinf_ref)
    k_f = rope(kf_ref, cosf_ref, sinf_ref) * k_scale
    q_b = rope(qb_ref, cosb_ref, sinb_ref)
    k_b = rope(kb_ref, cosb_ref, sinb_ref) * k_scale
    sf = sf_scr[...]
    sb = sb_scr[...]
    for step in range(n_chunks):
        sl = slice(step * c, (step + 1) * c)
        o, sf = _ret_chunk(q_f[sl], k_f[sl], vf_ref[sl, :], sf, *tabs[0], bmask, dm, hmask, True)
        of_ref[sl, :] = o
        cb = n_chunks - 1 - step
        sl = slice(cb * c, (cb + 1) * c)
        o, sb = _ret_chunk(q_b[sl], k_b[sl], vb_ref[sl, :], sb, *tabs[1], bmask, dm, hmask, False)
        ob_ref[sl, :] = o
    sf_scr[...] = sf
    sb_scr[...] = sb


def _retention(proj_l, proj_c, tabs, layer, masks, rope, batch, seq_len, ctx_len):
    n_chunks = 8
    blk = n_chunks * RET_CHUNK
    nblk = seq_len // blk
    cos_t, sin_t, swap = rope

    def lat(col, back):
        if back:
            return pl.BlockSpec((blk, BRANCH_W), lambda b, i: (b * nblk + nblk - 1 - i, col))
        return pl.BlockSpec((blk, BRANCH_W), lambda b, i: (b * nblk + i, col))

    def ctx(col):
        return pl.BlockSpec((ctx_len, BRANCH_W), lambda b, i: (b, col))

    def const(shape):
        return pl.BlockSpec(shape, lambda b, i: (0,) * len(shape))

    def per_layer(shape):
        return pl.BlockSpec((None,) + shape, lambda b, i: (layer,) + (0,) * len(shape))

    tab_f = pl.BlockSpec((blk, BRANCH_W), lambda b, i: (i, 0))
    tab_b = pl.BlockSpec((blk, BRANCH_W), lambda b, i: (nblk - 1 - i, 0))
    kern = functools.partial(_ret_kernel, n_chunks=n_chunks, n_ctx_chunks=ctx_len // RET_CHUNK)
    c = RET_CHUNK
    ctx_out = pl.BlockSpec((ctx_len, BRANCH_W), lambda b, i: (b, 0))
    o_f, o_b, oc_f, oc_b = pl.pallas_call(
        kern,
        grid=(batch, nblk),
        in_specs=[
            lat(COL_RQ, False), lat(COL_RK, False), lat(COL_RV, False),
            lat(COL_RQ, True), lat(COL_RK, True), lat(COL_RV, True),
            ctx(COL_RQ), ctx(COL_RK), ctx(COL_RV),
            tab_f, tab_f, tab_b, tab_b, const((BRANCH_W, BRANCH_W)),
            per_layer((2, c, BRANCH_W)), per_layer((2, c, BRANCH_W)), per_layer((2, BRANCH_W, BRANCH_W)),
            const((BRANCH_W, BRANCH_W)), per_layer((RET_HEADS * c, c)), const((RET_HEADS, BRANCH_W)),
        ],
        out_specs=[lat(0, False), lat(0, True), ctx_out, ctx_out],
        out_shape=[
            jax.ShapeDtypeStruct((batch * seq_len, BRANCH_W), F32),
            jax.ShapeDtypeStruct((batch * seq_len, BRANCH_W), F32),
            jax.ShapeDtypeStruct((batch * ctx_len, BRANCH_W), F32),
            jax.ShapeDtypeStruct((batch * ctx_len, BRANCH_W), F32),
        ],
        scratch_shapes=[pltpu.VMEM((BRANCH_W, BRANCH_W), F32), pltpu.VMEM((BRANCH_W, BRANCH_W), F32)],
        compiler_params=_cparams("arbitrary", "arbitrary"),
        name="retention",
    )(proj_l, proj_l, proj_l, proj_l, proj_l, proj_l, proj_c, proj_c, proj_c,
      cos_t, sin_t, cos_t, sin_t, swap,
      tabs['qd'], tabs['kd'], tabs['cd'], masks[0], tabs['dm'], masks[1])
    return (o_f, o_b), (oc_f, oc_b)


def _na_tables(rpb):
    kr, kw = NA_WIN_ROWS, NA_WIN_COLS
    col = np.arange(GRID_W)
    col_start = np.clip(col - kw // 2, 0, GRID_W - kw)
    in_win = (col[None, :] >= col_start[:, None]) & (col[None, :] < col_start[:, None] + kw)
    dc = np.clip(col[None, :] - col[:, None], -(kw - 1), kw - 1) + (kw - 1)
    pick_c = (dc[:, :, None] == np.arange(2 * kw - 1)[None, None, :]).astype(np.float32)
    by = jnp.einsum('hrc,qkc->hqrk', rpb.astype(F32), jnp.asarray(pick_c), precision=lax.Precision.HIGHEST)
    by = jnp.where(jnp.asarray(in_win)[None, :, None, :], by, NEG_BIG)
    bias = jnp.stack([by[:, :, v:v + kr, :] for v in range(kr)], axis=0)
    return bias.reshape(kr, NA_HEADS * GRID_W, kr * GRID_W)


def _na_head_mask():
    lane_h = np.repeat(np.arange(NA_HEADS), NA_DIM)
    hmask = (np.arange(NA_HEADS)[:, None] == lane_h[None, :]).astype(np.float32)
    return jnp.asarray(hmask, F32)


def _attend(qs, keys, vals, bias, kc, vc):
    s_ctx = _dot_nt(qs, kc)
    m = jnp.max(s_ctx, axis=-1, keepdims=True)
    if keys is not None:
        s_band = _dot_nt(qs, keys) + bias
        m = jnp.maximum(m, jnp.max(s_band, axis=-1, keepdims=True))
        p_band = jnp.exp(s_band - m)
    p_ctx = jnp.exp(s_ctx - m)
    l = jnp.sum(p_ctx, axis=-1, keepdims=True)
    o = _dot(p_ctx.astype(BF16), vc)
    if keys is not None:
        l = l + jnp.sum(p_band, axis=-1, keepdims=True)
        o = o + _dot(p_band.astype(BF16), vals)
    return o / l


def _stack_heads(q, hmask_scaled):
    return jnp.concatenate([q * hmask_scaled[h:h + 1] for h in range(NA_HEADS)], axis=0)


def _unstack_heads(o, hmask, n):
    out = o[0:n] * hmask[0:1]
    for h in range(1, NA_HEADS):
        out = out + o[h * n:(h + 1) * n] * hmask[h:h + 1]
    return out


def _na_kernel(q_ref, k_ref, v_ref, kc_ref, vc_ref, bias_ref, hm_ref, o_ref, *, n_grid_rows):
    i = pl.program_id(1)
    hmask = hm_ref[...]
    hms = (hmask * (NA_DIM ** -0.5)).astype(BF16)
    kc = kc_ref[...]
    vc = vc_ref[...]
    band = NA_WIN_ROWS * GRID_W
    for rr in range(NA_QROWS):
        r = i * NA_QROWS + rr
        rs = jnp.clip(r - NA_WIN_ROWS // 2, 0, n_grid_rows - NA_WIN_ROWS)
        var = rs - r + (NA_WIN_ROWS - 1)
        start = pl.multiple_of(rs * GRID_W, GRID_W)
        keys = k_ref[pl.ds(start, band), :]
        vals = v_ref[pl.ds(start, band), :]
        qs = _stack_heads(q_ref[rr * GRID_W:(rr + 1) * GRID_W, :], hms)
        o = _attend(qs, keys, vals, bias_ref[var], kc, vc)
        o_ref[rr * GRID_W:(rr + 1) * GRID_W, :] = _unstack_heads(o, hmask, GRID_W).astype(BF16)


def _na_ctx_kernel(q_ref, kc_ref, vc_ref, hm_ref, o_ref):
    hmask = hm_ref[...]
    hms = (hmask * (NA_DIM ** -0.5)).astype(BF16)
    n = q_ref.shape[0]
    o = _attend(_stack_heads(q_ref[...], hms), None, None, None, kc_ref[...], vc_ref[...])
    o_ref[...] = _unstack_heads(o, hmask, n).astype(BF16)


def _neighborhood(proj_l, proj_c, bias, layer, hmask, batch, seq_len, ctx_len, need_ctx_out):
    rows = seq_len // GRID_W
    qblk = NA_QROWS * GRID_W
    nq = seq_len // qblk
    out_l = pl.pallas_call(
        functools.partial(_na_kernel, n_grid_rows=rows),
        grid=(batch, nq),
        in_specs=[
            pl.BlockSpec((qblk, BRANCH_W), lambda b, i: (b * nq + i, COL_NQ)),
            pl.BlockSpec((seq_len, BRANCH_W), lambda b, i: (b, COL_NK)),
            pl.BlockSpec((seq_len, BRANCH_W), lambda b, i: (b, COL_NV)),
            pl.BlockSpec((ctx_len, BRANCH_W), lambda b, i: (b, COL_NK)),
            pl.BlockSpec((ctx_len, BRANCH_W), lambda b, i: (b, COL_NV)),
            pl.BlockSpec((None,) + bias.shape[1:], lambda b, i: (layer, 0, 0, 0)),
            pl.BlockSpec(hmask.shape, lambda b, i: (0, 0)),
        ],
        out_specs=pl.BlockSpec((qblk, BRANCH_W), lambda b, i: (b * nq + i, 0)),
        out_shape=jax.ShapeDtypeStruct((batch * seq_len, BRANCH_W), BF16),
        compiler_params=_cparams("arbitrary", "arbitrary"),
        name="neighborhood_attn",
    )(proj_l, proj_l, proj_l, proj_c, proj_c, bias, hmask)
    out_c = None
    if need_ctx_out:
        out_c = pl.pallas_call(
            _na_ctx_kernel,
            grid=(batch,),
            in_specs=[
                pl.BlockSpec((ctx_len, BRANCH_W), lambda b: (b, COL_NQ)),
                pl.BlockSpec((ctx_len, BRANCH_W), lambda b: (b, COL_NK)),
                pl.BlockSpec((ctx_len, BRANCH_W), lambda b: (b, COL_NV)),
                pl.BlockSpec(hmask.shape, lambda b: (0, 0)),
            ],
            out_specs=pl.BlockSpec((ctx_len, BRANCH_W), lambda b: (b, 0)),
            out_shape=jax.ShapeDtypeStruct((batch * ctx_len, BRANCH_W), BF16),
            compiler_params=_cparams("arbitrary"),
            name="context_attn",
        )(proj_c, proj_c, proj_c, hmask)
    return out_l, out_c


def _merge_kernel(x_ref, mod_ref, g_ref, gt0, gt1, gt2, gt3, fa_ref, fb_ref, s5a_ref, s5b_ref, rof_ref, rob_ref, rg_ref, na_ref,
                  wglu_ref, bglu_ref, gn_ref, avg_ref, wb_ref, wo_ref, o_ref, *, tiles_per_mod, mod_base):
    i = pl.program_id(0)
    _, _, gate_a = _mod_rows(mod_ref, i, tiles_per_mod, mod_base, 0)
    z = _gelu_tanh(jnp.concatenate([s5a_ref[...], s5b_ref[...]], axis=-1)).astype(BF16)
    zf = z.astype(F32)
    b_s5 = (zf * _sigmoid(_dot(z, wglu_ref[...]) + bglu_ref[...])).astype(BF16)
    o = rof_ref[...] + rob_ref[...]
    avg = avg_ref[...]
    hi, lo = _split_bf16(o)
    mu = _dot(hi, avg) + _dot(lo, avg)
    dlt = o - mu
    hi, lo = _split_bf16(dlt * dlt)
    var = _dot(hi, avg) + _dot(lo, avg)
    hn = dlt * lax.rsqrt(var + EPS) * gn_ref[...]
    b_ret = (_silu(rg_ref[...].astype(F32)) * hn).astype(BF16)
    b_fnet = jnp.concatenate([fa_ref[...], fb_ref[...]], axis=-1).astype(BF16)
    outs = (b_fnet, b_s5, b_ret, na_ref[...])
    gates = (gt0, gt1, gt2, gt3)
    y = (1.0 + jnp.tanh(gates[0][...].astype(F32))) * _dot(outs[0], wb_ref[0])
    for b in range(1, N_BRANCH):
        y = y + (1.0 + jnp.tanh(gates[b][...].astype(F32))) * _dot(outs[b], wb_ref[b])
    yo = _dot(y.astype(BF16), wo_ref[...])
    o_ref[...] = x_ref[...] + gate_a * _rms(yo, g_ref[...])


def _merge(x, mod, g1, proj, a, s5y, ret_o, na, lw, *, rows_per_mod, mod_base):
    rows, d = x.shape
    tm = min(512, rows)
    nt = rows // tm

    def row(shape, col=0):
        return pl.BlockSpec(shape, lambda i: (i, col))

    def const(arr):
        return pl.BlockSpec(arr.shape, lambda i: (0,) * arr.ndim)

    kern = functools.partial(_merge_kernel, tiles_per_mod=max(rows_per_mod // tm, 1), mod_base=mod_base)
    ins = [x, mod, g1.reshape(1, d), proj, proj, proj, proj, a[0], a[1], s5y[0], s5y[1], ret_o[0], ret_o[1], proj, na,
           lw['w_glu'], lw['b_glu'], lw['ret_gn'], lw['avg'], lw['w_branch'], lw['w_out']]
    specs = [
        row((tm, d)), const(mod), pl.BlockSpec((1, d), lambda i: (0, 0)),
        row((tm, d), 0), row((tm, d), 1), row((tm, d), 2), row((tm, d), 3),
        row((tm, 128)), row((tm, 128)), row((tm, 128)), row((tm, 128)),
        row((tm, BRANCH_W)), row((tm, BRANCH_W)),
        row((tm, BRANCH_W), COL_RG), row((tm, BRANCH_W)),
        const(lw['w_glu']), const(lw['b_glu']), const(lw['ret_gn']), const(lw['avg']),
        const(lw['w_branch']), const(lw['w_out']),
    ]
    return pl.pallas_call(
        kern,
        grid=(nt,),
        in_specs=specs,
        out_specs=row((tm, d)),
        out_shape=jax.ShapeDtypeStruct((rows, d), F32),
        compiler_params=_cparams("arbitrary"),
        name="merge_out",
    )(*ins)


def _ffn_kernel(x_ref, mod_ref, g2_ref, g3_ref, wg_ref, wu_ref, wd_ref, o_ref, *, tiles_per_mod, mod_base):
    i = pl.program_id(0)
    sh, sc, gate_f = _mod_rows(mod_ref, i, tiles_per_mod, mod_base, 3)
    x = x_ref[...]
    h = (_rms(x, g2_ref[...]) * (1.0 + sc) + sh).astype(BF16)
    act = (_silu(_dot(h, wg_ref[...])) * _dot(h, wu_ref[...])).astype(BF16)
    y = _dot(act, wd_ref[...])
    o_ref[...] = x + gate_f * _rms(y, g3_ref[...])


def _ffn_dense(x, mod, g2, g3, wg, wu, wd, *, rows_per_mod, mod_base):
    rows, d = x.shape
    d_ff = wg.shape[1]
    tm = min(512, rows)
    kern = functools.partial(_ffn_kernel, tiles_per_mod=max(rows_per_mod // tm, 1), mod_base=mod_base)

    def resident(shape):
        return pl.BlockSpec(shape, lambda i: (0, 0), pipeline_mode=pl.Buffered(1))

    return pl.pallas_call(
        kern,
        grid=(rows // tm,),
        in_specs=[
            pl.BlockSpec((tm, d), lambda i: (i, 0)),
            pl.BlockSpec(mod.shape, lambda i: (0, 0)),
            pl.BlockSpec((1, d), lambda i: (0, 0)),
            pl.BlockSpec((1, d), lambda i: (0, 0)),
            resident((d, d_ff)), resident((d, d_ff)), resident((d_ff, d)),
        ],
        out_specs=pl.BlockSpec((tm, d), lambda i: (i, 0)),
        out_shape=jax.ShapeDtypeStruct((rows, d), F32),
        compiler_params=_cparams("arbitrary"),
        name="ffn_dense",
    )(x, mod, g2.reshape(1, d), g3.reshape(1, d), wg, wu, wd)


def _router_kernel(x_ref, mod_ref, g2_ref, wr_ref, br_ref, tri_ref, h_ref, comb_ref, plan_ref, cnt_ref, cnt_scr,
                   *, tiles_per_mod, mod_base):
    i = pl.program_id(0)

    @pl.when(i == 0)
    def _():
        cnt_scr[...] = jnp.zeros_like(cnt_scr)

    sh, sc, _ = _mod_rows(mod_ref, i, tiles_per_mod, mod_base, 3)
    h = _rms(x_ref[...], g2_ref[...]) * (1.0 + sc) + sh
    h_ref[...] = _pack_pairs(h)
    h_hi, h_lo = _split_bf16(h)
    w_hi, w_lo = _split_bf16(wr_ref[...])
    logits = _dot(h_hi, w_hi) + _dot(h_lo, w_hi) + _dot(h_hi, w_lo) + br_ref[...]
    lane = lax.broadcasted_iota(jnp.int32, logits.shape, 1)
    v1 = jnp.max(logits, axis=-1, keepdims=True)
    i1 = jnp.min(jnp.where(logits == v1, lane, 128), axis=-1, keepdims=True)
    rest = jnp.where(lane == i1, NEG_BIG, logits)
    v2 = jnp.max(rest, axis=-1, keepdims=True)
    i2 = jnp.min(jnp.where(rest == v2, lane, 128), axis=-1, keepdims=True)
    e = jnp.exp(v2 - v1)
    w1 = 1.0 / (1.0 + e)
    w2 = e / (1.0 + e)
    meta = jnp.where(lane == 0, i1.astype(F32), 0.0) + jnp.where(lane == 1, i2.astype(F32), 0.0)
    meta = meta + jnp.where(lane == 2, w1, 0.0) + jnp.where(lane == 3, w2, 0.0)
    member = jnp.where((lane == i1) | (lane == i2), 1.0, 0.0)
    before = _dot(tri_ref[...], member.astype(BF16)) + cnt_scr[...]
    rank1 = jnp.sum(jnp.where(lane == i1, before, 0.0), axis=-1, keepdims=True)
    rank2 = jnp.sum(jnp.where(lane == i2, before, 0.0), axis=-1, keepdims=True)
    cnt_scr[...] += jnp.sum(member, axis=0, keepdims=True)
    cnt_ref[...] = cnt_scr[...]
    meta = meta + jnp.where(lane == 4, rank1, 0.0) + jnp.where(lane == 5, rank2, 0.0)
    comb_ref[...] = meta[:, :MOE_META_W]
    plan_ref[...] = meta.T[:MOE_META_W]


def _router(x, mod, g2, w_router, b_router, *, rows_per_mod, mod_base):
    rows, d = x.shape
    tm = min(512, rows)
    wr = jnp.zeros((d, 128), F32).at[:, :N_EXPERTS].set(w_router)
    br = jnp.full((1, 128), NEG_BIG, F32).at[0, :N_EXPERTS].set(b_router)
    tri = jnp.asarray(np.tril(np.ones((tm, tm), np.float32), -1), BF16)
    kern = functools.partial(_router_kernel, tiles_per_mod=max(rows_per_mod // tm, 1), mod_base=mod_base)
    return pl.pallas_call(
        kern,
        grid=(rows // tm,),
        in_specs=[
            pl.BlockSpec((tm, d), lambda i: (i, 0)),
            pl.BlockSpec(mod.shape, lambda i: (0, 0)),
            pl.BlockSpec((1, d), lambda i: (0, 0)),
            pl.BlockSpec((d, 128), lambda i: (0, 0)),
            pl.BlockSpec((1, 128), lambda i: (0, 0)),
            pl.BlockSpec((tm, tm), lambda i: (0, 0)),
        ],
        out_specs=[
            pl.BlockSpec((tm, d // 2), lambda i: (i, 0)),
            pl.BlockSpec((tm, MOE_META_W), lambda i: (i, 0)),
            pl.BlockSpec((MOE_META_W, tm), lambda i: (0, i)),
            pl.BlockSpec((1, 128), lambda i: (0, 0)),
        ],
        out_shape=[
            jax.ShapeDtypeStruct((rows, d // 2), jnp.int32),
            jax.ShapeDtypeStruct((rows, MOE_META_W), F32),
            jax.ShapeDtypeStruct((MOE_META_W, rows), F32),
            jax.ShapeDtypeStruct((1, 128), F32),
        ],
        scratch_shapes=[pltpu.VMEM((1, 128), F32)],
        compiler_params=_cparams("arbitrary"),
        name="moe_router",
    )(x, mod, g2.reshape(1, d), wr, br, tri)


def _sc_gather(table, idx):
    n_idx = idx.shape[0]
    width = table.shape[1]
    per_worker = n_idx // SC_WORKERS
    chunk_rows = math.gcd(per_worker, SC_GATHER_ROWS)
    n_chunks = per_worker // chunk_rows
    assert per_worker * SC_WORKERS == n_idx and chunk_rows % 8 == 0
    mesh = plsc.VectorSubcoreMesh(core_axis_name="c", subcore_axis_name="s")

    assert n_chunks % 2 == 0
    buf = [pltpu.VMEM((chunk_rows,), jnp.int32), pltpu.VMEM((chunk_rows, width), table.dtype),
           pltpu.SemaphoreType.DMA, pltpu.SemaphoreType.DMA]

    @functools.partial(
        pl.kernel, mesh=mesh,
        out_type=jax.ShapeDtypeStruct((n_idx, width), table.dtype),
        scratch_types=buf + buf,
        name="sc_row_gather",
    )
    def gather(table_hbm, idx_hbm, out_hbm, idx0, rows0, g0, w0, idx1, rows1, g1, w1):
        wid = lax.axis_index("s") * SC_CORES + lax.axis_index("c")
        base = wid * per_worker
        slots = ((idx0, rows0, g0, w0), (idx1, rows1, g1, w1))

        def fetch(j, slot):
            idx_v, rows_v, g, _ = slots[slot]
            pltpu.sync_copy(idx_hbm.at[pl.ds(base + j * chunk_rows, chunk_rows)], idx_v)
            pltpu.make_async_copy(table_hbm.at[idx_v], rows_v, g).start()

        def store(j, slot):
            idx_v, rows_v, g, w = slots[slot]
            pltpu.make_async_copy(table_hbm.at[idx_v], rows_v, g).wait()
            pltpu.make_async_copy(rows_v, out_hbm.at[pl.ds(base + j * chunk_rows, chunk_rows)], w).start()

        def drain(j, slot):
            _, rows_v, _, w = slots[slot]
            pltpu.make_async_copy(rows_v, out_hbm.at[pl.ds(base + j * chunk_rows, chunk_rows)], w).wait()

        fetch(0, 0)

        @pl.loop(0, n_chunks // 2)
        def _(jj):
            j = 2 * jj

            @pl.when(jj > 0)
            def _():
                drain(j - 1, 1)

            fetch(j + 1, 1)
            store(j, 0)

            @pl.when(j + 2 < n_chunks)
            def _():
                drain(j, 0)
                fetch(j + 2, 0)

            store(j + 1, 1)

        drain(n_chunks - 2, 0)
        drain(n_chunks - 1, 1)

    return gather(table, idx)


def _sc_scatter(table, idx, n_out):
    n_idx = idx.shape[0]
    rows, width = table.shape
    per_worker = n_idx // SC_WORKERS
    chunk_rows = math.gcd(per_worker, SC_GATHER_ROWS)
    n_chunks = per_worker // chunk_rows
    assert per_worker * SC_WORKERS == n_idx and chunk_rows % 8 == 0 and rows % per_worker == 0
    mesh = plsc.VectorSubcoreMesh(core_axis_name="c", subcore_axis_name="s")

    assert n_chunks % 2 == 0
    buf = [pltpu.VMEM((chunk_rows,), jnp.int32), pltpu.VMEM((chunk_rows, width), table.dtype),
           pltpu.SemaphoreType.DMA, pltpu.SemaphoreType.DMA]

    @functools.partial(
        pl.kernel, mesh=mesh,
        out_type=jax.ShapeDtypeStruct((n_out, width), table.dtype),
        scratch_types=buf + buf,
        name="sc_row_scatter",
    )
    def scatter(table_hbm, idx_hbm, out_hbm, idx0, rows0, l0, w0, idx1, rows1, l1, w1):
        wid = lax.axis_index("s") * SC_CORES + lax.axis_index("c")
        base = wid * per_worker
        slots = ((idx0, rows0, l0, w0), (idx1, rows1, l1, w1))

        def src(j):
            return table_hbm.at[pl.ds(lax.rem(base + j * chunk_rows, rows), chunk_rows)]

        def fetch(j, slot):
            idx_v, rows_v, l, _ = slots[slot]
            pltpu.sync_copy(idx_hbm.at[pl.ds(base + j * chunk_rows, chunk_rows)], idx_v)
            pltpu.make_async_copy(src(j), rows_v, l).start()

        def store(j, slot):
            idx_v, rows_v, l, w = slots[slot]
            pltpu.make_async_copy(src(j), rows_v, l).wait()
            pltpu.make_async_copy(rows_v, out_hbm.at[idx_v], w).start()

        def drain(slot):
            idx_v, rows_v, _, w = slots[slot]
            pltpu.make_async_copy(rows_v, out_hbm.at[idx_v], w).wait()

        fetch(0, 0)

        @pl.loop(0, n_chunks // 2)
        def _(jj):
            j = 2 * jj

            @pl.when(jj > 0)
            def _():
                drain(1)

            fetch(j + 1, 1)
            store(j, 0)

            @pl.when(j + 2 < n_chunks)
            def _():
                drain(0)
                fetch(j + 2, 0)

            store(j + 1, 1)

        drain(0)
        drain(1)

    return scatter(table, idx)


def _moe_plan(plan, counts_row, rows):
    tile = MOE_ROW_TILE
    n_tiles = (2 * rows) // tile + N_EXPERTS
    n_slots = n_tiles * tile
    counts = counts_row[0, :N_EXPERTS].astype(jnp.int32)
    padded = ((counts + tile - 1) // tile) * tile
    ends = jnp.cumsum(padded)
    starts = ends - padded
    ids = jnp.arange(N_EXPERTS, dtype=F32)[:, None]
    start_f = starts.astype(F32)[:, None]

    def slot(e_row, r_row):
        return jnp.sum(jnp.where(e_row[None, :] == ids, start_f, 0.0), axis=0) + r_row

    pos = jnp.concatenate([slot(plan[0], plan[4]), slot(plan[1], plan[5])])
    tile_start = jnp.arange(n_tiles, dtype=jnp.int32) * tile
    used = tile_start < ends[-1]
    tile_e = jnp.minimum(jnp.sum((tile_start[:, None] >= ends[None, :]).astype(jnp.int32), axis=1), N_EXPERTS - 1)
    last_e = jnp.max(jnp.where(used, tile_e, 0))
    tile_e = jnp.where(used, tile_e, last_e)
    valid_end = jnp.sum((tile_e[:, None] == jnp.arange(N_EXPERTS)[None, :]) * (starts + counts)[None, :], axis=1)
    n_valid = jnp.where(used, jnp.clip(valid_end - tile_start, 0, tile), 0).astype(jnp.int32)
    return pos.astype(jnp.int32), n_slots, tile_e.astype(jnp.int32), n_valid


def _moe_group_kernel(eid_ref, nval_ref, hs_ref, wg_ref, wu_ref, wd_ref, y_ref, acc_scr, *, n_f):
    w = pl.program_id(0)
    f = pl.program_id(1)
    nv = nval_ref[w]

    def run(n_rows):
        wg = wg_ref[...].astype(BF16)
        wu = wu_ref[...].astype(BF16)
        wd = wd_ref[...].astype(BF16)
        for r0 in range(0, n_rows, MOE_SUB_ROWS):
            rows = slice(r0, r0 + MOE_SUB_ROWS)
            hv = _unpack_pairs(hs_ref[rows, :])
            row = r0 + lax.broadcasted_iota(jnp.int32, hv.shape, 0)
            h = jnp.where(row < nv, hv, 0.0).astype(BF16)
            part = _dot((_silu(_dot(h, wg)) * _dot(h, wu)).astype(BF16), wd)
            acc = jnp.where(f == 0, 0.0, acc_scr[rows, :]) + part
            acc_scr[rows, :] = acc
            y_ref[rows, :] = _pack_pairs(acc)

    for groups in range(1, hs_ref.shape[0] // MOE_SUB_ROWS + 1):
        @pl.when((nv > (groups - 1) * MOE_SUB_ROWS) & (nv <= groups * MOE_SUB_ROWS))
        def _(groups=groups):
            run(groups * MOE_SUB_ROWS)


def _moe_grouped(hs, tile_e, n_valid, wg, wu, wd):
    n_slots = hs.shape[0]
    d = wg.shape[1]
    d_ff = wg.shape[2]
    tile = MOE_ROW_TILE
    tf = MOE_FF_TILE
    n_f = d_ff // tf

    def f_idx(f, nval, w):
        return jnp.where(nval[w] > 0, f, n_f - 1)

    grid_spec = pltpu.PrefetchScalarGridSpec(
        num_scalar_prefetch=2,
        grid=(n_slots // tile, n_f),
        in_specs=[
            pl.BlockSpec((tile, d // 2), lambda w, f, eid, nval: (w, 0)),
            pl.BlockSpec((None, d, tf), lambda w, f, eid, nval: (eid[w], 0, f_idx(f, nval, w))),
            pl.BlockSpec((None, d, tf), lambda w, f, eid, nval: (eid[w], 0, f_idx(f, nval, w))),
            pl.BlockSpec((None, tf, d), lambda w, f, eid, nval: (eid[w], f_idx(f, nval, w), 0)),
        ],
        out_specs=pl.BlockSpec((tile, d // 2), lambda w, f, eid, nval: (w, 0)),
        scratch_shapes=[pltpu.VMEM((tile, d), F32)],
    )
    return pl.pallas_call(
        functools.partial(_moe_group_kernel, n_f=n_f),
        grid_spec=grid_spec,
        out_shape=jax.ShapeDtypeStruct((n_slots, d // 2), jnp.int32),
        compiler_params=_cparams("arbitrary", "arbitrary"),
        name="moe_experts",
    )(tile_e, n_valid, hs, wg, wu, wd)


def _moe_out_kernel(x_ref, y1_ref, y2_ref, meta_ref, mod_ref, g3_ref, o_ref, *, tiles_per_mod, mod_base):
    i = pl.program_id(0)
    _, _, gate_f = _mod_rows(mod_ref, i, tiles_per_mod, mod_base, 3)
    meta = meta_ref[...]
    y = meta[:, 2:3] * _unpack_pairs(y1_ref[...]) + meta[:, 3:4] * _unpack_pairs(y2_ref[...])
    o_ref[...] = x_ref[...] + gate_f * _rms(y, g3_ref[...])


def _moe_combine(x, yg, meta, mod, g3, *, rows_per_mod, mod_base):
    rows, d = x.shape
    tm = math.gcd(1024, rows_per_mod)
    nt = rows // tm
    kern = functools.partial(_moe_out_kernel, tiles_per_mod=max(rows_per_mod // tm, 1), mod_base=mod_base)
    return pl.pallas_call(
        kern,
        grid=(nt,),
        in_specs=[
            pl.BlockSpec((tm, d), lambda i: (i, 0)),
            pl.BlockSpec((tm, d // 2), lambda i: (i, 0)),
            pl.BlockSpec((tm, d // 2), lambda i: (nt + i, 0)),
            pl.BlockSpec((tm, MOE_META_W), lambda i: (i, 0)),
            pl.BlockSpec(mod.shape, lambda i: (0, 0)),
            pl.BlockSpec((1, d), lambda i: (0, 0)),
        ],
        out_specs=pl.BlockSpec((tm, d), lambda i: (i, 0)),
        out_shape=jax.ShapeDtypeStruct((rows, d), F32),
        compiler_params=_cparams("arbitrary"),
        name="moe_combine",
    )(x, yg, yg, meta, mod, g3.reshape(1, d))


def _moe_sparse(x, routed, mod, g3, wg, wu, wd, *, rows_per_mod, mod_base):
    h, meta, plan, counts = routed
    rows = x.shape[0]
    pos, n_slots, tile_e, n_valid = _moe_plan(plan, counts, rows)
    hs = _sc_scatter(h, pos, n_slots)
    ys = _moe_grouped(hs, tile_e, n_valid, wg, wu, wd)
    yg = _sc_gather(ys, pos)
    return _moe_combine(x, yg, meta, mod, g3, rows_per_mod=rows_per_mod, mod_base=mod_base)


def _cast_kernel(w_ref, o_ref, *, scale):
    w = w_ref[...]
    o_ref[...] = (w if scale == 1.0 else w * scale).astype(BF16)


def _cast_bf16(w_stack, layer, scale=1.0):
    squeeze = w_stack.ndim == 3
    w4 = w_stack[:, None] if squeeze else w_stack
    _, n_e, k, n = w4.shape
    bk = min(k, 256)
    out = pl.pallas_call(
        functools.partial(_cast_kernel, scale=scale),
        grid=(n_e, k // bk),
        in_specs=[pl.BlockSpec((None, None, bk, n), lambda e, i: (layer, e, i, 0))],
        out_specs=pl.BlockSpec((None, bk, n), lambda e, i: (e, i, 0)),
        out_shape=jax.ShapeDtypeStruct((n_e, k, n), BF16),
        compiler_params=_cparams("arbitrary", "arbitrary"),
        name="cast_weights",
    )(w4)
    return out[0] if squeeze else out


def _permute_w_in(w_in_stack, layer):
    _, k, n = w_in_stack.shape
    n_blocks = n // BRANCH_W
    shift = 9
    n_gate_blocks = N_BRANCH * D_MODEL // BRANCH_W

    per_step = 5
    assert n_blocks % per_step == 0

    def permute_kernel(*refs):
        o_ref = refs[-1]
        for s, w_ref in enumerate(refs[:-1]):
            scale = jnp.where(pl.program_id(0) * per_step + s < n_gate_blocks, 0.5, 1.0)
            o_ref[:, s * BRANCH_W:(s + 1) * BRANCH_W] = (w_ref[...] * scale).astype(BF16)

    def src(s):
        return pl.BlockSpec((None, k, BRANCH_W), lambda j: (layer, 0, (j * per_step + s + shift) % n_blocks))

    return pl.pallas_call(
        permute_kernel,
        grid=(n_blocks // per_step,),
        in_specs=[src(s) for s in range(per_step)],
        out_specs=pl.BlockSpec((k, per_step * BRANCH_W), lambda j: (0, j)),
        out_shape=jax.ShapeDtypeStruct((k, n), BF16),
        compiler_params=_cparams("arbitrary"),
        name="cast_permute_w_in",
    )(*([w_in_stack] * per_step))


def kernel(x, c, ctx, c_ctx, w_mod, b_mod, norm_g, w_in, s5_a_re, s5_a_im, s5_log_dt, s5_b_re, s5_b_im, s5_c_re, s5_c_im, s5_d, s5_w_glu, s5_b_glu, ret_decay, ret_gn, na_rpb, w_branch, w_out, ffn_w_gate, ffn_w_up, ffn_w_down, moe_w_router, moe_b_router, moe_w_gate, moe_w_up, moe_w_down):
    batch, seq_len, d = x.shape
    ctx_len = ctx.shape[1]
    depth = w_mod.shape[0]
    cond = jnp.concatenate([c, c_ctx[None, :]], axis=0)
    mod_all = _modulation(cond, w_mod, b_mod)
    rope = _rope_tables(seq_len)
    lane_h = np.repeat(np.arange(RET_HEADS), RET_DIM)
    avg = jnp.asarray((lane_h[:, None] == lane_h[None, :]).astype(np.float32) / RET_DIM, BF16)

    xl = x.reshape(batch * seq_len, d)
    xc = ctx.reshape(batch * ctx_len, d)
    lat = dict(rows_per_mod=seq_len, mod_base=0)
    cxt = dict(rows_per_mod=batch * ctx_len, mod_base=batch)

    s5_tabs = jax.vmap(functools.partial(_s5_tables, batch=batch))(
        s5_a_re, s5_a_im, s5_log_dt, s5_b_re, s5_b_im, s5_c_re, s5_c_im, s5_d)
    ret_tabs = jax.vmap(_ret_tables)(ret_decay)
    ret_masks = _ret_masks()
    na_bias = jax.vmap(_na_tables)(na_rpb)
    na_hmask = _na_head_mask()

    for layer in range(depth):
        last = layer == depth - 1
        need_ctx = not last
        mod = mod_all[layer]
        ng = norm_g[layer]
        w_in_bf = _permute_w_in(w_in, layer)
        lw = dict(w_glu=s5_w_glu[layer].astype(BF16), b_glu=s5_b_glu[layer].reshape(1, BRANCH_W).astype(F32),
                  ret_gn=ret_gn[layer].reshape(1, BRANCH_W).astype(F32), avg=avg,
                  w_branch=_cast_bf16(w_branch, layer, 0.5), w_out=_cast_bf16(w_out, layer))

        proj_l, f_l, *s_in_l = _in_proj(xl, mod, ng[0], w_in_bf, **lat)
        proj_c, f_c, *s_in_c = _in_proj(xc, mod, ng[0], w_in_bf, **cxt)

        a_l = _fourier_latent(f_l, batch, seq_len)
        s_l, s_c = _s5_mixer(s_in_l, s_in_c, s5_tabs, layer, batch)
        r_l, r_c = _retention(proj_l, proj_c, ret_tabs, layer, ret_masks, rope, batch, seq_len, ctx_len)
        n_l, n_c = _neighborhood(proj_l, proj_c, na_bias, layer, na_hmask, batch, seq_len, ctx_len, need_ctx)

        xl = _merge(xl, mod, ng[1], proj_l, a_l, s_l, r_l, n_l, lw, **lat)
        if need_ctx:
            a_c = _fourier_ctx(f_c, batch, ctx_len)
            xc = _merge(xc, mod, ng[1], proj_c, a_c, s_c, r_c, n_c, lw, **cxt)

        i = layer // 2
        if layer % 2 == 0:
            wg, wu, wd = _cast_bf16(ffn_w_gate, i), _cast_bf16(ffn_w_up, i), _cast_bf16(ffn_w_down, i)
            xl = _ffn_dense(xl, mod, ng[2], ng[3], wg, wu, wd, **lat)
            if need_ctx:
                xc = _ffn_dense(xc, mod, ng[2], ng[3], wg, wu, wd, **cxt)
        else:
            wg, wu, wd = moe_w_gate[i], moe_w_up[i], moe_w_down[i]
            routed = _router(xl, mod, ng[2], moe_w_router[i], moe_b_router[i], **lat)
            xl = _moe_sparse(xl, routed, mod, ng[3], wg, wu, wd, **lat)
            if need_ctx:
                routed_c = _router(xc, mod, ng[2], moe_w_router[i], moe_b_router[i], **cxt)
                xc = _moe_sparse(xc, routed_c, mod, ng[3], wg, wu, wd, **cxt)
    return xl.reshape(batch, seq_len, d)
```
